```python
import math
import jax, jax.numpy as jnp
from jax import lax
import numpy as np

D_MODEL = 1024
BATCH = 16
SEQ = 4096
DEPTH = 2

N_MIXERS = 2
N_SUB = 3
D_FF = 2816
MLSTM_HEADS = 8
MLSTM_DQK = D_MODEL // (2 * MLSTM_HEADS)
MLSTM_DV = D_MODEL // MLSTM_HEADS
MLSTM_CHUNK = 64
CONV_WIDTH = 4
MLSTM_QK_COLS = 2 * MLSTM_HEADS * MLSTM_DQK
MLSTM_V_COLS = MLSTM_HEADS * MLSTM_DV
MLSTM_PROJ = MLSTM_QK_COLS + 2 * MLSTM_V_COLS + 2 * MLSTM_HEADS
DIL_GROUPS = ((128, 1), (512, 4), (2048, 16))
N_GROUPS = len(DIL_GROUPS)
ATTN_HEADS = 8
ATTN_HEAD_DIM = D_MODEL // ATTN_HEADS
ATTN_PROJ = N_GROUPS * 3 * ATTN_HEADS * ATTN_HEAD_DIM
ALPHA = (2 * DEPTH) ** 0.25
BETA = (8 * DEPTH) ** -0.25
LN_EPS = 1e-5
RMS_EPS = 1e-6
N_MLSTM_LAYERS = (DEPTH + 1) // 2
N_ATTN_LAYERS = DEPTH // 2

kernel_name = 'hybrid_mlstm_dilated_attn_macaron_deepnorm_adaln'


def _layer_norm(x, g, b):
    xf = x.astype(jnp.float32)
    mu = xf.mean(-1, keepdims=True)
    var = jnp.square(xf - mu).mean(-1, keepdims=True)
    return ((xf - mu) * lax.rsqrt(var + LN_EPS) * g.astype(jnp.float32) + b.astype(jnp.float32)).astype(x.dtype)


def _swiglu(h, w_in, w_out):
    g, u = jnp.split(h @ w_in, 2, axis=-1)
    return (jax.nn.silu(g) * u) @ w_out


def _causal_dwconv(x, w):
    k_w, ch = w.shape
    xp = jnp.pad(x, ((0, 0), (k_w - 1, 0), (0, 0)))
    return lax.conv_general_dilated(xp, w[:, None, :].astype(x.dtype), window_strides=(1,), padding='VALID',
                                    dimension_numbers=('NWC', 'WIO', 'NWC'), feature_group_count=ch)


def _mlstm_cell_chunkwise(q, k, v, i_pre, log_f):
    B, S, H, DK = q.shape
    DV = v.shape[-1]
    L = MLSTM_CHUNK
    nc = S // L

    def chunks(t):
        return jnp.moveaxis(t.reshape((B, nc, L, H) + t.shape[3:]), 3, 1)

    qc = chunks(q.astype(jnp.float32)) * (DK ** -0.5)
    kc = chunks(k.astype(jnp.float32))
    vc = chunks(v.astype(jnp.float32))
    ig = chunks(i_pre)
    bcum = jnp.cumsum(chunks(log_f), axis=-1)
    b_last = bcum[..., -1]

    a = b_last[..., None] - bcum + ig
    m_loc = a.max(-1)
    wa = jnp.exp(a - m_loc[..., None])
    c_loc = jnp.einsum('bhcl,bhclv,bhclk->bhcvk', wa, vc, kc)
    n_loc = jnp.einsum('bhcl,bhclk->bhck', wa, kc)

    def step(carry, inp):
        c_st, n_st, m_st = carry
        cl, nl, ml, bl = inp
        m_new = jnp.maximum(bl + m_st, ml)
        sp = jnp.exp(bl + m_st - m_new)
        sl = jnp.exp(ml - m_new)
        c_new = sp[..., None, None] * c_st + sl[..., None, None] * cl
        n_new = sp[..., None] * n_st + sl[..., None] * nl
        return (c_new, n_new, m_new), (c_st, n_st, m_st)

    init = (jnp.zeros((B, H, DV, DK), jnp.float32), jnp.zeros((B, H, DK), jnp.float32),
            jnp.zeros((B, H), jnp.float32))
    xs = (jnp.moveaxis(c_loc, 2, 0), jnp.moveaxis(n_loc, 2, 0), jnp.moveaxis(m_loc, 2, 0), jnp.moveaxis(b_last, 2, 0))
    _, (c_prev, n_prev, m_prev) = lax.scan(step, init, xs)
    c_prev = jnp.moveaxis(c_prev, 0, 2)
    n_prev = jnp.moveaxis(n_prev, 0, 2)
    m_prev = jnp.moveaxis(m_prev, 0, 2)

    causal = jnp.tril(jnp.ones((L, L), dtype=bool))
    dlog = jnp.where(causal, bcum[..., :, None] - bcum[..., None, :] + ig[..., None, :], -jnp.inf)
    inter_log = bcum + m_prev[..., None]
    m = jnp.maximum(inter_log, dlog.max(-1))
    sc = jnp.einsum('bhcik,bhcjk->bhcij', qc, kc) * jnp.exp(dlog - m[..., None])
    inter_w = jnp.exp(inter_log - m)
    num = jnp.einsum('bhcij,bhcjv->bhciv', sc, vc) + inter_w[..., None] * jnp.einsum('bhcik,bhcvk->bhciv', qc, c_prev)
    den = sc.sum(-1) + inter_w * jnp.einsum('bhcik,bhck->bhci', qc, n_prev)
    h = num / jnp.maximum(jnp.abs(den), jnp.exp(-m))[..., None]
    return jnp.moveaxis(h, 1, 3).reshape(B, S, H, DV)


def _mlstm_mixer(h, w_in, gate_bias, conv_w, head_gain, w_out):
    B, S, _ = h.shape
    proj = h @ w_in
    qk = proj[..., :MLSTM_QK_COLS]
    v = proj[..., MLSTM_QK_COLS:MLSTM_QK_COLS + MLSTM_V_COLS]
    o = proj[..., MLSTM_QK_COLS + MLSTM_V_COLS:MLSTM_QK_COLS + 2 * MLSTM_V_COLS]
    gates = proj[..., MLSTM_QK_COLS + 2 * MLSTM_V_COLS:].astype(jnp.float32) + gate_bias.astype(jnp.float32)
    qk = jax.nn.silu(_causal_dwconv(qk, conv_w))
    q = qk[..., :MLSTM_QK_COLS // 2].reshape(B, S, MLSTM_HEADS, MLSTM_DQK)
    k = qk[..., MLSTM_QK_COLS // 2:].reshape(B, S, MLSTM_HEADS, MLSTM_DQK)
    v = v.reshape(B, S, MLSTM_HEADS, MLSTM_DV)
    i_pre = gates[..., :MLSTM_HEADS]
    log_f = jax.nn.log_sigmoid(gates[..., MLSTM_HEADS:])
    ht = _mlstm_cell_chunkwise(q, k, v, i_pre, log_f)
    ht = ht * lax.rsqrt(jnp.mean(jnp.square(ht), axis=-1, keepdims=True) + RMS_EPS)
    ht = ht.reshape(B, S, MLSTM_V_COLS) * head_gain.astype(jnp.float32)
    y = jax.nn.sigmoid(o.astype(jnp.float32)) * ht
    return y.astype(h.dtype) @ w_out


def _dilated_window_attention(q, k, v, window, dilation):
    B, S, H, Dh = q.shape
    blk = window // dilation
    unit = blk * dilation
    s_pad = -(-S // unit) * unit
    nb = s_pad // unit

    def to_blocks(t):
        t = jnp.pad(t.astype(jnp.float32), ((0, 0), (0, s_pad - S), (0, 0), (0, 0)))
        t = jnp.swapaxes(t.reshape(B, s_pad // dilation, dilation, H, Dh), 1, 2)
        return t.reshape(B, dilation, nb, blk, H, Dh)

    def with_prev(t):
        prev = jnp.pad(t, ((0, 0), (0, 0), (1, 0), (0, 0), (0, 0), (0, 0)))[:, :, :-1]
        return jnp.concatenate([prev, t], axis=3)

    qb = to_blocks(q)
    kc = with_prev(to_blocks(k))
    vc = with_prev(to_blocks(v))
    s = jnp.einsum('brnqhd,brnkhd->brnhqk', qb, kc) * (Dh ** -0.5)
    qi = jnp.arange(blk)[:, None]
    ki = jnp.arange(2 * blk)[None, :]
    band = (ki >= qi) & (ki <= qi + blk)
    has_prev = (jnp.arange(nb) > 0)[:, None, None] | (ki >= blk)[None]
    mask = band[None] & has_prev
    s = jnp.where(mask[:, None], s, -jnp.inf)
    mx = s.max(-1, keepdims=True)
    e = jnp.exp(s - mx)
    den = e.sum(-1)
    o = jnp.einsum('brnhqk,brnkhd->brnqhd', e, vc) / jnp.swapaxes(den, 3, 4)[..., None]
    lse = jnp.swapaxes(mx[..., 0] + jnp.log(den), 3, 4)

    def from_blocks(t):
        t = t.reshape((B, dilation, s_pad // dilation) + t.shape[4:])
        t = jnp.swapaxes(t, 1, 2).reshape((B, s_pad) + t.shape[3:])
        return t[:, :S]

    return from_blocks(o), from_blocks(lse)


def _dilated_mixer(h, w_in, w_out):
    B, S, _ = h.shape
    proj = (h @ w_in).reshape(B, S, N_GROUPS, 3, ATTN_HEADS, ATTN_HEAD_DIM)
    outs, lses = [], []
    for g, (window, dilation) in enumerate(DIL_GROUPS):
        o_g, lse_g = _dilated_window_attention(proj[:, :, g, 0], proj[:, :, g, 1], proj[:, :, g, 2], window, dilation)
        outs.append(o_g)
        lses.append(lse_g)
    wts = jax.nn.softmax(jnp.stack(lses, 0), axis=0)
    o = jnp.sum(wts[..., None] * jnp.stack(outs, 0), axis=0)
    return o.reshape(B, S, ATTN_HEADS * ATTN_HEAD_DIM).astype(h.dtype) @ w_out


def _dense(key, fan_in, shape, scale=1.0):
    return jax.random.normal(key, shape, jnp.float32) * (scale * fan_in ** -0.5)


def _fwd_setup_inputs(seed: int = 0) -> dict:
    key = jax.random.key(seed)
    ks = jax.random.split(key, 20)
    na, nb_ = N_MLSTM_LAYERS, N_ATTN_LAYERS
    x = jax.random.normal(ks[0], (BATCH, SEQ, D_MODEL), jnp.float32)
    c = jax.random.normal(ks[1], (BATCH, D_MODEL), jnp.float32)
    ada_w = _dense(ks[2], D_MODEL, (DEPTH, D_MODEL, N_SUB * 3 * D_MODEL), 0.1)
    ada_b = 0.01 * jax.random.normal(ks[3], (DEPTH, N_SUB * 3 * D_MODEL), jnp.float32)
    ln_g = 1.0 + 0.02 * jax.random.normal(ks[4], (DEPTH, N_SUB, D_MODEL), jnp.float32)
    ln_b = 0.02 * jax.random.normal(ks[5], (DEPTH, N_SUB, D_MODEL), jnp.float32)
    ffn_w_in = _dense(ks[6], D_MODEL, (DEPTH, 2, D_MODEL, 2 * D_FF))
    ffn_w_out = _dense(ks[7], D_FF, (DEPTH, 2, D_FF, D_MODEL), BETA)
    mlstm_w_in = jnp.concatenate([
        _dense(ks[8], D_MODEL, (na, D_MODEL, MLSTM_QK_COLS)),
        _dense(ks[9], D_MODEL, (na, D_MODEL, MLSTM_V_COLS), BETA),
        _dense(ks[10], D_MODEL, (na, D_MODEL, MLSTM_V_COLS + 2 * MLSTM_HEADS)),
    ], axis=-1)
    mlstm_gate_bias = jnp.concatenate([
        0.1 * jax.random.normal(ks[11], (na, MLSTM_HEADS), jnp.float32),
        3.0 + 3.0 * jax.random.uniform(ks[12], (na, MLSTM_HEADS), jnp.float32),
    ], axis=-1)
    mlstm_conv_w = _dense(ks[13], CONV_WIDTH, (na, CONV_WIDTH, MLSTM_QK_COLS))
    mlstm_head_gain = 1.0 + 0.02 * jax.random.normal(ks[14], (na, MLSTM_V_COLS), jnp.float32)
    mlstm_w_out = _dense(ks[15], MLSTM_V_COLS, (na, MLSTM_V_COLS, D_MODEL), BETA)
    qkv_scale = jnp.array([1.0, 1.0, BETA], jnp.float32)[:, None]
    attn_w_in = (_dense(ks[16], D_MODEL, (nb_, D_MODEL, N_GROUPS, 3, ATTN_HEADS * ATTN_HEAD_DIM)) * qkv_scale
                 ).reshape(nb_, D_MODEL, ATTN_PROJ)
    attn_w_out = _dense(ks[17], ATTN_HEADS * ATTN_HEAD_DIM, (nb_, ATTN_HEADS * ATTN_HEAD_DIM, D_MODEL), BETA)
    return {'x': x, 'c': c, 'ada_w': ada_w, 'ada_b': ada_b, 'ln_g': ln_g, 'ln_b': ln_b,
            'ffn_w_in': ffn_w_in, 'ffn_w_out': ffn_w_out, 'mlstm_w_in': mlstm_w_in,
            'mlstm_gate_bias': mlstm_gate_bias, 'mlstm_conv_w': mlstm_conv_w,
            'mlstm_head_gain': mlstm_head_gain, 'mlstm_w_out': mlstm_w_out,
            'attn_w_in': attn_w_in, 'attn_w_out': attn_w_out}


def _fwd_reference(x, c, ada_w, ada_b, ln_g, ln_b, ffn_w_in, ffn_w_out, mlstm_w_in, mlstm_gate_bias,
              mlstm_conv_w, mlstm_head_gain, mlstm_w_out, attn_w_in, attn_w_out):
    B = x.shape[0]
    cond = jax.nn.silu(c)
    for layer in range(DEPTH):
        mod = (cond @ ada_w[layer] + ada_b[layer]).reshape(B, N_SUB, 3, 1, D_MODEL)

        def modulate(h, s):
            return h * (1.0 + mod[:, s, 1]) + mod[:, s, 0]

        def post(h, out, s, weight):
            return _layer_norm(ALPHA * h + weight * (1.0 + mod[:, s, 2]) * out, ln_g[layer, s], ln_b[layer, s])

        x = post(x, _swiglu(modulate(x, 0), ffn_w_in[layer, 0], ffn_w_out[layer, 0]), 0, 0.5)
        j = layer // N_MIXERS
        if layer % N_MIXERS == 0:
            y = _mlstm_mixer(modulate(x, 1), mlstm_w_in[j], mlstm_gate_bias[j], mlstm_conv_w[j],
                             mlstm_head_gain[j], mlstm_w_out[j])
        else:
            y = _dilated_mixer(modulate(x, 1), attn_w_in[j], attn_w_out[j])
        x = post(x, y, 1, 1.0)
        x = post(x, _swiglu(modulate(x, 2), ffn_w_in[layer, 1], ffn_w_out[layer, 1]), 2, 0.5)
    return x


import jax as _jax
import jax.numpy as _jnp

TWIN_FORMAT = 'train_step'
FWD_PARAMS = ['x', 'c', 'ada_w', 'ada_b', 'ln_g', 'ln_b', 'ffn_w_in', 'ffn_w_out', 'mlstm_w_in', 'mlstm_gate_bias', 'mlstm_conv_w', 'mlstm_head_gain', 'mlstm_w_out', 'attn_w_in', 'attn_w_out']
TWIN_WEIGHTS = ['ada_w', 'ada_b', 'ln_g', 'ln_b', 'ffn_w_in', 'ffn_w_out', 'mlstm_w_in', 'mlstm_gate_bias', 'mlstm_conv_w', 'mlstm_head_gain', 'mlstm_w_out', 'attn_w_in', 'attn_w_out']
TWIN_DIFF_INPUT = 'x'
TWIN_INPUTS = ['x', 'c', 'ada_w', 'ada_b', 'ln_g', 'ln_b', 'ffn_w_in', 'ffn_w_out', 'mlstm_w_in', 'mlstm_gate_bias', 'mlstm_conv_w', 'mlstm_head_gain', 'mlstm_w_out', 'attn_w_in', 'attn_w_out', 'loss_target', 'm_ada_w', 'm_ada_b', 'm_ln_g', 'm_ln_b', 'm_ffn_w_in', 'm_ffn_w_out', 'm_mlstm_w_in', 'm_mlstm_gate_bias', 'm_mlstm_conv_w', 'm_mlstm_head_gain', 'm_mlstm_w_out', 'm_attn_w_in', 'm_attn_w_out', 'v_ada_w', 'v_ada_b', 'v_ln_g', 'v_ln_b', 'v_ffn_w_in', 'v_ffn_w_out', 'v_mlstm_w_in', 'v_mlstm_gate_bias', 'v_mlstm_conv_w', 'v_mlstm_head_gain', 'v_mlstm_w_out', 'v_attn_w_in', 'v_attn_w_out']
TWIN_OUTPUTS = ['loss', 'grad_x', 'grad_ada_w', 'grad_ada_b', 'grad_ln_g', 'grad_ln_b', 'grad_ffn_w_in', 'grad_ffn_w_out', 'grad_mlstm_w_in', 'grad_mlstm_gate_bias', 'grad_mlstm_conv_w', 'grad_mlstm_head_gain', 'grad_mlstm_w_out', 'grad_attn_w_in', 'grad_attn_w_out', 'delta_ada_w', 'delta_ada_b', 'delta_ln_g', 'delta_ln_b', 'delta_ffn_w_in', 'delta_ffn_w_out', 'delta_mlstm_w_in', 'delta_mlstm_gate_bias', 'delta_mlstm_conv_w', 'delta_mlstm_head_gain', 'delta_mlstm_w_out', 'delta_attn_w_in', 'delta_attn_w_out', 'new_m_ada_w', 'new_m_ada_b', 'new_m_ln_g', 'new_m_ln_b', 'new_m_ffn_w_in', 'new_m_ffn_w_out', 'new_m_mlstm_w_in', 'new_m_mlstm_gate_bias', 'new_m_mlstm_conv_w', 'new_m_mlstm_head_gain', 'new_m_mlstm_w_out', 'new_m_attn_w_in', 'new_m_attn_w_out', 'new_v_ada_w', 'new_v_ada_b', 'new_v_ln_g', 'new_v_ln_b', 'new_v_ffn_w_in', 'new_v_ffn_w_out', 'new_v_mlstm_w_in', 'new_v_mlstm_gate_bias', 'new_v_mlstm_conv_w', 'new_v_mlstm_head_gain', 'new_v_mlstm_w_out', 'new_v_attn_w_in', 'new_v_attn_w_out']
TWIN_LEAF_KINDS = {'loss': 'loss', 'grad_x': 'grad_x', 'grad_ada_w': 'grad_w', 'grad_ada_b': 'grad_w', 'grad_ln_g': 'grad_w', 'grad_ln_b': 'grad_w', 'grad_ffn_w_in': 'grad_w', 'grad_ffn_w_out': 'grad_w', 'grad_mlstm_w_in': 'grad_w', 'grad_mlstm_gate_bias': 'grad_w', 'grad_mlstm_conv_w': 'grad_w', 'grad_mlstm_head_gain': 'grad_w', 'grad_mlstm_w_out': 'grad_w', 'grad_attn_w_in': 'grad_w', 'grad_attn_w_out': 'grad_w', 'delta_ada_w': 'delta_w', 'delta_ada_b': 'delta_w', 'delta_ln_g': 'delta_w', 'delta_ln_b': 'delta_w', 'delta_ffn_w_in': 'delta_w', 'delta_ffn_w_out': 'delta_w', 'delta_mlstm_w_in': 'delta_w', 'delta_mlstm_gate_bias': 'delta_w', 'delta_mlstm_conv_w': 'delta_w', 'delta_mlstm_head_gain': 'delta_w', 'delta_mlstm_w_out': 'delta_w', 'delta_attn_w_in': 'delta_w', 'delta_attn_w_out': 'delta_w', 'new_m_ada_w': 'new_m', 'new_m_ada_b': 'new_m', 'new_m_ln_g': 'new_m', 'new_m_ln_b': 'new_m', 'new_m_ffn_w_in': 'new_m', 'new_m_ffn_w_out': 'new_m', 'new_m_mlstm_w_in': 'new_m', 'new_m_mlstm_gate_bias': 'new_m', 'new_m_mlstm_conv_w': 'new_m', 'new_m_mlstm_head_gain': 'new_m', 'new_m_mlstm_w_out': 'new_m', 'new_m_attn_w_in': 'new_m', 'new_m_attn_w_out': 'new_m', 'new_v_ada_w': 'new_v', 'new_v_ada_b': 'new_v', 'new_v_ln_g': 'new_v', 'new_v_ln_b': 'new_v', 'new_v_ffn_w_in': 'new_v', 'new_v_ffn_w_out': 'new_v', 'new_v_mlstm_w_in': 'new_v', 'new_v_mlstm_gate_bias': 'new_v', 'new_v_mlstm_conv_w': 'new_v', 'new_v_mlstm_head_gain': 'new_v', 'new_v_mlstm_w_out': 'new_v', 'new_v_attn_w_in': 'new_v', 'new_v_attn_w_out': 'new_v'}


def _forward(args):
    return _fwd_reference(*[args[k] for k in FWD_PARAMS])


def _output_shape():
    out = _jax.eval_shape(lambda: _forward(_fwd_setup_inputs(0)))
    return out.shape, out.dtype

N_MICROBATCH = 1
ADAM_LR = 0.001
ADAM_B1 = 0.9
ADAM_B2 = 0.999
ADAM_EPS = 1e-08
ADAM_WD = 0.01
ADAM_STEP = 10
PER_EXAMPLE_BATCH_AXIS = {'x': 0, 'c': 0, 'loss_target': 0}
SHARED_INPUTS = []
_WEIGHT_DTYPES = {'ada_w': _jnp.float32, 'ada_b': _jnp.float32, 'ln_g': _jnp.float32, 'ln_b': _jnp.float32, 'ffn_w_in': _jnp.float32, 'ffn_w_out': _jnp.float32, 'mlstm_w_in': _jnp.float32, 'mlstm_gate_bias': _jnp.float32, 'mlstm_conv_w': _jnp.float32, 'mlstm_head_gain': _jnp.float32, 'mlstm_w_out': _jnp.float32, 'attn_w_in': _jnp.float32, 'attn_w_out': _jnp.float32}
MOMENT_SCALE = {'ada_w': 7.508172e-02, 'ada_b': 1.843535e-01, 'ln_g': 2.621142e+01, 'ln_b': 1.494653e+00, 'ffn_w_in': 1.655933e-02, 'ffn_w_out': 5.404293e-02, 'mlstm_w_in': 7.599670e-02, 'mlstm_gate_bias': 5.151280e-01, 'mlstm_conv_w': 6.111064e-02, 'mlstm_head_gain': 4.892570e-02, 'mlstm_w_out': 1.051281e-01, 'attn_w_in': 7.157559e-03, 'attn_w_out': 1.866415e-02}


def _to_microbatches(a, axis):
    t = _jnp.moveaxis(a, axis, 0)
    t = t.reshape((N_MICROBATCH, t.shape[0] // N_MICROBATCH) + t.shape[1:])
    return _jnp.moveaxis(t, 1, axis + 1)


def setup_inputs(seed: int = 0) -> dict:
    inp = _fwd_setup_inputs(seed)
    key = _jax.random.fold_in(_jax.random.key(seed), 7919)
    shape, _ = _output_shape()
    out = dict(inp)
    out["loss_target"] = _jax.random.normal(_jax.random.fold_in(key, 0), shape, _jnp.float32)
    for i, name in enumerate(TWIN_WEIGHTS):
        w = inp[name].astype(_jnp.float32)
        if MOMENT_SCALE is None:
            s = _jnp.sqrt(_jnp.mean(_jnp.square(w)) + 1e-30)
        else:
            s = MOMENT_SCALE[name]
        km, kv = _jax.random.split(_jax.random.fold_in(key, i + 1))
        out[name] = w
        out["m_" + name] = s * _jax.random.normal(km, w.shape, _jnp.float32)
        out["v_" + name] = (s * s) * _jax.random.uniform(kv, w.shape, _jnp.float32, 0.5, 1.5)
    if N_MICROBATCH > 1:
        for name, axis in PER_EXAMPLE_BATCH_AXIS.items():
            out[name] = _to_microbatches(out[name], axis)
    return {'x': out['x'], 'c': out['c'], 'ada_w': out['ada_w'], 'ada_b': out['ada_b'], 'ln_g': out['ln_g'], 'ln_b': out['ln_b'], 'ffn_w_in': out['ffn_w_in'], 'ffn_w_out': out['ffn_w_out'], 'mlstm_w_in': out['mlstm_w_in'], 'mlstm_gate_bias': out['mlstm_gate_bias'], 'mlstm_conv_w': out['mlstm_conv_w'], 'mlstm_head_gain': out['mlstm_head_gain'], 'mlstm_w_out': out['mlstm_w_out'], 'attn_w_in': out['attn_w_in'], 'attn_w_out': out['attn_w_out'], 'loss_target': out['loss_target'], 'm_ada_w': out['m_ada_w'], 'm_ada_b': out['m_ada_b'], 'm_ln_g': out['m_ln_g'], 'm_ln_b': out['m_ln_b'], 'm_ffn_w_in': out['m_ffn_w_in'], 'm_ffn_w_out': out['m_ffn_w_out'], 'm_mlstm_w_in': out['m_mlstm_w_in'], 'm_mlstm_gate_bias': out['m_mlstm_gate_bias'], 'm_mlstm_conv_w': out['m_mlstm_conv_w'], 'm_mlstm_head_gain': out['m_mlstm_head_gain'], 'm_mlstm_w_out': out['m_mlstm_w_out'], 'm_attn_w_in': out['m_attn_w_in'], 'm_attn_w_out': out['m_attn_w_out'], 'v_ada_w': out['v_ada_w'], 'v_ada_b': out['v_ada_b'], 'v_ln_g': out['v_ln_g'], 'v_ln_b': out['v_ln_b'], 'v_ffn_w_in': out['v_ffn_w_in'], 'v_ffn_w_out': out['v_ffn_w_out'], 'v_mlstm_w_in': out['v_mlstm_w_in'], 'v_mlstm_gate_bias': out['v_mlstm_gate_bias'], 'v_mlstm_conv_w': out['v_mlstm_conv_w'], 'v_mlstm_head_gain': out['v_mlstm_head_gain'], 'v_mlstm_w_out': out['v_mlstm_w_out'], 'v_attn_w_in': out['v_attn_w_in'], 'v_attn_w_out': out['v_attn_w_out']}


def _loss(weights, diff, rest, loss_target):
    with _jax.named_scope("forward"):
        args = {**rest, TWIN_DIFF_INPUT: diff, **{k: w.astype(_WEIGHT_DTYPES[k]) for k, w in weights.items()}}
        y = _forward(args)
    with _jax.named_scope("loss_head"):
        err = _jnp.square(y.astype(_jnp.float32) - loss_target)
        return 0.5 * _jnp.sum(_jnp.mean(err, axis=-1)) if err.ndim else 0.5 * err


def _adamw(w, g, m, v):
    m = ADAM_B1 * m + (1.0 - ADAM_B1) * g
    v = ADAM_B2 * v + (1.0 - ADAM_B2) * _jnp.square(g)
    m_hat = m / (1.0 - ADAM_B1 ** ADAM_STEP)
    v_hat = v / (1.0 - ADAM_B2 ** ADAM_STEP)
    delta = -ADAM_LR * (m_hat / (_jnp.sqrt(v_hat) + ADAM_EPS) + ADAM_WD * w)
    return delta, m, v


def reference(x, c, ada_w, ada_b, ln_g, ln_b, ffn_w_in, ffn_w_out, mlstm_w_in, mlstm_gate_bias, mlstm_conv_w, mlstm_head_gain, mlstm_w_out, attn_w_in, attn_w_out, loss_target, m_ada_w, m_ada_b, m_ln_g, m_ln_b, m_ffn_w_in, m_ffn_w_out, m_mlstm_w_in, m_mlstm_gate_bias, m_mlstm_conv_w, m_mlstm_head_gain, m_mlstm_w_out, m_attn_w_in, m_attn_w_out, v_ada_w, v_ada_b, v_ln_g, v_ln_b, v_ffn_w_in, v_ffn_w_out, v_mlstm_w_in, v_mlstm_gate_bias, v_mlstm_conv_w, v_mlstm_head_gain, v_mlstm_w_out, v_attn_w_in, v_attn_w_out):
    given = dict(x=x, c=c, ada_w=ada_w, ada_b=ada_b, ln_g=ln_g, ln_b=ln_b, ffn_w_in=ffn_w_in, ffn_w_out=ffn_w_out, mlstm_w_in=mlstm_w_in, mlstm_gate_bias=mlstm_gate_bias, mlstm_conv_w=mlstm_conv_w, mlstm_head_gain=mlstm_head_gain, mlstm_w_out=mlstm_w_out, attn_w_in=attn_w_in, attn_w_out=attn_w_out, loss_target=loss_target, m_ada_w=m_ada_w, m_ada_b=m_ada_b, m_ln_g=m_ln_g, m_ln_b=m_ln_b, m_ffn_w_in=m_ffn_w_in, m_ffn_w_out=m_ffn_w_out, m_mlstm_w_in=m_mlstm_w_in, m_mlstm_gate_bias=m_mlstm_gate_bias, m_mlstm_conv_w=m_mlstm_conv_w, m_mlstm_head_gain=m_mlstm_head_gain, m_mlstm_w_out=m_mlstm_w_out, m_attn_w_in=m_attn_w_in, m_attn_w_out=m_attn_w_out, v_ada_w=v_ada_w, v_ada_b=v_ada_b, v_ln_g=v_ln_g, v_ln_b=v_ln_b, v_ffn_w_in=v_ffn_w_in, v_ffn_w_out=v_ffn_w_out, v_mlstm_w_in=v_mlstm_w_in, v_mlstm_gate_bias=v_mlstm_gate_bias, v_mlstm_conv_w=v_mlstm_conv_w, v_mlstm_head_gain=v_mlstm_head_gain, v_mlstm_w_out=v_mlstm_w_out, v_attn_w_in=v_attn_w_in, v_attn_w_out=v_attn_w_out)
    weights = {n: given[n] for n in TWIN_WEIGHTS}
    shared = {n: given[n] for n in SHARED_INPUTS}
    per_example = {n: given[n] for n in ['x', 'c']}
    grad_fn = _jax.value_and_grad(_loss, argnums=(0, 1))

    def one_microbatch(ex, loss_target):
        ex = dict(ex)
        diff = ex.pop(TWIN_DIFF_INPUT)
        return grad_fn(weights, diff, {**shared, **ex}, loss_target)

    if N_MICROBATCH == 1:
        loss, (grad_w, grad_x) = one_microbatch(per_example, given["loss_target"])
    else:
        def body(carry, xs):
            loss_sum, grad_sum = carry
            l_k, (gw_k, gx_k) = one_microbatch(xs[0], xs[1])
            with _jax.named_scope("update"):
                return (loss_sum + l_k, _jax.tree.map(_jnp.add, grad_sum, gw_k)), gx_k

        init = (_jnp.zeros((), _jnp.float32), _jax.tree.map(_jnp.zeros_like, weights))
        (loss, grad_w), grad_x = _jax.lax.scan(body, init, (per_example, given["loss_target"]))
    with _jax.named_scope("update"):
        delta_w, new_m, new_v = {}, {}, {}
        for n in TWIN_WEIGHTS:
            delta_w[n], new_m[n], new_v[n] = _adamw(weights[n], grad_w[n], given["m_" + n], given["v_" + n])
    return (loss, grad_x, *[grad_w[n] for n in TWIN_WEIGHTS], *[delta_w[n] for n in TWIN_WEIGHTS],
            *[new_m[n] for n in TWIN_WEIGHTS], *[new_v[n] for n in TWIN_WEIGHTS])
```

```python
import functools
import math

import jax
import jax.numpy as jnp
from jax import lax
from jax.experimental import pallas as pl
from jax.experimental.pallas import tpu as pltpu

F32 = jnp.float32
BF16 = jnp.bfloat16

N_DEV = 8
MESH_AXES = ("x", "y", "c")
D = 1024
DEPTH = 2
D_FF = 2816
HEADS = 8
M_DQK = 64
M_DV = 128
M_CHUNK = 64
M_PROJ = 3088
M_PROJ_PAD = 3200
A_PROJ = 9216
DIL_GROUPS = ((128, 1), (512, 4), (2048, 16))
A_BLK = 128
ALPHA = (2 * DEPTH) ** 0.25
LN_EPS = 1e-5
RMS_EPS = 1e-6
ADAM_LR = 0.001
ADAM_B1 = 0.9
ADAM_B2 = 0.999
ADAM_EPS = 1e-08
ADAM_WD = 0.01
ADAM_STEP = 10
NEG = -1e30
V7X_VMEM_LIMIT = 56 * 1024 * 1024
PACK_COLS = 1024
MESH_ID = pl.DeviceIdType.MESH


def _cp(n_axes):
    return pltpu.CompilerParams(dimension_semantics=("arbitrary",) * n_axes,
                                vmem_limit_bytes=V7X_VMEM_LIMIT)


def _dot(a, b):
    return jnp.dot(a, b, preferred_element_type=F32)


def _dot_nt(a, b):
    return lax.dot_general(a, b, (((1,), (1,)), ((), ())), preferred_element_type=F32)


def _dot_tn(a, b):
    return lax.dot_general(a, b, (((0,), (0,)), ((), ())), preferred_element_type=F32)


def _sum0(a):
    return jnp.sum(a, axis=0, keepdims=True)


def _sum1(a):
    return jnp.sum(a, axis=1, keepdims=True)


def _round(a):
    return a.astype(BF16).astype(F32)


def _sigmoid(a):
    return 1.0 / (1.0 + jnp.exp(-a))


def _tile(n, pref):
    t = min(n, pref)
    while n % t:
        t //= 2
    return t


def all_gather(arrs, name):
    n = len(arrs)

    def body(*refs):
        ins, outs = refs[:n], refs[n:2 * n]
        send_sems, recv_sems, local_sems = refs[2 * n:]
        x, y, c = lax.axis_index("x"), lax.axis_index("y"), lax.axis_index("c")
        me, sibling = (x, y, c), (x, y, 1 - c)
        chips = [(1 - x, y), (x, 1 - y), (1 - x, 1 - y)]

        def slot(a, p):
            return outs[a].at[4 * p[0] + 2 * p[1] + p[2]]

        def copy(a, k, block, to, src=None):
            return pltpu.make_async_remote_copy(
                src_ref=slot(a, block) if src is None else src, dst_ref=slot(a, block),
                send_sem=send_sems.at[7 * a + k], recv_sem=recv_sems.at[7 * a + k],
                device_id=to, device_id_type=MESH_ID)

        mine, first, passed = [], [], []
        for a in range(n):
            cp = pltpu.make_async_copy(ins[a], slot(a, me), local_sems.at[a])
            cp.start()
            mine.append(cp)
            first.append(copy(a, 0, me, sibling, src=ins[a]))
            first += [copy(a, 1 + j, me, (*chip, c), src=ins[a]) for j, chip in enumerate(chips)]
        for cp in first:
            cp.start()
        for a in range(n):
            for j, chip in enumerate(chips):
                copy(a, 1 + j, (*chip, c), me).wait_recv()
                cp = copy(a, 4 + j, (*chip, c), sibling)
                cp.start()
                passed.append(cp)
        for a in range(n):
            copy(a, 0, sibling, me).wait_recv()
            for j, chip in enumerate(chips):
                copy(a, 4 + j, (*chip, 1 - c), me).wait_recv()
        for cp in first + passed:
            cp.wait_send()
        for cp in mine:
            cp.wait()

    any_spec = pl.BlockSpec(memory_space=pl.ANY)
    return pl.pallas_call(
        body, name=name,
        out_shape=[jax.ShapeDtypeStruct((N_DEV,) + a.shape, a.dtype) for a in arrs],
        in_specs=[any_spec] * n, out_specs=[any_spec] * n,
        scratch_shapes=[pltpu.SemaphoreType.DMA((7 * n,)), pltpu.SemaphoreType.DMA((7 * n,)),
                        pltpu.SemaphoreType.DMA((n,))],
    )(*arrs)


def exchange_shards(send, name):
    def body(s_ref, r_ref, send_sems, recv_sems, local_sem):
        x, y, c = lax.axis_index("x"), lax.axis_index("y"), lax.axis_index("c")
        me = 4 * x + 2 * y + c
        own = pltpu.make_async_copy(s_ref.at[me], r_ref.at[me], local_sem)
        own.start()
        copies = []
        for k in range(1, N_DEV):
            px = 1 - x if (k >> 2) & 1 else x
            py = 1 - y if (k >> 1) & 1 else y
            pc = 1 - c if k & 1 else c
            cp = pltpu.make_async_remote_copy(
                src_ref=s_ref.at[4 * px + 2 * py + pc], dst_ref=r_ref.at[me],
                send_sem=send_sems.at[k - 1], recv_sem=recv_sems.at[k - 1],
                device_id=(px, py, pc), device_id_type=MESH_ID)
            cp.start()
            copies.append(cp)
        for cp in copies:
            cp.wait_send()
            cp.wait_recv()
        own.wait()

    any_spec = pl.BlockSpec(memory_space=pl.ANY)
    return pl.pallas_call(
        body, name=name, out_shape=jax.ShapeDtypeStruct(send.shape, send.dtype),
        in_specs=[any_spec], out_specs=any_spec,
        scratch_shapes=[pltpu.SemaphoreType.DMA((N_DEV - 1,)), pltpu.SemaphoreType.DMA((N_DEV - 1,)),
                        pltpu.SemaphoreType.DMA(())],
    )(send)


def shard_sum(own, recv, onehot, name):
    R, C = own.shape
    tr = _tile(R, 512)

    def body(oh_ref, own_ref, recv_ref, o_ref):
        acc = None
        for j in range(N_DEV):
            term = jnp.where(oh_ref[j] > 0.5, own_ref[...], recv_ref[j].astype(F32))
            acc = term if acc is None else acc + term
        o_ref[...] = acc

    return pl.pallas_call(
        body, name=name, grid=(R // tr,),
        in_specs=[pl.BlockSpec(memory_space=pltpu.SMEM),
                  pl.BlockSpec((tr, C), lambda i: (i, 0)),
                  pl.BlockSpec((N_DEV, tr, C), lambda i: (0, i, 0))],
        out_specs=pl.BlockSpec((tr, C), lambda i: (i, 0)),
        out_shape=jax.ShapeDtypeStruct((R, C), F32), compiler_params=_cp(1),
    )(onehot, own, recv)


def sum_leading(a, name):
    _, R, C = a.shape

    def body(a_ref, o_ref):
        acc = a_ref[0]
        for j in range(1, N_DEV):
            acc = acc + a_ref[j]
        o_ref[...] = acc

    return pl.pallas_call(body, name=name, out_shape=jax.ShapeDtypeStruct((R, C), F32),
                          compiler_params=_cp(0))(a)


def modmm(x, mod3, w, out_dtype, name, tn):
    T, Dm = x.shape
    N = w.shape[1]
    Bl = mod3.shape[0]
    tm = _tile(T // Bl, 1024)
    tpb = T // Bl // tm

    def body(x_ref, mod_ref, w_ref, o_ref, h_ref, hs):
        @pl.when(pl.program_id(1) == 0)
        def _():
            m = mod_ref[0]
            hs[...] = (x_ref[...] * (1.0 + m[1:2, :]) + m[0:1, :]).astype(BF16)
            h_ref[...] = hs[...]
        o_ref[...] = _dot(hs[...], w_ref[...]).astype(o_ref.dtype)

    return pl.pallas_call(
        body, name=name, grid=(T // tm, N // tn),
        in_specs=[pl.BlockSpec((tm, Dm), lambda i, j: (i, 0)),
                  pl.BlockSpec((1, 3, Dm), lambda i, j: (i // tpb, 0, 0)),
                  pl.BlockSpec((Dm, tn), lambda i, j: (0, j))],
        out_specs=[pl.BlockSpec((tm, tn), lambda i, j: (i, j)),
                   pl.BlockSpec((tm, Dm), lambda i, j: (i, 0))],
        out_shape=[jax.ShapeDtypeStruct((T, N), out_dtype), jax.ShapeDtypeStruct((T, Dm), BF16)],
        scratch_shapes=[pltpu.VMEM((tm, Dm), BF16)], compiler_params=_cp(2),
    )(x, mod3, w)


def modmm_bwd(dp, w, x, mod3, dxres, name, tn):
    T, Dm = x.shape
    N = w.shape[1]
    Bl = mod3.shape[0]
    tm = _tile(T // Bl, 512)
    tpb = T // Bl // tm
    nj = N // tn

    def body(dp_ref, w_ref, x_ref, mod_ref, dxr_ref, dx_ref, dsh_ref, dsc_ref, acc):
        i, j = pl.program_id(0), pl.program_id(1)

        @pl.when(j == 0)
        def _():
            acc[...] = jnp.zeros_like(acc)
        acc[...] += _dot_nt(dp_ref[...], w_ref[...])

        @pl.when(j == nj - 1)
        def _():
            dh = acc[...]
            xx = x_ref[...]
            dx_ref[...] = dxr_ref[...] + dh * (1.0 + mod_ref[0][1:2, :])

            @pl.when(i % tpb == 0)
            def _():
                dsh_ref[...] = jnp.zeros_like(dsh_ref)
                dsc_ref[...] = jnp.zeros_like(dsc_ref)
            dsh_ref[0] += _sum0(dh)
            dsc_ref[0] += _sum0(dh * xx)

    return pl.pallas_call(
        body, name=name, grid=(T // tm, nj),
        in_specs=[pl.BlockSpec((tm, tn), lambda i, j: (i, j)),
                  pl.BlockSpec((Dm, tn), lambda i, j: (0, j)),
                  pl.BlockSpec((tm, Dm), lambda i, j: (i, 0)),
                  pl.BlockSpec((1, 3, Dm), lambda i, j: (i // tpb, 0, 0)),
                  pl.BlockSpec((tm, Dm), lambda i, j: (i, 0))],
        out_specs=[pl.BlockSpec((tm, Dm), lambda i, j: (i, 0)),
                   pl.BlockSpec((1, 1, Dm), lambda i, j: (i // tpb, 0, 0)),
                   pl.BlockSpec((1, 1, Dm), lambda i, j: (i // tpb, 0, 0))],
        out_shape=[jax.ShapeDtypeStruct((T, Dm), F32), jax.ShapeDtypeStruct((Bl, 1, Dm), F32),
                   jax.ShapeDtypeStruct((Bl, 1, Dm), F32)],
        scratch_shapes=[pltpu.VMEM((tm, Dm), F32)], compiler_params=_cp(2),
    )(dp, w, x, mod3, dxres)


def _ln_stats(z):
    mu = jnp.mean(z, axis=-1, keepdims=True)
    zc = z - mu
    var = jnp.mean(zc * zc, axis=-1, keepdims=True)
    rstd = lax.rsqrt(var + LN_EPS)
    return zc * rstd, rstd


def proj_post(a, w, x, mod3, lng, lnb, weight, name, tk):
    T, K = a.shape
    Dm = w.shape[1]
    Bl = mod3.shape[0]
    tm = _tile(T // Bl, 512)
    tpb = T // Bl // tm
    nk = K // tk

    def body(a_ref, w_ref, x_ref, mod_ref, g_ref, b_ref, out_ref, xn_ref, acc):
        k = pl.program_id(1)

        @pl.when(k == 0)
        def _():
            acc[...] = jnp.zeros_like(acc)
        acc[...] += _dot(a_ref[...], w_ref[...])

        @pl.when(k == nk - 1)
        def _():
            out = acc[...]
            out_ref[...] = out
            z = ALPHA * x_ref[...] + (weight * (1.0 + mod_ref[0][2:3, :])) * out
            xhat, _ = _ln_stats(z)
            xn_ref[...] = xhat * g_ref[...] + b_ref[...]

    return pl.pallas_call(
        body, name=name, grid=(T // tm, nk),
        in_specs=[pl.BlockSpec((tm, tk), lambda i, k: (i, k)),
                  pl.BlockSpec((tk, Dm), lambda i, k: (k, 0)),
                  pl.BlockSpec((tm, Dm), lambda i, k: (i, 0)),
                  pl.BlockSpec((1, 3, Dm), lambda i, k: (i // tpb, 0, 0)),
                  pl.BlockSpec((1, Dm), lambda i, k: (0, 0)),
                  pl.BlockSpec((1, Dm), lambda i, k: (0, 0))],
        out_specs=[pl.BlockSpec((tm, Dm), lambda i, k: (i, 0)),
                   pl.BlockSpec((tm, Dm), lambda i, k: (i, 0))],
        out_shape=[jax.ShapeDtypeStruct((T, Dm), F32), jax.ShapeDtypeStruct((T, Dm), F32)],
        scratch_shapes=[pltpu.VMEM((tm, Dm), F32)], compiler_params=_cp(2),
    )(a, w, x, mod3, lng, lnb)


def post_bwd(dxn, x, out, mod3, lng, w, weight, name, tk):
    T, Dm = x.shape
    K = w.shape[0]
    Bl = mod3.shape[0]
    tm = _tile(T // Bl, 512)
    tpb = T // Bl // tm

    def body(dxn_ref, x_ref, out_ref, mod_ref, g_ref, w_ref,
             dxr_ref, dout_ref, da_ref, dg_ref, db_ref, dgate_ref, dout_s):
        i, k = pl.program_id(0), pl.program_id(1)

        @pl.when(k == 0)
        def _():
            out = out_ref[...]
            dxn = dxn_ref[...]
            coef = weight * (1.0 + mod_ref[0][2:3, :])
            xhat, rstd = _ln_stats(ALPHA * x_ref[...] + coef * out)
            dyh = dxn * g_ref[...]
            dz = rstd * (dyh - jnp.mean(dyh, axis=-1, keepdims=True)
                         - xhat * jnp.mean(dyh * xhat, axis=-1, keepdims=True))
            dxr_ref[...] = ALPHA * dz
            dout_s[...] = (coef * dz).astype(BF16)
            dout_ref[...] = dout_s[...]

            @pl.when(i == 0)
            def _():
                dg_ref[...] = jnp.zeros_like(dg_ref)
                db_ref[...] = jnp.zeros_like(db_ref)

            @pl.when(i % tpb == 0)
            def _():
                dgate_ref[...] = jnp.zeros_like(dgate_ref)
            dg_ref[...] += _sum0(dxn * xhat)
            db_ref[...] += _sum0(dxn)
            dgate_ref[0] += _sum0((weight * out) * dz)
        da_ref[...] = _dot_nt(dout_s[...], w_ref[...]).astype(BF16)

    return pl.pallas_call(
        body, name=name, grid=(T // tm, K // tk),
        in_specs=[pl.BlockSpec((tm, Dm), lambda i, k: (i, 0)),
                  pl.BlockSpec((tm, Dm), lambda i, k: (i, 0)),
                  pl.BlockSpec((tm, Dm), lambda i, k: (i, 0)),
                  pl.BlockSpec((1, 3, Dm), lambda i, k: (i // tpb, 0, 0)),
                  pl.BlockSpec((1, Dm), lambda i, k: (0, 0)),
                  pl.BlockSpec((tk, Dm), lambda i, k: (k, 0))],
        out_specs=[pl.BlockSpec((tm, Dm), lambda i, k: (i, 0)),
                   pl.BlockSpec((tm, Dm), lambda i, k: (i, 0)),
                   pl.BlockSpec((tm, tk), lambda i, k: (i, k)),
                   pl.BlockSpec((1, Dm), lambda i, k: (0, 0)),
                   pl.BlockSpec((1, Dm), lambda i, k: (0, 0)),
                   pl.BlockSpec((1, 1, Dm), lambda i, k: (i // tpb, 0, 0))],
        out_shape=[jax.ShapeDtypeStruct((T, Dm), F32), jax.ShapeDtypeStruct((T, Dm), BF16),
                   jax.ShapeDtypeStruct((T, K), BF16), jax.ShapeDtypeStruct((1, Dm), F32),
                   jax.ShapeDtypeStruct((1, Dm), F32), jax.ShapeDtypeStruct((Bl, 1, Dm), F32)],
        scratch_shapes=[pltpu.VMEM((tm, Dm), BF16)], compiler_params=_cp(2),
    )(dxn, x, out, mod3, lng, w)


def mm_tn(a, b, name, tk, tn):
    T, K = a.shape
    N = b.shape[1]
    tt = _tile(T, 1024)

    def body(a_ref, b_ref, o_ref):
        @pl.when(pl.program_id(2) == 0)
        def _():
            o_ref[...] = jnp.zeros_like(o_ref)
        o_ref[...] += _dot_tn(a_ref[...], b_ref[...])

    return pl.pallas_call(
        body, name=name, grid=(K // tk, N // tn, T // tt),
        in_specs=[pl.BlockSpec((tt, tk), lambda i, j, t: (t, i)),
                  pl.BlockSpec((tt, tn), lambda i, j, t: (t, j))],
        out_specs=pl.BlockSpec((tk, tn), lambda i, j, t: (i, j)),
        out_shape=jax.ShapeDtypeStruct((K, N), F32), compiler_params=_cp(3),
    )(a, b)


def ffn_in(x, mod3, w, name, tf):
    T, Dm = x.shape
    Fh = w.shape[1] // 2
    Bl = mod3.shape[0]
    tm = _tile(T // Bl, 512)
    tpb = T // Bl // tm
    nj = Fh // tf

    def body(x_ref, mod_ref, wg_ref, wu_ref, a_ref, g_ref, u_ref, h_ref, hs):
        @pl.when(pl.program_id(1) == 0)
        def _():
            m = mod_ref[0]
            hs[...] = (x_ref[...] * (1.0 + m[1:2, :]) + m[0:1, :]).astype(BF16)
            h_ref[...] = hs[...]
        g = _dot(hs[...], wg_ref[...])
        u = _dot(hs[...], wu_ref[...])
        a_ref[...] = (g * _sigmoid(g) * u).astype(BF16)
        g_ref[...] = g.astype(BF16)
        u_ref[...] = u.astype(BF16)

    col = pl.BlockSpec((tm, tf), lambda i, j: (i, j))
    return pl.pallas_call(
        body, name=name, grid=(T // tm, nj),
        in_specs=[pl.BlockSpec((tm, Dm), lambda i, j: (i, 0)),
                  pl.BlockSpec((1, 3, Dm), lambda i, j: (i // tpb, 0, 0)),
                  pl.BlockSpec((Dm, tf), lambda i, j: (0, j)),
                  pl.BlockSpec((Dm, tf), lambda i, j: (0, nj + j))],
        out_specs=[col, col, col, pl.BlockSpec((tm, Dm), lambda i, j: (i, 0))],
        out_shape=[jax.ShapeDtypeStruct((T, Fh), BF16)] * 3 + [jax.ShapeDtypeStruct((T, Dm), BF16)],
        scratch_shapes=[pltpu.VMEM((tm, Dm), BF16)], compiler_params=_cp(2),
    )(x, mod3, w, w)


def swiglu_bwd(da, g, u, name):
    T, Fh = g.shape
    tm = _tile(T, 256)

    def body(da_ref, g_ref, u_ref, o_ref):
        gg = g_ref[...].astype(F32)
        da_ = da_ref[...].astype(F32)
        s = _sigmoid(gg)
        o_ref[:, :Fh] = (da_ * u_ref[...].astype(F32) * (s * (1.0 + gg * (1.0 - s)))).astype(BF16)
        o_ref[:, Fh:] = (da_ * (gg * s)).astype(BF16)

    half = pl.BlockSpec((tm, Fh), lambda i: (i, 0))
    return pl.pallas_call(
        body, name=name, grid=(T // tm,), in_specs=[half, half, half],
        out_specs=pl.BlockSpec((tm, 2 * Fh), lambda i: (i, 0)),
        out_shape=jax.ShapeDtypeStruct((T, 2 * Fh), BF16), compiler_params=_cp(1),
    )(da, g, u)


def loss_head(y, tgt, name):
    T, Dm = y.shape
    tm = _tile(T, 512)
    nt = T // tm

    def body(y_ref, t_ref, dy_ref, l_ref, acc):
        i = pl.program_id(0)

        @pl.when(i == 0)
        def _():
            acc[...] = jnp.zeros_like(acc)
        e = y_ref[...] - t_ref[...]
        dy_ref[...] = e * (1.0 / Dm)
        acc[...] += _sum0(e * e)

        @pl.when(i == nt - 1)
        def _():
            l_ref[...] = jnp.broadcast_to(_sum1(acc[...]) * (0.5 / Dm), l_ref.shape)

    return pl.pallas_call(
        body, name=name, grid=(nt,),
        in_specs=[pl.BlockSpec((tm, Dm), lambda i: (i, 0)), pl.BlockSpec((tm, Dm), lambda i: (i, 0))],
        out_specs=[pl.BlockSpec((tm, Dm), lambda i: (i, 0)), pl.BlockSpec((1, 128), lambda i: (0, 0))],
        out_shape=[jax.ShapeDtypeStruct((T, Dm), F32), jax.ShapeDtypeStruct((1, 128), F32)],
        scratch_shapes=[pltpu.VMEM((1, Dm), F32)], compiler_params=_cp(1),
    )(y, tgt)


def adamw(w, g, m, v, name):
    R, C = w.shape
    tr = _tile(R, 512) if R % 8 == 0 else R

    def body(w_ref, g_ref, m_ref, v_ref, d_ref, nm_ref, nv_ref):
        gg = g_ref[...]
        mm = ADAM_B1 * m_ref[...] + (1.0 - ADAM_B1) * gg
        vv = ADAM_B2 * v_ref[...] + (1.0 - ADAM_B2) * (gg * gg)
        m_hat = mm / (1.0 - ADAM_B1 ** ADAM_STEP)
        v_hat = vv / (1.0 - ADAM_B2 ** ADAM_STEP)
        d_ref[...] = -ADAM_LR * (m_hat / (jnp.sqrt(v_hat) + ADAM_EPS) + ADAM_WD * w_ref[...])
        nm_ref[...] = mm
        nv_ref[...] = vv

    spec = pl.BlockSpec((tr, C), lambda i: (i, 0))
    return pl.pallas_call(
        body, name=name, grid=(R // tr,), in_specs=[spec] * 4, out_specs=[spec] * 3,
        out_shape=[jax.ShapeDtypeStruct((R, C), F32)] * 3, compiler_params=_cp(1),
    )(w, g, m, v)


def ada_fwd(c_all, ada_w, ada_b_cols, name):
    Lr, Dm, Nc = ada_w.shape
    Bg = c_all.shape[0]

    def body(c_ref, w_ref, b_ref, o_ref):
        cc = c_ref[...]
        cond = cc * _sigmoid(cc)
        o_ref[0] = _dot(cond.astype(BF16), w_ref[0].astype(BF16)) + b_ref[0]

    return pl.pallas_call(
        body, name=name, grid=(Lr,),
        in_specs=[pl.BlockSpec((Bg, Dm), lambda l: (0, 0)),
                  pl.BlockSpec((1, Dm, Nc), lambda l: (l, 0, 0)),
                  pl.BlockSpec((1, 1, Nc), lambda l: (l, 0, 0))],
        out_specs=pl.BlockSpec((1, Bg, Nc), lambda l: (l, 0, 0)),
        out_shape=jax.ShapeDtypeStruct((Lr, Bg, Nc), F32), compiler_params=_cp(1),
    )(c_all, ada_w, ada_b_cols)


def ada_bwd(c_all_t, dmod_cols, dmod_all, name):
    Dm, Bg = c_all_t.shape
    Lr, _, Nc = dmod_cols.shape
    Nf = dmod_all.shape[2]

    def body(c_ref, dm_ref, da_ref, gw_ref, gb_ref):
        cc = c_ref[...]
        cond = cc * _sigmoid(cc)
        gw_ref[0] = _dot(cond.astype(BF16), dm_ref[0].astype(BF16))
        gb_ref[0] = _sum0(da_ref[0])

    return pl.pallas_call(
        body, name=name, grid=(Lr,),
        in_specs=[pl.BlockSpec((Dm, Bg), lambda l: (0, 0)),
                  pl.BlockSpec((1, Bg, Nc), lambda l: (l, 0, 0)),
                  pl.BlockSpec((1, Bg, Nf), lambda l: (l, 0, 0))],
        out_specs=[pl.BlockSpec((1, Dm, Nc), lambda l: (l, 0, 0)),
                   pl.BlockSpec((1, 1, Nf), lambda l: (l, 0, 0))],
        out_shape=[jax.ShapeDtypeStruct((Lr, Dm, Nc), F32), jax.ShapeDtypeStruct((Lr, 1, Nf), F32)],
        compiler_params=_cp(1),
    )(c_all_t, dmod_cols, dmod_all)


def _conv_taps(x, w, rows):
    shifted = [x]
    c = w[3:4, :] * x
    for k in range(1, 4):
        xs = jnp.where(rows >= k, pltpu.roll(x, k, 0), 0.0)
        shifted.append(xs)
        c = c + w[3 - k:4 - k, :] * xs
    return c, shifted


def conv_silu(proj3, conv_w, name):
    Bl, S, _ = proj3.shape
    ncb = conv_w.shape[1] // 128

    def body(x_ref, w_ref, o_ref):
        rows = lax.broadcasted_iota(jnp.int32, (S, 128), 0)
        c, _ = _conv_taps(_round(x_ref[0]), _round(w_ref[...]), rows)
        o_ref[0] = c * _sigmoid(c)

    return pl.pallas_call(
        body, name=name, grid=(Bl, ncb),
        in_specs=[pl.BlockSpec((1, S, 128), lambda b, j: (b, 0, j)),
                  pl.BlockSpec((4, 128), lambda b, j: (0, j))],
        out_specs=pl.BlockSpec((1, S, 128), lambda b, j: (b, 0, j)),
        out_shape=jax.ShapeDtypeStruct((Bl, S, conv_w.shape[1]), F32), compiler_params=_cp(2),
    )(proj3, conv_w)


def conv_silu_bwd(proj3, conv_w, dq, dk, name):
    Bl, S, _ = proj3.shape
    nq = dq.shape[2] // 128

    def body(x_ref, w_ref, dq_ref, dk_ref, dx_ref, dw_ref):
        j = pl.program_id(1)
        rows = lax.broadcasted_iota(jnp.int32, (S, 128), 0)
        w = _round(w_ref[...])
        c, shifted = _conv_taps(_round(x_ref[0]), w, rows)
        s = _sigmoid(c)
        dact = jnp.where(j < nq, dq_ref[0], dk_ref[0])
        dc = _round(dact * (s * (1.0 + c * (1.0 - s))))
        dx = w[3:4, :] * dc
        dws = [_sum0(dc * shifted[0])]
        for k in range(1, 4):
            up = jnp.where(rows < S - k, pltpu.roll(dc, S - k, 0), 0.0)
            dx = dx + w[3 - k:4 - k, :] * up
            dws.append(_sum0(dc * shifted[k]))
        dx_ref[0] = dx.astype(BF16)
        tap = lax.broadcasted_iota(jnp.int32, (4, 128), 0)
        dw_ref[0] = functools.reduce(lambda a, b: a + b, [jnp.where(tap == 3 - k, dws[k], 0.0) for k in range(4)])

    return pl.pallas_call(
        body, name=name, grid=(Bl, 2 * nq),
        in_specs=[pl.BlockSpec((1, S, 128), lambda b, j: (b, 0, j)),
                  pl.BlockSpec((4, 128), lambda b, j: (0, j)),
                  pl.BlockSpec((1, S, 128), lambda b, j: (b, 0, jnp.minimum(j, nq - 1))),
                  pl.BlockSpec((1, S, 128), lambda b, j: (b, 0, jnp.maximum(j - nq, 0)))],
        out_specs=[pl.BlockSpec((1, S, 128), lambda b, j: (b, 0, j)),
                   pl.BlockSpec((1, 4, 128), lambda b, j: (b, 0, j))],
        out_shape=[jax.ShapeDtypeStruct((Bl, S, 2 * nq * 128), BF16),
                   jax.ShapeDtypeStruct((Bl, 4, 2 * nq * 128), F32)],
        compiler_params=_cp(2),
    )(proj3, conv_w, dq, dk)


def _log_sigmoid(a):
    return jnp.minimum(a, 0.0) - jnp.log(1.0 + jnp.exp(-jnp.abs(a)))


def _chunk_state(kc, vc, gi, bcum, b_last, C, n, m):
    a = b_last - bcum + gi
    m_loc = jnp.max(a, axis=0, keepdims=True)
    wa = jnp.exp(a - m_loc)
    c_loc = _dot_tn((wa * vc).astype(BF16), kc.astype(BF16))
    n_loc = _sum0(_round(wa) * _round(kc))
    m_new = jnp.maximum(b_last + m, m_loc)
    sp = jnp.exp(b_last + m - m_new)
    sl = jnp.exp(m_loc - m_new)
    return sp * C + sl * c_loc, sp * n + sl * n_loc, m_new, wa, sp, sl


def _chunk_out(qs, kc, vc, gi_row, bcum, bcum_row, low, C, n, m):
    inter_log = bcum + m
    dlog = jnp.where(low, bcum - bcum_row + gi_row, NEG)
    m_i = jnp.maximum(inter_log, jnp.max(dlog, axis=1, keepdims=True))
    dm = jnp.exp(dlog - m_i)
    iw = jnp.exp(inter_log - m_i)
    qs_b, k_b, v_b = qs.astype(BF16), kc.astype(BF16), vc.astype(BF16)
    sc = _dot_nt(qs_b, k_b) * dm
    qc_ = _dot_nt(qs_b, C.astype(BF16))
    qn = _sum1(_round(qs) * _round(n))
    num = _dot(sc.astype(BF16), v_b) + iw * qc_
    den = _sum1(sc) + iw * qn
    floor = jnp.exp(-m_i)
    dn = jnp.maximum(jnp.abs(den), floor)
    return dict(hc=num / dn, den=den, dn=dn, floor=floor, sc=sc, dm=dm, iw=iw, qc=qc_, qn=qn,
                qs_b=qs_b, k_b=k_b, v_b=v_b)


def _cell_consts(L):
    ri = lax.broadcasted_iota(jnp.int32, (L, L), 0)
    ci = lax.broadcasted_iota(jnp.int32, (L, L), 1)
    return ri == ci, ci <= ri, ri <= ci


def _load_chunk(q_ref, k_ref, v_ref, g_ref, gb_ref, off, L, hh, h, lane):
    qmask = (lane >= M_DQK * hh) & (lane < M_DQK * (hh + 1))
    qc = jnp.where(qmask, q_ref[0, pl.ds(off, L), :], 0.0)
    kc = jnp.where(qmask, k_ref[0, pl.ds(off, L), :], 0.0)
    vc = v_ref[0, pl.ds(off, L), :]
    G = g_ref[0, pl.ds(off, L), :] + gb_ref[...]
    gi = _sum1(jnp.where(lane == h, G, 0.0))
    gf = _sum1(jnp.where(lane == h + HEADS, G, 0.0))
    return qmask, qc, kc, vc, gi, gf


def _gate_rows(gi, gf, eye, low, upp):
    lf = _log_sigmoid(gf)
    lf_row = _sum0(jnp.where(eye, lf, 0.0))
    gi_row = _sum0(jnp.where(eye, gi, 0.0))
    bcum = _sum1(jnp.where(low, lf_row, 0.0))
    bcum_row = _sum0(jnp.where(upp, lf, 0.0))
    b_last = _sum0(lf)
    return gi_row, bcum, bcum_row, b_last


def _cell_in_specs(S):
    nq = HEADS * M_DQK // 128
    return [pl.BlockSpec((1, S, 128), lambda b, h: (b, 0, h // 2)),
            pl.BlockSpec((1, S, 128), lambda b, h: (b, 0, nq + h // 2)),
            pl.BlockSpec((1, S, 128), lambda b, h: (b, 0, D // 128 + h)),
            pl.BlockSpec((1, S, 128), lambda b, h: (b, 0, 2 * D // 128 + h)),
            pl.BlockSpec((1, S, 128), lambda b, h: (b, 0, 3 * D // 128)),
            pl.BlockSpec((1, 128), lambda b, h: (0, h)),
            pl.BlockSpec((1, 128), lambda b, h: (0, 0))]


def mlstm_cell_fwd(qk3, proj3, gain, gbias, name):
    Bl, S, _ = qk3.shape
    L = M_CHUNK
    nc = S // L
    scale = M_DQK ** -0.5

    def body(q_ref, k_ref, v_ref, o_ref, g_ref, gain_ref, gb_ref, y_ref):
        h = pl.program_id(1)
        hh = h % 2
        lane = lax.broadcasted_iota(jnp.int32, (L, 128), 1)
        eye, low, upp = _cell_consts(L)
        gain_h = gain_ref[...]

        def step(c, carry):
            C, n, m = carry
            off = pl.multiple_of(c * L, L)
            _, qc, kc, vc, gi, gf = _load_chunk(q_ref, k_ref, v_ref, g_ref, gb_ref, off, L, hh, h, lane)
            gi_row, bcum, bcum_row, b_last = _gate_rows(gi, gf, eye, low, upp)
            r = _chunk_out(qc * scale, kc, vc, gi_row, bcum, bcum_row, low, C, n, m)
            hc = r["hc"]
            hn = hc * lax.rsqrt(jnp.mean(hc * hc, axis=-1, keepdims=True) + RMS_EPS)
            oc = o_ref[0, pl.ds(off, L), :]
            y_ref[0, pl.ds(off, L), :] = (_sigmoid(oc) * hn * gain_h).astype(BF16)
            C2, n2, m2, _, _, _ = _chunk_state(kc, vc, gi, bcum, b_last, C, n, m)
            return C2, n2, m2

        lax.fori_loop(0, nc, step, (jnp.zeros((M_DV, 128), F32), jnp.zeros((1, 128), F32),
                                    jnp.zeros((1, 1), F32)))

    return pl.pallas_call(
        body, name=name, grid=(Bl, HEADS), in_specs=_cell_in_specs(S),
        out_specs=pl.BlockSpec((1, S, 128), lambda b, h: (b, 0, h)),
        out_shape=jax.ShapeDtypeStruct((Bl, S, D), BF16), compiler_params=_cp(2),
    )(qk3, qk3, proj3, proj3, proj3, gain, gbias)


def mlstm_cell_bwd(qk3, proj3, gain, gbias, dy3, name):
    Bl, S, _ = qk3.shape
    L = M_CHUNK
    nc = S // L
    scale = M_DQK ** -0.5

    def body(q_ref, k_ref, v_ref, o_ref, g_ref, gain_ref, gb_ref, dy_ref,
             dq_ref, dk_ref, dv_ref, do_ref, dg_ref, dgain_ref, dgb_ref, C_all, n_all, m_all):
        h = pl.program_id(1)
        hh = h % 2
        lane = lax.broadcasted_iota(jnp.int32, (L, 128), 1)
        rowi = lax.broadcasted_iota(jnp.int32, (L, 1), 0)
        eye, low, upp = _cell_consts(L)
        gain_h = gain_ref[...]

        def head_backward():
            def fstep(c, carry):
                C, n, m = carry
                C_all[c] = C
                n_all[c] = n
                m_all[c] = jnp.broadcast_to(m, (1, 128))
                off = pl.multiple_of(c * L, L)
                _, _, kc, vc, gi, gf = _load_chunk(q_ref, k_ref, v_ref, g_ref, gb_ref, off, L, hh, h, lane)
                _, bcum, _, b_last = _gate_rows(gi, gf, eye, low, upp)
                C2, n2, m2, _, _, _ = _chunk_state(kc, vc, gi, bcum, b_last, C, n, m)
                return C2, n2, m2

            lax.fori_loop(0, nc, fstep, (jnp.zeros((M_DV, 128), F32), jnp.zeros((1, 128), F32),
                                         jnp.zeros((1, 1), F32)))

            def bstep(t, carry):
                dC_n, dn_n, dgain = carry
                c = nc - 1 - t
                off = pl.multiple_of(c * L, L)
                C, n, m = C_all[c], n_all[c], m_all[c][:, 0:1]
                qmask, qc, kc, vc, gi, gf = _load_chunk(q_ref, k_ref, v_ref, g_ref, gb_ref, off, L, hh, h, lane)
                gi_row, bcum, bcum_row, b_last = _gate_rows(gi, gf, eye, low, upp)
                qs = qc * scale
                r = _chunk_out(qs, kc, vc, gi_row, bcum, bcum_row, low, C, n, m)
                _, _, _, wa, sp, sl = _chunk_state(kc, vc, gi, bcum, b_last, C, n, m)
                hc, den, dn, sc, dm, iw, qn = r["hc"], r["den"], r["dn"], r["sc"], r["dm"], r["iw"], r["qn"]
                qs_b, k_b, v_b = r["qs_b"], r["k_b"], r["v_b"]
                dy = dy_ref[0, pl.ds(off, L), :].astype(F32)
                oc = o_ref[0, pl.ds(off, L), :]
                sig_o = _sigmoid(oc)
                rr = lax.rsqrt(jnp.mean(hc * hc, axis=-1, keepdims=True) + RMS_EPS)
                hn = hc * rr
                dgain = dgain + _sum0(dy * sig_o * hn)
                do_ref[0, pl.ds(off, L), :] = (
                    dy * hn * gain_h * sig_o * (1.0 - sig_o)).astype(BF16)
                dhn = dy * sig_o * gain_h
                dhc = rr * dhn - hc * (rr * rr * rr) * jnp.mean(dhn * hc, axis=-1, keepdims=True)
                dnum = dhc / dn
                gden = -_sum1(dhc * hc) / dn
                dden = jnp.where(jnp.abs(den) > r["floor"], gden * jnp.sign(den), 0.0)
                dnum_b = dnum.astype(BF16)
                dsc = _dot_nt(dnum_b, v_b) + dden
                dv = _dot_tn(sc.astype(BF16), dnum_b)
                diw = _sum1(dnum * r["qc"]) + dden * qn
                dqc_b = (iw * dnum).astype(BF16)
                wq = iw * dden
                dqs = _dot(dqc_b, C.astype(BF16)) + wq * n
                dC_out = _dot_tn(dqc_b, qs_b)
                dn_out = _sum0(wq * qs)
                dS_b = (dsc * dm).astype(BF16)
                gm = dsc * sc
                dqs = dqs + _dot(dS_b, k_b)
                dk = _dot_tn(dS_b, qs_b)
                dbc = _sum1(gm) + diw * iw
                colg = _sum0(gm)
                dC_p = sp * dC_n + dC_out
                dn_p = sp * dn_n + dn_out
                dcl_b = (sl * dC_n).astype(BF16)
                dn_loc = sl * dn_n
                dsp = _sum1(_sum0(dC_n * C)) + _sum1(dn_n * n)
                db_last = dsp * sp
                t1 = _dot(v_b, dcl_b) + dn_loc
                dwa = _sum1(t1 * kc)
                dv = dv + wa * _dot_nt(k_b, dcl_b)
                dk = dk + wa * t1
                da = dwa * wa
                db_last = db_last + _sum0(da)
                dbc = dbc - da + jnp.where(rowi == L - 1, db_last, 0.0)
                dbc_row = _sum0(jnp.where(eye, dbc, 0.0)) - colg
                dgi = da + _sum1(jnp.where(eye, colg, 0.0))
                dlf = _sum1(jnp.where(upp, dbc_row, 0.0))
                dgf = dlf * _sigmoid(-gf)
                dq = jnp.where(qmask, dqs * scale, 0.0)
                dk = jnp.where(qmask, dk, 0.0)
                slab = jnp.where(lane == h, dgi, 0.0) + jnp.where(lane == h + HEADS, dgf, 0.0)
                dv_ref[0, pl.ds(off, L), :] = dv.astype(BF16)

                @pl.when(hh == 0)
                def _():
                    dq_ref[0, pl.ds(off, L), :] = dq
                    dk_ref[0, pl.ds(off, L), :] = dk

                @pl.when(hh != 0)
                def _():
                    dq_ref[0, pl.ds(off, L), :] += dq
                    dk_ref[0, pl.ds(off, L), :] += dk

                @pl.when(h == 0)
                def _():
                    dg_ref[0, pl.ds(off, L), :] = slab

                @pl.when(h > 0)
                def _():
                    dg_ref[0, pl.ds(off, L), :] += slab
                return dC_p, dn_p, dgain

            _, _, dgain = lax.fori_loop(0, nc, bstep, (jnp.zeros((M_DV, 128), F32), jnp.zeros((1, 128), F32),
                                                       jnp.zeros((1, M_DV), F32)))
            dgain_ref[0] = dgain

        head_backward()

        @pl.when(h == HEADS - 1)
        def _():
            dgb_ref[0] = _sum0(dg_ref[0])

    return pl.pallas_call(
        body, name=name, grid=(Bl, HEADS),
        in_specs=_cell_in_specs(S) + [pl.BlockSpec((1, S, 128), lambda b, h: (b, 0, h))],
        out_specs=[pl.BlockSpec((1, S, 128), lambda b, h: (b, 0, h // 2)),
                   pl.BlockSpec((1, S, 128), lambda b, h: (b, 0, h // 2)),
                   pl.BlockSpec((1, S, 128), lambda b, h: (b, 0, h)),
                   pl.BlockSpec((1, S, 128), lambda b, h: (b, 0, h)),
                   pl.BlockSpec((1, S, 128), lambda b, h: (b, 0, 0)),
                   pl.BlockSpec((1, 1, 128), lambda b, h: (b, 0, h)),
                   pl.BlockSpec((1, 1, 128), lambda b, h: (b, 0, 0))],
        out_shape=[jax.ShapeDtypeStruct((Bl, S, D // 2), F32), jax.ShapeDtypeStruct((Bl, S, D // 2), F32),
                   jax.ShapeDtypeStruct((Bl, S, D), BF16), jax.ShapeDtypeStruct((Bl, S, D), BF16),
                   jax.ShapeDtypeStruct((Bl, S, 128), F32), jax.ShapeDtypeStruct((Bl, 1, D), F32),
                   jax.ShapeDtypeStruct((Bl, 1, 128), F32)],
        scratch_shapes=[pltpu.VMEM((nc, M_DV, 128), F32), pltpu.VMEM((nc, 1, 128), F32),
                        pltpu.VMEM((nc, 1, 128), F32)],
        compiler_params=_cp(2),
    )(qk3, qk3, proj3, proj3, proj3, gain, gbias, dy3)


def _attn_scores(q, kc, kp, n, row, col, scale):
    s_c = jnp.where(col <= row, _dot_nt(q, kc) * scale, NEG)
    s_p = jnp.where(jnp.logical_and(col >= row, n > 0), _dot_nt(q, kp) * scale, NEG)
    return s_c, s_p


def attn_fwd(proj, Bl, S, g, dil, name):
    Sd = S // dil
    nb = Sd // A_BLK
    scale = A_BLK ** -0.5
    pv = proj.reshape(Bl, Sd, dil * A_PROJ)
    ncol = A_PROJ // 128

    def body(q_ref, k_ref, v_ref, o_ref, l_ref):
        row = lax.broadcasted_iota(jnp.int32, (A_BLK, A_BLK), 0)
        col = lax.broadcasted_iota(jnp.int32, (A_BLK, A_BLK), 1)

        def step(n, carry):
            off = pl.multiple_of(n * A_BLK, A_BLK)
            offp = pl.multiple_of(jnp.maximum(n - 1, 0) * A_BLK, A_BLK)
            q = q_ref[0, pl.ds(off, A_BLK), :]
            s_c, s_p = _attn_scores(q, k_ref[0, pl.ds(off, A_BLK), :], k_ref[0, pl.ds(offp, A_BLK), :],
                                    n, row, col, scale)
            m = jnp.maximum(jnp.max(s_c, axis=1, keepdims=True), jnp.max(s_p, axis=1, keepdims=True))
            p_c = jnp.exp(s_c - m)
            p_p = jnp.exp(s_p - m)
            den = _sum1(p_c) + _sum1(p_p)
            o = _dot(p_c.astype(BF16), v_ref[0, pl.ds(off, A_BLK), :]) + _dot(
                p_p.astype(BF16), v_ref[0, pl.ds(offp, A_BLK), :])
            o_ref[0, pl.ds(off, A_BLK), :] = o / den
            l_ref[0, pl.ds(off, A_BLK), :] = jnp.broadcast_to(m + jnp.log(den), (A_BLK, 128))
            return carry

        lax.fori_loop(0, nb, step, 0)

    def spec(j):
        return pl.BlockSpec((1, Sd, 128), lambda b, r, h: (b, 0, r * ncol + g * 24 + j * HEADS + h))

    ospec = pl.BlockSpec((1, Sd, 128), lambda b, r, h: (b, 0, r * HEADS + h))
    o, lse = pl.pallas_call(
        body, name=name, grid=(Bl, dil, HEADS),
        in_specs=[spec(0), spec(1), spec(2)], out_specs=[ospec, ospec],
        out_shape=[jax.ShapeDtypeStruct((Bl, Sd, dil * D), F32)] * 2, compiler_params=_cp(3),
    )(pv, pv, pv)
    return o.reshape(Bl * S, D), lse.reshape(Bl * S, D)


def attn_merge(os_, lses, name):
    T = os_[0].shape[0]
    tm = _tile(T, 512)
    ng = len(os_)

    def body(*refs):
        o_refs, l_refs = refs[:ng], refs[ng:2 * ng]
        ob_ref, of_ref, lt_ref = refs[2 * ng:]
        ls = [r[...] for r in l_refs]
        m = functools.reduce(jnp.maximum, ls)
        ws = [jnp.exp(l - m) for l in ls]
        den = functools.reduce(lambda a, b: a + b, ws)
        o = functools.reduce(lambda a, b: a + b, [w * r[...] for w, r in zip(ws, o_refs)]) / den
        of_ref[...] = o
        ob_ref[...] = o.astype(BF16)
        lt_ref[...] = m + jnp.log(den)

    spec = pl.BlockSpec((tm, D), lambda i: (i, 0))
    return pl.pallas_call(
        body, name=name, grid=(T // tm,), in_specs=[spec] * (2 * ng), out_specs=[spec] * 3,
        out_shape=[jax.ShapeDtypeStruct((T, D), BF16), jax.ShapeDtypeStruct((T, D), F32),
                   jax.ShapeDtypeStruct((T, D), F32)],
        compiler_params=_cp(1),
    )(*os_, *lses)


def attn_bwd(proj, do, o, lse, Bl, S, g, dil, name):
    Sd = S // dil
    nb = Sd // A_BLK
    scale = A_BLK ** -0.5
    pv = proj.reshape(Bl, Sd, dil * A_PROJ)
    ncol = A_PROJ // 128
    dov, ov, lv = (t.reshape(Bl, Sd, dil * D) for t in (do, o, lse))

    def body(q_ref, k_ref, v_ref, do_ref, o_ref, l_ref, dq_ref, dk_ref, dv_ref, dk_s, dv_s):
        row = lax.broadcasted_iota(jnp.int32, (A_BLK, A_BLK), 0)
        col = lax.broadcasted_iota(jnp.int32, (A_BLK, A_BLK), 1)
        dk_s[...] = jnp.zeros_like(dk_s)
        dv_s[...] = jnp.zeros_like(dv_s)

        def step(n, carry):
            off = pl.multiple_of(n * A_BLK, A_BLK)
            offp = pl.multiple_of(jnp.maximum(n - 1, 0) * A_BLK, A_BLK)
            q = q_ref[0, pl.ds(off, A_BLK), :]
            kc, kp = k_ref[0, pl.ds(off, A_BLK), :], k_ref[0, pl.ds(offp, A_BLK), :]
            vc, vp = v_ref[0, pl.ds(off, A_BLK), :], v_ref[0, pl.ds(offp, A_BLK), :]
            do_b = do_ref[0, pl.ds(off, A_BLK), :]
            delta = _sum1(do_b.astype(F32) * o_ref[0, pl.ds(off, A_BLK), :])
            lt = l_ref[0, pl.ds(off, A_BLK), :][:, 0:1]
            s_c, s_p = _attn_scores(q, kc, kp, n, row, col, scale)
            p_c = jnp.exp(s_c - lt)
            p_p = jnp.exp(s_p - lt)
            ds_c = (p_c * (_dot_nt(do_b, vc) - delta) * scale).astype(BF16)
            ds_p = (p_p * (_dot_nt(do_b, vp) - delta) * scale).astype(BF16)
            dq_ref[0, pl.ds(off, A_BLK), :] = (_dot(ds_c, kc) + _dot(ds_p, kp)).astype(BF16)
            dk_s[pl.ds(off, A_BLK), :] += _dot_tn(ds_c, q)
            dk_s[pl.ds(offp, A_BLK), :] += _dot_tn(ds_p, q)
            dv_s[pl.ds(off, A_BLK), :] += _dot_tn(p_c.astype(BF16), do_b)
            dv_s[pl.ds(offp, A_BLK), :] += _dot_tn(p_p.astype(BF16), do_b)
            return carry

        lax.fori_loop(0, nb, step, 0)
        dk_ref[0] = dk_s[...].astype(BF16)
        dv_ref[0] = dv_s[...].astype(BF16)

    def spec(j):
        return pl.BlockSpec((1, Sd, 128), lambda b, r, h: (b, 0, r * ncol + g * 24 + j * HEADS + h))

    ospec = pl.BlockSpec((1, Sd, 128), lambda b, r, h: (b, 0, r * HEADS + h))
    outs = pl.pallas_call(
        body, name=name, grid=(Bl, dil, HEADS),
        in_specs=[spec(0), spec(1), spec(2), ospec, ospec, ospec], out_specs=[ospec] * 3,
        out_shape=[jax.ShapeDtypeStruct((Bl, Sd, dil * D), BF16)] * 3,
        scratch_shapes=[pltpu.VMEM((Sd, 128), F32), pltpu.VMEM((Sd, 128), F32)],
        compiler_params=_cp(3),
    )(pv, pv, pv, dov, ov, lv)
    return [t.reshape(Bl * S, D) for t in outs]


def ffn_fwd(x, mod3, w_in, w_out, lng, lnb, tag):
    a, g, u, h = ffn_in(x, mod3, w_in, tag + "_in", tf=1408)
    out, xn = proj_post(a, w_out, x, mod3, lng, lnb, 0.5, tag + "_out", tk=1408)
    return xn, (x, out, g, u, h, a)


def ffn_bwd(dxn, saved, mod3, w_in, w_out, lng, tag):
    x, out, g, u, h, a = saved
    dxres, dout, da, dlg, dlb, dgate = post_bwd(dxn, x, out, mod3, lng, w_out, 0.5, tag + "_outb", tk=1408)
    dgu = swiglu_bwd(da, g, u, tag + "_actb")
    dx, dsh, dsc = modmm_bwd(dgu, w_in, x, mod3, dxres, tag + "_inb", tn=1408)
    dw_in = mm_tn(h, dgu, tag + "_dwin", tk=1024, tn=1408)
    dw_out = mm_tn(a, dout, tag + "_dwout", tk=1408, tn=1024)
    return dx, dw_in, dw_out, dlg, dlb, jnp.concatenate([dsh, dsc, dgate], axis=1)


def mlstm_fwd(x, mod3, w_in, w_out, conv_w, gain, gbias, lng, lnb, Bl, S):
    proj, h = modmm(x, mod3, w_in, F32, "ml_in", tn=M_PROJ_PAD // 5)
    proj3 = proj.reshape(Bl, S, M_PROJ_PAD)
    qk3 = conv_silu(proj3, conv_w, "ml_conv")
    y3 = mlstm_cell_fwd(qk3, proj3, gain, gbias, "ml_cell")
    y = y3.reshape(Bl * S, D)
    out, xn = proj_post(y, w_out, x, mod3, lng, lnb, 1.0, "ml_out", tk=D)
    return xn, (x, out, h, proj3, qk3, y)


def mlstm_bwd(dxn, saved, mod3, w_in, w_out, conv_w, gain, gbias, lng, Bl, S):
    x, out, h, proj3, qk3, y = saved
    dxres, dout, dy, dlg, dlb, dgate = post_bwd(dxn, x, out, mod3, lng, w_out, 1.0, "ml_outb", tk=D)
    dq, dk, dv, do, dg, dgain, dgb = mlstm_cell_bwd(qk3, proj3, gain, gbias, dy.reshape(Bl, S, D), "ml_cellb")
    dqk, dconv = conv_silu_bwd(proj3, conv_w, dq, dk, "ml_convb")
    dproj = jnp.concatenate([dqk, dv, do, dg.astype(BF16)], axis=2).reshape(Bl * S, M_PROJ_PAD)
    dx, dsh, dsc = modmm_bwd(dproj, w_in, x, mod3, dxres, "ml_inb", tn=M_PROJ_PAD // 5)
    dw_in = mm_tn(h, dproj, "ml_dwin", tk=1024, tn=M_PROJ_PAD // 5)
    dw_out = mm_tn(y, dout, "ml_dwout", tk=1024, tn=1024)
    small = (jnp.sum(dconv, axis=0), jnp.sum(dgain, axis=0), jnp.sum(dgb, axis=0)[:, :2 * HEADS])
    return dx, dw_in[:, :M_PROJ], dw_out, dlg, dlb, jnp.concatenate([dsh, dsc, dgate], axis=1), small


def attn_mixer_fwd(x, mod3, w_in, w_out, lng, lnb, Bl, S):
    proj, h = modmm(x, mod3, w_in, BF16, "at_in", tn=2304)
    os_, lses = [], []
    for g, (_, dil) in enumerate(DIL_GROUPS):
        o_g, l_g = attn_fwd(proj, Bl, S, g, dil, "at_core%d" % g)
        os_.append(o_g)
        lses.append(l_g)
    ob, of, lt = attn_merge(os_, lses, "at_merge")
    out, xn = proj_post(ob, w_out, x, mod3, lng, lnb, 1.0, "at_out", tk=D)
    return xn, (x, out, h, proj, ob, of, lt)


def attn_mixer_bwd(dxn, saved, mod3, w_in, w_out, lng, Bl, S):
    x, out, h, proj, ob, of, lt = saved
    dxres, dout, do, dlg, dlb, dgate = post_bwd(dxn, x, out, mod3, lng, w_out, 1.0, "at_outb", tk=D)
    parts = []
    for g, (_, dil) in enumerate(DIL_GROUPS):
        parts += attn_bwd(proj, do, of, lt, Bl, S, g, dil, "at_coreb%d" % g)
    dproj = jnp.concatenate(parts, axis=1)
    dx, dsh, dsc = modmm_bwd(dproj, w_in, x, mod3, dxres, "at_inb", tn=2304)
    dw_in = mm_tn(h, dproj, "at_dwin", tk=1024, tn=2304)
    dw_out = mm_tn(ob, dout, "at_dwout", tk=1024, tn=1024)
    return dx, dw_in, dw_out, dlg, dlb, jnp.concatenate([dsh, dsc, dgate], axis=1)


BIG = ("ffn_w_in", "ffn_w_out", "mlstm_w_in", "mlstm_w_out", "attn_w_in", "attn_w_out")
SHARD_AXIS = {"ffn_w_in": 3, "ffn_w_out": 2, "mlstm_w_in": 2, "mlstm_w_out": 1, "attn_w_in": 2, "attn_w_out": 1}


def _pack_rows(shards):
    rows = [math.prod(s.shape) // PACK_COLS for s in shards]
    padded = [-(-r // 16) * 16 for r in rows]
    total = -(-sum(padded) // 512) * 512
    return rows, padded, total


def _pad_rows(a, n, axis):
    pad = [(0, 0)] * a.ndim
    pad[axis] = (0, n - a.shape[axis])
    return jnp.pad(a, pad)


def _unstack(stacked, axis):
    full = jnp.moveaxis(stacked, 0, axis)
    shp = list(full.shape)
    shp[axis:axis + 2] = [shp[axis] * shp[axis + 1]]
    return full.reshape(shp)


def _restack(full, axis):
    shp = list(full.shape)
    shp[axis:axis + 1] = [N_DEV, shp[axis] // N_DEV]
    return jnp.moveaxis(full.reshape(shp), axis, 0)


def kernel(x, c, ada_w, ada_b, ln_g, ln_b, ffn_w_in, ffn_w_out, mlstm_w_in, mlstm_gate_bias, mlstm_conv_w, mlstm_head_gain, mlstm_w_out, attn_w_in, attn_w_out, loss_target, m_ada_w, m_ada_b, m_ln_g, m_ln_b, m_ffn_w_in, m_ffn_w_out, m_mlstm_w_in, m_mlstm_gate_bias, m_mlstm_conv_w, m_mlstm_head_gain, m_mlstm_w_out, m_attn_w_in, m_attn_w_out, v_ada_w, v_ada_b, v_ln_g, v_ln_b, v_ffn_w_in, v_ffn_w_out, v_mlstm_w_in, v_mlstm_gate_bias, v_mlstm_conv_w, v_mlstm_head_gain, v_mlstm_w_out, v_attn_w_in, v_attn_w_out):
    Bl, S, _ = x.shape
    T = Bl * S
    Bg = Bl * N_DEV
    me = 4 * lax.axis_index("x") + 2 * lax.axis_index("y") + lax.axis_index("c")
    onehot = (jnp.arange(N_DEV) == me).astype(F32)
    weights = dict(ada_w=ada_w, ada_b=ada_b, ln_g=ln_g, ln_b=ln_b, ffn_w_in=ffn_w_in, ffn_w_out=ffn_w_out,
                   mlstm_w_in=mlstm_w_in, mlstm_gate_bias=mlstm_gate_bias, mlstm_conv_w=mlstm_conv_w,
                   mlstm_head_gain=mlstm_head_gain, mlstm_w_out=mlstm_w_out, attn_w_in=attn_w_in,
                   attn_w_out=attn_w_out)
    m_in = dict(ada_w=m_ada_w, ada_b=m_ada_b, ln_g=m_ln_g, ln_b=m_ln_b, ffn_w_in=m_ffn_w_in,
                ffn_w_out=m_ffn_w_out, mlstm_w_in=m_mlstm_w_in, mlstm_gate_bias=m_mlstm_gate_bias,
                mlstm_conv_w=m_mlstm_conv_w, mlstm_head_gain=m_mlstm_head_gain, mlstm_w_out=m_mlstm_w_out,
                attn_w_in=m_attn_w_in, attn_w_out=m_attn_w_out)
    v_in = dict(ada_w=v_ada_w, ada_b=v_ada_b, ln_g=v_ln_g, ln_b=v_ln_b, ffn_w_in=v_ffn_w_in,
                ffn_w_out=v_ffn_w_out, mlstm_w_in=v_mlstm_w_in, mlstm_gate_bias=v_mlstm_gate_bias,
                mlstm_conv_w=v_mlstm_conv_w, mlstm_head_gain=v_mlstm_head_gain, mlstm_w_out=v_mlstm_w_out,
                attn_w_in=v_attn_w_in, attn_w_out=v_attn_w_out)

    rows, prows, total = _pack_rows([weights[k] for k in BIG])
    pack = jnp.concatenate([_pad_rows(weights[k].astype(BF16).reshape(-1, PACK_COLS), pr, 0)
                            for k, pr in zip(BIG, prows)]
                           + [jnp.zeros((total - sum(prows), PACK_COLS), BF16)], axis=0)
    small = jnp.concatenate([c.reshape(-1), ln_g.reshape(-1), ln_b.reshape(-1), mlstm_conv_w.reshape(-1)])
    n_small = small.shape[0]
    small = jnp.pad(small, (0, -n_small % (8 * PACK_COLS))).reshape(-1, PACK_COLS)
    pack_all, small_all = all_gather([pack, small], "ag_params")
    full = {}
    r0 = 0
    for k, r, pr in zip(BIG, rows, prows):
        full[k] = _unstack(pack_all[:, r0:r0 + r].reshape((N_DEV,) + weights[k].shape), SHARD_AXIS[k])
        r0 += pr
    small_flat = small_all.reshape(N_DEV, -1)
    o0 = 0
    c_all = small_flat[:, o0:o0 + c.size].reshape(Bg, D)
    o0 += c.size
    lng_full = _unstack(small_flat[:, o0:o0 + ln_g.size].reshape((N_DEV,) + ln_g.shape), 2)
    o0 += ln_g.size
    lnb_full = _unstack(small_flat[:, o0:o0 + ln_b.size].reshape((N_DEV,) + ln_b.shape), 2)
    o0 += ln_b.size
    conv_full = _unstack(small_flat[:, o0:o0 + mlstm_conv_w.size].reshape((N_DEV,) + mlstm_conv_w.shape), 2)[0]
    mw_in = jnp.pad(full["mlstm_w_in"][0], ((0, 0), (0, M_PROJ_PAD - M_PROJ)))
    gbias = jnp.pad(mlstm_gate_bias, ((0, 0), (0, 128 - 2 * HEADS)))

    ncols = ada_w.shape[2]
    ada_b_cols = lax.dynamic_slice_in_dim(ada_b, me * ncols, ncols, axis=1).reshape(DEPTH, 1, ncols)
    mod_cols = ada_fwd(c_all, ada_w, ada_b_cols, "ada_fwd")
    (mod_g,) = all_gather([mod_cols.reshape(DEPTH * Bg, ncols)], "ag_mod")
    mod_full = _unstack(mod_g.reshape(N_DEV, DEPTH, Bg, ncols), 2)
    mod_mine = lax.dynamic_slice_in_dim(mod_full, me * Bl, Bl, axis=1).reshape(DEPTH, Bl, 3, 3, D)

    xt = x.reshape(T, D)
    saved = []
    for layer in range(DEPTH):
        def lnp(s, layer=layer):
            return lng_full[layer, s].reshape(1, D), lnb_full[layer, s].reshape(1, D)
        md = mod_mine[layer]
        xt, sv0 = ffn_fwd(xt, md[:, 0], full["ffn_w_in"][layer, 0], full["ffn_w_out"][layer, 0], *lnp(0),
                          "f%da" % layer)
        if layer % 2 == 0:
            xt, sv1 = mlstm_fwd(xt, md[:, 1], mw_in, full["mlstm_w_out"][0], conv_full, mlstm_head_gain, gbias,
                                *lnp(1), Bl, S)
        else:
            xt, sv1 = attn_mixer_fwd(xt, md[:, 1], full["attn_w_in"][0], full["attn_w_out"][0], *lnp(1), Bl, S)
        xt, sv2 = ffn_fwd(xt, md[:, 2], full["ffn_w_in"][layer, 1], full["ffn_w_out"][layer, 1], *lnp(2),
                          "f%db" % layer)
        saved.append((sv0, sv1, sv2))

    dxt, lsum = loss_head(xt, loss_target.reshape(T, D), "loss")
    loss = lax.psum(lsum[0, 0], MESH_AXES)

    gfull = {}
    dmod, dlg_all, dlb_all = [None] * DEPTH, [None] * DEPTH, [None] * DEPTH
    dw_fin, dw_fout = [[None, None] for _ in range(DEPTH)], [[None, None] for _ in range(DEPTH)]
    ml_small = None
    for layer in reversed(range(DEPTH)):
        md = mod_mine[layer]
        sv0, sv1, sv2 = saved[layer]
        dxt, dwi, dwo, dlg2, dlb2, dm2 = ffn_bwd(dxt, sv2, md[:, 2], full["ffn_w_in"][layer, 1],
                                                 full["ffn_w_out"][layer, 1],
                                                 lng_full[layer, 2].reshape(1, D), "f%db" % layer)
        dw_fin[layer][1], dw_fout[layer][1] = dwi, dwo
        lg1 = lng_full[layer, 1].reshape(1, D)
        if layer % 2 == 0:
            dxt, gwi, gwo, dlg1, dlb1, dm1, ml_small = mlstm_bwd(
                dxt, sv1, md[:, 1], mw_in, full["mlstm_w_out"][0], conv_full, mlstm_head_gain, gbias, lg1, Bl, S)
            gfull["mlstm_w_in"], gfull["mlstm_w_out"] = gwi[None], gwo[None]
        else:
            dxt, gwi, gwo, dlg1, dlb1, dm1 = attn_mixer_bwd(
                dxt, sv1, md[:, 1], full["attn_w_in"][0], full["attn_w_out"][0], lg1, Bl, S)
            gfull["attn_w_in"], gfull["attn_w_out"] = gwi[None], gwo[None]
        dxt, dwi, dwo, dlg0, dlb0, dm0 = ffn_bwd(dxt, sv0, md[:, 0], full["ffn_w_in"][layer, 0],
                                                 full["ffn_w_out"][layer, 0],
                                                 lng_full[layer, 0].reshape(1, D), "f%da" % layer)
        dw_fin[layer][0], dw_fout[layer][0] = dwi, dwo
        dmod[layer] = jnp.stack([dm0, dm1, dm2], axis=1).reshape(Bl, 9 * D)
        dlg_all[layer] = jnp.concatenate([dlg0, dlg1, dlg2], axis=0)
        dlb_all[layer] = jnp.concatenate([dlb0, dlb1, dlb2], axis=0)
    grad_x = dxt.reshape(Bl, S, D)
    gfull["ffn_w_in"] = jnp.stack([jnp.stack(r) for r in dw_fin])
    gfull["ffn_w_out"] = jnp.stack([jnp.stack(r) for r in dw_fout])

    gpack = jnp.concatenate(
        [_pad_rows(_restack(gfull[k], SHARD_AXIS[k]).reshape(N_DEV, -1, PACK_COLS), pr, 1)
         for k, pr in zip(BIG, prows)]
        + [jnp.zeros((N_DEV, total - sum(prows), PACK_COLS), F32)], axis=1)
    own = lax.dynamic_index_in_dim(gpack, me, axis=0, keepdims=False)
    recv = exchange_shards(gpack.astype(BF16), "rs_grads")
    gsum = shard_sum(own, recv, onehot, "rs_sum")
    grads = {}
    r0 = 0
    for k, r, pr in zip(BIG, rows, prows):
        grads[k] = gsum[r0:r0 + r].reshape(weights[k].shape)
        r0 += pr

    dconv, dgain, dgbias = ml_small
    parts = [jnp.stack(dmod).reshape(-1), dgbias.reshape(-1), dgain.reshape(-1),
             jnp.stack(dlg_all).reshape(-1), jnp.stack(dlb_all).reshape(-1), dconv.reshape(-1)]
    sizes = [p.shape[0] for p in parts]
    flat = jnp.concatenate(parts)
    flat = jnp.pad(flat, (0, -flat.shape[0] % (8 * PACK_COLS))).reshape(-1, PACK_COLS)
    (sm_all,) = all_gather([flat], "ag_small")
    sm_sum = sum_leading(sm_all, "small_sum").reshape(-1)
    dmod_all = sm_all.reshape(N_DEV, -1)[:, :sizes[0]].reshape(N_DEV, DEPTH, Bl, 9 * D)
    dmod_all = jnp.moveaxis(dmod_all, 0, 1).reshape(DEPTH, Bg, 9 * D)
    o0 = sizes[0]
    grads["mlstm_gate_bias"] = sm_sum[o0:o0 + sizes[1]].reshape(mlstm_gate_bias.shape)
    o0 += sizes[1]
    grads["mlstm_head_gain"] = sm_sum[o0:o0 + sizes[2]].reshape(mlstm_head_gain.shape)
    o0 += sizes[2]
    nl = ln_g.shape[2]
    g_lng = sm_sum[o0:o0 + sizes[3]].reshape(DEPTH, 3, D)
    o0 += sizes[3]
    g_lnb = sm_sum[o0:o0 + sizes[4]].reshape(DEPTH, 3, D)
    o0 += sizes[4]
    g_conv = sm_sum[o0:o0 + sizes[5]].reshape(1, 4, D)
    grads["ln_g"] = lax.dynamic_slice_in_dim(g_lng, me * nl, nl, axis=2)
    grads["ln_b"] = lax.dynamic_slice_in_dim(g_lnb, me * nl, nl, axis=2)
    grads["mlstm_conv_w"] = lax.dynamic_slice_in_dim(g_conv, me * nl, nl, axis=2)
    dmod_cols = lax.dynamic_slice_in_dim(dmod_all, me * ncols, ncols, axis=2)
    gw, gb = ada_bwd(c_all.T, dmod_cols, dmod_all, "ada_bwd")
    grads["ada_w"] = gw
    grads["ada_b"] = gb.reshape(ada_b.shape)

    names = ["ada_w", "ada_b", "ln_g", "ln_b", "ffn_w_in", "ffn_w_out", "mlstm_w_in", "mlstm_gate_bias",
             "mlstm_conv_w", "mlstm_head_gain", "mlstm_w_out", "attn_w_in", "attn_w_out"]
    deltas, new_m, new_v = [], [], []
    for k in names:
        w = weights[k]
        shp2 = (math.prod(w.shape[:-1]), w.shape[-1])
        d_, m_, v_ = adamw(w.reshape(shp2), grads[k].reshape(shp2), m_in[k].reshape(shp2), v_in[k].reshape(shp2),
                           "adamw_" + k)
        deltas.append(d_.reshape(w.shape))
        new_m.append(m_.reshape(w.shape))
        new_v.append(v_.reshape(w.shape))
    return (loss, grad_x, *[grads[k] for k in names], *deltas, *new_m, *new_v)
```

```python
import functools
import math

import jax
import jax.numpy as jnp
from jax import lax
from jax.experimental import pallas as pl
from jax.experimental.pallas import tpu as pltpu

F32 = jnp.float32
BF16 = jnp.bfloat16

N_DEV = 8
MESH_AXES = ("x", "y", "c")
D = 1024
DEPTH = 2
D_FF = 2816
HEADS = 8
M_DQK = 64
M_DV = 128
M_CHUNK = 64
M_SLAB = 512
M_PROJ = 3088
M_PROJ_PAD = 3200
A_PROJ = 9216
DIL_GROUPS = ((128, 1), (512, 4), (2048, 16))
A_BLK = 128
ALPHA = (2 * DEPTH) ** 0.25
LN_EPS = 1e-5
RMS_EPS = 1e-6
ADAM_LR = 0.001
ADAM_B1 = 0.9
ADAM_B2 = 0.999
ADAM_EPS = 1e-08
ADAM_WD = 0.01
ADAM_STEP = 10
NEG = -1e30
V7X_VMEM_LIMIT = 56 * 1024 * 1024
PACK_COLS = 1024
MESH_ID = pl.DeviceIdType.MESH


def _cp(n_axes):
    return pltpu.CompilerParams(dimension_semantics=("arbitrary",) * n_axes,
                                vmem_limit_bytes=V7X_VMEM_LIMIT)


def _dot(a, b):
    return jnp.dot(a, b, preferred_element_type=F32)


def _dot_nt(a, b):
    return lax.dot_general(a, b, (((1,), (1,)), ((), ())), preferred_element_type=F32)


def _dot_tn(a, b):
    return lax.dot_general(a, b, (((0,), (0,)), ((), ())), preferred_element_type=F32)


def _sum0(a):
    return jnp.sum(a, axis=0, keepdims=True)


def _sum1(a):
    return jnp.sum(a, axis=1, keepdims=True)


def _round(a):
    return a.astype(BF16).astype(F32)


def _sigmoid(a):
    return 1.0 / (1.0 + jnp.exp(-a))


def _tile(n, pref):
    t = min(n, pref)
    while n % t:
        t //= 2
    return t


def all_gather(arrs, name):
    n = len(arrs)

    def body(*refs):
        ins, outs = refs[:n], refs[n:2 * n]
        send_sems, recv_sems, local_sems = refs[2 * n:]
        x, y, c = lax.axis_index("x"), lax.axis_index("y"), lax.axis_index("c")
        me, sibling = (x, y, c), (x, y, 1 - c)
        chips = [(1 - x, y), (x, 1 - y), (1 - x, 1 - y)]

        def slot(a, p):
            return outs[a].at[4 * p[0] + 2 * p[1] + p[2]]

        def copy(a, k, block, to, src=None):
            return pltpu.make_async_remote_copy(
                src_ref=slot(a, block) if src is None else src, dst_ref=slot(a, block),
                send_sem=send_sems.at[7 * a + k], recv_sem=recv_sems.at[7 * a + k],
                device_id=to, device_id_type=MESH_ID)

        mine, first, passed = [], [], []
        for a in range(n):
            cp = pltpu.make_async_copy(ins[a], slot(a, me), local_sems.at[a])
            cp.start()
            mine.append(cp)
            first.append(copy(a, 0, me, sibling, src=ins[a]))
            first += [copy(a, 1 + j, me, (*chip, c), src=ins[a]) for j, chip in enumerate(chips)]
        for cp in first:
            cp.start()
        for a in range(n):
            for j, chip in enumerate(chips):
                copy(a, 1 + j, (*chip, c), me).wait_recv()
                cp = copy(a, 4 + j, (*chip, c), sibling)
                cp.start()
                passed.append(cp)
        for a in range(n):
            copy(a, 0, sibling, me).wait_recv()
            for j, chip in enumerate(chips):
                copy(a, 4 + j, (*chip, 1 - c), me).wait_recv()
        for cp in first + passed:
            cp.wait_send()
        for cp in mine:
            cp.wait()

    any_spec = pl.BlockSpec(memory_space=pl.ANY)
    return pl.pallas_call(
        body, name=name,
        out_shape=[jax.ShapeDtypeStruct((N_DEV,) + a.shape, a.dtype) for a in arrs],
        in_specs=[any_spec] * n, out_specs=[any_spec] * n,
        scratch_shapes=[pltpu.SemaphoreType.DMA((7 * n,)), pltpu.SemaphoreType.DMA((7 * n,)),
                        pltpu.SemaphoreType.DMA((n,))],
    )(*arrs)


def exchange_shards(send, name):
    def body(s_ref, r_ref, send_sems, recv_sems, local_sem):
        x, y, c = lax.axis_index("x"), lax.axis_index("y"), lax.axis_index("c")
        me = 4 * x + 2 * y + c
        own = pltpu.make_async_copy(s_ref.at[me], r_ref.at[me], local_sem)
        own.start()
        copies = []
        for k in range(1, N_DEV):
            px = 1 - x if (k >> 2) & 1 else x
            py = 1 - y if (k >> 1) & 1 else y
            pc = 1 - c if k & 1 else c
            cp = pltpu.make_async_remote_copy(
                src_ref=s_ref.at[4 * px + 2 * py + pc], dst_ref=r_ref.at[me],
                send_sem=send_sems.at[k - 1], recv_sem=recv_sems.at[k - 1],
                device_id=(px, py, pc), device_id_type=MESH_ID)
            cp.start()
            copies.append(cp)
        for cp in copies:
            cp.wait_send()
            cp.wait_recv()
        own.wait()

    any_spec = pl.BlockSpec(memory_space=pl.ANY)
    return pl.pallas_call(
        body, name=name, out_shape=jax.ShapeDtypeStruct(send.shape, send.dtype),
        in_specs=[any_spec], out_specs=any_spec,
        scratch_shapes=[pltpu.SemaphoreType.DMA((N_DEV - 1,)), pltpu.SemaphoreType.DMA((N_DEV - 1,)),
                        pltpu.SemaphoreType.DMA(())],
    )(send)


def shard_sum(own, recv, onehot, name):
    R, C = own.shape
    tr = _tile(R, 512)

    def body(oh_ref, own_ref, recv_ref, o_ref):
        acc = None
        for j in range(N_DEV):
            term = jnp.where(oh_ref[j] > 0.5, own_ref[...], recv_ref[j].astype(F32))
            acc = term if acc is None else acc + term
        o_ref[...] = acc

    return pl.pallas_call(
        body, name=name, grid=(R // tr,),
        in_specs=[pl.BlockSpec(memory_space=pltpu.SMEM),
                  pl.BlockSpec((tr, C), lambda i: (i, 0)),
                  pl.BlockSpec((N_DEV, tr, C), lambda i: (0, i, 0))],
        out_specs=pl.BlockSpec((tr, C), lambda i: (i, 0)),
        out_shape=jax.ShapeDtypeStruct((R, C), F32), compiler_params=_cp(1),
    )(onehot, own, recv)


def sum_leading(a, name):
    _, R, C = a.shape

    def body(a_ref, o_ref):
        acc = a_ref[0]
        for j in range(1, N_DEV):
            acc = acc + a_ref[j]
        o_ref[...] = acc

    return pl.pallas_call(body, name=name, out_shape=jax.ShapeDtypeStruct((R, C), F32),
                          compiler_params=_cp(0))(a)


def modmm(x, mod3, w, out_dtype, name, tn):
    T, Dm = x.shape
    N = w.shape[1]
    Bl = mod3.shape[0]
    tm = _tile(T // Bl, 1024)
    tpb = T // Bl // tm

    def body(x_ref, mod_ref, w_ref, o_ref, h_ref, hs):
        @pl.when(pl.program_id(1) == 0)
        def _():
            m = mod_ref[0]
            hs[...] = (x_ref[...] * (1.0 + m[1:2, :]) + m[0:1, :]).astype(BF16)
            h_ref[...] = hs[...]
        o_ref[...] = _dot(hs[...], w_ref[...]).astype(o_ref.dtype)

    return pl.pallas_call(
        body, name=name, grid=(T // tm, N // tn),
        in_specs=[pl.BlockSpec((tm, Dm), lambda i, j: (i, 0)),
                  pl.BlockSpec((1, 3, Dm), lambda i, j: (i // tpb, 0, 0)),
                  pl.BlockSpec((Dm, tn), lambda i, j: (0, j))],
        out_specs=[pl.BlockSpec((tm, tn), lambda i, j: (i, j)),
                   pl.BlockSpec((tm, Dm), lambda i, j: (i, 0))],
        out_shape=[jax.ShapeDtypeStruct((T, N), out_dtype), jax.ShapeDtypeStruct((T, Dm), BF16)],
        scratch_shapes=[pltpu.VMEM((tm, Dm), BF16)], compiler_params=_cp(2),
    )(x, mod3, w)


def modmm_bwd(dp, w, x, mod3, dxres, name, tn, tm=512):
    T, Dm = x.shape
    N = w.shape[1]
    Bl = mod3.shape[0]
    tm = _tile(T // Bl, tm)
    tpb = T // Bl // tm
    nj = N // tn

    def body(dp_ref, w_ref, x_ref, mod_ref, dxr_ref, dx_ref, dsh_ref, dsc_ref, acc):
        i, j = pl.program_id(0), pl.program_id(1)

        @pl.when(j == 0)
        def _():
            acc[...] = jnp.zeros_like(acc)
        acc[...] += _dot_nt(dp_ref[...], w_ref[...])

        @pl.when(j == nj - 1)
        def _():
            dh = acc[...]
            xx = x_ref[...]
            dx_ref[...] = dxr_ref[...] + dh * (1.0 + mod_ref[0][1:2, :])

            @pl.when(i % tpb == 0)
            def _():
                dsh_ref[...] = jnp.zeros_like(dsh_ref)
                dsc_ref[...] = jnp.zeros_like(dsc_ref)
            dsh_ref[0] += _sum0(dh)
            dsc_ref[0] += _sum0(dh * xx)

    return pl.pallas_call(
        body, name=name, grid=(T // tm, nj),
        in_specs=[pl.BlockSpec((tm, tn), lambda i, j: (i, j)),
                  pl.BlockSpec((Dm, tn), lambda i, j: (0, j)),
                  pl.BlockSpec((tm, Dm), lambda i, j: (i, 0)),
                  pl.BlockSpec((1, 3, Dm), lambda i, j: (i // tpb, 0, 0)),
                  pl.BlockSpec((tm, Dm), lambda i, j: (i, 0))],
        out_specs=[pl.BlockSpec((tm, Dm), lambda i, j: (i, 0)),
                   pl.BlockSpec((1, 1, Dm), lambda i, j: (i // tpb, 0, 0)),
                   pl.BlockSpec((1, 1, Dm), lambda i, j: (i // tpb, 0, 0))],
        out_shape=[jax.ShapeDtypeStruct((T, Dm), F32), jax.ShapeDtypeStruct((Bl, 1, Dm), F32),
                   jax.ShapeDtypeStruct((Bl, 1, Dm), F32)],
        scratch_shapes=[pltpu.VMEM((tm, Dm), F32)], compiler_params=_cp(2),
    )(dp, w, x, mod3, dxres)


def _ln_stats(z):
    mu = jnp.mean(z, axis=-1, keepdims=True)
    zc = z - mu
    var = jnp.mean(zc * zc, axis=-1, keepdims=True)
    rstd = lax.rsqrt(var + LN_EPS)
    return zc * rstd, rstd


def proj_post(a, w, x, mod3, lng, lnb, weight, name, tk):
    T, K = a.shape
    Dm = w.shape[1]
    Bl = mod3.shape[0]
    tm = _tile(T // Bl, 512)
    tpb = T // Bl // tm
    nk = K // tk

    def body(a_ref, w_ref, x_ref, mod_ref, g_ref, b_ref, out_ref, xn_ref, acc):
        k = pl.program_id(1)

        @pl.when(k == 0)
        def _():
            acc[...] = jnp.zeros_like(acc)
        acc[...] += _dot(a_ref[...], w_ref[...])

        @pl.when(k == nk - 1)
        def _():
            out = acc[...]
            out_ref[...] = out
            z = ALPHA * x_ref[...] + (weight * (1.0 + mod_ref[0][2:3, :])) * out
            xhat, _ = _ln_stats(z)
            xn_ref[...] = xhat * g_ref[...] + b_ref[...]

    return pl.pallas_call(
        body, name=name, grid=(T // tm, nk),
        in_specs=[pl.BlockSpec((tm, tk), lambda i, k: (i, k)),
                  pl.BlockSpec((tk, Dm), lambda i, k: (k, 0)),
                  pl.BlockSpec((tm, Dm), lambda i, k: (i, 0)),
                  pl.BlockSpec((1, 3, Dm), lambda i, k: (i // tpb, 0, 0)),
                  pl.BlockSpec((1, Dm), lambda i, k: (0, 0)),
                  pl.BlockSpec((1, Dm), lambda i, k: (0, 0))],
        out_specs=[pl.BlockSpec((tm, Dm), lambda i, k: (i, 0)),
                   pl.BlockSpec((tm, Dm), lambda i, k: (i, 0))],
        out_shape=[jax.ShapeDtypeStruct((T, Dm), F32), jax.ShapeDtypeStruct((T, Dm), F32)],
        scratch_shapes=[pltpu.VMEM((tm, Dm), F32)], compiler_params=_cp(2),
    )(a, w, x, mod3, lng, lnb)


def post_bwd(dxn, x, out, mod3, lng, w, weight, name, tk, tm=512, gu=None):
    T, Dm = x.shape
    K = w.shape[0]
    Bl = mod3.shape[0]
    tm = _tile(T // Bl, tm)
    tpb = T // Bl // tm
    fused = gu is not None
    assert not fused or tk == K

    def body(dxn_ref, x_ref, out_ref, mod_ref, g_ref, w_ref, *rest):
        if fused:
            gg_ref, uu_ref = rest[:2]
            rest = rest[2:]
        dxr_ref, dout_ref, da_ref, dg_ref, db_ref, dgate_ref, dout_s = rest
        i, k = pl.program_id(0), pl.program_id(1)

        @pl.when(k == 0)
        def _():
            out = out_ref[...]
            dxn = dxn_ref[...]
            coef = weight * (1.0 + mod_ref[0][2:3, :])
            xhat, rstd = _ln_stats(ALPHA * x_ref[...] + coef * out)
            dyh = dxn * g_ref[...]
            dz = rstd * (dyh - jnp.mean(dyh, axis=-1, keepdims=True)
                         - xhat * jnp.mean(dyh * xhat, axis=-1, keepdims=True))
            dxr_ref[...] = ALPHA * dz
            dout_s[...] = (coef * dz).astype(BF16)
            dout_ref[...] = dout_s[...]

            @pl.when(i == 0)
            def _():
                dg_ref[...] = jnp.zeros_like(dg_ref)
                db_ref[...] = jnp.zeros_like(db_ref)

            @pl.when(i % tpb == 0)
            def _():
                dgate_ref[...] = jnp.zeros_like(dgate_ref)
            dg_ref[...] += _sum0(dxn * xhat)
            db_ref[...] += _sum0(dxn)
            dgate_ref[0] += _sum0((weight * out) * dz)
        da = _dot_nt(dout_s[...], w_ref[...])
        if fused:
            gg = gg_ref[...].astype(F32)
            s = _sigmoid(gg)
            da_ref[:, :K] = (da * uu_ref[...].astype(F32) * (s * (1.0 + gg * (1.0 - s)))).astype(BF16)
            da_ref[:, K:] = (da * (gg * s)).astype(BF16)
        else:
            da_ref[...] = da.astype(BF16)

    row = pl.BlockSpec((tm, Dm), lambda i, k: (i, 0))
    vec = pl.BlockSpec((1, Dm), lambda i, k: (0, 0))
    wide = pl.BlockSpec((tm, tk), lambda i, k: (i, k))
    da_cols = 2 * K if fused else K
    return pl.pallas_call(
        body, name=name, grid=(T // tm, K // tk),
        in_specs=[row, row, row, pl.BlockSpec((1, 3, Dm), lambda i, k: (i // tpb, 0, 0)), vec,
                  pl.BlockSpec((tk, Dm), lambda i, k: (k, 0))] + ([wide, wide] if fused else []),
        out_specs=[row, row,
                   pl.BlockSpec((tm, da_cols), lambda i, k: (i, 0)) if fused else wide,
                   vec, vec, pl.BlockSpec((1, 1, Dm), lambda i, k: (i // tpb, 0, 0))],
        out_shape=[jax.ShapeDtypeStruct((T, Dm), F32), jax.ShapeDtypeStruct((T, Dm), BF16),
                   jax.ShapeDtypeStruct((T, da_cols), BF16), jax.ShapeDtypeStruct((1, Dm), F32),
                   jax.ShapeDtypeStruct((1, Dm), F32), jax.ShapeDtypeStruct((Bl, 1, Dm), F32)],
        scratch_shapes=[pltpu.VMEM((tm, Dm), BF16)], compiler_params=_cp(2),
    )(dxn, x, out, mod3, lng, w, *(gu if fused else ()))


def mm_tn(a, b, name, tk, tn):
    T, K = a.shape
    N = b.shape[1]
    tt = _tile(T, 1024)

    def body(a_ref, b_ref, o_ref):
        @pl.when(pl.program_id(2) == 0)
        def _():
            o_ref[...] = jnp.zeros_like(o_ref)
        o_ref[...] += _dot_tn(a_ref[...], b_ref[...])

    return pl.pallas_call(
        body, name=name, grid=(K // tk, N // tn, T // tt),
        in_specs=[pl.BlockSpec((tt, tk), lambda i, j, t: (t, i)),
                  pl.BlockSpec((tt, tn), lambda i, j, t: (t, j))],
        out_specs=pl.BlockSpec((tk, tn), lambda i, j, t: (i, j)),
        out_shape=jax.ShapeDtypeStruct((K, N), F32), compiler_params=_cp(3),
    )(a, b)


def ffn_in(x, mod3, w, name, tf):
    T, Dm = x.shape
    Fh = w.shape[1] // 2
    Bl = mod3.shape[0]
    tm = _tile(T // Bl, 512)
    tpb = T // Bl // tm
    nj = Fh // tf

    def body(x_ref, mod_ref, wg_ref, wu_ref, a_ref, g_ref, u_ref, h_ref):
        m = mod_ref[0]
        h = (x_ref[...] * (1.0 + m[1:2, :]) + m[0:1, :]).astype(BF16)
        h_ref[0] = h
        g = _dot(h, wg_ref[...])
        u = _dot(h, wu_ref[...])
        a_ref[...] = (g * _sigmoid(g) * u).astype(BF16)
        g_ref[...] = g.astype(BF16)
        u_ref[...] = u.astype(BF16)

    col = pl.BlockSpec((tm, tf), lambda j, i: (i, j))
    a, g, u, h = pl.pallas_call(
        body, name=name, grid=(nj, T // tm),
        in_specs=[pl.BlockSpec((tm, Dm), lambda j, i: (i, 0)),
                  pl.BlockSpec((1, 3, Dm), lambda j, i: (i // tpb, 0, 0)),
                  pl.BlockSpec((Dm, tf), lambda j, i: (0, j)),
                  pl.BlockSpec((Dm, tf), lambda j, i: (0, nj + j))],
        out_specs=[col, col, col, pl.BlockSpec((1, tm, Dm), lambda j, i: (j, i, 0))],
        out_shape=[jax.ShapeDtypeStruct((T, Fh), BF16)] * 3 + [jax.ShapeDtypeStruct((nj, T, Dm), BF16)],
        compiler_params=_cp(2),
    )(x, mod3, w, w)
    return a, g, u, h[0]


def loss_head(y, tgt, name):
    T, Dm = y.shape
    tm = _tile(T, 512)
    nt = T // tm

    def body(y_ref, t_ref, dy_ref, l_ref, acc):
        i = pl.program_id(0)

        @pl.when(i == 0)
        def _():
            acc[...] = jnp.zeros_like(acc)
        e = y_ref[...] - t_ref[...]
        dy_ref[...] = e * (1.0 / Dm)
        acc[...] += _sum0(e * e)

        @pl.when(i == nt - 1)
        def _():
            l_ref[...] = jnp.broadcast_to(_sum1(acc[...]) * (0.5 / Dm), l_ref.shape)

    return pl.pallas_call(
        body, name=name, grid=(nt,),
        in_specs=[pl.BlockSpec((tm, Dm), lambda i: (i, 0)), pl.BlockSpec((tm, Dm), lambda i: (i, 0))],
        out_specs=[pl.BlockSpec((tm, Dm), lambda i: (i, 0)), pl.BlockSpec((1, 128), lambda i: (0, 0))],
        out_shape=[jax.ShapeDtypeStruct((T, Dm), F32), jax.ShapeDtypeStruct((1, 128), F32)],
        scratch_shapes=[pltpu.VMEM((1, Dm), F32)], compiler_params=_cp(1),
    )(y, tgt)


def adamw(w, g, m, v, name):
    R, C = w.shape
    tr = _tile(R, 512) if R % 8 == 0 else R

    def body(w_ref, g_ref, m_ref, v_ref, d_ref, nm_ref, nv_ref):
        gg = g_ref[...]
        mm = ADAM_B1 * m_ref[...] + (1.0 - ADAM_B1) * gg
        vv = ADAM_B2 * v_ref[...] + (1.0 - ADAM_B2) * (gg * gg)
        m_hat = mm / (1.0 - ADAM_B1 ** ADAM_STEP)
        v_hat = vv / (1.0 - ADAM_B2 ** ADAM_STEP)
        d_ref[...] = -ADAM_LR * (m_hat / (jnp.sqrt(v_hat) + ADAM_EPS) + ADAM_WD * w_ref[...])
        nm_ref[...] = mm
        nv_ref[...] = vv

    spec = pl.BlockSpec((tr, C), lambda i: (i, 0))
    return pl.pallas_call(
        body, name=name, grid=(R // tr,), in_specs=[spec] * 4, out_specs=[spec] * 3,
        out_shape=[jax.ShapeDtypeStruct((R, C), F32)] * 3, compiler_params=_cp(1),
    )(w, g, m, v)


def ada_fwd(c_all, ada_w, ada_b_cols, name):
    Lr, Dm, Nc = ada_w.shape
    Bg = c_all.shape[0]

    def body(c_ref, w_ref, b_ref, o_ref):
        cc = c_ref[...]
        cond = cc * _sigmoid(cc)
        o_ref[0] = _dot(cond.astype(BF16), w_ref[0].astype(BF16)) + b_ref[0]

    return pl.pallas_call(
        body, name=name, grid=(Lr,),
        in_specs=[pl.BlockSpec((Bg, Dm), lambda l: (0, 0)),
                  pl.BlockSpec((1, Dm, Nc), lambda l: (l, 0, 0)),
                  pl.BlockSpec((1, 1, Nc), lambda l: (l, 0, 0))],
        out_specs=pl.BlockSpec((1, Bg, Nc), lambda l: (l, 0, 0)),
        out_shape=jax.ShapeDtypeStruct((Lr, Bg, Nc), F32), compiler_params=_cp(1),
    )(c_all, ada_w, ada_b_cols)


def ada_bwd(c_all_t, dmod_cols, dmod_all, name):
    Dm, Bg = c_all_t.shape
    Lr, _, Nc = dmod_cols.shape
    Nf = dmod_all.shape[2]

    def body(c_ref, dm_ref, da_ref, gw_ref, gb_ref):
        cc = c_ref[...]
        cond = cc * _sigmoid(cc)
        gw_ref[0] = _dot(cond.astype(BF16), dm_ref[0].astype(BF16))
        gb_ref[0] = _sum0(da_ref[0])

    return pl.pallas_call(
        body, name=name, grid=(Lr,),
        in_specs=[pl.BlockSpec((Dm, Bg), lambda l: (0, 0)),
                  pl.BlockSpec((1, Bg, Nc), lambda l: (l, 0, 0)),
                  pl.BlockSpec((1, Bg, Nf), lambda l: (l, 0, 0))],
        out_specs=[pl.BlockSpec((1, Dm, Nc), lambda l: (l, 0, 0)),
                   pl.BlockSpec((1, 1, Nf), lambda l: (l, 0, 0))],
        out_shape=[jax.ShapeDtypeStruct((Lr, Dm, Nc), F32), jax.ShapeDtypeStruct((Lr, 1, Nf), F32)],
        compiler_params=_cp(1),
    )(c_all_t, dmod_cols, dmod_all)


def _conv_taps(x, w, rows):
    shifted = [x]
    c = w[3:4, :] * x
    for k in range(1, 4):
        xs = jnp.where(rows >= k, pltpu.roll(x, k, 0), 0.0)
        shifted.append(xs)
        c = c + w[3 - k:4 - k, :] * xs
    return c, shifted


def conv_silu(proj3, conv_w, name):
    Bl, S, _ = proj3.shape
    ncb = conv_w.shape[1] // 128

    def body(x_ref, w_ref, o_ref):
        rows = lax.broadcasted_iota(jnp.int32, (S, 128), 0)
        c, _ = _conv_taps(_round(x_ref[0]), _round(w_ref[...]), rows)
        o_ref[0] = c * _sigmoid(c)

    return pl.pallas_call(
        body, name=name, grid=(Bl, ncb),
        in_specs=[pl.BlockSpec((1, S, 128), lambda b, j: (b, 0, j)),
                  pl.BlockSpec((4, 128), lambda b, j: (0, j))],
        out_specs=pl.BlockSpec((1, S, 128), lambda b, j: (b, 0, j)),
        out_shape=jax.ShapeDtypeStruct((Bl, S, conv_w.shape[1]), F32), compiler_params=_cp(2),
    )(proj3, conv_w)


def conv_silu_bwd(proj3, conv_w, dq, dk, name):
    Bl, S, _ = proj3.shape
    nq = dq.shape[2] // 128

    def body(x_ref, w_ref, dq_ref, dk_ref, dx_ref, dw_ref):
        j = pl.program_id(1)
        rows = lax.broadcasted_iota(jnp.int32, (S, 128), 0)
        w = _round(w_ref[...])
        c, shifted = _conv_taps(_round(x_ref[0]), w, rows)
        s = _sigmoid(c)
        dact = jnp.where(j < nq, dq_ref[0], dk_ref[0])
        dc = _round(dact * (s * (1.0 + c * (1.0 - s))))
        dx = w[3:4, :] * dc
        dws = [_sum0(dc * shifted[0])]
        for k in range(1, 4):
            up = jnp.where(rows < S - k, pltpu.roll(dc, S - k, 0), 0.0)
            dx = dx + w[3 - k:4 - k, :] * up
            dws.append(_sum0(dc * shifted[k]))
        dx_ref[0] = dx.astype(BF16)
        tap = lax.broadcasted_iota(jnp.int32, (4, 128), 0)
        dw_ref[0] = functools.reduce(lambda a, b: a + b, [jnp.where(tap == 3 - k, dws[k], 0.0) for k in range(4)])

    return pl.pallas_call(
        body, name=name, grid=(Bl, 2 * nq),
        in_specs=[pl.BlockSpec((1, S, 128), lambda b, j: (b, 0, j)),
                  pl.BlockSpec((4, 128), lambda b, j: (0, j)),
                  pl.BlockSpec((1, S, 128), lambda b, j: (b, 0, jnp.minimum(j, nq - 1))),
                  pl.BlockSpec((1, S, 128), lambda b, j: (b, 0, jnp.maximum(j - nq, 0)))],
        out_specs=[pl.BlockSpec((1, S, 128), lambda b, j: (b, 0, j)),
                   pl.BlockSpec((1, 4, 128), lambda b, j: (b, 0, j))],
        out_shape=[jax.ShapeDtypeStruct((Bl, S, 2 * nq * 128), BF16),
                   jax.ShapeDtypeStruct((Bl, 4, 2 * nq * 128), F32)],
        compiler_params=_cp(2),
    )(proj3, conv_w, dq, dk)


def _log_sigmoid(a):
    return jnp.minimum(a, 0.0) - jnp.log(1.0 + jnp.exp(-jnp.abs(a)))


def _chunk_state(kc, vc, gi, bcum, b_last, C, n, m):
    a = b_last - bcum + gi
    m_loc = jnp.max(a, axis=0, keepdims=True)
    wa = jnp.exp(a - m_loc)
    c_loc = _dot_tn((wa * vc).astype(BF16), kc.astype(BF16))
    n_loc = _sum0(_round(wa) * _round(kc))
    m_new = jnp.maximum(b_last + m, m_loc)
    sp = jnp.exp(b_last + m - m_new)
    sl = jnp.exp(m_loc - m_new)
    return sp * C + sl * c_loc, sp * n + sl * n_loc, m_new, wa, sp, sl


def _chunk_out(qs, kc, vc, gi_row, bcum, bcum_row, low, C, n, m):
    inter_log = bcum + m
    dlog = jnp.where(low, bcum - bcum_row + gi_row, NEG)
    m_i = jnp.maximum(inter_log, jnp.max(dlog, axis=1, keepdims=True))
    dm = jnp.exp(dlog - m_i)
    iw = jnp.exp(inter_log - m_i)
    qs_b, k_b, v_b = qs.astype(BF16), kc.astype(BF16), vc.astype(BF16)
    sc = _dot_nt(qs_b, k_b) * dm
    qc_ = _dot_nt(qs_b, C.astype(BF16))
    qn = _sum1(_round(qs) * _round(n))
    num = _dot(sc.astype(BF16), v_b) + iw * qc_
    den = _sum1(sc) + iw * qn
    floor = jnp.exp(-m_i)
    dn = jnp.maximum(jnp.abs(den), floor)
    return dict(hc=num / dn, den=den, dn=dn, floor=floor, sc=sc, dm=dm, iw=iw, qc=qc_, qn=qn,
                qs_b=qs_b, k_b=k_b, v_b=v_b)


def _cell_consts(L):
    ri = lax.broadcasted_iota(jnp.int32, (L, L), 0)
    ci = lax.broadcasted_iota(jnp.int32, (L, L), 1)
    return ri == ci, ci <= ri, ri <= ci


def _load_chunk(q_ref, k_ref, v_ref, G, off, L, h, lane):
    hh = h % 2
    qmask = (lane >= M_DQK * hh) & (lane < M_DQK * (hh + 1))
    pair = pl.ds(128 * (h // 2), 128)
    qc = jnp.where(qmask, q_ref[0, pl.ds(off, L), pair], 0.0)
    kc = jnp.where(qmask, k_ref[0, pl.ds(off, L), pair], 0.0)
    vc = v_ref[0, pl.ds(off, L), pl.ds(M_DV * h, M_DV)]
    gi = _sum1(jnp.where(lane == h, G, 0.0))
    gf = _sum1(jnp.where(lane == h + HEADS, G, 0.0))
    return qmask, qc, kc, vc, gi, gf


def _gate_rows(gi, gf, eye, low, upp):
    lf = _log_sigmoid(gf)
    lf_row = _sum0(jnp.where(eye, lf, 0.0))
    gi_row = _sum0(jnp.where(eye, gi, 0.0))
    bcum = _sum1(jnp.where(low, lf_row, 0.0))
    bcum_row = _sum0(jnp.where(upp, lf, 0.0))
    b_last = _sum0(lf)
    return gi_row, bcum, bcum_row, b_last


def _cell_specs(SB, cpb, blk):
    def seq(width, col):
        return pl.BlockSpec((1, SB, width), lambda b, s: (b, blk(s), col))

    def state(rows):
        return pl.BlockSpec((1, HEADS, cpb, rows, 128), lambda b, s: (b, 0, blk(s), 0, 0))

    ins = [seq(D // 2, 0), seq(D // 2, 1), seq(D, 1), seq(D, 2), seq(128, 3 * D // 128),
           pl.BlockSpec((1, D), lambda b, s: (0, 0)), pl.BlockSpec((1, 128), lambda b, s: (0, 0))]
    return ins, [state(M_DV), state(1), state(1)], seq


def mlstm_cell_fwd(qk3, proj3, gain, gbias, name):
    Bl, S, _ = qk3.shape
    L = M_CHUNK
    SB = min(M_SLAB, S)
    cpb, nc = SB // L, S // L
    scale = M_DQK ** -0.5

    def body(q_ref, k_ref, v_ref, o_ref, g_ref, gain_ref, gb_ref, y_ref, cst_ref, nst_ref, mst_ref, C_s, n_s, m_s):
        @pl.when(pl.program_id(1) == 0)
        def _():
            C_s[...] = jnp.zeros_like(C_s)
            n_s[...] = jnp.zeros_like(n_s)
            m_s[...] = jnp.zeros_like(m_s)
        lane = lax.broadcasted_iota(jnp.int32, (L, 128), 1)
        eye, low, upp = _cell_consts(L)

        def step(c, carry):
            off = pl.multiple_of(c * L, L)
            G = g_ref[0, pl.ds(off, L), :] + gb_ref[...]
            for h in range(HEADS):
                C, n, mb = C_s[h], n_s[h], m_s[h]
                cst_ref[0, h, c] = C
                nst_ref[0, h, c] = n
                mst_ref[0, h, c] = mb
                m = mb[:, 0:1]
                _, qc, kc, vc, gi, gf = _load_chunk(q_ref, k_ref, v_ref, G, off, L, h, lane)
                gi_row, bcum, bcum_row, b_last = _gate_rows(gi, gf, eye, low, upp)
                r = _chunk_out(qc * scale, kc, vc, gi_row, bcum, bcum_row, low, C, n, m)
                hc = r["hc"]
                hn = hc * lax.rsqrt(jnp.mean(hc * hc, axis=-1, keepdims=True) + RMS_EPS)
                cols = pl.ds(M_DV * h, M_DV)
                oc = o_ref[0, pl.ds(off, L), cols]
                y_ref[0, pl.ds(off, L), cols] = (_sigmoid(oc) * hn * gain_ref[:, cols]).astype(BF16)
                C2, n2, m2, _, _, _ = _chunk_state(kc, vc, gi, bcum, b_last, C, n, m)
                C_s[h] = C2
                n_s[h] = n2
                m_s[h] = jnp.broadcast_to(m2, (1, 128))
            return carry

        lax.fori_loop(0, cpb, step, 0)

    ins, states, seq = _cell_specs(SB, cpb, lambda s: s)
    return pl.pallas_call(
        body, name=name, grid=(Bl, S // SB), in_specs=ins, out_specs=[seq(D, 0)] + states,
        out_shape=[jax.ShapeDtypeStruct((Bl, S, D), BF16),
                   jax.ShapeDtypeStruct((Bl, HEADS, nc, M_DV, 128), F32),
                   jax.ShapeDtypeStruct((Bl, HEADS, nc, 1, 128), F32),
                   jax.ShapeDtypeStruct((Bl, HEADS, nc, 1, 128), F32)],
        scratch_shapes=[pltpu.VMEM((HEADS, M_DV, 128), F32), pltpu.VMEM((HEADS, 1, 128), F32),
                        pltpu.VMEM((HEADS, 1, 128), F32)],
        compiler_params=_cp(2),
    )(qk3, qk3, proj3, proj3, proj3, gain, gbias)


def mlstm_cell_bwd(qk3, proj3, gain, gbias, dy3, states, name):
    Bl, S, _ = qk3.shape
    L = M_CHUNK
    SB = min(M_SLAB, S)
    cpb, nsb = SB // L, S // SB
    scale = M_DQK ** -0.5

    def body(q_ref, k_ref, v_ref, o_ref, g_ref, gain_ref, gb_ref, cst_ref, nst_ref, mst_ref, dy_ref,
             dq_ref, dk_ref, dv_ref, do_ref, dg_ref, dgain_ref, dgb_ref, dC_s, dn_s, dgain_s, dgb_s):
        s = pl.program_id(1)

        @pl.when(s == 0)
        def _():
            dC_s[...] = jnp.zeros_like(dC_s)
            dn_s[...] = jnp.zeros_like(dn_s)
            dgain_s[...] = jnp.zeros_like(dgain_s)
            dgb_s[...] = jnp.zeros_like(dgb_s)
        lane = lax.broadcasted_iota(jnp.int32, (L, 128), 1)
        rowi = lax.broadcasted_iota(jnp.int32, (L, 1), 0)
        eye, low, upp = _cell_consts(L)

        def bstep(t, carry):
            c = cpb - 1 - t
            off = pl.multiple_of(c * L, L)
            G = g_ref[0, pl.ds(off, L), :] + gb_ref[...]
            slab = jnp.zeros((L, 128), F32)
            dq_pair = dk_pair = None
            for h in range(HEADS):
                cols = pl.ds(M_DV * h, M_DV)
                gain_h = gain_ref[:, cols]
                C, n, m = cst_ref[0, h, c], nst_ref[0, h, c], mst_ref[0, h, c][:, 0:1]
                dC_n, dn_n = dC_s[h], dn_s[h]
                qmask, qc, kc, vc, gi, gf = _load_chunk(q_ref, k_ref, v_ref, G, off, L, h, lane)
                gi_row, bcum, bcum_row, b_last = _gate_rows(gi, gf, eye, low, upp)
                qs = qc * scale
                r = _chunk_out(qs, kc, vc, gi_row, bcum, bcum_row, low, C, n, m)
                _, _, _, wa, sp, sl = _chunk_state(kc, vc, gi, bcum, b_last, C, n, m)
                hc, den, dn, sc, dm, iw, qn = r["hc"], r["den"], r["dn"], r["sc"], r["dm"], r["iw"], r["qn"]
                qs_b, k_b, v_b = r["qs_b"], r["k_b"], r["v_b"]
                dy = dy_ref[0, pl.ds(off, L), cols].astype(F32)
                oc = o_ref[0, pl.ds(off, L), cols]
                sig_o = _sigmoid(oc)
                rr = lax.rsqrt(jnp.mean(hc * hc, axis=-1, keepdims=True) + RMS_EPS)
                hn = hc * rr
                dgain_s[:, cols] += _sum0(dy * sig_o * hn)
                do_ref[0, pl.ds(off, L), cols] = (
                    dy * hn * gain_h * sig_o * (1.0 - sig_o)).astype(BF16)
                dhn = dy * sig_o * gain_h
                dhc = rr * dhn - hc * (rr * rr * rr) * jnp.mean(dhn * hc, axis=-1, keepdims=True)
                dnum = dhc / dn
                gden = -_sum1(dhc * hc) / dn
                dden = jnp.where(jnp.abs(den) > r["floor"], gden * jnp.sign(den), 0.0)
                dnum_b = dnum.astype(BF16)
                dsc = _dot_nt(dnum_b, v_b) + dden
                dv = _dot_tn(sc.astype(BF16), dnum_b)
                diw = _sum1(dnum * r["qc"]) + dden * qn
                dqc_b = (iw * dnum).astype(BF16)
                wq = iw * dden
                dqs = _dot(dqc_b, C.astype(BF16)) + wq * n
                dC_out = _dot_tn(dqc_b, qs_b)
                dn_out = _sum0(wq * qs)
                dS_b = (dsc * dm).astype(BF16)
                gm = dsc * sc
                dqs = dqs + _dot(dS_b, k_b)
                dk = _dot_tn(dS_b, qs_b)
                dbc = _sum1(gm) + diw * iw
                colg = _sum0(gm)
                dC_p = sp * dC_n + dC_out
                dn_p = sp * dn_n + dn_out
                dcl_b = (sl * dC_n).astype(BF16)
                dn_loc = sl * dn_n
                dsp = _sum1(_sum0(dC_n * C)) + _sum1(dn_n * n)
                db_last = dsp * sp
                t1 = _dot(v_b, dcl_b) + dn_loc
                dwa = _sum1(t1 * kc)
                dv = dv + wa * _dot_nt(k_b, dcl_b)
                dk = dk + wa * t1
                da = dwa * wa
                db_last = db_last + _sum0(da)
                dbc = dbc - da + jnp.where(rowi == L - 1, db_last, 0.0)
                dbc_row = _sum0(jnp.where(eye, dbc, 0.0)) - colg
                dgi = da + _sum1(jnp.where(eye, colg, 0.0))
                dlf = _sum1(jnp.where(upp, dbc_row, 0.0))
                dgf = dlf * _sigmoid(-gf)
                dq = jnp.where(qmask, dqs * scale, 0.0)
                dk = jnp.where(qmask, dk, 0.0)
                slab = slab + jnp.where(lane == h, dgi, 0.0) + jnp.where(lane == h + HEADS, dgf, 0.0)
                dv_ref[0, pl.ds(off, L), cols] = dv.astype(BF16)
                dC_s[h] = dC_p
                dn_s[h] = dn_p
                if h % 2 == 0:
                    dq_pair, dk_pair = dq, dk
                else:
                    pair = pl.ds(128 * (h // 2), 128)
                    dq_ref[0, pl.ds(off, L), pair] = dq_pair + dq
                    dk_ref[0, pl.ds(off, L), pair] = dk_pair + dk
            dg_ref[0, pl.ds(off, L), :] = slab
            dgb_s[...] += _sum0(slab)
            return carry

        lax.fori_loop(0, cpb, bstep, 0)

        @pl.when(s == nsb - 1)
        def _():
            dgain_ref[0] = dgain_s[...]
            dgb_ref[0] = dgb_s[...]

    ins, states_specs, seq = _cell_specs(SB, cpb, lambda s: nsb - 1 - s)
    once = lambda width: pl.BlockSpec((1, 1, width), lambda b, s: (b, 0, 0))
    return pl.pallas_call(
        body, name=name, grid=(Bl, nsb), in_specs=ins + states_specs + [seq(D, 0)],
        out_specs=[seq(D // 2, 0), seq(D // 2, 0), seq(D, 0), seq(D, 0), seq(128, 0), once(D), once(128)],
        out_shape=[jax.ShapeDtypeStruct((Bl, S, D // 2), F32), jax.ShapeDtypeStruct((Bl, S, D // 2), F32),
                   jax.ShapeDtypeStruct((Bl, S, D), BF16), jax.ShapeDtypeStruct((Bl, S, D), BF16),
                   jax.ShapeDtypeStruct((Bl, S, 128), F32), jax.ShapeDtypeStruct((Bl, 1, D), F32),
                   jax.ShapeDtypeStruct((Bl, 1, 128), F32)],
        scratch_shapes=[pltpu.VMEM((HEADS, M_DV, 128), F32), pltpu.VMEM((HEADS, 1, 128), F32),
                        pltpu.VMEM((1, D), F32), pltpu.VMEM((1, 128), F32)],
        compiler_params=_cp(2),
    )(qk3, qk3, proj3, proj3, proj3, gain, gbias, *states, dy3)


def _attn_scores(q, kc, kp, n, row, col, scale):
    s_c = jnp.where(col <= row, _dot_nt(q, kc) * scale, NEG)
    s_p = jnp.where(jnp.logical_and(col >= row, n > 0), _dot_nt(q, kp) * scale, NEG)
    return s_c, s_p


def attn_fwd(proj, Bl, S, g, dil, name):
    Sd = S // dil
    nb = Sd // A_BLK
    scale = A_BLK ** -0.5
    pv = proj.reshape(Bl, Sd, dil * A_PROJ)
    ncol = A_PROJ // 128

    def body(q_ref, k_ref, v_ref, o_ref, l_ref):
        row = lax.broadcasted_iota(jnp.int32, (A_BLK, A_BLK), 0)
        col = lax.broadcasted_iota(jnp.int32, (A_BLK, A_BLK), 1)

        def step(n, carry):
            off = pl.multiple_of(n * A_BLK, A_BLK)
            offp = pl.multiple_of(jnp.maximum(n - 1, 0) * A_BLK, A_BLK)
            q = q_ref[0, pl.ds(off, A_BLK), :]
            s_c, s_p = _attn_scores(q, k_ref[0, pl.ds(off, A_BLK), :], k_ref[0, pl.ds(offp, A_BLK), :],
                                    n, row, col, scale)
            m = jnp.maximum(jnp.max(s_c, axis=1, keepdims=True), jnp.max(s_p, axis=1, keepdims=True))
            p_c = jnp.exp(s_c - m)
            p_p = jnp.exp(s_p - m)
            den = _sum1(p_c) + _sum1(p_p)
            o = _dot(p_c.astype(BF16), v_ref[0, pl.ds(off, A_BLK), :]) + _dot(
                p_p.astype(BF16), v_ref[0, pl.ds(offp, A_BLK), :])
            o_ref[0, pl.ds(off, A_BLK), :] = o / den
            l_ref[0, pl.ds(off, A_BLK), :] = jnp.broadcast_to(m + jnp.log(den), (A_BLK, 128))
            return carry

        lax.fori_loop(0, nb, step, 0)

    def spec(j):
        return pl.BlockSpec((1, Sd, 128), lambda b, r, h: (b, 0, r * ncol + g * 24 + j * HEADS + h))

    ospec = pl.BlockSpec((1, Sd, 128), lambda b, r, h: (b, 0, r * HEADS + h))
    o, lse = pl.pallas_call(
        body, name=name, grid=(Bl, dil, HEADS),
        in_specs=[spec(0), spec(1), spec(2)], out_specs=[ospec, ospec],
        out_shape=[jax.ShapeDtypeStruct((Bl, Sd, dil * D), F32)] * 2, compiler_params=_cp(3),
    )(pv, pv, pv)
    return o.reshape(Bl * S, D), lse.reshape(Bl * S, D)


def attn_merge(os_, lses, name):
    T = os_[0].shape[0]
    tm = _tile(T, 512)
    ng = len(os_)

    def body(*refs):
        o_refs, l_refs = refs[:ng], refs[ng:2 * ng]
        ob_ref, of_ref, lt_ref = refs[2 * ng:]
        ls = [r[...] for r in l_refs]
        m = functools.reduce(jnp.maximum, ls)
        ws = [jnp.exp(l - m) for l in ls]
        den = functools.reduce(lambda a, b: a + b, ws)
        o = functools.reduce(lambda a, b: a + b, [w * r[...] for w, r in zip(ws, o_refs)]) / den
        of_ref[...] = o
        ob_ref[...] = o.astype(BF16)
        lt_ref[...] = m + jnp.log(den)

    spec = pl.BlockSpec((tm, D), lambda i: (i, 0))
    return pl.pallas_call(
        body, name=name, grid=(T // tm,), in_specs=[spec] * (2 * ng), out_specs=[spec] * 3,
        out_shape=[jax.ShapeDtypeStruct((T, D), BF16), jax.ShapeDtypeStruct((T, D), F32),
                   jax.ShapeDtypeStruct((T, D), F32)],
        compiler_params=_cp(1),
    )(*os_, *lses)


def attn_bwd(proj, do, o, lse, Bl, S, g, dil, name):
    Sd = S // dil
    nb = Sd // A_BLK
    scale = A_BLK ** -0.5
    pv = proj.reshape(Bl, Sd, dil * A_PROJ)
    ncol = A_PROJ // 128
    dov, ov, lv = (t.reshape(Bl, Sd, dil * D) for t in (do, o, lse))

    def body(q_ref, k_ref, v_ref, do_ref, o_ref, l_ref, dq_ref, dk_ref, dv_ref, dk_s, dv_s):
        row = lax.broadcasted_iota(jnp.int32, (A_BLK, A_BLK), 0)
        col = lax.broadcasted_iota(jnp.int32, (A_BLK, A_BLK), 1)
        dk_s[...] = jnp.zeros_like(dk_s)
        dv_s[...] = jnp.zeros_like(dv_s)

        def step(n, carry):
            off = pl.multiple_of(n * A_BLK, A_BLK)
            offp = pl.multiple_of(jnp.maximum(n - 1, 0) * A_BLK, A_BLK)
            q = q_ref[0, pl.ds(off, A_BLK), :]
            kc, kp = k_ref[0, pl.ds(off, A_BLK), :], k_ref[0, pl.ds(offp, A_BLK), :]
            vc, vp = v_ref[0, pl.ds(off, A_BLK), :], v_ref[0, pl.ds(offp, A_BLK), :]
            do_b = do_ref[0, pl.ds(off, A_BLK), :]
            delta = _sum1(do_b.astype(F32) * o_ref[0, pl.ds(off, A_BLK), :])
            lt = l_ref[0, pl.ds(off, A_BLK), :][:, 0:1]
            s_c, s_p = _attn_scores(q, kc, kp, n, row, col, scale)
            p_c = jnp.exp(s_c - lt)
            p_p = jnp.exp(s_p - lt)
            ds_c = (p_c * (_dot_nt(do_b, vc) - delta) * scale).astype(BF16)
            ds_p = (p_p * (_dot_nt(do_b, vp) - delta) * scale).astype(BF16)
            dq_ref[0, pl.ds(off, A_BLK), :] = (_dot(ds_c, kc) + _dot(ds_p, kp)).astype(BF16)
            dk_s[pl.ds(off, A_BLK), :] += _dot_tn(ds_c, q)
            dk_s[pl.ds(offp, A_BLK), :] += _dot_tn(ds_p, q)
            dv_s[pl.ds(off, A_BLK), :] += _dot_tn(p_c.astype(BF16), do_b)
            dv_s[pl.ds(offp, A_BLK), :] += _dot_tn(p_p.astype(BF16), do_b)
            return carry

        lax.fori_loop(0, nb, step, 0)
        dk_ref[0] = dk_s[...].astype(BF16)
        dv_ref[0] = dv_s[...].astype(BF16)

    def spec(j):
        return pl.BlockSpec((1, Sd, 128), lambda b, r, h: (b, 0, r * ncol + g * 24 + j * HEADS + h))

    ospec = pl.BlockSpec((1, Sd, 128), lambda b, r, h: (b, 0, r * HEADS + h))
    outs = pl.pallas_call(
        body, name=name, grid=(Bl, dil, HEADS),
        in_specs=[spec(0), spec(1), spec(2), ospec, ospec, ospec], out_specs=[ospec] * 3,
        out_shape=[jax.ShapeDtypeStruct((Bl, Sd, dil * D), BF16)] * 3,
        scratch_shapes=[pltpu.VMEM((Sd, 128), F32), pltpu.VMEM((Sd, 128), F32)],
        compiler_params=_cp(3),
    )(pv, pv, pv, dov, ov, lv)
    return [t.reshape(Bl * S, D) for t in outs]


def ffn_fwd(x, mod3, w_in, w_out, lng, lnb, tag):
    a, g, u, h = ffn_in(x, mod3, w_in, tag + "_in", tf=1408)
    out, xn = proj_post(a, w_out, x, mod3, lng, lnb, 0.5, tag + "_out", tk=D_FF)
    return xn, (x, out, g, u, h, a)


def ffn_bwd(dxn, saved, mod3, w_in, w_out, lng, tag):
    x, out, g, u, h, a = saved
    dxres, dout, dgu, dlg, dlb, dgate = post_bwd(dxn, x, out, mod3, lng, w_out, 0.5, tag + "_outb", tk=D_FF,
                                                 tm=256, gu=(g, u))
    dx, dsh, dsc = modmm_bwd(dgu, w_in, x, mod3, dxres, tag + "_inb", tn=2 * D_FF, tm=256)
    dw_in = mm_tn(h, dgu, tag + "_dwin", tk=1024, tn=1408)
    dw_out = mm_tn(a, dout, tag + "_dwout", tk=1408, tn=1024)
    return dx, dw_in, dw_out, dlg, dlb, jnp.concatenate([dsh, dsc, dgate], axis=1)


def mlstm_fwd(x, mod3, w_in, w_out, conv_w, gain, gbias, lng, lnb, Bl, S):
    proj, h = modmm(x, mod3, w_in, F32, "ml_in", tn=M_PROJ_PAD // 5)
    proj3 = proj.reshape(Bl, S, M_PROJ_PAD)
    qk3 = conv_silu(proj3, conv_w, "ml_conv")
    y3, *states = mlstm_cell_fwd(qk3, proj3, gain, gbias, "ml_cell")
    y = y3.reshape(Bl * S, D)
    out, xn = proj_post(y, w_out, x, mod3, lng, lnb, 1.0, "ml_out", tk=D)
    return xn, (x, out, h, proj3, qk3, y, states)


def mlstm_bwd(dxn, saved, mod3, w_in, w_out, conv_w, gain, gbias, lng, Bl, S):
    x, out, h, proj3, qk3, y, states = saved
    dxres, dout, dy, dlg, dlb, dgate = post_bwd(dxn, x, out, mod3, lng, w_out, 1.0, "ml_outb", tk=D)
    dq, dk, dv, do, dg, dgain, dgb = mlstm_cell_bwd(qk3, proj3, gain, gbias, dy.reshape(Bl, S, D), states,
                                                    "ml_cellb")
    dqk, dconv = conv_silu_bwd(proj3, conv_w, dq, dk, "ml_convb")
    dproj = jnp.concatenate([dqk, dv, do, dg.astype(BF16)], axis=2).reshape(Bl * S, M_PROJ_PAD)
    dx, dsh, dsc = modmm_bwd(dproj, w_in, x, mod3, dxres, "ml_inb", tn=M_PROJ_PAD // 5)
    dw_in = mm_tn(h, dproj, "ml_dwin", tk=1024, tn=M_PROJ_PAD // 5)
    dw_out = mm_tn(y, dout, "ml_dwout", tk=1024, tn=1024)
    small = (jnp.sum(dconv, axis=0), jnp.sum(dgain, axis=0), jnp.sum(dgb, axis=0)[:, :2 * HEADS])
    return dx, dw_in[:, :M_PROJ], dw_out, dlg, dlb, jnp.concatenate([dsh, dsc, dgate], axis=1), small


def attn_mixer_fwd(x, mod3, w_in, w_out, lng, lnb, Bl, S):
    proj, h = modmm(x, mod3, w_in, BF16, "at_in", tn=2304)
    os_, lses = [], []
    for g, (_, dil) in enumerate(DIL_GROUPS):
        o_g, l_g = attn_fwd(proj, Bl, S, g, dil, "at_core%d" % g)
        os_.append(o_g)
        lses.append(l_g)
    ob, of, lt = attn_merge(os_, lses, "at_merge")
    out, xn = proj_post(ob, w_out, x, mod3, lng, lnb, 1.0, "at_out", tk=D)
    return xn, (x, out, h, proj, ob, of, lt)


def attn_mixer_bwd(dxn, saved, mod3, w_in, w_out, lng, Bl, S):
    x, out, h, proj, ob, of, lt = saved
    dxres, dout, do, dlg, dlb, dgate = post_bwd(dxn, x, out, mod3, lng, w_out, 1.0, "at_outb", tk=D)
    parts = []
    for g, (_, dil) in enumerate(DIL_GROUPS):
        parts += attn_bwd(proj, do, of, lt, Bl, S, g, dil, "at_coreb%d" % g)
    dproj = jnp.concatenate(parts, axis=1)
    dx, dsh, dsc = modmm_bwd(dproj, w_in, x, mod3, dxres, "at_inb", tn=2304)
    dw_in = mm_tn(h, dproj, "at_dwin", tk=1024, tn=2304)
    dw_out = mm_tn(ob, dout, "at_dwout", tk=1024, tn=1024)
    return dx, dw_in, dw_out, dlg, dlb, jnp.concatenate([dsh, dsc, dgate], axis=1)


BIG = ("ffn_w_in", "ffn_w_out", "mlstm_w_in", "mlstm_w_out", "attn_w_in", "attn_w_out")
SHARD_AXIS = {"ffn_w_in": 3, "ffn_w_out": 2, "mlstm_w_in": 2, "mlstm_w_out": 1, "attn_w_in": 2, "attn_w_out": 1}


def _pack_rows(shards):
    rows = [math.prod(s.shape) // PACK_COLS for s in shards]
    padded = [-(-r // 16) * 16 for r in rows]
    total = -(-sum(padded) // 512) * 512
    return rows, padded, total


def _pad_rows(a, n, axis):
    pad = [(0, 0)] * a.ndim
    pad[axis] = (0, n - a.shape[axis])
    return jnp.pad(a, pad)


def _unstack(stacked, axis):
    full = jnp.moveaxis(stacked, 0, axis)
    shp = list(full.shape)
    shp[axis:axis + 2] = [shp[axis] * shp[axis + 1]]
    return full.reshape(shp)


def _restack(full, axis):
    shp = list(full.shape)
    shp[axis:axis + 1] = [N_DEV, shp[axis] // N_DEV]
    return jnp.moveaxis(full.reshape(shp), axis, 0)


def kernel(x, c, ada_w, ada_b, ln_g, ln_b, ffn_w_in, ffn_w_out, mlstm_w_in, mlstm_gate_bias, mlstm_conv_w, mlstm_head_gain, mlstm_w_out, attn_w_in, attn_w_out, loss_target, m_ada_w, m_ada_b, m_ln_g, m_ln_b, m_ffn_w_in, m_ffn_w_out, m_mlstm_w_in, m_mlstm_gate_bias, m_mlstm_conv_w, m_mlstm_head_gain, m_mlstm_w_out, m_attn_w_in, m_attn_w_out, v_ada_w, v_ada_b, v_ln_g, v_ln_b, v_ffn_w_in, v_ffn_w_out, v_mlstm_w_in, v_mlstm_gate_bias, v_mlstm_conv_w, v_mlstm_head_gain, v_mlstm_w_out, v_attn_w_in, v_attn_w_out):
    Bl, S, _ = x.shape
    T = Bl * S
    Bg = Bl * N_DEV
    me = 4 * lax.axis_index("x") + 2 * lax.axis_index("y") + lax.axis_index("c")
    onehot = (jnp.arange(N_DEV) == me).astype(F32)
    weights = dict(ada_w=ada_w, ada_b=ada_b, ln_g=ln_g, ln_b=ln_b, ffn_w_in=ffn_w_in, ffn_w_out=ffn_w_out,
                   mlstm_w_in=mlstm_w_in, mlstm_gate_bias=mlstm_gate_bias, mlstm_conv_w=mlstm_conv_w,
                   mlstm_head_gain=mlstm_head_gain, mlstm_w_out=mlstm_w_out, attn_w_in=attn_w_in,
                   attn_w_out=attn_w_out)
    m_in = dict(ada_w=m_ada_w, ada_b=m_ada_b, ln_g=m_ln_g, ln_b=m_ln_b, ffn_w_in=m_ffn_w_in,
                ffn_w_out=m_ffn_w_out, mlstm_w_in=m_mlstm_w_in, mlstm_gate_bias=m_mlstm_gate_bias,
                mlstm_conv_w=m_mlstm_conv_w, mlstm_head_gain=m_mlstm_head_gain, mlstm_w_out=m_mlstm_w_out,
                attn_w_in=m_attn_w_in, attn_w_out=m_attn_w_out)
    v_in = dict(ada_w=v_ada_w, ada_b=v_ada_b, ln_g=v_ln_g, ln_b=v_ln_b, ffn_w_in=v_ffn_w_in,
                ffn_w_out=v_ffn_w_out, mlstm_w_in=v_mlstm_w_in, mlstm_gate_bias=v_mlstm_gate_bias,
                mlstm_conv_w=v_mlstm_conv_w, mlstm_head_gain=v_mlstm_head_gain, mlstm_w_out=v_mlstm_w_out,
                attn_w_in=v_attn_w_in, attn_w_out=v_attn_w_out)

    rows, prows, total = _pack_rows([weights[k] for k in BIG])
    pack = jnp.concatenate([_pad_rows(weights[k].astype(BF16).reshape(-1, PACK_COLS), pr, 0)
                            for k, pr in zip(BIG, prows)]
                           + [jnp.zeros((total - sum(prows), PACK_COLS), BF16)], axis=0)
    small = jnp.concatenate([c.reshape(-1), ln_g.reshape(-1), ln_b.reshape(-1), mlstm_conv_w.reshape(-1)])
    n_small = small.shape[0]
    small = jnp.pad(small, (0, -n_small % (8 * PACK_COLS))).reshape(-1, PACK_COLS)
    pack_all, small_all = all_gather([pack, small], "ag_params")
    full = {}
    r0 = 0
    for k, r, pr in zip(BIG, rows, prows):
        full[k] = _unstack(pack_all[:, r0:r0 + r].reshape((N_DEV,) + weights[k].shape), SHARD_AXIS[k])
        r0 += pr
    small_flat = small_all.reshape(N_DEV, -1)
    o0 = 0
    c_all = small_flat[:, o0:o0 + c.size].reshape(Bg, D)
    o0 += c.size
    lng_full = _unstack(small_flat[:, o0:o0 + ln_g.size].reshape((N_DEV,) + ln_g.shape), 2)
    o0 += ln_g.size
    lnb_full = _unstack(small_flat[:, o0:o0 + ln_b.size].reshape((N_DEV,) + ln_b.shape), 2)
    o0 += ln_b.size
    conv_full = _unstack(small_flat[:, o0:o0 + mlstm_conv_w.size].reshape((N_DEV,) + mlstm_conv_w.shape), 2)[0]
    mw_in = jnp.pad(full["mlstm_w_in"][0], ((0, 0), (0, M_PROJ_PAD - M_PROJ)))
    gbias = jnp.pad(mlstm_gate_bias, ((0, 0), (0, 128 - 2 * HEADS)))

    ncols = ada_w.shape[2]
    ada_b_cols = lax.dynamic_slice_in_dim(ada_b, me * ncols, ncols, axis=1).reshape(DEPTH, 1, ncols)
    mod_cols = ada_fwd(c_all, ada_w, ada_b_cols, "ada_fwd")
    (mod_g,) = all_gather([mod_cols.reshape(DEPTH * Bg, ncols)], "ag_mod")
    mod_full = _unstack(mod_g.reshape(N_DEV, DEPTH, Bg, ncols), 2)
    mod_mine = lax.dynamic_slice_in_dim(mod_full, me * Bl, Bl, axis=1).reshape(DEPTH, Bl, 3, 3, D)

    xt = x.reshape(T, D)
    saved = []
    for layer in range(DEPTH):
        def lnp(s, layer=layer):
            return lng_full[layer, s].reshape(1, D), lnb_full[layer, s].reshape(1, D)
        md = mod_mine[layer]
        xt, sv0 = ffn_fwd(xt, md[:, 0], full["ffn_w_in"][layer, 0], full["ffn_w_out"][layer, 0], *lnp(0),
                          "f%da" % layer)
        if layer % 2 == 0:
            xt, sv1 = mlstm_fwd(xt, md[:, 1], mw_in, full["mlstm_w_out"][0], conv_full, mlstm_head_gain, gbias,
                                *lnp(1), Bl, S)
        else:
            xt, sv1 = attn_mixer_fwd(xt, md[:, 1], full["attn_w_in"][0], full["attn_w_out"][0], *lnp(1), Bl, S)
        xt, sv2 = ffn_fwd(xt, md[:, 2], full["ffn_w_in"][layer, 1], full["ffn_w_out"][layer, 1], *lnp(2),
                          "f%db" % layer)
        saved.append((sv0, sv1, sv2))

    dxt, lsum = loss_head(xt, loss_target.reshape(T, D), "loss")
    loss = lax.psum(lsum[0, 0], MESH_AXES)

    gfull = {}
    dmod, dlg_all, dlb_all = [None] * DEPTH, [None] * DEPTH, [None] * DEPTH
    dw_fin, dw_fout = [[None, None] for _ in range(DEPTH)], [[None, None] for _ in range(DEPTH)]
    ml_small = None
    for layer in reversed(range(DEPTH)):
        md = mod_mine[layer]
        sv0, sv1, sv2 = saved[layer]
        dxt, dwi, dwo, dlg2, dlb2, dm2 = ffn_bwd(dxt, sv2, md[:, 2], full["ffn_w_in"][layer, 1],
                                                 full["ffn_w_out"][layer, 1],
                                                 lng_full[layer, 2].reshape(1, D), "f%db" % layer)
        dw_fin[layer][1], dw_fout[layer][1] = dwi, dwo
        lg1 = lng_full[layer, 1].reshape(1, D)
        if layer % 2 == 0:
            dxt, gwi, gwo, dlg1, dlb1, dm1, ml_small = mlstm_bwd(
                dxt, sv1, md[:, 1], mw_in, full["mlstm_w_out"][0], conv_full, mlstm_head_gain, gbias, lg1, Bl, S)
            gfull["mlstm_w_in"], gfull["mlstm_w_out"] = gwi[None], gwo[None]
        else:
            dxt, gwi, gwo, dlg1, dlb1, dm1 = attn_mixer_bwd(
                dxt, sv1, md[:, 1], full["attn_w_in"][0], full["attn_w_out"][0], lg1, Bl, S)
            gfull["attn_w_in"], gfull["attn_w_out"] = gwi[None], gwo[None]
        dxt, dwi, dwo, dlg0, dlb0, dm0 = ffn_bwd(dxt, sv0, md[:, 0], full["ffn_w_in"][layer, 0],
                                                 full["ffn_w_out"][layer, 0],
                                                 lng_full[layer, 0].reshape(1, D), "f%da" % layer)
        dw_fin[layer][0], dw_fout[layer][0] = dwi, dwo
        dmod[layer] = jnp.stack([dm0, dm1, dm2], axis=1).reshape(Bl, 9 * D)
        dlg_all[layer] = jnp.concatenate([dlg0, dlg1, dlg2], axis=0)
        dlb_all[layer] = jnp.concatenate([dlb0, dlb1, dlb2], axis=0)
    grad_x = dxt.reshape(Bl, S, D)
    gfull["ffn_w_in"] = jnp.stack([jnp.stack(r) for r in dw_fin])
    gfull["ffn_w_out"] = jnp.stack([jnp.stack(r) for r in dw_fout])

    gpack = jnp.concatenate(
        [_pad_rows(_restack(gfull[k], SHARD_AXIS[k]).reshape(N_DEV, -1, PACK_COLS), pr, 1)
         for k, pr in zip(BIG, prows)]
        + [jnp.zeros((N_DEV, total - sum(prows), PACK_COLS), F32)], axis=1)
    own = lax.dynamic_index_in_dim(gpack, me, axis=0, keepdims=False)
    recv = exchange_shards(gpack.astype(BF16), "rs_grads")
    gsum = shard_sum(own, recv, onehot, "rs_sum")
    grads = {}
    r0 = 0
    for k, r, pr in zip(BIG, rows, prows):
        grads[k] = gsum[r0:r0 + r].reshape(weights[k].shape)
        r0 += pr

    dconv, dgain, dgbias = ml_small
    parts = [jnp.stack(dmod).reshape(-1), dgbias.reshape(-1), dgain.reshape(-1),
             jnp.stack(dlg_all).reshape(-1), jnp.stack(dlb_all).reshape(-1), dconv.reshape(-1)]
    sizes = [p.shape[0] for p in parts]
    flat = jnp.concatenate(parts)
    flat = jnp.pad(flat, (0, -flat.shape[0] % (8 * PACK_COLS))).reshape(-1, PACK_COLS)
    (sm_all,) = all_gather([flat], "ag_small")
    sm_sum = sum_leading(sm_all, "small_sum").reshape(-1)
    dmod_all = sm_all.reshape(N_DEV, -1)[:, :sizes[0]].reshape(N_DEV, DEPTH, Bl, 9 * D)
    dmod_all = jnp.moveaxis(dmod_all, 0, 1).reshape(DEPTH, Bg, 9 * D)
    o0 = sizes[0]
    grads["mlstm_gate_bias"] = sm_sum[o0:o0 + sizes[1]].reshape(mlstm_gate_bias.shape)
    o0 += sizes[1]
    grads["mlstm_head_gain"] = sm_sum[o0:o0 + sizes[2]].reshape(mlstm_head_gain.shape)
    o0 += sizes[2]
    nl = ln_g.shape[2]
    g_lng = sm_sum[o0:o0 + sizes[3]].reshape(DEPTH, 3, D)
    o0 += sizes[3]
    g_lnb = sm_sum[o0:o0 + sizes[4]].reshape(DEPTH, 3, D)
    o0 += sizes[4]
    g_conv = sm_sum[o0:o0 + sizes[5]].reshape(1, 4, D)
    grads["ln_g"] = lax.dynamic_slice_in_dim(g_lng, me * nl, nl, axis=2)
    grads["ln_b"] = lax.dynamic_slice_in_dim(g_lnb, me * nl, nl, axis=2)
    grads["mlstm_conv_w"] = lax.dynamic_slice_in_dim(g_conv, me * nl, nl, axis=2)
    dmod_cols = lax.dynamic_slice_in_dim(dmod_all, me * ncols, ncols, axis=2)
    gw, gb = ada_bwd(c_all.T, dmod_cols, dmod_all, "ada_bwd")
    grads["ada_w"] = gw
    grads["ada_b"] = gb.reshape(ada_b.shape)

    names = ["ada_w", "ada_b", "ln_g", "ln_b", "ffn_w_in", "ffn_w_out", "mlstm_w_in", "mlstm_gate_bias",
             "mlstm_conv_w", "mlstm_head_gain", "mlstm_w_out", "attn_w_in", "attn_w_out"]
    deltas, new_m, new_v = [], [], []
    for k in names:
        w = weights[k]
        shp2 = (math.prod(w.shape[:-1]), w.shape[-1])
        d_, m_, v_ = adamw(w.reshape(shp2), grads[k].reshape(shp2), m_in[k].reshape(shp2), v_in[k].reshape(shp2),
                           "adamw_" + k)
        deltas.append(d_.reshape(w.shape))
        new_m.append(m_.reshape(w.shape))
        new_v.append(v_.reshape(w.shape))
    return (loss, grad_x, *[grads[k] for k in names], *deltas, *new_m, *new_v)
```

```python
import functools
import math

import jax
import jax.numpy as jnp
from jax import lax
from jax.experimental import pallas as pl
from jax.experimental.pallas import tpu as pltpu

F32 = jnp.float32
BF16 = jnp.bfloat16

N_DEV = 8
MESH_AXES = ("x", "y", "c")
D = 1024
DEPTH = 2
D_FF = 2816
HEADS = 8
M_DQK = 64
M_DV = 128
M_CHUNK = 64
M_SLAB = 512
M_PROJ = 3088
M_PROJ_PAD = 3200
A_PROJ = 9216
DIL_GROUPS = ((128, 1), (512, 4), (2048, 16))
A_BLK = 128
ALPHA = (2 * DEPTH) ** 0.25
LN_EPS = 1e-5
RMS_EPS = 1e-6
ADAM_LR = 0.001
ADAM_B1 = 0.9
ADAM_B2 = 0.999
ADAM_EPS = 1e-08
ADAM_WD = 0.01
ADAM_STEP = 10
NEG = -1e30
V7X_VMEM_LIMIT = 56 * 1024 * 1024
PACK_COLS = 1024
MESH_ID = pl.DeviceIdType.MESH
ANY_SPEC = pl.BlockSpec(memory_space=pl.ANY)


def _cp(n_axes):
    return pltpu.CompilerParams(dimension_semantics=("arbitrary",) * n_axes,
                                vmem_limit_bytes=V7X_VMEM_LIMIT)


def _dot(a, b):
    return jnp.dot(a, b, preferred_element_type=F32)


def _dot_nt(a, b):
    return lax.dot_general(a, b, (((1,), (1,)), ((), ())), preferred_element_type=F32)


def _dot_tn(a, b):
    return lax.dot_general(a, b, (((0,), (0,)), ((), ())), preferred_element_type=F32)


def _sum0(a):
    return jnp.sum(a, axis=0, keepdims=True)


def _sum1(a):
    return jnp.sum(a, axis=1, keepdims=True)


def _round(a):
    return a.astype(BF16).astype(F32)


def _sigmoid(a):
    return 1.0 / (1.0 + jnp.exp(-a))


def _tile(n, pref):
    t = min(n, pref)
    while n % t:
        t //= 2
    return t


def all_gather(arrs, name):
    n = len(arrs)

    def body(*refs):
        gather = Gather(refs[:n], refs[n:2 * n], *refs[2 * n:])
        gather.start()
        gather.finish()

    return pl.pallas_call(
        body, name=name, out_shape=Gather.out_shape(arrs),
        in_specs=[ANY_SPEC] * n, out_specs=[ANY_SPEC] * n, scratch_shapes=Gather.scratch(n),
    )(*arrs)


class Gather:
    def __init__(self, ins, outs, send_sems, recv_sems, local_sems):
        x, y, c = lax.axis_index("x"), lax.axis_index("y"), lax.axis_index("c")
        me, sibling = (x, y, c), (x, y, 1 - c)
        chips = [(1 - x, y), (x, 1 - y), (1 - x, 1 - y)]

        def slot(a, p):
            return outs[a].at[4 * p[0] + 2 * p[1] + p[2]]

        def copy(a, k, block, to, src=None):
            return pltpu.make_async_remote_copy(
                src_ref=slot(a, block) if src is None else src, dst_ref=slot(a, block),
                send_sem=send_sems.at[7 * a + k], recv_sem=recv_sems.at[7 * a + k],
                device_id=to, device_id_type=MESH_ID)

        n = len(ins)
        self.mine = [pltpu.make_async_copy(ins[a], slot(a, me), local_sems.at[a]) for a in range(n)]
        self.first, self.over_ici, self.passed, self.from_sibling = [], [], [], []
        for a in range(n):
            self.first.append(copy(a, 0, me, sibling, src=ins[a]))
            self.from_sibling.append(copy(a, 0, sibling, me))
            for j, chip in enumerate(chips):
                self.first.append(copy(a, 1 + j, me, (*chip, c), src=ins[a]))
                self.over_ici.append(copy(a, 1 + j, (*chip, c), me))
                self.passed.append(copy(a, 4 + j, (*chip, c), sibling))
                self.from_sibling.append(copy(a, 4 + j, (*chip, 1 - c), me))

    @staticmethod
    def out_shape(arrs):
        return [jax.ShapeDtypeStruct((N_DEV,) + a.shape, a.dtype) for a in arrs]

    @staticmethod
    def scratch(n):
        return [pltpu.SemaphoreType.DMA((7 * n,)), pltpu.SemaphoreType.DMA((7 * n,)),
                pltpu.SemaphoreType.DMA((n,))]

    def start(self):
        for cp in self.mine + self.first:
            cp.start()

    def finish(self):
        for landed, onward in zip(self.over_ici, self.passed):
            landed.wait_recv()
            onward.start()
        for cp in self.from_sibling:
            cp.wait_recv()
        for cp in self.first + self.passed:
            cp.wait_send()
        for cp in self.mine:
            cp.wait()


class Exchange:
    def __init__(self, s_ref, r_ref, send_sems, recv_sems, local_sem):
        x, y, c = lax.axis_index("x"), lax.axis_index("y"), lax.axis_index("c")
        me = 4 * x + 2 * y + c
        self.own = pltpu.make_async_copy(s_ref.at[me], r_ref.at[me], local_sem)
        self.copies = []
        for k in range(1, N_DEV):
            px = 1 - x if (k >> 2) & 1 else x
            py = 1 - y if (k >> 1) & 1 else y
            pc = 1 - c if k & 1 else c
            self.copies.append(pltpu.make_async_remote_copy(
                src_ref=s_ref.at[4 * px + 2 * py + pc], dst_ref=r_ref.at[me],
                send_sem=send_sems.at[k - 1], recv_sem=recv_sems.at[k - 1],
                device_id=(px, py, pc), device_id_type=MESH_ID))

    @staticmethod
    def scratch():
        return [pltpu.SemaphoreType.DMA((N_DEV - 1,)), pltpu.SemaphoreType.DMA((N_DEV - 1,)),
                pltpu.SemaphoreType.DMA(())]

    def start(self):
        for cp in [self.own] + self.copies:
            cp.start()

    def finish(self):
        for cp in self.copies:
            cp.wait_send()
            cp.wait_recv()
        self.own.wait()


def exchange_shards(send, name):
    def body(s_ref, r_ref, *sems):
        exchange = Exchange(s_ref, r_ref, *sems)
        exchange.start()
        exchange.finish()

    return pl.pallas_call(
        body, name=name, out_shape=jax.ShapeDtypeStruct(send.shape, send.dtype),
        in_specs=[ANY_SPEC], out_specs=ANY_SPEC, scratch_shapes=Exchange.scratch(),
    )(send)


def shard_sum(own, recv, onehot, name):
    R, C = own.shape
    tr = _tile(R, 512)

    def body(oh_ref, own_ref, recv_ref, o_ref):
        acc = None
        for j in range(N_DEV):
            term = jnp.where(oh_ref[j] > 0.5, own_ref[...], recv_ref[j].astype(F32))
            acc = term if acc is None else acc + term
        o_ref[...] = acc

    return pl.pallas_call(
        body, name=name, grid=(R // tr,),
        in_specs=[pl.BlockSpec(memory_space=pltpu.SMEM),
                  pl.BlockSpec((tr, C), lambda i: (i, 0)),
                  pl.BlockSpec((N_DEV, tr, C), lambda i: (0, i, 0))],
        out_specs=pl.BlockSpec((tr, C), lambda i: (i, 0)),
        out_shape=jax.ShapeDtypeStruct((R, C), F32), compiler_params=_cp(1),
    )(onehot, own, recv)


def sum_leading(a, name):
    _, R, C = a.shape

    def body(a_ref, o_ref):
        acc = a_ref[0]
        for j in range(1, N_DEV):
            acc = acc + a_ref[j]
        o_ref[...] = acc

    return pl.pallas_call(body, name=name, out_shape=jax.ShapeDtypeStruct((R, C), F32),
                          compiler_params=_cp(0))(a)


def modmm(x, mod3, w, out_dtype, name, tn):
    T, Dm = x.shape
    N = w.shape[1]
    Bl = mod3.shape[0]
    tm = _tile(T // Bl, 1024)
    tpb = T // Bl // tm

    def body(x_ref, mod_ref, w_ref, o_ref, h_ref, hs):
        @pl.when(pl.program_id(1) == 0)
        def _():
            m = mod_ref[0]
            hs[...] = (x_ref[...] * (1.0 + m[1:2, :]) + m[0:1, :]).astype(BF16)
            h_ref[...] = hs[...]
        o_ref[...] = _dot(hs[...], w_ref[...]).astype(o_ref.dtype)

    return pl.pallas_call(
        body, name=name, grid=(T // tm, N // tn),
        in_specs=[pl.BlockSpec((tm, Dm), lambda i, j: (i, 0)),
                  pl.BlockSpec((1, 3, Dm), lambda i, j: (i // tpb, 0, 0)),
                  pl.BlockSpec((Dm, tn), lambda i, j: (0, j))],
        out_specs=[pl.BlockSpec((tm, tn), lambda i, j: (i, j)),
                   pl.BlockSpec((tm, Dm), lambda i, j: (i, 0))],
        out_shape=[jax.ShapeDtypeStruct((T, N), out_dtype), jax.ShapeDtypeStruct((T, Dm), BF16)],
        scratch_shapes=[pltpu.VMEM((tm, Dm), BF16)], compiler_params=_cp(2),
    )(x, mod3, w)


def modmm_bwd(dp, w, x, mod3, dxres, name, tn, tm=512):
    T, Dm = x.shape
    N = w.shape[1]
    Bl = mod3.shape[0]
    tm = _tile(T // Bl, tm)
    tpb = T // Bl // tm
    nj = N // tn

    def body(dp_ref, w_ref, x_ref, mod_ref, dxr_ref, dx_ref, dsh_ref, dsc_ref, acc):
        i, j = pl.program_id(0), pl.program_id(1)

        @pl.when(j == 0)
        def _():
            acc[...] = jnp.zeros_like(acc)
        acc[...] += _dot_nt(dp_ref[...], w_ref[...])

        @pl.when(j == nj - 1)
        def _():
            dh = acc[...]
            xx = x_ref[...]
            dx_ref[...] = dxr_ref[...] + dh * (1.0 + mod_ref[0][1:2, :])

            @pl.when(i % tpb == 0)
            def _():
                dsh_ref[...] = jnp.zeros_like(dsh_ref)
                dsc_ref[...] = jnp.zeros_like(dsc_ref)
            dsh_ref[0] += _sum0(dh)
            dsc_ref[0] += _sum0(dh * xx)

    return pl.pallas_call(
        body, name=name, grid=(T // tm, nj),
        in_specs=[pl.BlockSpec((tm, tn), lambda i, j: (i, j)),
                  pl.BlockSpec((Dm, tn), lambda i, j: (0, j)),
                  pl.BlockSpec((tm, Dm), lambda i, j: (i, 0)),
                  pl.BlockSpec((1, 3, Dm), lambda i, j: (i // tpb, 0, 0)),
                  pl.BlockSpec((tm, Dm), lambda i, j: (i, 0))],
        out_specs=[pl.BlockSpec((tm, Dm), lambda i, j: (i, 0)),
                   pl.BlockSpec((1, 1, Dm), lambda i, j: (i // tpb, 0, 0)),
                   pl.BlockSpec((1, 1, Dm), lambda i, j: (i // tpb, 0, 0))],
        out_shape=[jax.ShapeDtypeStruct((T, Dm), F32), jax.ShapeDtypeStruct((Bl, 1, Dm), F32),
                   jax.ShapeDtypeStruct((Bl, 1, Dm), F32)],
        scratch_shapes=[pltpu.VMEM((tm, Dm), F32)], compiler_params=_cp(2),
    )(dp, w, x, mod3, dxres)


def _ln_stats(z):
    mu = jnp.mean(z, axis=-1, keepdims=True)
    zc = z - mu
    var = jnp.mean(zc * zc, axis=-1, keepdims=True)
    rstd = lax.rsqrt(var + LN_EPS)
    return zc * rstd, rstd


def proj_post(a, w, x, mod3, lng, lnb, weight, name, tk):
    T, K = a.shape
    Dm = w.shape[1]
    Bl = mod3.shape[0]
    tm = _tile(T // Bl, 512)
    tpb = T // Bl // tm
    nk = K // tk

    def body(a_ref, w_ref, x_ref, mod_ref, g_ref, b_ref, out_ref, xn_ref, acc):
        k = pl.program_id(1)

        @pl.when(k == 0)
        def _():
            acc[...] = jnp.zeros_like(acc)
        acc[...] += _dot(a_ref[...], w_ref[...])

        @pl.when(k == nk - 1)
        def _():
            out = acc[...]
            out_ref[...] = out
            z = ALPHA * x_ref[...] + (weight * (1.0 + mod_ref[0][2:3, :])) * out
            xhat, _ = _ln_stats(z)
            xn_ref[...] = xhat * g_ref[...] + b_ref[...]

    return pl.pallas_call(
        body, name=name, grid=(T // tm, nk),
        in_specs=[pl.BlockSpec((tm, tk), lambda i, k: (i, k)),
                  pl.BlockSpec((tk, Dm), lambda i, k: (k, 0)),
                  pl.BlockSpec((tm, Dm), lambda i, k: (i, 0)),
                  pl.BlockSpec((1, 3, Dm), lambda i, k: (i // tpb, 0, 0)),
                  pl.BlockSpec((1, Dm), lambda i, k: (0, 0)),
                  pl.BlockSpec((1, Dm), lambda i, k: (0, 0))],
        out_specs=[pl.BlockSpec((tm, Dm), lambda i, k: (i, 0)),
                   pl.BlockSpec((tm, Dm), lambda i, k: (i, 0))],
        out_shape=[jax.ShapeDtypeStruct((T, Dm), F32), jax.ShapeDtypeStruct((T, Dm), F32)],
        scratch_shapes=[pltpu.VMEM((tm, Dm), F32)], compiler_params=_cp(2),
    )(a, w, x, mod3, lng, lnb)


def post_bwd(dxn, x, out, mod3, lng, w, weight, name, tk, tm=512, gu=None):
    T, Dm = x.shape
    K = w.shape[0]
    Bl = mod3.shape[0]
    tm = _tile(T // Bl, tm)
    tpb = T // Bl // tm
    fused = gu is not None
    assert not fused or tk == K

    def body(dxn_ref, x_ref, out_ref, mod_ref, g_ref, w_ref, *rest):
        if fused:
            gg_ref, uu_ref = rest[:2]
            rest = rest[2:]
        dxr_ref, dout_ref, da_ref, dg_ref, db_ref, dgate_ref, dout_s = rest
        i, k = pl.program_id(0), pl.program_id(1)

        @pl.when(k == 0)
        def _():
            out = out_ref[...]
            dxn = dxn_ref[...]
            coef = weight * (1.0 + mod_ref[0][2:3, :])
            xhat, rstd = _ln_stats(ALPHA * x_ref[...] + coef * out)
            dyh = dxn * g_ref[...]
            dz = rstd * (dyh - jnp.mean(dyh, axis=-1, keepdims=True)
                         - xhat * jnp.mean(dyh * xhat, axis=-1, keepdims=True))
            dxr_ref[...] = ALPHA * dz
            dout_s[...] = (coef * dz).astype(BF16)
            dout_ref[...] = dout_s[...]

            @pl.when(i == 0)
            def _():
                dg_ref[...] = jnp.zeros_like(dg_ref)
                db_ref[...] = jnp.zeros_like(db_ref)

            @pl.when(i % tpb == 0)
            def _():
                dgate_ref[...] = jnp.zeros_like(dgate_ref)
            dg_ref[...] += _sum0(dxn * xhat)
            db_ref[...] += _sum0(dxn)
            dgate_ref[0] += _sum0((weight * out) * dz)
        da = _dot_nt(dout_s[...], w_ref[...])
        if fused:
            gg = gg_ref[...].astype(F32)
            s = _sigmoid(gg)
            da_ref[:, :K] = (da * uu_ref[...].astype(F32) * (s * (1.0 + gg * (1.0 - s)))).astype(BF16)
            da_ref[:, K:] = (da * (gg * s)).astype(BF16)
        else:
            da_ref[...] = da.astype(BF16)

    row = pl.BlockSpec((tm, Dm), lambda i, k: (i, 0))
    vec = pl.BlockSpec((1, Dm), lambda i, k: (0, 0))
    wide = pl.BlockSpec((tm, tk), lambda i, k: (i, k))
    da_cols = 2 * K if fused else K
    return pl.pallas_call(
        body, name=name, grid=(T // tm, K // tk),
        in_specs=[row, row, row, pl.BlockSpec((1, 3, Dm), lambda i, k: (i // tpb, 0, 0)), vec,
                  pl.BlockSpec((tk, Dm), lambda i, k: (k, 0))] + ([wide, wide] if fused else []),
        out_specs=[row, row,
                   pl.BlockSpec((tm, da_cols), lambda i, k: (i, 0)) if fused else wide,
                   vec, vec, pl.BlockSpec((1, 1, Dm), lambda i, k: (i // tpb, 0, 0))],
        out_shape=[jax.ShapeDtypeStruct((T, Dm), F32), jax.ShapeDtypeStruct((T, Dm), BF16),
                   jax.ShapeDtypeStruct((T, da_cols), BF16), jax.ShapeDtypeStruct((1, Dm), F32),
                   jax.ShapeDtypeStruct((1, Dm), F32), jax.ShapeDtypeStruct((Bl, 1, Dm), F32)],
        scratch_shapes=[pltpu.VMEM((tm, Dm), BF16)], compiler_params=_cp(2),
    )(dxn, x, out, mod3, lng, w, *(gu if fused else ()))


def mm_tn(a, b, name, tk, tn):
    T, K = a.shape
    N = b.shape[1]
    tt = _tile(T, 1024)

    def body(a_ref, b_ref, o_ref):
        @pl.when(pl.program_id(2) == 0)
        def _():
            o_ref[...] = jnp.zeros_like(o_ref)
        o_ref[...] += _dot_tn(a_ref[...], b_ref[...])

    return pl.pallas_call(
        body, name=name, grid=(K // tk, N // tn, T // tt),
        in_specs=[pl.BlockSpec((tt, tk), lambda i, j, t: (t, i)),
                  pl.BlockSpec((tt, tn), lambda i, j, t: (t, j))],
        out_specs=pl.BlockSpec((tk, tn), lambda i, j, t: (i, j)),
        out_shape=jax.ShapeDtypeStruct((K, N), F32), compiler_params=_cp(3),
    )(a, b)


def ffn_in(x, mod3, w, name, tf):
    T, Dm = x.shape
    Fh = w.shape[1] // 2
    Bl = mod3.shape[0]
    tm = _tile(T // Bl, 512)
    tpb = T // Bl // tm
    nj = Fh // tf

    def body(x_ref, mod_ref, wg_ref, wu_ref, a_ref, g_ref, u_ref, h_ref):
        m = mod_ref[0]
        h = (x_ref[...] * (1.0 + m[1:2, :]) + m[0:1, :]).astype(BF16)
        h_ref[0] = h
        g = _dot(h, wg_ref[...])
        u = _dot(h, wu_ref[...])
        a_ref[...] = (g * _sigmoid(g) * u).astype(BF16)
        g_ref[...] = g.astype(BF16)
        u_ref[...] = u.astype(BF16)

    col = pl.BlockSpec((tm, tf), lambda j, i: (i, j))
    a, g, u, h = pl.pallas_call(
        body, name=name, grid=(nj, T // tm),
        in_specs=[pl.BlockSpec((tm, Dm), lambda j, i: (i, 0)),
                  pl.BlockSpec((1, 3, Dm), lambda j, i: (i // tpb, 0, 0)),
                  pl.BlockSpec((Dm, tf), lambda j, i: (0, j)),
                  pl.BlockSpec((Dm, tf), lambda j, i: (0, nj + j))],
        out_specs=[col, col, col, pl.BlockSpec((1, tm, Dm), lambda j, i: (j, i, 0))],
        out_shape=[jax.ShapeDtypeStruct((T, Fh), BF16)] * 3 + [jax.ShapeDtypeStruct((nj, T, Dm), BF16)],
        compiler_params=_cp(2),
    )(x, mod3, w, w)
    return a, g, u, h[0]


def loss_head(y, tgt, name):
    T, Dm = y.shape
    tm = _tile(T, 512)
    nt = T // tm

    def body(y_ref, t_ref, dy_ref, l_ref, acc):
        i = pl.program_id(0)

        @pl.when(i == 0)
        def _():
            acc[...] = jnp.zeros_like(acc)
        e = y_ref[...] - t_ref[...]
        dy_ref[...] = e * (1.0 / Dm)
        acc[...] += _sum0(e * e)

        @pl.when(i == nt - 1)
        def _():
            l_ref[...] = jnp.broadcast_to(_sum1(acc[...]) * (0.5 / Dm), l_ref.shape)

    return pl.pallas_call(
        body, name=name, grid=(nt,),
        in_specs=[pl.BlockSpec((tm, Dm), lambda i: (i, 0)), pl.BlockSpec((tm, Dm), lambda i: (i, 0))],
        out_specs=[pl.BlockSpec((tm, Dm), lambda i: (i, 0)), pl.BlockSpec((1, 128), lambda i: (0, 0))],
        out_shape=[jax.ShapeDtypeStruct((T, Dm), F32), jax.ShapeDtypeStruct((1, 128), F32)],
        scratch_shapes=[pltpu.VMEM((1, Dm), F32)], compiler_params=_cp(1),
    )(y, tgt)


def adamw(w, g, m, v, name):
    R, C = w.shape
    tr = _tile(R, 512) if R % 8 == 0 else R

    def body(w_ref, g_ref, m_ref, v_ref, d_ref, nm_ref, nv_ref):
        gg = g_ref[...]
        mm = ADAM_B1 * m_ref[...] + (1.0 - ADAM_B1) * gg
        vv = ADAM_B2 * v_ref[...] + (1.0 - ADAM_B2) * (gg * gg)
        m_hat = mm / (1.0 - ADAM_B1 ** ADAM_STEP)
        v_hat = vv / (1.0 - ADAM_B2 ** ADAM_STEP)
        d_ref[...] = -ADAM_LR * (m_hat / (jnp.sqrt(v_hat) + ADAM_EPS) + ADAM_WD * w_ref[...])
        nm_ref[...] = mm
        nv_ref[...] = vv

    spec = pl.BlockSpec((tr, C), lambda i: (i, 0))
    return pl.pallas_call(
        body, name=name, grid=(R // tr,), in_specs=[spec] * 4, out_specs=[spec] * 3,
        out_shape=[jax.ShapeDtypeStruct((R, C), F32)] * 3, compiler_params=_cp(1),
    )(w, g, m, v)


def ada_fwd(c_all, ada_w, ada_b_cols, name):
    Lr, Dm, Nc = ada_w.shape
    Bg = c_all.shape[0]

    def body(c_ref, w_ref, b_ref, o_ref):
        cc = c_ref[...]
        cond = cc * _sigmoid(cc)
        o_ref[0] = _dot(cond.astype(BF16), w_ref[0].astype(BF16)) + b_ref[0]

    return pl.pallas_call(
        body, name=name, grid=(Lr,),
        in_specs=[pl.BlockSpec((Bg, Dm), lambda l: (0, 0)),
                  pl.BlockSpec((1, Dm, Nc), lambda l: (l, 0, 0)),
                  pl.BlockSpec((1, 1, Nc), lambda l: (l, 0, 0))],
        out_specs=pl.BlockSpec((1, Bg, Nc), lambda l: (l, 0, 0)),
        out_shape=jax.ShapeDtypeStruct((Lr, Bg, Nc), F32), compiler_params=_cp(1),
    )(c_all, ada_w, ada_b_cols)


def ada_bwd(c_all_t, dmod_cols, dmod_all, name):
    Dm, Bg = c_all_t.shape
    Lr, _, Nc = dmod_cols.shape
    Nf = dmod_all.shape[2]

    def body(c_ref, dm_ref, da_ref, gw_ref, gb_ref):
        cc = c_ref[...]
        cond = cc * _sigmoid(cc)
        gw_ref[0] = _dot(cond.astype(BF16), dm_ref[0].astype(BF16))
        gb_ref[0] = _sum0(da_ref[0])

    return pl.pallas_call(
        body, name=name, grid=(Lr,),
        in_specs=[pl.BlockSpec((Dm, Bg), lambda l: (0, 0)),
                  pl.BlockSpec((1, Bg, Nc), lambda l: (l, 0, 0)),
                  pl.BlockSpec((1, Bg, Nf), lambda l: (l, 0, 0))],
        out_specs=[pl.BlockSpec((1, Dm, Nc), lambda l: (l, 0, 0)),
                   pl.BlockSpec((1, 1, Nf), lambda l: (l, 0, 0))],
        out_shape=[jax.ShapeDtypeStruct((Lr, Dm, Nc), F32), jax.ShapeDtypeStruct((Lr, 1, Nf), F32)],
        compiler_params=_cp(1),
    )(c_all_t, dmod_cols, dmod_all)


def _conv_taps(x, w, rows):
    shifted = [x]
    c = w[3:4, :] * x
    for k in range(1, 4):
        xs = jnp.where(rows >= k, pltpu.roll(x, k, 0), 0.0)
        shifted.append(xs)
        c = c + w[3 - k:4 - k, :] * xs
    return c, shifted


def conv_silu(proj3, conv_w, name):
    Bl, S, _ = proj3.shape
    ncb = conv_w.shape[1] // 128

    def body(x_ref, w_ref, o_ref):
        rows = lax.broadcasted_iota(jnp.int32, (S, 128), 0)
        c, _ = _conv_taps(_round(x_ref[0]), _round(w_ref[...]), rows)
        o_ref[0] = c * _sigmoid(c)

    return pl.pallas_call(
        body, name=name, grid=(Bl, ncb),
        in_specs=[pl.BlockSpec((1, S, 128), lambda b, j: (b, 0, j)),
                  pl.BlockSpec((4, 128), lambda b, j: (0, j))],
        out_specs=pl.BlockSpec((1, S, 128), lambda b, j: (b, 0, j)),
        out_shape=jax.ShapeDtypeStruct((Bl, S, conv_w.shape[1]), F32), compiler_params=_cp(2),
    )(proj3, conv_w)


def conv_silu_bwd(proj3, conv_w, dq, dk, name):
    Bl, S, _ = proj3.shape
    nq = dq.shape[2] // 128

    def body(x_ref, w_ref, dq_ref, dk_ref, dx_ref, dw_ref):
        j = pl.program_id(1)
        rows = lax.broadcasted_iota(jnp.int32, (S, 128), 0)
        w = _round(w_ref[...])
        c, shifted = _conv_taps(_round(x_ref[0]), w, rows)
        s = _sigmoid(c)
        dact = jnp.where(j < nq, dq_ref[0], dk_ref[0])
        dc = _round(dact * (s * (1.0 + c * (1.0 - s))))
        dx = w[3:4, :] * dc
        dws = [_sum0(dc * shifted[0])]
        for k in range(1, 4):
            up = jnp.where(rows < S - k, pltpu.roll(dc, S - k, 0), 0.0)
            dx = dx + w[3 - k:4 - k, :] * up
            dws.append(_sum0(dc * shifted[k]))
        dx_ref[0] = dx.astype(BF16)
        tap = lax.broadcasted_iota(jnp.int32, (4, 128), 0)
        dw_ref[0] = functools.reduce(lambda a, b: a + b, [jnp.where(tap == 3 - k, dws[k], 0.0) for k in range(4)])

    return pl.pallas_call(
        body, name=name, grid=(Bl, 2 * nq),
        in_specs=[pl.BlockSpec((1, S, 128), lambda b, j: (b, 0, j)),
                  pl.BlockSpec((4, 128), lambda b, j: (0, j)),
                  pl.BlockSpec((1, S, 128), lambda b, j: (b, 0, jnp.minimum(j, nq - 1))),
                  pl.BlockSpec((1, S, 128), lambda b, j: (b, 0, jnp.maximum(j - nq, 0)))],
        out_specs=[pl.BlockSpec((1, S, 128), lambda b, j: (b, 0, j)),
                   pl.BlockSpec((1, 4, 128), lambda b, j: (b, 0, j))],
        out_shape=[jax.ShapeDtypeStruct((Bl, S, 2 * nq * 128), BF16),
                   jax.ShapeDtypeStruct((Bl, 4, 2 * nq * 128), F32)],
        compiler_params=_cp(2),
    )(proj3, conv_w, dq, dk)


def _log_sigmoid(a):
    return jnp.minimum(a, 0.0) - jnp.log(1.0 + jnp.exp(-jnp.abs(a)))


def _chunk_state(kc, vc, gi, bcum, b_last, C, n, m):
    a = b_last - bcum + gi
    m_loc = jnp.max(a, axis=0, keepdims=True)
    wa = jnp.exp(a - m_loc)
    c_loc = _dot_tn((wa * vc).astype(BF16), kc.astype(BF16))
    n_loc = _sum0(_round(wa) * _round(kc))
    m_new = jnp.maximum(b_last + m, m_loc)
    sp = jnp.exp(b_last + m - m_new)
    sl = jnp.exp(m_loc - m_new)
    return sp * C + sl * c_loc, sp * n + sl * n_loc, m_new, wa, sp, sl


def _chunk_out(qs, kc, vc, gi_row, bcum, bcum_row, low, C, n, m):
    inter_log = bcum + m
    dlog = jnp.where(low, bcum - bcum_row + gi_row, NEG)
    m_i = jnp.maximum(inter_log, jnp.max(dlog, axis=1, keepdims=True))
    dm = jnp.exp(dlog - m_i)
    iw = jnp.exp(inter_log - m_i)
    qs_b, k_b, v_b = qs.astype(BF16), kc.astype(BF16), vc.astype(BF16)
    sc = _dot_nt(qs_b, k_b) * dm
    qc_ = _dot_nt(qs_b, C.astype(BF16))
    qn = _sum1(_round(qs) * _round(n))
    num = _dot(sc.astype(BF16), v_b) + iw * qc_
    den = _sum1(sc) + iw * qn
    floor = jnp.exp(-m_i)
    dn = jnp.maximum(jnp.abs(den), floor)
    return dict(hc=num / dn, den=den, dn=dn, floor=floor, sc=sc, dm=dm, iw=iw, qc=qc_, qn=qn,
                qs_b=qs_b, k_b=k_b, v_b=v_b)


def _cell_consts(L):
    ri = lax.broadcasted_iota(jnp.int32, (L, L), 0)
    ci = lax.broadcasted_iota(jnp.int32, (L, L), 1)
    return ri == ci, ci <= ri, ri <= ci


def _load_chunk(q_ref, k_ref, v_ref, G, off, L, h, lane):
    hh = h % 2
    qmask = (lane >= M_DQK * hh) & (lane < M_DQK * (hh + 1))
    pair = pl.ds(128 * (h // 2), 128)
    qc = jnp.where(qmask, q_ref[0, pl.ds(off, L), pair], 0.0)
    kc = jnp.where(qmask, k_ref[0, pl.ds(off, L), pair], 0.0)
    vc = v_ref[0, pl.ds(off, L), pl.ds(M_DV * h, M_DV)]
    gi = _sum1(jnp.where(lane == h, G, 0.0))
    gf = _sum1(jnp.where(lane == h + HEADS, G, 0.0))
    return qmask, qc, kc, vc, gi, gf


def _gate_rows(gi, gf, eye, low, upp):
    lf = _log_sigmoid(gf)
    lf_row = _sum0(jnp.where(eye, lf, 0.0))
    gi_row = _sum0(jnp.where(eye, gi, 0.0))
    bcum = _sum1(jnp.where(low, lf_row, 0.0))
    bcum_row = _sum0(jnp.where(upp, lf, 0.0))
    b_last = _sum0(lf)
    return gi_row, bcum, bcum_row, b_last


def _cell_specs(SB, cpb, blk):
    def seq(width, col):
        return pl.BlockSpec((1, SB, width), lambda b, s: (b, blk(s), col))

    def state(rows):
        return pl.BlockSpec((1, HEADS, cpb, rows, 128), lambda b, s: (b, 0, blk(s), 0, 0))

    ins = [seq(D // 2, 0), seq(D // 2, 1), seq(D, 1), seq(D, 2), seq(128, 3 * D // 128),
           pl.BlockSpec((1, D), lambda b, s: (0, 0)), pl.BlockSpec((1, 128), lambda b, s: (0, 0))]
    return ins, [state(M_DV), state(1), state(1)], seq


def mlstm_cell_fwd(qk3, proj3, gain, gbias, name, gather=()):
    Bl, S, _ = qk3.shape
    L = M_CHUNK
    SB = min(M_SLAB, S)
    cpb, nc, nsb = SB // L, S // L, S // SB
    scale = M_DQK ** -0.5
    ng = len(gather)

    def body(q_ref, k_ref, v_ref, o_ref, g_ref, gain_ref, gb_ref, *rest):
        ag_in, (y_ref, cst_ref, nst_ref, mst_ref), rest = rest[:ng], rest[ng:ng + 4], rest[ng + 4:]
        ag_out, (C_s, n_s, m_s), ag_sems = rest[:ng], rest[ng:ng + 3], rest[ng + 3:]
        b, sl = pl.program_id(0), pl.program_id(1)
        if ng:
            @pl.when((b == 0) & (sl == 0))
            def _():
                Gather(ag_in, ag_out, *ag_sems).start()

        @pl.when(sl == 0)
        def _():
            C_s[...] = jnp.zeros_like(C_s)
            n_s[...] = jnp.zeros_like(n_s)
            m_s[...] = jnp.zeros_like(m_s)
        lane = lax.broadcasted_iota(jnp.int32, (L, 128), 1)
        eye, low, upp = _cell_consts(L)

        def step(c, carry):
            off = pl.multiple_of(c * L, L)
            G = g_ref[0, pl.ds(off, L), :] + gb_ref[...]
            for h in range(HEADS):
                C, n, mb = C_s[h], n_s[h], m_s[h]
                cst_ref[0, h, c] = C
                nst_ref[0, h, c] = n
                mst_ref[0, h, c] = mb
                m = mb[:, 0:1]
                _, qc, kc, vc, gi, gf = _load_chunk(q_ref, k_ref, v_ref, G, off, L, h, lane)
                gi_row, bcum, bcum_row, b_last = _gate_rows(gi, gf, eye, low, upp)
                r = _chunk_out(qc * scale, kc, vc, gi_row, bcum, bcum_row, low, C, n, m)
                hc = r["hc"]
                hn = hc * lax.rsqrt(jnp.mean(hc * hc, axis=-1, keepdims=True) + RMS_EPS)
                cols = pl.ds(M_DV * h, M_DV)
                oc = o_ref[0, pl.ds(off, L), cols]
                y_ref[0, pl.ds(off, L), cols] = (_sigmoid(oc) * hn * gain_ref[:, cols]).astype(BF16)
                C2, n2, m2, _, _, _ = _chunk_state(kc, vc, gi, bcum, b_last, C, n, m)
                C_s[h] = C2
                n_s[h] = n2
                m_s[h] = jnp.broadcast_to(m2, (1, 128))
            return carry

        lax.fori_loop(0, cpb, step, 0)
        if ng:
            @pl.when((b == Bl - 1) & (sl == nsb - 1))
            def _():
                Gather(ag_in, ag_out, *ag_sems).finish()

    ins, states, seq = _cell_specs(SB, cpb, lambda s: s)
    return pl.pallas_call(
        body, name=name, grid=(Bl, nsb), in_specs=ins + [ANY_SPEC] * ng,
        out_specs=[seq(D, 0)] + states + [ANY_SPEC] * ng,
        out_shape=[jax.ShapeDtypeStruct((Bl, S, D), BF16),
                   jax.ShapeDtypeStruct((Bl, HEADS, nc, M_DV, 128), F32),
                   jax.ShapeDtypeStruct((Bl, HEADS, nc, 1, 128), F32),
                   jax.ShapeDtypeStruct((Bl, HEADS, nc, 1, 128), F32)] + Gather.out_shape(gather),
        scratch_shapes=[pltpu.VMEM((HEADS, M_DV, 128), F32), pltpu.VMEM((HEADS, 1, 128), F32),
                        pltpu.VMEM((HEADS, 1, 128), F32)] + (Gather.scratch(ng) if ng else []),
        compiler_params=_cp(2),
    )(qk3, qk3, proj3, proj3, proj3, gain, gbias, *gather)


def mlstm_cell_bwd(qk3, proj3, gain, gbias, dy3, states, name, exchange=None):
    Bl, S, _ = qk3.shape
    L = M_CHUNK
    SB = min(M_SLAB, S)
    cpb, nsb = SB // L, S // SB
    scale = M_DQK ** -0.5
    nx = 0 if exchange is None else 1

    def body(q_ref, k_ref, v_ref, o_ref, g_ref, gain_ref, gb_ref, cst_ref, nst_ref, mst_ref, dy_ref, *rest):
        x_in, rest = rest[:nx], rest[nx:]
        (dq_ref, dk_ref, dv_ref, do_ref, dg_ref, dgain_ref, dgb_ref), rest = rest[:7], rest[7:]
        x_out, (dC_s, dn_s, dgain_s, dgb_s), x_sems = rest[:nx], rest[nx:nx + 4], rest[nx + 4:]
        b, s = pl.program_id(0), pl.program_id(1)
        if nx:
            @pl.when((b == 0) & (s == 0))
            def _():
                Exchange(x_in[0], x_out[0], *x_sems).start()

        @pl.when(s == 0)
        def _():
            dC_s[...] = jnp.zeros_like(dC_s)
            dn_s[...] = jnp.zeros_like(dn_s)
            dgain_s[...] = jnp.zeros_like(dgain_s)
            dgb_s[...] = jnp.zeros_like(dgb_s)
        lane = lax.broadcasted_iota(jnp.int32, (L, 128), 1)
        rowi = lax.broadcasted_iota(jnp.int32, (L, 1), 0)
        eye, low, upp = _cell_consts(L)

        def bstep(t, carry):
            c = cpb - 1 - t
            off = pl.multiple_of(c * L, L)
            G = g_ref[0, pl.ds(off, L), :] + gb_ref[...]
            slab = jnp.zeros((L, 128), F32)
            dq_pair = dk_pair = None
            for h in range(HEADS):
                cols = pl.ds(M_DV * h, M_DV)
                gain_h = gain_ref[:, cols]
                C, n, m = cst_ref[0, h, c], nst_ref[0, h, c], mst_ref[0, h, c][:, 0:1]
                dC_n, dn_n = dC_s[h], dn_s[h]
                qmask, qc, kc, vc, gi, gf = _load_chunk(q_ref, k_ref, v_ref, G, off, L, h, lane)
                gi_row, bcum, bcum_row, b_last = _gate_rows(gi, gf, eye, low, upp)
                qs = qc * scale
                r = _chunk_out(qs, kc, vc, gi_row, bcum, bcum_row, low, C, n, m)
                _, _, _, wa, sp, sl = _chunk_state(kc, vc, gi, bcum, b_last, C, n, m)
                hc, den, dn, sc, dm, iw, qn = r["hc"], r["den"], r["dn"], r["sc"], r["dm"], r["iw"], r["qn"]
                qs_b, k_b, v_b = r["qs_b"], r["k_b"], r["v_b"]
                dy = dy_ref[0, pl.ds(off, L), cols].astype(F32)
                oc = o_ref[0, pl.ds(off, L), cols]
                sig_o = _sigmoid(oc)
                rr = lax.rsqrt(jnp.mean(hc * hc, axis=-1, keepdims=True) + RMS_EPS)
                hn = hc * rr
                dgain_s[:, cols] += _sum0(dy * sig_o * hn)
                do_ref[0, pl.ds(off, L), cols] = (
                    dy * hn * gain_h * sig_o * (1.0 - sig_o)).astype(BF16)
                dhn = dy * sig_o * gain_h
                dhc = rr * dhn - hc * (rr * rr * rr) * jnp.mean(dhn * hc, axis=-1, keepdims=True)
                dnum = dhc / dn
                gden = -_sum1(dhc * hc) / dn
                dden = jnp.where(jnp.abs(den) > r["floor"], gden * jnp.sign(den), 0.0)
                dnum_b = dnum.astype(BF16)
                dsc = _dot_nt(dnum_b, v_b) + dden
                dv = _dot_tn(sc.astype(BF16), dnum_b)
                diw = _sum1(dnum * r["qc"]) + dden * qn
                dqc_b = (iw * dnum).astype(BF16)
                wq = iw * dden
                dqs = _dot(dqc_b, C.astype(BF16)) + wq * n
                dC_out = _dot_tn(dqc_b, qs_b)
                dn_out = _sum0(wq * qs)
                dS_b = (dsc * dm).astype(BF16)
                gm = dsc * sc
                dqs = dqs + _dot(dS_b, k_b)
                dk = _dot_tn(dS_b, qs_b)
                dbc = _sum1(gm) + diw * iw
                colg = _sum0(gm)
                dC_p = sp * dC_n + dC_out
                dn_p = sp * dn_n + dn_out
                dcl_b = (sl * dC_n).astype(BF16)
                dn_loc = sl * dn_n
                dsp = _sum1(_sum0(dC_n * C)) + _sum1(dn_n * n)
                db_last = dsp * sp
                t1 = _dot(v_b, dcl_b) + dn_loc
                dwa = _sum1(t1 * kc)
                dv = dv + wa * _dot_nt(k_b, dcl_b)
                dk = dk + wa * t1
                da = dwa * wa
                db_last = db_last + _sum0(da)
                dbc = dbc - da + jnp.where(rowi == L - 1, db_last, 0.0)
                dbc_row = _sum0(jnp.where(eye, dbc, 0.0)) - colg
                dgi = da + _sum1(jnp.where(eye, colg, 0.0))
                dlf = _sum1(jnp.where(upp, dbc_row, 0.0))
                dgf = dlf * _sigmoid(-gf)
                dq = jnp.where(qmask, dqs * scale, 0.0)
                dk = jnp.where(qmask, dk, 0.0)
                slab = slab + jnp.where(lane == h, dgi, 0.0) + jnp.where(lane == h + HEADS, dgf, 0.0)
                dv_ref[0, pl.ds(off, L), cols] = dv.astype(BF16)
                dC_s[h] = dC_p
                dn_s[h] = dn_p
                if h % 2 == 0:
                    dq_pair, dk_pair = dq, dk
                else:
                    pair = pl.ds(128 * (h // 2), 128)
                    dq_ref[0, pl.ds(off, L), pair] = dq_pair + dq
                    dk_ref[0, pl.ds(off, L), pair] = dk_pair + dk
            dg_ref[0, pl.ds(off, L), :] = slab
            dgb_s[...] += _sum0(slab)
            return carry

        lax.fori_loop(0, cpb, bstep, 0)

        @pl.when(s == nsb - 1)
        def _():
            dgain_ref[0] = dgain_s[...]
            dgb_ref[0] = dgb_s[...]
        if nx:
            @pl.when((b == Bl - 1) & (s == nsb - 1))
            def _():
                Exchange(x_in[0], x_out[0], *x_sems).finish()

    ins, states_specs, seq = _cell_specs(SB, cpb, lambda s: nsb - 1 - s)
    once = lambda width: pl.BlockSpec((1, 1, width), lambda b, s: (b, 0, 0))
    sent = [exchange] * nx
    return pl.pallas_call(
        body, name=name, grid=(Bl, nsb), in_specs=ins + states_specs + [seq(D, 0)] + [ANY_SPEC] * nx,
        out_specs=[seq(D // 2, 0), seq(D // 2, 0), seq(D, 0), seq(D, 0), seq(128, 0), once(D), once(128)]
        + [ANY_SPEC] * nx,
        out_shape=[jax.ShapeDtypeStruct((Bl, S, D // 2), F32), jax.ShapeDtypeStruct((Bl, S, D // 2), F32),
                   jax.ShapeDtypeStruct((Bl, S, D), BF16), jax.ShapeDtypeStruct((Bl, S, D), BF16),
                   jax.ShapeDtypeStruct((Bl, S, 128), F32), jax.ShapeDtypeStruct((Bl, 1, D), F32),
                   jax.ShapeDtypeStruct((Bl, 1, 128), F32)]
        + [jax.ShapeDtypeStruct(a.shape, a.dtype) for a in sent],
        scratch_shapes=[pltpu.VMEM((HEADS, M_DV, 128), F32), pltpu.VMEM((HEADS, 1, 128), F32),
                        pltpu.VMEM((1, D), F32), pltpu.VMEM((1, 128), F32)] + (Exchange.scratch() if nx else []),
        compiler_params=_cp(2),
    )(qk3, qk3, proj3, proj3, proj3, gain, gbias, *states, dy3, *sent)


def _attn_scores(q, kc, kp, n, row, col, scale):
    s_c = jnp.where(col <= row, _dot_nt(q, kc) * scale, NEG)
    s_p = jnp.where(jnp.logical_and(col >= row, n > 0), _dot_nt(q, kp) * scale, NEG)
    return s_c, s_p


def attn_fwd(proj, Bl, S, g, dil, name):
    Sd = S // dil
    nb = Sd // A_BLK
    scale = A_BLK ** -0.5
    pv = proj.reshape(Bl, Sd, dil * A_PROJ)
    ncol = A_PROJ // 128

    def body(q_ref, k_ref, v_ref, o_ref, l_ref):
        row = lax.broadcasted_iota(jnp.int32, (A_BLK, A_BLK), 0)
        col = lax.broadcasted_iota(jnp.int32, (A_BLK, A_BLK), 1)

        def step(n, carry):
            off = pl.multiple_of(n * A_BLK, A_BLK)
            offp = pl.multiple_of(jnp.maximum(n - 1, 0) * A_BLK, A_BLK)
            q = q_ref[0, pl.ds(off, A_BLK), :]
            s_c, s_p = _attn_scores(q, k_ref[0, pl.ds(off, A_BLK), :], k_ref[0, pl.ds(offp, A_BLK), :],
                                    n, row, col, scale)
            m = jnp.maximum(jnp.max(s_c, axis=1, keepdims=True), jnp.max(s_p, axis=1, keepdims=True))
            p_c = jnp.exp(s_c - m)
            p_p = jnp.exp(s_p - m)
            den = _sum1(p_c) + _sum1(p_p)
            o = _dot(p_c.astype(BF16), v_ref[0, pl.ds(off, A_BLK), :]) + _dot(
                p_p.astype(BF16), v_ref[0, pl.ds(offp, A_BLK), :])
            o_ref[0, pl.ds(off, A_BLK), :] = o / den
            l_ref[0, pl.ds(off, A_BLK), :] = jnp.broadcast_to(m + jnp.log(den), (A_BLK, 128))
            return carry

        lax.fori_loop(0, nb, step, 0)

    def spec(j):
        return pl.BlockSpec((1, Sd, 128), lambda b, r, h: (b, 0, r * ncol + g * 24 + j * HEADS + h))

    ospec = pl.BlockSpec((1, Sd, 128), lambda b, r, h: (b, 0, r * HEADS + h))
    o, lse = pl.pallas_call(
        body, name=name, grid=(Bl, dil, HEADS),
        in_specs=[spec(0), spec(1), spec(2)], out_specs=[ospec, ospec],
        out_shape=[jax.ShapeDtypeStruct((Bl, Sd, dil * D), F32)] * 2, compiler_params=_cp(3),
    )(pv, pv, pv)
    return o.reshape(Bl * S, D), lse.reshape(Bl * S, D)


def attn_merge(os_, lses, name):
    T = os_[0].shape[0]
    tm = _tile(T, 512)
    ng = len(os_)

    def body(*refs):
        o_refs, l_refs = refs[:ng], refs[ng:2 * ng]
        ob_ref, of_ref, lt_ref = refs[2 * ng:]
        ls = [r[...] for r in l_refs]
        m = functools.reduce(jnp.maximum, ls)
        ws = [jnp.exp(l - m) for l in ls]
        den = functools.reduce(lambda a, b: a + b, ws)
        o = functools.reduce(lambda a, b: a + b, [w * r[...] for w, r in zip(ws, o_refs)]) / den
        of_ref[...] = o
        ob_ref[...] = o.astype(BF16)
        lt_ref[...] = m + jnp.log(den)

    spec = pl.BlockSpec((tm, D), lambda i: (i, 0))
    return pl.pallas_call(
        body, name=name, grid=(T // tm,), in_specs=[spec] * (2 * ng), out_specs=[spec] * 3,
        out_shape=[jax.ShapeDtypeStruct((T, D), BF16), jax.ShapeDtypeStruct((T, D), F32),
                   jax.ShapeDtypeStruct((T, D), F32)],
        compiler_params=_cp(1),
    )(*os_, *lses)


def attn_bwd(proj, do, o, lse, Bl, S, g, dil, name):
    Sd = S // dil
    nb = Sd // A_BLK
    scale = A_BLK ** -0.5
    pv = proj.reshape(Bl, Sd, dil * A_PROJ)
    ncol = A_PROJ // 128
    dov, ov, lv = (t.reshape(Bl, Sd, dil * D) for t in (do, o, lse))

    def body(q_ref, k_ref, v_ref, do_ref, o_ref, l_ref, dq_ref, dk_ref, dv_ref, dk_s, dv_s):
        row = lax.broadcasted_iota(jnp.int32, (A_BLK, A_BLK), 0)
        col = lax.broadcasted_iota(jnp.int32, (A_BLK, A_BLK), 1)
        dk_s[...] = jnp.zeros_like(dk_s)
        dv_s[...] = jnp.zeros_like(dv_s)

        def step(n, carry):
            off = pl.multiple_of(n * A_BLK, A_BLK)
            offp = pl.multiple_of(jnp.maximum(n - 1, 0) * A_BLK, A_BLK)
            q = q_ref[0, pl.ds(off, A_BLK), :]
            kc, kp = k_ref[0, pl.ds(off, A_BLK), :], k_ref[0, pl.ds(offp, A_BLK), :]
            vc, vp = v_ref[0, pl.ds(off, A_BLK), :], v_ref[0, pl.ds(offp, A_BLK), :]
            do_b = do_ref[0, pl.ds(off, A_BLK), :]
            delta = _sum1(do_b.astype(F32) * o_ref[0, pl.ds(off, A_BLK), :])
            lt = l_ref[0, pl.ds(off, A_BLK), :][:, 0:1]
            s_c, s_p = _attn_scores(q, kc, kp, n, row, col, scale)
            p_c = jnp.exp(s_c - lt)
            p_p = jnp.exp(s_p - lt)
            ds_c = (p_c * (_dot_nt(do_b, vc) - delta) * scale).astype(BF16)
            ds_p = (p_p * (_dot_nt(do_b, vp) - delta) * scale).astype(BF16)
            dq_ref[0, pl.ds(off, A_BLK), :] = (_dot(ds_c, kc) + _dot(ds_p, kp)).astype(BF16)
            dk_s[pl.ds(off, A_BLK), :] += _dot_tn(ds_c, q)
            dk_s[pl.ds(offp, A_BLK), :] += _dot_tn(ds_p, q)
            dv_s[pl.ds(off, A_BLK), :] += _dot_tn(p_c.astype(BF16), do_b)
            dv_s[pl.ds(offp, A_BLK), :] += _dot_tn(p_p.astype(BF16), do_b)
            return carry

        lax.fori_loop(0, nb, step, 0)
        dk_ref[0] = dk_s[...].astype(BF16)
        dv_ref[0] = dv_s[...].astype(BF16)

    def spec(j):
        return pl.BlockSpec((1, Sd, 128), lambda b, r, h: (b, 0, r * ncol + g * 24 + j * HEADS + h))

    ospec = pl.BlockSpec((1, Sd, 128), lambda b, r, h: (b, 0, r * HEADS + h))
    outs = pl.pallas_call(
        body, name=name, grid=(Bl, dil, HEADS),
        in_specs=[spec(0), spec(1), spec(2), ospec, ospec, ospec], out_specs=[ospec] * 3,
        out_shape=[jax.ShapeDtypeStruct((Bl, Sd, dil * D), BF16)] * 3,
        scratch_shapes=[pltpu.VMEM((Sd, 128), F32), pltpu.VMEM((Sd, 128), F32)],
        compiler_params=_cp(3),
    )(pv, pv, pv, dov, ov, lv)
    return [t.reshape(Bl * S, D) for t in outs]


def ffn_fwd(x, mod3, w_in, w_out, lng, lnb, tag):
    a, g, u, h = ffn_in(x, mod3, w_in, tag + "_in", tf=1408)
    out, xn = proj_post(a, w_out, x, mod3, lng, lnb, 0.5, tag + "_out", tk=D_FF)
    return xn, (x, out, g, u, h, a)


def ffn_bwd(dxn, saved, mod3, w_in, w_out, lng, tag):
    x, out, g, u, h, a = saved
    dxres, dout, dgu, dlg, dlb, dgate = post_bwd(dxn, x, out, mod3, lng, w_out, 0.5, tag + "_outb", tk=D_FF,
                                                 tm=256, gu=(g, u))
    dx, dsh, dsc = modmm_bwd(dgu, w_in, x, mod3, dxres, tag + "_inb", tn=2 * D_FF, tm=256)
    dw_in = mm_tn(h, dgu, tag + "_dwin", tk=1024, tn=1408)
    dw_out = mm_tn(a, dout, tag + "_dwout", tk=1408, tn=1024)
    return dx, dw_in, dw_out, dlg, dlb, jnp.concatenate([dsh, dsc, dgate], axis=1)


def mlstm_fwd(x, mod3, w_in, w_out, conv_w, gain, gbias, lng, lnb, Bl, S, gather=()):
    proj, h = modmm(x, mod3, w_in, F32, "ml_in", tn=M_PROJ_PAD // 5)
    proj3 = proj.reshape(Bl, S, M_PROJ_PAD)
    qk3 = conv_silu(proj3, conv_w, "ml_conv")
    y3, *rest = mlstm_cell_fwd(qk3, proj3, gain, gbias, "ml_cell", gather=gather)
    states, gathered = rest[:3], rest[3:]
    y = y3.reshape(Bl * S, D)
    out, xn = proj_post(y, w_out, x, mod3, lng, lnb, 1.0, "ml_out", tk=D)
    return xn, (x, out, h, proj3, qk3, y, states), gathered


def mlstm_bwd(dxn, saved, mod3, w_in, w_out, conv_w, gain, gbias, lng, Bl, S, exchange=None):
    x, out, h, proj3, qk3, y, states = saved
    dxres, dout, dy, dlg, dlb, dgate = post_bwd(dxn, x, out, mod3, lng, w_out, 1.0, "ml_outb", tk=D)
    dq, dk, dv, do, dg, dgain, dgb, *received = mlstm_cell_bwd(qk3, proj3, gain, gbias, dy.reshape(Bl, S, D),
                                                               states, "ml_cellb", exchange=exchange)
    dqk, dconv = conv_silu_bwd(proj3, conv_w, dq, dk, "ml_convb")
    dproj = jnp.concatenate([dqk, dv, do, dg.astype(BF16)], axis=2).reshape(Bl * S, M_PROJ_PAD)
    dx, dsh, dsc = modmm_bwd(dproj, w_in, x, mod3, dxres, "ml_inb", tn=M_PROJ_PAD // 5)
    dw_in = mm_tn(h, dproj, "ml_dwin", tk=1024, tn=M_PROJ_PAD // 5)
    dw_out = mm_tn(y, dout, "ml_dwout", tk=1024, tn=1024)
    small = (jnp.sum(dconv, axis=0), jnp.sum(dgain, axis=0), jnp.sum(dgb, axis=0)[:, :2 * HEADS])
    dmod3 = jnp.concatenate([dsh, dsc, dgate], axis=1)
    return (dx, dw_in[:, :M_PROJ], dw_out, dlg, dlb, dmod3, small) + tuple(received)


def attn_mixer_fwd(x, mod3, w_in, w_out, lng, lnb, Bl, S):
    proj, h = modmm(x, mod3, w_in, BF16, "at_in", tn=2304)
    os_, lses = [], []
    for g, (_, dil) in enumerate(DIL_GROUPS):
        o_g, l_g = attn_fwd(proj, Bl, S, g, dil, "at_core%d" % g)
        os_.append(o_g)
        lses.append(l_g)
    ob, of, lt = attn_merge(os_, lses, "at_merge")
    out, xn = proj_post(ob, w_out, x, mod3, lng, lnb, 1.0, "at_out", tk=D)
    return xn, (x, out, h, proj, ob, of, lt)


def attn_mixer_bwd(dxn, saved, mod3, w_in, w_out, lng, Bl, S):
    x, out, h, proj, ob, of, lt = saved
    dxres, dout, do, dlg, dlb, dgate = post_bwd(dxn, x, out, mod3, lng, w_out, 1.0, "at_outb", tk=D)
    parts = []
    for g, (_, dil) in enumerate(DIL_GROUPS):
        parts += attn_bwd(proj, do, of, lt, Bl, S, g, dil, "at_coreb%d" % g)
    dproj = jnp.concatenate(parts, axis=1)
    dx, dsh, dsc = modmm_bwd(dproj, w_in, x, mod3, dxres, "at_inb", tn=2304)
    dw_in = mm_tn(h, dproj, "at_dwin", tk=1024, tn=2304)
    dw_out = mm_tn(ob, dout, "at_dwout", tk=1024, tn=1024)
    return dx, dw_in, dw_out, dlg, dlb, jnp.concatenate([dsh, dsc, dgate], axis=1)


PARTS = (
    (("ffn_w_in", 0, 2), ("ffn_w_out", 0, 1), ("mlstm_w_in", 0, 1), ("mlstm_w_out", 0, 0)),
    (("ffn_w_in", 1, 2), ("ffn_w_out", 1, 1), ("attn_w_in", 0, 1), ("attn_w_out", 0, 0)),
)


def _pack_rows(shards):
    rows = [math.prod(s.shape) // PACK_COLS for s in shards]
    padded = [-(-r // 16) * 16 for r in rows]
    total = -(-sum(padded) // 128) * 128
    return rows, padded, total


def _pad_rows(a, n, axis):
    pad = [(0, 0)] * a.ndim
    pad[axis] = (0, n - a.shape[axis])
    return jnp.pad(a, pad)


def pack_shards(shards):
    _, padded, total = _pack_rows(shards)
    return jnp.concatenate([_pad_rows(s.astype(BF16).reshape(-1, PACK_COLS), pr, 0) for s, pr in zip(shards, padded)]
                           + [jnp.zeros((total - sum(padded), PACK_COLS), BF16)], axis=0)


def unpack_gathered(pack_all, shards, axes):
    rows, padded, _ = _pack_rows(shards)
    starts = [sum(padded[:i]) for i in range(len(shards))]
    return [_unstack(pack_all[:, r0:r0 + r].reshape((N_DEV,) + s.shape), ax)
            for s, ax, r, r0 in zip(shards, axes, rows, starts)]


def pack_grads(fulls, shards, axes):
    _, padded, total = _pack_rows(shards)
    return jnp.concatenate(
        [_pad_rows(_restack(g, ax).reshape(N_DEV, -1, PACK_COLS), pr, 1) for g, ax, pr in zip(fulls, axes, padded)]
        + [jnp.zeros((N_DEV, total - sum(padded), PACK_COLS), F32)], axis=1)


def unpack_shards(pack, shards):
    rows, padded, _ = _pack_rows(shards)
    starts = [sum(padded[:i]) for i in range(len(shards))]
    return [pack[r0:r0 + r].reshape(s.shape) for s, r, r0 in zip(shards, rows, starts)]


def _unstack(stacked, axis):
    full = jnp.moveaxis(stacked, 0, axis)
    shp = list(full.shape)
    shp[axis:axis + 2] = [shp[axis] * shp[axis + 1]]
    return full.reshape(shp)


def _restack(full, axis):
    shp = list(full.shape)
    shp[axis:axis + 1] = [N_DEV, shp[axis] // N_DEV]
    return jnp.moveaxis(full.reshape(shp), axis, 0)


def kernel(x, c, ada_w, ada_b, ln_g, ln_b, ffn_w_in, ffn_w_out, mlstm_w_in, mlstm_gate_bias, mlstm_conv_w, mlstm_head_gain, mlstm_w_out, attn_w_in, attn_w_out, loss_target, m_ada_w, m_ada_b, m_ln_g, m_ln_b, m_ffn_w_in, m_ffn_w_out, m_mlstm_w_in, m_mlstm_gate_bias, m_mlstm_conv_w, m_mlstm_head_gain, m_mlstm_w_out, m_attn_w_in, m_attn_w_out, v_ada_w, v_ada_b, v_ln_g, v_ln_b, v_ffn_w_in, v_ffn_w_out, v_mlstm_w_in, v_mlstm_gate_bias, v_mlstm_conv_w, v_mlstm_head_gain, v_mlstm_w_out, v_attn_w_in, v_attn_w_out):
    Bl, S, _ = x.shape
    T = Bl * S
    Bg = Bl * N_DEV
    me = 4 * lax.axis_index("x") + 2 * lax.axis_index("y") + lax.axis_index("c")
    onehot = (jnp.arange(N_DEV) == me).astype(F32)
    weights = dict(ada_w=ada_w, ada_b=ada_b, ln_g=ln_g, ln_b=ln_b, ffn_w_in=ffn_w_in, ffn_w_out=ffn_w_out,
                   mlstm_w_in=mlstm_w_in, mlstm_gate_bias=mlstm_gate_bias, mlstm_conv_w=mlstm_conv_w,
                   mlstm_head_gain=mlstm_head_gain, mlstm_w_out=mlstm_w_out, attn_w_in=attn_w_in,
                   attn_w_out=attn_w_out)
    m_in = dict(ada_w=m_ada_w, ada_b=m_ada_b, ln_g=m_ln_g, ln_b=m_ln_b, ffn_w_in=m_ffn_w_in,
                ffn_w_out=m_ffn_w_out, mlstm_w_in=m_mlstm_w_in, mlstm_gate_bias=m_mlstm_gate_bias,
                mlstm_conv_w=m_mlstm_conv_w, mlstm_head_gain=m_mlstm_head_gain, mlstm_w_out=m_mlstm_w_out,
                attn_w_in=m_attn_w_in, attn_w_out=m_attn_w_out)
    v_in = dict(ada_w=v_ada_w, ada_b=v_ada_b, ln_g=v_ln_g, ln_b=v_ln_b, ffn_w_in=v_ffn_w_in,
                ffn_w_out=v_ffn_w_out, mlstm_w_in=v_mlstm_w_in, mlstm_gate_bias=v_mlstm_gate_bias,
                mlstm_conv_w=v_mlstm_conv_w, mlstm_head_gain=v_mlstm_head_gain, mlstm_w_out=v_mlstm_w_out,
                attn_w_in=v_attn_w_in, attn_w_out=v_attn_w_out)

    shards = [[weights[k][i] for k, i, _ in PARTS[layer]] for layer in range(DEPTH)]
    axes = [[ax for _, _, ax in PARTS[layer]] for layer in range(DEPTH)]
    keys = [[k for k, _, _ in PARTS[layer]] for layer in range(DEPTH)]
    packs = [pack_shards(s) for s in shards]
    small = jnp.concatenate([c.reshape(-1), ln_g.reshape(-1), ln_b.reshape(-1), mlstm_conv_w.reshape(-1)])
    n_small = small.shape[0]
    small = jnp.pad(small, (0, -n_small % (8 * PACK_COLS))).reshape(-1, PACK_COLS)
    pack0_all, small_all = all_gather([packs[0], small], "ag_params")
    full = [dict(zip(keys[0], unpack_gathered(pack0_all, shards[0], axes[0]))), None]
    small_flat = small_all.reshape(N_DEV, -1)
    o0 = 0
    c_all = small_flat[:, o0:o0 + c.size].reshape(Bg, D)
    o0 += c.size
    lng_full = _unstack(small_flat[:, o0:o0 + ln_g.size].reshape((N_DEV,) + ln_g.shape), 2)
    o0 += ln_g.size
    lnb_full = _unstack(small_flat[:, o0:o0 + ln_b.size].reshape((N_DEV,) + ln_b.shape), 2)
    o0 += ln_b.size
    conv_full = _unstack(small_flat[:, o0:o0 + mlstm_conv_w.size].reshape((N_DEV,) + mlstm_conv_w.shape), 2)[0]
    mw_in = jnp.pad(full[0]["mlstm_w_in"], ((0, 0), (0, M_PROJ_PAD - M_PROJ)))
    gbias = jnp.pad(mlstm_gate_bias, ((0, 0), (0, 128 - 2 * HEADS)))

    ncols = ada_w.shape[2]
    ada_b_cols = lax.dynamic_slice_in_dim(ada_b, me * ncols, ncols, axis=1).reshape(DEPTH, 1, ncols)
    mod_cols = ada_fwd(c_all, ada_w, ada_b_cols, "ada_fwd")
    (mod_g,) = all_gather([mod_cols.reshape(DEPTH * Bg, ncols)], "ag_mod")
    mod_full = _unstack(mod_g.reshape(N_DEV, DEPTH, Bg, ncols), 2)
    mod_mine = lax.dynamic_slice_in_dim(mod_full, me * Bl, Bl, axis=1).reshape(DEPTH, Bl, 3, 3, D)

    xt = x.reshape(T, D)
    saved = []
    for layer in range(DEPTH):
        def lnp(s, layer=layer):
            return lng_full[layer, s].reshape(1, D), lnb_full[layer, s].reshape(1, D)
        md = mod_mine[layer]
        fw = full[layer]
        xt, sv0 = ffn_fwd(xt, md[:, 0], fw["ffn_w_in"][0], fw["ffn_w_out"][0], *lnp(0), "f%da" % layer)
        if layer % 2 == 0:
            xt, sv1, (pack1_all,) = mlstm_fwd(xt, md[:, 1], mw_in, fw["mlstm_w_out"], conv_full, mlstm_head_gain,
                                              gbias, *lnp(1), Bl, S, gather=[packs[1]])
            full[1] = dict(zip(keys[1], unpack_gathered(pack1_all, shards[1], axes[1])))
        else:
            xt, sv1 = attn_mixer_fwd(xt, md[:, 1], fw["attn_w_in"], fw["attn_w_out"], *lnp(1), Bl, S)
        xt, sv2 = ffn_fwd(xt, md[:, 2], fw["ffn_w_in"][1], fw["ffn_w_out"][1], *lnp(2), "f%db" % layer)
        saved.append((sv0, sv1, sv2))

    dxt, lsum = loss_head(xt, loss_target.reshape(T, D), "loss")
    loss = lax.psum(lsum[0, 0], MESH_AXES)

    dmod, dlg_all, dlb_all = [None] * DEPTH, [None] * DEPTH, [None] * DEPTH
    gpacks, recvs = [None] * DEPTH, [None] * DEPTH
    ml_small = None
    for layer in reversed(range(DEPTH)):
        md = mod_mine[layer]
        fw = full[layer]
        sv0, sv1, sv2 = saved[layer]
        dxt, dwi2, dwo2, dlg2, dlb2, dm2 = ffn_bwd(dxt, sv2, md[:, 2], fw["ffn_w_in"][1], fw["ffn_w_out"][1],
                                                   lng_full[layer, 2].reshape(1, D), "f%db" % layer)
        lg1 = lng_full[layer, 1].reshape(1, D)
        if layer % 2 == 0:
            dxt, gwi, gwo, dlg1, dlb1, dm1, ml_small, recvs[1] = mlstm_bwd(
                dxt, sv1, md[:, 1], mw_in, fw["mlstm_w_out"], conv_full, mlstm_head_gain, gbias, lg1, Bl, S,
                exchange=gpacks[1].astype(BF16))
        else:
            dxt, gwi, gwo, dlg1, dlb1, dm1 = attn_mixer_bwd(dxt, sv1, md[:, 1], fw["attn_w_in"], fw["attn_w_out"],
                                                            lg1, Bl, S)
        dxt, dwi0, dwo0, dlg0, dlb0, dm0 = ffn_bwd(dxt, sv0, md[:, 0], fw["ffn_w_in"][0], fw["ffn_w_out"][0],
                                                   lng_full[layer, 0].reshape(1, D), "f%da" % layer)
        gpacks[layer] = pack_grads([jnp.stack([dwi0, dwi2]), jnp.stack([dwo0, dwo2]), gwi, gwo],
                                   shards[layer], axes[layer])
        dmod[layer] = jnp.stack([dm0, dm1, dm2], axis=1).reshape(Bl, 9 * D)
        dlg_all[layer] = jnp.concatenate([dlg0, dlg1, dlg2], axis=0)
        dlb_all[layer] = jnp.concatenate([dlb0, dlb1, dlb2], axis=0)
    grad_x = dxt.reshape(Bl, S, D)
    recvs[0] = exchange_shards(gpacks[0].astype(BF16), "rs_grads")

    gshard = {}
    for layer in range(DEPTH):
        own = lax.dynamic_index_in_dim(gpacks[layer], me, axis=0, keepdims=False)
        gsum = shard_sum(own, recvs[layer], onehot, "rs_sum%d" % layer)
        for k, g in zip(keys[layer], unpack_shards(gsum, shards[layer])):
            gshard.setdefault(k, []).append(g)
    grads = {k: jnp.stack(v) for k, v in gshard.items()}

    dconv, dgain, dgbias = ml_small
    parts = [jnp.stack(dmod).reshape(-1), dgbias.reshape(-1), dgain.reshape(-1),
             jnp.stack(dlg_all).reshape(-1), jnp.stack(dlb_all).reshape(-1), dconv.reshape(-1)]
    sizes = [p.shape[0] for p in parts]
    flat = jnp.concatenate(parts)
    flat = jnp.pad(flat, (0, -flat.shape[0] % (8 * PACK_COLS))).reshape(-1, PACK_COLS)
    (sm_all,) = all_gather([flat], "ag_small")
    sm_sum = sum_leading(sm_all, "small_sum").reshape(-1)
    dmod_all = sm_all.reshape(N_DEV, -1)[:, :sizes[0]].reshape(N_DEV, DEPTH, Bl, 9 * D)
    dmod_all = jnp.moveaxis(dmod_all, 0, 1).reshape(DEPTH, Bg, 9 * D)
    o0 = sizes[0]
    grads["mlstm_gate_bias"] = sm_sum[o0:o0 + sizes[1]].reshape(mlstm_gate_bias.shape)
    o0 += sizes[1]
    grads["mlstm_head_gain"] = sm_sum[o0:o0 + sizes[2]].reshape(mlstm_head_gain.shape)
    o0 += sizes[2]
    nl = ln_g.shape[2]
    g_lng = sm_sum[o0:o0 + sizes[3]].reshape(DEPTH, 3, D)
    o0 += sizes[3]
    g_lnb = sm_sum[o0:o0 + sizes[4]].reshape(DEPTH, 3, D)
    o0 += sizes[4]
    g_conv = sm_sum[o0:o0 + sizes[5]].reshape(1, 4, D)
    grads["ln_g"] = lax.dynamic_slice_in_dim(g_lng, me * nl, nl, axis=2)
    grads["ln_b"] = lax.dynamic_slice_in_dim(g_lnb, me * nl, nl, axis=2)
    grads["mlstm_conv_w"] = lax.dynamic_slice_in_dim(g_conv, me * nl, nl, axis=2)
    dmod_cols = lax.dynamic_slice_in_dim(dmod_all, me * ncols, ncols, axis=2)
    gw, gb = ada_bwd(c_all.T, dmod_cols, dmod_all, "ada_bwd")
    grads["ada_w"] = gw
    grads["ada_b"] = gb.reshape(ada_b.shape)

    names = ["ada_w", "ada_b", "ln_g", "ln_b", "ffn_w_in", "ffn_w_out", "mlstm_w_in", "mlstm_gate_bias",
             "mlstm_conv_w", "mlstm_head_gain", "mlstm_w_out", "attn_w_in", "attn_w_out"]
    deltas, new_m, new_v = [], [], []
    for k in names:
        w = weights[k]
        shp2 = (math.prod(w.shape[:-1]), w.shape[-1])
        d_, m_, v_ = adamw(w.reshape(shp2), grads[k].reshape(shp2), m_in[k].reshape(shp2), v_in[k].reshape(shp2),
                           "adamw_" + k)
        deltas.append(d_.reshape(w.shape))
        new_m.append(m_.reshape(w.shape))
        new_v.append(v_.reshape(w.shape))
    return (loss, grad_x, *[grads[k] for k in names], *deltas, *new_m, *new_v)
```

```python
import functools
import math

import jax
import jax.numpy as jnp
from jax import lax
from jax.experimental import pallas as pl
from jax.experimental.pallas import tpu as pltpu

F32 = jnp.float32
BF16 = jnp.bfloat16

N_DEV = 8
MESH_AXES = ("x", "y", "c")
D = 1024
DEPTH = 2
D_FF = 2816
HEADS = 8
M_DQK = 64
M_DV = 128
M_CHUNK = 64
M_SLAB = 512
M_PROJ = 3088
M_PROJ_PAD = 3200
A_PROJ = 9216
DIL_GROUPS = ((128, 1), (512, 4), (2048, 16))
A_BLK = 128
ALPHA = (2 * DEPTH) ** 0.25
LN_EPS = 1e-5
RMS_EPS = 1e-6
ADAM_LR = 0.001
ADAM_B1 = 0.9
ADAM_B2 = 0.999
ADAM_EPS = 1e-08
ADAM_WD = 0.01
ADAM_STEP = 10
NEG = -1e30
V7X_VMEM_LIMIT = 56 * 1024 * 1024
PACK_COLS = 1024
MESH_ID = pl.DeviceIdType.MESH
ANY_SPEC = pl.BlockSpec(memory_space=pl.ANY)


def _cp(n_axes):
    return pltpu.CompilerParams(dimension_semantics=("arbitrary",) * n_axes,
                                vmem_limit_bytes=V7X_VMEM_LIMIT)


def _dot(a, b):
    return jnp.dot(a, b, preferred_element_type=F32)


def _dot_nt(a, b):
    return lax.dot_general(a, b, (((1,), (1,)), ((), ())), preferred_element_type=F32)


def _dot_tn(a, b):
    return lax.dot_general(a, b, (((0,), (0,)), ((), ())), preferred_element_type=F32)


def _sum0(a):
    return jnp.sum(a, axis=0, keepdims=True)


def _sum1(a):
    return jnp.sum(a, axis=1, keepdims=True)


def _round(a):
    return a.astype(BF16).astype(F32)


def _sigmoid(a):
    return 1.0 / (1.0 + jnp.exp(-a))


def _tile(n, pref):
    t = min(n, pref)
    while n % t:
        t //= 2
    return t


def all_gather(arrs, name):
    n = len(arrs)

    def body(*refs):
        gather = Gather(refs[:n], refs[n:2 * n], *refs[2 * n:])
        gather.start()
        gather.finish()

    return pl.pallas_call(
        body, name=name, out_shape=Gather.out_shape(arrs),
        in_specs=[ANY_SPEC] * n, out_specs=[ANY_SPEC] * n, scratch_shapes=Gather.scratch(n),
    )(*arrs)


class Gather:
    def __init__(self, ins, outs, send_sems, recv_sems, local_sems):
        x, y, c = lax.axis_index("x"), lax.axis_index("y"), lax.axis_index("c")
        me, sibling = (x, y, c), (x, y, 1 - c)
        chips = [(1 - x, y), (x, 1 - y), (1 - x, 1 - y)]

        def slot(a, p):
            return outs[a].at[4 * p[0] + 2 * p[1] + p[2]]

        def copy(a, k, block, to, src=None):
            return pltpu.make_async_remote_copy(
                src_ref=slot(a, block) if src is None else src, dst_ref=slot(a, block),
                send_sem=send_sems.at[7 * a + k], recv_sem=recv_sems.at[7 * a + k],
                device_id=to, device_id_type=MESH_ID)

        n = len(ins)
        self.mine = [pltpu.make_async_copy(ins[a], slot(a, me), local_sems.at[a]) for a in range(n)]
        self.first, self.over_ici, self.passed, self.from_sibling = [], [], [], []
        for a in range(n):
            self.first.append(copy(a, 0, me, sibling, src=ins[a]))
            self.from_sibling.append(copy(a, 0, sibling, me))
            for j, chip in enumerate(chips):
                self.first.append(copy(a, 1 + j, me, (*chip, c), src=ins[a]))
                self.over_ici.append(copy(a, 1 + j, (*chip, c), me))
                self.passed.append(copy(a, 4 + j, (*chip, c), sibling))
                self.from_sibling.append(copy(a, 4 + j, (*chip, 1 - c), me))

    @staticmethod
    def out_shape(arrs):
        return [jax.ShapeDtypeStruct((N_DEV,) + a.shape, a.dtype) for a in arrs]

    @staticmethod
    def scratch(n):
        return [pltpu.SemaphoreType.DMA((7 * n,)), pltpu.SemaphoreType.DMA((7 * n,)),
                pltpu.SemaphoreType.DMA((n,))]

    def start(self):
        for cp in self.mine + self.first:
            cp.start()

    def finish(self):
        for landed, onward in zip(self.over_ici, self.passed):
            landed.wait_recv()
            onward.start()
        for cp in self.from_sibling:
            cp.wait_recv()
        for cp in self.first + self.passed:
            cp.wait_send()
        for cp in self.mine:
            cp.wait()


class Exchange:
    def __init__(self, sends, recvs, send_sems, recv_sems, local_sems):
        x, y, c = lax.axis_index("x"), lax.axis_index("y"), lax.axis_index("c")
        me = 4 * x + 2 * y + c
        self.own = [pltpu.make_async_copy(s.at[me], r.at[me], local_sems.at[a])
                    for a, (s, r) in enumerate(zip(sends, recvs))]
        self.copies = []
        for a, (s_ref, r_ref) in enumerate(zip(sends, recvs)):
            for k in range(1, N_DEV):
                px = 1 - x if (k >> 2) & 1 else x
                py = 1 - y if (k >> 1) & 1 else y
                pc = 1 - c if k & 1 else c
                self.copies.append(pltpu.make_async_remote_copy(
                    src_ref=s_ref.at[4 * px + 2 * py + pc], dst_ref=r_ref.at[me],
                    send_sem=send_sems.at[7 * a + k - 1], recv_sem=recv_sems.at[7 * a + k - 1],
                    device_id=(px, py, pc), device_id_type=MESH_ID))

    @staticmethod
    def scratch(n):
        return [pltpu.SemaphoreType.DMA((7 * n,)), pltpu.SemaphoreType.DMA((7 * n,)),
                pltpu.SemaphoreType.DMA((n,))]

    def start(self):
        for cp in self.own + self.copies:
            cp.start()

    def finish(self):
        for cp in self.copies:
            cp.wait_send()
            cp.wait_recv()
        for cp in self.own:
            cp.wait()


def exchange_shards(sends, name):
    n = len(sends)

    def body(*refs):
        exchange = Exchange(refs[:n], refs[n:2 * n], *refs[2 * n:])
        exchange.start()
        exchange.finish()

    return pl.pallas_call(
        body, name=name, out_shape=[jax.ShapeDtypeStruct(s.shape, s.dtype) for s in sends],
        in_specs=[ANY_SPEC] * n, out_specs=[ANY_SPEC] * n, scratch_shapes=Exchange.scratch(n),
    )(*sends)


def shard_sum(own, recv, onehot, name):
    R, C = own.shape
    tr = _tile(R, 512)

    def body(oh_ref, own_ref, recv_ref, o_ref):
        acc = None
        for j in range(N_DEV):
            term = jnp.where(oh_ref[j] > 0.5, own_ref[...], recv_ref[j].astype(F32))
            acc = term if acc is None else acc + term
        o_ref[...] = acc

    return pl.pallas_call(
        body, name=name, grid=(R // tr,),
        in_specs=[pl.BlockSpec(memory_space=pltpu.SMEM),
                  pl.BlockSpec((tr, C), lambda i: (i, 0)),
                  pl.BlockSpec((N_DEV, tr, C), lambda i: (0, i, 0))],
        out_specs=pl.BlockSpec((tr, C), lambda i: (i, 0)),
        out_shape=jax.ShapeDtypeStruct((R, C), F32), compiler_params=_cp(1),
    )(onehot, own, recv)


def sum_leading(a, name):
    _, R, C = a.shape

    def body(a_ref, o_ref):
        acc = a_ref[0]
        for j in range(1, N_DEV):
            acc = acc + a_ref[j]
        o_ref[...] = acc

    return pl.pallas_call(body, name=name, out_shape=jax.ShapeDtypeStruct((R, C), F32),
                          compiler_params=_cp(0))(a)


def _col_chunks(w, tn):
    if w.ndim == 3:
        return w.shape[0], w.shape[2], pl.BlockSpec((None, w.shape[1], w.shape[2]), lambda i, j: (j, 0, 0))
    return w.shape[1] // tn, tn, pl.BlockSpec((w.shape[0], tn), lambda i, j: (0, j))


def modmm(x, mod3, w, out_dtype, name, tn=None):
    T, Dm = x.shape
    nj, tn, w_spec = _col_chunks(w, tn)
    N = nj * tn
    Bl = mod3.shape[0]
    tm = _tile(T // Bl, 1024)
    tpb = T // Bl // tm

    def body(x_ref, mod_ref, w_ref, o_ref, h_ref, hs):
        @pl.when(pl.program_id(1) == 0)
        def _():
            m = mod_ref[0]
            hs[...] = (x_ref[...] * (1.0 + m[1:2, :]) + m[0:1, :]).astype(BF16)
            h_ref[...] = hs[...]
        o_ref[...] = _dot(hs[...], w_ref[...]).astype(o_ref.dtype)

    return pl.pallas_call(
        body, name=name, grid=(T // tm, nj),
        in_specs=[pl.BlockSpec((tm, Dm), lambda i, j: (i, 0)),
                  pl.BlockSpec((1, 3, Dm), lambda i, j: (i // tpb, 0, 0)), w_spec],
        out_specs=[pl.BlockSpec((tm, tn), lambda i, j: (i, j)),
                   pl.BlockSpec((tm, Dm), lambda i, j: (i, 0))],
        out_shape=[jax.ShapeDtypeStruct((T, N), out_dtype), jax.ShapeDtypeStruct((T, Dm), BF16)],
        scratch_shapes=[pltpu.VMEM((tm, Dm), BF16)], compiler_params=_cp(2),
    )(x, mod3, w)


def modmm_bwd(dp, w, x, mod3, dxres, name, tn=None, tm=512):
    T, Dm = x.shape
    Bl = mod3.shape[0]
    tm = _tile(T // Bl, tm)
    tpb = T // Bl // tm
    resident = dp.ndim == 3
    if resident:
        nc, nj = dp.shape[0], 1
        dp_spec = pl.BlockSpec((nc, tm, dp.shape[2]), lambda i, j: (0, i, 0))
        w_spec = pl.BlockSpec(w.shape, lambda i, j: (0, 0, 0))
    else:
        nj, tn, w_spec = _col_chunks(w, tn)
        dp_spec = pl.BlockSpec((tm, tn), lambda i, j: (i, j))

    def body(dp_ref, w_ref, x_ref, mod_ref, dxr_ref, dx_ref, dsh_ref, dsc_ref, acc):
        i, j = pl.program_id(0), pl.program_id(1)

        @pl.when(j == 0)
        def _():
            acc[...] = jnp.zeros_like(acc)
        if resident:
            for c in range(nc):
                acc[...] += _dot_nt(dp_ref[c], w_ref[c])
        else:
            acc[...] += _dot_nt(dp_ref[...], w_ref[...])

        @pl.when(j == nj - 1)
        def _():
            dh = acc[...]
            xx = x_ref[...]
            dx_ref[...] = dxr_ref[...] + dh * (1.0 + mod_ref[0][1:2, :])

            @pl.when(i % tpb == 0)
            def _():
                dsh_ref[...] = jnp.zeros_like(dsh_ref)
                dsc_ref[...] = jnp.zeros_like(dsc_ref)
            dsh_ref[0] += _sum0(dh)
            dsc_ref[0] += _sum0(dh * xx)

    return pl.pallas_call(
        body, name=name, grid=(T // tm, nj),
        in_specs=[dp_spec, w_spec,
                  pl.BlockSpec((tm, Dm), lambda i, j: (i, 0)),
                  pl.BlockSpec((1, 3, Dm), lambda i, j: (i // tpb, 0, 0)),
                  pl.BlockSpec((tm, Dm), lambda i, j: (i, 0))],
        out_specs=[pl.BlockSpec((tm, Dm), lambda i, j: (i, 0)),
                   pl.BlockSpec((1, 1, Dm), lambda i, j: (i // tpb, 0, 0)),
                   pl.BlockSpec((1, 1, Dm), lambda i, j: (i // tpb, 0, 0))],
        out_shape=[jax.ShapeDtypeStruct((T, Dm), F32), jax.ShapeDtypeStruct((Bl, 1, Dm), F32),
                   jax.ShapeDtypeStruct((Bl, 1, Dm), F32)],
        scratch_shapes=[pltpu.VMEM((tm, Dm), F32)], compiler_params=_cp(2),
    )(dp, w, x, mod3, dxres)


def _ln_stats(z):
    mu = jnp.mean(z, axis=-1, keepdims=True)
    zc = z - mu
    var = jnp.mean(zc * zc, axis=-1, keepdims=True)
    rstd = lax.rsqrt(var + LN_EPS)
    return zc * rstd, rstd


def proj_post(a, w, x, mod3, lng, lnb, weight, name):
    nk, T, tk = a.shape
    Dm = w.shape[2]
    Bl = mod3.shape[0]
    tm = _tile(T // Bl, 512)
    tpb = T // Bl // tm

    def body(a_ref, w_ref, x_ref, mod_ref, g_ref, b_ref, out_ref, xn_ref):
        out = _dot(a_ref[0], w_ref[0])
        for k in range(1, nk):
            out = out + _dot(a_ref[k], w_ref[k])
        out_ref[...] = out
        z = ALPHA * x_ref[...] + (weight * (1.0 + mod_ref[0][2:3, :])) * out
        xhat, _ = _ln_stats(z)
        xn_ref[...] = xhat * g_ref[...] + b_ref[...]

    row = pl.BlockSpec((tm, Dm), lambda i: (i, 0))
    vec = pl.BlockSpec((1, Dm), lambda i: (0, 0))
    return pl.pallas_call(
        body, name=name, grid=(T // tm,),
        in_specs=[pl.BlockSpec((nk, tm, tk), lambda i: (0, i, 0)),
                  pl.BlockSpec((nk, tk, Dm), lambda i: (0, 0, 0)),
                  row, pl.BlockSpec((1, 3, Dm), lambda i: (i // tpb, 0, 0)), vec, vec],
        out_specs=[row, row],
        out_shape=[jax.ShapeDtypeStruct((T, Dm), F32), jax.ShapeDtypeStruct((T, Dm), F32)],
        compiler_params=_cp(1),
    )(a, w, x, mod3, lng, lnb)


def post_bwd(dxn, x, out, mod3, lng, w, weight, name, tm=512, gu=None):
    T, Dm = x.shape
    nk, tk, _ = w.shape
    Bl = mod3.shape[0]
    tm = _tile(T // Bl, tm)
    tpb = T // Bl // tm
    fused = gu is not None

    def body(dxn_ref, x_ref, out_ref, mod_ref, g_ref, w_ref, *rest):
        if fused:
            gg_ref, uu_ref = rest[:2]
            rest = rest[2:]
        dxr_ref, dout_ref, da_ref, dg_ref, db_ref, dgate_ref = rest
        i = pl.program_id(0)
        out = out_ref[...]
        dxn = dxn_ref[...]
        coef = weight * (1.0 + mod_ref[0][2:3, :])
        xhat, rstd = _ln_stats(ALPHA * x_ref[...] + coef * out)
        dyh = dxn * g_ref[...]
        dz = rstd * (dyh - jnp.mean(dyh, axis=-1, keepdims=True)
                     - xhat * jnp.mean(dyh * xhat, axis=-1, keepdims=True))
        dxr_ref[...] = ALPHA * dz
        dout = (coef * dz).astype(BF16)
        dout_ref[...] = dout

        @pl.when(i == 0)
        def _():
            dg_ref[...] = jnp.zeros_like(dg_ref)
            db_ref[...] = jnp.zeros_like(db_ref)

        @pl.when(i % tpb == 0)
        def _():
            dgate_ref[...] = jnp.zeros_like(dgate_ref)
        dg_ref[...] += _sum0(dxn * xhat)
        db_ref[...] += _sum0(dxn)
        dgate_ref[0] += _sum0((weight * out) * dz)
        for k in range(nk):
            da = _dot_nt(dout, w_ref[k])
            if fused:
                gg = gg_ref[k].astype(F32)
                s = _sigmoid(gg)
                da_ref[k] = (da * uu_ref[k].astype(F32) * (s * (1.0 + gg * (1.0 - s)))).astype(BF16)
                da_ref[nk + k] = (da * (gg * s)).astype(BF16)
            else:
                da_ref[k] = da.astype(BF16)

    row = pl.BlockSpec((tm, Dm), lambda i: (i, 0))
    vec = pl.BlockSpec((1, Dm), lambda i: (0, 0))
    wide = pl.BlockSpec((nk, tm, tk), lambda i: (0, i, 0))
    nda = 2 * nk if fused else nk
    return pl.pallas_call(
        body, name=name, grid=(T // tm,),
        in_specs=[row, row, row, pl.BlockSpec((1, 3, Dm), lambda i: (i // tpb, 0, 0)), vec,
                  pl.BlockSpec((nk, tk, Dm), lambda i: (0, 0, 0))] + ([wide, wide] if fused else []),
        out_specs=[row, row, pl.BlockSpec((nda, tm, tk), lambda i: (0, i, 0)),
                   vec, vec, pl.BlockSpec((1, 1, Dm), lambda i: (i // tpb, 0, 0))],
        out_shape=[jax.ShapeDtypeStruct((T, Dm), F32), jax.ShapeDtypeStruct((T, Dm), BF16),
                   jax.ShapeDtypeStruct((nda, T, tk), BF16), jax.ShapeDtypeStruct((1, Dm), F32),
                   jax.ShapeDtypeStruct((1, Dm), F32), jax.ShapeDtypeStruct((Bl, 1, Dm), F32)],
        compiler_params=_cp(1),
    )(dxn, x, out, mod3, lng, w, *(gu if fused else ()))


def mm_tn(a, b, name, bw=None):
    a3, b3 = a.ndim == 3, b.ndim == 3
    nk, T, tk = a.shape if a3 else (1,) + a.shape
    nc, wn = (b.shape[0], b.shape[2]) if b3 else (b.shape[1] // bw, bw)
    tt = _tile(T, 2048)
    nt = T // tt

    def body(a_ref, b_ref, o_ref, ob_ref):
        t = pl.program_id(2)

        @pl.when(t == 0)
        def _():
            o_ref[...] = jnp.zeros_like(o_ref)
        o_ref[...] += _dot_tn(a_ref[...], b_ref[...])

        @pl.when(t == nt - 1)
        def _():
            ob_ref[...] = o_ref[...].astype(BF16)

    a_spec = (pl.BlockSpec((None, tt, tk), lambda k, c, t: (k, t, 0)) if a3
              else pl.BlockSpec((tt, tk), lambda k, c, t: (t, 0)))
    b_spec = (pl.BlockSpec((None, tt, wn), lambda k, c, t: (c, t, 0)) if b3
              else pl.BlockSpec((tt, wn), lambda k, c, t: (t, c)))
    o_spec = pl.BlockSpec((None, tk, wn), lambda k, c, t: (k * nc + c, 0, 0))
    return pl.pallas_call(
        body, name=name, grid=(nk, nc, nt), in_specs=[a_spec, b_spec], out_specs=[o_spec, o_spec],
        out_shape=[jax.ShapeDtypeStruct((nk * nc, tk, wn), F32), jax.ShapeDtypeStruct((nk * nc, tk, wn), BF16)],
        compiler_params=_cp(3),
    )(a, b)


def ffn_in(x, mod3, w, name):
    T, Dm = x.shape
    nj, tf = w.shape[0] // 2, w.shape[2]
    Bl = mod3.shape[0]
    tm = _tile(T // Bl, 512)
    tpb = T // Bl // tm

    def body(x_ref, mod_ref, wg_ref, wu_ref, a_ref, g_ref, u_ref, h_ref):
        m = mod_ref[0]
        h = (x_ref[...] * (1.0 + m[1:2, :]) + m[0:1, :]).astype(BF16)
        h_ref[...] = h
        g = _dot(h, wg_ref[...])
        u = _dot(h, wu_ref[...])
        a_ref[...] = (g * _sigmoid(g) * u).astype(BF16)
        g_ref[...] = g.astype(BF16)
        u_ref[...] = u.astype(BF16)

    col = pl.BlockSpec((None, tm, tf), lambda j, i: (j, i, 0))
    a, g, u, h = pl.pallas_call(
        body, name=name, grid=(nj, T // tm),
        in_specs=[pl.BlockSpec((tm, Dm), lambda j, i: (i, 0)),
                  pl.BlockSpec((1, 3, Dm), lambda j, i: (i // tpb, 0, 0)),
                  pl.BlockSpec((None, Dm, tf), lambda j, i: (j, 0, 0)),
                  pl.BlockSpec((None, Dm, tf), lambda j, i: (nj + j, 0, 0))],
        out_specs=[col, col, col, pl.BlockSpec((None, tm, Dm), lambda j, i: (j, i, 0))],
        out_shape=[jax.ShapeDtypeStruct((nj, T, tf), BF16)] * 3 + [jax.ShapeDtypeStruct((nj, T, Dm), BF16)],
        compiler_params=_cp(2),
    )(x, mod3, w, w)
    return a, g, u, h[0]


def loss_head(y, tgt, name):
    T, Dm = y.shape
    tm = _tile(T, 512)
    nt = T // tm

    def body(y_ref, t_ref, dy_ref, l_ref, acc):
        i = pl.program_id(0)

        @pl.when(i == 0)
        def _():
            acc[...] = jnp.zeros_like(acc)
        e = y_ref[...] - t_ref[...]
        dy_ref[...] = e * (1.0 / Dm)
        acc[...] += _sum0(e * e)

        @pl.when(i == nt - 1)
        def _():
            l_ref[...] = jnp.broadcast_to(_sum1(acc[...]) * (0.5 / Dm), l_ref.shape)

    return pl.pallas_call(
        body, name=name, grid=(nt,),
        in_specs=[pl.BlockSpec((tm, Dm), lambda i: (i, 0)), pl.BlockSpec((tm, Dm), lambda i: (i, 0))],
        out_specs=[pl.BlockSpec((tm, Dm), lambda i: (i, 0)), pl.BlockSpec((1, 128), lambda i: (0, 0))],
        out_shape=[jax.ShapeDtypeStruct((T, Dm), F32), jax.ShapeDtypeStruct((1, 128), F32)],
        scratch_shapes=[pltpu.VMEM((1, Dm), F32)], compiler_params=_cp(1),
    )(y, tgt)


def adamw(w, g, m, v, name):
    R, C = w.shape
    tr = _tile(R, 512) if R % 8 == 0 else R

    def body(w_ref, g_ref, m_ref, v_ref, d_ref, nm_ref, nv_ref):
        gg = g_ref[...]
        mm = ADAM_B1 * m_ref[...] + (1.0 - ADAM_B1) * gg
        vv = ADAM_B2 * v_ref[...] + (1.0 - ADAM_B2) * (gg * gg)
        m_hat = mm / (1.0 - ADAM_B1 ** ADAM_STEP)
        v_hat = vv / (1.0 - ADAM_B2 ** ADAM_STEP)
        d_ref[...] = -ADAM_LR * (m_hat / (jnp.sqrt(v_hat) + ADAM_EPS) + ADAM_WD * w_ref[...])
        nm_ref[...] = mm
        nv_ref[...] = vv

    spec = pl.BlockSpec((tr, C), lambda i: (i, 0))
    return pl.pallas_call(
        body, name=name, grid=(R // tr,), in_specs=[spec] * 4, out_specs=[spec] * 3,
        out_shape=[jax.ShapeDtypeStruct((R, C), F32)] * 3, compiler_params=_cp(1),
    )(w, g, m, v)


def ada_fwd(c_all, ada_w, ada_b_cols, name):
    Lr, Dm, Nc = ada_w.shape
    Bg = c_all.shape[0]

    def body(c_ref, w_ref, b_ref, o_ref):
        cc = c_ref[...]
        cond = cc * _sigmoid(cc)
        o_ref[0] = _dot(cond.astype(BF16), w_ref[0].astype(BF16)) + b_ref[0]

    return pl.pallas_call(
        body, name=name, grid=(Lr,),
        in_specs=[pl.BlockSpec((Bg, Dm), lambda l: (0, 0)),
                  pl.BlockSpec((1, Dm, Nc), lambda l: (l, 0, 0)),
                  pl.BlockSpec((1, 1, Nc), lambda l: (l, 0, 0))],
        out_specs=pl.BlockSpec((1, Bg, Nc), lambda l: (l, 0, 0)),
        out_shape=jax.ShapeDtypeStruct((Lr, Bg, Nc), F32), compiler_params=_cp(1),
    )(c_all, ada_w, ada_b_cols)


def ada_bwd(c_all_t, dmod_cols, dmod_all, name):
    Dm, Bg = c_all_t.shape
    Lr, _, Nc = dmod_cols.shape
    Nf = dmod_all.shape[2]

    def body(c_ref, dm_ref, da_ref, gw_ref, gb_ref):
        cc = c_ref[...]
        cond = cc * _sigmoid(cc)
        gw_ref[0] = _dot(cond.astype(BF16), dm_ref[0].astype(BF16))
        gb_ref[0] = _sum0(da_ref[0])

    return pl.pallas_call(
        body, name=name, grid=(Lr,),
        in_specs=[pl.BlockSpec((Dm, Bg), lambda l: (0, 0)),
                  pl.BlockSpec((1, Bg, Nc), lambda l: (l, 0, 0)),
                  pl.BlockSpec((1, Bg, Nf), lambda l: (l, 0, 0))],
        out_specs=[pl.BlockSpec((1, Dm, Nc), lambda l: (l, 0, 0)),
                   pl.BlockSpec((1, 1, Nf), lambda l: (l, 0, 0))],
        out_shape=[jax.ShapeDtypeStruct((Lr, Dm, Nc), F32), jax.ShapeDtypeStruct((Lr, 1, Nf), F32)],
        compiler_params=_cp(1),
    )(c_all_t, dmod_cols, dmod_all)


def _conv_taps(x, w, rows):
    shifted = [x]
    c = w[3:4, :] * x
    for k in range(1, 4):
        xs = jnp.where(rows >= k, pltpu.roll(x, k, 0), 0.0)
        shifted.append(xs)
        c = c + w[3 - k:4 - k, :] * xs
    return c, shifted


def conv_silu(proj3, conv_w, name):
    Bl, S, _ = proj3.shape
    ncb = conv_w.shape[1] // 128

    def body(x_ref, w_ref, o_ref):
        rows = lax.broadcasted_iota(jnp.int32, (S, 128), 0)
        c, _ = _conv_taps(_round(x_ref[0]), _round(w_ref[...]), rows)
        o_ref[0] = c * _sigmoid(c)

    return pl.pallas_call(
        body, name=name, grid=(Bl, ncb),
        in_specs=[pl.BlockSpec((1, S, 128), lambda b, j: (b, 0, j)),
                  pl.BlockSpec((4, 128), lambda b, j: (0, j))],
        out_specs=pl.BlockSpec((1, S, 128), lambda b, j: (b, 0, j)),
        out_shape=jax.ShapeDtypeStruct((Bl, S, conv_w.shape[1]), F32), compiler_params=_cp(2),
    )(proj3, conv_w)


def conv_silu_bwd(proj3, conv_w, dq, dk, name):
    Bl, S, _ = proj3.shape
    nq = dq.shape[2] // 128

    def body(x_ref, w_ref, dq_ref, dk_ref, dx_ref, dw_ref):
        j = pl.program_id(1)
        rows = lax.broadcasted_iota(jnp.int32, (S, 128), 0)
        w = _round(w_ref[...])
        c, shifted = _conv_taps(_round(x_ref[0]), w, rows)
        s = _sigmoid(c)
        dact = jnp.where(j < nq, dq_ref[0], dk_ref[0])
        dc = _round(dact * (s * (1.0 + c * (1.0 - s))))
        dx = w[3:4, :] * dc
        dws = [_sum0(dc * shifted[0])]
        for k in range(1, 4):
            up = jnp.where(rows < S - k, pltpu.roll(dc, S - k, 0), 0.0)
            dx = dx + w[3 - k:4 - k, :] * up
            dws.append(_sum0(dc * shifted[k]))
        dx_ref[0] = dx.astype(BF16)
        tap = lax.broadcasted_iota(jnp.int32, (4, 128), 0)
        dw_ref[0] = functools.reduce(lambda a, b: a + b, [jnp.where(tap == 3 - k, dws[k], 0.0) for k in range(4)])

    return pl.pallas_call(
        body, name=name, grid=(Bl, 2 * nq),
        in_specs=[pl.BlockSpec((1, S, 128), lambda b, j: (b, 0, j)),
                  pl.BlockSpec((4, 128), lambda b, j: (0, j)),
                  pl.BlockSpec((1, S, 128), lambda b, j: (b, 0, jnp.minimum(j, nq - 1))),
                  pl.BlockSpec((1, S, 128), lambda b, j: (b, 0, jnp.maximum(j - nq, 0)))],
        out_specs=[pl.BlockSpec((1, S, 128), lambda b, j: (b, 0, j)),
                   pl.BlockSpec((1, 4, 128), lambda b, j: (b, 0, j))],
        out_shape=[jax.ShapeDtypeStruct((Bl, S, 2 * nq * 128), BF16),
                   jax.ShapeDtypeStruct((Bl, 4, 2 * nq * 128), F32)],
        compiler_params=_cp(2),
    )(proj3, conv_w, dq, dk)


def _log_sigmoid(a):
    return jnp.minimum(a, 0.0) - jnp.log(1.0 + jnp.exp(-jnp.abs(a)))


def _chunk_state(kc, vc, gi, bcum, b_last, C, n, m):
    a = b_last - bcum + gi
    m_loc = jnp.max(a, axis=0, keepdims=True)
    wa = jnp.exp(a - m_loc)
    c_loc = _dot_tn((wa * vc).astype(BF16), kc.astype(BF16))
    n_loc = _sum0(_round(wa) * _round(kc))
    m_new = jnp.maximum(b_last + m, m_loc)
    sp = jnp.exp(b_last + m - m_new)
    sl = jnp.exp(m_loc - m_new)
    return sp * C + sl * c_loc, sp * n + sl * n_loc, m_new, wa, sp, sl


def _chunk_out(qs, kc, vc, gi_row, bcum, bcum_row, low, C, n, m):
    inter_log = bcum + m
    dlog = jnp.where(low, bcum - bcum_row + gi_row, NEG)
    m_i = jnp.maximum(inter_log, jnp.max(dlog, axis=1, keepdims=True))
    dm = jnp.exp(dlog - m_i)
    iw = jnp.exp(inter_log - m_i)
    qs_b, k_b, v_b = qs.astype(BF16), kc.astype(BF16), vc.astype(BF16)
    sc = _dot_nt(qs_b, k_b) * dm
    qc_ = _dot_nt(qs_b, C.astype(BF16))
    qn = _sum1(_round(qs) * _round(n))
    num = _dot(sc.astype(BF16), v_b) + iw * qc_
    den = _sum1(sc) + iw * qn
    floor = jnp.exp(-m_i)
    dn = jnp.maximum(jnp.abs(den), floor)
    return dict(hc=num / dn, den=den, dn=dn, floor=floor, sc=sc, dm=dm, iw=iw, qc=qc_, qn=qn,
                qs_b=qs_b, k_b=k_b, v_b=v_b)


def _cell_consts(L):
    ri = lax.broadcasted_iota(jnp.int32, (L, L), 0)
    ci = lax.broadcasted_iota(jnp.int32, (L, L), 1)
    return ri == ci, ci <= ri, ri <= ci


def _load_chunk(q_ref, k_ref, v_ref, G, off, L, h, lane):
    hh = h % 2
    qmask = (lane >= M_DQK * hh) & (lane < M_DQK * (hh + 1))
    pair = pl.ds(128 * (h // 2), 128)
    qc = jnp.where(qmask, q_ref[0, pl.ds(off, L), pair], 0.0)
    kc = jnp.where(qmask, k_ref[0, pl.ds(off, L), pair], 0.0)
    vc = v_ref[0, pl.ds(off, L), pl.ds(M_DV * h, M_DV)]
    gi = _sum1(jnp.where(lane == h, G, 0.0))
    gf = _sum1(jnp.where(lane == h + HEADS, G, 0.0))
    return qmask, qc, kc, vc, gi, gf


def _gate_rows(gi, gf, eye, low, upp):
    lf = _log_sigmoid(gf)
    lf_row = _sum0(jnp.where(eye, lf, 0.0))
    gi_row = _sum0(jnp.where(eye, gi, 0.0))
    bcum = _sum1(jnp.where(low, lf_row, 0.0))
    bcum_row = _sum0(jnp.where(upp, lf, 0.0))
    b_last = _sum0(lf)
    return gi_row, bcum, bcum_row, b_last


def _cell_specs(SB, cpb, blk):
    def seq(width, col):
        return pl.BlockSpec((1, SB, width), lambda b, s: (b, blk(s), col))

    def state(rows):
        return pl.BlockSpec((1, HEADS, cpb, rows, 128), lambda b, s: (b, 0, blk(s), 0, 0))

    ins = [seq(D // 2, 0), seq(D // 2, 1), seq(D, 1), seq(D, 2), seq(128, 3 * D // 128),
           pl.BlockSpec((1, D), lambda b, s: (0, 0)), pl.BlockSpec((1, 128), lambda b, s: (0, 0))]
    return ins, [state(M_DV), state(1), state(1)], seq


def mlstm_cell_fwd(qk3, proj3, gain, gbias, name, gather=()):
    Bl, S, _ = qk3.shape
    L = M_CHUNK
    SB = min(M_SLAB, S)
    cpb, nc, nsb = SB // L, S // L, S // SB
    scale = M_DQK ** -0.5
    ng = len(gather)

    def body(q_ref, k_ref, v_ref, o_ref, g_ref, gain_ref, gb_ref, *rest):
        ag_in, (y_ref, cst_ref, nst_ref, mst_ref), rest = rest[:ng], rest[ng:ng + 4], rest[ng + 4:]
        ag_out, (C_s, n_s, m_s), ag_sems = rest[:ng], rest[ng:ng + 3], rest[ng + 3:]
        b, sl = pl.program_id(0), pl.program_id(1)
        if ng:
            @pl.when((b == 0) & (sl == 0))
            def _():
                Gather(ag_in, ag_out, *ag_sems).start()

        @pl.when(sl == 0)
        def _():
            C_s[...] = jnp.zeros_like(C_s)
            n_s[...] = jnp.zeros_like(n_s)
            m_s[...] = jnp.zeros_like(m_s)
        lane = lax.broadcasted_iota(jnp.int32, (L, 128), 1)
        eye, low, upp = _cell_consts(L)

        def step(c, carry):
            off = pl.multiple_of(c * L, L)
            G = g_ref[0, pl.ds(off, L), :] + gb_ref[...]
            for h in range(HEADS):
                C, n, mb = C_s[h], n_s[h], m_s[h]
                cst_ref[0, h, c] = C
                nst_ref[0, h, c] = n
                mst_ref[0, h, c] = mb
                m = mb[:, 0:1]
                _, qc, kc, vc, gi, gf = _load_chunk(q_ref, k_ref, v_ref, G, off, L, h, lane)
                gi_row, bcum, bcum_row, b_last = _gate_rows(gi, gf, eye, low, upp)
                r = _chunk_out(qc * scale, kc, vc, gi_row, bcum, bcum_row, low, C, n, m)
                hc = r["hc"]
                hn = hc * lax.rsqrt(jnp.mean(hc * hc, axis=-1, keepdims=True) + RMS_EPS)
                cols = pl.ds(M_DV * h, M_DV)
                oc = o_ref[0, pl.ds(off, L), cols]
                y_ref[0, pl.ds(off, L), cols] = (_sigmoid(oc) * hn * gain_ref[:, cols]).astype(BF16)
                C2, n2, m2, _, _, _ = _chunk_state(kc, vc, gi, bcum, b_last, C, n, m)
                C_s[h] = C2
                n_s[h] = n2
                m_s[h] = jnp.broadcast_to(m2, (1, 128))
            return carry

        lax.fori_loop(0, cpb, step, 0)
        if ng:
            @pl.when((b == Bl - 1) & (sl == nsb - 1))
            def _():
                Gather(ag_in, ag_out, *ag_sems).finish()

    ins, states, seq = _cell_specs(SB, cpb, lambda s: s)
    return pl.pallas_call(
        body, name=name, grid=(Bl, nsb), in_specs=ins + [ANY_SPEC] * ng,
        out_specs=[seq(D, 0)] + states + [ANY_SPEC] * ng,
        out_shape=[jax.ShapeDtypeStruct((Bl, S, D), BF16),
                   jax.ShapeDtypeStruct((Bl, HEADS, nc, M_DV, 128), F32),
                   jax.ShapeDtypeStruct((Bl, HEADS, nc, 1, 128), F32),
                   jax.ShapeDtypeStruct((Bl, HEADS, nc, 1, 128), F32)] + Gather.out_shape(gather),
        scratch_shapes=[pltpu.VMEM((HEADS, M_DV, 128), F32), pltpu.VMEM((HEADS, 1, 128), F32),
                        pltpu.VMEM((HEADS, 1, 128), F32)] + (Gather.scratch(ng) if ng else []),
        compiler_params=_cp(2),
    )(qk3, qk3, proj3, proj3, proj3, gain, gbias, *gather)


def mlstm_cell_bwd(qk3, proj3, gain, gbias, dy3, states, name, exchange=()):
    Bl, S, _ = qk3.shape
    L = M_CHUNK
    SB = min(M_SLAB, S)
    cpb, nsb = SB // L, S // SB
    scale = M_DQK ** -0.5
    nx = len(exchange)

    def body(q_ref, k_ref, v_ref, o_ref, g_ref, gain_ref, gb_ref, cst_ref, nst_ref, mst_ref, dy_ref, *rest):
        x_in, rest = rest[:nx], rest[nx:]
        (dq_ref, dk_ref, dv_ref, do_ref, dg_ref, dgain_ref, dgb_ref), rest = rest[:7], rest[7:]
        x_out, (dC_s, dn_s, dgain_s, dgb_s), x_sems = rest[:nx], rest[nx:nx + 4], rest[nx + 4:]
        b, s = pl.program_id(0), pl.program_id(1)
        if nx:
            @pl.when((b == 0) & (s == 0))
            def _():
                Exchange(x_in, x_out, *x_sems).start()

        @pl.when(s == 0)
        def _():
            dC_s[...] = jnp.zeros_like(dC_s)
            dn_s[...] = jnp.zeros_like(dn_s)
            dgain_s[...] = jnp.zeros_like(dgain_s)
            dgb_s[...] = jnp.zeros_like(dgb_s)
        lane = lax.broadcasted_iota(jnp.int32, (L, 128), 1)
        rowi = lax.broadcasted_iota(jnp.int32, (L, 1), 0)
        eye, low, upp = _cell_consts(L)

        def bstep(t, carry):
            c = cpb - 1 - t
            off = pl.multiple_of(c * L, L)
            G = g_ref[0, pl.ds(off, L), :] + gb_ref[...]
            slab = jnp.zeros((L, 128), F32)
            dq_pair = dk_pair = None
            for h in range(HEADS):
                cols = pl.ds(M_DV * h, M_DV)
                gain_h = gain_ref[:, cols]
                C, n, m = cst_ref[0, h, c], nst_ref[0, h, c], mst_ref[0, h, c][:, 0:1]
                dC_n, dn_n = dC_s[h], dn_s[h]
                qmask, qc, kc, vc, gi, gf = _load_chunk(q_ref, k_ref, v_ref, G, off, L, h, lane)
                gi_row, bcum, bcum_row, b_last = _gate_rows(gi, gf, eye, low, upp)
                qs = qc * scale
                r = _chunk_out(qs, kc, vc, gi_row, bcum, bcum_row, low, C, n, m)
                _, _, _, wa, sp, sl = _chunk_state(kc, vc, gi, bcum, b_last, C, n, m)
                hc, den, dn, sc, dm, iw, qn = r["hc"], r["den"], r["dn"], r["sc"], r["dm"], r["iw"], r["qn"]
                qs_b, k_b, v_b = r["qs_b"], r["k_b"], r["v_b"]
                dy = dy_ref[0, pl.ds(off, L), cols].astype(F32)
                oc = o_ref[0, pl.ds(off, L), cols]
                sig_o = _sigmoid(oc)
                rr = lax.rsqrt(jnp.mean(hc * hc, axis=-1, keepdims=True) + RMS_EPS)
                hn = hc * rr
                dgain_s[:, cols] += _sum0(dy * sig_o * hn)
                do_ref[0, pl.ds(off, L), cols] = (
                    dy * hn * gain_h * sig_o * (1.0 - sig_o)).astype(BF16)
                dhn = dy * sig_o * gain_h
                dhc = rr * dhn - hc * (rr * rr * rr) * jnp.mean(dhn * hc, axis=-1, keepdims=True)
                dnum = dhc / dn
                gden = -_sum1(dhc * hc) / dn
                dden = jnp.where(jnp.abs(den) > r["floor"], gden * jnp.sign(den), 0.0)
                dnum_b = dnum.astype(BF16)
                dsc = _dot_nt(dnum_b, v_b) + dden
                dv = _dot_tn(sc.astype(BF16), dnum_b)
                diw = _sum1(dnum * r["qc"]) + dden * qn
                dqc_b = (iw * dnum).astype(BF16)
                wq = iw * dden
                dqs = _dot(dqc_b, C.astype(BF16)) + wq * n
                dC_out = _dot_tn(dqc_b, qs_b)
                dn_out = _sum0(wq * qs)
                dS_b = (dsc * dm).astype(BF16)
                gm = dsc * sc
                dqs = dqs + _dot(dS_b, k_b)
                dk = _dot_tn(dS_b, qs_b)
                dbc = _sum1(gm) + diw * iw
                colg = _sum0(gm)
                dC_p = sp * dC_n + dC_out
                dn_p = sp * dn_n + dn_out
                dcl_b = (sl * dC_n).astype(BF16)
                dn_loc = sl * dn_n
                dsp = _sum1(_sum0(dC_n * C)) + _sum1(dn_n * n)
                db_last = dsp * sp
                t1 = _dot(v_b, dcl_b) + dn_loc
                dwa = _sum1(t1 * kc)
                dv = dv + wa * _dot_nt(k_b, dcl_b)
                dk = dk + wa * t1
                da = dwa * wa
                db_last = db_last + _sum0(da)
                dbc = dbc - da + jnp.where(rowi == L - 1, db_last, 0.0)
                dbc_row = _sum0(jnp.where(eye, dbc, 0.0)) - colg
                dgi = da + _sum1(jnp.where(eye, colg, 0.0))
                dlf = _sum1(jnp.where(upp, dbc_row, 0.0))
                dgf = dlf * _sigmoid(-gf)
                dq = jnp.where(qmask, dqs * scale, 0.0)
                dk = jnp.where(qmask, dk, 0.0)
                slab = slab + jnp.where(lane == h, dgi, 0.0) + jnp.where(lane == h + HEADS, dgf, 0.0)
                dv_ref[0, pl.ds(off, L), cols] = dv.astype(BF16)
                dC_s[h] = dC_p
                dn_s[h] = dn_p
                if h % 2 == 0:
                    dq_pair, dk_pair = dq, dk
                else:
                    pair = pl.ds(128 * (h // 2), 128)
                    dq_ref[0, pl.ds(off, L), pair] = dq_pair + dq
                    dk_ref[0, pl.ds(off, L), pair] = dk_pair + dk
            dg_ref[0, pl.ds(off, L), :] = slab
            dgb_s[...] += _sum0(slab)
            return carry

        lax.fori_loop(0, cpb, bstep, 0)

        @pl.when(s == nsb - 1)
        def _():
            dgain_ref[0] = dgain_s[...]
            dgb_ref[0] = dgb_s[...]
        if nx:
            @pl.when((b == Bl - 1) & (s == nsb - 1))
            def _():
                Exchange(x_in, x_out, *x_sems).finish()

    ins, states_specs, seq = _cell_specs(SB, cpb, lambda s: nsb - 1 - s)
    once = lambda width: pl.BlockSpec((1, 1, width), lambda b, s: (b, 0, 0))
    sent = list(exchange)
    return pl.pallas_call(
        body, name=name, grid=(Bl, nsb), in_specs=ins + states_specs + [seq(D, 0)] + [ANY_SPEC] * nx,
        out_specs=[seq(D // 2, 0), seq(D // 2, 0), seq(D, 0), seq(D, 0), seq(128, 0), once(D), once(128)]
        + [ANY_SPEC] * nx,
        out_shape=[jax.ShapeDtypeStruct((Bl, S, D // 2), F32), jax.ShapeDtypeStruct((Bl, S, D // 2), F32),
                   jax.ShapeDtypeStruct((Bl, S, D), BF16), jax.ShapeDtypeStruct((Bl, S, D), BF16),
                   jax.ShapeDtypeStruct((Bl, S, 128), F32), jax.ShapeDtypeStruct((Bl, 1, D), F32),
                   jax.ShapeDtypeStruct((Bl, 1, 128), F32)]
        + [jax.ShapeDtypeStruct(a.shape, a.dtype) for a in sent],
        scratch_shapes=[pltpu.VMEM((HEADS, M_DV, 128), F32), pltpu.VMEM((HEADS, 1, 128), F32),
                        pltpu.VMEM((1, D), F32), pltpu.VMEM((1, 128), F32)] + (Exchange.scratch(nx) if nx else []),
        compiler_params=_cp(2),
    )(qk3, qk3, proj3, proj3, proj3, gain, gbias, *states, dy3, *sent)


def _attn_scores(q, kc, kp, n, row, col, scale):
    s_c = jnp.where(col <= row, _dot_nt(q, kc) * scale, NEG)
    s_p = jnp.where(jnp.logical_and(col >= row, n > 0), _dot_nt(q, kp) * scale, NEG)
    return s_c, s_p


def attn_fwd(proj, Bl, S, g, dil, name):
    Sd = S // dil
    nb = Sd // A_BLK
    scale = A_BLK ** -0.5
    pv = proj.reshape(Bl, Sd, dil * A_PROJ)
    ncol = A_PROJ // 128

    def body(q_ref, k_ref, v_ref, o_ref, l_ref):
        row = lax.broadcasted_iota(jnp.int32, (A_BLK, A_BLK), 0)
        col = lax.broadcasted_iota(jnp.int32, (A_BLK, A_BLK), 1)

        def step(n, carry):
            off = pl.multiple_of(n * A_BLK, A_BLK)
            offp = pl.multiple_of(jnp.maximum(n - 1, 0) * A_BLK, A_BLK)
            q = q_ref[0, pl.ds(off, A_BLK), :]
            s_c, s_p = _attn_scores(q, k_ref[0, pl.ds(off, A_BLK), :], k_ref[0, pl.ds(offp, A_BLK), :],
                                    n, row, col, scale)
            m = jnp.maximum(jnp.max(s_c, axis=1, keepdims=True), jnp.max(s_p, axis=1, keepdims=True))
            p_c = jnp.exp(s_c - m)
            p_p = jnp.exp(s_p - m)
            den = _sum1(p_c) + _sum1(p_p)
            o = _dot(p_c.astype(BF16), v_ref[0, pl.ds(off, A_BLK), :]) + _dot(
                p_p.astype(BF16), v_ref[0, pl.ds(offp, A_BLK), :])
            o_ref[0, pl.ds(off, A_BLK), :] = o / den
            l_ref[0, pl.ds(off, A_BLK), :] = jnp.broadcast_to(m + jnp.log(den), (A_BLK, 128))
            return carry

        lax.fori_loop(0, nb, step, 0)

    def spec(j):
        return pl.BlockSpec((1, Sd, 128), lambda b, r, h: (b, 0, r * ncol + g * 24 + j * HEADS + h))

    ospec = pl.BlockSpec((1, Sd, 128), lambda b, r, h: (b, 0, r * HEADS + h))
    o, lse = pl.pallas_call(
        body, name=name, grid=(Bl, dil, HEADS),
        in_specs=[spec(0), spec(1), spec(2)], out_specs=[ospec, ospec],
        out_shape=[jax.ShapeDtypeStruct((Bl, Sd, dil * D), F32)] * 2, compiler_params=_cp(3),
    )(pv, pv, pv)
    return o.reshape(Bl * S, D), lse.reshape(Bl * S, D)


def attn_merge(os_, lses, name):
    T = os_[0].shape[0]
    tm = _tile(T, 512)
    ng = len(os_)

    def body(*refs):
        o_refs, l_refs = refs[:ng], refs[ng:2 * ng]
        ob_ref, of_ref, lt_ref = refs[2 * ng:]
        ls = [r[...] for r in l_refs]
        m = functools.reduce(jnp.maximum, ls)
        ws = [jnp.exp(l - m) for l in ls]
        den = functools.reduce(lambda a, b: a + b, ws)
        o = functools.reduce(lambda a, b: a + b, [w * r[...] for w, r in zip(ws, o_refs)]) / den
        of_ref[...] = o
        ob_ref[...] = o.astype(BF16)
        lt_ref[...] = m + jnp.log(den)

    spec = pl.BlockSpec((tm, D), lambda i: (i, 0))
    return pl.pallas_call(
        body, name=name, grid=(T // tm,), in_specs=[spec] * (2 * ng), out_specs=[spec] * 3,
        out_shape=[jax.ShapeDtypeStruct((T, D), BF16), jax.ShapeDtypeStruct((T, D), F32),
                   jax.ShapeDtypeStruct((T, D), F32)],
        compiler_params=_cp(1),
    )(*os_, *lses)


def attn_bwd(proj, do, o, lse, Bl, S, g, dil, name):
    Sd = S // dil
    nb = Sd // A_BLK
    scale = A_BLK ** -0.5
    pv = proj.reshape(Bl, Sd, dil * A_PROJ)
    ncol = A_PROJ // 128
    dov, ov, lv = (t.reshape(Bl, Sd, dil * D) for t in (do, o, lse))

    def body(q_ref, k_ref, v_ref, do_ref, o_ref, l_ref, dq_ref, dk_ref, dv_ref, dk_s, dv_s):
        row = lax.broadcasted_iota(jnp.int32, (A_BLK, A_BLK), 0)
        col = lax.broadcasted_iota(jnp.int32, (A_BLK, A_BLK), 1)
        dk_s[...] = jnp.zeros_like(dk_s)
        dv_s[...] = jnp.zeros_like(dv_s)

        def step(n, carry):
            off = pl.multiple_of(n * A_BLK, A_BLK)
            offp = pl.multiple_of(jnp.maximum(n - 1, 0) * A_BLK, A_BLK)
            q = q_ref[0, pl.ds(off, A_BLK), :]
            kc, kp = k_ref[0, pl.ds(off, A_BLK), :], k_ref[0, pl.ds(offp, A_BLK), :]
            vc, vp = v_ref[0, pl.ds(off, A_BLK), :], v_ref[0, pl.ds(offp, A_BLK), :]
            do_b = do_ref[0, pl.ds(off, A_BLK), :]
            delta = _sum1(do_b.astype(F32) * o_ref[0, pl.ds(off, A_BLK), :])
            lt = l_ref[0, pl.ds(off, A_BLK), :][:, 0:1]
            s_c, s_p = _attn_scores(q, kc, kp, n, row, col, scale)
            p_c = jnp.exp(s_c - lt)
            p_p = jnp.exp(s_p - lt)
            ds_c = (p_c * (_dot_nt(do_b, vc) - delta) * scale).astype(BF16)
            ds_p = (p_p * (_dot_nt(do_b, vp) - delta) * scale).astype(BF16)
            dq_ref[0, pl.ds(off, A_BLK), :] = (_dot(ds_c, kc) + _dot(ds_p, kp)).astype(BF16)
            dk_s[pl.ds(off, A_BLK), :] += _dot_tn(ds_c, q)
            dk_s[pl.ds(offp, A_BLK), :] += _dot_tn(ds_p, q)
            dv_s[pl.ds(off, A_BLK), :] += _dot_tn(p_c.astype(BF16), do_b)
            dv_s[pl.ds(offp, A_BLK), :] += _dot_tn(p_p.astype(BF16), do_b)
            return carry

        lax.fori_loop(0, nb, step, 0)
        dk_ref[0] = dk_s[...].astype(BF16)
        dv_ref[0] = dv_s[...].astype(BF16)

    def spec(j):
        return pl.BlockSpec((1, Sd, 128), lambda b, r, h: (b, 0, r * ncol + g * 24 + j * HEADS + h))

    ospec = pl.BlockSpec((1, Sd, 128), lambda b, r, h: (b, 0, r * HEADS + h))
    outs = pl.pallas_call(
        body, name=name, grid=(Bl, dil, HEADS),
        in_specs=[spec(0), spec(1), spec(2), ospec, ospec, ospec], out_specs=[ospec] * 3,
        out_shape=[jax.ShapeDtypeStruct((Bl, Sd, dil * D), BF16)] * 3,
        scratch_shapes=[pltpu.VMEM((Sd, 128), F32), pltpu.VMEM((Sd, 128), F32)],
        compiler_params=_cp(3),
    )(pv, pv, pv, dov, ov, lv)
    return [t.reshape(Bl * S, D) for t in outs]


def _as_slots(pair, shape):
    return tuple(t.reshape(shape) for t in pair)


def ffn_fwd(x, mod3, w_in, w_out, lng, lnb, tag):
    a, g, u, h = ffn_in(x, mod3, w_in, tag + "_in")
    out, xn = proj_post(a, w_out, x, mod3, lng, lnb, 0.5, tag + "_out")
    return xn, (x, out, g, u, h, a)


def ffn_bwd(dxn, saved, mod3, w_in, w_out, lng, tag):
    x, out, g, u, h, a = saved
    dxres, dout, dgu, dlg, dlb, dgate = post_bwd(dxn, x, out, mod3, lng, w_out, 0.5, tag + "_outb", tm=256,
                                                 gu=(g, u))
    dx, dsh, dsc = modmm_bwd(dgu, w_in, x, mod3, dxres, tag + "_inb", tm=256)
    dw_in = mm_tn(h, dgu, tag + "_dwin")
    dw_out = _as_slots(mm_tn(a, dout, tag + "_dwout", bw=D), (N_DEV, D_FF // N_DEV, D))
    return dx, [dw_in, dw_out], dlg, dlb, jnp.concatenate([dsh, dsc, dgate], axis=1)


def mlstm_fwd(x, mod3, w_in, w_out, conv_w, gain, gbias, lng, lnb, Bl, S, gather=()):
    proj, h = modmm(x, mod3, w_in, F32, "ml_in", tn=M_PROJ_PAD // 5)
    proj3 = proj.reshape(Bl, S, M_PROJ_PAD)
    qk3 = conv_silu(proj3, conv_w, "ml_conv")
    y3, *rest = mlstm_cell_fwd(qk3, proj3, gain, gbias, "ml_cell", gather=gather)
    states, gathered = rest[:3], rest[3:]
    y = y3.reshape(Bl * S, D)
    out, xn = proj_post(y[None], w_out, x, mod3, lng, lnb, 1.0, "ml_out")
    return xn, (x, out, h, proj3, qk3, y, states), gathered


def mlstm_bwd(dxn, saved, mod3, w_in, w_out, conv_w, gain, gbias, lng, Bl, S, exchange=()):
    x, out, h, proj3, qk3, y, states = saved
    dxres, dout, dy, dlg, dlb, dgate = post_bwd(dxn, x, out, mod3, lng, w_out, 1.0, "ml_outb")
    dq, dk, dv, do, dg, dgain, dgb, *received = mlstm_cell_bwd(qk3, proj3, gain, gbias, dy.reshape(Bl, S, D),
                                                               states, "ml_cellb", exchange=exchange)
    dqk, dconv = conv_silu_bwd(proj3, conv_w, dq, dk, "ml_convb")
    dproj = jnp.concatenate([dqk, dv, do, dg.astype(BF16)], axis=2).reshape(Bl * S, M_PROJ_PAD)
    dx, dsh, dsc = modmm_bwd(dproj, w_in, x, mod3, dxres, "ml_inb", tn=M_PROJ_PAD // 5)
    dwi, _ = mm_tn(h, dproj, "ml_dwin", bw=M_PROJ_PAD // 5)
    dwi = _restack(jnp.moveaxis(dwi, 0, 1).reshape(D, M_PROJ_PAD)[:, :M_PROJ], 1)
    dw_out = _as_slots(mm_tn(y, dout, "ml_dwout", bw=D), (N_DEV, D // N_DEV, D))
    small = (jnp.sum(dconv, axis=0), jnp.sum(dgain, axis=0), jnp.sum(dgb, axis=0)[:, :2 * HEADS])
    dmod3 = jnp.concatenate([dsh, dsc, dgate], axis=1)
    return dx, [(dwi, dwi.astype(BF16)), dw_out], dlg, dlb, dmod3, small, received


def attn_mixer_fwd(x, mod3, w_in, w_out, lng, lnb, Bl, S):
    proj, h = modmm(x, mod3, w_in, BF16, "at_in")
    os_, lses = [], []
    for g, (_, dil) in enumerate(DIL_GROUPS):
        o_g, l_g = attn_fwd(proj, Bl, S, g, dil, "at_core%d" % g)
        os_.append(o_g)
        lses.append(l_g)
    ob, of, lt = attn_merge(os_, lses, "at_merge")
    out, xn = proj_post(ob[None], w_out, x, mod3, lng, lnb, 1.0, "at_out")
    return xn, (x, out, h, proj, ob, of, lt)


def attn_mixer_bwd(dxn, saved, mod3, w_in, w_out, lng, Bl, S):
    x, out, h, proj, ob, of, lt = saved
    dxres, dout, do, dlg, dlb, dgate = post_bwd(dxn, x, out, mod3, lng, w_out, 1.0, "at_outb")
    do = do[0]
    parts = []
    for g, (_, dil) in enumerate(DIL_GROUPS):
        parts += attn_bwd(proj, do, of, lt, Bl, S, g, dil, "at_coreb%d" % g)
    dproj = jnp.concatenate(parts, axis=1)
    dx, dsh, dsc = modmm_bwd(dproj, w_in, x, mod3, dxres, "at_inb")
    dw_in = mm_tn(h, dproj, "at_dwin", bw=w_in.shape[2])
    dw_out = _as_slots(mm_tn(ob, dout, "at_dwout", bw=D), (N_DEV, D // N_DEV, D))
    return dx, [dw_in, dw_out], dlg, dlb, jnp.concatenate([dsh, dsc, dgate], axis=1)


def _unstack(stacked, axis):
    full = jnp.moveaxis(stacked, 0, axis)
    shp = list(full.shape)
    shp[axis:axis + 2] = [shp[axis] * shp[axis + 1]]
    return full.reshape(shp)


def _restack(full, axis):
    shp = list(full.shape)
    shp[axis:axis + 1] = [N_DEV, shp[axis] // N_DEV]
    return jnp.moveaxis(full.reshape(shp), axis, 0)


def kernel(x, c, ada_w, ada_b, ln_g, ln_b, ffn_w_in, ffn_w_out, mlstm_w_in, mlstm_gate_bias, mlstm_conv_w, mlstm_head_gain, mlstm_w_out, attn_w_in, attn_w_out, loss_target, m_ada_w, m_ada_b, m_ln_g, m_ln_b, m_ffn_w_in, m_ffn_w_out, m_mlstm_w_in, m_mlstm_gate_bias, m_mlstm_conv_w, m_mlstm_head_gain, m_mlstm_w_out, m_attn_w_in, m_attn_w_out, v_ada_w, v_ada_b, v_ln_g, v_ln_b, v_ffn_w_in, v_ffn_w_out, v_mlstm_w_in, v_mlstm_gate_bias, v_mlstm_conv_w, v_mlstm_head_gain, v_mlstm_w_out, v_attn_w_in, v_attn_w_out):
    Bl, S, _ = x.shape
    T = Bl * S
    Bg = Bl * N_DEV
    me = 4 * lax.axis_index("x") + 2 * lax.axis_index("y") + lax.axis_index("c")
    onehot = (jnp.arange(N_DEV) == me).astype(F32)
    weights = dict(ada_w=ada_w, ada_b=ada_b, ln_g=ln_g, ln_b=ln_b, ffn_w_in=ffn_w_in, ffn_w_out=ffn_w_out,
                   mlstm_w_in=mlstm_w_in, mlstm_gate_bias=mlstm_gate_bias, mlstm_conv_w=mlstm_conv_w,
                   mlstm_head_gain=mlstm_head_gain, mlstm_w_out=mlstm_w_out, attn_w_in=attn_w_in,
                   attn_w_out=attn_w_out)
    m_in = dict(ada_w=m_ada_w, ada_b=m_ada_b, ln_g=m_ln_g, ln_b=m_ln_b, ffn_w_in=m_ffn_w_in,
                ffn_w_out=m_ffn_w_out, mlstm_w_in=m_mlstm_w_in, mlstm_gate_bias=m_mlstm_gate_bias,
                mlstm_conv_w=m_mlstm_conv_w, mlstm_head_gain=m_mlstm_head_gain, mlstm_w_out=m_mlstm_w_out,
                attn_w_in=m_attn_w_in, attn_w_out=m_attn_w_out)
    v_in = dict(ada_w=v_ada_w, ada_b=v_ada_b, ln_g=v_ln_g, ln_b=v_ln_b, ffn_w_in=v_ffn_w_in,
                ffn_w_out=v_ffn_w_out, mlstm_w_in=v_mlstm_w_in, mlstm_gate_bias=v_mlstm_gate_bias,
                mlstm_conv_w=v_mlstm_conv_w, mlstm_head_gain=v_mlstm_head_gain, mlstm_w_out=v_mlstm_w_out,
                attn_w_in=v_attn_w_in, attn_w_out=v_attn_w_out)

    mixer = ("mlstm", "attn")
    shards = [[ffn_w_in[layer, 0], ffn_w_in[layer, 1], ffn_w_out[layer, 0], ffn_w_out[layer, 1],
               weights[mixer[layer] + "_w_in"][0], weights[mixer[layer] + "_w_out"][0]] for layer in range(DEPTH)]
    sends = [[s.astype(BF16) for s in layer_shards] for layer_shards in shards]
    small = jnp.concatenate([c.reshape(-1), ln_g.reshape(-1), ln_b.reshape(-1), mlstm_conv_w.reshape(-1)])
    n_small = small.shape[0]
    small = jnp.pad(small, (0, -n_small % (8 * PACK_COLS))).reshape(-1, PACK_COLS)

    def gathered_weights(g):
        return ((g[0], g[1]), (g[2].reshape(4, D_FF // 4, D), g[3].reshape(4, D_FF // 4, D)), g[4],
                g[5].reshape(1, D, D))

    *g0, small_all = all_gather(sends[0] + [small], "ag_params")
    full = [gathered_weights(g0), None]
    small_flat = small_all.reshape(N_DEV, -1)
    o0 = 0
    c_all = small_flat[:, o0:o0 + c.size].reshape(Bg, D)
    o0 += c.size
    lng_full = _unstack(small_flat[:, o0:o0 + ln_g.size].reshape((N_DEV,) + ln_g.shape), 2)
    o0 += ln_g.size
    lnb_full = _unstack(small_flat[:, o0:o0 + ln_b.size].reshape((N_DEV,) + ln_b.shape), 2)
    o0 += ln_b.size
    conv_full = _unstack(small_flat[:, o0:o0 + mlstm_conv_w.size].reshape((N_DEV,) + mlstm_conv_w.shape), 2)[0]
    mw_in = jnp.pad(_unstack(full[0][2], 1), ((0, 0), (0, M_PROJ_PAD - M_PROJ)))
    gbias = jnp.pad(mlstm_gate_bias, ((0, 0), (0, 128 - 2 * HEADS)))

    ncols = ada_w.shape[2]
    ada_b_cols = lax.dynamic_slice_in_dim(ada_b, me * ncols, ncols, axis=1).reshape(DEPTH, 1, ncols)
    mod_cols = ada_fwd(c_all, ada_w, ada_b_cols, "ada_fwd")
    (mod_g,) = all_gather([mod_cols.reshape(DEPTH * Bg, ncols)], "ag_mod")
    mod_full = _unstack(mod_g.reshape(N_DEV, DEPTH, Bg, ncols), 2)
    mod_mine = lax.dynamic_slice_in_dim(mod_full, me * Bl, Bl, axis=1).reshape(DEPTH, Bl, 3, 3, D)

    xt = x.reshape(T, D)
    saved = []
    for layer in range(DEPTH):
        def lnp(s, layer=layer):
            return lng_full[layer, s].reshape(1, D), lnb_full[layer, s].reshape(1, D)
        md = mod_mine[layer]
        f_in, f_out, mix_in, mix_out = full[layer]
        xt, sv0 = ffn_fwd(xt, md[:, 0], f_in[0], f_out[0], *lnp(0), "f%da" % layer)
        if layer % 2 == 0:
            xt, sv1, g1 = mlstm_fwd(xt, md[:, 1], mw_in, mix_out, conv_full, mlstm_head_gain, gbias, *lnp(1), Bl, S,
                                    gather=sends[1])
            full[1] = gathered_weights(g1)
        else:
            xt, sv1 = attn_mixer_fwd(xt, md[:, 1], mix_in, mix_out, *lnp(1), Bl, S)
        xt, sv2 = ffn_fwd(xt, md[:, 2], f_in[1], f_out[1], *lnp(2), "f%db" % layer)
        saved.append((sv0, sv1, sv2))

    dxt, lsum = loss_head(xt, loss_target.reshape(T, D), "loss")
    loss = lax.psum(lsum[0, 0], MESH_AXES)

    dmod, dlg_all, dlb_all = [None] * DEPTH, [None] * DEPTH, [None] * DEPTH
    wgrads, recvs = [None] * DEPTH, [None] * DEPTH
    ml_small = None
    for layer in reversed(range(DEPTH)):
        md = mod_mine[layer]
        f_in, f_out, mix_in, mix_out = full[layer]
        sv0, sv1, sv2 = saved[layer]
        dxt, dw2, dlg2, dlb2, dm2 = ffn_bwd(dxt, sv2, md[:, 2], f_in[1], f_out[1],
                                            lng_full[layer, 2].reshape(1, D), "f%db" % layer)
        lg1 = lng_full[layer, 1].reshape(1, D)
        if layer % 2 == 0:
            dxt, dw1, dlg1, dlb1, dm1, ml_small, recvs[1] = mlstm_bwd(
                dxt, sv1, md[:, 1], mw_in, mix_out, conv_full, mlstm_head_gain, gbias, lg1, Bl, S,
                exchange=[b16 for _, b16 in wgrads[1]])
        else:
            dxt, dw1, dlg1, dlb1, dm1 = attn_mixer_bwd(dxt, sv1, md[:, 1], mix_in, mix_out, lg1, Bl, S)
        dxt, dw0, dlg0, dlb0, dm0 = ffn_bwd(dxt, sv0, md[:, 0], f_in[0], f_out[0],
                                            lng_full[layer, 0].reshape(1, D), "f%da" % layer)
        wgrads[layer] = [dw0[0], dw2[0], dw0[1], dw2[1], dw1[0], dw1[1]]
        dmod[layer] = jnp.stack([dm0, dm1, dm2], axis=1).reshape(Bl, 9 * D)
        dlg_all[layer] = jnp.concatenate([dlg0, dlg1, dlg2], axis=0)
        dlb_all[layer] = jnp.concatenate([dlb0, dlb1, dlb2], axis=0)
    grad_x = dxt.reshape(Bl, S, D)
    recvs[0] = exchange_shards([b16 for _, b16 in wgrads[0]], "rs_grads")

    gsh = [[shard_sum(lax.dynamic_index_in_dim(f32, me, axis=0, keepdims=False), recv, onehot,
                      "rs_sum%d_%d" % (layer, i))
            for i, ((f32, _), recv) in enumerate(zip(wgrads[layer], recvs[layer]))] for layer in range(DEPTH)]
    grads = {"ffn_w_in": jnp.stack([jnp.stack(g[0:2]) for g in gsh]),
             "ffn_w_out": jnp.stack([jnp.stack(g[2:4]) for g in gsh]),
             "mlstm_w_in": gsh[0][4][None], "mlstm_w_out": gsh[0][5][None],
             "attn_w_in": gsh[1][4][None], "attn_w_out": gsh[1][5][None]}

    dconv, dgain, dgbias = ml_small
    parts = [jnp.stack(dmod).reshape(-1), dgbias.reshape(-1), dgain.reshape(-1),
             jnp.stack(dlg_all).reshape(-1), jnp.stack(dlb_all).reshape(-1), dconv.reshape(-1)]
    sizes = [p.shape[0] for p in parts]
    flat = jnp.concatenate(parts)
    flat = jnp.pad(flat, (0, -flat.shape[0] % (8 * PACK_COLS))).reshape(-1, PACK_COLS)
    (sm_all,) = all_gather([flat], "ag_small")
    sm_sum = sum_leading(sm_all, "small_sum").reshape(-1)
    dmod_all = sm_all.reshape(N_DEV, -1)[:, :sizes[0]].reshape(N_DEV, DEPTH, Bl, 9 * D)
    dmod_all = jnp.moveaxis(dmod_all, 0, 1).reshape(DEPTH, Bg, 9 * D)
    o0 = sizes[0]
    grads["mlstm_gate_bias"] = sm_sum[o0:o0 + sizes[1]].reshape(mlstm_gate_bias.shape)
    o0 += sizes[1]
    grads["mlstm_head_gain"] = sm_sum[o0:o0 + sizes[2]].reshape(mlstm_head_gain.shape)
    o0 += sizes[2]
    nl = ln_g.shape[2]
    g_lng = sm_sum[o0:o0 + sizes[3]].reshape(DEPTH, 3, D)
    o0 += sizes[3]
    g_lnb = sm_sum[o0:o0 + sizes[4]].reshape(DEPTH, 3, D)
    o0 += sizes[4]
    g_conv = sm_sum[o0:o0 + sizes[5]].reshape(1, 4, D)
    grads["ln_g"] = lax.dynamic_slice_in_dim(g_lng, me * nl, nl, axis=2)
    grads["ln_b"] = lax.dynamic_slice_in_dim(g_lnb, me * nl, nl, axis=2)
    grads["mlstm_conv_w"] = lax.dynamic_slice_in_dim(g_conv, me * nl, nl, axis=2)
    dmod_cols = lax.dynamic_slice_in_dim(dmod_all, me * ncols, ncols, axis=2)
    gw, gb = ada_bwd(c_all.T, dmod_cols, dmod_all, "ada_bwd")
    grads["ada_w"] = gw
    grads["ada_b"] = gb.reshape(ada_b.shape)

    names = ["ada_w", "ada_b", "ln_g", "ln_b", "ffn_w_in", "ffn_w_out", "mlstm_w_in", "mlstm_gate_bias",
             "mlstm_conv_w", "mlstm_head_gain", "mlstm_w_out", "attn_w_in", "attn_w_out"]
    deltas, new_m, new_v = [], [], []
    for k in names:
        w = weights[k]
        shp2 = (math.prod(w.shape[:-1]), w.shape[-1])
        d_, m_, v_ = adamw(w.reshape(shp2), grads[k].reshape(shp2), m_in[k].reshape(shp2), v_in[k].reshape(shp2),
                           "adamw_" + k)
        deltas.append(d_.reshape(w.shape))
        new_m.append(m_.reshape(w.shape))
        new_v.append(v_.reshape(w.shape))
    return (loss, grad_x, *[grads[k] for k in names], *deltas, *new_m, *new_v)
```

```python
import functools
import math

import jax
import jax.numpy as jnp
from jax import lax
from jax.experimental import pallas as pl
from jax.experimental.pallas import tpu as pltpu

F32 = jnp.float32
BF16 = jnp.bfloat16

N_DEV = 8
MESH_AXES = ("x", "y", "c")
D = 1024
DEPTH = 2
D_FF = 2816
HEADS = 8
M_DQK = 64
M_DV = 128
M_CHUNK = 64
M_SLAB = 512
M_PROJ = 3088
M_PROJ_PAD = 3200
A_PROJ = 9216
DIL_GROUPS = ((128, 1), (512, 4), (2048, 16))
A_BLK = 128
ALPHA = (2 * DEPTH) ** 0.25
LN_EPS = 1e-5
RMS_EPS = 1e-6
ADAM_LR = 0.001
ADAM_B1 = 0.9
ADAM_B2 = 0.999
ADAM_EPS = 1e-08
ADAM_WD = 0.01
ADAM_STEP = 10
NEG = -1e30
V7X_VMEM_LIMIT = 56 * 1024 * 1024
PACK_COLS = 1024
MESH_ID = pl.DeviceIdType.MESH
ANY_SPEC = pl.BlockSpec(memory_space=pl.ANY)


def _cp(n_axes):
    return pltpu.CompilerParams(dimension_semantics=("arbitrary",) * n_axes,
                                vmem_limit_bytes=V7X_VMEM_LIMIT)


def _dot(a, b):
    return jnp.dot(a, b, preferred_element_type=F32)


def _dot_nt(a, b):
    return lax.dot_general(a, b, (((1,), (1,)), ((), ())), preferred_element_type=F32)


def _dot_tn(a, b):
    return lax.dot_general(a, b, (((0,), (0,)), ((), ())), preferred_element_type=F32)


def _sum0(a):
    return jnp.sum(a, axis=0, keepdims=True)


def _sum1(a):
    return jnp.sum(a, axis=1, keepdims=True)


def _round(a):
    return a.astype(BF16).astype(F32)


def _sigmoid(a):
    return 1.0 / (1.0 + jnp.exp(-a))


def _tile(n, pref):
    t = min(n, pref)
    while n % t:
        t //= 2
    return t


def all_gather(arrs, name):
    n = len(arrs)

    def body(*refs):
        gather = Gather(refs[:n], refs[n:2 * n], *refs[2 * n:])
        gather.start()
        gather.finish()

    return pl.pallas_call(
        body, name=name, out_shape=Gather.out_shape(arrs),
        in_specs=[ANY_SPEC] * n, out_specs=[ANY_SPEC] * n, scratch_shapes=Gather.scratch(n),
    )(*arrs)


class Gather:
    def __init__(self, ins, outs, send_sems, recv_sems, local_sems):
        x, y, c = lax.axis_index("x"), lax.axis_index("y"), lax.axis_index("c")
        me, sibling = (x, y, c), (x, y, 1 - c)
        chips = [(1 - x, y), (x, 1 - y), (1 - x, 1 - y)]

        def slot(a, p):
            return outs[a].at[4 * p[0] + 2 * p[1] + p[2]]

        def copy(a, k, block, to, src=None):
            return pltpu.make_async_remote_copy(
                src_ref=slot(a, block) if src is None else src, dst_ref=slot(a, block),
                send_sem=send_sems.at[7 * a + k], recv_sem=recv_sems.at[7 * a + k],
                device_id=to, device_id_type=MESH_ID)

        n = len(ins)
        self.mine = [pltpu.make_async_copy(ins[a], slot(a, me), local_sems.at[a]) for a in range(n)]
        self.first, self.over_ici, self.passed, self.from_sibling = [], [], [], []
        for a in range(n):
            self.first.append(copy(a, 0, me, sibling, src=ins[a]))
            self.from_sibling.append(copy(a, 0, sibling, me))
            for j, chip in enumerate(chips):
                self.first.append(copy(a, 1 + j, me, (*chip, c), src=ins[a]))
                self.over_ici.append(copy(a, 1 + j, (*chip, c), me))
                self.passed.append(copy(a, 4 + j, (*chip, c), sibling))
                self.from_sibling.append(copy(a, 4 + j, (*chip, 1 - c), me))

    @staticmethod
    def out_shape(arrs):
        return [jax.ShapeDtypeStruct((N_DEV,) + a.shape, a.dtype) for a in arrs]

    @staticmethod
    def scratch(n):
        return [pltpu.SemaphoreType.DMA((7 * n,)), pltpu.SemaphoreType.DMA((7 * n,)),
                pltpu.SemaphoreType.DMA((n,))]

    def start(self):
        for cp in self.mine + self.first:
            cp.start()

    def finish(self):
        for landed, onward in zip(self.over_ici, self.passed):
            landed.wait_recv()
            onward.start()
        for cp in self.from_sibling:
            cp.wait_recv()
        for cp in self.first + self.passed:
            cp.wait_send()
        for cp in self.mine:
            cp.wait()


class Exchange:
    def __init__(self, sends, recvs, send_sems, recv_sems, local_sems):
        x, y, c = lax.axis_index("x"), lax.axis_index("y"), lax.axis_index("c")
        me = 4 * x + 2 * y + c
        self.own = [pltpu.make_async_copy(s.at[me], r.at[me], local_sems.at[a])
                    for a, (s, r) in enumerate(zip(sends, recvs))]
        self.copies = []
        for a, (s_ref, r_ref) in enumerate(zip(sends, recvs)):
            for k in range(1, N_DEV):
                px = 1 - x if (k >> 2) & 1 else x
                py = 1 - y if (k >> 1) & 1 else y
                pc = 1 - c if k & 1 else c
                self.copies.append(pltpu.make_async_remote_copy(
                    src_ref=s_ref.at[4 * px + 2 * py + pc], dst_ref=r_ref.at[me],
                    send_sem=send_sems.at[7 * a + k - 1], recv_sem=recv_sems.at[7 * a + k - 1],
                    device_id=(px, py, pc), device_id_type=MESH_ID))

    @staticmethod
    def scratch(n):
        return [pltpu.SemaphoreType.DMA((7 * n,)), pltpu.SemaphoreType.DMA((7 * n,)),
                pltpu.SemaphoreType.DMA((n,))]

    def start(self):
        for cp in self.own + self.copies:
            cp.start()

    def finish(self):
        for cp in self.copies:
            cp.wait_send()
            cp.wait_recv()
        for cp in self.own:
            cp.wait()


def host_comm(body, grid, n_in, n_out, gather=(), exchange=()):
    ng, nx = len(gather), len(exchange)
    if ng + nx == 0:
        return body, [], [], [], []

    def hosted(*refs):
        ins, c_in, rest = refs[:n_in], refs[n_in:n_in + ng + nx], refs[n_in + ng + nx:]
        outs, c_out, rest = rest[:n_out], rest[n_out:n_out + ng + nx], rest[n_out + ng + nx:]
        n_sems = 3 * ((ng > 0) + (nx > 0))
        scratch, sems = rest[:len(rest) - n_sems], rest[len(rest) - n_sems:]

        def comms():
            made = [Gather(c_in[:ng], c_out[:ng], *sems[:3])] if ng else []
            return made + ([Exchange(c_in[ng:], c_out[ng:], *sems[-3:])] if nx else [])

        ids = [pl.program_id(a) for a in range(len(grid))]

        @pl.when(functools.reduce(jnp.logical_and, [i == 0 for i in ids]))
        def _():
            for cm in comms():
                cm.start()
        body(*ins, *outs, *scratch)

        @pl.when(functools.reduce(jnp.logical_and, [i == g - 1 for i, g in zip(ids, grid)]))
        def _():
            for cm in comms():
                cm.finish()

    shapes = Gather.out_shape(gather) + [jax.ShapeDtypeStruct(a.shape, a.dtype) for a in exchange]
    scratch = (Gather.scratch(ng) if ng else []) + (Exchange.scratch(nx) if nx else [])
    return hosted, [ANY_SPEC] * (ng + nx), [ANY_SPEC] * (ng + nx), shapes, scratch


def exchange_shards(sends, name):
    n = len(sends)

    def body(*refs):
        exchange = Exchange(refs[:n], refs[n:2 * n], *refs[2 * n:])
        exchange.start()
        exchange.finish()

    return pl.pallas_call(
        body, name=name, out_shape=[jax.ShapeDtypeStruct(s.shape, s.dtype) for s in sends],
        in_specs=[ANY_SPEC] * n, out_specs=[ANY_SPEC] * n, scratch_shapes=Exchange.scratch(n),
    )(*sends)


def shard_sum(own, recv, onehot, name):
    R, C = own.shape
    tr = _tile(R, 512)

    def body(oh_ref, own_ref, recv_ref, o_ref):
        acc = None
        for j in range(N_DEV):
            term = jnp.where(oh_ref[j] > 0.5, own_ref[...], recv_ref[j].astype(F32))
            acc = term if acc is None else acc + term
        o_ref[...] = acc

    return pl.pallas_call(
        body, name=name, grid=(R // tr,),
        in_specs=[pl.BlockSpec(memory_space=pltpu.SMEM),
                  pl.BlockSpec((tr, C), lambda i: (i, 0)),
                  pl.BlockSpec((N_DEV, tr, C), lambda i: (0, i, 0))],
        out_specs=pl.BlockSpec((tr, C), lambda i: (i, 0)),
        out_shape=jax.ShapeDtypeStruct((R, C), F32), compiler_params=_cp(1),
    )(onehot, own, recv)


def sum_leading(a, name):
    _, R, C = a.shape

    def body(a_ref, o_ref):
        acc = a_ref[0]
        for j in range(1, N_DEV):
            acc = acc + a_ref[j]
        o_ref[...] = acc

    return pl.pallas_call(body, name=name, out_shape=jax.ShapeDtypeStruct((R, C), F32),
                          compiler_params=_cp(0))(a)


def _col_chunks(w, tn):
    if w.ndim == 3:
        return w.shape[0], w.shape[2], pl.BlockSpec((None, w.shape[1], w.shape[2]), lambda i, j: (j, 0, 0))
    return w.shape[1] // tn, tn, pl.BlockSpec((w.shape[0], tn), lambda i, j: (0, j))


def modmm(x, mod3, w, out_dtype, name, tn=None):
    T, Dm = x.shape
    nj, tn, w_spec = _col_chunks(w, tn)
    N = nj * tn
    Bl = mod3.shape[0]
    tm = _tile(T // Bl, 1024)
    tpb = T // Bl // tm

    def body(x_ref, mod_ref, w_ref, o_ref, h_ref, hs):
        @pl.when(pl.program_id(1) == 0)
        def _():
            m = mod_ref[0]
            hs[...] = (x_ref[...] * (1.0 + m[1:2, :]) + m[0:1, :]).astype(BF16)
            h_ref[...] = hs[...]
        o_ref[...] = _dot(hs[...], w_ref[...]).astype(o_ref.dtype)

    return pl.pallas_call(
        body, name=name, grid=(T // tm, nj),
        in_specs=[pl.BlockSpec((tm, Dm), lambda i, j: (i, 0)),
                  pl.BlockSpec((1, 3, Dm), lambda i, j: (i // tpb, 0, 0)), w_spec],
        out_specs=[pl.BlockSpec((tm, tn), lambda i, j: (i, j)),
                   pl.BlockSpec((tm, Dm), lambda i, j: (i, 0))],
        out_shape=[jax.ShapeDtypeStruct((T, N), out_dtype), jax.ShapeDtypeStruct((T, Dm), BF16)],
        scratch_shapes=[pltpu.VMEM((tm, Dm), BF16)], compiler_params=_cp(2),
    )(x, mod3, w)


def modmm_bwd(dp, w, x, mod3, dxres, name, tn=None, tm=512):
    T, Dm = x.shape
    Bl = mod3.shape[0]
    tm = _tile(T // Bl, tm)
    tpb = T // Bl // tm
    resident = dp.ndim == 3
    if resident:
        nc, nj = dp.shape[0], 1
        dp_spec = pl.BlockSpec((nc, tm, dp.shape[2]), lambda i, j: (0, i, 0))
        w_spec = pl.BlockSpec(w.shape, lambda i, j: (0, 0, 0))
    else:
        nj, tn, w_spec = _col_chunks(w, tn)
        dp_spec = pl.BlockSpec((tm, tn), lambda i, j: (i, j))

    def body(dp_ref, w_ref, x_ref, mod_ref, dxr_ref, dx_ref, dsh_ref, dsc_ref, acc):
        i, j = pl.program_id(0), pl.program_id(1)

        @pl.when(j == 0)
        def _():
            acc[...] = jnp.zeros_like(acc)
        if resident:
            for c in range(nc):
                acc[...] += _dot_nt(dp_ref[c], w_ref[c])
        else:
            acc[...] += _dot_nt(dp_ref[...], w_ref[...])

        @pl.when(j == nj - 1)
        def _():
            dh = acc[...]
            xx = x_ref[...]
            dx_ref[...] = dxr_ref[...] + dh * (1.0 + mod_ref[0][1:2, :])

            @pl.when(i % tpb == 0)
            def _():
                dsh_ref[...] = jnp.zeros_like(dsh_ref)
                dsc_ref[...] = jnp.zeros_like(dsc_ref)
            dsh_ref[0] += _sum0(dh)
            dsc_ref[0] += _sum0(dh * xx)

    return pl.pallas_call(
        body, name=name, grid=(T // tm, nj),
        in_specs=[dp_spec, w_spec,
                  pl.BlockSpec((tm, Dm), lambda i, j: (i, 0)),
                  pl.BlockSpec((1, 3, Dm), lambda i, j: (i // tpb, 0, 0)),
                  pl.BlockSpec((tm, Dm), lambda i, j: (i, 0))],
        out_specs=[pl.BlockSpec((tm, Dm), lambda i, j: (i, 0)),
                   pl.BlockSpec((1, 1, Dm), lambda i, j: (i // tpb, 0, 0)),
                   pl.BlockSpec((1, 1, Dm), lambda i, j: (i // tpb, 0, 0))],
        out_shape=[jax.ShapeDtypeStruct((T, Dm), F32), jax.ShapeDtypeStruct((Bl, 1, Dm), F32),
                   jax.ShapeDtypeStruct((Bl, 1, Dm), F32)],
        scratch_shapes=[pltpu.VMEM((tm, Dm), F32)], compiler_params=_cp(2),
    )(dp, w, x, mod3, dxres)


def _ln_stats(z):
    mu = jnp.mean(z, axis=-1, keepdims=True)
    zc = z - mu
    var = jnp.mean(zc * zc, axis=-1, keepdims=True)
    rstd = lax.rsqrt(var + LN_EPS)
    return zc * rstd, rstd


def proj_post(a, w, x, mod3, lng, lnb, weight, name):
    nk, T, tk = a.shape
    Dm = w.shape[2]
    Bl = mod3.shape[0]
    tm = _tile(T // Bl, 512)
    tpb = T // Bl // tm

    def body(a_ref, w_ref, x_ref, mod_ref, g_ref, b_ref, out_ref, xn_ref):
        out = _dot(a_ref[0], w_ref[0])
        for k in range(1, nk):
            out = out + _dot(a_ref[k], w_ref[k])
        out_ref[...] = out
        z = ALPHA * x_ref[...] + (weight * (1.0 + mod_ref[0][2:3, :])) * out
        xhat, _ = _ln_stats(z)
        xn_ref[...] = xhat * g_ref[...] + b_ref[...]

    row = pl.BlockSpec((tm, Dm), lambda i: (i, 0))
    vec = pl.BlockSpec((1, Dm), lambda i: (0, 0))
    return pl.pallas_call(
        body, name=name, grid=(T // tm,),
        in_specs=[pl.BlockSpec((nk, tm, tk), lambda i: (0, i, 0)),
                  pl.BlockSpec((nk, tk, Dm), lambda i: (0, 0, 0)),
                  row, pl.BlockSpec((1, 3, Dm), lambda i: (i // tpb, 0, 0)), vec, vec],
        out_specs=[row, row],
        out_shape=[jax.ShapeDtypeStruct((T, Dm), F32), jax.ShapeDtypeStruct((T, Dm), F32)],
        compiler_params=_cp(1),
    )(a, w, x, mod3, lng, lnb)


def post_bwd(dxn, x, out, mod3, lng, w, weight, name, tm=512, gu=None, exchange=()):
    T, Dm = x.shape
    nk, tk, _ = w.shape
    Bl = mod3.shape[0]
    tm = _tile(T // Bl, tm)
    tpb = T // Bl // tm
    fused = gu is not None

    def body(dxn_ref, x_ref, out_ref, mod_ref, g_ref, w_ref, *rest):
        if fused:
            gg_ref, uu_ref = rest[:2]
            rest = rest[2:]
        dxr_ref, dout_ref, da_ref, dg_ref, db_ref, dgate_ref = rest
        i = pl.program_id(0)
        out = out_ref[...]
        dxn = dxn_ref[...]
        coef = weight * (1.0 + mod_ref[0][2:3, :])
        xhat, rstd = _ln_stats(ALPHA * x_ref[...] + coef * out)
        dyh = dxn * g_ref[...]
        dz = rstd * (dyh - jnp.mean(dyh, axis=-1, keepdims=True)
                     - xhat * jnp.mean(dyh * xhat, axis=-1, keepdims=True))
        dxr_ref[...] = ALPHA * dz
        dout = (coef * dz).astype(BF16)
        dout_ref[...] = dout

        @pl.when(i == 0)
        def _():
            dg_ref[...] = jnp.zeros_like(dg_ref)
            db_ref[...] = jnp.zeros_like(db_ref)

        @pl.when(i % tpb == 0)
        def _():
            dgate_ref[...] = jnp.zeros_like(dgate_ref)
        dg_ref[...] += _sum0(dxn * xhat)
        db_ref[...] += _sum0(dxn)
        dgate_ref[0] += _sum0((weight * out) * dz)
        for k in range(nk):
            da = _dot_nt(dout, w_ref[k])
            if fused:
                gg = gg_ref[k].astype(F32)
                s = _sigmoid(gg)
                da_ref[k] = (da * uu_ref[k].astype(F32) * (s * (1.0 + gg * (1.0 - s)))).astype(BF16)
                da_ref[nk + k] = (da * (gg * s)).astype(BF16)
            else:
                da_ref[k] = da.astype(BF16)

    row = pl.BlockSpec((tm, Dm), lambda i: (i, 0))
    vec = pl.BlockSpec((1, Dm), lambda i: (0, 0))
    wide = pl.BlockSpec((nk, tm, tk), lambda i: (0, i, 0))
    nda = 2 * nk if fused else nk
    grid = (T // tm,)
    body, c_in, c_out, c_shape, c_scratch = host_comm(body, grid, 8 if fused else 6, 6, exchange=exchange)
    *results, = pl.pallas_call(
        body, name=name, grid=grid,
        in_specs=[row, row, row, pl.BlockSpec((1, 3, Dm), lambda i: (i // tpb, 0, 0)), vec,
                  pl.BlockSpec((nk, tk, Dm), lambda i: (0, 0, 0))] + ([wide, wide] if fused else []) + c_in,
        out_specs=[row, row, pl.BlockSpec((nda, tm, tk), lambda i: (0, i, 0)),
                   vec, vec, pl.BlockSpec((1, 1, Dm), lambda i: (i // tpb, 0, 0))] + c_out,
        out_shape=[jax.ShapeDtypeStruct((T, Dm), F32), jax.ShapeDtypeStruct((T, Dm), BF16),
                   jax.ShapeDtypeStruct((nda, T, tk), BF16), jax.ShapeDtypeStruct((1, Dm), F32),
                   jax.ShapeDtypeStruct((1, Dm), F32), jax.ShapeDtypeStruct((Bl, 1, Dm), F32)] + c_shape,
        scratch_shapes=c_scratch, compiler_params=_cp(1),
    )(dxn, x, out, mod3, lng, w, *(gu if fused else ()), *exchange)
    return tuple(results[:6]) + ((results[6:],) if exchange else ())


def mm_tn(a, b, name, bw=None):
    a3, b3 = a.ndim == 3, b.ndim == 3
    nk, T, tk = a.shape if a3 else (1,) + a.shape
    nc, wn = (b.shape[0], b.shape[2]) if b3 else (b.shape[1] // bw, bw)
    tt = _tile(T, 2048)
    nt = T // tt

    def body(a_ref, b_ref, o_ref, ob_ref):
        t = pl.program_id(2)

        @pl.when(t == 0)
        def _():
            o_ref[...] = jnp.zeros_like(o_ref)
        o_ref[...] += _dot_tn(a_ref[...], b_ref[...])

        @pl.when(t == nt - 1)
        def _():
            ob_ref[...] = o_ref[...].astype(BF16)

    a_spec = (pl.BlockSpec((None, tt, tk), lambda k, c, t: (k, t, 0)) if a3
              else pl.BlockSpec((tt, tk), lambda k, c, t: (t, 0)))
    b_spec = (pl.BlockSpec((None, tt, wn), lambda k, c, t: (c, t, 0)) if b3
              else pl.BlockSpec((tt, wn), lambda k, c, t: (t, c)))
    o_spec = pl.BlockSpec((None, tk, wn), lambda k, c, t: (k * nc + c, 0, 0))
    return pl.pallas_call(
        body, name=name, grid=(nk, nc, nt), in_specs=[a_spec, b_spec], out_specs=[o_spec, o_spec],
        out_shape=[jax.ShapeDtypeStruct((nk * nc, tk, wn), F32), jax.ShapeDtypeStruct((nk * nc, tk, wn), BF16)],
        compiler_params=_cp(3),
    )(a, b)


def ffn_in(x, mod3, w, name, gather=()):
    T, Dm = x.shape
    nj, tf = w.shape[0] // 2, w.shape[2]
    Bl = mod3.shape[0]
    tm = _tile(T // Bl, 1024)
    tpb = T // Bl // tm

    def body(x_ref, mod_ref, wg_ref, wu_ref, a_ref, g_ref, u_ref, h_ref):
        m = mod_ref[0]
        h = (x_ref[...] * (1.0 + m[1:2, :]) + m[0:1, :]).astype(BF16)
        h_ref[...] = h
        g = _dot(h, wg_ref[...])
        u = _dot(h, wu_ref[...])
        a_ref[...] = (g * _sigmoid(g) * u).astype(BF16)
        g_ref[...] = g.astype(BF16)
        u_ref[...] = u.astype(BF16)

    col = pl.BlockSpec((None, tm, tf), lambda j, i: (j, i, 0))
    grid = (nj, T // tm)
    body, c_in, c_out, c_shape, c_scratch = host_comm(body, grid, 4, 4, gather=gather)
    a, g, u, h, *gathered = pl.pallas_call(
        body, name=name, grid=grid,
        in_specs=[pl.BlockSpec((tm, Dm), lambda j, i: (i, 0)),
                  pl.BlockSpec((1, 3, Dm), lambda j, i: (i // tpb, 0, 0)),
                  pl.BlockSpec((None, Dm, tf), lambda j, i: (j, 0, 0)),
                  pl.BlockSpec((None, Dm, tf), lambda j, i: (nj + j, 0, 0))] + c_in,
        out_specs=[col, col, col, pl.BlockSpec((None, tm, Dm), lambda j, i: (j, i, 0))] + c_out,
        out_shape=[jax.ShapeDtypeStruct((nj, T, tf), BF16)] * 3 + [jax.ShapeDtypeStruct((nj, T, Dm), BF16)]
        + c_shape,
        scratch_shapes=c_scratch, compiler_params=_cp(2),
    )(x, mod3, w, w, *gather)
    return a, g, u, h[0], gathered


def loss_head(y, tgt, name):
    T, Dm = y.shape
    tm = _tile(T, 512)
    nt = T // tm

    def body(y_ref, t_ref, dy_ref, l_ref, acc):
        i = pl.program_id(0)

        @pl.when(i == 0)
        def _():
            acc[...] = jnp.zeros_like(acc)
        e = y_ref[...] - t_ref[...]
        dy_ref[...] = e * (1.0 / Dm)
        acc[...] += _sum0(e * e)

        @pl.when(i == nt - 1)
        def _():
            l_ref[...] = jnp.broadcast_to(_sum1(acc[...]) * (0.5 / Dm), l_ref.shape)

    return pl.pallas_call(
        body, name=name, grid=(nt,),
        in_specs=[pl.BlockSpec((tm, Dm), lambda i: (i, 0)), pl.BlockSpec((tm, Dm), lambda i: (i, 0))],
        out_specs=[pl.BlockSpec((tm, Dm), lambda i: (i, 0)), pl.BlockSpec((1, 128), lambda i: (0, 0))],
        out_shape=[jax.ShapeDtypeStruct((T, Dm), F32), jax.ShapeDtypeStruct((1, 128), F32)],
        scratch_shapes=[pltpu.VMEM((1, Dm), F32)], compiler_params=_cp(1),
    )(y, tgt)


def adamw(w, g, m, v, name):
    R, C = w.shape
    tr = _tile(R, 512) if R % 8 == 0 else R

    def body(w_ref, g_ref, m_ref, v_ref, d_ref, nm_ref, nv_ref):
        gg = g_ref[...]
        mm = ADAM_B1 * m_ref[...] + (1.0 - ADAM_B1) * gg
        vv = ADAM_B2 * v_ref[...] + (1.0 - ADAM_B2) * (gg * gg)
        m_hat = mm / (1.0 - ADAM_B1 ** ADAM_STEP)
        v_hat = vv / (1.0 - ADAM_B2 ** ADAM_STEP)
        d_ref[...] = -ADAM_LR * (m_hat / (jnp.sqrt(v_hat) + ADAM_EPS) + ADAM_WD * w_ref[...])
        nm_ref[...] = mm
        nv_ref[...] = vv

    spec = pl.BlockSpec((tr, C), lambda i: (i, 0))
    return pl.pallas_call(
        body, name=name, grid=(R // tr,), in_specs=[spec] * 4, out_specs=[spec] * 3,
        out_shape=[jax.ShapeDtypeStruct((R, C), F32)] * 3, compiler_params=_cp(1),
    )(w, g, m, v)


def ada_fwd(c_all, ada_w, ada_b_cols, name):
    Lr, Dm, Nc = ada_w.shape
    Bg = c_all.shape[0]

    def body(c_ref, w_ref, b_ref, o_ref):
        cc = c_ref[...]
        cond = cc * _sigmoid(cc)
        o_ref[0] = _dot(cond.astype(BF16), w_ref[0].astype(BF16)) + b_ref[0]

    return pl.pallas_call(
        body, name=name, grid=(Lr,),
        in_specs=[pl.BlockSpec((Bg, Dm), lambda l: (0, 0)),
                  pl.BlockSpec((1, Dm, Nc), lambda l: (l, 0, 0)),
                  pl.BlockSpec((1, 1, Nc), lambda l: (l, 0, 0))],
        out_specs=pl.BlockSpec((1, Bg, Nc), lambda l: (l, 0, 0)),
        out_shape=jax.ShapeDtypeStruct((Lr, Bg, Nc), F32), compiler_params=_cp(1),
    )(c_all, ada_w, ada_b_cols)


def ada_bwd(c_all_t, dmod_cols, dmod_all, name):
    Dm, Bg = c_all_t.shape
    Lr, _, Nc = dmod_cols.shape
    Nf = dmod_all.shape[2]

    def body(c_ref, dm_ref, da_ref, gw_ref, gb_ref):
        cc = c_ref[...]
        cond = cc * _sigmoid(cc)
        gw_ref[0] = _dot(cond.astype(BF16), dm_ref[0].astype(BF16))
        gb_ref[0] = _sum0(da_ref[0])

    return pl.pallas_call(
        body, name=name, grid=(Lr,),
        in_specs=[pl.BlockSpec((Dm, Bg), lambda l: (0, 0)),
                  pl.BlockSpec((1, Bg, Nc), lambda l: (l, 0, 0)),
                  pl.BlockSpec((1, Bg, Nf), lambda l: (l, 0, 0))],
        out_specs=[pl.BlockSpec((1, Dm, Nc), lambda l: (l, 0, 0)),
                   pl.BlockSpec((1, 1, Nf), lambda l: (l, 0, 0))],
        out_shape=[jax.ShapeDtypeStruct((Lr, Dm, Nc), F32), jax.ShapeDtypeStruct((Lr, 1, Nf), F32)],
        compiler_params=_cp(1),
    )(c_all_t, dmod_cols, dmod_all)


def _conv_taps(x, w, rows):
    shifted = [x]
    c = w[3:4, :] * x
    for k in range(1, 4):
        xs = jnp.where(rows >= k, pltpu.roll(x, k, 0), 0.0)
        shifted.append(xs)
        c = c + w[3 - k:4 - k, :] * xs
    return c, shifted


def conv_silu(proj3, conv_w, name):
    Bl, S, _ = proj3.shape
    ncb = conv_w.shape[1] // 128

    def body(x_ref, w_ref, o_ref):
        rows = lax.broadcasted_iota(jnp.int32, (S, 128), 0)
        c, _ = _conv_taps(_round(x_ref[0]), _round(w_ref[...]), rows)
        o_ref[0] = c * _sigmoid(c)

    return pl.pallas_call(
        body, name=name, grid=(Bl, ncb),
        in_specs=[pl.BlockSpec((1, S, 128), lambda b, j: (b, 0, j)),
                  pl.BlockSpec((4, 128), lambda b, j: (0, j))],
        out_specs=pl.BlockSpec((1, S, 128), lambda b, j: (b, 0, j)),
        out_shape=jax.ShapeDtypeStruct((Bl, S, conv_w.shape[1]), F32), compiler_params=_cp(2),
    )(proj3, conv_w)


def conv_silu_bwd(proj3, conv_w, dq, dk, name):
    Bl, S, _ = proj3.shape
    nq = dq.shape[2] // 128

    def body(x_ref, w_ref, dq_ref, dk_ref, dx_ref, dw_ref):
        j = pl.program_id(1)
        rows = lax.broadcasted_iota(jnp.int32, (S, 128), 0)
        w = _round(w_ref[...])
        c, shifted = _conv_taps(_round(x_ref[0]), w, rows)
        s = _sigmoid(c)
        dact = jnp.where(j < nq, dq_ref[0], dk_ref[0])
        dc = _round(dact * (s * (1.0 + c * (1.0 - s))))
        dx = w[3:4, :] * dc
        dws = [_sum0(dc * shifted[0])]
        for k in range(1, 4):
            up = jnp.where(rows < S - k, pltpu.roll(dc, S - k, 0), 0.0)
            dx = dx + w[3 - k:4 - k, :] * up
            dws.append(_sum0(dc * shifted[k]))
        dx_ref[0] = dx.astype(BF16)
        tap = lax.broadcasted_iota(jnp.int32, (4, 128), 0)
        dw_ref[0] = functools.reduce(lambda a, b: a + b, [jnp.where(tap == 3 - k, dws[k], 0.0) for k in range(4)])

    return pl.pallas_call(
        body, name=name, grid=(Bl, 2 * nq),
        in_specs=[pl.BlockSpec((1, S, 128), lambda b, j: (b, 0, j)),
                  pl.BlockSpec((4, 128), lambda b, j: (0, j)),
                  pl.BlockSpec((1, S, 128), lambda b, j: (b, 0, jnp.minimum(j, nq - 1))),
                  pl.BlockSpec((1, S, 128), lambda b, j: (b, 0, jnp.maximum(j - nq, 0)))],
        out_specs=[pl.BlockSpec((1, S, 128), lambda b, j: (b, 0, j)),
                   pl.BlockSpec((1, 4, 128), lambda b, j: (b, 0, j))],
        out_shape=[jax.ShapeDtypeStruct((Bl, S, 2 * nq * 128), BF16),
                   jax.ShapeDtypeStruct((Bl, 4, 2 * nq * 128), F32)],
        compiler_params=_cp(2),
    )(proj3, conv_w, dq, dk)


def _log_sigmoid(a):
    return jnp.minimum(a, 0.0) - jnp.log(1.0 + jnp.exp(-jnp.abs(a)))


def _chunk_state(kc, vc, gi, bcum, b_last, C, n, m):
    a = b_last - bcum + gi
    m_loc = jnp.max(a, axis=0, keepdims=True)
    wa = jnp.exp(a - m_loc)
    c_loc = _dot_tn((wa * vc).astype(BF16), kc.astype(BF16))
    n_loc = _sum0(_round(wa) * _round(kc))
    m_new = jnp.maximum(b_last + m, m_loc)
    sp = jnp.exp(b_last + m - m_new)
    sl = jnp.exp(m_loc - m_new)
    return sp * C + sl * c_loc, sp * n + sl * n_loc, m_new, wa, sp, sl


def _chunk_out(qs, kc, vc, gi_row, bcum, bcum_row, low, C, n, m):
    inter_log = bcum + m
    dlog = jnp.where(low, bcum - bcum_row + gi_row, NEG)
    m_i = jnp.maximum(inter_log, jnp.max(dlog, axis=1, keepdims=True))
    dm = jnp.exp(dlog - m_i)
    iw = jnp.exp(inter_log - m_i)
    qs_b, k_b, v_b = qs.astype(BF16), kc.astype(BF16), vc.astype(BF16)
    sc = _dot_nt(qs_b, k_b) * dm
    qc_ = _dot_nt(qs_b, C.astype(BF16))
    qn = _sum1(_round(qs) * _round(n))
    num = _dot(sc.astype(BF16), v_b) + iw * qc_
    den = _sum1(sc) + iw * qn
    floor = jnp.exp(-m_i)
    dn = jnp.maximum(jnp.abs(den), floor)
    return dict(hc=num / dn, den=den, dn=dn, floor=floor, sc=sc, dm=dm, iw=iw, qc=qc_, qn=qn,
                qs_b=qs_b, k_b=k_b, v_b=v_b)


def _cell_consts(L):
    ri = lax.broadcasted_iota(jnp.int32, (L, L), 0)
    ci = lax.broadcasted_iota(jnp.int32, (L, L), 1)
    return ri == ci, ci <= ri, ri <= ci


def _load_chunk(q_ref, k_ref, v_ref, G, off, L, h, lane):
    hh = h % 2
    qmask = (lane >= M_DQK * hh) & (lane < M_DQK * (hh + 1))
    pair = pl.ds(128 * (h // 2), 128)
    qc = jnp.where(qmask, q_ref[0, pl.ds(off, L), pair], 0.0)
    kc = jnp.where(qmask, k_ref[0, pl.ds(off, L), pair], 0.0)
    vc = v_ref[0, pl.ds(off, L), pl.ds(M_DV * h, M_DV)]
    gi = _sum1(jnp.where(lane == h, G, 0.0))
    gf = _sum1(jnp.where(lane == h + HEADS, G, 0.0))
    return qmask, qc, kc, vc, gi, gf


def _gate_rows(gi, gf, eye, low, upp):
    lf = _log_sigmoid(gf)
    lf_row = _sum0(jnp.where(eye, lf, 0.0))
    gi_row = _sum0(jnp.where(eye, gi, 0.0))
    bcum = _sum1(jnp.where(low, lf_row, 0.0))
    bcum_row = _sum0(jnp.where(upp, lf, 0.0))
    b_last = _sum0(lf)
    return gi_row, bcum, bcum_row, b_last


def _cell_specs(SB, cpb, blk):
    def seq(width, col):
        return pl.BlockSpec((1, SB, width), lambda b, s: (b, blk(s), col))

    def state(rows):
        return pl.BlockSpec((1, HEADS, cpb, rows, 128), lambda b, s: (b, 0, blk(s), 0, 0))

    ins = [seq(D // 2, 0), seq(D // 2, 1), seq(D, 1), seq(D, 2), seq(128, 3 * D // 128),
           pl.BlockSpec((1, D), lambda b, s: (0, 0)), pl.BlockSpec((1, 128), lambda b, s: (0, 0))]
    return ins, [state(M_DV), state(1), state(1)], seq


def mlstm_cell_fwd(qk3, proj3, gain, gbias, name, gather=()):
    Bl, S, _ = qk3.shape
    L = M_CHUNK
    SB = min(M_SLAB, S)
    cpb, nc, nsb = SB // L, S // L, S // SB
    scale = M_DQK ** -0.5

    def body(q_ref, k_ref, v_ref, o_ref, g_ref, gain_ref, gb_ref, y_ref, cst_ref, nst_ref, mst_ref, C_s, n_s, m_s):
        @pl.when(pl.program_id(1) == 0)
        def _():
            C_s[...] = jnp.zeros_like(C_s)
            n_s[...] = jnp.zeros_like(n_s)
            m_s[...] = jnp.zeros_like(m_s)
        lane = lax.broadcasted_iota(jnp.int32, (L, 128), 1)
        eye, low, upp = _cell_consts(L)

        def step(c, carry):
            off = pl.multiple_of(c * L, L)
            G = g_ref[0, pl.ds(off, L), :] + gb_ref[...]
            for h in range(HEADS):
                C, n, mb = C_s[h], n_s[h], m_s[h]
                cst_ref[0, h, c] = C
                nst_ref[0, h, c] = n
                mst_ref[0, h, c] = mb
                m = mb[:, 0:1]
                _, qc, kc, vc, gi, gf = _load_chunk(q_ref, k_ref, v_ref, G, off, L, h, lane)
                gi_row, bcum, bcum_row, b_last = _gate_rows(gi, gf, eye, low, upp)
                r = _chunk_out(qc * scale, kc, vc, gi_row, bcum, bcum_row, low, C, n, m)
                hc = r["hc"]
                hn = hc * lax.rsqrt(jnp.mean(hc * hc, axis=-1, keepdims=True) + RMS_EPS)
                cols = pl.ds(M_DV * h, M_DV)
                oc = o_ref[0, pl.ds(off, L), cols]
                y_ref[0, pl.ds(off, L), cols] = (_sigmoid(oc) * hn * gain_ref[:, cols]).astype(BF16)
                C2, n2, m2, _, _, _ = _chunk_state(kc, vc, gi, bcum, b_last, C, n, m)
                C_s[h] = C2
                n_s[h] = n2
                m_s[h] = jnp.broadcast_to(m2, (1, 128))
            return carry

        lax.fori_loop(0, cpb, step, 0)

    ins, states, seq = _cell_specs(SB, cpb, lambda s: s)
    grid = (Bl, nsb)
    body, c_in, c_out, c_shape, c_scratch = host_comm(body, grid, 7, 4, gather=gather)
    return pl.pallas_call(
        body, name=name, grid=grid, in_specs=ins + c_in, out_specs=[seq(D, 0)] + states + c_out,
        out_shape=[jax.ShapeDtypeStruct((Bl, S, D), BF16),
                   jax.ShapeDtypeStruct((Bl, HEADS, nc, M_DV, 128), F32),
                   jax.ShapeDtypeStruct((Bl, HEADS, nc, 1, 128), F32),
                   jax.ShapeDtypeStruct((Bl, HEADS, nc, 1, 128), F32)] + c_shape,
        scratch_shapes=[pltpu.VMEM((HEADS, M_DV, 128), F32), pltpu.VMEM((HEADS, 1, 128), F32),
                        pltpu.VMEM((HEADS, 1, 128), F32)] + c_scratch,
        compiler_params=_cp(2),
    )(qk3, qk3, proj3, proj3, proj3, gain, gbias, *gather)


def mlstm_cell_bwd(qk3, proj3, gain, gbias, dy3, states, name, exchange=()):
    Bl, S, _ = qk3.shape
    L = M_CHUNK
    SB = min(M_SLAB, S)
    cpb, nsb = SB // L, S // SB
    scale = M_DQK ** -0.5

    def body(q_ref, k_ref, v_ref, o_ref, g_ref, gain_ref, gb_ref, cst_ref, nst_ref, mst_ref, dy_ref,
             dq_ref, dk_ref, dv_ref, do_ref, dg_ref, dgain_ref, dgb_ref, dC_s, dn_s, dgain_s, dgb_s):
        s = pl.program_id(1)

        @pl.when(s == 0)
        def _():
            dC_s[...] = jnp.zeros_like(dC_s)
            dn_s[...] = jnp.zeros_like(dn_s)
            dgain_s[...] = jnp.zeros_like(dgain_s)
            dgb_s[...] = jnp.zeros_like(dgb_s)
        lane = lax.broadcasted_iota(jnp.int32, (L, 128), 1)
        rowi = lax.broadcasted_iota(jnp.int32, (L, 1), 0)
        eye, low, upp = _cell_consts(L)

        def bstep(t, carry):
            c = cpb - 1 - t
            off = pl.multiple_of(c * L, L)
            G = g_ref[0, pl.ds(off, L), :] + gb_ref[...]
            slab = jnp.zeros((L, 128), F32)
            dq_pair = dk_pair = None
            for h in range(HEADS):
                cols = pl.ds(M_DV * h, M_DV)
                gain_h = gain_ref[:, cols]
                C, n, m = cst_ref[0, h, c], nst_ref[0, h, c], mst_ref[0, h, c][:, 0:1]
                dC_n, dn_n = dC_s[h], dn_s[h]
                qmask, qc, kc, vc, gi, gf = _load_chunk(q_ref, k_ref, v_ref, G, off, L, h, lane)
                gi_row, bcum, bcum_row, b_last = _gate_rows(gi, gf, eye, low, upp)
                qs = qc * scale
                r = _chunk_out(qs, kc, vc, gi_row, bcum, bcum_row, low, C, n, m)
                _, _, _, wa, sp, sl = _chunk_state(kc, vc, gi, bcum, b_last, C, n, m)
                hc, den, dn, sc, dm, iw, qn = r["hc"], r["den"], r["dn"], r["sc"], r["dm"], r["iw"], r["qn"]
                qs_b, k_b, v_b = r["qs_b"], r["k_b"], r["v_b"]
                dy = dy_ref[0, pl.ds(off, L), cols].astype(F32)
                oc = o_ref[0, pl.ds(off, L), cols]
                sig_o = _sigmoid(oc)
                rr = lax.rsqrt(jnp.mean(hc * hc, axis=-1, keepdims=True) + RMS_EPS)
                hn = hc * rr
                dgain_s[:, cols] += _sum0(dy * sig_o * hn)
                do_ref[0, pl.ds(off, L), cols] = (
                    dy * hn * gain_h * sig_o * (1.0 - sig_o)).astype(BF16)
                dhn = dy * sig_o * gain_h
                dhc = rr * dhn - hc * (rr * rr * rr) * jnp.mean(dhn * hc, axis=-1, keepdims=True)
                dnum = dhc / dn
                gden = -_sum1(dhc * hc) / dn
                dden = jnp.where(jnp.abs(den) > r["floor"], gden * jnp.sign(den), 0.0)
                dnum_b = dnum.astype(BF16)
                dsc = _dot_nt(dnum_b, v_b) + dden
                dv = _dot_tn(sc.astype(BF16), dnum_b)
                diw = _sum1(dnum * r["qc"]) + dden * qn
                dqc_b = (iw * dnum).astype(BF16)
                wq = iw * dden
                dqs = _dot(dqc_b, C.astype(BF16)) + wq * n
                dC_out = _dot_tn(dqc_b, qs_b)
                dn_out = _sum0(wq * qs)
                dS_b = (dsc * dm).astype(BF16)
                gm = dsc * sc
                dqs = dqs + _dot(dS_b, k_b)
                dk = _dot_tn(dS_b, qs_b)
                dbc = _sum1(gm) + diw * iw
                colg = _sum0(gm)
                dC_p = sp * dC_n + dC_out
                dn_p = sp * dn_n + dn_out
                dcl_b = (sl * dC_n).astype(BF16)
                dn_loc = sl * dn_n
                dsp = _sum1(_sum0(dC_n * C)) + _sum1(dn_n * n)
                db_last = dsp * sp
                t1 = _dot(v_b, dcl_b) + dn_loc
                dwa = _sum1(t1 * kc)
                dv = dv + wa * _dot_nt(k_b, dcl_b)
                dk = dk + wa * t1
                da = dwa * wa
                db_last = db_last + _sum0(da)
                dbc = dbc - da + jnp.where(rowi == L - 1, db_last, 0.0)
                dbc_row = _sum0(jnp.where(eye, dbc, 0.0)) - colg
                dgi = da + _sum1(jnp.where(eye, colg, 0.0))
                dlf = _sum1(jnp.where(upp, dbc_row, 0.0))
                dgf = dlf * _sigmoid(-gf)
                dq = jnp.where(qmask, dqs * scale, 0.0)
                dk = jnp.where(qmask, dk, 0.0)
                slab = slab + jnp.where(lane == h, dgi, 0.0) + jnp.where(lane == h + HEADS, dgf, 0.0)
                dv_ref[0, pl.ds(off, L), cols] = dv.astype(BF16)
                dC_s[h] = dC_p
                dn_s[h] = dn_p
                if h % 2 == 0:
                    dq_pair, dk_pair = dq, dk
                else:
                    pair = pl.ds(128 * (h // 2), 128)
                    dq_ref[0, pl.ds(off, L), pair] = dq_pair + dq
                    dk_ref[0, pl.ds(off, L), pair] = dk_pair + dk
            dg_ref[0, pl.ds(off, L), :] = slab
            dgb_s[...] += _sum0(slab)
            return carry

        lax.fori_loop(0, cpb, bstep, 0)

        @pl.when(s == nsb - 1)
        def _():
            dgain_ref[0] = dgain_s[...]
            dgb_ref[0] = dgb_s[...]

    ins, states_specs, seq = _cell_specs(SB, cpb, lambda s: nsb - 1 - s)
    once = lambda width: pl.BlockSpec((1, 1, width), lambda b, s: (b, 0, 0))
    grid = (Bl, nsb)
    body, c_in, c_out, c_shape, c_scratch = host_comm(body, grid, 11, 7, exchange=exchange)
    return pl.pallas_call(
        body, name=name, grid=grid, in_specs=ins + states_specs + [seq(D, 0)] + c_in,
        out_specs=[seq(D // 2, 0), seq(D // 2, 0), seq(D, 0), seq(D, 0), seq(128, 0), once(D), once(128)] + c_out,
        out_shape=[jax.ShapeDtypeStruct((Bl, S, D // 2), F32), jax.ShapeDtypeStruct((Bl, S, D // 2), F32),
                   jax.ShapeDtypeStruct((Bl, S, D), BF16), jax.ShapeDtypeStruct((Bl, S, D), BF16),
                   jax.ShapeDtypeStruct((Bl, S, 128), F32), jax.ShapeDtypeStruct((Bl, 1, D), F32),
                   jax.ShapeDtypeStruct((Bl, 1, 128), F32)] + c_shape,
        scratch_shapes=[pltpu.VMEM((HEADS, M_DV, 128), F32), pltpu.VMEM((HEADS, 1, 128), F32),
                        pltpu.VMEM((1, D), F32), pltpu.VMEM((1, 128), F32)] + c_scratch,
        compiler_params=_cp(2),
    )(qk3, qk3, proj3, proj3, proj3, gain, gbias, *states, dy3, *exchange)


def _attn_scores(q, kc, kp, n, row, col, scale):
    s_c = jnp.where(col <= row, _dot_nt(q, kc) * scale, NEG)
    s_p = jnp.where(jnp.logical_and(col >= row, n > 0), _dot_nt(q, kp) * scale, NEG)
    return s_c, s_p


def attn_fwd(proj, Bl, S, g, dil, name):
    Sd = S // dil
    nb = Sd // A_BLK
    scale = A_BLK ** -0.5
    pv = proj.reshape(Bl, Sd, dil * A_PROJ)
    ncol = A_PROJ // 128

    def body(q_ref, k_ref, v_ref, o_ref, l_ref):
        row = lax.broadcasted_iota(jnp.int32, (A_BLK, A_BLK), 0)
        col = lax.broadcasted_iota(jnp.int32, (A_BLK, A_BLK), 1)

        def step(n, carry):
            off = pl.multiple_of(n * A_BLK, A_BLK)
            offp = pl.multiple_of(jnp.maximum(n - 1, 0) * A_BLK, A_BLK)
            q = q_ref[0, pl.ds(off, A_BLK), :]
            s_c, s_p = _attn_scores(q, k_ref[0, pl.ds(off, A_BLK), :], k_ref[0, pl.ds(offp, A_BLK), :],
                                    n, row, col, scale)
            m = jnp.maximum(jnp.max(s_c, axis=1, keepdims=True), jnp.max(s_p, axis=1, keepdims=True))
            p_c = jnp.exp(s_c - m)
            p_p = jnp.exp(s_p - m)
            den = _sum1(p_c) + _sum1(p_p)
            o = _dot(p_c.astype(BF16), v_ref[0, pl.ds(off, A_BLK), :]) + _dot(
                p_p.astype(BF16), v_ref[0, pl.ds(offp, A_BLK), :])
            o_ref[0, pl.ds(off, A_BLK), :] = o / den
            l_ref[0, pl.ds(off, A_BLK), :] = jnp.broadcast_to(m + jnp.log(den), (A_BLK, 128))
            return carry

        lax.fori_loop(0, nb, step, 0)

    def spec(j):
        return pl.BlockSpec((1, Sd, 128), lambda b, r, h: (b, 0, r * ncol + g * 24 + j * HEADS + h))

    ospec = pl.BlockSpec((1, Sd, 128), lambda b, r, h: (b, 0, r * HEADS + h))
    o, lse = pl.pallas_call(
        body, name=name, grid=(Bl, dil, HEADS),
        in_specs=[spec(0), spec(1), spec(2)], out_specs=[ospec, ospec],
        out_shape=[jax.ShapeDtypeStruct((Bl, Sd, dil * D), F32)] * 2, compiler_params=_cp(3),
    )(pv, pv, pv)
    return o.reshape(Bl * S, D), lse.reshape(Bl * S, D)


def attn_merge(os_, lses, name):
    T = os_[0].shape[0]
    tm = _tile(T, 512)
    ng = len(os_)

    def body(*refs):
        o_refs, l_refs = refs[:ng], refs[ng:2 * ng]
        ob_ref, of_ref, lt_ref = refs[2 * ng:]
        ls = [r[...] for r in l_refs]
        m = functools.reduce(jnp.maximum, ls)
        ws = [jnp.exp(l - m) for l in ls]
        den = functools.reduce(lambda a, b: a + b, ws)
        o = functools.reduce(lambda a, b: a + b, [w * r[...] for w, r in zip(ws, o_refs)]) / den
        of_ref[...] = o
        ob_ref[...] = o.astype(BF16)
        lt_ref[...] = m + jnp.log(den)

    spec = pl.BlockSpec((tm, D), lambda i: (i, 0))
    return pl.pallas_call(
        body, name=name, grid=(T // tm,), in_specs=[spec] * (2 * ng), out_specs=[spec] * 3,
        out_shape=[jax.ShapeDtypeStruct((T, D), BF16), jax.ShapeDtypeStruct((T, D), F32),
                   jax.ShapeDtypeStruct((T, D), F32)],
        compiler_params=_cp(1),
    )(*os_, *lses)


def attn_bwd(proj, do, o, lse, Bl, S, g, dil, name):
    Sd = S // dil
    nb = Sd // A_BLK
    scale = A_BLK ** -0.5
    pv = proj.reshape(Bl, Sd, dil * A_PROJ)
    ncol = A_PROJ // 128
    dov, ov, lv = (t.reshape(Bl, Sd, dil * D) for t in (do, o, lse))

    def body(q_ref, k_ref, v_ref, do_ref, o_ref, l_ref, dq_ref, dk_ref, dv_ref, dk_s, dv_s):
        row = lax.broadcasted_iota(jnp.int32, (A_BLK, A_BLK), 0)
        col = lax.broadcasted_iota(jnp.int32, (A_BLK, A_BLK), 1)
        dk_s[...] = jnp.zeros_like(dk_s)
        dv_s[...] = jnp.zeros_like(dv_s)

        def step(n, carry):
            off = pl.multiple_of(n * A_BLK, A_BLK)
            offp = pl.multiple_of(jnp.maximum(n - 1, 0) * A_BLK, A_BLK)
            q = q_ref[0, pl.ds(off, A_BLK), :]
            kc, kp = k_ref[0, pl.ds(off, A_BLK), :], k_ref[0, pl.ds(offp, A_BLK), :]
            vc, vp = v_ref[0, pl.ds(off, A_BLK), :], v_ref[0, pl.ds(offp, A_BLK), :]
            do_b = do_ref[0, pl.ds(off, A_BLK), :]
            delta = _sum1(do_b.astype(F32) * o_ref[0, pl.ds(off, A_BLK), :])
            lt = l_ref[0, pl.ds(off, A_BLK), :][:, 0:1]
            s_c, s_p = _attn_scores(q, kc, kp, n, row, col, scale)
            p_c = jnp.exp(s_c - lt)
            p_p = jnp.exp(s_p - lt)
            ds_c = (p_c * (_dot_nt(do_b, vc) - delta) * scale).astype(BF16)
            ds_p = (p_p * (_dot_nt(do_b, vp) - delta) * scale).astype(BF16)
            dq_ref[0, pl.ds(off, A_BLK), :] = (_dot(ds_c, kc) + _dot(ds_p, kp)).astype(BF16)
            dk_s[pl.ds(off, A_BLK), :] += _dot_tn(ds_c, q)
            dk_s[pl.ds(offp, A_BLK), :] += _dot_tn(ds_p, q)
            dv_s[pl.ds(off, A_BLK), :] += _dot_tn(p_c.astype(BF16), do_b)
            dv_s[pl.ds(offp, A_BLK), :] += _dot_tn(p_p.astype(BF16), do_b)
            return carry

        lax.fori_loop(0, nb, step, 0)
        dk_ref[0] = dk_s[...].astype(BF16)
        dv_ref[0] = dv_s[...].astype(BF16)

    def spec(j):
        return pl.BlockSpec((1, Sd, 128), lambda b, r, h: (b, 0, r * ncol + g * 24 + j * HEADS + h))

    ospec = pl.BlockSpec((1, Sd, 128), lambda b, r, h: (b, 0, r * HEADS + h))
    outs = pl.pallas_call(
        body, name=name, grid=(Bl, dil, HEADS),
        in_specs=[spec(0), spec(1), spec(2), ospec, ospec, ospec], out_specs=[ospec] * 3,
        out_shape=[jax.ShapeDtypeStruct((Bl, Sd, dil * D), BF16)] * 3,
        scratch_shapes=[pltpu.VMEM((Sd, 128), F32), pltpu.VMEM((Sd, 128), F32)],
        compiler_params=_cp(3),
    )(pv, pv, pv, dov, ov, lv)
    return [t.reshape(Bl * S, D) for t in outs]


def _as_slots(pair, shape):
    return tuple(t.reshape(shape) for t in pair)


def ffn_fwd(x, mod3, w_in, w_out, lng, lnb, tag, gather=()):
    a, g, u, h, gathered = ffn_in(x, mod3, w_in, tag + "_in", gather=gather)
    out, xn = proj_post(a, w_out, x, mod3, lng, lnb, 0.5, tag + "_out")
    return xn, (x, out, g, u, h, a), gathered


def ffn_bwd(dxn, saved, mod3, w_in, w_out, lng, tag, exchange=()):
    x, out, g, u, h, a = saved
    dxres, dout, dgu, dlg, dlb, dgate, *received = post_bwd(dxn, x, out, mod3, lng, w_out, 0.5, tag + "_outb",
                                                            tm=256, gu=(g, u), exchange=exchange)
    dx, dsh, dsc = modmm_bwd(dgu, w_in, x, mod3, dxres, tag + "_inb", tm=256)
    dw_in = mm_tn(h, dgu, tag + "_dwin")
    dw_out = _as_slots(mm_tn(a, dout, tag + "_dwout", bw=D), (N_DEV, D_FF // N_DEV, D))
    dmod3 = jnp.concatenate([dsh, dsc, dgate], axis=1)
    return dx, [dw_in, dw_out], dlg, dlb, dmod3, (received[0] if received else [])


def mlstm_fwd(x, mod3, w_in, w_out, conv_w, gain, gbias, lng, lnb, Bl, S, gather=()):
    proj, h = modmm(x, mod3, w_in, F32, "ml_in", tn=M_PROJ_PAD // 5)
    proj3 = proj.reshape(Bl, S, M_PROJ_PAD)
    qk3 = conv_silu(proj3, conv_w, "ml_conv")
    y3, *rest = mlstm_cell_fwd(qk3, proj3, gain, gbias, "ml_cell", gather=gather)
    states, gathered = rest[:3], rest[3:]
    y = y3.reshape(Bl * S, D)
    out, xn = proj_post(y[None], w_out, x, mod3, lng, lnb, 1.0, "ml_out")
    return xn, (x, out, h, proj3, qk3, y, states), gathered


def mlstm_bwd(dxn, saved, mod3, w_in, w_out, conv_w, gain, gbias, lng, Bl, S, exchange=()):
    x, out, h, proj3, qk3, y, states = saved
    dxres, dout, dy, dlg, dlb, dgate = post_bwd(dxn, x, out, mod3, lng, w_out, 1.0, "ml_outb")
    dq, dk, dv, do, dg, dgain, dgb, *received = mlstm_cell_bwd(qk3, proj3, gain, gbias, dy.reshape(Bl, S, D),
                                                               states, "ml_cellb", exchange=exchange)
    dqk, dconv = conv_silu_bwd(proj3, conv_w, dq, dk, "ml_convb")
    dproj = jnp.concatenate([dqk, dv, do, dg.astype(BF16)], axis=2).reshape(Bl * S, M_PROJ_PAD)
    dx, dsh, dsc = modmm_bwd(dproj, w_in, x, mod3, dxres, "ml_inb", tn=M_PROJ_PAD // 5)
    dwi, _ = mm_tn(h, dproj, "ml_dwin", bw=M_PROJ_PAD // 5)
    dwi = _restack(jnp.moveaxis(dwi, 0, 1).reshape(D, M_PROJ_PAD)[:, :M_PROJ], 1)
    dw_out = _as_slots(mm_tn(y, dout, "ml_dwout", bw=D), (N_DEV, D // N_DEV, D))
    small = (jnp.sum(dconv, axis=0), jnp.sum(dgain, axis=0), jnp.sum(dgb, axis=0)[:, :2 * HEADS])
    dmod3 = jnp.concatenate([dsh, dsc, dgate], axis=1)
    return dx, [(dwi, dwi.astype(BF16)), dw_out], dlg, dlb, dmod3, small, received


def attn_mixer_fwd(x, mod3, w_in, w_out, lng, lnb, Bl, S):
    proj, h = modmm(x, mod3, w_in, BF16, "at_in")
    os_, lses = [], []
    for g, (_, dil) in enumerate(DIL_GROUPS):
        o_g, l_g = attn_fwd(proj, Bl, S, g, dil, "at_core%d" % g)
        os_.append(o_g)
        lses.append(l_g)
    ob, of, lt = attn_merge(os_, lses, "at_merge")
    out, xn = proj_post(ob[None], w_out, x, mod3, lng, lnb, 1.0, "at_out")
    return xn, (x, out, h, proj, ob, of, lt)


def attn_mixer_bwd(dxn, saved, mod3, w_in, w_out, lng, Bl, S):
    x, out, h, proj, ob, of, lt = saved
    dxres, dout, do, dlg, dlb, dgate = post_bwd(dxn, x, out, mod3, lng, w_out, 1.0, "at_outb")
    do = do[0]
    parts = []
    for g, (_, dil) in enumerate(DIL_GROUPS):
        parts += attn_bwd(proj, do, of, lt, Bl, S, g, dil, "at_coreb%d" % g)
    dproj = jnp.concatenate(parts, axis=1)
    dx, dsh, dsc = modmm_bwd(dproj, w_in, x, mod3, dxres, "at_inb")
    dw_in = mm_tn(h, dproj, "at_dwin", bw=w_in.shape[2])
    dw_out = _as_slots(mm_tn(ob, dout, "at_dwout", bw=D), (N_DEV, D // N_DEV, D))
    return dx, [dw_in, dw_out], dlg, dlb, jnp.concatenate([dsh, dsc, dgate], axis=1)


def _unstack(stacked, axis):
    full = jnp.moveaxis(stacked, 0, axis)
    shp = list(full.shape)
    shp[axis:axis + 2] = [shp[axis] * shp[axis + 1]]
    return full.reshape(shp)


def _restack(full, axis):
    shp = list(full.shape)
    shp[axis:axis + 1] = [N_DEV, shp[axis] // N_DEV]
    return jnp.moveaxis(full.reshape(shp), axis, 0)


def kernel(x, c, ada_w, ada_b, ln_g, ln_b, ffn_w_in, ffn_w_out, mlstm_w_in, mlstm_gate_bias, mlstm_conv_w, mlstm_head_gain, mlstm_w_out, attn_w_in, attn_w_out, loss_target, m_ada_w, m_ada_b, m_ln_g, m_ln_b, m_ffn_w_in, m_ffn_w_out, m_mlstm_w_in, m_mlstm_gate_bias, m_mlstm_conv_w, m_mlstm_head_gain, m_mlstm_w_out, m_attn_w_in, m_attn_w_out, v_ada_w, v_ada_b, v_ln_g, v_ln_b, v_ffn_w_in, v_ffn_w_out, v_mlstm_w_in, v_mlstm_gate_bias, v_mlstm_conv_w, v_mlstm_head_gain, v_mlstm_w_out, v_attn_w_in, v_attn_w_out):
    Bl, S, _ = x.shape
    T = Bl * S
    Bg = Bl * N_DEV
    me = 4 * lax.axis_index("x") + 2 * lax.axis_index("y") + lax.axis_index("c")
    onehot = (jnp.arange(N_DEV) == me).astype(F32)
    weights = dict(ada_w=ada_w, ada_b=ada_b, ln_g=ln_g, ln_b=ln_b, ffn_w_in=ffn_w_in, ffn_w_out=ffn_w_out,
                   mlstm_w_in=mlstm_w_in, mlstm_gate_bias=mlstm_gate_bias, mlstm_conv_w=mlstm_conv_w,
                   mlstm_head_gain=mlstm_head_gain, mlstm_w_out=mlstm_w_out, attn_w_in=attn_w_in,
                   attn_w_out=attn_w_out)
    m_in = dict(ada_w=m_ada_w, ada_b=m_ada_b, ln_g=m_ln_g, ln_b=m_ln_b, ffn_w_in=m_ffn_w_in,
                ffn_w_out=m_ffn_w_out, mlstm_w_in=m_mlstm_w_in, mlstm_gate_bias=m_mlstm_gate_bias,
                mlstm_conv_w=m_mlstm_conv_w, mlstm_head_gain=m_mlstm_head_gain, mlstm_w_out=m_mlstm_w_out,
                attn_w_in=m_attn_w_in, attn_w_out=m_attn_w_out)
    v_in = dict(ada_w=v_ada_w, ada_b=v_ada_b, ln_g=v_ln_g, ln_b=v_ln_b, ffn_w_in=v_ffn_w_in,
                ffn_w_out=v_ffn_w_out, mlstm_w_in=v_mlstm_w_in, mlstm_gate_bias=v_mlstm_gate_bias,
                mlstm_conv_w=v_mlstm_conv_w, mlstm_head_gain=v_mlstm_head_gain, mlstm_w_out=v_mlstm_w_out,
                attn_w_in=v_attn_w_in, attn_w_out=v_attn_w_out)

    mixer = ("mlstm", "attn")
    shards = [[ffn_w_in[layer, 0], ffn_w_in[layer, 1], ffn_w_out[layer, 0], ffn_w_out[layer, 1],
               weights[mixer[layer] + "_w_in"][0], weights[mixer[layer] + "_w_out"][0]] for layer in range(DEPTH)]
    sends = [[s.astype(BF16) for s in layer_shards] for layer_shards in shards]
    small = jnp.concatenate([c.reshape(-1), ln_g.reshape(-1), ln_b.reshape(-1), mlstm_conv_w.reshape(-1)])
    n_small = small.shape[0]
    small = jnp.pad(small, (0, -n_small % (8 * PACK_COLS))).reshape(-1, PACK_COLS)

    def gathered_weights(g):
        return ((g[0], g[1]), (g[2].reshape(4, D_FF // 4, D), g[3].reshape(4, D_FF // 4, D)), g[4],
                g[5].reshape(1, D, D))

    first_in, first_out, small_all = all_gather([sends[0][0], sends[0][2], small], "ag_params")
    full = [None, None]
    small_flat = small_all.reshape(N_DEV, -1)
    o0 = 0
    c_all = small_flat[:, o0:o0 + c.size].reshape(Bg, D)
    o0 += c.size
    lng_full = _unstack(small_flat[:, o0:o0 + ln_g.size].reshape((N_DEV,) + ln_g.shape), 2)
    o0 += ln_g.size
    lnb_full = _unstack(small_flat[:, o0:o0 + ln_b.size].reshape((N_DEV,) + ln_b.shape), 2)
    o0 += ln_b.size
    conv_full = _unstack(small_flat[:, o0:o0 + mlstm_conv_w.size].reshape((N_DEV,) + mlstm_conv_w.shape), 2)[0]
    gbias =jnp.pad(mlstm_gate_bias, ((0, 0), (0, 128 - 2 * HEADS)))

    ncols = ada_w.shape[2]
    ada_b_cols = lax.dynamic_slice_in_dim(ada_b, me * ncols, ncols, axis=1).reshape(DEPTH, 1, ncols)
    mod_cols = ada_fwd(c_all, ada_w, ada_b_cols, "ada_fwd")
    (mod_g,) = all_gather([mod_cols.reshape(DEPTH * Bg, ncols)], "ag_mod")
    mod_full = _unstack(mod_g.reshape(N_DEV, DEPTH, Bg, ncols), 2)
    mod_mine = lax.dynamic_slice_in_dim(mod_full, me * Bl, Bl, axis=1).reshape(DEPTH, Bl, 3, 3, D)

    xt = x.reshape(T, D)
    saved = []
    for layer in range(DEPTH):
        def lnp(s, layer=layer):
            return lng_full[layer, s].reshape(1, D), lnb_full[layer, s].reshape(1, D)
        md = mod_mine[layer]
        if layer == 0:
            xt, sv0, late = ffn_fwd(xt, md[:, 0], first_in, first_out.reshape(4, D_FF // 4, D), *lnp(0), "f0a",
                                    gather=[sends[0][i] for i in (1, 3, 4, 5)])
            full[0] = gathered_weights([first_in, late[0], first_out, late[1], late[2], late[3]])
            mw_in = jnp.pad(_unstack(full[0][2], 1), ((0, 0), (0, M_PROJ_PAD - M_PROJ)))
        else:
            xt, sv0, _ = ffn_fwd(xt, md[:, 0], full[layer][0][0], full[layer][1][0], *lnp(0), "f%da" % layer)
        f_in, f_out, mix_in, mix_out = full[layer]
        if layer % 2 == 0:
            xt, sv1, g1 = mlstm_fwd(xt, md[:, 1], mw_in, mix_out, conv_full, mlstm_head_gain, gbias, *lnp(1), Bl, S,
                                    gather=sends[1])
            full[1] = gathered_weights(g1)
        else:
            xt, sv1 = attn_mixer_fwd(xt, md[:, 1], mix_in, mix_out, *lnp(1), Bl, S)
        xt, sv2, _ = ffn_fwd(xt, md[:, 2], f_in[1], f_out[1], *lnp(2), "f%db" % layer)
        saved.append((sv0, sv1, sv2))

    dxt, lsum = loss_head(xt, loss_target.reshape(T, D), "loss")
    loss = lax.psum(lsum[0, 0], MESH_AXES)

    dmod, dlg_all, dlb_all = [None] * DEPTH, [None] * DEPTH, [None] * DEPTH
    wgrads = [None] * DEPTH
    recvs = [[None] * 6 for _ in range(DEPTH)]
    ml_small = None
    for layer in reversed(range(DEPTH)):
        md = mod_mine[layer]
        f_in, f_out, mix_in, mix_out = full[layer]
        sv0, sv1, sv2 = saved[layer]
        dxt, dw2, dlg2, dlb2, dm2, _ = ffn_bwd(dxt, sv2, md[:, 2], f_in[1], f_out[1],
                                               lng_full[layer, 2].reshape(1, D), "f%db" % layer)
        lg1 = lng_full[layer, 1].reshape(1, D)
        if layer % 2 == 0:
            dxt, dw1, dlg1, dlb1, dm1, ml_small, got = mlstm_bwd(
                dxt, sv1, md[:, 1], mw_in, mix_out, conv_full, mlstm_head_gain, gbias, lg1, Bl, S,
                exchange=[b16 for _, b16 in wgrads[1]] + [dw2[0][1], dw2[1][1]])
            recvs[1], recvs[0][1], recvs[0][3] = got[:6], got[6], got[7]
            dxt, dw0, dlg0, dlb0, dm0, got = ffn_bwd(dxt, sv0, md[:, 0], f_in[0], f_out[0],
                                                     lng_full[layer, 0].reshape(1, D), "f%da" % layer,
                                                     exchange=[dw1[0][1], dw1[1][1]])
            recvs[0][4], recvs[0][5] = got
        else:
            dxt, dw1, dlg1, dlb1, dm1 = attn_mixer_bwd(dxt, sv1, md[:, 1], mix_in, mix_out, lg1, Bl, S)
            dxt, dw0, dlg0, dlb0, dm0, _ = ffn_bwd(dxt, sv0, md[:, 0], f_in[0], f_out[0],
                                                   lng_full[layer, 0].reshape(1, D), "f%da" % layer)
        wgrads[layer] = [dw0[0], dw2[0], dw0[1], dw2[1], dw1[0], dw1[1]]
        dmod[layer] = jnp.stack([dm0, dm1, dm2], axis=1).reshape(Bl, 9 * D)
        dlg_all[layer] = jnp.concatenate([dlg0, dlg1, dlg2], axis=0)
        dlb_all[layer] = jnp.concatenate([dlb0, dlb1, dlb2], axis=0)
    grad_x = dxt.reshape(Bl, S, D)
    recvs[0][0], recvs[0][2] = exchange_shards([wgrads[0][0][1], wgrads[0][2][1]], "rs_grads")

    gsh = [[shard_sum(lax.dynamic_index_in_dim(f32, me, axis=0, keepdims=False), recv, onehot,
                      "rs_sum%d_%d" % (layer, i))
            for i, ((f32, _), recv) in enumerate(zip(wgrads[layer], recvs[layer]))] for layer in range(DEPTH)]
    grads = {"ffn_w_in": jnp.stack([jnp.stack(g[0:2]) for g in gsh]),
             "ffn_w_out": jnp.stack([jnp.stack(g[2:4]) for g in gsh]),
             "mlstm_w_in": gsh[0][4][None], "mlstm_w_out": gsh[0][5][None],
             "attn_w_in": gsh[1][4][None], "attn_w_out": gsh[1][5][None]}

    dconv, dgain, dgbias = ml_small
    parts = [jnp.stack(dmod).reshape(-1), dgbias.reshape(-1), dgain.reshape(-1),
             jnp.stack(dlg_all).reshape(-1), jnp.stack(dlb_all).reshape(-1), dconv.reshape(-1)]
    sizes = [p.shape[0] for p in parts]
    flat = jnp.concatenate(parts)
    flat = jnp.pad(flat, (0, -flat.shape[0] % (8 * PACK_COLS))).reshape(-1, PACK_COLS)
    (sm_all,) = all_gather([flat], "ag_small")
    sm_sum = sum_leading(sm_all, "small_sum").reshape(-1)
    dmod_all = sm_all.reshape(N_DEV, -1)[:, :sizes[0]].reshape(N_DEV, DEPTH, Bl, 9 * D)
    dmod_all = jnp.moveaxis(dmod_all, 0, 1).reshape(DEPTH, Bg, 9 * D)
    o0 = sizes[0]
    grads["mlstm_gate_bias"] = sm_sum[o0:o0 + sizes[1]].reshape(mlstm_gate_bias.shape)
    o0 += sizes[1]
    grads["mlstm_head_gain"] = sm_sum[o0:o0 + sizes[2]].reshape(mlstm_head_gain.shape)
    o0 += sizes[2]
    nl = ln_g.shape[2]
    g_lng = sm_sum[o0:o0 + sizes[3]].reshape(DEPTH, 3, D)
    o0 += sizes[3]
    g_lnb = sm_sum[o0:o0 + sizes[4]].reshape(DEPTH, 3, D)
    o0 += sizes[4]
    g_conv = sm_sum[o0:o0 + sizes[5]].reshape(1, 4, D)
    grads["ln_g"] = lax.dynamic_slice_in_dim(g_lng, me * nl, nl, axis=2)
    grads["ln_b"] = lax.dynamic_slice_in_dim(g_lnb, me * nl, nl, axis=2)
    grads["mlstm_conv_w"] = lax.dynamic_slice_in_dim(g_conv, me * nl, nl, axis=2)
    dmod_cols = lax.dynamic_slice_in_dim(dmod_all, me * ncols, ncols, axis=2)
    gw, gb = ada_bwd(c_all.T, dmod_cols, dmod_all, "ada_bwd")
    grads["ada_w"] = gw
    grads["ada_b"] = gb.reshape(ada_b.shape)

    names = ["ada_w", "ada_b", "ln_g", "ln_b", "ffn_w_in", "ffn_w_out", "mlstm_w_in", "mlstm_gate_bias",
             "mlstm_conv_w", "mlstm_head_gain", "mlstm_w_out", "attn_w_in", "attn_w_out"]
    deltas, new_m, new_v = [], [], []
    for k in names:
        w = weights[k]
        shp2 = (math.prod(w.shape[:-1]), w.shape[-1])
        d_, m_, v_ = adamw(w.reshape(shp2), grads[k].reshape(shp2), m_in[k].reshape(shp2), v_in[k].reshape(shp2),
                           "adamw_" + k)
        deltas.append(d_.reshape(w.shape))
        new_m.append(m_.reshape(w.shape))
        new_v.append(v_.reshape(w.shape))
    return (loss, grad_x, *[grads[k] for k in names], *deltas, *new_m, *new_v)
```

```python
import functools
import math

import jax
import jax.numpy as jnp
from jax import lax
from jax.experimental import pallas as pl
from jax.experimental.pallas import tpu as pltpu

F32 = jnp.float32
BF16 = jnp.bfloat16

N_DEV = 8
MESH_AXES = ("x", "y", "c")
D = 1024
DEPTH = 2
D_FF = 2816
HEADS = 8
M_DQK = 64
M_DV = 128
M_CHUNK = 64
M_SLAB = 512
M_PROJ = 3088
M_PROJ_PAD = 3200
A_PROJ = 9216
DIL_GROUPS = ((128, 1), (512, 4), (2048, 16))
A_BLK = 128
ALPHA = (2 * DEPTH) ** 0.25
LN_EPS = 1e-5
RMS_EPS = 1e-6
ADAM_LR = 0.001
ADAM_B1 = 0.9
ADAM_B2 = 0.999
ADAM_EPS = 1e-08
ADAM_WD = 0.01
ADAM_STEP = 10
NEG = -1e30
V7X_VMEM_LIMIT = 56 * 1024 * 1024
PACK_COLS = 1024
MESH_ID = pl.DeviceIdType.MESH
ANY_SPEC = pl.BlockSpec(memory_space=pl.ANY)


def _cp(n_axes):
    return pltpu.CompilerParams(dimension_semantics=("arbitrary",) * n_axes,
                                vmem_limit_bytes=V7X_VMEM_LIMIT)


def _dot(a, b):
    return jnp.dot(a, b, preferred_element_type=F32)


def _dot_nt(a, b):
    return lax.dot_general(a, b, (((1,), (1,)), ((), ())), preferred_element_type=F32)


def _dot_tn(a, b):
    return lax.dot_general(a, b, (((0,), (0,)), ((), ())), preferred_element_type=F32)


def _sum0(a):
    return jnp.sum(a, axis=0, keepdims=True)


def _sum1(a):
    return jnp.sum(a, axis=1, keepdims=True)


def _round(a):
    return a.astype(BF16).astype(F32)


def _sigmoid(a):
    return 1.0 / (1.0 + jnp.exp(-a))


def _tile(n, pref):
    t = min(n, pref)
    while n % t:
        t //= 2
    return t


def all_gather(arrs, name):
    n = len(arrs)

    def body(*refs):
        gather = Gather(refs[:n], refs[n:2 * n], *refs[2 * n:])
        gather.start()
        gather.finish()

    return pl.pallas_call(
        body, name=name, out_shape=Gather.out_shape(arrs),
        in_specs=[ANY_SPEC] * n, out_specs=[ANY_SPEC] * n, scratch_shapes=Gather.scratch(n),
    )(*arrs)


class Gather:
    def __init__(self, ins, outs, send_sems, recv_sems, local_sems):
        x, y, c = lax.axis_index("x"), lax.axis_index("y"), lax.axis_index("c")
        me, sibling = (x, y, c), (x, y, 1 - c)
        chips = [(1 - x, y), (x, 1 - y), (1 - x, 1 - y)]

        def slot(a, p):
            return outs[a].at[4 * p[0] + 2 * p[1] + p[2]]

        def copy(a, k, block, to, src=None):
            return pltpu.make_async_remote_copy(
                src_ref=slot(a, block) if src is None else src, dst_ref=slot(a, block),
                send_sem=send_sems.at[7 * a + k], recv_sem=recv_sems.at[7 * a + k],
                device_id=to, device_id_type=MESH_ID)

        n = len(ins)
        self.mine = [pltpu.make_async_copy(ins[a], slot(a, me), local_sems.at[a]) for a in range(n)]
        self.first, self.over_ici, self.passed, self.from_sibling = [], [], [], []
        for a in range(n):
            self.first.append(copy(a, 0, me, sibling, src=ins[a]))
            self.from_sibling.append(copy(a, 0, sibling, me))
            for j, chip in enumerate(chips):
                self.first.append(copy(a, 1 + j, me, (*chip, c), src=ins[a]))
                self.over_ici.append(copy(a, 1 + j, (*chip, c), me))
                self.passed.append(copy(a, 4 + j, (*chip, c), sibling))
                self.from_sibling.append(copy(a, 4 + j, (*chip, 1 - c), me))

    @staticmethod
    def out_shape(arrs):
        return [jax.ShapeDtypeStruct((N_DEV,) + a.shape, a.dtype) for a in arrs]

    @staticmethod
    def scratch(n):
        return [pltpu.SemaphoreType.DMA((7 * n,)), pltpu.SemaphoreType.DMA((7 * n,)),
                pltpu.SemaphoreType.DMA((n,))]

    def start(self):
        for cp in self.mine + self.first:
            cp.start()

    def finish(self):
        for landed, onward in zip(self.over_ici, self.passed):
            landed.wait_recv()
            onward.start()
        for cp in self.from_sibling:
            cp.wait_recv()
        for cp in self.first + self.passed:
            cp.wait_send()
        for cp in self.mine:
            cp.wait()


class Exchange:
    def __init__(self, sends, recvs, send_sems, recv_sems, local_sems):
        x, y, c = lax.axis_index("x"), lax.axis_index("y"), lax.axis_index("c")
        me = 4 * x + 2 * y + c
        self.own = [pltpu.make_async_copy(s.at[me], r.at[me], local_sems.at[a])
                    for a, (s, r) in enumerate(zip(sends, recvs))]
        self.copies = []
        for a, (s_ref, r_ref) in enumerate(zip(sends, recvs)):
            for k in range(1, N_DEV):
                px = 1 - x if (k >> 2) & 1 else x
                py = 1 - y if (k >> 1) & 1 else y
                pc = 1 - c if k & 1 else c
                self.copies.append(pltpu.make_async_remote_copy(
                    src_ref=s_ref.at[4 * px + 2 * py + pc], dst_ref=r_ref.at[me],
                    send_sem=send_sems.at[7 * a + k - 1], recv_sem=recv_sems.at[7 * a + k - 1],
                    device_id=(px, py, pc), device_id_type=MESH_ID))

    @staticmethod
    def scratch(n):
        return [pltpu.SemaphoreType.DMA((7 * n,)), pltpu.SemaphoreType.DMA((7 * n,)),
                pltpu.SemaphoreType.DMA((n,))]

    def start(self):
        for cp in self.own + self.copies:
            cp.start()

    def finish(self):
        for cp in self.copies:
            cp.wait_send()
            cp.wait_recv()
        for cp in self.own:
            cp.wait()


def host_comm(body, grid, n_in, n_out, gather=(), exchange=()):
    ng, nx = len(gather), len(exchange)
    if ng + nx == 0:
        return body, [], [], [], []

    def hosted(*refs):
        ins, c_in, rest = refs[:n_in], refs[n_in:n_in + ng + nx], refs[n_in + ng + nx:]
        outs, c_out, rest = rest[:n_out], rest[n_out:n_out + ng + nx], rest[n_out + ng + nx:]
        n_sems = 3 * ((ng > 0) + (nx > 0))
        scratch, sems = rest[:len(rest) - n_sems], rest[len(rest) - n_sems:]

        def comms():
            made = [Gather(c_in[:ng], c_out[:ng], *sems[:3])] if ng else []
            return made + ([Exchange(c_in[ng:], c_out[ng:], *sems[-3:])] if nx else [])

        ids = [pl.program_id(a) for a in range(len(grid))]

        @pl.when(functools.reduce(jnp.logical_and, [i == 0 for i in ids]))
        def _():
            for cm in comms():
                cm.start()
        body(*ins, *outs, *scratch)

        @pl.when(functools.reduce(jnp.logical_and, [i == g - 1 for i, g in zip(ids, grid)]))
        def _():
            for cm in comms():
                cm.finish()

    shapes = Gather.out_shape(gather) + [jax.ShapeDtypeStruct(a.shape, a.dtype) for a in exchange]
    scratch = (Gather.scratch(ng) if ng else []) + (Exchange.scratch(nx) if nx else [])
    return hosted, [ANY_SPEC] * (ng + nx), [ANY_SPEC] * (ng + nx), shapes, scratch


def exchange_shards(sends, name):
    n = len(sends)

    def body(*refs):
        exchange = Exchange(refs[:n], refs[n:2 * n], *refs[2 * n:])
        exchange.start()
        exchange.finish()

    return pl.pallas_call(
        body, name=name, out_shape=[jax.ShapeDtypeStruct(s.shape, s.dtype) for s in sends],
        in_specs=[ANY_SPEC] * n, out_specs=[ANY_SPEC] * n, scratch_shapes=Exchange.scratch(n),
    )(*sends)


def shard_sum(own, recv, onehot, name):
    R, C = own.shape
    tr = _tile(R, 512)

    def body(oh_ref, own_ref, recv_ref, o_ref):
        acc = None
        for j in range(N_DEV):
            term = jnp.where(oh_ref[j] > 0.5, own_ref[...], recv_ref[j].astype(F32))
            acc = term if acc is None else acc + term
        o_ref[...] = acc

    return pl.pallas_call(
        body, name=name, grid=(R // tr,),
        in_specs=[pl.BlockSpec(memory_space=pltpu.SMEM),
                  pl.BlockSpec((tr, C), lambda i: (i, 0)),
                  pl.BlockSpec((N_DEV, tr, C), lambda i: (0, i, 0))],
        out_specs=pl.BlockSpec((tr, C), lambda i: (i, 0)),
        out_shape=jax.ShapeDtypeStruct((R, C), F32), compiler_params=_cp(1),
    )(onehot, own, recv)


def sum_leading(a, name):
    _, R, C = a.shape

    def body(a_ref, o_ref):
        acc = a_ref[0]
        for j in range(1, N_DEV):
            acc = acc + a_ref[j]
        o_ref[...] = acc

    return pl.pallas_call(body, name=name, out_shape=jax.ShapeDtypeStruct((R, C), F32),
                          compiler_params=_cp(0))(a)


def _col_chunks(w, tn):
    if w.ndim == 3:
        return w.shape[0], w.shape[2], pl.BlockSpec((None, w.shape[1], w.shape[2]), lambda i, j: (j, 0, 0))
    return w.shape[1] // tn, tn, pl.BlockSpec((w.shape[0], tn), lambda i, j: (0, j))


def modmm(x, mod3, w, out_dtype, name, tn=None):
    T, Dm = x.shape
    nj, tn, w_spec = _col_chunks(w, tn)
    N = nj * tn
    Bl = mod3.shape[0]
    tm = _tile(T // Bl, 1024)
    tpb = T // Bl // tm

    def body(x_ref, mod_ref, w_ref, o_ref, h_ref, hs):
        @pl.when(pl.program_id(1) == 0)
        def _():
            m = mod_ref[0]
            hs[...] = (x_ref[...] * (1.0 + m[1:2, :]) + m[0:1, :]).astype(BF16)
            h_ref[...] = hs[...]
        o_ref[...] = _dot(hs[...], w_ref[...]).astype(o_ref.dtype)

    return pl.pallas_call(
        body, name=name, grid=(T // tm, nj),
        in_specs=[pl.BlockSpec((tm, Dm), lambda i, j: (i, 0)),
                  pl.BlockSpec((1, 3, Dm), lambda i, j: (i // tpb, 0, 0)), w_spec],
        out_specs=[pl.BlockSpec((tm, tn), lambda i, j: (i, j)),
                   pl.BlockSpec((tm, Dm), lambda i, j: (i, 0))],
        out_shape=[jax.ShapeDtypeStruct((T, N), out_dtype), jax.ShapeDtypeStruct((T, Dm), BF16)],
        scratch_shapes=[pltpu.VMEM((tm, Dm), BF16)], compiler_params=_cp(2),
    )(x, mod3, w)


def modmm_bwd(dp, w, x, mod3, dxres, name, tn=None, tm=512):
    T, Dm = x.shape
    Bl = mod3.shape[0]
    tm = _tile(T // Bl, tm)
    tpb = T // Bl // tm
    resident = dp.ndim == 3
    if resident:
        nc, nj = dp.shape[0], 1
        dp_spec = pl.BlockSpec((nc, tm, dp.shape[2]), lambda i, j: (0, i, 0))
        w_spec = pl.BlockSpec(w.shape, lambda i, j: (0, 0, 0))
    else:
        nj, tn, w_spec = _col_chunks(w, tn)
        dp_spec = pl.BlockSpec((tm, tn), lambda i, j: (i, j))

    def body(dp_ref, w_ref, x_ref, mod_ref, dxr_ref, dx_ref, dsh_ref, dsc_ref, acc):
        i, j = pl.program_id(0), pl.program_id(1)

        @pl.when(j == 0)
        def _():
            acc[...] = jnp.zeros_like(acc)
        if resident:
            for c in range(nc):
                acc[...] += _dot_nt(dp_ref[c], w_ref[c])
        else:
            acc[...] += _dot_nt(dp_ref[...], w_ref[...])

        @pl.when(j == nj - 1)
        def _():
            dh = acc[...]
            xx = x_ref[...]
            dx_ref[...] = dxr_ref[...] + dh * (1.0 + mod_ref[0][1:2, :])

            @pl.when(i % tpb == 0)
            def _():
                dsh_ref[...] = jnp.zeros_like(dsh_ref)
                dsc_ref[...] = jnp.zeros_like(dsc_ref)
            dsh_ref[0] += _sum0(dh)
            dsc_ref[0] += _sum0(dh * xx)

    return pl.pallas_call(
        body, name=name, grid=(T // tm, nj),
        in_specs=[dp_spec, w_spec,
                  pl.BlockSpec((tm, Dm), lambda i, j: (i, 0)),
                  pl.BlockSpec((1, 3, Dm), lambda i, j: (i // tpb, 0, 0)),
                  pl.BlockSpec((tm, Dm), lambda i, j: (i, 0))],
        out_specs=[pl.BlockSpec((tm, Dm), lambda i, j: (i, 0)),
                   pl.BlockSpec((1, 1, Dm), lambda i, j: (i // tpb, 0, 0)),
                   pl.BlockSpec((1, 1, Dm), lambda i, j: (i // tpb, 0, 0))],
        out_shape=[jax.ShapeDtypeStruct((T, Dm), F32), jax.ShapeDtypeStruct((Bl, 1, Dm), F32),
                   jax.ShapeDtypeStruct((Bl, 1, Dm), F32)],
        scratch_shapes=[pltpu.VMEM((tm, Dm), F32)], compiler_params=_cp(2),
    )(dp, w, x, mod3, dxres)


def _ln_stats(z):
    mu = jnp.mean(z, axis=-1, keepdims=True)
    zc = z - mu
    var = jnp.mean(zc * zc, axis=-1, keepdims=True)
    rstd = lax.rsqrt(var + LN_EPS)
    return zc * rstd, rstd


def proj_post(a, w, x, mod3, lng, lnb, weight, name):
    nk, T, tk = a.shape
    Dm = w.shape[2]
    Bl = mod3.shape[0]
    tm = _tile(T // Bl, 512)
    tpb = T // Bl // tm

    def body(a_ref, w_ref, x_ref, mod_ref, g_ref, b_ref, out_ref, xn_ref):
        out = _dot(a_ref[0], w_ref[0])
        for k in range(1, nk):
            out = out + _dot(a_ref[k], w_ref[k])
        out_ref[...] = out
        z = ALPHA * x_ref[...] + (weight * (1.0 + mod_ref[0][2:3, :])) * out
        xhat, _ = _ln_stats(z)
        xn_ref[...] = xhat * g_ref[...] + b_ref[...]

    row = pl.BlockSpec((tm, Dm), lambda i: (i, 0))
    vec = pl.BlockSpec((1, Dm), lambda i: (0, 0))
    return pl.pallas_call(
        body, name=name, grid=(T // tm,),
        in_specs=[pl.BlockSpec((nk, tm, tk), lambda i: (0, i, 0)),
                  pl.BlockSpec((nk, tk, Dm), lambda i: (0, 0, 0)),
                  row, pl.BlockSpec((1, 3, Dm), lambda i: (i // tpb, 0, 0)), vec, vec],
        out_specs=[row, row],
        out_shape=[jax.ShapeDtypeStruct((T, Dm), F32), jax.ShapeDtypeStruct((T, Dm), F32)],
        compiler_params=_cp(1),
    )(a, w, x, mod3, lng, lnb)


def post_bwd(dxn, x, out, mod3, lng, w, weight, name, tm=512, gu=None, exchange=()):
    T, Dm = x.shape
    nk, tk, _ = w.shape
    Bl = mod3.shape[0]
    tm = _tile(T // Bl, tm)
    tpb = T // Bl // tm
    fused = gu is not None

    def body(dxn_ref, x_ref, out_ref, mod_ref, g_ref, w_ref, *rest):
        if fused:
            gg_ref, uu_ref = rest[:2]
            rest = rest[2:]
        dxr_ref, dout_ref, da_ref, dg_ref, db_ref, dgate_ref = rest
        i = pl.program_id(0)
        out = out_ref[...]
        dxn = dxn_ref[...]
        coef = weight * (1.0 + mod_ref[0][2:3, :])
        xhat, rstd = _ln_stats(ALPHA * x_ref[...] + coef * out)
        dyh = dxn * g_ref[...]
        dz = rstd * (dyh - jnp.mean(dyh, axis=-1, keepdims=True)
                     - xhat * jnp.mean(dyh * xhat, axis=-1, keepdims=True))
        dxr_ref[...] = ALPHA * dz
        dout = (coef * dz).astype(BF16)
        dout_ref[...] = dout

        @pl.when(i == 0)
        def _():
            dg_ref[...] = jnp.zeros_like(dg_ref)
            db_ref[...] = jnp.zeros_like(db_ref)

        @pl.when(i % tpb == 0)
        def _():
            dgate_ref[...] = jnp.zeros_like(dgate_ref)
        dg_ref[...] += _sum0(dxn * xhat)
        db_ref[...] += _sum0(dxn)
        dgate_ref[0] += _sum0((weight * out) * dz)
        for k in range(nk):
            da = _dot_nt(dout, w_ref[k])
            if fused:
                gg = gg_ref[k].astype(F32)
                s = _sigmoid(gg)
                da_ref[k] = (da * uu_ref[k].astype(F32) * (s * (1.0 + gg * (1.0 - s)))).astype(BF16)
                da_ref[nk + k] = (da * (gg * s)).astype(BF16)
            else:
                da_ref[k] = da.astype(BF16)

    row = pl.BlockSpec((tm, Dm), lambda i: (i, 0))
    vec = pl.BlockSpec((1, Dm), lambda i: (0, 0))
    wide = pl.BlockSpec((nk, tm, tk), lambda i: (0, i, 0))
    nda = 2 * nk if fused else nk
    grid = (T // tm,)
    body, c_in, c_out, c_shape, c_scratch = host_comm(body, grid, 8 if fused else 6, 6, exchange=exchange)
    *results, = pl.pallas_call(
        body, name=name, grid=grid,
        in_specs=[row, row, row, pl.BlockSpec((1, 3, Dm), lambda i: (i // tpb, 0, 0)), vec,
                  pl.BlockSpec((nk, tk, Dm), lambda i: (0, 0, 0))] + ([wide, wide] if fused else []) + c_in,
        out_specs=[row, row, pl.BlockSpec((nda, tm, tk), lambda i: (0, i, 0)),
                   vec, vec, pl.BlockSpec((1, 1, Dm), lambda i: (i // tpb, 0, 0))] + c_out,
        out_shape=[jax.ShapeDtypeStruct((T, Dm), F32), jax.ShapeDtypeStruct((T, Dm), BF16),
                   jax.ShapeDtypeStruct((nda, T, tk), BF16), jax.ShapeDtypeStruct((1, Dm), F32),
                   jax.ShapeDtypeStruct((1, Dm), F32), jax.ShapeDtypeStruct((Bl, 1, Dm), F32)] + c_shape,
        scratch_shapes=c_scratch, compiler_params=_cp(1),
    )(dxn, x, out, mod3, lng, w, *(gu if fused else ()), *exchange)
    return tuple(results[:6]) + ((results[6:],) if exchange else ())


def mm_tn(a, b, name, bw=None):
    a3, b3 = a.ndim == 3, b.ndim == 3
    nk, T, tk = a.shape if a3 else (1,) + a.shape
    nc, wn = (b.shape[0], b.shape[2]) if b3 else (b.shape[1] // bw, bw)
    tt = _tile(T, 2048)
    nt = T // tt

    def body(a_ref, b_ref, o_ref, ob_ref):
        t = pl.program_id(2)

        @pl.when(t == 0)
        def _():
            o_ref[...] = jnp.zeros_like(o_ref)
        o_ref[...] += _dot_tn(a_ref[...], b_ref[...])

        @pl.when(t == nt - 1)
        def _():
            ob_ref[...] = o_ref[...].astype(BF16)

    a_spec = (pl.BlockSpec((None, tt, tk), lambda k, c, t: (k, t, 0)) if a3
              else pl.BlockSpec((tt, tk), lambda k, c, t: (t, 0)))
    b_spec = (pl.BlockSpec((None, tt, wn), lambda k, c, t: (c, t, 0)) if b3
              else pl.BlockSpec((tt, wn), lambda k, c, t: (t, c)))
    o_spec = pl.BlockSpec((None, tk, wn), lambda k, c, t: (k * nc + c, 0, 0))
    return pl.pallas_call(
        body, name=name, grid=(nk, nc, nt), in_specs=[a_spec, b_spec], out_specs=[o_spec, o_spec],
        out_shape=[jax.ShapeDtypeStruct((nk * nc, tk, wn), F32), jax.ShapeDtypeStruct((nk * nc, tk, wn), BF16)],
        compiler_params=_cp(3),
    )(a, b)


def ffn_in(x, mod3, w, name, gather=()):
    T, Dm = x.shape
    nj, tf = w.shape[0] // 2, w.shape[2]
    Bl = mod3.shape[0]
    tm = _tile(T // Bl, 1024)
    tpb = T // Bl // tm

    def body(x_ref, mod_ref, wg_ref, wu_ref, a_ref, g_ref, u_ref, h_ref):
        m = mod_ref[0]
        h = (x_ref[...] * (1.0 + m[1:2, :]) + m[0:1, :]).astype(BF16)
        h_ref[...] = h
        g = _dot(h, wg_ref[...])
        u = _dot(h, wu_ref[...])
        a_ref[...] = (g * _sigmoid(g) * u).astype(BF16)
        g_ref[...] = g.astype(BF16)
        u_ref[...] = u.astype(BF16)

    col = pl.BlockSpec((None, tm, tf), lambda j, i: (j, i, 0))
    grid = (nj, T // tm)
    body, c_in, c_out, c_shape, c_scratch = host_comm(body, grid, 4, 4, gather=gather)
    a, g, u, h, *gathered = pl.pallas_call(
        body, name=name, grid=grid,
        in_specs=[pl.BlockSpec((tm, Dm), lambda j, i: (i, 0)),
                  pl.BlockSpec((1, 3, Dm), lambda j, i: (i // tpb, 0, 0)),
                  pl.BlockSpec((None, Dm, tf), lambda j, i: (j, 0, 0)),
                  pl.BlockSpec((None, Dm, tf), lambda j, i: (nj + j, 0, 0))] + c_in,
        out_specs=[col, col, col, pl.BlockSpec((None, tm, Dm), lambda j, i: (j, i, 0))] + c_out,
        out_shape=[jax.ShapeDtypeStruct((nj, T, tf), BF16)] * 3 + [jax.ShapeDtypeStruct((nj, T, Dm), BF16)]
        + c_shape,
        scratch_shapes=c_scratch, compiler_params=_cp(2),
    )(x, mod3, w, w, *gather)
    return a, g, u, h[0], gathered


def loss_head(y, tgt, name):
    T, Dm = y.shape
    tm = _tile(T, 512)
    nt = T // tm

    def body(y_ref, t_ref, dy_ref, l_ref, acc):
        i = pl.program_id(0)

        @pl.when(i == 0)
        def _():
            acc[...] = jnp.zeros_like(acc)
        e = y_ref[...] - t_ref[...]
        dy_ref[...] = e * (1.0 / Dm)
        acc[...] += _sum0(e * e)

        @pl.when(i == nt - 1)
        def _():
            l_ref[...] = jnp.broadcast_to(_sum1(acc[...]) * (0.5 / Dm), l_ref.shape)

    return pl.pallas_call(
        body, name=name, grid=(nt,),
        in_specs=[pl.BlockSpec((tm, Dm), lambda i: (i, 0)), pl.BlockSpec((tm, Dm), lambda i: (i, 0))],
        out_specs=[pl.BlockSpec((tm, Dm), lambda i: (i, 0)), pl.BlockSpec((1, 128), lambda i: (0, 0))],
        out_shape=[jax.ShapeDtypeStruct((T, Dm), F32), jax.ShapeDtypeStruct((1, 128), F32)],
        scratch_shapes=[pltpu.VMEM((1, Dm), F32)], compiler_params=_cp(1),
    )(y, tgt)


def adamw(w, g, m, v, name):
    R, C = w.shape
    tr = _tile(R, 512) if R % 8 == 0 else R

    def body(w_ref, g_ref, m_ref, v_ref, d_ref, nm_ref, nv_ref):
        gg = g_ref[...]
        mm = ADAM_B1 * m_ref[...] + (1.0 - ADAM_B1) * gg
        vv = ADAM_B2 * v_ref[...] + (1.0 - ADAM_B2) * (gg * gg)
        m_hat = mm / (1.0 - ADAM_B1 ** ADAM_STEP)
        v_hat = vv / (1.0 - ADAM_B2 ** ADAM_STEP)
        d_ref[...] = -ADAM_LR * (m_hat / (jnp.sqrt(v_hat) + ADAM_EPS) + ADAM_WD * w_ref[...])
        nm_ref[...] = mm
        nv_ref[...] = vv

    spec = pl.BlockSpec((tr, C), lambda i: (i, 0))
    return pl.pallas_call(
        body, name=name, grid=(R // tr,), in_specs=[spec] * 4, out_specs=[spec] * 3,
        out_shape=[jax.ShapeDtypeStruct((R, C), F32)] * 3, compiler_params=_cp(1),
    )(w, g, m, v)


def ada_fwd(c_all, ada_w, ada_b_cols, name):
    Lr, Dm, Nc = ada_w.shape
    Bg = c_all.shape[0]

    def body(c_ref, w_ref, b_ref, o_ref):
        cc = c_ref[...]
        cond = cc * _sigmoid(cc)
        o_ref[0] = _dot(cond.astype(BF16), w_ref[0].astype(BF16)) + b_ref[0]

    return pl.pallas_call(
        body, name=name, grid=(Lr,),
        in_specs=[pl.BlockSpec((Bg, Dm), lambda l: (0, 0)),
                  pl.BlockSpec((1, Dm, Nc), lambda l: (l, 0, 0)),
                  pl.BlockSpec((1, 1, Nc), lambda l: (l, 0, 0))],
        out_specs=pl.BlockSpec((1, Bg, Nc), lambda l: (l, 0, 0)),
        out_shape=jax.ShapeDtypeStruct((Lr, Bg, Nc), F32), compiler_params=_cp(1),
    )(c_all, ada_w, ada_b_cols)


def ada_bwd(c_all_t, dmod_cols, dmod_all, name):
    Dm, Bg = c_all_t.shape
    Lr, _, Nc = dmod_cols.shape
    Nf = dmod_all.shape[2]

    def body(c_ref, dm_ref, da_ref, gw_ref, gb_ref):
        cc = c_ref[...]
        cond = cc * _sigmoid(cc)
        gw_ref[0] = _dot(cond.astype(BF16), dm_ref[0].astype(BF16))
        gb_ref[0] = _sum0(da_ref[0])

    return pl.pallas_call(
        body, name=name, grid=(Lr,),
        in_specs=[pl.BlockSpec((Dm, Bg), lambda l: (0, 0)),
                  pl.BlockSpec((1, Bg, Nc), lambda l: (l, 0, 0)),
                  pl.BlockSpec((1, Bg, Nf), lambda l: (l, 0, 0))],
        out_specs=[pl.BlockSpec((1, Dm, Nc), lambda l: (l, 0, 0)),
                   pl.BlockSpec((1, 1, Nf), lambda l: (l, 0, 0))],
        out_shape=[jax.ShapeDtypeStruct((Lr, Dm, Nc), F32), jax.ShapeDtypeStruct((Lr, 1, Nf), F32)],
        compiler_params=_cp(1),
    )(c_all_t, dmod_cols, dmod_all)


def _conv_taps(x, w, rows):
    shifted = [x]
    c = w[3:4, :] * x
    for k in range(1, 4):
        xs = jnp.where(rows >= k, pltpu.roll(x, k, 0), 0.0)
        shifted.append(xs)
        c = c + w[3 - k:4 - k, :] * xs
    return c, shifted


def conv_silu(proj3, conv_w, name):
    Bl, S, _ = proj3.shape
    ncb = conv_w.shape[1] // 128

    def body(x_ref, w_ref, o_ref):
        rows = lax.broadcasted_iota(jnp.int32, (S, 128), 0)
        c, _ = _conv_taps(_round(x_ref[0]), _round(w_ref[...]), rows)
        o_ref[0] = c * _sigmoid(c)

    return pl.pallas_call(
        body, name=name, grid=(Bl, ncb),
        in_specs=[pl.BlockSpec((1, S, 128), lambda b, j: (b, 0, j)),
                  pl.BlockSpec((4, 128), lambda b, j: (0, j))],
        out_specs=pl.BlockSpec((1, S, 128), lambda b, j: (b, 0, j)),
        out_shape=jax.ShapeDtypeStruct((Bl, S, conv_w.shape[1]), F32), compiler_params=_cp(2),
    )(proj3, conv_w)


def conv_silu_bwd(proj3, conv_w, dq, dk, name):
    Bl, S, _ = proj3.shape
    nq = dq.shape[2] // 128

    def body(x_ref, w_ref, dq_ref, dk_ref, dx_ref, dw_ref):
        j = pl.program_id(1)
        rows = lax.broadcasted_iota(jnp.int32, (S, 128), 0)
        w = _round(w_ref[...])
        c, shifted = _conv_taps(_round(x_ref[0]), w, rows)
        s = _sigmoid(c)
        dact = jnp.where(j < nq, dq_ref[0], dk_ref[0])
        dc = _round(dact * (s * (1.0 + c * (1.0 - s))))
        dx = w[3:4, :] * dc
        dws = [_sum0(dc * shifted[0])]
        for k in range(1, 4):
            up = jnp.where(rows < S - k, pltpu.roll(dc, S - k, 0), 0.0)
            dx = dx + w[3 - k:4 - k, :] * up
            dws.append(_sum0(dc * shifted[k]))
        dx_ref[0] = dx.astype(BF16)
        tap = lax.broadcasted_iota(jnp.int32, (4, 128), 0)
        dw_ref[0] = functools.reduce(lambda a, b: a + b, [jnp.where(tap == 3 - k, dws[k], 0.0) for k in range(4)])

    return pl.pallas_call(
        body, name=name, grid=(Bl, 2 * nq),
        in_specs=[pl.BlockSpec((1, S, 128), lambda b, j: (b, 0, j)),
                  pl.BlockSpec((4, 128), lambda b, j: (0, j)),
                  pl.BlockSpec((1, S, 128), lambda b, j: (b, 0, jnp.minimum(j, nq - 1))),
                  pl.BlockSpec((1, S, 128), lambda b, j: (b, 0, jnp.maximum(j - nq, 0)))],
        out_specs=[pl.BlockSpec((1, S, 128), lambda b, j: (b, 0, j)),
                   pl.BlockSpec((1, 4, 128), lambda b, j: (b, 0, j))],
        out_shape=[jax.ShapeDtypeStruct((Bl, S, 2 * nq * 128), BF16),
                   jax.ShapeDtypeStruct((Bl, 4, 2 * nq * 128), F32)],
        compiler_params=_cp(2),
    )(proj3, conv_w, dq, dk)


def _log_sigmoid(a):
    return jnp.minimum(a, 0.0) - jnp.log(1.0 + jnp.exp(-jnp.abs(a)))


def _chunk_state(kc, vc, gi, bcum, b_last, C, n, m):
    a = b_last - bcum + gi
    m_loc = jnp.max(a, axis=0, keepdims=True)
    wa = jnp.exp(a - m_loc)
    c_loc = _dot_tn((wa * vc).astype(BF16), kc.astype(BF16))
    n_loc = _sum0(_round(wa) * _round(kc))
    m_new = jnp.maximum(b_last + m, m_loc)
    sp = jnp.exp(b_last + m - m_new)
    sl = jnp.exp(m_loc - m_new)
    return sp * C + sl * c_loc, sp * n + sl * n_loc, m_new, wa, sp, sl


def _chunk_out(qs, kc, vc, gi_row, bcum, bcum_row, low, C, n, m):
    inter_log = bcum + m
    dlog = jnp.where(low, bcum - bcum_row + gi_row, NEG)
    m_i = jnp.maximum(inter_log, jnp.max(dlog, axis=1, keepdims=True))
    dm = jnp.exp(dlog - m_i)
    iw = jnp.exp(inter_log - m_i)
    qs_b, k_b, v_b = qs.astype(BF16), kc.astype(BF16), vc.astype(BF16)
    sc = _dot_nt(qs_b, k_b) * dm
    qc_ = _dot_nt(qs_b, C.astype(BF16))
    qn = _sum1(_round(qs) * _round(n))
    num = _dot(sc.astype(BF16), v_b) + iw * qc_
    den = _sum1(sc) + iw * qn
    floor = jnp.exp(-m_i)
    dn = jnp.maximum(jnp.abs(den), floor)
    return dict(hc=num / dn, den=den, dn=dn, floor=floor, sc=sc, dm=dm, iw=iw, qc=qc_, qn=qn,
                qs_b=qs_b, k_b=k_b, v_b=v_b)


def _cell_consts(L):
    ri = lax.broadcasted_iota(jnp.int32, (L, L), 0)
    ci = lax.broadcasted_iota(jnp.int32, (L, L), 1)
    return ri == ci, ci <= ri, ri <= ci


def _load_chunk(q_ref, k_ref, v_ref, G, off, L, h, lane):
    hh = h % 2
    qmask = (lane >= M_DQK * hh) & (lane < M_DQK * (hh + 1))
    pair = pl.ds(128 * (h // 2), 128)
    qc = jnp.where(qmask, q_ref[0, pl.ds(off, L), pair], 0.0)
    kc = jnp.where(qmask, k_ref[0, pl.ds(off, L), pair], 0.0)
    vc = v_ref[0, pl.ds(off, L), pl.ds(M_DV * h, M_DV)]
    gi = _sum1(jnp.where(lane == h, G, 0.0))
    gf = _sum1(jnp.where(lane == h + HEADS, G, 0.0))
    return qmask, qc, kc, vc, gi, gf


def _gate_rows(gi, gf, eye, low, upp):
    lf = _log_sigmoid(gf)
    lf_row = _sum0(jnp.where(eye, lf, 0.0))
    gi_row = _sum0(jnp.where(eye, gi, 0.0))
    bcum = _sum1(jnp.where(low, lf_row, 0.0))
    bcum_row = _sum0(jnp.where(upp, lf, 0.0))
    b_last = _sum0(lf)
    return gi_row, bcum, bcum_row, b_last


def _cell_specs(SB, cpb, blk):
    def seq(width, col):
        return pl.BlockSpec((1, SB, width), lambda b, s: (b, blk(s), col))

    def state(rows):
        return pl.BlockSpec((1, HEADS, cpb, rows, 128), lambda b, s: (b, 0, blk(s), 0, 0))

    ins = [seq(D // 2, 0), seq(D // 2, 1), seq(D, 1), seq(D, 2), seq(128, 3 * D // 128),
           pl.BlockSpec((1, D), lambda b, s: (0, 0)), pl.BlockSpec((1, 128), lambda b, s: (0, 0))]
    return ins, [state(M_DV), state(1), state(1)], seq


def mlstm_cell_fwd(qk3, proj3, gain, gbias, name, gather=()):
    Bl, S, _ = qk3.shape
    L = M_CHUNK
    SB = min(M_SLAB, S)
    cpb, nc, nsb = SB // L, S // L, S // SB
    scale = M_DQK ** -0.5

    def body(q_ref, k_ref, v_ref, o_ref, g_ref, gain_ref, gb_ref, y_ref, cst_ref, nst_ref, mst_ref, C_s, n_s, m_s):
        @pl.when(pl.program_id(1) == 0)
        def _():
            C_s[...] = jnp.zeros_like(C_s)
            n_s[...] = jnp.zeros_like(n_s)
            m_s[...] = jnp.zeros_like(m_s)
        lane = lax.broadcasted_iota(jnp.int32, (L, 128), 1)
        eye, low, upp = _cell_consts(L)

        def step(c, carry):
            off = pl.multiple_of(c * L, L)
            G = g_ref[0, pl.ds(off, L), :] + gb_ref[...]
            for h in range(HEADS):
                C, n, mb = C_s[h], n_s[h], m_s[h]
                cst_ref[0, h, c] = C
                nst_ref[0, h, c] = n
                mst_ref[0, h, c] = mb
                m = mb[:, 0:1]
                _, qc, kc, vc, gi, gf = _load_chunk(q_ref, k_ref, v_ref, G, off, L, h, lane)
                gi_row, bcum, bcum_row, b_last = _gate_rows(gi, gf, eye, low, upp)
                r = _chunk_out(qc * scale, kc, vc, gi_row, bcum, bcum_row, low, C, n, m)
                hc = r["hc"]
                hn = hc * lax.rsqrt(jnp.mean(hc * hc, axis=-1, keepdims=True) + RMS_EPS)
                cols = pl.ds(M_DV * h, M_DV)
                oc = o_ref[0, pl.ds(off, L), cols]
                y_ref[0, pl.ds(off, L), cols] = (_sigmoid(oc) * hn * gain_ref[:, cols]).astype(BF16)
                C2, n2, m2, _, _, _ = _chunk_state(kc, vc, gi, bcum, b_last, C, n, m)
                C_s[h] = C2
                n_s[h] = n2
                m_s[h] = jnp.broadcast_to(m2, (1, 128))
            return carry

        lax.fori_loop(0, cpb, step, 0)

    ins, states, seq = _cell_specs(SB, cpb, lambda s: s)
    grid = (Bl, nsb)
    body, c_in, c_out, c_shape, c_scratch = host_comm(body, grid, 7, 4, gather=gather)
    return pl.pallas_call(
        body, name=name, grid=grid, in_specs=ins + c_in, out_specs=[seq(D, 0)] + states + c_out,
        out_shape=[jax.ShapeDtypeStruct((Bl, S, D), BF16),
                   jax.ShapeDtypeStruct((Bl, HEADS, nc, M_DV, 128), F32),
                   jax.ShapeDtypeStruct((Bl, HEADS, nc, 1, 128), F32),
                   jax.ShapeDtypeStruct((Bl, HEADS, nc, 1, 128), F32)] + c_shape,
        scratch_shapes=[pltpu.VMEM((HEADS, M_DV, 128), F32), pltpu.VMEM((HEADS, 1, 128), F32),
                        pltpu.VMEM((HEADS, 1, 128), F32)] + c_scratch,
        compiler_params=_cp(2),
    )(qk3, qk3, proj3, proj3, proj3, gain, gbias, *gather)


def mlstm_cell_bwd(qk3, proj3, gain, gbias, dy3, states, name, exchange=()):
    Bl, S, _ = qk3.shape
    L = M_CHUNK
    SB = min(M_SLAB, S)
    cpb, nsb = SB // L, S // SB
    scale = M_DQK ** -0.5

    def body(q_ref, k_ref, v_ref, o_ref, g_ref, gain_ref, gb_ref, cst_ref, nst_ref, mst_ref, dy_ref,
             dq_ref, dk_ref, dv_ref, do_ref, dg_ref, dgain_ref, dgb_ref, dC_s, dn_s, dgain_s, dgb_s):
        s = pl.program_id(1)

        @pl.when(s == 0)
        def _():
            dC_s[...] = jnp.zeros_like(dC_s)
            dn_s[...] = jnp.zeros_like(dn_s)
            dgain_s[...] = jnp.zeros_like(dgain_s)
            dgb_s[...] = jnp.zeros_like(dgb_s)
        lane = lax.broadcasted_iota(jnp.int32, (L, 128), 1)
        rowi = lax.broadcasted_iota(jnp.int32, (L, 1), 0)
        eye, low, upp = _cell_consts(L)

        def bstep(t, carry):
            c = cpb - 1 - t
            off = pl.multiple_of(c * L, L)
            G = g_ref[0, pl.ds(off, L), :] + gb_ref[...]
            slab = jnp.zeros((L, 128), F32)
            dq_pair = dk_pair = None
            for h in range(HEADS):
                cols = pl.ds(M_DV * h, M_DV)
                gain_h = gain_ref[:, cols]
                C, n, m = cst_ref[0, h, c], nst_ref[0, h, c], mst_ref[0, h, c][:, 0:1]
                dC_n, dn_n = dC_s[h], dn_s[h]
                qmask, qc, kc, vc, gi, gf = _load_chunk(q_ref, k_ref, v_ref, G, off, L, h, lane)
                gi_row, bcum, bcum_row, b_last = _gate_rows(gi, gf, eye, low, upp)
                qs = qc * scale
                r = _chunk_out(qs, kc, vc, gi_row, bcum, bcum_row, low, C, n, m)
                _, _, _, wa, sp, sl = _chunk_state(kc, vc, gi, bcum, b_last, C, n, m)
                hc, den, dn, sc, dm, iw, qn = r["hc"], r["den"], r["dn"], r["sc"], r["dm"], r["iw"], r["qn"]
                qs_b, k_b, v_b = r["qs_b"], r["k_b"], r["v_b"]
                dy = dy_ref[0, pl.ds(off, L), cols].astype(F32)
                oc = o_ref[0, pl.ds(off, L), cols]
                sig_o = _sigmoid(oc)
                rr = lax.rsqrt(jnp.mean(hc * hc, axis=-1, keepdims=True) + RMS_EPS)
                hn = hc * rr
                dgain_s[:, cols] += _sum0(dy * sig_o * hn)
                do_ref[0, pl.ds(off, L), cols] = (
                    dy * hn * gain_h * sig_o * (1.0 - sig_o)).astype(BF16)
                dhn = dy * sig_o * gain_h
                dhc = rr * dhn - hc * (rr * rr * rr) * jnp.mean(dhn * hc, axis=-1, keepdims=True)
                dnum = dhc / dn
                gden = -_sum1(dhc * hc) / dn
                dden = jnp.where(jnp.abs(den) > r["floor"], gden * jnp.sign(den), 0.0)
                dnum_b = dnum.astype(BF16)
                dsc = _dot_nt(dnum_b, v_b) + dden
                dv = _dot_tn(sc.astype(BF16), dnum_b)
                diw = _sum1(dnum * r["qc"]) + dden * qn
                dqc_b = (iw * dnum).astype(BF16)
                wq = iw * dden
                dqs = _dot(dqc_b, C.astype(BF16)) + wq * n
                dC_out = _dot_tn(dqc_b, qs_b)
                dn_out = _sum0(wq * qs)
                dS_b = (dsc * dm).astype(BF16)
                gm = dsc * sc
                dqs = dqs + _dot(dS_b, k_b)
                dk = _dot_tn(dS_b, qs_b)
                dbc = _sum1(gm) + diw * iw
                colg = _sum0(gm)
                dC_p = sp * dC_n + dC_out
                dn_p = sp * dn_n + dn_out
                dcl_b = (sl * dC_n).astype(BF16)
                dn_loc = sl * dn_n
                dsp = _sum1(_sum0(dC_n * C)) + _sum1(dn_n * n)
                db_last = dsp * sp
                t1 = _dot(v_b, dcl_b) + dn_loc
                dwa = _sum1(t1 * kc)
                dv = dv + wa * _dot_nt(k_b, dcl_b)
                dk = dk + wa * t1
                da = dwa * wa
                db_last = db_last + _sum0(da)
                dbc = dbc - da + jnp.where(rowi == L - 1, db_last, 0.0)
                dbc_row = _sum0(jnp.where(eye, dbc, 0.0)) - colg
                dgi = da + _sum1(jnp.where(eye, colg, 0.0))
                dlf = _sum1(jnp.where(upp, dbc_row, 0.0))
                dgf = dlf * _sigmoid(-gf)
                dq = jnp.where(qmask, dqs * scale, 0.0)
                dk = jnp.where(qmask, dk, 0.0)
                slab = slab + jnp.where(lane == h, dgi, 0.0) + jnp.where(lane == h + HEADS, dgf, 0.0)
                dv_ref[0, pl.ds(off, L), cols] = dv.astype(BF16)
                dC_s[h] = dC_p
                dn_s[h] = dn_p
                if h % 2 == 0:
                    dq_pair, dk_pair = dq, dk
                else:
                    pair = pl.ds(128 * (h // 2), 128)
                    dq_ref[0, pl.ds(off, L), pair] = dq_pair + dq
                    dk_ref[0, pl.ds(off, L), pair] = dk_pair + dk
            dg_ref[0, pl.ds(off, L), :] = slab
            dgb_s[...] += _sum0(slab)
            return carry

        lax.fori_loop(0, cpb, bstep, 0)

        @pl.when(s == nsb - 1)
        def _():
            dgain_ref[0] = dgain_s[...]
            dgb_ref[0] = dgb_s[...]

    ins, states_specs, seq = _cell_specs(SB, cpb, lambda s: nsb - 1 - s)
    once = lambda width: pl.BlockSpec((1, 1, width), lambda b, s: (b, 0, 0))
    grid = (Bl, nsb)
    body, c_in, c_out, c_shape, c_scratch = host_comm(body, grid, 11, 7, exchange=exchange)
    return pl.pallas_call(
        body, name=name, grid=grid, in_specs=ins + states_specs + [seq(D, 0)] + c_in,
        out_specs=[seq(D // 2, 0), seq(D // 2, 0), seq(D, 0), seq(D, 0), seq(128, 0), once(D), once(128)] + c_out,
        out_shape=[jax.ShapeDtypeStruct((Bl, S, D // 2), F32), jax.ShapeDtypeStruct((Bl, S, D // 2), F32),
                   jax.ShapeDtypeStruct((Bl, S, D), BF16), jax.ShapeDtypeStruct((Bl, S, D), BF16),
                   jax.ShapeDtypeStruct((Bl, S, 128), F32), jax.ShapeDtypeStruct((Bl, 1, D), F32),
                   jax.ShapeDtypeStruct((Bl, 1, 128), F32)] + c_shape,
        scratch_shapes=[pltpu.VMEM((HEADS, M_DV, 128), F32), pltpu.VMEM((HEADS, 1, 128), F32),
                        pltpu.VMEM((1, D), F32), pltpu.VMEM((1, 128), F32)] + c_scratch,
        compiler_params=_cp(2),
    )(qk3, qk3, proj3, proj3, proj3, gain, gbias, *states, dy3, *exchange)


def _attn_scores(q, kc, kp, n, row, col, scale):
    s_c = jnp.where(col <= row, _dot_nt(q, kc) * scale, NEG)
    s_p = jnp.where(jnp.logical_and(col >= row, n > 0), _dot_nt(q, kp) * scale, NEG)
    return s_c, s_p


def _to_streams(src, dst, tmp, dil, Sd):
    if dil == 1:
        dst[...] = src[...].astype(dst.dtype)
        return
    if src.dtype != F32:
        tmp[...] = src[...].astype(F32)
        src = tmp
    for r in range(dil):
        dst[pl.ds(r * Sd, Sd), :] = src[pl.ds(r, Sd, stride=dil), :].astype(dst.dtype)


def _from_streams(src, dst, dil, Sd):
    if dil == 1:
        dst[...] = src[...]
        return
    for r in range(dil):
        dst[pl.ds(r, Sd, stride=dil), :] = src[pl.ds(r * Sd, Sd), :]


def attn_fwd(proj, Bl, S, g, dil, name):
    Sd = S // dil
    nb = Sd // A_BLK
    scale = A_BLK ** -0.5
    pv = proj.reshape(Bl, S, A_PROJ)

    def body(q_ref, k_ref, v_ref, o_ref, l_ref, tmp, qs, ks, vs, os_, ls):
        row = lax.broadcasted_iota(jnp.int32, (A_BLK, A_BLK), 0)
        col = lax.broadcasted_iota(jnp.int32, (A_BLK, A_BLK), 1)
        for src, dst in ((q_ref, qs), (k_ref, ks), (v_ref, vs)):
            _to_streams(src.at[0], dst, tmp, dil, Sd)

        def step(i, carry):
            n = i % nb
            off = pl.multiple_of(i * A_BLK, A_BLK)
            offp = pl.multiple_of(jnp.maximum(i - 1, 0) * A_BLK, A_BLK)
            q = qs[pl.ds(off, A_BLK), :]
            s_c, s_p = _attn_scores(q, ks[pl.ds(off, A_BLK), :], ks[pl.ds(offp, A_BLK), :], n, row, col, scale)
            m = jnp.maximum(jnp.max(s_c, axis=1, keepdims=True), jnp.max(s_p, axis=1, keepdims=True))
            p_c = jnp.exp(s_c - m)
            p_p = jnp.exp(s_p - m)
            den = _sum1(p_c) + _sum1(p_p)
            o = _dot(p_c.astype(BF16), vs[pl.ds(off, A_BLK), :]) + _dot(p_p.astype(BF16), vs[pl.ds(offp, A_BLK), :])
            os_[pl.ds(off, A_BLK), :] = o / den
            ls[pl.ds(off, A_BLK), :] = jnp.broadcast_to(m + jnp.log(den), (A_BLK, 128))
            return carry

        lax.fori_loop(0, dil * nb, step, 0)
        _from_streams(os_, o_ref.at[0], dil, Sd)
        _from_streams(ls, l_ref.at[0], dil, Sd)

    def spec(j):
        return pl.BlockSpec((1, S, 128), lambda b, h: (b, 0, g * 24 + j * HEADS + h))

    ospec = pl.BlockSpec((1, S, 128), lambda b, h: (b, 0, h))
    o, lse = pl.pallas_call(
        body, name=name, grid=(Bl, HEADS),
        in_specs=[spec(0), spec(1), spec(2)], out_specs=[ospec, ospec],
        out_shape=[jax.ShapeDtypeStruct((Bl, S, D), F32)] * 2,
        scratch_shapes=[pltpu.VMEM((S, 128), F32)] + [pltpu.VMEM((S, 128), BF16)] * 3 + [pltpu.VMEM((S, 128), F32)] * 2,
        compiler_params=_cp(2),
    )(pv, pv, pv)
    return o.reshape(Bl * S, D), lse.reshape(Bl * S, D)


def attn_merge(os_, lses, name):
    T = os_[0].shape[0]
    tm = _tile(T, 512)
    ng = len(os_)

    def body(*refs):
        o_refs, l_refs = refs[:ng], refs[ng:2 * ng]
        ob_ref, of_ref, lt_ref = refs[2 * ng:]
        ls = [r[...] for r in l_refs]
        m = functools.reduce(jnp.maximum, ls)
        ws = [jnp.exp(l - m) for l in ls]
        den = functools.reduce(lambda a, b: a + b, ws)
        o = functools.reduce(lambda a, b: a + b, [w * r[...] for w, r in zip(ws, o_refs)]) / den
        of_ref[...] = o
        ob_ref[...] = o.astype(BF16)
        lt_ref[...] = m + jnp.log(den)

    spec = pl.BlockSpec((tm, D), lambda i: (i, 0))
    return pl.pallas_call(
        body, name=name, grid=(T // tm,), in_specs=[spec] * (2 * ng), out_specs=[spec] * 3,
        out_shape=[jax.ShapeDtypeStruct((T, D), BF16), jax.ShapeDtypeStruct((T, D), F32),
                   jax.ShapeDtypeStruct((T, D), F32)],
        compiler_params=_cp(1),
    )(*os_, *lses)


def attn_bwd(proj, do, o, lse, Bl, S, g, dil, name):
    Sd = S // dil
    nb = Sd // A_BLK
    scale = A_BLK ** -0.5
    pv = proj.reshape(Bl, S, A_PROJ)
    dov, ov, lv = (t.reshape(Bl, S, D) for t in (do, o, lse))

    def body(q_ref, k_ref, v_ref, do_ref, o_ref, l_ref, dq_ref, dk_ref, dv_ref,
             tmp, qs, ks, vs, dos, dls, lts, dq_s, dk_s, dv_s):
        row = lax.broadcasted_iota(jnp.int32, (A_BLK, A_BLK), 0)
        col = lax.broadcasted_iota(jnp.int32, (A_BLK, A_BLK), 1)
        for src, dst in ((q_ref, qs), (k_ref, ks), (v_ref, vs), (do_ref, dos), (l_ref, lts)):
            _to_streams(src.at[0], dst, tmp, dil, Sd)
        tmp[...] = jnp.broadcast_to(_sum1(do_ref[0].astype(F32) * o_ref[0]), (S, 128))
        _to_streams(tmp, dls, None, dil, Sd)
        dk_s[...] = jnp.zeros_like(dk_s)
        dv_s[...] = jnp.zeros_like(dv_s)

        def step(i, carry):
            n = i % nb
            off = pl.multiple_of(i * A_BLK, A_BLK)
            offp = pl.multiple_of(jnp.maximum(i - 1, 0) * A_BLK, A_BLK)
            q = qs[pl.ds(off, A_BLK), :]
            kc, kp = ks[pl.ds(off, A_BLK), :], ks[pl.ds(offp, A_BLK), :]
            vc, vp = vs[pl.ds(off, A_BLK), :], vs[pl.ds(offp, A_BLK), :]
            do_b = dos[pl.ds(off, A_BLK), :]
            delta = dls[pl.ds(off, A_BLK), :][:, 0:1]
            lt = lts[pl.ds(off, A_BLK), :][:, 0:1]
            s_c, s_p = _attn_scores(q, kc, kp, n, row, col, scale)
            p_c = jnp.exp(s_c - lt)
            p_p = jnp.exp(s_p - lt)
            ds_c = (p_c * (_dot_nt(do_b, vc) - delta) * scale).astype(BF16)
            ds_p = (p_p * (_dot_nt(do_b, vp) - delta) * scale).astype(BF16)
            dq_s[pl.ds(off, A_BLK), :] = _dot(ds_c, kc) + _dot(ds_p, kp)
            dk_s[pl.ds(off, A_BLK), :] += _dot_tn(ds_c, q)
            dk_s[pl.ds(offp, A_BLK), :] += _dot_tn(ds_p, q)
            dv_s[pl.ds(off, A_BLK), :] += _dot_tn(p_c.astype(BF16), do_b)
            dv_s[pl.ds(offp, A_BLK), :] += _dot_tn(p_p.astype(BF16), do_b)
            return carry

        lax.fori_loop(0, dil * nb, step, 0)
        for src, dst in ((dq_s, dq_ref), (dk_s, dk_ref), (dv_s, dv_ref)):
            _from_streams(src, tmp, dil, Sd)
            dst[0] = tmp[...].astype(BF16)

    def spec(j):
        return pl.BlockSpec((1, S, 128), lambda b, h: (b, 0, g * 24 + j * HEADS + h))

    ospec = pl.BlockSpec((1, S, 128), lambda b, h: (b, 0, h))
    slab = lambda dt: pltpu.VMEM((S, 128), dt)
    outs = pl.pallas_call(
        body, name=name, grid=(Bl, HEADS),
        in_specs=[spec(0), spec(1), spec(2), ospec, ospec, ospec], out_specs=[ospec] * 3,
        out_shape=[jax.ShapeDtypeStruct((Bl, S, D), BF16)] * 3,
        scratch_shapes=[slab(F32)] + [slab(BF16)] * 4 + [slab(F32)] * 5,
        compiler_params=_cp(2),
    )(pv, pv, pv, dov, ov, lv)
    return [t.reshape(Bl * S, D) for t in outs]


def _as_slots(pair, shape):
    return tuple(t.reshape(shape) for t in pair)


def ffn_fwd(x, mod3, w_in, w_out, lng, lnb, tag, gather=()):
    a, g, u, h, gathered = ffn_in(x, mod3, w_in, tag + "_in", gather=gather)
    out, xn = proj_post(a, w_out, x, mod3, lng, lnb, 0.5, tag + "_out")
    return xn, (x, out, g, u, h, a), gathered


def ffn_bwd(dxn, saved, mod3, w_in, w_out, lng, tag, exchange=()):
    x, out, g, u, h, a = saved
    dxres, dout, dgu, dlg, dlb, dgate, *received = post_bwd(dxn, x, out, mod3, lng, w_out, 0.5, tag + "_outb",
                                                            tm=256, gu=(g, u), exchange=exchange)
    dx, dsh, dsc = modmm_bwd(dgu, w_in, x, mod3, dxres, tag + "_inb", tm=256)
    dw_in = mm_tn(h, dgu, tag + "_dwin")
    dw_out = _as_slots(mm_tn(a, dout, tag + "_dwout", bw=D), (N_DEV, D_FF // N_DEV, D))
    dmod3 = jnp.concatenate([dsh, dsc, dgate], axis=1)
    return dx, [dw_in, dw_out], dlg, dlb, dmod3, (received[0] if received else [])


def mlstm_fwd(x, mod3, w_in, w_out, conv_w, gain, gbias, lng, lnb, Bl, S, gather=()):
    proj, h = modmm(x, mod3, w_in, F32, "ml_in", tn=M_PROJ_PAD // 5)
    proj3 = proj.reshape(Bl, S, M_PROJ_PAD)
    qk3 = conv_silu(proj3, conv_w, "ml_conv")
    y3, *rest = mlstm_cell_fwd(qk3, proj3, gain, gbias, "ml_cell", gather=gather)
    states, gathered = rest[:3], rest[3:]
    y = y3.reshape(Bl * S, D)
    out, xn = proj_post(y[None], w_out, x, mod3, lng, lnb, 1.0, "ml_out")
    return xn, (x, out, h, proj3, qk3, y, states), gathered


def mlstm_bwd(dxn, saved, mod3, w_in, w_out, conv_w, gain, gbias, lng, Bl, S, exchange=()):
    x, out, h, proj3, qk3, y, states = saved
    dxres, dout, dy, dlg, dlb, dgate = post_bwd(dxn, x, out, mod3, lng, w_out, 1.0, "ml_outb")
    dq, dk, dv, do, dg, dgain, dgb, *received = mlstm_cell_bwd(qk3, proj3, gain, gbias, dy.reshape(Bl, S, D),
                                                               states, "ml_cellb", exchange=exchange)
    dqk, dconv = conv_silu_bwd(proj3, conv_w, dq, dk, "ml_convb")
    dproj = jnp.concatenate([dqk, dv, do, dg.astype(BF16)], axis=2).reshape(Bl * S, M_PROJ_PAD)
    dx, dsh, dsc = modmm_bwd(dproj, w_in, x, mod3, dxres, "ml_inb", tn=M_PROJ_PAD // 5)
    dwi, _ = mm_tn(h, dproj, "ml_dwin", bw=M_PROJ_PAD // 5)
    dwi = _restack(jnp.moveaxis(dwi, 0, 1).reshape(D, M_PROJ_PAD)[:, :M_PROJ], 1)
    dw_out = _as_slots(mm_tn(y, dout, "ml_dwout", bw=D), (N_DEV, D // N_DEV, D))
    small = (jnp.sum(dconv, axis=0), jnp.sum(dgain, axis=0), jnp.sum(dgb, axis=0)[:, :2 * HEADS])
    dmod3 = jnp.concatenate([dsh, dsc, dgate], axis=1)
    return dx, [(dwi, dwi.astype(BF16)), dw_out], dlg, dlb, dmod3, small, received


def attn_mixer_fwd(x, mod3, w_in, w_out, lng, lnb, Bl, S):
    proj, h = modmm(x, mod3, w_in, BF16, "at_in")
    os_, lses = [], []
    for g, (_, dil) in enumerate(DIL_GROUPS):
        o_g, l_g = attn_fwd(proj, Bl, S, g, dil, "at_core%d" % g)
        os_.append(o_g)
        lses.append(l_g)
    ob, of, lt = attn_merge(os_, lses, "at_merge")
    out, xn = proj_post(ob[None], w_out, x, mod3, lng, lnb, 1.0, "at_out")
    return xn, (x, out, h, proj, ob, of, lt)


def attn_mixer_bwd(dxn, saved, mod3, w_in, w_out, lng, Bl, S):
    x, out, h, proj, ob, of, lt = saved
    dxres, dout, do, dlg, dlb, dgate = post_bwd(dxn, x, out, mod3, lng, w_out, 1.0, "at_outb")
    do = do[0]
    parts = []
    for g, (_, dil) in enumerate(DIL_GROUPS):
        parts += attn_bwd(proj, do, of, lt, Bl, S, g, dil, "at_coreb%d" % g)
    dproj = jnp.concatenate(parts, axis=1)
    dx, dsh, dsc = modmm_bwd(dproj, w_in, x, mod3, dxres, "at_inb")
    dw_in = mm_tn(h, dproj, "at_dwin", bw=w_in.shape[2])
    dw_out = _as_slots(mm_tn(ob, dout, "at_dwout", bw=D), (N_DEV, D // N_DEV, D))
    return dx, [dw_in, dw_out], dlg, dlb, jnp.concatenate([dsh, dsc, dgate], axis=1)


def _unstack(stacked, axis):
    full = jnp.moveaxis(stacked, 0, axis)
    shp = list(full.shape)
    shp[axis:axis + 2] = [shp[axis] * shp[axis + 1]]
    return full.reshape(shp)


def _restack(full, axis):
    shp = list(full.shape)
    shp[axis:axis + 1] = [N_DEV, shp[axis] // N_DEV]
    return jnp.moveaxis(full.reshape(shp), axis, 0)


def kernel(x, c, ada_w, ada_b, ln_g, ln_b, ffn_w_in, ffn_w_out, mlstm_w_in, mlstm_gate_bias, mlstm_conv_w, mlstm_head_gain, mlstm_w_out, attn_w_in, attn_w_out, loss_target, m_ada_w, m_ada_b, m_ln_g, m_ln_b, m_ffn_w_in, m_ffn_w_out, m_mlstm_w_in, m_mlstm_gate_bias, m_mlstm_conv_w, m_mlstm_head_gain, m_mlstm_w_out, m_attn_w_in, m_attn_w_out, v_ada_w, v_ada_b, v_ln_g, v_ln_b, v_ffn_w_in, v_ffn_w_out, v_mlstm_w_in, v_mlstm_gate_bias, v_mlstm_conv_w, v_mlstm_head_gain, v_mlstm_w_out, v_attn_w_in, v_attn_w_out):
    Bl, S, _ = x.shape
    T = Bl * S
    Bg = Bl * N_DEV
    me = 4 * lax.axis_index("x") + 2 * lax.axis_index("y") + lax.axis_index("c")
    onehot = (jnp.arange(N_DEV) == me).astype(F32)
    weights = dict(ada_w=ada_w, ada_b=ada_b, ln_g=ln_g, ln_b=ln_b, ffn_w_in=ffn_w_in, ffn_w_out=ffn_w_out,
                   mlstm_w_in=mlstm_w_in, mlstm_gate_bias=mlstm_gate_bias, mlstm_conv_w=mlstm_conv_w,
                   mlstm_head_gain=mlstm_head_gain, mlstm_w_out=mlstm_w_out, attn_w_in=attn_w_in,
                   attn_w_out=attn_w_out)
    m_in = dict(ada_w=m_ada_w, ada_b=m_ada_b, ln_g=m_ln_g, ln_b=m_ln_b, ffn_w_in=m_ffn_w_in,
                ffn_w_out=m_ffn_w_out, mlstm_w_in=m_mlstm_w_in, mlstm_gate_bias=m_mlstm_gate_bias,
                mlstm_conv_w=m_mlstm_conv_w, mlstm_head_gain=m_mlstm_head_gain, mlstm_w_out=m_mlstm_w_out,
                attn_w_in=m_attn_w_in, attn_w_out=m_attn_w_out)
    v_in = dict(ada_w=v_ada_w, ada_b=v_ada_b, ln_g=v_ln_g, ln_b=v_ln_b, ffn_w_in=v_ffn_w_in,
                ffn_w_out=v_ffn_w_out, mlstm_w_in=v_mlstm_w_in, mlstm_gate_bias=v_mlstm_gate_bias,
                mlstm_conv_w=v_mlstm_conv_w, mlstm_head_gain=v_mlstm_head_gain, mlstm_w_out=v_mlstm_w_out,
                attn_w_in=v_attn_w_in, attn_w_out=v_attn_w_out)

    mixer = ("mlstm", "attn")
    shards = [[ffn_w_in[layer, 0], ffn_w_in[layer, 1], ffn_w_out[layer, 0], ffn_w_out[layer, 1],
               weights[mixer[layer] + "_w_in"][0], weights[mixer[layer] + "_w_out"][0]] for layer in range(DEPTH)]
    sends = [[s.astype(BF16) for s in layer_shards] for layer_shards in shards]
    small = jnp.concatenate([c.reshape(-1), ln_g.reshape(-1), ln_b.reshape(-1), mlstm_conv_w.reshape(-1)])
    n_small = small.shape[0]
    small = jnp.pad(small, (0, -n_small % (8 * PACK_COLS))).reshape(-1, PACK_COLS)

    def gathered_weights(g):
        return ((g[0], g[1]), (g[2].reshape(4, D_FF // 4, D), g[3].reshape(4, D_FF // 4, D)), g[4],
                g[5].reshape(1, D, D))

    first_in, first_out, small_all = all_gather([sends[0][0], sends[0][2], small], "ag_params")
    full = [None, None]
    small_flat = small_all.reshape(N_DEV, -1)
    o0 = 0
    c_all = small_flat[:, o0:o0 + c.size].reshape(Bg, D)
    o0 += c.size
    lng_full = _unstack(small_flat[:, o0:o0 + ln_g.size].reshape((N_DEV,) + ln_g.shape), 2)
    o0 += ln_g.size
    lnb_full = _unstack(small_flat[:, o0:o0 + ln_b.size].reshape((N_DEV,) + ln_b.shape), 2)
    o0 += ln_b.size
    conv_full = _unstack(small_flat[:, o0:o0 + mlstm_conv_w.size].reshape((N_DEV,) + mlstm_conv_w.shape), 2)[0]
    gbias =jnp.pad(mlstm_gate_bias, ((0, 0), (0, 128 - 2 * HEADS)))

    ncols = ada_w.shape[2]
    ada_b_cols = lax.dynamic_slice_in_dim(ada_b, me * ncols, ncols, axis=1).reshape(DEPTH, 1, ncols)
    mod_cols = ada_fwd(c_all, ada_w, ada_b_cols, "ada_fwd")
    (mod_g,) = all_gather([mod_cols.reshape(DEPTH * Bg, ncols)], "ag_mod")
    mod_full = _unstack(mod_g.reshape(N_DEV, DEPTH, Bg, ncols), 2)
    mod_mine = lax.dynamic_slice_in_dim(mod_full, me * Bl, Bl, axis=1).reshape(DEPTH, Bl, 3, 3, D)

    xt = x.reshape(T, D)
    saved = []
    for layer in range(DEPTH):
        def lnp(s, layer=layer):
            return lng_full[layer, s].reshape(1, D), lnb_full[layer, s].reshape(1, D)
        md = mod_mine[layer]
        if layer == 0:
            xt, sv0, late = ffn_fwd(xt, md[:, 0], first_in, first_out.reshape(4, D_FF // 4, D), *lnp(0), "f0a",
                                    gather=[sends[0][i] for i in (1, 3, 4, 5)])
            full[0] = gathered_weights([first_in, late[0], first_out, late[1], late[2], late[3]])
            mw_in = jnp.pad(_unstack(full[0][2], 1), ((0, 0), (0, M_PROJ_PAD - M_PROJ)))
        else:
            xt, sv0, _ = ffn_fwd(xt, md[:, 0], full[layer][0][0], full[layer][1][0], *lnp(0), "f%da" % layer)
        f_in, f_out, mix_in, mix_out = full[layer]
        if layer % 2 == 0:
            xt, sv1, g1 = mlstm_fwd(xt, md[:, 1], mw_in, mix_out, conv_full, mlstm_head_gain, gbias, *lnp(1), Bl, S,
                                    gather=sends[1])
            full[1] = gathered_weights(g1)
        else:
            xt, sv1 = attn_mixer_fwd(xt, md[:, 1], mix_in, mix_out, *lnp(1), Bl, S)
        xt, sv2, _ = ffn_fwd(xt, md[:, 2], f_in[1], f_out[1], *lnp(2), "f%db" % layer)
        saved.append((sv0, sv1, sv2))

    dxt, lsum = loss_head(xt, loss_target.reshape(T, D), "loss")
    loss = lax.psum(lsum[0, 0], MESH_AXES)

    dmod, dlg_all, dlb_all = [None] * DEPTH, [None] * DEPTH, [None] * DEPTH
    wgrads = [None] * DEPTH
    recvs = [[None] * 6 for _ in range(DEPTH)]
    ml_small = None
    for layer in reversed(range(DEPTH)):
        md = mod_mine[layer]
        f_in, f_out, mix_in, mix_out = full[layer]
        sv0, sv1, sv2 = saved[layer]
        dxt, dw2, dlg2, dlb2, dm2, _ = ffn_bwd(dxt, sv2, md[:, 2], f_in[1], f_out[1],
                                               lng_full[layer, 2].reshape(1, D), "f%db" % layer)
        lg1 = lng_full[layer, 1].reshape(1, D)
        if layer % 2 == 0:
            dxt, dw1, dlg1, dlb1, dm1, ml_small, got = mlstm_bwd(
                dxt, sv1, md[:, 1], mw_in, mix_out, conv_full, mlstm_head_gain, gbias, lg1, Bl, S,
                exchange=[b16 for _, b16 in wgrads[1]] + [dw2[0][1], dw2[1][1]])
            recvs[1], recvs[0][1], recvs[0][3] = got[:6], got[6], got[7]
            dxt, dw0, dlg0, dlb0, dm0, got = ffn_bwd(dxt, sv0, md[:, 0], f_in[0], f_out[0],
                                                     lng_full[layer, 0].reshape(1, D), "f%da" % layer,
                                                     exchange=[dw1[0][1], dw1[1][1]])
            recvs[0][4], recvs[0][5] = got
        else:
            dxt, dw1, dlg1, dlb1, dm1 = attn_mixer_bwd(dxt, sv1, md[:, 1], mix_in, mix_out, lg1, Bl, S)
            dxt, dw0, dlg0, dlb0, dm0, _ = ffn_bwd(dxt, sv0, md[:, 0], f_in[0], f_out[0],
                                                   lng_full[layer, 0].reshape(1, D), "f%da" % layer)
        wgrads[layer] = [dw0[0], dw2[0], dw0[1], dw2[1], dw1[0], dw1[1]]
        dmod[layer] = jnp.stack([dm0, dm1, dm2], axis=1).reshape(Bl, 9 * D)
        dlg_all[layer] = jnp.concatenate([dlg0, dlg1, dlg2], axis=0)
        dlb_all[layer] = jnp.concatenate([dlb0, dlb1, dlb2], axis=0)
    grad_x = dxt.reshape(Bl, S, D)
    recvs[0][0], recvs[0][2] = exchange_shards([wgrads[0][0][1], wgrads[0][2][1]], "rs_grads")

    gsh = [[shard_sum(lax.dynamic_index_in_dim(f32, me, axis=0, keepdims=False), recv, onehot,
                      "rs_sum%d_%d" % (layer, i))
            for i, ((f32, _), recv) in enumerate(zip(wgrads[layer], recvs[layer]))] for layer in range(DEPTH)]
    grads = {"ffn_w_in": jnp.stack([jnp.stack(g[0:2]) for g in gsh]),
             "ffn_w_out": jnp.stack([jnp.stack(g[2:4]) for g in gsh]),
             "mlstm_w_in": gsh[0][4][None], "mlstm_w_out": gsh[0][5][None],
             "attn_w_in": gsh[1][4][None], "attn_w_out": gsh[1][5][None]}

    dconv, dgain, dgbias = ml_small
    parts = [jnp.stack(dmod).reshape(-1), dgbias.reshape(-1), dgain.reshape(-1),
             jnp.stack(dlg_all).reshape(-1), jnp.stack(dlb_all).reshape(-1), dconv.reshape(-1)]
    sizes = [p.shape[0] for p in parts]
    flat = jnp.concatenate(parts)
    flat = jnp.pad(flat, (0, -flat.shape[0] % (8 * PACK_COLS))).reshape(-1, PACK_COLS)
    (sm_all,) = all_gather([flat], "ag_small")
    sm_sum = sum_leading(sm_all, "small_sum").reshape(-1)
    dmod_all = sm_all.reshape(N_DEV, -1)[:, :sizes[0]].reshape(N_DEV, DEPTH, Bl, 9 * D)
    dmod_all = jnp.moveaxis(dmod_all, 0, 1).reshape(DEPTH, Bg, 9 * D)
    o0 = sizes[0]
    grads["mlstm_gate_bias"] = sm_sum[o0:o0 + sizes[1]].reshape(mlstm_gate_bias.shape)
    o0 += sizes[1]
    grads["mlstm_head_gain"] = sm_sum[o0:o0 + sizes[2]].reshape(mlstm_head_gain.shape)
    o0 += sizes[2]
    nl = ln_g.shape[2]
    g_lng = sm_sum[o0:o0 + sizes[3]].reshape(DEPTH, 3, D)
    o0 += sizes[3]
    g_lnb = sm_sum[o0:o0 + sizes[4]].reshape(DEPTH, 3, D)
    o0 += sizes[4]
    g_conv = sm_sum[o0:o0 + sizes[5]].reshape(1, 4, D)
    grads["ln_g"] = lax.dynamic_slice_in_dim(g_lng, me * nl, nl, axis=2)
    grads["ln_b"] = lax.dynamic_slice_in_dim(g_lnb, me * nl, nl, axis=2)
    grads["mlstm_conv_w"] = lax.dynamic_slice_in_dim(g_conv, me * nl, nl, axis=2)
    dmod_cols = lax.dynamic_slice_in_dim(dmod_all, me * ncols, ncols, axis=2)
    gw, gb = ada_bwd(c_all.T, dmod_cols, dmod_all, "ada_bwd")
    grads["ada_w"] = gw
    grads["ada_b"] = gb.reshape(ada_b.shape)

    names = ["ada_w", "ada_b", "ln_g", "ln_b", "ffn_w_in", "ffn_w_out", "mlstm_w_in", "mlstm_gate_bias",
             "mlstm_conv_w", "mlstm_head_gain", "mlstm_w_out", "attn_w_in", "attn_w_out"]
    deltas, new_m, new_v = [], [], []
    for k in names:
        w = weights[k]
        shp2 = (math.prod(w.shape[:-1]), w.shape[-1])
        d_, m_, v_ = adamw(w.reshape(shp2), grads[k].reshape(shp2), m_in[k].reshape(shp2), v_in[k].reshape(shp2),
                           "adamw_" + k)
        deltas.append(d_.reshape(w.shape))
        new_m.append(m_.reshape(w.shape))
        new_v.append(v_.reshape(w.shape))
    return (loss, grad_x, *[grads[k] for k in names], *deltas, *new_m, *new_v)
```

```python
import functools
import math

import jax
import jax.numpy as jnp
from jax import lax
from jax.experimental import pallas as pl
from jax.experimental.pallas import tpu as pltpu

F32 = jnp.float32
BF16 = jnp.bfloat16

N_DEV = 8
MESH_AXES = ("x", "y", "c")
D = 1024
DEPTH = 2
D_FF = 2816
HEADS = 8
M_DQK = 64
M_DV = 128
M_CHUNK = 64
M_SLAB = 512
M_PROJ = 3088
M_PROJ_PAD = 3200
A_PROJ = 9216
DIL_GROUPS = ((128, 1), (512, 4), (2048, 16))
A_BLK = 128
ALPHA = (2 * DEPTH) ** 0.25
LN_EPS = 1e-5
RMS_EPS = 1e-6
ADAM_LR = 0.001
ADAM_B1 = 0.9
ADAM_B2 = 0.999
ADAM_EPS = 1e-08
ADAM_WD = 0.01
ADAM_STEP = 10
NEG = -1e30
V7X_VMEM_LIMIT = 56 * 1024 * 1024
PACK_COLS = 1024
MESH_ID = pl.DeviceIdType.MESH
ANY_SPEC = pl.BlockSpec(memory_space=pl.ANY)


def _cp(n_axes):
    return pltpu.CompilerParams(dimension_semantics=("arbitrary",) * n_axes,
                                vmem_limit_bytes=V7X_VMEM_LIMIT)


def _dot(a, b):
    return jnp.dot(a, b, preferred_element_type=F32)


def _dot_nt(a, b):
    return lax.dot_general(a, b, (((1,), (1,)), ((), ())), preferred_element_type=F32)


def _dot_tn(a, b):
    return lax.dot_general(a, b, (((0,), (0,)), ((), ())), preferred_element_type=F32)


def _sum0(a):
    return jnp.sum(a, axis=0, keepdims=True)


def _sum1(a):
    return jnp.sum(a, axis=1, keepdims=True)


def _round(a):
    return a.astype(BF16).astype(F32)


def _sigmoid(a):
    return 1.0 / (1.0 + jnp.exp(-a))


def _tile(n, pref):
    t = min(n, pref)
    while n % t:
        t //= 2
    return t


def all_gather(arrs, name):
    n = len(arrs)

    def body(*refs):
        gather = Gather(refs[:n], refs[n:2 * n], *refs[2 * n:])
        gather.start()
        gather.finish()

    return pl.pallas_call(
        body, name=name, out_shape=Gather.out_shape(arrs),
        in_specs=[ANY_SPEC] * n, out_specs=[ANY_SPEC] * n, scratch_shapes=Gather.scratch(n),
    )(*arrs)


class Gather:
    def __init__(self, ins, outs, send_sems, recv_sems, local_sems):
        x, y, c = lax.axis_index("x"), lax.axis_index("y"), lax.axis_index("c")
        me, sibling = (x, y, c), (x, y, 1 - c)
        chips = [(1 - x, y), (x, 1 - y), (1 - x, 1 - y)]

        def slot(a, p):
            return outs[a].at[4 * p[0] + 2 * p[1] + p[2]]

        def copy(a, k, block, to, src=None):
            return pltpu.make_async_remote_copy(
                src_ref=slot(a, block) if src is None else src, dst_ref=slot(a, block),
                send_sem=send_sems.at[7 * a + k], recv_sem=recv_sems.at[7 * a + k],
                device_id=to, device_id_type=MESH_ID)

        n = len(ins)
        self.mine = [pltpu.make_async_copy(ins[a], slot(a, me), local_sems.at[a]) for a in range(n)]
        self.first, self.over_ici, self.passed, self.from_sibling = [], [], [], []
        for a in range(n):
            self.first.append(copy(a, 0, me, sibling, src=ins[a]))
            self.from_sibling.append(copy(a, 0, sibling, me))
            for j, chip in enumerate(chips):
                self.first.append(copy(a, 1 + j, me, (*chip, c), src=ins[a]))
                self.over_ici.append(copy(a, 1 + j, (*chip, c), me))
                self.passed.append(copy(a, 4 + j, (*chip, c), sibling))
                self.from_sibling.append(copy(a, 4 + j, (*chip, 1 - c), me))

    @staticmethod
    def out_shape(arrs):
        return [jax.ShapeDtypeStruct((N_DEV,) + a.shape, a.dtype) for a in arrs]

    @staticmethod
    def scratch(n):
        return [pltpu.SemaphoreType.DMA((7 * n,)), pltpu.SemaphoreType.DMA((7 * n,)),
                pltpu.SemaphoreType.DMA((n,))]

    def start(self):
        for cp in self.mine + self.first:
            cp.start()

    def finish(self):
        for landed, onward in zip(self.over_ici, self.passed):
            landed.wait_recv()
            onward.start()
        for cp in self.from_sibling:
            cp.wait_recv()
        for cp in self.first + self.passed:
            cp.wait_send()
        for cp in self.mine:
            cp.wait()


class Exchange:
    def __init__(self, sends, recvs, send_sems, recv_sems, local_sems):
        x, y, c = lax.axis_index("x"), lax.axis_index("y"), lax.axis_index("c")
        me = 4 * x + 2 * y + c
        self.own = [pltpu.make_async_copy(s.at[me], r.at[me], local_sems.at[a])
                    for a, (s, r) in enumerate(zip(sends, recvs))]
        self.copies = []
        for a, (s_ref, r_ref) in enumerate(zip(sends, recvs)):
            for k in range(1, N_DEV):
                px = 1 - x if (k >> 2) & 1 else x
                py = 1 - y if (k >> 1) & 1 else y
                pc = 1 - c if k & 1 else c
                self.copies.append(pltpu.make_async_remote_copy(
                    src_ref=s_ref.at[4 * px + 2 * py + pc], dst_ref=r_ref.at[me],
                    send_sem=send_sems.at[7 * a + k - 1], recv_sem=recv_sems.at[7 * a + k - 1],
                    device_id=(px, py, pc), device_id_type=MESH_ID))

    @staticmethod
    def scratch(n):
        return [pltpu.SemaphoreType.DMA((7 * n,)), pltpu.SemaphoreType.DMA((7 * n,)),
                pltpu.SemaphoreType.DMA((n,))]

    def start(self):
        for cp in self.own + self.copies:
            cp.start()

    def finish(self):
        for cp in self.copies:
            cp.wait_send()
            cp.wait_recv()
        for cp in self.own:
            cp.wait()


def host_comm(body, grid, n_in, n_out, gather=(), exchange=()):
    ng, nx = len(gather), len(exchange)
    if ng + nx == 0:
        return body, [], [], [], []

    def hosted(*refs):
        ins, c_in, rest = refs[:n_in], refs[n_in:n_in + ng + nx], refs[n_in + ng + nx:]
        outs, c_out, rest = rest[:n_out], rest[n_out:n_out + ng + nx], rest[n_out + ng + nx:]
        n_sems = 3 * ((ng > 0) + (nx > 0))
        scratch, sems = rest[:len(rest) - n_sems], rest[len(rest) - n_sems:]

        def comms():
            made = [Gather(c_in[:ng], c_out[:ng], *sems[:3])] if ng else []
            return made + ([Exchange(c_in[ng:], c_out[ng:], *sems[-3:])] if nx else [])

        ids = [pl.program_id(a) for a in range(len(grid))]

        @pl.when(functools.reduce(jnp.logical_and, [i == 0 for i in ids]))
        def _():
            for cm in comms():
                cm.start()
        body(*ins, *outs, *scratch)

        @pl.when(functools.reduce(jnp.logical_and, [i == g - 1 for i, g in zip(ids, grid)]))
        def _():
            for cm in comms():
                cm.finish()

    shapes = Gather.out_shape(gather) + [jax.ShapeDtypeStruct(a.shape, a.dtype) for a in exchange]
    scratch = (Gather.scratch(ng) if ng else []) + (Exchange.scratch(nx) if nx else [])
    return hosted, [ANY_SPEC] * (ng + nx), [ANY_SPEC] * (ng + nx), shapes, scratch


def shard_sum(own, recv, onehot, name):
    R, C = own.shape
    tr = _tile(R, 512)

    def body(oh_ref, own_ref, recv_ref, o_ref):
        acc = None
        for j in range(N_DEV):
            term = jnp.where(oh_ref[j] > 0.5, own_ref[...], recv_ref[j].astype(F32))
            acc = term if acc is None else acc + term
        o_ref[...] = acc

    return pl.pallas_call(
        body, name=name, grid=(R // tr,),
        in_specs=[pl.BlockSpec(memory_space=pltpu.SMEM),
                  pl.BlockSpec((tr, C), lambda i: (i, 0)),
                  pl.BlockSpec((N_DEV, tr, C), lambda i: (0, i, 0))],
        out_specs=pl.BlockSpec((tr, C), lambda i: (i, 0)),
        out_shape=jax.ShapeDtypeStruct((R, C), F32), compiler_params=_cp(1),
    )(onehot, own, recv)


def sum_leading(a, name):
    _, R, C = a.shape

    def body(a_ref, o_ref):
        acc = a_ref[0]
        for j in range(1, N_DEV):
            acc = acc + a_ref[j]
        o_ref[...] = acc

    return pl.pallas_call(body, name=name, out_shape=jax.ShapeDtypeStruct((R, C), F32),
                          compiler_params=_cp(0))(a)


def _col_chunks(w, tn):
    if w.ndim == 3:
        return w.shape[0], w.shape[2], pl.BlockSpec((None, w.shape[1], w.shape[2]), lambda i, j: (j, 0, 0))
    return w.shape[1] // tn, tn, pl.BlockSpec((w.shape[0], tn), lambda i, j: (0, j))


def modmm(x, mod3, w, out_dtype, name, tn=None):
    T, Dm = x.shape
    nj, tn, w_spec = _col_chunks(w, tn)
    N = nj * tn
    Bl = mod3.shape[0]
    tm = _tile(T // Bl, 1024)
    tpb = T // Bl // tm

    def body(x_ref, mod_ref, w_ref, o_ref, h_ref, hs):
        @pl.when(pl.program_id(1) == 0)
        def _():
            m = mod_ref[0]
            hs[...] = (x_ref[...] * (1.0 + m[1:2, :]) + m[0:1, :]).astype(BF16)
            h_ref[...] = hs[...]
        o_ref[...] = _dot(hs[...], w_ref[...]).astype(o_ref.dtype)

    return pl.pallas_call(
        body, name=name, grid=(T // tm, nj),
        in_specs=[pl.BlockSpec((tm, Dm), lambda i, j: (i, 0)),
                  pl.BlockSpec((1, 3, Dm), lambda i, j: (i // tpb, 0, 0)), w_spec],
        out_specs=[pl.BlockSpec((tm, tn), lambda i, j: (i, j)),
                   pl.BlockSpec((tm, Dm), lambda i, j: (i, 0))],
        out_shape=[jax.ShapeDtypeStruct((T, N), out_dtype), jax.ShapeDtypeStruct((T, Dm), BF16)],
        scratch_shapes=[pltpu.VMEM((tm, Dm), BF16)], compiler_params=_cp(2),
    )(x, mod3, w)


def modmm_bwd(dp, w, x, mod3, dxres, name, tn=None, tm=512, exchange=()):
    T, Dm = x.shape
    Bl = mod3.shape[0]
    tm = _tile(T // Bl, tm)
    tpb = T // Bl // tm
    resident = dp.ndim == 3
    if resident:
        nc, nj = dp.shape[0], 1
        dp_spec = pl.BlockSpec((nc, tm, dp.shape[2]), lambda i, j: (0, i, 0))
        w_spec = pl.BlockSpec(w.shape, lambda i, j: (0, 0, 0))
    else:
        nj, tn, w_spec = _col_chunks(w, tn)
        dp_spec = pl.BlockSpec((tm, tn), lambda i, j: (i, j))

    def body(dp_ref, w_ref, x_ref, mod_ref, dxr_ref, dx_ref, dsh_ref, dsc_ref, acc):
        i, j = pl.program_id(0), pl.program_id(1)

        @pl.when(j == 0)
        def _():
            acc[...] = jnp.zeros_like(acc)
        if resident:
            for c in range(nc):
                acc[...] += _dot_nt(dp_ref[c], w_ref[c])
        else:
            acc[...] += _dot_nt(dp_ref[...], w_ref[...])

        @pl.when(j == nj - 1)
        def _():
            dh = acc[...]
            xx = x_ref[...]
            dx_ref[...] = dxr_ref[...] + dh * (1.0 + mod_ref[0][1:2, :])

            @pl.when(i % tpb == 0)
            def _():
                dsh_ref[...] = jnp.zeros_like(dsh_ref)
                dsc_ref[...] = jnp.zeros_like(dsc_ref)
            dsh_ref[0] += _sum0(dh)
            dsc_ref[0] += _sum0(dh * xx)

    grid = (T // tm, nj)
    body, c_in, c_out, c_shape, c_scratch = host_comm(body, grid, 5, 3, exchange=exchange)
    dx, dsh, dsc, *received = pl.pallas_call(
        body, name=name, grid=grid,
        in_specs=[dp_spec, w_spec,
                  pl.BlockSpec((tm, Dm), lambda i, j: (i, 0)),
                  pl.BlockSpec((1, 3, Dm), lambda i, j: (i // tpb, 0, 0)),
                  pl.BlockSpec((tm, Dm), lambda i, j: (i, 0))] + c_in,
        out_specs=[pl.BlockSpec((tm, Dm), lambda i, j: (i, 0)),
                   pl.BlockSpec((1, 1, Dm), lambda i, j: (i // tpb, 0, 0)),
                   pl.BlockSpec((1, 1, Dm), lambda i, j: (i // tpb, 0, 0))] + c_out,
        out_shape=[jax.ShapeDtypeStruct((T, Dm), F32), jax.ShapeDtypeStruct((Bl, 1, Dm), F32),
                   jax.ShapeDtypeStruct((Bl, 1, Dm), F32)] + c_shape,
        scratch_shapes=[pltpu.VMEM((tm, Dm), F32)] + c_scratch, compiler_params=_cp(2),
    )(dp, w, x, mod3, dxres, *exchange)
    return (dx, dsh, dsc, received) if exchange else (dx, dsh, dsc)


def _ln_stats(z):
    mu = jnp.mean(z, axis=-1, keepdims=True)
    zc = z - mu
    var = jnp.mean(zc * zc, axis=-1, keepdims=True)
    rstd = lax.rsqrt(var + LN_EPS)
    return zc * rstd, rstd


def proj_post(a, w, x, mod3, lng, lnb, weight, name):
    nk, T, tk = a.shape
    Dm = w.shape[2]
    Bl = mod3.shape[0]
    tm = _tile(T // Bl, 512)
    tpb = T // Bl // tm

    def body(a_ref, w_ref, x_ref, mod_ref, g_ref, b_ref, out_ref, xn_ref):
        out = _dot(a_ref[0], w_ref[0])
        for k in range(1, nk):
            out = out + _dot(a_ref[k], w_ref[k])
        out_ref[...] = out
        z = ALPHA * x_ref[...] + (weight * (1.0 + mod_ref[0][2:3, :])) * out
        xhat, _ = _ln_stats(z)
        xn_ref[...] = xhat * g_ref[...] + b_ref[...]

    row = pl.BlockSpec((tm, Dm), lambda i: (i, 0))
    vec = pl.BlockSpec((1, Dm), lambda i: (0, 0))
    return pl.pallas_call(
        body, name=name, grid=(T // tm,),
        in_specs=[pl.BlockSpec((nk, tm, tk), lambda i: (0, i, 0)),
                  pl.BlockSpec((nk, tk, Dm), lambda i: (0, 0, 0)),
                  row, pl.BlockSpec((1, 3, Dm), lambda i: (i // tpb, 0, 0)), vec, vec],
        out_specs=[row, row],
        out_shape=[jax.ShapeDtypeStruct((T, Dm), F32), jax.ShapeDtypeStruct((T, Dm), F32)],
        compiler_params=_cp(1),
    )(a, w, x, mod3, lng, lnb)


def post_bwd(dxn, x, out, mod3, lng, w, weight, name, tm=512, gu=None, exchange=()):
    T, Dm = x.shape
    nk, tk, _ = w.shape
    Bl = mod3.shape[0]
    tm = _tile(T // Bl, tm)
    tpb = T // Bl // tm
    fused = gu is not None

    def body(dxn_ref, x_ref, out_ref, mod_ref, g_ref, w_ref, *rest):
        if fused:
            gg_ref, uu_ref = rest[:2]
            rest = rest[2:]
        dxr_ref, dout_ref, da_ref, dg_ref, db_ref, dgate_ref = rest
        i = pl.program_id(0)
        out = out_ref[...]
        dxn = dxn_ref[...]
        coef = weight * (1.0 + mod_ref[0][2:3, :])
        xhat, rstd = _ln_stats(ALPHA * x_ref[...] + coef * out)
        dyh = dxn * g_ref[...]
        dz = rstd * (dyh - jnp.mean(dyh, axis=-1, keepdims=True)
                     - xhat * jnp.mean(dyh * xhat, axis=-1, keepdims=True))
        dxr_ref[...] = ALPHA * dz
        dout = (coef * dz).astype(BF16)
        dout_ref[...] = dout

        @pl.when(i == 0)
        def _():
            dg_ref[...] = jnp.zeros_like(dg_ref)
            db_ref[...] = jnp.zeros_like(db_ref)

        @pl.when(i % tpb == 0)
        def _():
            dgate_ref[...] = jnp.zeros_like(dgate_ref)
        dg_ref[...] += _sum0(dxn * xhat)
        db_ref[...] += _sum0(dxn)
        dgate_ref[0] += _sum0((weight * out) * dz)
        for k in range(nk):
            da = _dot_nt(dout, w_ref[k])
            if fused:
                gg = gg_ref[k].astype(F32)
                s = _sigmoid(gg)
                da_ref[k] = (da * uu_ref[k].astype(F32) * (s * (1.0 + gg * (1.0 - s)))).astype(BF16)
                da_ref[nk + k] = (da * (gg * s)).astype(BF16)
            else:
                da_ref[k] = da.astype(BF16)

    row = pl.BlockSpec((tm, Dm), lambda i: (i, 0))
    vec = pl.BlockSpec((1, Dm), lambda i: (0, 0))
    wide = pl.BlockSpec((nk, tm, tk), lambda i: (0, i, 0))
    nda = 2 * nk if fused else nk
    grid = (T // tm,)
    body, c_in, c_out, c_shape, c_scratch = host_comm(body, grid, 8 if fused else 6, 6, exchange=exchange)
    *results, = pl.pallas_call(
        body, name=name, grid=grid,
        in_specs=[row, row, row, pl.BlockSpec((1, 3, Dm), lambda i: (i // tpb, 0, 0)), vec,
                  pl.BlockSpec((nk, tk, Dm), lambda i: (0, 0, 0))] + ([wide, wide] if fused else []) + c_in,
        out_specs=[row, row, pl.BlockSpec((nda, tm, tk), lambda i: (0, i, 0)),
                   vec, vec, pl.BlockSpec((1, 1, Dm), lambda i: (i // tpb, 0, 0))] + c_out,
        out_shape=[jax.ShapeDtypeStruct((T, Dm), F32), jax.ShapeDtypeStruct((T, Dm), BF16),
                   jax.ShapeDtypeStruct((nda, T, tk), BF16), jax.ShapeDtypeStruct((1, Dm), F32),
                   jax.ShapeDtypeStruct((1, Dm), F32), jax.ShapeDtypeStruct((Bl, 1, Dm), F32)] + c_shape,
        scratch_shapes=c_scratch, compiler_params=_cp(1),
    )(dxn, x, out, mod3, lng, w, *(gu if fused else ()), *exchange)
    return tuple(results[:6]) + ((results[6:],) if exchange else ())


def mm_tn(a, b, name, bw=None, a_copies=False):
    a3, b3 = a.ndim == 3, b.ndim == 3
    nk, T, tk = a.shape if a3 else (1,) + a.shape
    if a_copies:
        nk = 1
    nc, wn = (b.shape[0], b.shape[2]) if b3 else (b.shape[1] // bw, bw)
    tt = _tile(T, 2048)
    nt = T // tt

    def body(a_ref, b_ref, o_ref, ob_ref):
        t = pl.program_id(2)

        @pl.when(t == 0)
        def _():
            o_ref[...] = jnp.zeros_like(o_ref)
        o_ref[...] += _dot_tn(a_ref[...], b_ref[...])

        @pl.when(t == nt - 1)
        def _():
            ob_ref[...] = o_ref[...].astype(BF16)

    a_spec = (pl.BlockSpec((None, tt, tk), lambda k, c, t: (k, t, 0)) if a3
              else pl.BlockSpec((tt, tk), lambda k, c, t: (t, 0)))
    b_spec = (pl.BlockSpec((None, tt, wn), lambda k, c, t: (c, t, 0)) if b3
              else pl.BlockSpec((tt, wn), lambda k, c, t: (t, c)))
    o_spec = pl.BlockSpec((None, tk, wn), lambda k, c, t: (k * nc + c, 0, 0))
    return pl.pallas_call(
        body, name=name, grid=(nk, nc, nt), in_specs=[a_spec, b_spec], out_specs=[o_spec, o_spec],
        out_shape=[jax.ShapeDtypeStruct((nk * nc, tk, wn), F32), jax.ShapeDtypeStruct((nk * nc, tk, wn), BF16)],
        compiler_params=_cp(3),
    )(a, b)


def ffn_in(x, mod3, w, name, gather=()):
    T, Dm = x.shape
    nj, tf = w.shape[0] // 2, w.shape[2]
    Bl = mod3.shape[0]
    tm = _tile(T // Bl, 1024)
    tpb = T // Bl // tm

    def body(x_ref, mod_ref, wg_ref, wu_ref, a_ref, g_ref, u_ref, h_ref):
        m = mod_ref[0]
        h = (x_ref[...] * (1.0 + m[1:2, :]) + m[0:1, :]).astype(BF16)
        h_ref[...] = h
        g = _dot(h, wg_ref[...])
        u = _dot(h, wu_ref[...])
        a_ref[...] = (g * _sigmoid(g) * u).astype(BF16)
        g_ref[...] = g.astype(BF16)
        u_ref[...] = u.astype(BF16)

    col = pl.BlockSpec((None, tm, tf), lambda j, i: (j, i, 0))
    grid = (nj, T // tm)
    body, c_in, c_out, c_shape, c_scratch = host_comm(body, grid, 4, 4, gather=gather)
    a, g, u, h, *gathered = pl.pallas_call(
        body, name=name, grid=grid,
        in_specs=[pl.BlockSpec((tm, Dm), lambda j, i: (i, 0)),
                  pl.BlockSpec((1, 3, Dm), lambda j, i: (i // tpb, 0, 0)),
                  pl.BlockSpec((None, Dm, tf), lambda j, i: (j, 0, 0)),
                  pl.BlockSpec((None, Dm, tf), lambda j, i: (nj + j, 0, 0))] + c_in,
        out_specs=[col, col, col, pl.BlockSpec((None, tm, Dm), lambda j, i: (j, i, 0))] + c_out,
        out_shape=[jax.ShapeDtypeStruct((nj, T, tf), BF16)] * 3 + [jax.ShapeDtypeStruct((nj, T, Dm), BF16)]
        + c_shape,
        scratch_shapes=c_scratch, compiler_params=_cp(2),
    )(x, mod3, w, w, *gather)
    return a, g, u, h, gathered


def loss_head(y, tgt, name):
    T, Dm = y.shape
    tm = _tile(T, 512)
    nt = T // tm

    def body(y_ref, t_ref, dy_ref, l_ref, acc):
        i = pl.program_id(0)

        @pl.when(i == 0)
        def _():
            acc[...] = jnp.zeros_like(acc)
        e = y_ref[...] - t_ref[...]
        dy_ref[...] = e * (1.0 / Dm)
        acc[...] += _sum0(e * e)

        @pl.when(i == nt - 1)
        def _():
            l_ref[...] = jnp.broadcast_to(_sum1(acc[...]) * (0.5 / Dm), l_ref.shape)

    return pl.pallas_call(
        body, name=name, grid=(nt,),
        in_specs=[pl.BlockSpec((tm, Dm), lambda i: (i, 0)), pl.BlockSpec((tm, Dm), lambda i: (i, 0))],
        out_specs=[pl.BlockSpec((tm, Dm), lambda i: (i, 0)), pl.BlockSpec((1, 128), lambda i: (0, 0))],
        out_shape=[jax.ShapeDtypeStruct((T, Dm), F32), jax.ShapeDtypeStruct((1, 128), F32)],
        scratch_shapes=[pltpu.VMEM((1, Dm), F32)], compiler_params=_cp(1),
    )(y, tgt)


def adamw(w, g, m, v, name):
    R, C = w.shape
    tr = _tile(R, 512) if R % 8 == 0 else R

    def body(w_ref, g_ref, m_ref, v_ref, d_ref, nm_ref, nv_ref):
        gg = g_ref[...]
        mm = ADAM_B1 * m_ref[...] + (1.0 - ADAM_B1) * gg
        vv = ADAM_B2 * v_ref[...] + (1.0 - ADAM_B2) * (gg * gg)
        m_hat = mm / (1.0 - ADAM_B1 ** ADAM_STEP)
        v_hat = vv / (1.0 - ADAM_B2 ** ADAM_STEP)
        d_ref[...] = -ADAM_LR * (m_hat / (jnp.sqrt(v_hat) + ADAM_EPS) + ADAM_WD * w_ref[...])
        nm_ref[...] = mm
        nv_ref[...] = vv

    spec = pl.BlockSpec((tr, C), lambda i: (i, 0))
    return pl.pallas_call(
        body, name=name, grid=(R // tr,), in_specs=[spec] * 4, out_specs=[spec] * 3,
        out_shape=[jax.ShapeDtypeStruct((R, C), F32)] * 3, compiler_params=_cp(1),
    )(w, g, m, v)


def ada_fwd(c_all, ada_w, ada_b_cols, name):
    Lr, Dm, Nc = ada_w.shape
    Bg = c_all.shape[0]

    def body(c_ref, w_ref, b_ref, o_ref):
        cc = c_ref[...]
        cond = cc * _sigmoid(cc)
        o_ref[0] = _dot(cond.astype(BF16), w_ref[0].astype(BF16)) + b_ref[0]

    return pl.pallas_call(
        body, name=name, grid=(Lr,),
        in_specs=[pl.BlockSpec((Bg, Dm), lambda l: (0, 0)),
                  pl.BlockSpec((1, Dm, Nc), lambda l: (l, 0, 0)),
                  pl.BlockSpec((1, 1, Nc), lambda l: (l, 0, 0))],
        out_specs=pl.BlockSpec((1, Bg, Nc), lambda l: (l, 0, 0)),
        out_shape=jax.ShapeDtypeStruct((Lr, Bg, Nc), F32), compiler_params=_cp(1),
    )(c_all, ada_w, ada_b_cols)


def ada_bwd(c_all_t, dmod_cols, dmod_all, name):
    Dm, Bg = c_all_t.shape
    Lr, _, Nc = dmod_cols.shape
    Nf = dmod_all.shape[2]

    def body(c_ref, dm_ref, da_ref, gw_ref, gb_ref):
        cc = c_ref[...]
        cond = cc * _sigmoid(cc)
        gw_ref[0] = _dot(cond.astype(BF16), dm_ref[0].astype(BF16))
        gb_ref[0] = _sum0(da_ref[0])

    return pl.pallas_call(
        body, name=name, grid=(Lr,),
        in_specs=[pl.BlockSpec((Dm, Bg), lambda l: (0, 0)),
                  pl.BlockSpec((1, Bg, Nc), lambda l: (l, 0, 0)),
                  pl.BlockSpec((1, Bg, Nf), lambda l: (l, 0, 0))],
        out_specs=[pl.BlockSpec((1, Dm, Nc), lambda l: (l, 0, 0)),
                   pl.BlockSpec((1, 1, Nf), lambda l: (l, 0, 0))],
        out_shape=[jax.ShapeDtypeStruct((Lr, Dm, Nc), F32), jax.ShapeDtypeStruct((Lr, 1, Nf), F32)],
        compiler_params=_cp(1),
    )(c_all_t, dmod_cols, dmod_all)


def _conv_taps(x, w, rows):
    shifted = [x]
    c = w[3:4, :] * x
    for k in range(1, 4):
        xs = jnp.where(rows >= k, pltpu.roll(x, k, 0), 0.0)
        shifted.append(xs)
        c = c + w[3 - k:4 - k, :] * xs
    return c, shifted


def conv_silu(proj3, conv_w, name):
    Bl, S, _ = proj3.shape
    ncb = conv_w.shape[1] // 128

    def body(x_ref, w_ref, o_ref):
        rows = lax.broadcasted_iota(jnp.int32, (S, 128), 0)
        c, _ = _conv_taps(_round(x_ref[0]), _round(w_ref[...]), rows)
        o_ref[0] = c * _sigmoid(c)

    return pl.pallas_call(
        body, name=name, grid=(Bl, ncb),
        in_specs=[pl.BlockSpec((1, S, 128), lambda b, j: (b, 0, j)),
                  pl.BlockSpec((4, 128), lambda b, j: (0, j))],
        out_specs=pl.BlockSpec((1, S, 128), lambda b, j: (b, 0, j)),
        out_shape=jax.ShapeDtypeStruct((Bl, S, conv_w.shape[1]), F32), compiler_params=_cp(2),
    )(proj3, conv_w)


def conv_silu_bwd(proj3, conv_w, dq, dk, name):
    Bl, S, _ = proj3.shape
    nq = dq.shape[2] // 128

    def body(x_ref, w_ref, dq_ref, dk_ref, dx_ref, dw_ref):
        j = pl.program_id(1)
        rows = lax.broadcasted_iota(jnp.int32, (S, 128), 0)
        w = _round(w_ref[...])
        c, shifted = _conv_taps(_round(x_ref[0]), w, rows)
        s = _sigmoid(c)
        dact = jnp.where(j < nq, dq_ref[0], dk_ref[0])
        dc = _round(dact * (s * (1.0 + c * (1.0 - s))))
        dx = w[3:4, :] * dc
        dws = [_sum0(dc * shifted[0])]
        for k in range(1, 4):
            up = jnp.where(rows < S - k, pltpu.roll(dc, S - k, 0), 0.0)
            dx = dx + w[3 - k:4 - k, :] * up
            dws.append(_sum0(dc * shifted[k]))
        dx_ref[0] = dx.astype(BF16)
        tap = lax.broadcasted_iota(jnp.int32, (4, 128), 0)
        dw_ref[0] = functools.reduce(lambda a, b: a + b, [jnp.where(tap == 3 - k, dws[k], 0.0) for k in range(4)])

    return pl.pallas_call(
        body, name=name, grid=(Bl, 2 * nq),
        in_specs=[pl.BlockSpec((1, S, 128), lambda b, j: (b, 0, j)),
                  pl.BlockSpec((4, 128), lambda b, j: (0, j)),
                  pl.BlockSpec((1, S, 128), lambda b, j: (b, 0, jnp.minimum(j, nq - 1))),
                  pl.BlockSpec((1, S, 128), lambda b, j: (b, 0, jnp.maximum(j - nq, 0)))],
        out_specs=[pl.BlockSpec((1, S, 128), lambda b, j: (b, 0, j)),
                   pl.BlockSpec((1, 4, 128), lambda b, j: (b, 0, j))],
        out_shape=[jax.ShapeDtypeStruct((Bl, S, 2 * nq * 128), BF16),
                   jax.ShapeDtypeStruct((Bl, 4, 2 * nq * 128), F32)],
        compiler_params=_cp(2),
    )(proj3, conv_w, dq, dk)


def _log_sigmoid(a):
    return jnp.minimum(a, 0.0) - jnp.log(1.0 + jnp.exp(-jnp.abs(a)))


def _chunk_state(kc, vc, gi, bcum, b_last, C, n, m):
    a = b_last - bcum + gi
    m_loc = jnp.max(a, axis=0, keepdims=True)
    wa = jnp.exp(a - m_loc)
    c_loc = _dot_tn((wa * vc).astype(BF16), kc.astype(BF16))
    n_loc = _sum0(_round(wa) * _round(kc))
    m_new = jnp.maximum(b_last + m, m_loc)
    sp = jnp.exp(b_last + m - m_new)
    sl = jnp.exp(m_loc - m_new)
    return sp * C + sl * c_loc, sp * n + sl * n_loc, m_new, wa, sp, sl


def _chunk_out(qs, kc, vc, gi_row, bcum, bcum_row, low, C, n, m):
    inter_log = bcum + m
    dlog = jnp.where(low, bcum - bcum_row + gi_row, NEG)
    m_i = jnp.maximum(inter_log, jnp.max(dlog, axis=1, keepdims=True))
    dm = jnp.exp(dlog - m_i)
    iw = jnp.exp(inter_log - m_i)
    qs_b, k_b, v_b = qs.astype(BF16), kc.astype(BF16), vc.astype(BF16)
    sc = _dot_nt(qs_b, k_b) * dm
    qc_ = _dot_nt(qs_b, C.astype(BF16))
    qn = _sum1(_round(qs) * _round(n))
    num = _dot(sc.astype(BF16), v_b) + iw * qc_
    den = _sum1(sc) + iw * qn
    floor = jnp.exp(-m_i)
    dn = jnp.maximum(jnp.abs(den), floor)
    return dict(hc=num / dn, den=den, dn=dn, floor=floor, sc=sc, dm=dm, iw=iw, qc=qc_, qn=qn,
                qs_b=qs_b, k_b=k_b, v_b=v_b)


def _cell_consts(L):
    ri = lax.broadcasted_iota(jnp.int32, (L, L), 0)
    ci = lax.broadcasted_iota(jnp.int32, (L, L), 1)
    return ri == ci, ci <= ri, ri <= ci


def _load_chunk(q_ref, k_ref, v_ref, G, off, L, h, lane):
    hh = h % 2
    qmask = (lane >= M_DQK * hh) & (lane < M_DQK * (hh + 1))
    pair = pl.ds(128 * (h // 2), 128)
    qc = jnp.where(qmask, q_ref[0, pl.ds(off, L), pair], 0.0)
    kc = jnp.where(qmask, k_ref[0, pl.ds(off, L), pair], 0.0)
    vc = v_ref[0, pl.ds(off, L), pl.ds(M_DV * h, M_DV)]
    gi = _sum1(jnp.where(lane == h, G, 0.0))
    gf = _sum1(jnp.where(lane == h + HEADS, G, 0.0))
    return qmask, qc, kc, vc, gi, gf


def _gate_rows(gi, gf, eye, low, upp):
    lf = _log_sigmoid(gf)
    lf_row = _sum0(jnp.where(eye, lf, 0.0))
    gi_row = _sum0(jnp.where(eye, gi, 0.0))
    bcum = _sum1(jnp.where(low, lf_row, 0.0))
    bcum_row = _sum0(jnp.where(upp, lf, 0.0))
    b_last = _sum0(lf)
    return gi_row, bcum, bcum_row, b_last


def _cell_specs(SB, cpb, blk):
    def seq(width, col):
        return pl.BlockSpec((1, SB, width), lambda b, s: (b, blk(s), col))

    def state(rows):
        return pl.BlockSpec((1, HEADS, cpb, rows, 128), lambda b, s: (b, 0, blk(s), 0, 0))

    ins = [seq(D // 2, 0), seq(D // 2, 1), seq(D, 1), seq(D, 2), seq(128, 3 * D // 128),
           pl.BlockSpec((1, D), lambda b, s: (0, 0)), pl.BlockSpec((1, 128), lambda b, s: (0, 0))]
    return ins, [state(M_DV), state(1), state(1)], seq


def mlstm_cell_fwd(qk3, proj3, gain, gbias, name, gather=()):
    Bl, S, _ = qk3.shape
    L = M_CHUNK
    SB = min(M_SLAB, S)
    cpb, nc, nsb = SB // L, S // L, S // SB
    scale = M_DQK ** -0.5

    def body(q_ref, k_ref, v_ref, o_ref, g_ref, gain_ref, gb_ref, y_ref, cst_ref, nst_ref, mst_ref, *state):
        C_s, n_s, m_s = state[:HEADS], state[HEADS:2 * HEADS], state[2 * HEADS:]

        @pl.when(pl.program_id(1) == 0)
        def _():
            for ref in state:
                ref[...] = jnp.zeros_like(ref)
        lane = lax.broadcasted_iota(jnp.int32, (L, 128), 1)
        eye, low, upp = _cell_consts(L)

        def step(c, carry):
            off = pl.multiple_of(c * L, L)
            G = g_ref[0, pl.ds(off, L), :] + gb_ref[...]
            for h in range(HEADS):
                C, n, mb = C_s[h][...], n_s[h][...], m_s[h][...]
                cst_ref[0, h, c] = C
                nst_ref[0, h, c] = n
                mst_ref[0, h, c] = mb
                m = mb[:, 0:1]
                _, qc, kc, vc, gi, gf = _load_chunk(q_ref, k_ref, v_ref, G, off, L, h, lane)
                gi_row, bcum, bcum_row, b_last = _gate_rows(gi, gf, eye, low, upp)
                r = _chunk_out(qc * scale, kc, vc, gi_row, bcum, bcum_row, low, C, n, m)
                hc = r["hc"]
                hn = hc * lax.rsqrt(jnp.mean(hc * hc, axis=-1, keepdims=True) + RMS_EPS)
                cols = pl.ds(M_DV * h, M_DV)
                oc = o_ref[0, pl.ds(off, L), cols]
                y_ref[0, pl.ds(off, L), cols] = (_sigmoid(oc) * hn * gain_ref[:, cols]).astype(BF16)
                C2, n2, m2, _, _, _ = _chunk_state(kc, vc, gi, bcum, b_last, C, n, m)
                C_s[h][...] = C2
                n_s[h][...] = n2
                m_s[h][...] = jnp.broadcast_to(m2, (1, 128))
            return carry

        lax.fori_loop(0, cpb, step, 0)

    ins, states, seq = _cell_specs(SB, cpb, lambda s: s)
    grid = (Bl, nsb)
    body, c_in, c_out, c_shape, c_scratch = host_comm(body, grid, 7, 4, gather=gather)
    return pl.pallas_call(
        body, name=name, grid=grid, in_specs=ins + c_in, out_specs=[seq(D, 0)] + states + c_out,
        out_shape=[jax.ShapeDtypeStruct((Bl, S, D), BF16),
                   jax.ShapeDtypeStruct((Bl, HEADS, nc, M_DV, 128), F32),
                   jax.ShapeDtypeStruct((Bl, HEADS, nc, 1, 128), F32),
                   jax.ShapeDtypeStruct((Bl, HEADS, nc, 1, 128), F32)] + c_shape,
        scratch_shapes=[pltpu.VMEM((M_DV, 128), F32)] * HEADS + [pltpu.VMEM((1, 128), F32)] * (2 * HEADS) + c_scratch,
        compiler_params=_cp(2),
    )(qk3, qk3, proj3, proj3, proj3, gain, gbias, *gather)


def mlstm_cell_bwd(qk3, proj3, gain, gbias, dy3, states, name, exchange=()):
    Bl, S, _ = qk3.shape
    L = M_CHUNK
    SB = min(M_SLAB, S)
    cpb, nsb = SB // L, S // SB
    scale = M_DQK ** -0.5

    def body(q_ref, k_ref, v_ref, o_ref, g_ref, gain_ref, gb_ref, cst_ref, nst_ref, mst_ref, dy_ref,
             dq_ref, dk_ref, dv_ref, do_ref, dg_ref, dgain_ref, dgb_ref, *state):
        dC_s, dn_s, dgain_s, dgb_s = state[:HEADS], state[HEADS:2 * HEADS], state[2 * HEADS:3 * HEADS], state[-1]
        s = pl.program_id(1)

        @pl.when(s == 0)
        def _():
            for ref in state:
                ref[...] = jnp.zeros_like(ref)
        lane = lax.broadcasted_iota(jnp.int32, (L, 128), 1)
        rowi = lax.broadcasted_iota(jnp.int32, (L, 1), 0)
        eye, low, upp = _cell_consts(L)

        def bstep(t, carry):
            c = cpb - 1 - t
            off = pl.multiple_of(c * L, L)
            G = g_ref[0, pl.ds(off, L), :] + gb_ref[...]
            slab = jnp.zeros((L, 128), F32)
            dq_pair = dk_pair = None
            for h in range(HEADS):
                cols = pl.ds(M_DV * h, M_DV)
                gain_h = gain_ref[:, cols]
                C, n, m = cst_ref[0, h, c], nst_ref[0, h, c], mst_ref[0, h, c][:, 0:1]
                dC_n, dn_n = dC_s[h][...], dn_s[h][...]
                qmask, qc, kc, vc, gi, gf = _load_chunk(q_ref, k_ref, v_ref, G, off, L, h, lane)
                gi_row, bcum, bcum_row, b_last = _gate_rows(gi, gf, eye, low, upp)
                qs = qc * scale
                r = _chunk_out(qs, kc, vc, gi_row, bcum, bcum_row, low, C, n, m)
                _, _, _, wa, sp, sl = _chunk_state(kc, vc, gi, bcum, b_last, C, n, m)
                hc, den, dn, sc, dm, iw, qn = r["hc"], r["den"], r["dn"], r["sc"], r["dm"], r["iw"], r["qn"]
                qs_b, k_b, v_b = r["qs_b"], r["k_b"], r["v_b"]
                dy = dy_ref[0, pl.ds(off, L), cols].astype(F32)
                oc = o_ref[0, pl.ds(off, L), cols]
                sig_o = _sigmoid(oc)
                rr = lax.rsqrt(jnp.mean(hc * hc, axis=-1, keepdims=True) + RMS_EPS)
                hn = hc * rr
                dgain_s[h][...] += _sum0(dy * sig_o * hn)
                do_ref[0, pl.ds(off, L), cols] = (
                    dy * hn * gain_h * sig_o * (1.0 - sig_o)).astype(BF16)
                dhn = dy * sig_o * gain_h
                dhc = rr * dhn - hc * (rr * rr * rr) * jnp.mean(dhn * hc, axis=-1, keepdims=True)
                dnum = dhc / dn
                gden = -_sum1(dhc * hc) / dn
                dden = jnp.where(jnp.abs(den) > r["floor"], gden * jnp.sign(den), 0.0)
                dnum_b = dnum.astype(BF16)
                dsc = _dot_nt(dnum_b, v_b) + dden
                dv = _dot_tn(sc.astype(BF16), dnum_b)
                diw = _sum1(dnum * r["qc"]) + dden * qn
                dqc_b = (iw * dnum).astype(BF16)
                wq = iw * dden
                dqs = _dot(dqc_b, C.astype(BF16)) + wq * n
                dC_out = _dot_tn(dqc_b, qs_b)
                dn_out = _sum0(wq * qs)
                dS_b = (dsc * dm).astype(BF16)
                gm = dsc * sc
                dqs = dqs + _dot(dS_b, k_b)
                dk = _dot_tn(dS_b, qs_b)
                dbc = _sum1(gm) + diw * iw
                colg = _sum0(gm)
                dC_p = sp * dC_n + dC_out
                dn_p = sp * dn_n + dn_out
                dcl_b = (sl * dC_n).astype(BF16)
                dn_loc = sl * dn_n
                dsp = _sum1(_sum0(dC_n * C)) + _sum1(dn_n * n)
                db_last = dsp * sp
                t1 = _dot(v_b, dcl_b) + dn_loc
                dwa = _sum1(t1 * kc)
                dv = dv + wa * _dot_nt(k_b, dcl_b)
                dk = dk + wa * t1
                da = dwa * wa
                db_last = db_last + _sum0(da)
                dbc = dbc - da + jnp.where(rowi == L - 1, db_last, 0.0)
                dbc_row = _sum0(jnp.where(eye, dbc, 0.0)) - colg
                dgi = da + _sum1(jnp.where(eye, colg, 0.0))
                dlf = _sum1(jnp.where(upp, dbc_row, 0.0))
                dgf = dlf * _sigmoid(-gf)
                dq = jnp.where(qmask, dqs * scale, 0.0)
                dk = jnp.where(qmask, dk, 0.0)
                slab = slab + jnp.where(lane == h, dgi, 0.0) + jnp.where(lane == h + HEADS, dgf, 0.0)
                dv_ref[0, pl.ds(off, L), cols] = dv.astype(BF16)
                dC_s[h][...] = dC_p
                dn_s[h][...] = dn_p
                if h % 2 == 0:
                    dq_pair, dk_pair = dq, dk
                else:
                    pair = pl.ds(128 * (h // 2), 128)
                    dq_ref[0, pl.ds(off, L), pair] = dq_pair + dq
                    dk_ref[0, pl.ds(off, L), pair] = dk_pair + dk
            dg_ref[0, pl.ds(off, L), :] = slab
            dgb_s[...] += _sum0(slab)
            return carry

        lax.fori_loop(0, cpb, bstep, 0)

        @pl.when(s == nsb - 1)
        def _():
            for h in range(HEADS):
                dgain_ref[0, :, pl.ds(M_DV * h, M_DV)] = dgain_s[h][...]
            dgb_ref[0] = dgb_s[...]

    ins, states_specs, seq = _cell_specs(SB, cpb, lambda s: nsb - 1 - s)
    once = lambda width: pl.BlockSpec((1, 1, width), lambda b, s: (b, 0, 0))
    grid = (Bl, nsb)
    body, c_in, c_out, c_shape, c_scratch = host_comm(body, grid, 11, 7, exchange=exchange)
    return pl.pallas_call(
        body, name=name, grid=grid, in_specs=ins + states_specs + [seq(D, 0)] + c_in,
        out_specs=[seq(D // 2, 0), seq(D // 2, 0), seq(D, 0), seq(D, 0), seq(128, 0), once(D), once(128)] + c_out,
        out_shape=[jax.ShapeDtypeStruct((Bl, S, D // 2), F32), jax.ShapeDtypeStruct((Bl, S, D // 2), F32),
                   jax.ShapeDtypeStruct((Bl, S, D), BF16), jax.ShapeDtypeStruct((Bl, S, D), BF16),
                   jax.ShapeDtypeStruct((Bl, S, 128), F32), jax.ShapeDtypeStruct((Bl, 1, D), F32),
                   jax.ShapeDtypeStruct((Bl, 1, 128), F32)] + c_shape,
        scratch_shapes=[pltpu.VMEM((M_DV, 128), F32)] * HEADS + [pltpu.VMEM((1, 128), F32)] * (2 * HEADS + 1)
        + c_scratch,
        compiler_params=_cp(2),
    )(qk3, qk3, proj3, proj3, proj3, gain, gbias, *states, dy3, *exchange)


def _attn_scores(q, kc, kp, n, row, col, scale):
    s_c = jnp.where(col <= row, _dot_nt(q, kc) * scale, NEG)
    s_p = jnp.where(jnp.logical_and(col >= row, n > 0), _dot_nt(q, kp) * scale, NEG)
    return s_c, s_p


def _to_streams(src, dst, tmp, dil, Sd):
    if dil == 1:
        dst[...] = src[...].astype(dst.dtype)
        return
    if src.dtype != F32:
        tmp[...] = src[...].astype(F32)
        src = tmp
    for r in range(dil):
        dst[pl.ds(r * Sd, Sd), :] = src[pl.ds(r, Sd, stride=dil), :].astype(dst.dtype)


def _from_streams(src, dst, dil, Sd):
    if dil == 1:
        dst[...] = src[...]
        return
    for r in range(dil):
        dst[pl.ds(r, Sd, stride=dil), :] = src[pl.ds(r * Sd, Sd), :]


def attn_fwd(proj, Bl, S, g, dil, name):
    Sd = S // dil
    nb = Sd // A_BLK
    scale = A_BLK ** -0.5
    pv = proj.reshape(Bl, S, A_PROJ)

    def body(q_ref, k_ref, v_ref, o_ref, l_ref, tmp, qs, ks, vs, os_, ls):
        row = lax.broadcasted_iota(jnp.int32, (A_BLK, A_BLK), 0)
        col = lax.broadcasted_iota(jnp.int32, (A_BLK, A_BLK), 1)
        for src, dst in ((q_ref, qs), (k_ref, ks), (v_ref, vs)):
            _to_streams(src.at[0], dst, tmp, dil, Sd)

        def step(i, carry):
            n = i % nb
            off = pl.multiple_of(i * A_BLK, A_BLK)
            offp = pl.multiple_of(jnp.maximum(i - 1, 0) * A_BLK, A_BLK)
            q = qs[pl.ds(off, A_BLK), :]
            s_c, s_p = _attn_scores(q, ks[pl.ds(off, A_BLK), :], ks[pl.ds(offp, A_BLK), :], n, row, col, scale)
            m = jnp.maximum(jnp.max(s_c, axis=1, keepdims=True), jnp.max(s_p, axis=1, keepdims=True))
            p_c = jnp.exp(s_c - m)
            p_p = jnp.exp(s_p - m)
            den = _sum1(p_c) + _sum1(p_p)
            o = _dot(p_c.astype(BF16), vs[pl.ds(off, A_BLK), :]) + _dot(p_p.astype(BF16), vs[pl.ds(offp, A_BLK), :])
            os_[pl.ds(off, A_BLK), :] = o / den
            ls[pl.ds(off, A_BLK), :] = jnp.broadcast_to(m + jnp.log(den), (A_BLK, 128))
            return carry

        lax.fori_loop(0, dil * nb, step, 0)
        _from_streams(os_, o_ref.at[0], dil, Sd)
        _from_streams(ls, l_ref.at[0], dil, Sd)

    def spec(j):
        return pl.BlockSpec((1, S, 128), lambda b, h: (b, 0, g * 24 + j * HEADS + h))

    ospec = pl.BlockSpec((1, S, 128), lambda b, h: (b, 0, h))
    o, lse = pl.pallas_call(
        body, name=name, grid=(Bl, HEADS),
        in_specs=[spec(0), spec(1), spec(2)], out_specs=[ospec, ospec],
        out_shape=[jax.ShapeDtypeStruct((Bl, S, D), F32)] * 2,
        scratch_shapes=[pltpu.VMEM((S, 128), F32)] + [pltpu.VMEM((S, 128), BF16)] * 3 + [pltpu.VMEM((S, 128), F32)] * 2,
        compiler_params=_cp(2),
    )(pv, pv, pv)
    return o.reshape(Bl * S, D), lse.reshape(Bl * S, D)


def attn_merge(os_, lses, name):
    T = os_[0].shape[0]
    tm = _tile(T, 512)
    ng = len(os_)

    def body(*refs):
        o_refs, l_refs = refs[:ng], refs[ng:2 * ng]
        ob_ref, of_ref, lt_ref = refs[2 * ng:]
        ls = [r[...] for r in l_refs]
        m = functools.reduce(jnp.maximum, ls)
        ws = [jnp.exp(l - m) for l in ls]
        den = functools.reduce(lambda a, b: a + b, ws)
        o = functools.reduce(lambda a, b: a + b, [w * r[...] for w, r in zip(ws, o_refs)]) / den
        of_ref[...] = o
        ob_ref[...] = o.astype(BF16)
        lt_ref[...] = m + jnp.log(den)

    spec = pl.BlockSpec((tm, D), lambda i: (i, 0))
    return pl.pallas_call(
        body, name=name, grid=(T // tm,), in_specs=[spec] * (2 * ng), out_specs=[spec] * 3,
        out_shape=[jax.ShapeDtypeStruct((T, D), BF16), jax.ShapeDtypeStruct((T, D), F32),
                   jax.ShapeDtypeStruct((T, D), F32)],
        compiler_params=_cp(1),
    )(*os_, *lses)


def attn_bwd(proj, do, o, lse, Bl, S, g, dil, name):
    Sd = S // dil
    nb = Sd // A_BLK
    scale = A_BLK ** -0.5
    pv = proj.reshape(Bl, S, A_PROJ)
    dov, ov, lv = (t.reshape(Bl, S, D) for t in (do, o, lse))

    def body(q_ref, k_ref, v_ref, do_ref, o_ref, l_ref, dq_ref, dk_ref, dv_ref,
             tmp, qs, ks, vs, dos, dls, lts, dq_s, dk_s, dv_s):
        row = lax.broadcasted_iota(jnp.int32, (A_BLK, A_BLK), 0)
        col = lax.broadcasted_iota(jnp.int32, (A_BLK, A_BLK), 1)
        for src, dst in ((q_ref, qs), (k_ref, ks), (v_ref, vs), (do_ref, dos), (l_ref, lts)):
            _to_streams(src.at[0], dst, tmp, dil, Sd)
        tmp[...] = jnp.broadcast_to(_sum1(do_ref[0].astype(F32) * o_ref[0]), (S, 128))
        _to_streams(tmp, dls, None, dil, Sd)
        dk_s[...] = jnp.zeros_like(dk_s)
        dv_s[...] = jnp.zeros_like(dv_s)

        def step(i, carry):
            n = i % nb
            off = pl.multiple_of(i * A_BLK, A_BLK)
            offp = pl.multiple_of(jnp.maximum(i - 1, 0) * A_BLK, A_BLK)
            q = qs[pl.ds(off, A_BLK), :]
            kc, kp = ks[pl.ds(off, A_BLK), :], ks[pl.ds(offp, A_BLK), :]
            vc, vp = vs[pl.ds(off, A_BLK), :], vs[pl.ds(offp, A_BLK), :]
            do_b = dos[pl.ds(off, A_BLK), :]
            delta = dls[pl.ds(off, A_BLK), :][:, 0:1]
            lt = lts[pl.ds(off, A_BLK), :][:, 0:1]
            s_c, s_p = _attn_scores(q, kc, kp, n, row, col, scale)
            p_c = jnp.exp(s_c - lt)
            p_p = jnp.exp(s_p - lt)
            ds_c = (p_c * (_dot_nt(do_b, vc) - delta) * scale).astype(BF16)
            ds_p = (p_p * (_dot_nt(do_b, vp) - delta) * scale).astype(BF16)
            dq_s[pl.ds(off, A_BLK), :] = _dot(ds_c, kc) + _dot(ds_p, kp)
            dk_s[pl.ds(off, A_BLK), :] += _dot_tn(ds_c, q)
            dk_s[pl.ds(offp, A_BLK), :] += _dot_tn(ds_p, q)
            dv_s[pl.ds(off, A_BLK), :] += _dot_tn(p_c.astype(BF16), do_b)
            dv_s[pl.ds(offp, A_BLK), :] += _dot_tn(p_p.astype(BF16), do_b)
            return carry

        lax.fori_loop(0, dil * nb, step, 0)
        for src, dst in ((dq_s, dq_ref), (dk_s, dk_ref), (dv_s, dv_ref)):
            _from_streams(src, tmp, dil, Sd)
            dst[0] = tmp[...].astype(BF16)

    def spec(j):
        return pl.BlockSpec((1, S, 128), lambda b, h: (b, 0, g * 24 + j * HEADS + h))

    ospec = pl.BlockSpec((1, S, 128), lambda b, h: (b, 0, h))
    slab = lambda dt: pltpu.VMEM((S, 128), dt)
    outs = pl.pallas_call(
        body, name=name, grid=(Bl, HEADS),
        in_specs=[spec(0), spec(1), spec(2), ospec, ospec, ospec], out_specs=[ospec] * 3,
        out_shape=[jax.ShapeDtypeStruct((Bl, S, D), BF16)] * 3,
        scratch_shapes=[slab(F32)] + [slab(BF16)] * 4 + [slab(F32)] * 5,
        compiler_params=_cp(2),
    )(pv, pv, pv, dov, ov, lv)
    return [t.reshape(Bl * S, D) for t in outs]


def _as_slots(pair, shape):
    return tuple(t.reshape(shape) for t in pair)


def ffn_fwd(x, mod3, w_in, w_out, lng, lnb, tag, gather=()):
    a, g, u, h, gathered = ffn_in(x, mod3, w_in, tag + "_in", gather=gather)
    out, xn = proj_post(a, w_out, x, mod3, lng, lnb, 0.5, tag + "_out")
    return xn, (x, out, g, u, h, a), gathered


def ffn_bwd(dxn, saved, mod3, w_in, w_out, lng, tag, exchange=(), exchange_own=False):
    x, out, g, u, h, a = saved
    dxres, dout, dgu, dlg, dlb, dgate, *received = post_bwd(dxn, x, out, mod3, lng, w_out, 0.5, tag + "_outb",
                                                            tm=256, gu=(g, u), exchange=exchange)
    dw_in = mm_tn(h, dgu, tag + "_dwin", a_copies=True)
    dw_out = _as_slots(mm_tn(a, dout, tag + "_dwout", bw=D), (N_DEV, D_FF // N_DEV, D))
    dx, dsh, dsc, *own = modmm_bwd(dgu, w_in, x, mod3, dxres, tag + "_inb", tm=256,
                                   exchange=[dw_in[1], dw_out[1]] if exchange_own else ())
    dmod3 = jnp.concatenate([dsh, dsc, dgate], axis=1)
    return dx, [dw_in, dw_out], dlg, dlb, dmod3, (received[0] if received else []), (own[0] if own else [])


def mlstm_fwd(x, mod3, w_in, w_out, conv_w, gain, gbias, lng, lnb, Bl, S, gather=()):
    proj, h = modmm(x, mod3, w_in, F32, "ml_in", tn=M_PROJ_PAD // 5)
    proj3 = proj.reshape(Bl, S, M_PROJ_PAD)
    qk3 = conv_silu(proj3, conv_w, "ml_conv")
    y3, *rest = mlstm_cell_fwd(qk3, proj3, gain, gbias, "ml_cell", gather=gather)
    states, gathered = rest[:3], rest[3:]
    y = y3.reshape(Bl * S, D)
    out, xn = proj_post(y[None], w_out, x, mod3, lng, lnb, 1.0, "ml_out")
    return xn, (x, out, h, proj3, qk3, y, states), gathered


def mlstm_bwd(dxn, saved, mod3, w_in, w_out, conv_w, gain, gbias, lng, Bl, S, exchange=()):
    x, out, h, proj3, qk3, y, states = saved
    dxres, dout, dy, dlg, dlb, dgate = post_bwd(dxn, x, out, mod3, lng, w_out, 1.0, "ml_outb")
    dq, dk, dv, do, dg, dgain, dgb, *received = mlstm_cell_bwd(qk3, proj3, gain, gbias, dy.reshape(Bl, S, D),
                                                               states, "ml_cellb", exchange=exchange)
    dqk, dconv = conv_silu_bwd(proj3, conv_w, dq, dk, "ml_convb")
    dproj = jnp.concatenate([dqk, dv, do, dg.astype(BF16)], axis=2).reshape(Bl * S, M_PROJ_PAD)
    dx, dsh, dsc = modmm_bwd(dproj, w_in, x, mod3, dxres, "ml_inb", tn=M_PROJ_PAD // 5)
    dwi, _ = mm_tn(h, dproj, "ml_dwin", bw=M_PROJ_PAD // 5)
    dwi = _restack(jnp.moveaxis(dwi, 0, 1).reshape(D, M_PROJ_PAD)[:, :M_PROJ], 1)
    dw_out = _as_slots(mm_tn(y, dout, "ml_dwout", bw=D), (N_DEV, D // N_DEV, D))
    small = (jnp.sum(dconv, axis=0), jnp.sum(dgain, axis=0), jnp.sum(dgb, axis=0)[:, :2 * HEADS])
    dmod3 = jnp.concatenate([dsh, dsc, dgate], axis=1)
    return dx, [(dwi, dwi.astype(BF16)), dw_out], dlg, dlb, dmod3, small, received


def attn_mixer_fwd(x, mod3, w_in, w_out, lng, lnb, Bl, S):
    proj, h = modmm(x, mod3, w_in, BF16, "at_in")
    os_, lses = [], []
    for g, (_, dil) in enumerate(DIL_GROUPS):
        o_g, l_g = attn_fwd(proj, Bl, S, g, dil, "at_core%d" % g)
        os_.append(o_g)
        lses.append(l_g)
    ob, of, lt = attn_merge(os_, lses, "at_merge")
    out, xn = proj_post(ob[None], w_out, x, mod3, lng, lnb, 1.0, "at_out")
    return xn, (x, out, h, proj, ob, of, lt)


def attn_mixer_bwd(dxn, saved, mod3, w_in, w_out, lng, Bl, S):
    x, out, h, proj, ob, of, lt = saved
    dxres, dout, do, dlg, dlb, dgate = post_bwd(dxn, x, out, mod3, lng, w_out, 1.0, "at_outb")
    do = do[0]
    parts = []
    for g, (_, dil) in enumerate(DIL_GROUPS):
        parts += attn_bwd(proj, do, of, lt, Bl, S, g, dil, "at_coreb%d" % g)
    dproj = jnp.concatenate(parts, axis=1)
    dx, dsh, dsc = modmm_bwd(dproj, w_in, x, mod3, dxres, "at_inb")
    dw_in = mm_tn(h, dproj, "at_dwin", bw=w_in.shape[2])
    dw_out = _as_slots(mm_tn(ob, dout, "at_dwout", bw=D), (N_DEV, D // N_DEV, D))
    return dx, [dw_in, dw_out], dlg, dlb, jnp.concatenate([dsh, dsc, dgate], axis=1)


def _unstack(stacked, axis):
    full = jnp.moveaxis(stacked, 0, axis)
    shp = list(full.shape)
    shp[axis:axis + 2] = [shp[axis] * shp[axis + 1]]
    return full.reshape(shp)


def _restack(full, axis):
    shp = list(full.shape)
    shp[axis:axis + 1] = [N_DEV, shp[axis] // N_DEV]
    return jnp.moveaxis(full.reshape(shp), axis, 0)


def kernel(x, c, ada_w, ada_b, ln_g, ln_b, ffn_w_in, ffn_w_out, mlstm_w_in, mlstm_gate_bias, mlstm_conv_w, mlstm_head_gain, mlstm_w_out, attn_w_in, attn_w_out, loss_target, m_ada_w, m_ada_b, m_ln_g, m_ln_b, m_ffn_w_in, m_ffn_w_out, m_mlstm_w_in, m_mlstm_gate_bias, m_mlstm_conv_w, m_mlstm_head_gain, m_mlstm_w_out, m_attn_w_in, m_attn_w_out, v_ada_w, v_ada_b, v_ln_g, v_ln_b, v_ffn_w_in, v_ffn_w_out, v_mlstm_w_in, v_mlstm_gate_bias, v_mlstm_conv_w, v_mlstm_head_gain, v_mlstm_w_out, v_attn_w_in, v_attn_w_out):
    Bl, S, _ = x.shape
    T = Bl * S
    Bg = Bl * N_DEV
    me = 4 * lax.axis_index("x") + 2 * lax.axis_index("y") + lax.axis_index("c")
    onehot = (jnp.arange(N_DEV) == me).astype(F32)
    weights = dict(ada_w=ada_w, ada_b=ada_b, ln_g=ln_g, ln_b=ln_b, ffn_w_in=ffn_w_in, ffn_w_out=ffn_w_out,
                   mlstm_w_in=mlstm_w_in, mlstm_gate_bias=mlstm_gate_bias, mlstm_conv_w=mlstm_conv_w,
                   mlstm_head_gain=mlstm_head_gain, mlstm_w_out=mlstm_w_out, attn_w_in=attn_w_in,
                   attn_w_out=attn_w_out)
    m_in = dict(ada_w=m_ada_w, ada_b=m_ada_b, ln_g=m_ln_g, ln_b=m_ln_b, ffn_w_in=m_ffn_w_in,
                ffn_w_out=m_ffn_w_out, mlstm_w_in=m_mlstm_w_in, mlstm_gate_bias=m_mlstm_gate_bias,
                mlstm_conv_w=m_mlstm_conv_w, mlstm_head_gain=m_mlstm_head_gain, mlstm_w_out=m_mlstm_w_out,
                attn_w_in=m_attn_w_in, attn_w_out=m_attn_w_out)
    v_in = dict(ada_w=v_ada_w, ada_b=v_ada_b, ln_g=v_ln_g, ln_b=v_ln_b, ffn_w_in=v_ffn_w_in,
                ffn_w_out=v_ffn_w_out, mlstm_w_in=v_mlstm_w_in, mlstm_gate_bias=v_mlstm_gate_bias,
                mlstm_conv_w=v_mlstm_conv_w, mlstm_head_gain=v_mlstm_head_gain, mlstm_w_out=v_mlstm_w_out,
                attn_w_in=v_attn_w_in, attn_w_out=v_attn_w_out)

    mixer = ("mlstm", "attn")
    shards = [[ffn_w_in[layer, 0], ffn_w_in[layer, 1], ffn_w_out[layer, 0], ffn_w_out[layer, 1],
               weights[mixer[layer] + "_w_in"][0], weights[mixer[layer] + "_w_out"][0]] for layer in range(DEPTH)]
    sends = [[s.astype(BF16) for s in layer_shards] for layer_shards in shards]
    small = jnp.concatenate([c.reshape(-1), ln_g.reshape(-1), ln_b.reshape(-1), mlstm_conv_w.reshape(-1)])
    n_small = small.shape[0]
    small = jnp.pad(small, (0, -n_small % (8 * PACK_COLS))).reshape(-1, PACK_COLS)

    def gathered_weights(g):
        return ((g[0], g[1]), (g[2].reshape(4, D_FF // 4, D), g[3].reshape(4, D_FF // 4, D)), g[4],
                g[5].reshape(1, D, D))

    first_in, small_all = all_gather([sends[0][0], small], "ag_params")
    full = [None, None]
    small_flat = small_all.reshape(N_DEV, -1)
    o0 = 0
    c_all = small_flat[:, o0:o0 + c.size].reshape(Bg, D)
    o0 += c.size
    lng_full = _unstack(small_flat[:, o0:o0 + ln_g.size].reshape((N_DEV,) + ln_g.shape), 2)
    o0 += ln_g.size
    lnb_full = _unstack(small_flat[:, o0:o0 + ln_b.size].reshape((N_DEV,) + ln_b.shape), 2)
    o0 += ln_b.size
    conv_full = _unstack(small_flat[:, o0:o0 + mlstm_conv_w.size].reshape((N_DEV,) + mlstm_conv_w.shape), 2)[0]
    gbias =jnp.pad(mlstm_gate_bias, ((0, 0), (0, 128 - 2 * HEADS)))

    ncols = ada_w.shape[2]
    ada_b_cols = lax.dynamic_slice_in_dim(ada_b, me * ncols, ncols, axis=1).reshape(DEPTH, 1, ncols)
    mod_cols = ada_fwd(c_all, ada_w, ada_b_cols, "ada_fwd")
    (mod_g,) = all_gather([mod_cols.reshape(DEPTH * Bg, ncols)], "ag_mod")
    mod_full = _unstack(mod_g.reshape(N_DEV, DEPTH, Bg, ncols), 2)
    mod_mine = lax.dynamic_slice_in_dim(mod_full, me * Bl, Bl, axis=1).reshape(DEPTH, Bl, 3, 3, D)

    xt = x.reshape(T, D)
    saved = []
    for layer in range(DEPTH):
        def lnp(s, layer=layer):
            return lng_full[layer, s].reshape(1, D), lnb_full[layer, s].reshape(1, D)
        md = mod_mine[layer]
        if layer == 0:
            a, g, u, h, late = ffn_in(xt, md[:, 0], first_in, "f0a_in", gather=sends[0][1:])
            full[0] = gathered_weights([first_in] + late)
            out, xn = proj_post(a, full[0][1][0], xt, md[:, 0], *lnp(0), 0.5, "f0a_out")
            xt, sv0 = xn, (xt, out, g, u, h, a)
            mw_in = jnp.pad(_unstack(full[0][2], 1), ((0, 0), (0, M_PROJ_PAD - M_PROJ)))
        else:
            xt, sv0, _ = ffn_fwd(xt, md[:, 0], full[layer][0][0], full[layer][1][0], *lnp(0), "f%da" % layer)
        f_in, f_out, mix_in, mix_out = full[layer]
        if layer % 2 == 0:
            xt, sv1, g1 = mlstm_fwd(xt, md[:, 1], mw_in, mix_out, conv_full, mlstm_head_gain, gbias, *lnp(1), Bl, S,
                                    gather=sends[1])
            full[1] = gathered_weights(g1)
        else:
            xt, sv1 = attn_mixer_fwd(xt, md[:, 1], mix_in, mix_out, *lnp(1), Bl, S)
        xt, sv2, _ = ffn_fwd(xt, md[:, 2], f_in[1], f_out[1], *lnp(2), "f%db" % layer)
        saved.append((sv0, sv1, sv2))

    dxt, lsum = loss_head(xt, loss_target.reshape(T, D), "loss")
    loss = lax.psum(lsum[0, 0], MESH_AXES)

    dmod, dlg_all, dlb_all = [None] * DEPTH, [None] * DEPTH, [None] * DEPTH
    wgrads = [None] * DEPTH
    recvs = [[None] * 6 for _ in range(DEPTH)]
    ml_small = None
    for layer in reversed(range(DEPTH)):
        md = mod_mine[layer]
        f_in, f_out, mix_in, mix_out = full[layer]
        sv0, sv1, sv2 = saved[layer]
        dxt, dw2, dlg2, dlb2, dm2, _, _ = ffn_bwd(dxt, sv2, md[:, 2], f_in[1], f_out[1],
                                                  lng_full[layer, 2].reshape(1, D), "f%db" % layer)
        lg1 = lng_full[layer, 1].reshape(1, D)
        if layer % 2 == 0:
            dxt, dw1, dlg1, dlb1, dm1, ml_small, got = mlstm_bwd(
                dxt, sv1, md[:, 1], mw_in, mix_out, conv_full, mlstm_head_gain, gbias, lg1, Bl, S,
                exchange=[b16 for _, b16 in wgrads[1]] + [dw2[0][1], dw2[1][1]])
            recvs[1], recvs[0][1], recvs[0][3] = got[:6], got[6], got[7]
            dxt, dw0, dlg0, dlb0, dm0, got, own = ffn_bwd(dxt, sv0, md[:, 0], f_in[0], f_out[0],
                                                          lng_full[layer, 0].reshape(1, D), "f%da" % layer,
                                                          exchange=[dw1[0][1], dw1[1][1]], exchange_own=True)
            (recvs[0][4], recvs[0][5]), (recvs[0][0], recvs[0][2]) = got, own
        else:
            dxt, dw1, dlg1, dlb1, dm1 = attn_mixer_bwd(dxt, sv1, md[:, 1], mix_in, mix_out, lg1, Bl, S)
            dxt, dw0, dlg0, dlb0, dm0, _, _ = ffn_bwd(dxt, sv0, md[:, 0], f_in[0], f_out[0],
                                                      lng_full[layer, 0].reshape(1, D), "f%da" % layer)
        wgrads[layer] = [dw0[0], dw2[0], dw0[1], dw2[1], dw1[0], dw1[1]]
        dmod[layer] = jnp.stack([dm0, dm1, dm2], axis=1).reshape(Bl, 9 * D)
        dlg_all[layer] = jnp.concatenate([dlg0, dlg1, dlg2], axis=0)
        dlb_all[layer] = jnp.concatenate([dlb0, dlb1, dlb2], axis=0)
    grad_x = dxt.reshape(Bl, S, D)

    gsh = [[shard_sum(lax.dynamic_index_in_dim(f32, me, axis=0, keepdims=False), recv, onehot,
                      "rs_sum%d_%d" % (layer, i))
            for i, ((f32, _), recv) in enumerate(zip(wgrads[layer], recvs[layer]))] for layer in range(DEPTH)]
    grads = {"ffn_w_in": jnp.stack([jnp.stack(g[0:2]) for g in gsh]),
             "ffn_w_out": jnp.stack([jnp.stack(g[2:4]) for g in gsh]),
             "mlstm_w_in": gsh[0][4][None], "mlstm_w_out": gsh[0][5][None],
             "attn_w_in": gsh[1][4][None], "attn_w_out": gsh[1][5][None]}

    dconv, dgain, dgbias = ml_small
    parts = [jnp.stack(dmod).reshape(-1), dgbias.reshape(-1), dgain.reshape(-1),
             jnp.stack(dlg_all).reshape(-1), jnp.stack(dlb_all).reshape(-1), dconv.reshape(-1)]
    sizes = [p.shape[0] for p in parts]
    flat = jnp.concatenate(parts)
    flat = jnp.pad(flat, (0, -flat.shape[0] % (8 * PACK_COLS))).reshape(-1, PACK_COLS)
    (sm_all,) = all_gather([flat], "ag_small")
    sm_sum = sum_leading(sm_all, "small_sum").reshape(-1)
    dmod_all = sm_all.reshape(N_DEV, -1)[:, :sizes[0]].reshape(N_DEV, DEPTH, Bl, 9 * D)
    dmod_all = jnp.moveaxis(dmod_all, 0, 1).reshape(DEPTH, Bg, 9 * D)
    o0 = sizes[0]
    grads["mlstm_gate_bias"] = sm_sum[o0:o0 + sizes[1]].reshape(mlstm_gate_bias.shape)
    o0 += sizes[1]
    grads["mlstm_head_gain"] = sm_sum[o0:o0 + sizes[2]].reshape(mlstm_head_gain.shape)
    o0 += sizes[2]
    nl = ln_g.shape[2]
    g_lng = sm_sum[o0:o0 + sizes[3]].reshape(DEPTH, 3, D)
    o0 += sizes[3]
    g_lnb = sm_sum[o0:o0 + sizes[4]].reshape(DEPTH, 3, D)
    o0 += sizes[4]
    g_conv = sm_sum[o0:o0 + sizes[5]].reshape(1, 4, D)
    grads["ln_g"] = lax.dynamic_slice_in_dim(g_lng, me * nl, nl, axis=2)
    grads["ln_b"] = lax.dynamic_slice_in_dim(g_lnb, me * nl, nl, axis=2)
    grads["mlstm_conv_w"] = lax.dynamic_slice_in_dim(g_conv, me * nl, nl, axis=2)
    dmod_cols = lax.dynamic_slice_in_dim(dmod_all, me * ncols, ncols, axis=2)
    gw, gb = ada_bwd(c_all.T, dmod_cols, dmod_all, "ada_bwd")
    grads["ada_w"] = gw
    grads["ada_b"] = gb.reshape(ada_b.shape)

    names = ["ada_w", "ada_b", "ln_g", "ln_b", "ffn_w_in", "ffn_w_out", "mlstm_w_in", "mlstm_gate_bias",
             "mlstm_conv_w", "mlstm_head_gain", "mlstm_w_out", "attn_w_in", "attn_w_out"]
    deltas, new_m, new_v = [], [], []
    for k in names:
        w = weights[k]
        shp2 = (math.prod(w.shape[:-1]), w.shape[-1])
        d_, m_, v_ = adamw(w.reshape(shp2), grads[k].reshape(shp2), m_in[k].reshape(shp2), v_in[k].reshape(shp2),
                           "adamw_" + k)
        deltas.append(d_.reshape(w.shape))
        new_m.append(m_.reshape(w.shape))
        new_v.append(v_.reshape(w.shape))
    return (loss, grad_x, *[grads[k] for k in names], *deltas, *new_m, *new_v)
```

```python
import functools
import math

import jax
import jax.numpy as jnp
from jax import lax
from jax.experimental import pallas as pl
from jax.experimental.pallas import tpu as pltpu

F32 = jnp.float32
BF16 = jnp.bfloat16

N_DEV = 8
MESH_AXES = ("x", "y", "c")
D = 1024
DEPTH = 2
D_FF = 2816
HEADS = 8
M_DQK = 64
M_DV = 128
M_CHUNK = 64
M_SLAB = 512
M_PROJ = 3088
M_PROJ_PAD = 3200
A_PROJ = 9216
DIL_GROUPS = ((128, 1), (512, 4), (2048, 16))
A_BLK = 128
A_UNROLL = 4
ALPHA = (2 * DEPTH) ** 0.25
LN_EPS = 1e-5
RMS_EPS = 1e-6
ADAM_LR = 0.001
ADAM_B1 = 0.9
ADAM_B2 = 0.999
ADAM_EPS = 1e-08
ADAM_WD = 0.01
ADAM_STEP = 10
NEG = -1e30
V7X_VMEM_LIMIT = 56 * 1024 * 1024
PACK_COLS = 1024
MESH_ID = pl.DeviceIdType.MESH
ANY_SPEC = pl.BlockSpec(memory_space=pl.ANY)


def _cp(n_axes):
    return pltpu.CompilerParams(dimension_semantics=("arbitrary",) * n_axes,
                                vmem_limit_bytes=V7X_VMEM_LIMIT)


def _dot(a, b):
    return jnp.dot(a, b, preferred_element_type=F32)


def _dot_nt(a, b):
    return lax.dot_general(a, b, (((1,), (1,)), ((), ())), preferred_element_type=F32)


def _dot_tn(a, b):
    return lax.dot_general(a, b, (((0,), (0,)), ((), ())), preferred_element_type=F32)


def _sum0(a):
    return jnp.sum(a, axis=0, keepdims=True)


def _sum1(a):
    return jnp.sum(a, axis=1, keepdims=True)


def _round(a):
    return a.astype(BF16).astype(F32)


def _sigmoid(a):
    return 1.0 / (1.0 + jnp.exp(-a))


def _tile(n, pref):
    t = min(n, pref)
    while n % t:
        t //= 2
    return t


def all_gather(arrs, name):
    n = len(arrs)

    def body(*refs):
        gather = Gather(refs[:n], refs[n:2 * n], *refs[2 * n:])
        gather.start()
        gather.finish()

    return pl.pallas_call(
        body, name=name, out_shape=Gather.out_shape(arrs),
        in_specs=[ANY_SPEC] * n, out_specs=[ANY_SPEC] * n, scratch_shapes=Gather.scratch(n),
    )(*arrs)


class Gather:
    def __init__(self, ins, outs, send_sems, recv_sems, local_sems):
        x, y, c = lax.axis_index("x"), lax.axis_index("y"), lax.axis_index("c")
        me, sibling = (x, y, c), (x, y, 1 - c)
        chips = [(1 - x, y), (x, 1 - y), (1 - x, 1 - y)]

        def slot(a, p):
            return outs[a].at[4 * p[0] + 2 * p[1] + p[2]]

        def copy(a, k, block, to, src=None):
            return pltpu.make_async_remote_copy(
                src_ref=slot(a, block) if src is None else src, dst_ref=slot(a, block),
                send_sem=send_sems.at[7 * a + k], recv_sem=recv_sems.at[7 * a + k],
                device_id=to, device_id_type=MESH_ID)

        n = len(ins)
        self.mine = [pltpu.make_async_copy(ins[a], slot(a, me), local_sems.at[a]) for a in range(n)]
        self.first, self.over_ici, self.passed, self.from_sibling = [], [], [], []
        for a in range(n):
            self.first.append(copy(a, 0, me, sibling, src=ins[a]))
            self.from_sibling.append(copy(a, 0, sibling, me))
            for j, chip in enumerate(chips):
                self.first.append(copy(a, 1 + j, me, (*chip, c), src=ins[a]))
                self.over_ici.append(copy(a, 1 + j, (*chip, c), me))
                self.passed.append(copy(a, 4 + j, (*chip, c), sibling))
                self.from_sibling.append(copy(a, 4 + j, (*chip, 1 - c), me))

    @staticmethod
    def out_shape(arrs):
        return [jax.ShapeDtypeStruct((N_DEV,) + a.shape, a.dtype) for a in arrs]

    @staticmethod
    def scratch(n):
        return [pltpu.SemaphoreType.DMA((7 * n,)), pltpu.SemaphoreType.DMA((7 * n,)),
                pltpu.SemaphoreType.DMA((n,))]

    def start(self):
        for cp in self.mine + self.first:
            cp.start()

    def finish(self):
        for landed, onward in zip(self.over_ici, self.passed):
            landed.wait_recv()
            onward.start()
        for cp in self.from_sibling:
            cp.wait_recv()
        for cp in self.first + self.passed:
            cp.wait_send()
        for cp in self.mine:
            cp.wait()


class Exchange:
    def __init__(self, sends, recvs, send_sems, recv_sems, local_sems):
        x, y, c = lax.axis_index("x"), lax.axis_index("y"), lax.axis_index("c")
        me = 4 * x + 2 * y + c
        self.own = [pltpu.make_async_copy(s.at[me], r.at[me], local_sems.at[a])
                    for a, (s, r) in enumerate(zip(sends, recvs))]
        self.copies = []
        for a, (s_ref, r_ref) in enumerate(zip(sends, recvs)):
            for k in range(1, N_DEV):
                px = 1 - x if (k >> 2) & 1 else x
                py = 1 - y if (k >> 1) & 1 else y
                pc = 1 - c if k & 1 else c
                self.copies.append(pltpu.make_async_remote_copy(
                    src_ref=s_ref.at[4 * px + 2 * py + pc], dst_ref=r_ref.at[me],
                    send_sem=send_sems.at[7 * a + k - 1], recv_sem=recv_sems.at[7 * a + k - 1],
                    device_id=(px, py, pc), device_id_type=MESH_ID))

    @staticmethod
    def scratch(n):
        return [pltpu.SemaphoreType.DMA((7 * n,)), pltpu.SemaphoreType.DMA((7 * n,)),
                pltpu.SemaphoreType.DMA((n,))]

    def start(self):
        for cp in self.own + self.copies:
            cp.start()

    def finish(self):
        for cp in self.copies:
            cp.wait_send()
            cp.wait_recv()
        for cp in self.own:
            cp.wait()


def host_comm(body, grid, n_in, n_out, gather=(), exchange=()):
    ng, nx = len(gather), len(exchange)
    if ng + nx == 0:
        return body, [], [], [], []

    def hosted(*refs):
        ins, c_in, rest = refs[:n_in], refs[n_in:n_in + ng + nx], refs[n_in + ng + nx:]
        outs, c_out, rest = rest[:n_out], rest[n_out:n_out + ng + nx], rest[n_out + ng + nx:]
        n_sems = 3 * ((ng > 0) + (nx > 0))
        scratch, sems = rest[:len(rest) - n_sems], rest[len(rest) - n_sems:]

        def comms():
            made = [Gather(c_in[:ng], c_out[:ng], *sems[:3])] if ng else []
            return made + ([Exchange(c_in[ng:], c_out[ng:], *sems[-3:])] if nx else [])

        ids = [pl.program_id(a) for a in range(len(grid))]

        @pl.when(functools.reduce(jnp.logical_and, [i == 0 for i in ids]))
        def _():
            for cm in comms():
                cm.start()
        body(*ins, *outs, *scratch)

        @pl.when(functools.reduce(jnp.logical_and, [i == g - 1 for i, g in zip(ids, grid)]))
        def _():
            for cm in comms():
                cm.finish()

    shapes = Gather.out_shape(gather) + [jax.ShapeDtypeStruct(a.shape, a.dtype) for a in exchange]
    scratch = (Gather.scratch(ng) if ng else []) + (Exchange.scratch(nx) if nx else [])
    return hosted, [ANY_SPEC] * (ng + nx), [ANY_SPEC] * (ng + nx), shapes, scratch


def shard_sum(own, recv, onehot, name):
    R, C = own.shape
    tr = _tile(R, 512)

    def body(oh_ref, own_ref, recv_ref, o_ref):
        acc = None
        for j in range(N_DEV):
            term = jnp.where(oh_ref[j] > 0.5, own_ref[...], recv_ref[j].astype(F32))
            acc = term if acc is None else acc + term
        o_ref[...] = acc

    return pl.pallas_call(
        body, name=name, grid=(R // tr,),
        in_specs=[pl.BlockSpec(memory_space=pltpu.SMEM),
                  pl.BlockSpec((tr, C), lambda i: (i, 0)),
                  pl.BlockSpec((N_DEV, tr, C), lambda i: (0, i, 0))],
        out_specs=pl.BlockSpec((tr, C), lambda i: (i, 0)),
        out_shape=jax.ShapeDtypeStruct((R, C), F32), compiler_params=_cp(1),
    )(onehot, own, recv)


def sum_leading(a, name):
    _, R, C = a.shape

    def body(a_ref, o_ref):
        acc = a_ref[0]
        for j in range(1, N_DEV):
            acc = acc + a_ref[j]
        o_ref[...] = acc

    return pl.pallas_call(body, name=name, out_shape=jax.ShapeDtypeStruct((R, C), F32),
                          compiler_params=_cp(0))(a)


def _col_chunks(w, tn):
    if w.ndim == 3:
        return w.shape[0], w.shape[2], pl.BlockSpec((None, w.shape[1], w.shape[2]), lambda i, j: (j, 0, 0))
    return w.shape[1] // tn, tn, pl.BlockSpec((w.shape[0], tn), lambda i, j: (0, j))


def modmm(x, mod3, w, out_dtype, name, tn=None):
    T, Dm = x.shape
    nj, tn, w_spec = _col_chunks(w, tn)
    N = nj * tn
    Bl = mod3.shape[0]
    tm = _tile(T // Bl, 1024)
    tpb = T // Bl // tm

    def body(x_ref, mod_ref, w_ref, o_ref, h_ref, hs):
        @pl.when(pl.program_id(1) == 0)
        def _():
            m = mod_ref[0]
            hs[...] = (x_ref[...] * (1.0 + m[1:2, :]) + m[0:1, :]).astype(BF16)
            h_ref[...] = hs[...]
        o_ref[...] = _dot(hs[...], w_ref[...]).astype(o_ref.dtype)

    return pl.pallas_call(
        body, name=name, grid=(T // tm, nj),
        in_specs=[pl.BlockSpec((tm, Dm), lambda i, j: (i, 0)),
                  pl.BlockSpec((1, 3, Dm), lambda i, j: (i // tpb, 0, 0)), w_spec],
        out_specs=[pl.BlockSpec((tm, tn), lambda i, j: (i, j)),
                   pl.BlockSpec((tm, Dm), lambda i, j: (i, 0))],
        out_shape=[jax.ShapeDtypeStruct((T, N), out_dtype), jax.ShapeDtypeStruct((T, Dm), BF16)],
        scratch_shapes=[pltpu.VMEM((tm, Dm), BF16)], compiler_params=_cp(2),
    )(x, mod3, w)


def modmm_bwd(dp, w, x, mod3, dxres, name, tn=None, tm=512, exchange=()):
    T, Dm = x.shape
    Bl = mod3.shape[0]
    tm = _tile(T // Bl, tm)
    tpb = T // Bl // tm
    resident = dp.ndim == 3
    if resident:
        nc, nj = dp.shape[0], 1
        dp_spec = pl.BlockSpec((nc, tm, dp.shape[2]), lambda i, j: (0, i, 0))
        w_spec = pl.BlockSpec(w.shape, lambda i, j: (0, 0, 0))
    else:
        nj, tn, w_spec = _col_chunks(w, tn)
        dp_spec = pl.BlockSpec((tm, tn), lambda i, j: (i, j))

    def body(dp_ref, w_ref, x_ref, mod_ref, dxr_ref, dx_ref, dsh_ref, dsc_ref, acc):
        i, j = pl.program_id(0), pl.program_id(1)

        @pl.when(j == 0)
        def _():
            acc[...] = jnp.zeros_like(acc)
        if resident:
            for c in range(nc):
                acc[...] += _dot_nt(dp_ref[c], w_ref[c])
        else:
            acc[...] += _dot_nt(dp_ref[...], w_ref[...])

        @pl.when(j == nj - 1)
        def _():
            dh = acc[...]
            xx = x_ref[...]
            dx_ref[...] = dxr_ref[...] + dh * (1.0 + mod_ref[0][1:2, :])

            @pl.when(i % tpb == 0)
            def _():
                dsh_ref[...] = jnp.zeros_like(dsh_ref)
                dsc_ref[...] = jnp.zeros_like(dsc_ref)
            dsh_ref[0] += _sum0(dh)
            dsc_ref[0] += _sum0(dh * xx)

    grid = (T // tm, nj)
    body, c_in, c_out, c_shape, c_scratch = host_comm(body, grid, 5, 3, exchange=exchange)
    dx, dsh, dsc, *received = pl.pallas_call(
        body, name=name, grid=grid,
        in_specs=[dp_spec, w_spec,
                  pl.BlockSpec((tm, Dm), lambda i, j: (i, 0)),
                  pl.BlockSpec((1, 3, Dm), lambda i, j: (i // tpb, 0, 0)),
                  pl.BlockSpec((tm, Dm), lambda i, j: (i, 0))] + c_in,
        out_specs=[pl.BlockSpec((tm, Dm), lambda i, j: (i, 0)),
                   pl.BlockSpec((1, 1, Dm), lambda i, j: (i // tpb, 0, 0)),
                   pl.BlockSpec((1, 1, Dm), lambda i, j: (i // tpb, 0, 0))] + c_out,
        out_shape=[jax.ShapeDtypeStruct((T, Dm), F32), jax.ShapeDtypeStruct((Bl, 1, Dm), F32),
                   jax.ShapeDtypeStruct((Bl, 1, Dm), F32)] + c_shape,
        scratch_shapes=[pltpu.VMEM((tm, Dm), F32)] + c_scratch, compiler_params=_cp(2),
    )(dp, w, x, mod3, dxres, *exchange)
    return (dx, dsh, dsc, received) if exchange else (dx, dsh, dsc)


def _ln_stats(z):
    mu = jnp.mean(z, axis=-1, keepdims=True)
    zc = z - mu
    var = jnp.mean(zc * zc, axis=-1, keepdims=True)
    rstd = lax.rsqrt(var + LN_EPS)
    return zc * rstd, rstd


def proj_post(a, w, x, mod3, lng, lnb, weight, name):
    nk, T, tk = a.shape
    Dm = w.shape[2]
    Bl = mod3.shape[0]
    tm = _tile(T // Bl, 512)
    tpb = T // Bl // tm

    def body(a_ref, w_ref, x_ref, mod_ref, g_ref, b_ref, out_ref, xn_ref):
        out = _dot(a_ref[0], w_ref[0])
        for k in range(1, nk):
            out = out + _dot(a_ref[k], w_ref[k])
        out_ref[...] = out
        z = ALPHA * x_ref[...] + (weight * (1.0 + mod_ref[0][2:3, :])) * out
        xhat, _ = _ln_stats(z)
        xn_ref[...] = xhat * g_ref[...] + b_ref[...]

    row = pl.BlockSpec((tm, Dm), lambda i: (i, 0))
    vec = pl.BlockSpec((1, Dm), lambda i: (0, 0))
    return pl.pallas_call(
        body, name=name, grid=(T // tm,),
        in_specs=[pl.BlockSpec((nk, tm, tk), lambda i: (0, i, 0)),
                  pl.BlockSpec((nk, tk, Dm), lambda i: (0, 0, 0)),
                  row, pl.BlockSpec((1, 3, Dm), lambda i: (i // tpb, 0, 0)), vec, vec],
        out_specs=[row, row],
        out_shape=[jax.ShapeDtypeStruct((T, Dm), F32), jax.ShapeDtypeStruct((T, Dm), F32)],
        compiler_params=_cp(1),
    )(a, w, x, mod3, lng, lnb)


def post_bwd(dxn, x, out, mod3, lng, w, weight, name, tm=512, gu=None, exchange=()):
    T, Dm = x.shape
    nk, tk, _ = w.shape
    Bl = mod3.shape[0]
    tm = _tile(T // Bl, tm)
    tpb = T // Bl // tm
    fused = gu is not None

    def body(dxn_ref, x_ref, out_ref, mod_ref, g_ref, w_ref, *rest):
        if fused:
            gg_ref, uu_ref = rest[:2]
            rest = rest[2:]
        dxr_ref, dout_ref, da_ref, dg_ref, db_ref, dgate_ref = rest
        i = pl.program_id(0)
        out = out_ref[...]
        dxn = dxn_ref[...]
        coef = weight * (1.0 + mod_ref[0][2:3, :])
        xhat, rstd = _ln_stats(ALPHA * x_ref[...] + coef * out)
        dyh = dxn * g_ref[...]
        dz = rstd * (dyh - jnp.mean(dyh, axis=-1, keepdims=True)
                     - xhat * jnp.mean(dyh * xhat, axis=-1, keepdims=True))
        dxr_ref[...] = ALPHA * dz
        dout = (coef * dz).astype(BF16)
        dout_ref[...] = dout

        @pl.when(i == 0)
        def _():
            dg_ref[...] = jnp.zeros_like(dg_ref)
            db_ref[...] = jnp.zeros_like(db_ref)

        @pl.when(i % tpb == 0)
        def _():
            dgate_ref[...] = jnp.zeros_like(dgate_ref)
        dg_ref[...] += _sum0(dxn * xhat)
        db_ref[...] += _sum0(dxn)
        dgate_ref[0] += _sum0((weight * out) * dz)
        for k in range(nk):
            da = _dot_nt(dout, w_ref[k])
            if fused:
                gg = gg_ref[k].astype(F32)
                s = _sigmoid(gg)
                da_ref[k] = (da * uu_ref[k].astype(F32) * (s * (1.0 + gg * (1.0 - s)))).astype(BF16)
                da_ref[nk + k] = (da * (gg * s)).astype(BF16)
            else:
                da_ref[k] = da.astype(BF16)

    row = pl.BlockSpec((tm, Dm), lambda i: (i, 0))
    vec = pl.BlockSpec((1, Dm), lambda i: (0, 0))
    wide = pl.BlockSpec((nk, tm, tk), lambda i: (0, i, 0))
    nda = 2 * nk if fused else nk
    grid = (T // tm,)
    body, c_in, c_out, c_shape, c_scratch = host_comm(body, grid, 8 if fused else 6, 6, exchange=exchange)
    *results, = pl.pallas_call(
        body, name=name, grid=grid,
        in_specs=[row, row, row, pl.BlockSpec((1, 3, Dm), lambda i: (i // tpb, 0, 0)), vec,
                  pl.BlockSpec((nk, tk, Dm), lambda i: (0, 0, 0))] + ([wide, wide] if fused else []) + c_in,
        out_specs=[row, row, pl.BlockSpec((nda, tm, tk), lambda i: (0, i, 0)),
                   vec, vec, pl.BlockSpec((1, 1, Dm), lambda i: (i // tpb, 0, 0))] + c_out,
        out_shape=[jax.ShapeDtypeStruct((T, Dm), F32), jax.ShapeDtypeStruct((T, Dm), BF16),
                   jax.ShapeDtypeStruct((nda, T, tk), BF16), jax.ShapeDtypeStruct((1, Dm), F32),
                   jax.ShapeDtypeStruct((1, Dm), F32), jax.ShapeDtypeStruct((Bl, 1, Dm), F32)] + c_shape,
        scratch_shapes=c_scratch, compiler_params=_cp(1),
    )(dxn, x, out, mod3, lng, w, *(gu if fused else ()), *exchange)
    return tuple(results[:6]) + ((results[6:],) if exchange else ())


def mm_tn(a, b, name, bw=None, a_copies=False):
    a3, b3 = a.ndim == 3, b.ndim == 3
    nk, T, tk = a.shape if a3 else (1,) + a.shape
    if a_copies:
        nk = 1
    nc, wn = (b.shape[0], b.shape[2]) if b3 else (b.shape[1] // bw, bw)
    tt = _tile(T, 2048)
    nt = T // tt

    def body(a_ref, b_ref, o_ref, ob_ref):
        t = pl.program_id(2)

        @pl.when(t == 0)
        def _():
            o_ref[...] = jnp.zeros_like(o_ref)
        o_ref[...] += _dot_tn(a_ref[...], b_ref[...])

        @pl.when(t == nt - 1)
        def _():
            ob_ref[...] = o_ref[...].astype(BF16)

    a_spec = (pl.BlockSpec((None, tt, tk), lambda k, c, t: (k, t, 0)) if a3
              else pl.BlockSpec((tt, tk), lambda k, c, t: (t, 0)))
    b_spec = (pl.BlockSpec((None, tt, wn), lambda k, c, t: (c, t, 0)) if b3
              else pl.BlockSpec((tt, wn), lambda k, c, t: (t, c)))
    o_spec = pl.BlockSpec((None, tk, wn), lambda k, c, t: (k * nc + c, 0, 0))
    return pl.pallas_call(
        body, name=name, grid=(nk, nc, nt), in_specs=[a_spec, b_spec], out_specs=[o_spec, o_spec],
        out_shape=[jax.ShapeDtypeStruct((nk * nc, tk, wn), F32), jax.ShapeDtypeStruct((nk * nc, tk, wn), BF16)],
        compiler_params=_cp(3),
    )(a, b)


def ffn_in(x, mod3, w, name, gather=()):
    T, Dm = x.shape
    nj, tf = w.shape[0] // 2, w.shape[2]
    Bl = mod3.shape[0]
    tm = _tile(T // Bl, 1024)
    tpb = T // Bl // tm

    def body(x_ref, mod_ref, wg_ref, wu_ref, a_ref, g_ref, u_ref, h_ref):
        m = mod_ref[0]
        h = (x_ref[...] * (1.0 + m[1:2, :]) + m[0:1, :]).astype(BF16)
        h_ref[...] = h
        g = _dot(h, wg_ref[...])
        u = _dot(h, wu_ref[...])
        a_ref[...] = (g * _sigmoid(g) * u).astype(BF16)
        g_ref[...] = g.astype(BF16)
        u_ref[...] = u.astype(BF16)

    col = pl.BlockSpec((None, tm, tf), lambda j, i: (j, i, 0))
    grid = (nj, T // tm)
    body, c_in, c_out, c_shape, c_scratch = host_comm(body, grid, 4, 4, gather=gather)
    a, g, u, h, *gathered = pl.pallas_call(
        body, name=name, grid=grid,
        in_specs=[pl.BlockSpec((tm, Dm), lambda j, i: (i, 0)),
                  pl.BlockSpec((1, 3, Dm), lambda j, i: (i // tpb, 0, 0)),
                  pl.BlockSpec((None, Dm, tf), lambda j, i: (j, 0, 0)),
                  pl.BlockSpec((None, Dm, tf), lambda j, i: (nj + j, 0, 0))] + c_in,
        out_specs=[col, col, col, pl.BlockSpec((None, tm, Dm), lambda j, i: (j, i, 0))] + c_out,
        out_shape=[jax.ShapeDtypeStruct((nj, T, tf), BF16)] * 3 + [jax.ShapeDtypeStruct((nj, T, Dm), BF16)]
        + c_shape,
        scratch_shapes=c_scratch, compiler_params=_cp(2),
    )(x, mod3, w, w, *gather)
    return a, g, u, h, gathered


def loss_head(y, tgt, name):
    T, Dm = y.shape
    tm = _tile(T, 512)
    nt = T // tm

    def body(y_ref, t_ref, dy_ref, l_ref, acc):
        i = pl.program_id(0)

        @pl.when(i == 0)
        def _():
            acc[...] = jnp.zeros_like(acc)
        e = y_ref[...] - t_ref[...]
        dy_ref[...] = e * (1.0 / Dm)
        acc[...] += _sum0(e * e)

        @pl.when(i == nt - 1)
        def _():
            l_ref[...] = jnp.broadcast_to(_sum1(acc[...]) * (0.5 / Dm), l_ref.shape)

    return pl.pallas_call(
        body, name=name, grid=(nt,),
        in_specs=[pl.BlockSpec((tm, Dm), lambda i: (i, 0)), pl.BlockSpec((tm, Dm), lambda i: (i, 0))],
        out_specs=[pl.BlockSpec((tm, Dm), lambda i: (i, 0)), pl.BlockSpec((1, 128), lambda i: (0, 0))],
        out_shape=[jax.ShapeDtypeStruct((T, Dm), F32), jax.ShapeDtypeStruct((1, 128), F32)],
        scratch_shapes=[pltpu.VMEM((1, Dm), F32)], compiler_params=_cp(1),
    )(y, tgt)


def adamw(w, g, m, v, name):
    R, C = w.shape
    tr = _tile(R, 512) if R % 8 == 0 else R

    def body(w_ref, g_ref, m_ref, v_ref, d_ref, nm_ref, nv_ref):
        gg = g_ref[...]
        mm = ADAM_B1 * m_ref[...] + (1.0 - ADAM_B1) * gg
        vv = ADAM_B2 * v_ref[...] + (1.0 - ADAM_B2) * (gg * gg)
        m_hat = mm / (1.0 - ADAM_B1 ** ADAM_STEP)
        v_hat = vv / (1.0 - ADAM_B2 ** ADAM_STEP)
        d_ref[...] = -ADAM_LR * (m_hat / (jnp.sqrt(v_hat) + ADAM_EPS) + ADAM_WD * w_ref[...])
        nm_ref[...] = mm
        nv_ref[...] = vv

    spec = pl.BlockSpec((tr, C), lambda i: (i, 0))
    return pl.pallas_call(
        body, name=name, grid=(R // tr,), in_specs=[spec] * 4, out_specs=[spec] * 3,
        out_shape=[jax.ShapeDtypeStruct((R, C), F32)] * 3, compiler_params=_cp(1),
    )(w, g, m, v)


def ada_fwd(c_all, ada_w, ada_b_cols, name):
    Lr, Dm, Nc = ada_w.shape
    Bg = c_all.shape[0]

    def body(c_ref, w_ref, b_ref, o_ref):
        cc = c_ref[...]
        cond = cc * _sigmoid(cc)
        o_ref[0] = _dot(cond.astype(BF16), w_ref[0].astype(BF16)) + b_ref[0]

    return pl.pallas_call(
        body, name=name, grid=(Lr,),
        in_specs=[pl.BlockSpec((Bg, Dm), lambda l: (0, 0)),
                  pl.BlockSpec((1, Dm, Nc), lambda l: (l, 0, 0)),
                  pl.BlockSpec((1, 1, Nc), lambda l: (l, 0, 0))],
        out_specs=pl.BlockSpec((1, Bg, Nc), lambda l: (l, 0, 0)),
        out_shape=jax.ShapeDtypeStruct((Lr, Bg, Nc), F32), compiler_params=_cp(1),
    )(c_all, ada_w, ada_b_cols)


def ada_bwd(c_all_t, dmod_cols, dmod_all, name):
    Dm, Bg = c_all_t.shape
    Lr, _, Nc = dmod_cols.shape
    Nf = dmod_all.shape[2]

    def body(c_ref, dm_ref, da_ref, gw_ref, gb_ref):
        cc = c_ref[...]
        cond = cc * _sigmoid(cc)
        gw_ref[0] = _dot(cond.astype(BF16), dm_ref[0].astype(BF16))
        gb_ref[0] = _sum0(da_ref[0])

    return pl.pallas_call(
        body, name=name, grid=(Lr,),
        in_specs=[pl.BlockSpec((Dm, Bg), lambda l: (0, 0)),
                  pl.BlockSpec((1, Bg, Nc), lambda l: (l, 0, 0)),
                  pl.BlockSpec((1, Bg, Nf), lambda l: (l, 0, 0))],
        out_specs=[pl.BlockSpec((1, Dm, Nc), lambda l: (l, 0, 0)),
                   pl.BlockSpec((1, 1, Nf), lambda l: (l, 0, 0))],
        out_shape=[jax.ShapeDtypeStruct((Lr, Dm, Nc), F32), jax.ShapeDtypeStruct((Lr, 1, Nf), F32)],
        compiler_params=_cp(1),
    )(c_all_t, dmod_cols, dmod_all)


def _conv_taps(x, w, rows):
    shifted = [x]
    c = w[3:4, :] * x
    for k in range(1, 4):
        xs = jnp.where(rows >= k, pltpu.roll(x, k, 0), 0.0)
        shifted.append(xs)
        c = c + w[3 - k:4 - k, :] * xs
    return c, shifted


def conv_silu(proj3, conv_w, name):
    Bl, S, _ = proj3.shape
    ncb = conv_w.shape[1] // 128

    def body(x_ref, w_ref, o_ref):
        rows = lax.broadcasted_iota(jnp.int32, (S, 128), 0)
        c, _ = _conv_taps(_round(x_ref[0]), _round(w_ref[...]), rows)
        o_ref[0] = c * _sigmoid(c)

    return pl.pallas_call(
        body, name=name, grid=(Bl, ncb),
        in_specs=[pl.BlockSpec((1, S, 128), lambda b, j: (b, 0, j)),
                  pl.BlockSpec((4, 128), lambda b, j: (0, j))],
        out_specs=pl.BlockSpec((1, S, 128), lambda b, j: (b, 0, j)),
        out_shape=jax.ShapeDtypeStruct((Bl, S, conv_w.shape[1]), F32), compiler_params=_cp(2),
    )(proj3, conv_w)


def conv_silu_bwd(proj3, conv_w, dq, dk, name):
    Bl, S, _ = proj3.shape
    nq = dq.shape[2] // 128

    def body(x_ref, w_ref, dq_ref, dk_ref, dx_ref, dw_ref):
        j = pl.program_id(1)
        rows = lax.broadcasted_iota(jnp.int32, (S, 128), 0)
        w = _round(w_ref[...])
        c, shifted = _conv_taps(_round(x_ref[0]), w, rows)
        s = _sigmoid(c)
        dact = jnp.where(j < nq, dq_ref[0], dk_ref[0])
        dc = _round(dact * (s * (1.0 + c * (1.0 - s))))
        dx = w[3:4, :] * dc
        dws = [_sum0(dc * shifted[0])]
        for k in range(1, 4):
            up = jnp.where(rows < S - k, pltpu.roll(dc, S - k, 0), 0.0)
            dx = dx + w[3 - k:4 - k, :] * up
            dws.append(_sum0(dc * shifted[k]))
        dx_ref[0] = dx.astype(BF16)
        tap = lax.broadcasted_iota(jnp.int32, (4, 128), 0)
        dw_ref[0] = functools.reduce(lambda a, b: a + b, [jnp.where(tap == 3 - k, dws[k], 0.0) for k in range(4)])

    return pl.pallas_call(
        body, name=name, grid=(Bl, 2 * nq),
        in_specs=[pl.BlockSpec((1, S, 128), lambda b, j: (b, 0, j)),
                  pl.BlockSpec((4, 128), lambda b, j: (0, j)),
                  pl.BlockSpec((1, S, 128), lambda b, j: (b, 0, jnp.minimum(j, nq - 1))),
                  pl.BlockSpec((1, S, 128), lambda b, j: (b, 0, jnp.maximum(j - nq, 0)))],
        out_specs=[pl.BlockSpec((1, S, 128), lambda b, j: (b, 0, j)),
                   pl.BlockSpec((1, 4, 128), lambda b, j: (b, 0, j))],
        out_shape=[jax.ShapeDtypeStruct((Bl, S, 2 * nq * 128), BF16),
                   jax.ShapeDtypeStruct((Bl, 4, 2 * nq * 128), F32)],
        compiler_params=_cp(2),
    )(proj3, conv_w, dq, dk)


def _log_sigmoid(a):
    return jnp.minimum(a, 0.0) - jnp.log(1.0 + jnp.exp(-jnp.abs(a)))


def _chunk_state(kc, vc, gi, bcum, b_last, C, n, m):
    a = b_last - bcum + gi
    m_loc = jnp.max(a, axis=0, keepdims=True)
    wa = jnp.exp(a - m_loc)
    c_loc = _dot_tn((wa * vc).astype(BF16), kc.astype(BF16))
    n_loc = _sum0(_round(wa) * _round(kc))
    m_new = jnp.maximum(b_last + m, m_loc)
    sp = jnp.exp(b_last + m - m_new)
    sl = jnp.exp(m_loc - m_new)
    return sp * C + sl * c_loc, sp * n + sl * n_loc, m_new, wa, sp, sl


def _chunk_out(qs, kc, vc, gi_row, bcum, bcum_row, low, C, n, m):
    inter_log = bcum + m
    dlog = jnp.where(low, bcum - bcum_row + gi_row, NEG)
    m_i = jnp.maximum(inter_log, jnp.max(dlog, axis=1, keepdims=True))
    dm = jnp.exp(dlog - m_i)
    iw = jnp.exp(inter_log - m_i)
    qs_b, k_b, v_b = qs.astype(BF16), kc.astype(BF16), vc.astype(BF16)
    sc = _dot_nt(qs_b, k_b) * dm
    qc_ = _dot_nt(qs_b, C.astype(BF16))
    qn = _sum1(_round(qs) * _round(n))
    num = _dot(sc.astype(BF16), v_b) + iw * qc_
    den = _sum1(sc) + iw * qn
    floor = jnp.exp(-m_i)
    dn = jnp.maximum(jnp.abs(den), floor)
    return dict(hc=num / dn, den=den, dn=dn, floor=floor, sc=sc, dm=dm, iw=iw, qc=qc_, qn=qn,
                qs_b=qs_b, k_b=k_b, v_b=v_b)


def _cell_consts(L):
    ri = lax.broadcasted_iota(jnp.int32, (L, L), 0)
    ci = lax.broadcasted_iota(jnp.int32, (L, L), 1)
    return ri == ci, ci <= ri, ri <= ci


def _load_chunk(q_ref, k_ref, v_ref, G, off, L, h, lane):
    hh = h % 2
    qmask = (lane >= M_DQK * hh) & (lane < M_DQK * (hh + 1))
    pair = pl.ds(128 * (h // 2), 128)
    qc = jnp.where(qmask, q_ref[0, pl.ds(off, L), pair], 0.0)
    kc = jnp.where(qmask, k_ref[0, pl.ds(off, L), pair], 0.0)
    vc = v_ref[0, pl.ds(off, L), pl.ds(M_DV * h, M_DV)]
    gi = _sum1(jnp.where(lane == h, G, 0.0))
    gf = _sum1(jnp.where(lane == h + HEADS, G, 0.0))
    return qmask, qc, kc, vc, gi, gf


def _gate_rows(gi, gf, eye, low, upp):
    lf = _log_sigmoid(gf)
    lf_row = _sum0(jnp.where(eye, lf, 0.0))
    gi_row = _sum0(jnp.where(eye, gi, 0.0))
    bcum = _sum1(jnp.where(low, lf_row, 0.0))
    bcum_row = _sum0(jnp.where(upp, lf, 0.0))
    b_last = _sum0(lf)
    return gi_row, bcum, bcum_row, b_last


def _cell_specs(SB, cpb, blk):
    def seq(width, col):
        return pl.BlockSpec((1, SB, width), lambda b, s: (b, blk(s), col))

    def state(rows):
        return pl.BlockSpec((1, HEADS, cpb, rows, 128), lambda b, s: (b, 0, blk(s), 0, 0))

    ins = [seq(D // 2, 0), seq(D // 2, 1), seq(D, 1), seq(D, 2), seq(128, 3 * D // 128),
           pl.BlockSpec((1, D), lambda b, s: (0, 0)), pl.BlockSpec((1, 128), lambda b, s: (0, 0))]
    return ins, [state(M_DV), state(1), state(1)], seq


def mlstm_cell_fwd(qk3, proj3, gain, gbias, name, gather=()):
    Bl, S, _ = qk3.shape
    L = M_CHUNK
    SB = min(M_SLAB, S)
    cpb, nc, nsb = SB // L, S // L, S // SB
    scale = M_DQK ** -0.5

    def body(q_ref, k_ref, v_ref, o_ref, g_ref, gain_ref, gb_ref, y_ref, cst_ref, nst_ref, mst_ref, *state):
        C_s, n_s, m_s = state[:HEADS], state[HEADS:2 * HEADS], state[2 * HEADS:]

        @pl.when(pl.program_id(1) == 0)
        def _():
            for ref in state:
                ref[...] = jnp.zeros_like(ref)
        lane = lax.broadcasted_iota(jnp.int32, (L, 128), 1)
        eye, low, upp = _cell_consts(L)

        def step(c, carry):
            off = pl.multiple_of(c * L, L)
            G = g_ref[0, pl.ds(off, L), :] + gb_ref[...]
            for h in range(HEADS):
                C, n, mb = C_s[h][...], n_s[h][...], m_s[h][...]
                cst_ref[0, h, c] = C
                nst_ref[0, h, c] = n
                mst_ref[0, h, c] = mb
                m = mb[:, 0:1]
                _, qc, kc, vc, gi, gf = _load_chunk(q_ref, k_ref, v_ref, G, off, L, h, lane)
                gi_row, bcum, bcum_row, b_last = _gate_rows(gi, gf, eye, low, upp)
                r = _chunk_out(qc * scale, kc, vc, gi_row, bcum, bcum_row, low, C, n, m)
                hc = r["hc"]
                hn = hc * lax.rsqrt(jnp.mean(hc * hc, axis=-1, keepdims=True) + RMS_EPS)
                cols = pl.ds(M_DV * h, M_DV)
                oc = o_ref[0, pl.ds(off, L), cols]
                y_ref[0, pl.ds(off, L), cols] = (_sigmoid(oc) * hn * gain_ref[:, cols]).astype(BF16)
                C2, n2, m2, _, _, _ = _chunk_state(kc, vc, gi, bcum, b_last, C, n, m)
                C_s[h][...] = C2
                n_s[h][...] = n2
                m_s[h][...] = jnp.broadcast_to(m2, (1, 128))
            return carry

        lax.fori_loop(0, cpb, step, 0)

    ins, states, seq = _cell_specs(SB, cpb, lambda s: s)
    grid = (Bl, nsb)
    body, c_in, c_out, c_shape, c_scratch = host_comm(body, grid, 7, 4, gather=gather)
    return pl.pallas_call(
        body, name=name, grid=grid, in_specs=ins + c_in, out_specs=[seq(D, 0)] + states + c_out,
        out_shape=[jax.ShapeDtypeStruct((Bl, S, D), BF16),
                   jax.ShapeDtypeStruct((Bl, HEADS, nc, M_DV, 128), F32),
                   jax.ShapeDtypeStruct((Bl, HEADS, nc, 1, 128), F32),
                   jax.ShapeDtypeStruct((Bl, HEADS, nc, 1, 128), F32)] + c_shape,
        scratch_shapes=[pltpu.VMEM((M_DV, 128), F32)] * HEADS + [pltpu.VMEM((1, 128), F32)] * (2 * HEADS) + c_scratch,
        compiler_params=_cp(2),
    )(qk3, qk3, proj3, proj3, proj3, gain, gbias, *gather)


def mlstm_cell_bwd(qk3, proj3, gain, gbias, dy3, states, name, exchange=()):
    Bl, S, _ = qk3.shape
    L = M_CHUNK
    SB = min(M_SLAB, S)
    cpb, nsb = SB // L, S // SB
    scale = M_DQK ** -0.5

    def body(q_ref, k_ref, v_ref, o_ref, g_ref, gain_ref, gb_ref, cst_ref, nst_ref, mst_ref, dy_ref,
             dq_ref, dk_ref, dv_ref, do_ref, dg_ref, dgain_ref, dgb_ref, *state):
        dC_s, dn_s, dgain_s, dgb_s = state[:HEADS], state[HEADS:2 * HEADS], state[2 * HEADS:3 * HEADS], state[-1]
        s = pl.program_id(1)

        @pl.when(s == 0)
        def _():
            for ref in state:
                ref[...] = jnp.zeros_like(ref)
        lane = lax.broadcasted_iota(jnp.int32, (L, 128), 1)
        rowi = lax.broadcasted_iota(jnp.int32, (L, 1), 0)
        eye, low, upp = _cell_consts(L)

        def bstep(t, carry):
            c = cpb - 1 - t
            off = pl.multiple_of(c * L, L)
            G = g_ref[0, pl.ds(off, L), :] + gb_ref[...]
            slab = jnp.zeros((L, 128), F32)
            dq_pair = dk_pair = None
            for h in range(HEADS):
                cols = pl.ds(M_DV * h, M_DV)
                gain_h = gain_ref[:, cols]
                C, n, m = cst_ref[0, h, c], nst_ref[0, h, c], mst_ref[0, h, c][:, 0:1]
                dC_n, dn_n = dC_s[h][...], dn_s[h][...]
                qmask, qc, kc, vc, gi, gf = _load_chunk(q_ref, k_ref, v_ref, G, off, L, h, lane)
                gi_row, bcum, bcum_row, b_last = _gate_rows(gi, gf, eye, low, upp)
                qs = qc * scale
                r = _chunk_out(qs, kc, vc, gi_row, bcum, bcum_row, low, C, n, m)
                _, _, _, wa, sp, sl = _chunk_state(kc, vc, gi, bcum, b_last, C, n, m)
                hc, den, dn, sc, dm, iw, qn = r["hc"], r["den"], r["dn"], r["sc"], r["dm"], r["iw"], r["qn"]
                qs_b, k_b, v_b = r["qs_b"], r["k_b"], r["v_b"]
                dy = dy_ref[0, pl.ds(off, L), cols].astype(F32)
                oc = o_ref[0, pl.ds(off, L), cols]
                sig_o = _sigmoid(oc)
                rr = lax.rsqrt(jnp.mean(hc * hc, axis=-1, keepdims=True) + RMS_EPS)
                hn = hc * rr
                dgain_s[h][...] += _sum0(dy * sig_o * hn)
                do_ref[0, pl.ds(off, L), cols] = (
                    dy * hn * gain_h * sig_o * (1.0 - sig_o)).astype(BF16)
                dhn = dy * sig_o * gain_h
                dhc = rr * dhn - hc * (rr * rr * rr) * jnp.mean(dhn * hc, axis=-1, keepdims=True)
                dnum = dhc / dn
                gden = -_sum1(dhc * hc) / dn
                dden = jnp.where(jnp.abs(den) > r["floor"], gden * jnp.sign(den), 0.0)
                dnum_b = dnum.astype(BF16)
                dsc = _dot_nt(dnum_b, v_b) + dden
                dv = _dot_tn(sc.astype(BF16), dnum_b)
                diw = _sum1(dnum * r["qc"]) + dden * qn
                dqc_b = (iw * dnum).astype(BF16)
                wq = iw * dden
                dqs = _dot(dqc_b, C.astype(BF16)) + wq * n
                dC_out = _dot_tn(dqc_b, qs_b)
                dn_out = _sum0(wq * qs)
                dS_b = (dsc * dm).astype(BF16)
                gm = dsc * sc
                dqs = dqs + _dot(dS_b, k_b)
                dk = _dot_tn(dS_b, qs_b)
                dbc = _sum1(gm) + diw * iw
                colg = _sum0(gm)
                dC_p = sp * dC_n + dC_out
                dn_p = sp * dn_n + dn_out
                dcl_b = (sl * dC_n).astype(BF16)
                dn_loc = sl * dn_n
                dsp = _sum1(_sum0(dC_n * C)) + _sum1(dn_n * n)
                db_last = dsp * sp
                t1 = _dot(v_b, dcl_b) + dn_loc
                dwa = _sum1(t1 * kc)
                dv = dv + wa * _dot_nt(k_b, dcl_b)
                dk = dk + wa * t1
                da = dwa * wa
                db_last = db_last + _sum0(da)
                dbc = dbc - da + jnp.where(rowi == L - 1, db_last, 0.0)
                dbc_row = _sum0(jnp.where(eye, dbc, 0.0)) - colg
                dgi = da + _sum1(jnp.where(eye, colg, 0.0))
                dlf = _sum1(jnp.where(upp, dbc_row, 0.0))
                dgf = dlf * _sigmoid(-gf)
                dq = jnp.where(qmask, dqs * scale, 0.0)
                dk = jnp.where(qmask, dk, 0.0)
                slab = slab + jnp.where(lane == h, dgi, 0.0) + jnp.where(lane == h + HEADS, dgf, 0.0)
                dv_ref[0, pl.ds(off, L), cols] = dv.astype(BF16)
                dC_s[h][...] = dC_p
                dn_s[h][...] = dn_p
                if h % 2 == 0:
                    dq_pair, dk_pair = dq, dk
                else:
                    pair = pl.ds(128 * (h // 2), 128)
                    dq_ref[0, pl.ds(off, L), pair] = dq_pair + dq
                    dk_ref[0, pl.ds(off, L), pair] = dk_pair + dk
            dg_ref[0, pl.ds(off, L), :] = slab
            dgb_s[...] += _sum0(slab)
            return carry

        lax.fori_loop(0, cpb, bstep, 0)

        @pl.when(s == nsb - 1)
        def _():
            for h in range(HEADS):
                dgain_ref[0, :, pl.ds(M_DV * h, M_DV)] = dgain_s[h][...]
            dgb_ref[0] = dgb_s[...]

    ins, states_specs, seq = _cell_specs(SB, cpb, lambda s: nsb - 1 - s)
    once = lambda width: pl.BlockSpec((1, 1, width), lambda b, s: (b, 0, 0))
    grid = (Bl, nsb)
    body, c_in, c_out, c_shape, c_scratch = host_comm(body, grid, 11, 7, exchange=exchange)
    return pl.pallas_call(
        body, name=name, grid=grid, in_specs=ins + states_specs + [seq(D, 0)] + c_in,
        out_specs=[seq(D // 2, 0), seq(D // 2, 0), seq(D, 0), seq(D, 0), seq(128, 0), once(D), once(128)] + c_out,
        out_shape=[jax.ShapeDtypeStruct((Bl, S, D // 2), F32), jax.ShapeDtypeStruct((Bl, S, D // 2), F32),
                   jax.ShapeDtypeStruct((Bl, S, D), BF16), jax.ShapeDtypeStruct((Bl, S, D), BF16),
                   jax.ShapeDtypeStruct((Bl, S, 128), F32), jax.ShapeDtypeStruct((Bl, 1, D), F32),
                   jax.ShapeDtypeStruct((Bl, 1, 128), F32)] + c_shape,
        scratch_shapes=[pltpu.VMEM((M_DV, 128), F32)] * HEADS + [pltpu.VMEM((1, 128), F32)] * (2 * HEADS + 1)
        + c_scratch,
        compiler_params=_cp(2),
    )(qk3, qk3, proj3, proj3, proj3, gain, gbias, *states, dy3, *exchange)


def _attn_scores(q, kc, kp, n, row, col, scale):
    s_c = jnp.where(col <= row, _dot_nt(q, kc) * scale, NEG)
    s_p = jnp.where(jnp.logical_and(col >= row, n > 0), _dot_nt(q, kp) * scale, NEG)
    return s_c, s_p


def _to_streams(src, dst, tmp, dil, Sd):
    if dil == 1:
        dst[...] = src[...].astype(dst.dtype)
        return
    if src.dtype != F32:
        tmp[...] = src[...].astype(F32)
        src = tmp
    for r in range(dil):
        dst[pl.ds(r * Sd, Sd), :] = src[pl.ds(r, Sd, stride=dil), :].astype(dst.dtype)


def _from_streams(src, dst, dil, Sd):
    if dil == 1:
        dst[...] = src[...]
        return
    for r in range(dil):
        dst[pl.ds(r, Sd, stride=dil), :] = src[pl.ds(r * Sd, Sd), :]


def attn_fwd(proj, Bl, S, g, dil, name):
    Sd = S // dil
    nb = Sd // A_BLK
    scale = A_BLK ** -0.5
    pv = proj.reshape(Bl, S, A_PROJ)

    def body(q_ref, k_ref, v_ref, o_ref, l_ref, tmp, qs, ks, vs, os_, ls):
        row = lax.broadcasted_iota(jnp.int32, (A_BLK, A_BLK), 0)
        col = lax.broadcasted_iota(jnp.int32, (A_BLK, A_BLK), 1)
        for src, dst in ((q_ref, qs), (k_ref, ks), (v_ref, vs)):
            _to_streams(src.at[0], dst, tmp, dil, Sd)

        def step(i, carry):
            n = i % nb
            off = pl.multiple_of(i * A_BLK, A_BLK)
            offp = pl.multiple_of(jnp.maximum(i - 1, 0) * A_BLK, A_BLK)
            q = qs[pl.ds(off, A_BLK), :]
            s_c, s_p = _attn_scores(q, ks[pl.ds(off, A_BLK), :], ks[pl.ds(offp, A_BLK), :], n, row, col, scale)
            m = jnp.maximum(jnp.max(s_c, axis=1, keepdims=True), jnp.max(s_p, axis=1, keepdims=True))
            p_c = jnp.exp(s_c - m)
            p_p = jnp.exp(s_p - m)
            den = _sum1(p_c) + _sum1(p_p)
            o = _dot(p_c.astype(BF16), vs[pl.ds(off, A_BLK), :]) + _dot(p_p.astype(BF16), vs[pl.ds(offp, A_BLK), :])
            os_[pl.ds(off, A_BLK), :] = o / den
            ls[pl.ds(off, A_BLK), :] = jnp.broadcast_to(m + jnp.log(den), (A_BLK, 128))
            return carry

        lax.fori_loop(0, dil * nb, step, 0, unroll=A_UNROLL)
        _from_streams(os_, o_ref.at[0], dil, Sd)
        _from_streams(ls, l_ref.at[0], dil, Sd)

    def spec(j):
        return pl.BlockSpec((1, S, 128), lambda b, h: (b, 0, g * 24 + j * HEADS + h))

    ospec = pl.BlockSpec((1, S, 128), lambda b, h: (b, 0, h))
    o, lse = pl.pallas_call(
        body, name=name, grid=(Bl, HEADS),
        in_specs=[spec(0), spec(1), spec(2)], out_specs=[ospec, ospec],
        out_shape=[jax.ShapeDtypeStruct((Bl, S, D), F32)] * 2,
        scratch_shapes=[pltpu.VMEM((S, 128), F32)] + [pltpu.VMEM((S, 128), BF16)] * 3 + [pltpu.VMEM((S, 128), F32)] * 2,
        compiler_params=_cp(2),
    )(pv, pv, pv)
    return o.reshape(Bl * S, D), lse.reshape(Bl * S, D)


def attn_merge(os_, lses, name):
    T = os_[0].shape[0]
    tm = _tile(T, 512)
    ng = len(os_)

    def body(*refs):
        o_refs, l_refs = refs[:ng], refs[ng:2 * ng]
        ob_ref, of_ref, lt_ref = refs[2 * ng:]
        ls = [r[...] for r in l_refs]
        m = functools.reduce(jnp.maximum, ls)
        ws = [jnp.exp(l - m) for l in ls]
        den = functools.reduce(lambda a, b: a + b, ws)
        o = functools.reduce(lambda a, b: a + b, [w * r[...] for w, r in zip(ws, o_refs)]) / den
        of_ref[...] = o
        ob_ref[...] = o.astype(BF16)
        lt_ref[...] = m + jnp.log(den)

    spec = pl.BlockSpec((tm, D), lambda i: (i, 0))
    return pl.pallas_call(
        body, name=name, grid=(T // tm,), in_specs=[spec] * (2 * ng), out_specs=[spec] * 3,
        out_shape=[jax.ShapeDtypeStruct((T, D), BF16), jax.ShapeDtypeStruct((T, D), F32),
                   jax.ShapeDtypeStruct((T, D), F32)],
        compiler_params=_cp(1),
    )(*os_, *lses)


def attn_bwd(proj, do, o, lse, Bl, S, g, dil, name):
    Sd = S // dil
    nb = Sd // A_BLK
    scale = A_BLK ** -0.5
    pv = proj.reshape(Bl, S, A_PROJ)
    dov, ov, lv = (t.reshape(Bl, S, D) for t in (do, o, lse))

    def body(q_ref, k_ref, v_ref, do_ref, o_ref, l_ref, dq_ref, dk_ref, dv_ref,
             tmp, qs, ks, vs, dos, dls, lts, dq_s, dk_s, dv_s):
        row = lax.broadcasted_iota(jnp.int32, (A_BLK, A_BLK), 0)
        col = lax.broadcasted_iota(jnp.int32, (A_BLK, A_BLK), 1)
        for src, dst in ((q_ref, qs), (k_ref, ks), (v_ref, vs), (do_ref, dos), (l_ref, lts)):
            _to_streams(src.at[0], dst, tmp, dil, Sd)
        tmp[...] = jnp.broadcast_to(_sum1(do_ref[0].astype(F32) * o_ref[0]), (S, 128))
        _to_streams(tmp, dls, None, dil, Sd)
        dk_s[...] = jnp.zeros_like(dk_s)
        dv_s[...] = jnp.zeros_like(dv_s)

        def step(i, carry):
            n = i % nb
            off = pl.multiple_of(i * A_BLK, A_BLK)
            offp = pl.multiple_of(jnp.maximum(i - 1, 0) * A_BLK, A_BLK)
            q = qs[pl.ds(off, A_BLK), :]
            kc, kp = ks[pl.ds(off, A_BLK), :], ks[pl.ds(offp, A_BLK), :]
            vc, vp = vs[pl.ds(off, A_BLK), :], vs[pl.ds(offp, A_BLK), :]
            do_b = dos[pl.ds(off, A_BLK), :]
            delta = dls[pl.ds(off, A_BLK), :][:, 0:1]
            lt = lts[pl.ds(off, A_BLK), :][:, 0:1]
            s_c, s_p = _attn_scores(q, kc, kp, n, row, col, scale)
            p_c = jnp.exp(s_c - lt)
            p_p = jnp.exp(s_p - lt)
            ds_c = (p_c * (_dot_nt(do_b, vc) - delta) * scale).astype(BF16)
            ds_p = (p_p * (_dot_nt(do_b, vp) - delta) * scale).astype(BF16)
            dq_s[pl.ds(off, A_BLK), :] = _dot(ds_c, kc) + _dot(ds_p, kp)
            dk_s[pl.ds(off, A_BLK), :] += _dot_tn(ds_c, q)
            dk_s[pl.ds(offp, A_BLK), :] += _dot_tn(ds_p, q)
            dv_s[pl.ds(off, A_BLK), :] += _dot_tn(p_c.astype(BF16), do_b)
            dv_s[pl.ds(offp, A_BLK), :] += _dot_tn(p_p.astype(BF16), do_b)
            return carry

        lax.fori_loop(0, dil * nb, step, 0, unroll=A_UNROLL)
        for src, dst in ((dq_s, dq_ref), (dk_s, dk_ref), (dv_s, dv_ref)):
            _from_streams(src, tmp, dil, Sd)
            dst[0] = tmp[...].astype(BF16)

    def spec(j):
        return pl.BlockSpec((1, S, 128), lambda b, h: (b, 0, g * 24 + j * HEADS + h))

    ospec = pl.BlockSpec((1, S, 128), lambda b, h: (b, 0, h))
    slab = lambda dt: pltpu.VMEM((S, 128), dt)
    outs = pl.pallas_call(
        body, name=name, grid=(Bl, HEADS),
        in_specs=[spec(0), spec(1), spec(2), ospec, ospec, ospec], out_specs=[ospec] * 3,
        out_shape=[jax.ShapeDtypeStruct((Bl, S, D), BF16)] * 3,
        scratch_shapes=[slab(F32)] + [slab(BF16)] * 4 + [slab(F32)] * 5,
        compiler_params=_cp(2),
    )(pv, pv, pv, dov, ov, lv)
    return [t.reshape(Bl * S, D) for t in outs]


def _as_slots(pair, shape):
    return tuple(t.reshape(shape) for t in pair)


def ffn_fwd(x, mod3, w_in, w_out, lng, lnb, tag, gather=()):
    a, g, u, h, gathered = ffn_in(x, mod3, w_in, tag + "_in", gather=gather)
    out, xn = proj_post(a, w_out, x, mod3, lng, lnb, 0.5, tag + "_out")
    return xn, (x, out, g, u, h, a), gathered


def ffn_bwd(dxn, saved, mod3, w_in, w_out, lng, tag, exchange=(), exchange_own=False):
    x, out, g, u, h, a = saved
    dxres, dout, dgu, dlg, dlb, dgate, *received = post_bwd(dxn, x, out, mod3, lng, w_out, 0.5, tag + "_outb",
                                                            tm=256, gu=(g, u), exchange=exchange)
    dw_in = mm_tn(h, dgu, tag + "_dwin", a_copies=True)
    dw_out = _as_slots(mm_tn(a, dout, tag + "_dwout", bw=D), (N_DEV, D_FF // N_DEV, D))
    dx, dsh, dsc, *own = modmm_bwd(dgu, w_in, x, mod3, dxres, tag + "_inb", tm=256,
                                   exchange=[dw_in[1], dw_out[1]] if exchange_own else ())
    dmod3 = jnp.concatenate([dsh, dsc, dgate], axis=1)
    return dx, [dw_in, dw_out], dlg, dlb, dmod3, (received[0] if received else []), (own[0] if own else [])


def mlstm_fwd(x, mod3, w_in, w_out, conv_w, gain, gbias, lng, lnb, Bl, S, gather=()):
    proj, h = modmm(x, mod3, w_in, F32, "ml_in", tn=M_PROJ_PAD // 5)
    proj3 = proj.reshape(Bl, S, M_PROJ_PAD)
    qk3 = conv_silu(proj3, conv_w, "ml_conv")
    y3, *rest = mlstm_cell_fwd(qk3, proj3, gain, gbias, "ml_cell", gather=gather)
    states, gathered = rest[:3], rest[3:]
    y = y3.reshape(Bl * S, D)
    out, xn = proj_post(y[None], w_out, x, mod3, lng, lnb, 1.0, "ml_out")
    return xn, (x, out, h, proj3, qk3, y, states), gathered


def mlstm_bwd(dxn, saved, mod3, w_in, w_out, conv_w, gain, gbias, lng, Bl, S, exchange=()):
    x, out, h, proj3, qk3, y, states = saved
    dxres, dout, dy, dlg, dlb, dgate = post_bwd(dxn, x, out, mod3, lng, w_out, 1.0, "ml_outb")
    dq, dk, dv, do, dg, dgain, dgb, *received = mlstm_cell_bwd(qk3, proj3, gain, gbias, dy.reshape(Bl, S, D),
                                                               states, "ml_cellb", exchange=exchange)
    dqk, dconv = conv_silu_bwd(proj3, conv_w, dq, dk, "ml_convb")
    dproj = jnp.concatenate([dqk, dv, do, dg.astype(BF16)], axis=2).reshape(Bl * S, M_PROJ_PAD)
    dx, dsh, dsc = modmm_bwd(dproj, w_in, x, mod3, dxres, "ml_inb", tn=M_PROJ_PAD // 5)
    dwi, _ = mm_tn(h, dproj, "ml_dwin", bw=M_PROJ_PAD // 5)
    dwi = _restack(jnp.moveaxis(dwi, 0, 1).reshape(D, M_PROJ_PAD)[:, :M_PROJ], 1)
    dw_out = _as_slots(mm_tn(y, dout, "ml_dwout", bw=D), (N_DEV, D // N_DEV, D))
    small = (jnp.sum(dconv, axis=0), jnp.sum(dgain, axis=0), jnp.sum(dgb, axis=0)[:, :2 * HEADS])
    dmod3 = jnp.concatenate([dsh, dsc, dgate], axis=1)
    return dx, [(dwi, dwi.astype(BF16)), dw_out], dlg, dlb, dmod3, small, received


def attn_mixer_fwd(x, mod3, w_in, w_out, lng, lnb, Bl, S):
    proj, h = modmm(x, mod3, w_in, BF16, "at_in")
    os_, lses = [], []
    for g, (_, dil) in enumerate(DIL_GROUPS):
        o_g, l_g = attn_fwd(proj, Bl, S, g, dil, "at_core%d" % g)
        os_.append(o_g)
        lses.append(l_g)
    ob, of, lt = attn_merge(os_, lses, "at_merge")
    out, xn = proj_post(ob[None], w_out, x, mod3, lng, lnb, 1.0, "at_out")
    return xn, (x, out, h, proj, ob, of, lt)


def attn_mixer_bwd(dxn, saved, mod3, w_in, w_out, lng, Bl, S):
    x, out, h, proj, ob, of, lt = saved
    dxres, dout, do, dlg, dlb, dgate = post_bwd(dxn, x, out, mod3, lng, w_out, 1.0, "at_outb")
    do = do[0]
    parts = []
    for g, (_, dil) in enumerate(DIL_GROUPS):
        parts += attn_bwd(proj, do, of, lt, Bl, S, g, dil, "at_coreb%d" % g)
    dproj = jnp.concatenate(parts, axis=1)
    dx, dsh, dsc = modmm_bwd(dproj, w_in, x, mod3, dxres, "at_inb")
    dw_in = mm_tn(h, dproj, "at_dwin", bw=w_in.shape[2])
    dw_out = _as_slots(mm_tn(ob, dout, "at_dwout", bw=D), (N_DEV, D // N_DEV, D))
    return dx, [dw_in, dw_out], dlg, dlb, jnp.concatenate([dsh, dsc, dgate], axis=1)


def _unstack(stacked, axis):
    full = jnp.moveaxis(stacked, 0, axis)
    shp = list(full.shape)
    shp[axis:axis + 2] = [shp[axis] * shp[axis + 1]]
    return full.reshape(shp)


def _restack(full, axis):
    shp = list(full.shape)
    shp[axis:axis + 1] = [N_DEV, shp[axis] // N_DEV]
    return jnp.moveaxis(full.reshape(shp), axis, 0)


def kernel(x, c, ada_w, ada_b, ln_g, ln_b, ffn_w_in, ffn_w_out, mlstm_w_in, mlstm_gate_bias, mlstm_conv_w, mlstm_head_gain, mlstm_w_out, attn_w_in, attn_w_out, loss_target, m_ada_w, m_ada_b, m_ln_g, m_ln_b, m_ffn_w_in, m_ffn_w_out, m_mlstm_w_in, m_mlstm_gate_bias, m_mlstm_conv_w, m_mlstm_head_gain, m_mlstm_w_out, m_attn_w_in, m_attn_w_out, v_ada_w, v_ada_b, v_ln_g, v_ln_b, v_ffn_w_in, v_ffn_w_out, v_mlstm_w_in, v_mlstm_gate_bias, v_mlstm_conv_w, v_mlstm_head_gain, v_mlstm_w_out, v_attn_w_in, v_attn_w_out):
    Bl, S, _ = x.shape
    T = Bl * S
    Bg = Bl * N_DEV
    me = 4 * lax.axis_index("x") + 2 * lax.axis_index("y") + lax.axis_index("c")
    onehot = (jnp.arange(N_DEV) == me).astype(F32)
    weights = dict(ada_w=ada_w, ada_b=ada_b, ln_g=ln_g, ln_b=ln_b, ffn_w_in=ffn_w_in, ffn_w_out=ffn_w_out,
                   mlstm_w_in=mlstm_w_in, mlstm_gate_bias=mlstm_gate_bias, mlstm_conv_w=mlstm_conv_w,
                   mlstm_head_gain=mlstm_head_gain, mlstm_w_out=mlstm_w_out, attn_w_in=attn_w_in,
                   attn_w_out=attn_w_out)
    m_in = dict(ada_w=m_ada_w, ada_b=m_ada_b, ln_g=m_ln_g, ln_b=m_ln_b, ffn_w_in=m_ffn_w_in,
                ffn_w_out=m_ffn_w_out, mlstm_w_in=m_mlstm_w_in, mlstm_gate_bias=m_mlstm_gate_bias,
                mlstm_conv_w=m_mlstm_conv_w, mlstm_head_gain=m_mlstm_head_gain, mlstm_w_out=m_mlstm_w_out,
                attn_w_in=m_attn_w_in, attn_w_out=m_attn_w_out)
    v_in = dict(ada_w=v_ada_w, ada_b=v_ada_b, ln_g=v_ln_g, ln_b=v_ln_b, ffn_w_in=v_ffn_w_in,
                ffn_w_out=v_ffn_w_out, mlstm_w_in=v_mlstm_w_in, mlstm_gate_bias=v_mlstm_gate_bias,
                mlstm_conv_w=v_mlstm_conv_w, mlstm_head_gain=v_mlstm_head_gain, mlstm_w_out=v_mlstm_w_out,
                attn_w_in=v_attn_w_in, attn_w_out=v_attn_w_out)

    mixer = ("mlstm", "attn")
    shards = [[ffn_w_in[layer, 0], ffn_w_in[layer, 1], ffn_w_out[layer, 0], ffn_w_out[layer, 1],
               weights[mixer[layer] + "_w_in"][0], weights[mixer[layer] + "_w_out"][0]] for layer in range(DEPTH)]
    sends = [[s.astype(BF16) for s in layer_shards] for layer_shards in shards]
    small = jnp.concatenate([c.reshape(-1), ln_g.reshape(-1), ln_b.reshape(-1), mlstm_conv_w.reshape(-1)])
    n_small = small.shape[0]
    small = jnp.pad(small, (0, -n_small % (8 * PACK_COLS))).reshape(-1, PACK_COLS)

    def gathered_weights(g):
        return ((g[0], g[1]), (g[2].reshape(4, D_FF // 4, D), g[3].reshape(4, D_FF // 4, D)), g[4],
                g[5].reshape(1, D, D))

    first_in, small_all = all_gather([sends[0][0], small], "ag_params")
    full = [None, None]
    small_flat = small_all.reshape(N_DEV, -1)
    o0 = 0
    c_all = small_flat[:, o0:o0 + c.size].reshape(Bg, D)
    o0 += c.size
    lng_full = _unstack(small_flat[:, o0:o0 + ln_g.size].reshape((N_DEV,) + ln_g.shape), 2)
    o0 += ln_g.size
    lnb_full = _unstack(small_flat[:, o0:o0 + ln_b.size].reshape((N_DEV,) + ln_b.shape), 2)
    o0 += ln_b.size
    conv_full = _unstack(small_flat[:, o0:o0 + mlstm_conv_w.size].reshape((N_DEV,) + mlstm_conv_w.shape), 2)[0]
    gbias =jnp.pad(mlstm_gate_bias, ((0, 0), (0, 128 - 2 * HEADS)))

    ncols = ada_w.shape[2]
    ada_b_cols = lax.dynamic_slice_in_dim(ada_b, me * ncols, ncols, axis=1).reshape(DEPTH, 1, ncols)
    mod_cols = ada_fwd(c_all, ada_w, ada_b_cols, "ada_fwd")
    (mod_g,) = all_gather([mod_cols.reshape(DEPTH * Bg, ncols)], "ag_mod")
    mod_full = _unstack(mod_g.reshape(N_DEV, DEPTH, Bg, ncols), 2)
    mod_mine = lax.dynamic_slice_in_dim(mod_full, me * Bl, Bl, axis=1).reshape(DEPTH, Bl, 3, 3, D)

    xt = x.reshape(T, D)
    saved = []
    for layer in range(DEPTH):
        def lnp(s, layer=layer):
            return lng_full[layer, s].reshape(1, D), lnb_full[layer, s].reshape(1, D)
        md = mod_mine[layer]
        if layer == 0:
            a, g, u, h, late = ffn_in(xt, md[:, 0], first_in, "f0a_in", gather=sends[0][1:])
            full[0] = gathered_weights([first_in] + late)
            out, xn = proj_post(a, full[0][1][0], xt, md[:, 0], *lnp(0), 0.5, "f0a_out")
            xt, sv0 = xn, (xt, out, g, u, h, a)
            mw_in = jnp.pad(_unstack(full[0][2], 1), ((0, 0), (0, M_PROJ_PAD - M_PROJ)))
        else:
            xt, sv0, _ = ffn_fwd(xt, md[:, 0], full[layer][0][0], full[layer][1][0], *lnp(0), "f%da" % layer)
        f_in, f_out, mix_in, mix_out = full[layer]
        if layer % 2 == 0:
            xt, sv1, g1 = mlstm_fwd(xt, md[:, 1], mw_in, mix_out, conv_full, mlstm_head_gain, gbias, *lnp(1), Bl, S,
                                    gather=sends[1])
            full[1] = gathered_weights(g1)
        else:
            xt, sv1 = attn_mixer_fwd(xt, md[:, 1], mix_in, mix_out, *lnp(1), Bl, S)
        xt, sv2, _ = ffn_fwd(xt, md[:, 2], f_in[1], f_out[1], *lnp(2), "f%db" % layer)
        saved.append((sv0, sv1, sv2))

    dxt, lsum = loss_head(xt, loss_target.reshape(T, D), "loss")
    loss = lax.psum(lsum[0, 0], MESH_AXES)

    dmod, dlg_all, dlb_all = [None] * DEPTH, [None] * DEPTH, [None] * DEPTH
    wgrads = [None] * DEPTH
    recvs = [[None] * 6 for _ in range(DEPTH)]
    ml_small = None
    for layer in reversed(range(DEPTH)):
        md = mod_mine[layer]
        f_in, f_out, mix_in, mix_out = full[layer]
        sv0, sv1, sv2 = saved[layer]
        dxt, dw2, dlg2, dlb2, dm2, _, _ = ffn_bwd(dxt, sv2, md[:, 2], f_in[1], f_out[1],
                                                  lng_full[layer, 2].reshape(1, D), "f%db" % layer)
        lg1 = lng_full[layer, 1].reshape(1, D)
        if layer % 2 == 0:
            dxt, dw1, dlg1, dlb1, dm1, ml_small, got = mlstm_bwd(
                dxt, sv1, md[:, 1], mw_in, mix_out, conv_full, mlstm_head_gain, gbias, lg1, Bl, S,
                exchange=[b16 for _, b16 in wgrads[1]] + [dw2[0][1], dw2[1][1]])
            recvs[1], recvs[0][1], recvs[0][3] = got[:6], got[6], got[7]
            dxt, dw0, dlg0, dlb0, dm0, got, own = ffn_bwd(dxt, sv0, md[:, 0], f_in[0], f_out[0],
                                                          lng_full[layer, 0].reshape(1, D), "f%da" % layer,
                                                          exchange=[dw1[0][1], dw1[1][1]], exchange_own=True)
            (recvs[0][4], recvs[0][5]), (recvs[0][0], recvs[0][2]) = got, own
        else:
            dxt, dw1, dlg1, dlb1, dm1 = attn_mixer_bwd(dxt, sv1, md[:, 1], mix_in, mix_out, lg1, Bl, S)
            dxt, dw0, dlg0, dlb0, dm0, _, _ = ffn_bwd(dxt, sv0, md[:, 0], f_in[0], f_out[0],
                                                      lng_full[layer, 0].reshape(1, D), "f%da" % layer)
        wgrads[layer] = [dw0[0], dw2[0], dw0[1], dw2[1], dw1[0], dw1[1]]
        dmod[layer] = jnp.stack([dm0, dm1, dm2], axis=1).reshape(Bl, 9 * D)
        dlg_all[layer] = jnp.concatenate([dlg0, dlg1, dlg2], axis=0)
        dlb_all[layer] = jnp.concatenate([dlb0, dlb1, dlb2], axis=0)
    grad_x = dxt.reshape(Bl, S, D)

    gsh = [[shard_sum(lax.dynamic_index_in_dim(f32, me, axis=0, keepdims=False), recv, onehot,
                      "rs_sum%d_%d" % (layer, i))
            for i, ((f32, _), recv) in enumerate(zip(wgrads[layer], recvs[layer]))] for layer in range(DEPTH)]
    grads = {"ffn_w_in": jnp.stack([jnp.stack(g[0:2]) for g in gsh]),
             "ffn_w_out": jnp.stack([jnp.stack(g[2:4]) for g in gsh]),
             "mlstm_w_in": gsh[0][4][None], "mlstm_w_out": gsh[0][5][None],
             "attn_w_in": gsh[1][4][None], "attn_w_out": gsh[1][5][None]}

    dconv, dgain, dgbias = ml_small
    parts = [jnp.stack(dmod).reshape(-1), dgbias.reshape(-1), dgain.reshape(-1),
             jnp.stack(dlg_all).reshape(-1), jnp.stack(dlb_all).reshape(-1), dconv.reshape(-1)]
    sizes = [p.shape[0] for p in parts]
    flat = jnp.concatenate(parts)
    flat = jnp.pad(flat, (0, -flat.shape[0] % (8 * PACK_COLS))).reshape(-1, PACK_COLS)
    (sm_all,) = all_gather([flat], "ag_small")
    sm_sum = sum_leading(sm_all, "small_sum").reshape(-1)
    dmod_all = sm_all.reshape(N_DEV, -1)[:, :sizes[0]].reshape(N_DEV, DEPTH, Bl, 9 * D)
    dmod_all = jnp.moveaxis(dmod_all, 0, 1).reshape(DEPTH, Bg, 9 * D)
    o0 = sizes[0]
    grads["mlstm_gate_bias"] = sm_sum[o0:o0 + sizes[1]].reshape(mlstm_gate_bias.shape)
    o0 += sizes[1]
    grads["mlstm_head_gain"] = sm_sum[o0:o0 + sizes[2]].reshape(mlstm_head_gain.shape)
    o0 += sizes[2]
    nl = ln_g.shape[2]
    g_lng = sm_sum[o0:o0 + sizes[3]].reshape(DEPTH, 3, D)
    o0 += sizes[3]
    g_lnb = sm_sum[o0:o0 + sizes[4]].reshape(DEPTH, 3, D)
    o0 += sizes[4]
    g_conv = sm_sum[o0:o0 + sizes[5]].reshape(1, 4, D)
    grads["ln_g"] = lax.dynamic_slice_in_dim(g_lng, me * nl, nl, axis=2)
    grads["ln_b"] = lax.dynamic_slice_in_dim(g_lnb, me * nl, nl, axis=2)
    grads["mlstm_conv_w"] = lax.dynamic_slice_in_dim(g_conv, me * nl, nl, axis=2)
    dmod_cols = lax.dynamic_slice_in_dim(dmod_all, me * ncols, ncols, axis=2)
    gw, gb = ada_bwd(c_all.T, dmod_cols, dmod_all, "ada_bwd")
    grads["ada_w"] = gw
    grads["ada_b"] = gb.reshape(ada_b.shape)

    names = ["ada_w", "ada_b", "ln_g", "ln_b", "ffn_w_in", "ffn_w_out", "mlstm_w_in", "mlstm_gate_bias",
             "mlstm_conv_w", "mlstm_head_gain", "mlstm_w_out", "attn_w_in", "attn_w_out"]
    deltas, new_m, new_v = [], [], []
    for k in names:
        w = weights[k]
        shp2 = (math.prod(w.shape[:-1]), w.shape[-1])
        d_, m_, v_ = adamw(w.reshape(shp2), grads[k].reshape(shp2), m_in[k].reshape(shp2), v_in[k].reshape(shp2),
                           "adamw_" + k)
        deltas.append(d_.reshape(w.shape))
        new_m.append(m_.reshape(w.shape))
        new_v.append(v_.reshape(w.shape))
    return (loss, grad_x, *[grads[k] for k in names], *deltas, *new_m, *new_v)
```

```python
import functools
import math

import jax
import jax.numpy as jnp
from jax import lax
from jax.experimental import pallas as pl
from jax.experimental.pallas import tpu as pltpu

F32 = jnp.float32
BF16 = jnp.bfloat16

N_DEV = 8
MESH_AXES = ("x", "y", "c")
D = 1024
DEPTH = 2
D_FF = 2816
HEADS = 8
M_DQK = 64
M_DV = 128
M_CHUNK = 64
M_SLAB = 512
M_PROJ = 3088
M_PROJ_PAD = 3200
A_PROJ = 9216
DIL_GROUPS = ((128, 1), (512, 4), (2048, 16))
A_BLK = 128
A_UNROLL = 4
ALPHA = (2 * DEPTH) ** 0.25
LN_EPS = 1e-5
RMS_EPS = 1e-6
ADAM_LR = 0.001
ADAM_B1 = 0.9
ADAM_B2 = 0.999
ADAM_EPS = 1e-08
ADAM_WD = 0.01
ADAM_STEP = 10
NEG = -1e30
V7X_VMEM_LIMIT = 56 * 1024 * 1024
PACK_COLS = 1024
MESH_ID = pl.DeviceIdType.MESH
ANY_SPEC = pl.BlockSpec(memory_space=pl.ANY)


def _cp(n_axes):
    return pltpu.CompilerParams(dimension_semantics=("arbitrary",) * n_axes,
                                vmem_limit_bytes=V7X_VMEM_LIMIT)


def _dot(a, b):
    return jnp.dot(a, b, preferred_element_type=F32)


def _dot_nt(a, b):
    return lax.dot_general(a, b, (((1,), (1,)), ((), ())), preferred_element_type=F32)


def _dot_tn(a, b):
    return lax.dot_general(a, b, (((0,), (0,)), ((), ())), preferred_element_type=F32)


def _sum0(a):
    return jnp.sum(a, axis=0, keepdims=True)


def _sum1(a):
    return jnp.sum(a, axis=1, keepdims=True)


def _round(a):
    return a.astype(BF16).astype(F32)


def _sigmoid(a):
    return 1.0 / (1.0 + jnp.exp(-a))


def _tile(n, pref):
    t = min(n, pref)
    while n % t:
        t //= 2
    return t


def all_gather(arrs, name):
    n = len(arrs)

    def body(*refs):
        gather = Gather(refs[:n], refs[n:2 * n], *refs[2 * n:])
        gather.start()
        gather.finish()

    return pl.pallas_call(
        body, name=name, out_shape=Gather.out_shape(arrs),
        in_specs=[ANY_SPEC] * n, out_specs=[ANY_SPEC] * n, scratch_shapes=Gather.scratch(n),
    )(*arrs)


class Gather:
    def __init__(self, ins, outs, send_sems, recv_sems, local_sems):
        x, y, c = lax.axis_index("x"), lax.axis_index("y"), lax.axis_index("c")
        me, sibling = (x, y, c), (x, y, 1 - c)
        chips = [(1 - x, y), (x, 1 - y), (1 - x, 1 - y)]

        def slot(a, p):
            return outs[a].at[4 * p[0] + 2 * p[1] + p[2]]

        def copy(a, k, block, to, src=None):
            return pltpu.make_async_remote_copy(
                src_ref=slot(a, block) if src is None else src, dst_ref=slot(a, block),
                send_sem=send_sems.at[7 * a + k], recv_sem=recv_sems.at[7 * a + k],
                device_id=to, device_id_type=MESH_ID)

        n = len(ins)
        self.mine = [pltpu.make_async_copy(ins[a], slot(a, me), local_sems.at[a]) for a in range(n)]
        self.first, self.over_ici, self.passed, self.from_sibling = [], [], [], []
        for a in range(n):
            self.first.append(copy(a, 0, me, sibling, src=ins[a]))
            self.from_sibling.append(copy(a, 0, sibling, me))
            for j, chip in enumerate(chips):
                self.first.append(copy(a, 1 + j, me, (*chip, c), src=ins[a]))
                self.over_ici.append(copy(a, 1 + j, (*chip, c), me))
                self.passed.append(copy(a, 4 + j, (*chip, c), sibling))
                self.from_sibling.append(copy(a, 4 + j, (*chip, 1 - c), me))

    @staticmethod
    def out_shape(arrs):
        return [jax.ShapeDtypeStruct((N_DEV,) + a.shape, a.dtype) for a in arrs]

    @staticmethod
    def scratch(n):
        return [pltpu.SemaphoreType.DMA((7 * n,)), pltpu.SemaphoreType.DMA((7 * n,)),
                pltpu.SemaphoreType.DMA((n,))]

    def start(self):
        for cp in self.mine + self.first:
            cp.start()

    def finish(self):
        for landed, onward in zip(self.over_ici, self.passed):
            landed.wait_recv()
            onward.start()
        for cp in self.from_sibling:
            cp.wait_recv()
        for cp in self.first + self.passed:
            cp.wait_send()
        for cp in self.mine:
            cp.wait()


class Exchange:
    def __init__(self, sends, recvs, send_sems, recv_sems, local_sems):
        x, y, c = lax.axis_index("x"), lax.axis_index("y"), lax.axis_index("c")
        me = 4 * x + 2 * y + c
        self.own = [pltpu.make_async_copy(s.at[me], r.at[me], local_sems.at[a])
                    for a, (s, r) in enumerate(zip(sends, recvs))]
        self.copies = []
        for a, (s_ref, r_ref) in enumerate(zip(sends, recvs)):
            for k in range(1, N_DEV):
                px = 1 - x if (k >> 2) & 1 else x
                py = 1 - y if (k >> 1) & 1 else y
                pc = 1 - c if k & 1 else c
                self.copies.append(pltpu.make_async_remote_copy(
                    src_ref=s_ref.at[4 * px + 2 * py + pc], dst_ref=r_ref.at[me],
                    send_sem=send_sems.at[7 * a + k - 1], recv_sem=recv_sems.at[7 * a + k - 1],
                    device_id=(px, py, pc), device_id_type=MESH_ID))

    @staticmethod
    def scratch(n):
        return [pltpu.SemaphoreType.DMA((7 * n,)), pltpu.SemaphoreType.DMA((7 * n,)),
                pltpu.SemaphoreType.DMA((n,))]

    def start(self):
        for cp in self.own + self.copies:
            cp.start()

    def finish(self):
        for cp in self.copies:
            cp.wait_send()
            cp.wait_recv()
        for cp in self.own:
            cp.wait()


def host_comm(body, grid, n_in, n_out, gather=(), exchange=()):
    ng, nx = len(gather), len(exchange)
    if ng + nx == 0:
        return body, [], [], [], []

    def hosted(*refs):
        ins, c_in, rest = refs[:n_in], refs[n_in:n_in + ng + nx], refs[n_in + ng + nx:]
        outs, c_out, rest = rest[:n_out], rest[n_out:n_out + ng + nx], rest[n_out + ng + nx:]
        n_sems = 3 * ((ng > 0) + (nx > 0))
        scratch, sems = rest[:len(rest) - n_sems], rest[len(rest) - n_sems:]

        def comms():
            made = [Gather(c_in[:ng], c_out[:ng], *sems[:3])] if ng else []
            return made + ([Exchange(c_in[ng:], c_out[ng:], *sems[-3:])] if nx else [])

        ids = [pl.program_id(a) for a in range(len(grid))]

        @pl.when(functools.reduce(jnp.logical_and, [i == 0 for i in ids]))
        def _():
            for cm in comms():
                cm.start()
        body(*ins, *outs, *scratch)

        @pl.when(functools.reduce(jnp.logical_and, [i == g - 1 for i, g in zip(ids, grid)]))
        def _():
            for cm in comms():
                cm.finish()

    shapes = Gather.out_shape(gather) + [jax.ShapeDtypeStruct(a.shape, a.dtype) for a in exchange]
    scratch = (Gather.scratch(ng) if ng else []) + (Exchange.scratch(nx) if nx else [])
    return hosted, [ANY_SPEC] * (ng + nx), [ANY_SPEC] * (ng + nx), shapes, scratch


def shard_sum(own, recv, onehot, name):
    R, C = own.shape
    tr = _tile(R, 512)

    def body(oh_ref, own_ref, recv_ref, o_ref):
        acc = None
        for j in range(N_DEV):
            term = jnp.where(oh_ref[j] > 0.5, own_ref[...], recv_ref[j].astype(F32))
            acc = term if acc is None else acc + term
        o_ref[...] = acc

    return pl.pallas_call(
        body, name=name, grid=(R // tr,),
        in_specs=[pl.BlockSpec(memory_space=pltpu.SMEM),
                  pl.BlockSpec((tr, C), lambda i: (i, 0)),
                  pl.BlockSpec((N_DEV, tr, C), lambda i: (0, i, 0))],
        out_specs=pl.BlockSpec((tr, C), lambda i: (i, 0)),
        out_shape=jax.ShapeDtypeStruct((R, C), F32), compiler_params=_cp(1),
    )(onehot, own, recv)


def sum_leading(a, name):
    _, R, C = a.shape

    def body(a_ref, o_ref):
        acc = a_ref[0]
        for j in range(1, N_DEV):
            acc = acc + a_ref[j]
        o_ref[...] = acc

    return pl.pallas_call(body, name=name, out_shape=jax.ShapeDtypeStruct((R, C), F32),
                          compiler_params=_cp(0))(a)


def _col_chunks(w, tn):
    if w.ndim == 3:
        return w.shape[0], w.shape[2], pl.BlockSpec((None, w.shape[1], w.shape[2]), lambda i, j: (j, 0, 0))
    return w.shape[1] // tn, tn, pl.BlockSpec((w.shape[0], tn), lambda i, j: (0, j))


def modmm(x, mod3, w, out_dtype, name, tn=None):
    T, Dm = x.shape
    nj, tn, w_spec = _col_chunks(w, tn)
    N = nj * tn
    Bl = mod3.shape[0]
    tm = _tile(T // Bl, 1024)
    tpb = T // Bl // tm

    def body(x_ref, mod_ref, w_ref, o_ref, h_ref, hs):
        @pl.when(pl.program_id(1) == 0)
        def _():
            m = mod_ref[0]
            hs[...] = (x_ref[...] * (1.0 + m[1:2, :]) + m[0:1, :]).astype(BF16)
            h_ref[...] = hs[...]
        o_ref[...] = _dot(hs[...], w_ref[...]).astype(o_ref.dtype)

    return pl.pallas_call(
        body, name=name, grid=(T // tm, nj),
        in_specs=[pl.BlockSpec((tm, Dm), lambda i, j: (i, 0)),
                  pl.BlockSpec((1, 3, Dm), lambda i, j: (i // tpb, 0, 0)), w_spec],
        out_specs=[pl.BlockSpec((tm, tn), lambda i, j: (i, j)),
                   pl.BlockSpec((tm, Dm), lambda i, j: (i, 0))],
        out_shape=[jax.ShapeDtypeStruct((T, N), out_dtype), jax.ShapeDtypeStruct((T, Dm), BF16)],
        scratch_shapes=[pltpu.VMEM((tm, Dm), BF16)], compiler_params=_cp(2),
    )(x, mod3, w)


def modmm_bwd(dp, w, x, mod3, dxres, name, tn=None, tm=512, exchange=()):
    T, Dm = x.shape
    Bl = mod3.shape[0]
    tm = _tile(T // Bl, tm)
    tpb = T // Bl // tm
    resident = dp.ndim == 3
    if resident:
        nc, nj = dp.shape[0], 1
        dp_spec = pl.BlockSpec((nc, tm, dp.shape[2]), lambda i, j: (0, i, 0))
        w_spec = pl.BlockSpec(w.shape, lambda i, j: (0, 0, 0))
    else:
        nj, tn, w_spec = _col_chunks(w, tn)
        dp_spec = pl.BlockSpec((tm, tn), lambda i, j: (i, j))

    def body(dp_ref, w_ref, x_ref, mod_ref, dxr_ref, dx_ref, dsh_ref, dsc_ref, acc):
        i, j = pl.program_id(0), pl.program_id(1)

        @pl.when(j == 0)
        def _():
            acc[...] = jnp.zeros_like(acc)
        if resident:
            for c in range(nc):
                acc[...] += _dot_nt(dp_ref[c], w_ref[c])
        else:
            acc[...] += _dot_nt(dp_ref[...], w_ref[...])

        @pl.when(j == nj - 1)
        def _():
            dh = acc[...]
            xx = x_ref[...]
            dx_ref[...] = dxr_ref[...] + dh * (1.0 + mod_ref[0][1:2, :])

            @pl.when(i % tpb == 0)
            def _():
                dsh_ref[...] = jnp.zeros_like(dsh_ref)
                dsc_ref[...] = jnp.zeros_like(dsc_ref)
            dsh_ref[0] += _sum0(dh)
            dsc_ref[0] += _sum0(dh * xx)

    grid = (T // tm, nj)
    body, c_in, c_out, c_shape, c_scratch = host_comm(body, grid, 5, 3, exchange=exchange)
    dx, dsh, dsc, *received = pl.pallas_call(
        body, name=name, grid=grid,
        in_specs=[dp_spec, w_spec,
                  pl.BlockSpec((tm, Dm), lambda i, j: (i, 0)),
                  pl.BlockSpec((1, 3, Dm), lambda i, j: (i // tpb, 0, 0)),
                  pl.BlockSpec((tm, Dm), lambda i, j: (i, 0))] + c_in,
        out_specs=[pl.BlockSpec((tm, Dm), lambda i, j: (i, 0)),
                   pl.BlockSpec((1, 1, Dm), lambda i, j: (i // tpb, 0, 0)),
                   pl.BlockSpec((1, 1, Dm), lambda i, j: (i // tpb, 0, 0))] + c_out,
        out_shape=[jax.ShapeDtypeStruct((T, Dm), F32), jax.ShapeDtypeStruct((Bl, 1, Dm), F32),
                   jax.ShapeDtypeStruct((Bl, 1, Dm), F32)] + c_shape,
        scratch_shapes=[pltpu.VMEM((tm, Dm), F32)] + c_scratch, compiler_params=_cp(2),
    )(dp, w, x, mod3, dxres, *exchange)
    return (dx, dsh, dsc, received) if exchange else (dx, dsh, dsc)


def _ln_stats(z):
    mu = jnp.mean(z, axis=-1, keepdims=True)
    zc = z - mu
    var = jnp.mean(zc * zc, axis=-1, keepdims=True)
    rstd = lax.rsqrt(var + LN_EPS)
    return zc * rstd, rstd


def proj_post(a, w, x, mod3, lng, lnb, weight, name):
    nk, T, tk = a.shape
    Dm = w.shape[2]
    Bl = mod3.shape[0]
    tm = _tile(T // Bl, 512)
    tpb = T // Bl // tm

    def body(a_ref, w_ref, x_ref, mod_ref, g_ref, b_ref, out_ref, xn_ref):
        out = _dot(a_ref[0], w_ref[0])
        for k in range(1, nk):
            out = out + _dot(a_ref[k], w_ref[k])
        out_ref[...] = out
        z = ALPHA * x_ref[...] + (weight * (1.0 + mod_ref[0][2:3, :])) * out
        xhat, _ = _ln_stats(z)
        xn_ref[...] = xhat * g_ref[...] + b_ref[...]

    row = pl.BlockSpec((tm, Dm), lambda i: (i, 0))
    vec = pl.BlockSpec((1, Dm), lambda i: (0, 0))
    return pl.pallas_call(
        body, name=name, grid=(T // tm,),
        in_specs=[pl.BlockSpec((nk, tm, tk), lambda i: (0, i, 0)),
                  pl.BlockSpec((nk, tk, Dm), lambda i: (0, 0, 0)),
                  row, pl.BlockSpec((1, 3, Dm), lambda i: (i // tpb, 0, 0)), vec, vec],
        out_specs=[row, row],
        out_shape=[jax.ShapeDtypeStruct((T, Dm), F32), jax.ShapeDtypeStruct((T, Dm), F32)],
        compiler_params=_cp(1),
    )(a, w, x, mod3, lng, lnb)


def post_bwd(dxn, x, out, mod3, lng, w, weight, name, tm=512, gu=None, exchange=()):
    T, Dm = x.shape
    nk, tk, _ = w.shape
    Bl = mod3.shape[0]
    tm = _tile(T // Bl, tm)
    tpb = T // Bl // tm
    fused = gu is not None

    def body(dxn_ref, x_ref, out_ref, mod_ref, g_ref, w_ref, *rest):
        if fused:
            gg_ref, uu_ref = rest[:2]
            rest = rest[2:]
        dxr_ref, dout_ref, da_ref, dg_ref, db_ref, dgate_ref = rest
        i = pl.program_id(0)
        out = out_ref[...]
        dxn = dxn_ref[...]
        coef = weight * (1.0 + mod_ref[0][2:3, :])
        xhat, rstd = _ln_stats(ALPHA * x_ref[...] + coef * out)
        dyh = dxn * g_ref[...]
        dz = rstd * (dyh - jnp.mean(dyh, axis=-1, keepdims=True)
                     - xhat * jnp.mean(dyh * xhat, axis=-1, keepdims=True))
        dxr_ref[...] = ALPHA * dz
        dout = (coef * dz).astype(BF16)
        dout_ref[...] = dout

        @pl.when(i == 0)
        def _():
            dg_ref[...] = jnp.zeros_like(dg_ref)
            db_ref[...] = jnp.zeros_like(db_ref)

        @pl.when(i % tpb == 0)
        def _():
            dgate_ref[...] = jnp.zeros_like(dgate_ref)
        dg_ref[...] += _sum0(dxn * xhat)
        db_ref[...] += _sum0(dxn)
        dgate_ref[0] += _sum0((weight * out) * dz)
        for k in range(nk):
            da = _dot_nt(dout, w_ref[k])
            if fused:
                gg = gg_ref[k].astype(F32)
                s = _sigmoid(gg)
                da_ref[k] = (da * uu_ref[k].astype(F32) * (s * (1.0 + gg * (1.0 - s)))).astype(BF16)
                da_ref[nk + k] = (da * (gg * s)).astype(BF16)
            else:
                da_ref[k] = da.astype(BF16)

    row = pl.BlockSpec((tm, Dm), lambda i: (i, 0))
    vec = pl.BlockSpec((1, Dm), lambda i: (0, 0))
    wide = pl.BlockSpec((nk, tm, tk), lambda i: (0, i, 0))
    nda = 2 * nk if fused else nk
    grid = (T // tm,)
    body, c_in, c_out, c_shape, c_scratch = host_comm(body, grid, 8 if fused else 6, 6, exchange=exchange)
    *results, = pl.pallas_call(
        body, name=name, grid=grid,
        in_specs=[row, row, row, pl.BlockSpec((1, 3, Dm), lambda i: (i // tpb, 0, 0)), vec,
                  pl.BlockSpec((nk, tk, Dm), lambda i: (0, 0, 0))] + ([wide, wide] if fused else []) + c_in,
        out_specs=[row, row, pl.BlockSpec((nda, tm, tk), lambda i: (0, i, 0)),
                   vec, vec, pl.BlockSpec((1, 1, Dm), lambda i: (i // tpb, 0, 0))] + c_out,
        out_shape=[jax.ShapeDtypeStruct((T, Dm), F32), jax.ShapeDtypeStruct((T, Dm), BF16),
                   jax.ShapeDtypeStruct((nda, T, tk), BF16), jax.ShapeDtypeStruct((1, Dm), F32),
                   jax.ShapeDtypeStruct((1, Dm), F32), jax.ShapeDtypeStruct((Bl, 1, Dm), F32)] + c_shape,
        scratch_shapes=c_scratch, compiler_params=_cp(1),
    )(dxn, x, out, mod3, lng, w, *(gu if fused else ()), *exchange)
    return tuple(results[:6]) + ((results[6:],) if exchange else ())


def mm_tn(a, b, name, bw=None, a_copies=False):
    a3, b3 = a.ndim == 3, b.ndim == 3
    nk, T, tk = a.shape if a3 else (1,) + a.shape
    if a_copies:
        nk = 1
    nc, wn = (b.shape[0], b.shape[2]) if b3 else (b.shape[1] // bw, bw)
    tt = _tile(T, 2048)
    nt = T // tt

    def body(a_ref, b_ref, o_ref, ob_ref):
        t = pl.program_id(2)

        @pl.when(t == 0)
        def _():
            o_ref[...] = jnp.zeros_like(o_ref)
        o_ref[...] += _dot_tn(a_ref[...], b_ref[...])

        @pl.when(t == nt - 1)
        def _():
            ob_ref[...] = o_ref[...].astype(BF16)

    a_spec = (pl.BlockSpec((None, tt, tk), lambda k, c, t: (k, t, 0)) if a3
              else pl.BlockSpec((tt, tk), lambda k, c, t: (t, 0)))
    b_spec = (pl.BlockSpec((None, tt, wn), lambda k, c, t: (c, t, 0)) if b3
              else pl.BlockSpec((tt, wn), lambda k, c, t: (t, c)))
    o_spec = pl.BlockSpec((None, tk, wn), lambda k, c, t: (k * nc + c, 0, 0))
    return pl.pallas_call(
        body, name=name, grid=(nk, nc, nt), in_specs=[a_spec, b_spec], out_specs=[o_spec, o_spec],
        out_shape=[jax.ShapeDtypeStruct((nk * nc, tk, wn), F32), jax.ShapeDtypeStruct((nk * nc, tk, wn), BF16)],
        compiler_params=_cp(3),
    )(a, b)


def ffn_in(x, mod3, w, name, gather=()):
    T, Dm = x.shape
    nj, tf = w.shape[0] // 2, w.shape[2]
    Bl = mod3.shape[0]
    tm = _tile(T // Bl, 1024)
    tpb = T // Bl // tm

    def body(x_ref, mod_ref, wg_ref, wu_ref, a_ref, g_ref, u_ref, h_ref):
        m = mod_ref[0]
        h = (x_ref[...] * (1.0 + m[1:2, :]) + m[0:1, :]).astype(BF16)
        h_ref[...] = h
        g = _dot(h, wg_ref[...])
        u = _dot(h, wu_ref[...])
        a_ref[...] = (g * _sigmoid(g) * u).astype(BF16)
        g_ref[...] = g.astype(BF16)
        u_ref[...] = u.astype(BF16)

    col = pl.BlockSpec((None, tm, tf), lambda j, i: (j, i, 0))
    grid = (nj, T // tm)
    body, c_in, c_out, c_shape, c_scratch = host_comm(body, grid, 4, 4, gather=gather)
    a, g, u, h, *gathered = pl.pallas_call(
        body, name=name, grid=grid,
        in_specs=[pl.BlockSpec((tm, Dm), lambda j, i: (i, 0)),
                  pl.BlockSpec((1, 3, Dm), lambda j, i: (i // tpb, 0, 0)),
                  pl.BlockSpec((None, Dm, tf), lambda j, i: (j, 0, 0)),
                  pl.BlockSpec((None, Dm, tf), lambda j, i: (nj + j, 0, 0))] + c_in,
        out_specs=[col, col, col, pl.BlockSpec((None, tm, Dm), lambda j, i: (j, i, 0))] + c_out,
        out_shape=[jax.ShapeDtypeStruct((nj, T, tf), BF16)] * 3 + [jax.ShapeDtypeStruct((nj, T, Dm), BF16)]
        + c_shape,
        scratch_shapes=c_scratch, compiler_params=_cp(2),
    )(x, mod3, w, w, *gather)
    return a, g, u, h, gathered


def loss_head(y, tgt, name):
    T, Dm = y.shape
    tm = _tile(T, 512)
    nt = T // tm

    def body(y_ref, t_ref, dy_ref, l_ref, acc):
        i = pl.program_id(0)

        @pl.when(i == 0)
        def _():
            acc[...] = jnp.zeros_like(acc)
        e = y_ref[...] - t_ref[...]
        dy_ref[...] = e * (1.0 / Dm)
        acc[...] += _sum0(e * e)

        @pl.when(i == nt - 1)
        def _():
            l_ref[...] = jnp.broadcast_to(_sum1(acc[...]) * (0.5 / Dm), l_ref.shape)

    return pl.pallas_call(
        body, name=name, grid=(nt,),
        in_specs=[pl.BlockSpec((tm, Dm), lambda i: (i, 0)), pl.BlockSpec((tm, Dm), lambda i: (i, 0))],
        out_specs=[pl.BlockSpec((tm, Dm), lambda i: (i, 0)), pl.BlockSpec((1, 128), lambda i: (0, 0))],
        out_shape=[jax.ShapeDtypeStruct((T, Dm), F32), jax.ShapeDtypeStruct((1, 128), F32)],
        scratch_shapes=[pltpu.VMEM((1, Dm), F32)], compiler_params=_cp(1),
    )(y, tgt)


def adamw(w, g, m, v, name):
    R, C = w.shape
    tr = _tile(R, 512) if R % 8 == 0 else R

    def body(w_ref, g_ref, m_ref, v_ref, d_ref, nm_ref, nv_ref):
        gg = g_ref[...]
        mm = ADAM_B1 * m_ref[...] + (1.0 - ADAM_B1) * gg
        vv = ADAM_B2 * v_ref[...] + (1.0 - ADAM_B2) * (gg * gg)
        m_hat = mm / (1.0 - ADAM_B1 ** ADAM_STEP)
        v_hat = vv / (1.0 - ADAM_B2 ** ADAM_STEP)
        d_ref[...] = -ADAM_LR * (m_hat / (jnp.sqrt(v_hat) + ADAM_EPS) + ADAM_WD * w_ref[...])
        nm_ref[...] = mm
        nv_ref[...] = vv

    spec = pl.BlockSpec((tr, C), lambda i: (i, 0))
    return pl.pallas_call(
        body, name=name, grid=(R // tr,), in_specs=[spec] * 4, out_specs=[spec] * 3,
        out_shape=[jax.ShapeDtypeStruct((R, C), F32)] * 3, compiler_params=_cp(1),
    )(w, g, m, v)


def ada_fwd(c_all, ada_w, ada_b_cols, name):
    Lr, Dm, Nc = ada_w.shape
    Bg = c_all.shape[0]

    def body(c_ref, w_ref, b_ref, o_ref):
        cc = c_ref[...]
        cond = cc * _sigmoid(cc)
        o_ref[0] = _dot(cond.astype(BF16), w_ref[0].astype(BF16)) + b_ref[0]

    return pl.pallas_call(
        body, name=name, grid=(Lr,),
        in_specs=[pl.BlockSpec((Bg, Dm), lambda l: (0, 0)),
                  pl.BlockSpec((1, Dm, Nc), lambda l: (l, 0, 0)),
                  pl.BlockSpec((1, 1, Nc), lambda l: (l, 0, 0))],
        out_specs=pl.BlockSpec((1, Bg, Nc), lambda l: (l, 0, 0)),
        out_shape=jax.ShapeDtypeStruct((Lr, Bg, Nc), F32), compiler_params=_cp(1),
    )(c_all, ada_w, ada_b_cols)


def ada_bwd(c_all_t, dmod_cols, dmod_all, name):
    Dm, Bg = c_all_t.shape
    Lr, _, Nc = dmod_cols.shape
    Nf = dmod_all.shape[2]

    def body(c_ref, dm_ref, da_ref, gw_ref, gb_ref):
        cc = c_ref[...]
        cond = cc * _sigmoid(cc)
        gw_ref[0] = _dot(cond.astype(BF16), dm_ref[0].astype(BF16))
        gb_ref[0] = _sum0(da_ref[0])

    return pl.pallas_call(
        body, name=name, grid=(Lr,),
        in_specs=[pl.BlockSpec((Dm, Bg), lambda l: (0, 0)),
                  pl.BlockSpec((1, Bg, Nc), lambda l: (l, 0, 0)),
                  pl.BlockSpec((1, Bg, Nf), lambda l: (l, 0, 0))],
        out_specs=[pl.BlockSpec((1, Dm, Nc), lambda l: (l, 0, 0)),
                   pl.BlockSpec((1, 1, Nf), lambda l: (l, 0, 0))],
        out_shape=[jax.ShapeDtypeStruct((Lr, Dm, Nc), F32), jax.ShapeDtypeStruct((Lr, 1, Nf), F32)],
        compiler_params=_cp(1),
    )(c_all_t, dmod_cols, dmod_all)


def _conv_taps(x, w, rows):
    shifted = [x]
    c = w[3:4, :] * x
    for k in range(1, 4):
        xs = jnp.where(rows >= k, pltpu.roll(x, k, 0), 0.0)
        shifted.append(xs)
        c = c + w[3 - k:4 - k, :] * xs
    return c, shifted


def conv_silu(proj3, conv_w, name):
    Bl, S, _ = proj3.shape
    ncb = conv_w.shape[1] // 128

    def body(x_ref, w_ref, o_ref):
        rows = lax.broadcasted_iota(jnp.int32, (S, 128), 0)
        c, _ = _conv_taps(_round(x_ref[0]), _round(w_ref[...]), rows)
        o_ref[0] = c * _sigmoid(c)

    return pl.pallas_call(
        body, name=name, grid=(Bl, ncb),
        in_specs=[pl.BlockSpec((1, S, 128), lambda b, j: (b, 0, j)),
                  pl.BlockSpec((4, 128), lambda b, j: (0, j))],
        out_specs=pl.BlockSpec((1, S, 128), lambda b, j: (b, 0, j)),
        out_shape=jax.ShapeDtypeStruct((Bl, S, conv_w.shape[1]), F32), compiler_params=_cp(2),
    )(proj3, conv_w)


def conv_silu_bwd(proj3, conv_w, dq, dk, name):
    Bl, S, _ = proj3.shape
    nq = dq.shape[2] // 128

    def body(x_ref, w_ref, dq_ref, dk_ref, dx_ref, dw_ref):
        j = pl.program_id(1)
        rows = lax.broadcasted_iota(jnp.int32, (S, 128), 0)
        w = _round(w_ref[...])
        c, shifted = _conv_taps(_round(x_ref[0]), w, rows)
        s = _sigmoid(c)
        dact = jnp.where(j < nq, dq_ref[0], dk_ref[0])
        dc = _round(dact * (s * (1.0 + c * (1.0 - s))))
        dx = w[3:4, :] * dc
        dws = [_sum0(dc * shifted[0])]
        for k in range(1, 4):
            up = jnp.where(rows < S - k, pltpu.roll(dc, S - k, 0), 0.0)
            dx = dx + w[3 - k:4 - k, :] * up
            dws.append(_sum0(dc * shifted[k]))
        dx_ref[0] = dx.astype(BF16)
        tap = lax.broadcasted_iota(jnp.int32, (4, 128), 0)
        dw_ref[0] = functools.reduce(lambda a, b: a + b, [jnp.where(tap == 3 - k, dws[k], 0.0) for k in range(4)])

    return pl.pallas_call(
        body, name=name, grid=(Bl, 2 * nq),
        in_specs=[pl.BlockSpec((1, S, 128), lambda b, j: (b, 0, j)),
                  pl.BlockSpec((4, 128), lambda b, j: (0, j)),
                  pl.BlockSpec((1, S, 128), lambda b, j: (b, 0, jnp.minimum(j, nq - 1))),
                  pl.BlockSpec((1, S, 128), lambda b, j: (b, 0, jnp.maximum(j - nq, 0)))],
        out_specs=[pl.BlockSpec((1, S, 128), lambda b, j: (b, 0, j)),
                   pl.BlockSpec((1, 4, 128), lambda b, j: (b, 0, j))],
        out_shape=[jax.ShapeDtypeStruct((Bl, S, 2 * nq * 128), BF16),
                   jax.ShapeDtypeStruct((Bl, 4, 2 * nq * 128), F32)],
        compiler_params=_cp(2),
    )(proj3, conv_w, dq, dk)


def _log_sigmoid(a):
    return jnp.minimum(a, 0.0) - jnp.log(1.0 + jnp.exp(-jnp.abs(a)))


def _interleave(gens):
    live = list(gens)
    while live:
        still = []
        for g in live:
            try:
                next(g)
                still.append(g)
            except StopIteration:
                pass
        live = still


def _finish(gen):
    while True:
        try:
            next(gen)
        except StopIteration as done:
            return done.value


def _chunk_state(kc, vc, gi, bcum, b_last, C, n, m):
    a = b_last - bcum + gi
    m_loc = jnp.max(a, axis=0, keepdims=True)
    wa = jnp.exp(a - m_loc)
    c_loc = _dot_tn((wa * vc).astype(BF16), kc.astype(BF16))
    n_loc = _sum0(_round(wa) * _round(kc))
    m_new = jnp.maximum(b_last + m, m_loc)
    sp = jnp.exp(b_last + m - m_new)
    sl = jnp.exp(m_loc - m_new)
    yield
    return sp * C + sl * c_loc, sp * n + sl * n_loc, m_new, wa, sp, sl


def _chunk_out(qs, kc, vc, gi_row, bcum, bcum_row, low, C, n, m):
    inter_log = bcum + m
    dlog = jnp.where(low, bcum - bcum_row + gi_row, NEG)
    m_i = jnp.maximum(inter_log, jnp.max(dlog, axis=1, keepdims=True))
    dm = jnp.exp(dlog - m_i)
    iw = jnp.exp(inter_log - m_i)
    qs_b, k_b, v_b = qs.astype(BF16), kc.astype(BF16), vc.astype(BF16)
    sqk = _dot_nt(qs_b, k_b)
    qc_ = _dot_nt(qs_b, C.astype(BF16))
    qn = _sum1(_round(qs) * _round(n))
    floor = jnp.exp(-m_i)
    yield
    sc = sqk * dm
    sv = _dot(sc.astype(BF16), v_b)
    den = _sum1(sc) + iw * qn
    dn = jnp.maximum(jnp.abs(den), floor)
    yield
    num = sv + iw * qc_
    return dict(hc=num / dn, den=den, dn=dn, floor=floor, sc=sc, dm=dm, iw=iw, qc=qc_, qn=qn,
                qs_b=qs_b, k_b=k_b, v_b=v_b)


def _cell_consts(L):
    ri = lax.broadcasted_iota(jnp.int32, (L, L), 0)
    ci = lax.broadcasted_iota(jnp.int32, (L, L), 1)
    return ri == ci, ci <= ri, ri <= ci


def _load_chunk(q_ref, k_ref, v_ref, G, off, L, h, lane):
    hh = h % 2
    qmask = (lane >= M_DQK * hh) & (lane < M_DQK * (hh + 1))
    pair = pl.ds(128 * (h // 2), 128)
    qc = jnp.where(qmask, q_ref[0, pl.ds(off, L), pair], 0.0)
    kc = jnp.where(qmask, k_ref[0, pl.ds(off, L), pair], 0.0)
    vc = v_ref[0, pl.ds(off, L), pl.ds(M_DV * h, M_DV)]
    gi = _sum1(jnp.where(lane == h, G, 0.0))
    gf = _sum1(jnp.where(lane == h + HEADS, G, 0.0))
    return qmask, qc, kc, vc, gi, gf


def _gate_rows(gi, gf, eye, low, upp):
    lf = _log_sigmoid(gf)
    lf_row = _sum0(jnp.where(eye, lf, 0.0))
    gi_row = _sum0(jnp.where(eye, gi, 0.0))
    bcum = _sum1(jnp.where(low, lf_row, 0.0))
    bcum_row = _sum0(jnp.where(upp, lf, 0.0))
    b_last = _sum0(lf)
    return gi_row, bcum, bcum_row, b_last


def _cell_specs(SB, cpb, blk):
    def seq(width, col):
        return pl.BlockSpec((1, SB, width), lambda b, s: (b, blk(s), col))

    def state(rows):
        return pl.BlockSpec((1, HEADS, cpb, rows, 128), lambda b, s: (b, 0, blk(s), 0, 0))

    ins = [seq(D // 2, 0), seq(D // 2, 1), seq(D, 1), seq(D, 2), seq(128, 3 * D // 128),
           pl.BlockSpec((1, D), lambda b, s: (0, 0)), pl.BlockSpec((1, 128), lambda b, s: (0, 0))]
    return ins, [state(M_DV), state(1), state(1)], seq


def mlstm_cell_fwd(qk3, proj3, gain, gbias, name, gather=()):
    Bl, S, _ = qk3.shape
    L = M_CHUNK
    SB = min(M_SLAB, S)
    cpb, nc, nsb = SB // L, S // L, S // SB
    scale = M_DQK ** -0.5

    def body(q_ref, k_ref, v_ref, o_ref, g_ref, gain_ref, gb_ref, y_ref, cst_ref, nst_ref, mst_ref, *state):
        C_s, n_s, m_s = state[:HEADS], state[HEADS:2 * HEADS], state[2 * HEADS:]

        @pl.when(pl.program_id(1) == 0)
        def _():
            for ref in state:
                ref[...] = jnp.zeros_like(ref)
        lane = lax.broadcasted_iota(jnp.int32, (L, 128), 1)
        eye, low, upp = _cell_consts(L)

        def step(c, carry):
            off = pl.multiple_of(c * L, L)
            G = g_ref[0, pl.ds(off, L), :] + gb_ref[...]

            def head(h):
                C, n, mb = C_s[h][...], n_s[h][...], m_s[h][...]
                cst_ref[0, h, c] = C
                nst_ref[0, h, c] = n
                mst_ref[0, h, c] = mb
                m = mb[:, 0:1]
                _, qc, kc, vc, gi, gf = _load_chunk(q_ref, k_ref, v_ref, G, off, L, h, lane)
                gi_row, bcum, bcum_row, b_last = _gate_rows(gi, gf, eye, low, upp)
                state = _chunk_state(kc, vc, gi, bcum, b_last, C, n, m)
                next(state)
                r = yield from _chunk_out(qc * scale, kc, vc, gi_row, bcum, bcum_row, low, C, n, m)
                hc = r["hc"]
                hn = hc * lax.rsqrt(jnp.mean(hc * hc, axis=-1, keepdims=True) + RMS_EPS)
                cols = pl.ds(M_DV * h, M_DV)
                oc = o_ref[0, pl.ds(off, L), cols]
                y_ref[0, pl.ds(off, L), cols] = (_sigmoid(oc) * hn * gain_ref[:, cols]).astype(BF16)
                C2, n2, m2, _, _, _ = _finish(state)
                C_s[h][...] = C2
                n_s[h][...] = n2
                m_s[h][...] = jnp.broadcast_to(m2, (1, 128))

            _interleave(head(h) for h in range(HEADS))
            return carry

        lax.fori_loop(0, cpb, step, 0)

    ins, states, seq = _cell_specs(SB, cpb, lambda s: s)
    grid = (Bl, nsb)
    body, c_in, c_out, c_shape, c_scratch = host_comm(body, grid, 7, 4, gather=gather)
    return pl.pallas_call(
        body, name=name, grid=grid, in_specs=ins + c_in, out_specs=[seq(D, 0)] + states + c_out,
        out_shape=[jax.ShapeDtypeStruct((Bl, S, D), BF16),
                   jax.ShapeDtypeStruct((Bl, HEADS, nc, M_DV, 128), F32),
                   jax.ShapeDtypeStruct((Bl, HEADS, nc, 1, 128), F32),
                   jax.ShapeDtypeStruct((Bl, HEADS, nc, 1, 128), F32)] + c_shape,
        scratch_shapes=[pltpu.VMEM((M_DV, 128), F32)] * HEADS + [pltpu.VMEM((1, 128), F32)] * (2 * HEADS) + c_scratch,
        compiler_params=_cp(2),
    )(qk3, qk3, proj3, proj3, proj3, gain, gbias, *gather)


def mlstm_cell_bwd(qk3, proj3, gain, gbias, dy3, states, name, exchange=()):
    Bl, S, _ = qk3.shape
    L = M_CHUNK
    SB = min(M_SLAB, S)
    cpb, nsb = SB // L, S // SB
    scale = M_DQK ** -0.5

    def body(q_ref, k_ref, v_ref, o_ref, g_ref, gain_ref, gb_ref, cst_ref, nst_ref, mst_ref, dy_ref,
             dq_ref, dk_ref, dv_ref, do_ref, dg_ref, dgain_ref, dgb_ref, *state):
        dC_s, dn_s, dgain_s, dgb_s = state[:HEADS], state[HEADS:2 * HEADS], state[2 * HEADS:3 * HEADS], state[-1]
        s = pl.program_id(1)

        @pl.when(s == 0)
        def _():
            for ref in state:
                ref[...] = jnp.zeros_like(ref)
        lane = lax.broadcasted_iota(jnp.int32, (L, 128), 1)
        rowi = lax.broadcasted_iota(jnp.int32, (L, 1), 0)
        eye, low, upp = _cell_consts(L)

        def bstep(t, carry):
            c = cpb - 1 - t
            off = pl.multiple_of(c * L, L)
            G = g_ref[0, pl.ds(off, L), :] + gb_ref[...]
            shared = dict(slab=jnp.zeros((L, 128), F32))

            def head(h):
                cols = pl.ds(M_DV * h, M_DV)
                gain_h = gain_ref[:, cols]
                C, n, m = cst_ref[0, h, c], nst_ref[0, h, c], mst_ref[0, h, c][:, 0:1]
                dC_n, dn_n = dC_s[h][...], dn_s[h][...]
                qmask, qc, kc, vc, gi, gf = _load_chunk(q_ref, k_ref, v_ref, G, off, L, h, lane)
                gi_row, bcum, bcum_row, b_last = _gate_rows(gi, gf, eye, low, upp)
                qs = qc * scale
                _, _, _, wa, sp, sl = _finish(_chunk_state(kc, vc, gi, bcum, b_last, C, n, m))
                dcl_b = (sl * dC_n).astype(BF16)
                t1_mm = _dot(vc.astype(BF16), dcl_b)
                dv_mm = _dot_nt(kc.astype(BF16), dcl_b)
                r = yield from _chunk_out(qs, kc, vc, gi_row, bcum, bcum_row, low, C, n, m)
                hc, den, dn, sc, dm, iw, qn = r["hc"], r["den"], r["dn"], r["sc"], r["dm"], r["iw"], r["qn"]
                qs_b, k_b, v_b = r["qs_b"], r["k_b"], r["v_b"]
                dy = dy_ref[0, pl.ds(off, L), cols].astype(F32)
                oc = o_ref[0, pl.ds(off, L), cols]
                sig_o = _sigmoid(oc)
                rr = lax.rsqrt(jnp.mean(hc * hc, axis=-1, keepdims=True) + RMS_EPS)
                hn = hc * rr
                dgain_s[h][...] += _sum0(dy * sig_o * hn)
                do_ref[0, pl.ds(off, L), cols] = (
                    dy * hn * gain_h * sig_o * (1.0 - sig_o)).astype(BF16)
                dhn = dy * sig_o * gain_h
                dhc = rr * dhn - hc * (rr * rr * rr) * jnp.mean(dhn * hc, axis=-1, keepdims=True)
                dnum = dhc / dn
                gden = -_sum1(dhc * hc) / dn
                dden = jnp.where(jnp.abs(den) > r["floor"], gden * jnp.sign(den), 0.0)
                dnum_b = dnum.astype(BF16)
                dqc_b = (iw * dnum).astype(BF16)
                dsc_mm = _dot_nt(dnum_b, v_b)
                dv = _dot_tn(sc.astype(BF16), dnum_b)
                dqs_mm = _dot(dqc_b, C.astype(BF16))
                dC_out = _dot_tn(dqc_b, qs_b)
                diw = _sum1(dnum * r["qc"]) + dden * qn
                wq = iw * dden
                dn_out = _sum0(wq * qs)
                dn_loc = sl * dn_n
                dsp = _sum1(_sum0(dC_n * C)) + _sum1(dn_n * n)
                yield
                dsc = dsc_mm + dden
                dS_b = (dsc * dm).astype(BF16)
                gm = dsc * sc
                dqs2_mm = _dot(dS_b, k_b)
                dk = _dot_tn(dS_b, qs_b)
                dqs = dqs_mm + wq * n
                dbc = _sum1(gm) + diw * iw
                colg = _sum0(gm)
                dC_p = sp * dC_n + dC_out
                dn_p = sp * dn_n + dn_out
                db_last = dsp * sp
                t1 = t1_mm + dn_loc
                dwa = _sum1(t1 * kc)
                dv = dv + wa * dv_mm
                yield
                dqs = dqs + dqs2_mm
                dk = dk + wa * t1
                da = dwa * wa
                db_last = db_last + _sum0(da)
                dbc = dbc - da + jnp.where(rowi == L - 1, db_last, 0.0)
                dbc_row = _sum0(jnp.where(eye, dbc, 0.0)) - colg
                dgi = da + _sum1(jnp.where(eye, colg, 0.0))
                dlf = _sum1(jnp.where(upp, dbc_row, 0.0))
                dgf = dlf * _sigmoid(-gf)
                dq = jnp.where(qmask, dqs * scale, 0.0)
                dk = jnp.where(qmask, dk, 0.0)
                shared["slab"] = (shared["slab"] + jnp.where(lane == h, dgi, 0.0)
                                  + jnp.where(lane == h + HEADS, dgf, 0.0))
                dv_ref[0, pl.ds(off, L), cols] = dv.astype(BF16)
                dC_s[h][...] = dC_p
                dn_s[h][...] = dn_p
                if h % 2 == 0:
                    shared["dq"], shared["dk"] = dq, dk
                else:
                    pair = pl.ds(128 * (h // 2), 128)
                    dq_ref[0, pl.ds(off, L), pair] = shared["dq"] + dq
                    dk_ref[0, pl.ds(off, L), pair] = shared["dk"] + dk

            _interleave(head(h) for h in range(HEADS))
            dg_ref[0, pl.ds(off, L), :] = shared["slab"]
            dgb_s[...] += _sum0(shared["slab"])
            return carry

        lax.fori_loop(0, cpb, bstep, 0)

        @pl.when(s == nsb - 1)
        def _():
            for h in range(HEADS):
                dgain_ref[0, :, pl.ds(M_DV * h, M_DV)] = dgain_s[h][...]
            dgb_ref[0] = dgb_s[...]

    ins, states_specs, seq = _cell_specs(SB, cpb, lambda s: nsb - 1 - s)
    once = lambda width: pl.BlockSpec((1, 1, width), lambda b, s: (b, 0, 0))
    grid = (Bl, nsb)
    body, c_in, c_out, c_shape, c_scratch = host_comm(body, grid, 11, 7, exchange=exchange)
    return pl.pallas_call(
        body, name=name, grid=grid, in_specs=ins + states_specs + [seq(D, 0)] + c_in,
        out_specs=[seq(D // 2, 0), seq(D // 2, 0), seq(D, 0), seq(D, 0), seq(128, 0), once(D), once(128)] + c_out,
        out_shape=[jax.ShapeDtypeStruct((Bl, S, D // 2), F32), jax.ShapeDtypeStruct((Bl, S, D // 2), F32),
                   jax.ShapeDtypeStruct((Bl, S, D), BF16), jax.ShapeDtypeStruct((Bl, S, D), BF16),
                   jax.ShapeDtypeStruct((Bl, S, 128), F32), jax.ShapeDtypeStruct((Bl, 1, D), F32),
                   jax.ShapeDtypeStruct((Bl, 1, 128), F32)] + c_shape,
        scratch_shapes=[pltpu.VMEM((M_DV, 128), F32)] * HEADS + [pltpu.VMEM((1, 128), F32)] * (2 * HEADS + 1)
        + c_scratch,
        compiler_params=_cp(2),
    )(qk3, qk3, proj3, proj3, proj3, gain, gbias, *states, dy3, *exchange)


def _attn_scores(q, kc, kp, n, row, col, scale):
    s_c = jnp.where(col <= row, _dot_nt(q, kc) * scale, NEG)
    s_p = jnp.where(jnp.logical_and(col >= row, n > 0), _dot_nt(q, kp) * scale, NEG)
    return s_c, s_p


def _to_streams(src, dst, tmp, dil, Sd):
    if dil == 1:
        dst[...] = src[...].astype(dst.dtype)
        return
    if src.dtype != F32:
        tmp[...] = src[...].astype(F32)
        src = tmp
    for r in range(dil):
        dst[pl.ds(r * Sd, Sd), :] = src[pl.ds(r, Sd, stride=dil), :].astype(dst.dtype)


def _from_streams(src, dst, dil, Sd):
    if dil == 1:
        dst[...] = src[...]
        return
    for r in range(dil):
        dst[pl.ds(r, Sd, stride=dil), :] = src[pl.ds(r * Sd, Sd), :]


def attn_fwd(proj, Bl, S, g, dil, name):
    Sd = S // dil
    nb = Sd // A_BLK
    scale = A_BLK ** -0.5
    pv = proj.reshape(Bl, S, A_PROJ)

    def body(q_ref, k_ref, v_ref, o_ref, l_ref, tmp, qs, ks, vs, os_, ls):
        row = lax.broadcasted_iota(jnp.int32, (A_BLK, A_BLK), 0)
        col = lax.broadcasted_iota(jnp.int32, (A_BLK, A_BLK), 1)
        for src, dst in ((q_ref, qs), (k_ref, ks), (v_ref, vs)):
            _to_streams(src.at[0], dst, tmp, dil, Sd)

        def step(i, carry):
            n = i % nb
            off = pl.multiple_of(i * A_BLK, A_BLK)
            offp = pl.multiple_of(jnp.maximum(i - 1, 0) * A_BLK, A_BLK)
            q = qs[pl.ds(off, A_BLK), :]
            s_c, s_p = _attn_scores(q, ks[pl.ds(off, A_BLK), :], ks[pl.ds(offp, A_BLK), :], n, row, col, scale)
            m = jnp.maximum(jnp.max(s_c, axis=1, keepdims=True), jnp.max(s_p, axis=1, keepdims=True))
            p_c = jnp.exp(s_c - m)
            p_p = jnp.exp(s_p - m)
            den = _sum1(p_c) + _sum1(p_p)
            o = _dot(p_c.astype(BF16), vs[pl.ds(off, A_BLK), :]) + _dot(p_p.astype(BF16), vs[pl.ds(offp, A_BLK), :])
            os_[pl.ds(off, A_BLK), :] = o / den
            ls[pl.ds(off, A_BLK), :] = jnp.broadcast_to(m + jnp.log(den), (A_BLK, 128))
            return carry

        lax.fori_loop(0, dil * nb, step, 0, unroll=A_UNROLL)
        _from_streams(os_, o_ref.at[0], dil, Sd)
        _from_streams(ls, l_ref.at[0], dil, Sd)

    def spec(j):
        return pl.BlockSpec((1, S, 128), lambda b, h: (b, 0, g * 24 + j * HEADS + h))

    ospec = pl.BlockSpec((1, S, 128), lambda b, h: (b, 0, h))
    o, lse = pl.pallas_call(
        body, name=name, grid=(Bl, HEADS),
        in_specs=[spec(0), spec(1), spec(2)], out_specs=[ospec, ospec],
        out_shape=[jax.ShapeDtypeStruct((Bl, S, D), F32)] * 2,
        scratch_shapes=[pltpu.VMEM((S, 128), F32)] + [pltpu.VMEM((S, 128), BF16)] * 3 + [pltpu.VMEM((S, 128), F32)] * 2,
        compiler_params=_cp(2),
    )(pv, pv, pv)
    return o.reshape(Bl * S, D), lse.reshape(Bl * S, D)


def attn_merge(os_, lses, name):
    T = os_[0].shape[0]
    tm = _tile(T, 512)
    ng = len(os_)

    def body(*refs):
        o_refs, l_refs = refs[:ng], refs[ng:2 * ng]
        ob_ref, of_ref, lt_ref = refs[2 * ng:]
        ls = [r[...] for r in l_refs]
        m = functools.reduce(jnp.maximum, ls)
        ws = [jnp.exp(l - m) for l in ls]
        den = functools.reduce(lambda a, b: a + b, ws)
        o = functools.reduce(lambda a, b: a + b, [w * r[...] for w, r in zip(ws, o_refs)]) / den
        of_ref[...] = o
        ob_ref[...] = o.astype(BF16)
        lt_ref[...] = m + jnp.log(den)

    spec = pl.BlockSpec((tm, D), lambda i: (i, 0))
    return pl.pallas_call(
        body, name=name, grid=(T // tm,), in_specs=[spec] * (2 * ng), out_specs=[spec] * 3,
        out_shape=[jax.ShapeDtypeStruct((T, D), BF16), jax.ShapeDtypeStruct((T, D), F32),
                   jax.ShapeDtypeStruct((T, D), F32)],
        compiler_params=_cp(1),
    )(*os_, *lses)


def attn_bwd(proj, do, o, lse, Bl, S, g, dil, name):
    Sd = S // dil
    nb = Sd // A_BLK
    scale = A_BLK ** -0.5
    pv = proj.reshape(Bl, S, A_PROJ)
    dov, ov, lv = (t.reshape(Bl, S, D) for t in (do, o, lse))

    def body(q_ref, k_ref, v_ref, do_ref, o_ref, l_ref, dq_ref, dk_ref, dv_ref,
             tmp, qs, ks, vs, dos, dls, lts, dq_s, dk_s, dv_s):
        row = lax.broadcasted_iota(jnp.int32, (A_BLK, A_BLK), 0)
        col = lax.broadcasted_iota(jnp.int32, (A_BLK, A_BLK), 1)
        for src, dst in ((q_ref, qs), (k_ref, ks), (v_ref, vs), (do_ref, dos), (l_ref, lts)):
            _to_streams(src.at[0], dst, tmp, dil, Sd)
        tmp[...] = jnp.broadcast_to(_sum1(do_ref[0].astype(F32) * o_ref[0]), (S, 128))
        _to_streams(tmp, dls, None, dil, Sd)
        dk_s[...] = jnp.zeros_like(dk_s)
        dv_s[...] = jnp.zeros_like(dv_s)

        def step(i, carry):
            n = i % nb
            off = pl.multiple_of(i * A_BLK, A_BLK)
            offp = pl.multiple_of(jnp.maximum(i - 1, 0) * A_BLK, A_BLK)
            q = qs[pl.ds(off, A_BLK), :]
            kc, kp = ks[pl.ds(off, A_BLK), :], ks[pl.ds(offp, A_BLK), :]
            vc, vp = vs[pl.ds(off, A_BLK), :], vs[pl.ds(offp, A_BLK), :]
            do_b = dos[pl.ds(off, A_BLK), :]
            delta = dls[pl.ds(off, A_BLK), :][:, 0:1]
            lt = lts[pl.ds(off, A_BLK), :][:, 0:1]
            s_c, s_p = _attn_scores(q, kc, kp, n, row, col, scale)
            p_c = jnp.exp(s_c - lt)
            p_p = jnp.exp(s_p - lt)
            ds_c = (p_c * (_dot_nt(do_b, vc) - delta) * scale).astype(BF16)
            ds_p = (p_p * (_dot_nt(do_b, vp) - delta) * scale).astype(BF16)
            dq_s[pl.ds(off, A_BLK), :] = _dot(ds_c, kc) + _dot(ds_p, kp)
            dk_s[pl.ds(off, A_BLK), :] += _dot_tn(ds_c, q)
            dk_s[pl.ds(offp, A_BLK), :] += _dot_tn(ds_p, q)
            dv_s[pl.ds(off, A_BLK), :] += _dot_tn(p_c.astype(BF16), do_b)
            dv_s[pl.ds(offp, A_BLK), :] += _dot_tn(p_p.astype(BF16), do_b)
            return carry

        lax.fori_loop(0, dil * nb, step, 0, unroll=A_UNROLL)
        for src, dst in ((dq_s, dq_ref), (dk_s, dk_ref), (dv_s, dv_ref)):
            _from_streams(src, tmp, dil, Sd)
            dst[0] = tmp[...].astype(BF16)

    def spec(j):
        return pl.BlockSpec((1, S, 128), lambda b, h: (b, 0, g * 24 + j * HEADS + h))

    ospec = pl.BlockSpec((1, S, 128), lambda b, h: (b, 0, h))
    slab = lambda dt: pltpu.VMEM((S, 128), dt)
    outs = pl.pallas_call(
        body, name=name, grid=(Bl, HEADS),
        in_specs=[spec(0), spec(1), spec(2), ospec, ospec, ospec], out_specs=[ospec] * 3,
        out_shape=[jax.ShapeDtypeStruct((Bl, S, D), BF16)] * 3,
        scratch_shapes=[slab(F32)] + [slab(BF16)] * 4 + [slab(F32)] * 5,
        compiler_params=_cp(2),
    )(pv, pv, pv, dov, ov, lv)
    return [t.reshape(Bl * S, D) for t in outs]


def _as_slots(pair, shape):
    return tuple(t.reshape(shape) for t in pair)


def ffn_fwd(x, mod3, w_in, w_out, lng, lnb, tag, gather=()):
    a, g, u, h, gathered = ffn_in(x, mod3, w_in, tag + "_in", gather=gather)
    out, xn = proj_post(a, w_out, x, mod3, lng, lnb, 0.5, tag + "_out")
    return xn, (x, out, g, u, h, a), gathered


def ffn_bwd(dxn, saved, mod3, w_in, w_out, lng, tag, exchange=(), exchange_own=False):
    x, out, g, u, h, a = saved
    dxres, dout, dgu, dlg, dlb, dgate, *received = post_bwd(dxn, x, out, mod3, lng, w_out, 0.5, tag + "_outb",
                                                            tm=256, gu=(g, u), exchange=exchange)
    dw_in = mm_tn(h, dgu, tag + "_dwin", a_copies=True)
    dw_out = _as_slots(mm_tn(a, dout, tag + "_dwout", bw=D), (N_DEV, D_FF // N_DEV, D))
    dx, dsh, dsc, *own = modmm_bwd(dgu, w_in, x, mod3, dxres, tag + "_inb", tm=256,
                                   exchange=[dw_in[1], dw_out[1]] if exchange_own else ())
    dmod3 = jnp.concatenate([dsh, dsc, dgate], axis=1)
    return dx, [dw_in, dw_out], dlg, dlb, dmod3, (received[0] if received else []), (own[0] if own else [])


def mlstm_fwd(x, mod3, w_in, w_out, conv_w, gain, gbias, lng, lnb, Bl, S, gather=()):
    proj, h = modmm(x, mod3, w_in, F32, "ml_in", tn=M_PROJ_PAD // 5)
    proj3 = proj.reshape(Bl, S, M_PROJ_PAD)
    qk3 = conv_silu(proj3, conv_w, "ml_conv")
    y3, *rest = mlstm_cell_fwd(qk3, proj3, gain, gbias, "ml_cell", gather=gather)
    states, gathered = rest[:3], rest[3:]
    y = y3.reshape(Bl * S, D)
    out, xn = proj_post(y[None], w_out, x, mod3, lng, lnb, 1.0, "ml_out")
    return xn, (x, out, h, proj3, qk3, y, states), gathered


def mlstm_bwd(dxn, saved, mod3, w_in, w_out, conv_w, gain, gbias, lng, Bl, S, exchange=()):
    x, out, h, proj3, qk3, y, states = saved
    dxres, dout, dy, dlg, dlb, dgate = post_bwd(dxn, x, out, mod3, lng, w_out, 1.0, "ml_outb")
    dq, dk, dv, do, dg, dgain, dgb, *received = mlstm_cell_bwd(qk3, proj3, gain, gbias, dy.reshape(Bl, S, D),
                                                               states, "ml_cellb", exchange=exchange)
    dqk, dconv = conv_silu_bwd(proj3, conv_w, dq, dk, "ml_convb")
    dproj = jnp.concatenate([dqk, dv, do, dg.astype(BF16)], axis=2).reshape(Bl * S, M_PROJ_PAD)
    dx, dsh, dsc = modmm_bwd(dproj, w_in, x, mod3, dxres, "ml_inb", tn=M_PROJ_PAD // 5)
    dwi, _ = mm_tn(h, dproj, "ml_dwin", bw=M_PROJ_PAD // 5)
    dwi = _restack(jnp.moveaxis(dwi, 0, 1).reshape(D, M_PROJ_PAD)[:, :M_PROJ], 1)
    dw_out = _as_slots(mm_tn(y, dout, "ml_dwout", bw=D), (N_DEV, D // N_DEV, D))
    small = (jnp.sum(dconv, axis=0), jnp.sum(dgain, axis=0), jnp.sum(dgb, axis=0)[:, :2 * HEADS])
    dmod3 = jnp.concatenate([dsh, dsc, dgate], axis=1)
    return dx, [(dwi, dwi.astype(BF16)), dw_out], dlg, dlb, dmod3, small, received


def attn_mixer_fwd(x, mod3, w_in, w_out, lng, lnb, Bl, S):
    proj, h = modmm(x, mod3, w_in, BF16, "at_in")
    os_, lses = [], []
    for g, (_, dil) in enumerate(DIL_GROUPS):
        o_g, l_g = attn_fwd(proj, Bl, S, g, dil, "at_core%d" % g)
        os_.append(o_g)
        lses.append(l_g)
    ob, of, lt = attn_merge(os_, lses, "at_merge")
    out, xn = proj_post(ob[None], w_out, x, mod3, lng, lnb, 1.0, "at_out")
    return xn, (x, out, h, proj, ob, of, lt)


def attn_mixer_bwd(dxn, saved, mod3, w_in, w_out, lng, Bl, S):
    x, out, h, proj, ob, of, lt = saved
    dxres, dout, do, dlg, dlb, dgate = post_bwd(dxn, x, out, mod3, lng, w_out, 1.0, "at_outb")
    do = do[0]
    parts = []
    for g, (_, dil) in enumerate(DIL_GROUPS):
        parts += attn_bwd(proj, do, of, lt, Bl, S, g, dil, "at_coreb%d" % g)
    dproj = jnp.concatenate(parts, axis=1)
    dx, dsh, dsc = modmm_bwd(dproj, w_in, x, mod3, dxres, "at_inb")
    dw_in = mm_tn(h, dproj, "at_dwin", bw=w_in.shape[2])
    dw_out = _as_slots(mm_tn(ob, dout, "at_dwout", bw=D), (N_DEV, D // N_DEV, D))
    return dx, [dw_in, dw_out], dlg, dlb, jnp.concatenate([dsh, dsc, dgate], axis=1)


def _unstack(stacked, axis):
    full = jnp.moveaxis(stacked, 0, axis)
    shp = list(full.shape)
    shp[axis:axis + 2] = [shp[axis] * shp[axis + 1]]
    return full.reshape(shp)


def _restack(full, axis):
    shp = list(full.shape)
    shp[axis:axis + 1] = [N_DEV, shp[axis] // N_DEV]
    return jnp.moveaxis(full.reshape(shp), axis, 0)


def kernel(x, c, ada_w, ada_b, ln_g, ln_b, ffn_w_in, ffn_w_out, mlstm_w_in, mlstm_gate_bias, mlstm_conv_w, mlstm_head_gain, mlstm_w_out, attn_w_in, attn_w_out, loss_target, m_ada_w, m_ada_b, m_ln_g, m_ln_b, m_ffn_w_in, m_ffn_w_out, m_mlstm_w_in, m_mlstm_gate_bias, m_mlstm_conv_w, m_mlstm_head_gain, m_mlstm_w_out, m_attn_w_in, m_attn_w_out, v_ada_w, v_ada_b, v_ln_g, v_ln_b, v_ffn_w_in, v_ffn_w_out, v_mlstm_w_in, v_mlstm_gate_bias, v_mlstm_conv_w, v_mlstm_head_gain, v_mlstm_w_out, v_attn_w_in, v_attn_w_out):
    Bl, S, _ = x.shape
    T = Bl * S
    Bg = Bl * N_DEV
    me = 4 * lax.axis_index("x") + 2 * lax.axis_index("y") + lax.axis_index("c")
    onehot = (jnp.arange(N_DEV) == me).astype(F32)
    weights = dict(ada_w=ada_w, ada_b=ada_b, ln_g=ln_g, ln_b=ln_b, ffn_w_in=ffn_w_in, ffn_w_out=ffn_w_out,
                   mlstm_w_in=mlstm_w_in, mlstm_gate_bias=mlstm_gate_bias, mlstm_conv_w=mlstm_conv_w,
                   mlstm_head_gain=mlstm_head_gain, mlstm_w_out=mlstm_w_out, attn_w_in=attn_w_in,
                   attn_w_out=attn_w_out)
    m_in = dict(ada_w=m_ada_w, ada_b=m_ada_b, ln_g=m_ln_g, ln_b=m_ln_b, ffn_w_in=m_ffn_w_in,
                ffn_w_out=m_ffn_w_out, mlstm_w_in=m_mlstm_w_in, mlstm_gate_bias=m_mlstm_gate_bias,
                mlstm_conv_w=m_mlstm_conv_w, mlstm_head_gain=m_mlstm_head_gain, mlstm_w_out=m_mlstm_w_out,
                attn_w_in=m_attn_w_in, attn_w_out=m_attn_w_out)
    v_in = dict(ada_w=v_ada_w, ada_b=v_ada_b, ln_g=v_ln_g, ln_b=v_ln_b, ffn_w_in=v_ffn_w_in,
                ffn_w_out=v_ffn_w_out, mlstm_w_in=v_mlstm_w_in, mlstm_gate_bias=v_mlstm_gate_bias,
                mlstm_conv_w=v_mlstm_conv_w, mlstm_head_gain=v_mlstm_head_gain, mlstm_w_out=v_mlstm_w_out,
                attn_w_in=v_attn_w_in, attn_w_out=v_attn_w_out)

    mixer = ("mlstm", "attn")
    shards = [[ffn_w_in[layer, 0], ffn_w_in[layer, 1], ffn_w_out[layer, 0], ffn_w_out[layer, 1],
               weights[mixer[layer] + "_w_in"][0], weights[mixer[layer] + "_w_out"][0]] for layer in range(DEPTH)]
    sends = [[s.astype(BF16) for s in layer_shards] for layer_shards in shards]
    small = jnp.concatenate([c.reshape(-1), ln_g.reshape(-1), ln_b.reshape(-1), mlstm_conv_w.reshape(-1)])
    n_small = small.shape[0]
    small = jnp.pad(small, (0, -n_small % (8 * PACK_COLS))).reshape(-1, PACK_COLS)

    def gathered_weights(g):
        return ((g[0], g[1]), (g[2].reshape(4, D_FF // 4, D), g[3].reshape(4, D_FF // 4, D)), g[4],
                g[5].reshape(1, D, D))

    first_in, small_all = all_gather([sends[0][0], small], "ag_params")
    full = [None, None]
    small_flat = small_all.reshape(N_DEV, -1)
    o0 = 0
    c_all = small_flat[:, o0:o0 + c.size].reshape(Bg, D)
    o0 += c.size
    lng_full = _unstack(small_flat[:, o0:o0 + ln_g.size].reshape((N_DEV,) + ln_g.shape), 2)
    o0 += ln_g.size
    lnb_full = _unstack(small_flat[:, o0:o0 + ln_b.size].reshape((N_DEV,) + ln_b.shape), 2)
    o0 += ln_b.size
    conv_full = _unstack(small_flat[:, o0:o0 + mlstm_conv_w.size].reshape((N_DEV,) + mlstm_conv_w.shape), 2)[0]
    gbias =jnp.pad(mlstm_gate_bias, ((0, 0), (0, 128 - 2 * HEADS)))

    ncols = ada_w.shape[2]
    ada_b_cols = lax.dynamic_slice_in_dim(ada_b, me * ncols, ncols, axis=1).reshape(DEPTH, 1, ncols)
    mod_cols = ada_fwd(c_all, ada_w, ada_b_cols, "ada_fwd")
    (mod_g,) = all_gather([mod_cols.reshape(DEPTH * Bg, ncols)], "ag_mod")
    mod_full = _unstack(mod_g.reshape(N_DEV, DEPTH, Bg, ncols), 2)
    mod_mine = lax.dynamic_slice_in_dim(mod_full, me * Bl, Bl, axis=1).reshape(DEPTH, Bl, 3, 3, D)

    xt = x.reshape(T, D)
    saved = []
    for layer in range(DEPTH):
        def lnp(s, layer=layer):
            return lng_full[layer, s].reshape(1, D), lnb_full[layer, s].reshape(1, D)
        md = mod_mine[layer]
        if layer == 0:
            a, g, u, h, late = ffn_in(xt, md[:, 0], first_in, "f0a_in", gather=sends[0][1:])
            full[0] = gathered_weights([first_in] + late)
            out, xn = proj_post(a, full[0][1][0], xt, md[:, 0], *lnp(0), 0.5, "f0a_out")
            xt, sv0 = xn, (xt, out, g, u, h, a)
            mw_in = jnp.pad(_unstack(full[0][2], 1), ((0, 0), (0, M_PROJ_PAD - M_PROJ)))
        else:
            xt, sv0, _ = ffn_fwd(xt, md[:, 0], full[layer][0][0], full[layer][1][0], *lnp(0), "f%da" % layer)
        f_in, f_out, mix_in, mix_out = full[layer]
        if layer % 2 == 0:
            xt, sv1, g1 = mlstm_fwd(xt, md[:, 1], mw_in, mix_out, conv_full, mlstm_head_gain, gbias, *lnp(1), Bl, S,
                                    gather=sends[1])
            full[1] = gathered_weights(g1)
        else:
            xt, sv1 = attn_mixer_fwd(xt, md[:, 1], mix_in, mix_out, *lnp(1), Bl, S)
        xt, sv2, _ = ffn_fwd(xt, md[:, 2], f_in[1], f_out[1], *lnp(2), "f%db" % layer)
        saved.append((sv0, sv1, sv2))

    dxt, lsum = loss_head(xt, loss_target.reshape(T, D), "loss")
    loss = lax.psum(lsum[0, 0], MESH_AXES)

    dmod, dlg_all, dlb_all = [None] * DEPTH, [None] * DEPTH, [None] * DEPTH
    wgrads = [None] * DEPTH
    recvs = [[None] * 6 for _ in range(DEPTH)]
    ml_small = None
    for layer in reversed(range(DEPTH)):
        md = mod_mine[layer]
        f_in, f_out, mix_in, mix_out = full[layer]
        sv0, sv1, sv2 = saved[layer]
        dxt, dw2, dlg2, dlb2, dm2, _, _ = ffn_bwd(dxt, sv2, md[:, 2], f_in[1], f_out[1],
                                                  lng_full[layer, 2].reshape(1, D), "f%db" % layer)
        lg1 = lng_full[layer, 1].reshape(1, D)
        if layer % 2 == 0:
            dxt, dw1, dlg1, dlb1, dm1, ml_small, got = mlstm_bwd(
                dxt, sv1, md[:, 1], mw_in, mix_out, conv_full, mlstm_head_gain, gbias, lg1, Bl, S,
                exchange=[b16 for _, b16 in wgrads[1]] + [dw2[0][1], dw2[1][1]])
            recvs[1], recvs[0][1], recvs[0][3] = got[:6], got[6], got[7]
            dxt, dw0, dlg0, dlb0, dm0, got, own = ffn_bwd(dxt, sv0, md[:, 0], f_in[0], f_out[0],
                                                          lng_full[layer, 0].reshape(1, D), "f%da" % layer,
                                                          exchange=[dw1[0][1], dw1[1][1]], exchange_own=True)
            (recvs[0][4], recvs[0][5]), (recvs[0][0], recvs[0][2]) = got, own
        else:
            dxt, dw1, dlg1, dlb1, dm1 = attn_mixer_bwd(dxt, sv1, md[:, 1], mix_in, mix_out, lg1, Bl, S)
            dxt, dw0, dlg0, dlb0, dm0, _, _ = ffn_bwd(dxt, sv0, md[:, 0], f_in[0], f_out[0],
                                                      lng_full[layer, 0].reshape(1, D), "f%da" % layer)
        wgrads[layer] = [dw0[0], dw2[0], dw0[1], dw2[1], dw1[0], dw1[1]]
        dmod[layer] = jnp.stack([dm0, dm1, dm2], axis=1).reshape(Bl, 9 * D)
        dlg_all[layer] = jnp.concatenate([dlg0, dlg1, dlg2], axis=0)
        dlb_all[layer] = jnp.concatenate([dlb0, dlb1, dlb2], axis=0)
    grad_x = dxt.reshape(Bl, S, D)

    gsh = [[shard_sum(lax.dynamic_index_in_dim(f32, me, axis=0, keepdims=False), recv, onehot,
                      "rs_sum%d_%d" % (layer, i))
            for i, ((f32, _), recv) in enumerate(zip(wgrads[layer], recvs[layer]))] for layer in range(DEPTH)]
    grads = {"ffn_w_in": jnp.stack([jnp.stack(g[0:2]) for g in gsh]),
             "ffn_w_out": jnp.stack([jnp.stack(g[2:4]) for g in gsh]),
             "mlstm_w_in": gsh[0][4][None], "mlstm_w_out": gsh[0][5][None],
             "attn_w_in": gsh[1][4][None], "attn_w_out": gsh[1][5][None]}

    dconv, dgain, dgbias = ml_small
    parts = [jnp.stack(dmod).reshape(-1), dgbias.reshape(-1), dgain.reshape(-1),
             jnp.stack(dlg_all).reshape(-1), jnp.stack(dlb_all).reshape(-1), dconv.reshape(-1)]
    sizes = [p.shape[0] for p in parts]
    flat = jnp.concatenate(parts)
    flat = jnp.pad(flat, (0, -flat.shape[0] % (8 * PACK_COLS))).reshape(-1, PACK_COLS)
    (sm_all,) = all_gather([flat], "ag_small")
    sm_sum = sum_leading(sm_all, "small_sum").reshape(-1)
    dmod_all = sm_all.reshape(N_DEV, -1)[:, :sizes[0]].reshape(N_DEV, DEPTH, Bl, 9 * D)
    dmod_all = jnp.moveaxis(dmod_all, 0, 1).reshape(DEPTH, Bg, 9 * D)
    o0 = sizes[0]
    grads["mlstm_gate_bias"] = sm_sum[o0:o0 + sizes[1]].reshape(mlstm_gate_bias.shape)
    o0 += sizes[1]
    grads["mlstm_head_gain"] = sm_sum[o0:o0 + sizes[2]].reshape(mlstm_head_gain.shape)
    o0 += sizes[2]
    nl = ln_g.shape[2]
    g_lng = sm_sum[o0:o0 + sizes[3]].reshape(DEPTH, 3, D)
    o0 += sizes[3]
    g_lnb = sm_sum[o0:o0 + sizes[4]].reshape(DEPTH, 3, D)
    o0 += sizes[4]
    g_conv = sm_sum[o0:o0 + sizes[5]].reshape(1, 4, D)
    grads["ln_g"] = lax.dynamic_slice_in_dim(g_lng, me * nl, nl, axis=2)
    grads["ln_b"] = lax.dynamic_slice_in_dim(g_lnb, me * nl, nl, axis=2)
    grads["mlstm_conv_w"] = lax.dynamic_slice_in_dim(g_conv, me * nl, nl, axis=2)
    dmod_cols = lax.dynamic_slice_in_dim(dmod_all, me * ncols, ncols, axis=2)
    gw, gb = ada_bwd(c_all.T, dmod_cols, dmod_all, "ada_bwd")
    grads["ada_w"] = gw
    grads["ada_b"] = gb.reshape(ada_b.shape)

    names = ["ada_w", "ada_b", "ln_g", "ln_b", "ffn_w_in", "ffn_w_out", "mlstm_w_in", "mlstm_gate_bias",
             "mlstm_conv_w", "mlstm_head_gain", "mlstm_w_out", "attn_w_in", "attn_w_out"]
    deltas, new_m, new_v = [], [], []
    for k in names:
        w = weights[k]
        shp2 = (math.prod(w.shape[:-1]), w.shape[-1])
        d_, m_, v_ = adamw(w.reshape(shp2), grads[k].reshape(shp2), m_in[k].reshape(shp2), v_in[k].reshape(shp2),
                           "adamw_" + k)
        deltas.append(d_.reshape(w.shape))
        new_m.append(m_.reshape(w.shape))
        new_v.append(v_.reshape(w.shape))
    return (loss, grad_x, *[grads[k] for k in names], *deltas, *new_m, *new_v)
```

```python
import functools
import math

import jax
import jax.numpy as jnp
from jax import lax
from jax.experimental import pallas as pl
from jax.experimental.pallas import tpu as pltpu

F32 = jnp.float32
BF16 = jnp.bfloat16

N_DEV = 8
MESH_AXES = ("x", "y", "c")
D = 1024
DEPTH = 2
D_FF = 2816
HEADS = 8
M_DQK = 64
M_DV = 128
M_CHUNK = 64
M_SLAB = 512
M_PROJ = 3088
M_PROJ_PAD = 3200
A_PROJ = 9216
DIL_GROUPS = ((128, 1), (512, 4), (2048, 16))
A_BLK = 128
A_UNROLL = 4
ALPHA = (2 * DEPTH) ** 0.25
LN_EPS = 1e-5
RMS_EPS = 1e-6
ADAM_LR = 0.001
ADAM_B1 = 0.9
ADAM_B2 = 0.999
ADAM_EPS = 1e-08
ADAM_WD = 0.01
ADAM_STEP = 10
NEG = -1e30
V7X_VMEM_LIMIT = 56 * 1024 * 1024
PACK_COLS = 1024
MESH_ID = pl.DeviceIdType.MESH
ANY_SPEC = pl.BlockSpec(memory_space=pl.ANY)


def _cp(n_axes):
    return pltpu.CompilerParams(dimension_semantics=("arbitrary",) * n_axes,
                                vmem_limit_bytes=V7X_VMEM_LIMIT)


def _dot(a, b):
    return jnp.dot(a, b, preferred_element_type=F32)


def _dot_nt(a, b):
    return lax.dot_general(a, b, (((1,), (1,)), ((), ())), preferred_element_type=F32)


def _dot_tn(a, b):
    return lax.dot_general(a, b, (((0,), (0,)), ((), ())), preferred_element_type=F32)


def _sum0(a):
    return jnp.sum(a, axis=0, keepdims=True)


def _sum1(a):
    return jnp.sum(a, axis=1, keepdims=True)


def _round(a):
    return a.astype(BF16).astype(F32)


def _sigmoid(a):
    return 1.0 / (1.0 + jnp.exp(-a))


def _tile(n, pref):
    t = min(n, pref)
    while n % t:
        t //= 2
    return t


def all_gather(arrs, name):
    n = len(arrs)

    def body(*refs):
        gather = Gather(refs[:n], refs[n:2 * n], *refs[2 * n:])
        gather.start()
        gather.finish()

    return pl.pallas_call(
        body, name=name, out_shape=Gather.out_shape(arrs),
        in_specs=[ANY_SPEC] * n, out_specs=[ANY_SPEC] * n, scratch_shapes=Gather.scratch(n),
    )(*arrs)


class Gather:
    def __init__(self, ins, outs, send_sems, recv_sems, local_sems):
        x, y, c = lax.axis_index("x"), lax.axis_index("y"), lax.axis_index("c")
        me, sibling = (x, y, c), (x, y, 1 - c)
        chips = [(1 - x, y), (x, 1 - y), (1 - x, 1 - y)]

        def slot(a, p):
            return outs[a].at[4 * p[0] + 2 * p[1] + p[2]]

        def copy(a, k, block, to, src=None):
            return pltpu.make_async_remote_copy(
                src_ref=slot(a, block) if src is None else src, dst_ref=slot(a, block),
                send_sem=send_sems.at[7 * a + k], recv_sem=recv_sems.at[7 * a + k],
                device_id=to, device_id_type=MESH_ID)

        n = len(ins)
        self.mine = [pltpu.make_async_copy(ins[a], slot(a, me), local_sems.at[a]) for a in range(n)]
        self.first, self.over_ici, self.passed, self.from_sibling = [], [], [], []
        for a in range(n):
            self.first.append(copy(a, 0, me, sibling, src=ins[a]))
            self.from_sibling.append(copy(a, 0, sibling, me))
            for j, chip in enumerate(chips):
                self.first.append(copy(a, 1 + j, me, (*chip, c), src=ins[a]))
                self.over_ici.append(copy(a, 1 + j, (*chip, c), me))
                self.passed.append(copy(a, 4 + j, (*chip, c), sibling))
                self.from_sibling.append(copy(a, 4 + j, (*chip, 1 - c), me))

    @staticmethod
    def out_shape(arrs):
        return [jax.ShapeDtypeStruct((N_DEV,) + a.shape, a.dtype) for a in arrs]

    @staticmethod
    def scratch(n):
        return [pltpu.SemaphoreType.DMA((7 * n,)), pltpu.SemaphoreType.DMA((7 * n,)),
                pltpu.SemaphoreType.DMA((n,))]

    def start(self):
        for cp in self.mine + self.first:
            cp.start()

    def finish(self):
        for landed, onward in zip(self.over_ici, self.passed):
            landed.wait_recv()
            onward.start()
        for cp in self.from_sibling:
            cp.wait_recv()
        for cp in self.first + self.passed:
            cp.wait_send()
        for cp in self.mine:
            cp.wait()


class Exchange:
    def __init__(self, sends, recvs, send_sems, recv_sems, local_sems):
        x, y, c = lax.axis_index("x"), lax.axis_index("y"), lax.axis_index("c")
        me = 4 * x + 2 * y + c
        self.own = [pltpu.make_async_copy(s.at[me], r.at[me], local_sems.at[a])
                    for a, (s, r) in enumerate(zip(sends, recvs))]
        self.copies = []
        for a, (s_ref, r_ref) in enumerate(zip(sends, recvs)):
            for k in range(1, N_DEV):
                px = 1 - x if (k >> 2) & 1 else x
                py = 1 - y if (k >> 1) & 1 else y
                pc = 1 - c if k & 1 else c
                self.copies.append(pltpu.make_async_remote_copy(
                    src_ref=s_ref.at[4 * px + 2 * py + pc], dst_ref=r_ref.at[me],
                    send_sem=send_sems.at[7 * a + k - 1], recv_sem=recv_sems.at[7 * a + k - 1],
                    device_id=(px, py, pc), device_id_type=MESH_ID))

    @staticmethod
    def scratch(n):
        return [pltpu.SemaphoreType.DMA((7 * n,)), pltpu.SemaphoreType.DMA((7 * n,)),
                pltpu.SemaphoreType.DMA((n,))]

    def start(self):
        for cp in self.own + self.copies:
            cp.start()

    def finish(self):
        for cp in self.copies:
            cp.wait_send()
            cp.wait_recv()
        for cp in self.own:
            cp.wait()


def host_comm(body, grid, n_in, n_out, gather=(), exchange=()):
    ng, nx = len(gather), len(exchange)
    if ng + nx == 0:
        return body, [], [], [], []

    def hosted(*refs):
        ins, c_in, rest = refs[:n_in], refs[n_in:n_in + ng + nx], refs[n_in + ng + nx:]
        outs, c_out, rest = rest[:n_out], rest[n_out:n_out + ng + nx], rest[n_out + ng + nx:]
        n_sems = 3 * ((ng > 0) + (nx > 0))
        scratch, sems = rest[:len(rest) - n_sems], rest[len(rest) - n_sems:]

        def comms():
            made = [Gather(c_in[:ng], c_out[:ng], *sems[:3])] if ng else []
            return made + ([Exchange(c_in[ng:], c_out[ng:], *sems[-3:])] if nx else [])

        ids = [pl.program_id(a) for a in range(len(grid))]

        @pl.when(functools.reduce(jnp.logical_and, [i == 0 for i in ids]))
        def _():
            for cm in comms():
                cm.start()
        body(*ins, *outs, *scratch)

        @pl.when(functools.reduce(jnp.logical_and, [i == g - 1 for i, g in zip(ids, grid)]))
        def _():
            for cm in comms():
                cm.finish()

    shapes = Gather.out_shape(gather) + [jax.ShapeDtypeStruct(a.shape, a.dtype) for a in exchange]
    scratch = (Gather.scratch(ng) if ng else []) + (Exchange.scratch(nx) if nx else [])
    return hosted, [ANY_SPEC] * (ng + nx), [ANY_SPEC] * (ng + nx), shapes, scratch


def shard_sum(own, recv, onehot, name):
    R, C = own.shape
    tr = _tile(R, 512)

    def body(oh_ref, own_ref, recv_ref, o_ref):
        acc = None
        for j in range(N_DEV):
            term = jnp.where(oh_ref[j] > 0.5, own_ref[...], recv_ref[j].astype(F32))
            acc = term if acc is None else acc + term
        o_ref[...] = acc

    return pl.pallas_call(
        body, name=name, grid=(R // tr,),
        in_specs=[pl.BlockSpec(memory_space=pltpu.SMEM),
                  pl.BlockSpec((tr, C), lambda i: (i, 0)),
                  pl.BlockSpec((N_DEV, tr, C), lambda i: (0, i, 0))],
        out_specs=pl.BlockSpec((tr, C), lambda i: (i, 0)),
        out_shape=jax.ShapeDtypeStruct((R, C), F32), compiler_params=_cp(1),
    )(onehot, own, recv)


def sum_leading(a, name):
    _, R, C = a.shape

    def body(a_ref, o_ref):
        acc = a_ref[0]
        for j in range(1, N_DEV):
            acc = acc + a_ref[j]
        o_ref[...] = acc

    return pl.pallas_call(body, name=name, out_shape=jax.ShapeDtypeStruct((R, C), F32),
                          compiler_params=_cp(0))(a)


def _col_chunks(w, tn):
    if w.ndim == 3:
        return w.shape[0], w.shape[2], pl.BlockSpec((None, w.shape[1], w.shape[2]), lambda i, j: (j, 0, 0))
    return w.shape[1] // tn, tn, pl.BlockSpec((w.shape[0], tn), lambda i, j: (0, j))


def modmm(x, mod3, w, out_dtype, name, tn=None):
    T, Dm = x.shape
    nj, tn, w_spec = _col_chunks(w, tn)
    N = nj * tn
    Bl = mod3.shape[0]
    tm = _tile(T // Bl, 1024)
    tpb = T // Bl // tm

    def body(x_ref, mod_ref, w_ref, o_ref, h_ref, hs):
        @pl.when(pl.program_id(1) == 0)
        def _():
            m = mod_ref[0]
            hs[...] = (x_ref[...] * (1.0 + m[1:2, :]) + m[0:1, :]).astype(BF16)
            h_ref[...] = hs[...]
        o_ref[...] = _dot(hs[...], w_ref[...]).astype(o_ref.dtype)

    return pl.pallas_call(
        body, name=name, grid=(T // tm, nj),
        in_specs=[pl.BlockSpec((tm, Dm), lambda i, j: (i, 0)),
                  pl.BlockSpec((1, 3, Dm), lambda i, j: (i // tpb, 0, 0)), w_spec],
        out_specs=[pl.BlockSpec((tm, tn), lambda i, j: (i, j)),
                   pl.BlockSpec((tm, Dm), lambda i, j: (i, 0))],
        out_shape=[jax.ShapeDtypeStruct((T, N), out_dtype), jax.ShapeDtypeStruct((T, Dm), BF16)],
        scratch_shapes=[pltpu.VMEM((tm, Dm), BF16)], compiler_params=_cp(2),
    )(x, mod3, w)


def modmm_bwd(dp, w, x, mod3, dxres, name, tn=None, tm=512, exchange=()):
    T, Dm = x.shape
    Bl = mod3.shape[0]
    tm = _tile(T // Bl, tm)
    tpb = T // Bl // tm
    resident = dp.ndim == 3
    if resident:
        nc, nj = dp.shape[0], 1
        dp_spec = pl.BlockSpec((nc, tm, dp.shape[2]), lambda i, j: (0, i, 0))
        w_spec = pl.BlockSpec(w.shape, lambda i, j: (0, 0, 0))
    else:
        nj, tn, w_spec = _col_chunks(w, tn)
        dp_spec = pl.BlockSpec((tm, tn), lambda i, j: (i, j))

    def body(dp_ref, w_ref, x_ref, mod_ref, dxr_ref, dx_ref, dsh_ref, dsc_ref, acc):
        i, j = pl.program_id(0), pl.program_id(1)

        @pl.when(j == 0)
        def _():
            acc[...] = jnp.zeros_like(acc)
        if resident:
            for c in range(nc):
                acc[...] += _dot_nt(dp_ref[c], w_ref[c])
        else:
            acc[...] += _dot_nt(dp_ref[...], w_ref[...])

        @pl.when(j == nj - 1)
        def _():
            dh = acc[...]
            xx = x_ref[...]
            dx_ref[...] = dxr_ref[...] + dh * (1.0 + mod_ref[0][1:2, :])

            @pl.when(i % tpb == 0)
            def _():
                dsh_ref[...] = jnp.zeros_like(dsh_ref)
                dsc_ref[...] = jnp.zeros_like(dsc_ref)
            dsh_ref[0] += _sum0(dh)
            dsc_ref[0] += _sum0(dh * xx)

    grid = (T // tm, nj)
    body, c_in, c_out, c_shape, c_scratch = host_comm(body, grid, 5, 3, exchange=exchange)
    dx, dsh, dsc, *received = pl.pallas_call(
        body, name=name, grid=grid,
        in_specs=[dp_spec, w_spec,
                  pl.BlockSpec((tm, Dm), lambda i, j: (i, 0)),
                  pl.BlockSpec((1, 3, Dm), lambda i, j: (i // tpb, 0, 0)),
                  pl.BlockSpec((tm, Dm), lambda i, j: (i, 0))] + c_in,
        out_specs=[pl.BlockSpec((tm, Dm), lambda i, j: (i, 0)),
                   pl.BlockSpec((1, 1, Dm), lambda i, j: (i // tpb, 0, 0)),
                   pl.BlockSpec((1, 1, Dm), lambda i, j: (i // tpb, 0, 0))] + c_out,
        out_shape=[jax.ShapeDtypeStruct((T, Dm), F32), jax.ShapeDtypeStruct((Bl, 1, Dm), F32),
                   jax.ShapeDtypeStruct((Bl, 1, Dm), F32)] + c_shape,
        scratch_shapes=[pltpu.VMEM((tm, Dm), F32)] + c_scratch, compiler_params=_cp(2),
    )(dp, w, x, mod3, dxres, *exchange)
    return (dx, dsh, dsc, received) if exchange else (dx, dsh, dsc)


def _ln_stats(z):
    mu = jnp.mean(z, axis=-1, keepdims=True)
    zc = z - mu
    var = jnp.mean(zc * zc, axis=-1, keepdims=True)
    rstd = lax.rsqrt(var + LN_EPS)
    return zc * rstd, rstd


def proj_post(a, w, x, mod3, lng, lnb, weight, name):
    nk, T, tk = a.shape
    Dm = w.shape[2]
    Bl = mod3.shape[0]
    tm = _tile(T // Bl, 512)
    tpb = T // Bl // tm

    def body(a_ref, w_ref, x_ref, mod_ref, g_ref, b_ref, out_ref, xn_ref):
        out = _dot(a_ref[0], w_ref[0])
        for k in range(1, nk):
            out = out + _dot(a_ref[k], w_ref[k])
        out_ref[...] = out
        z = ALPHA * x_ref[...] + (weight * (1.0 + mod_ref[0][2:3, :])) * out
        xhat, _ = _ln_stats(z)
        xn_ref[...] = xhat * g_ref[...] + b_ref[...]

    row = pl.BlockSpec((tm, Dm), lambda i: (i, 0))
    vec = pl.BlockSpec((1, Dm), lambda i: (0, 0))
    return pl.pallas_call(
        body, name=name, grid=(T // tm,),
        in_specs=[pl.BlockSpec((nk, tm, tk), lambda i: (0, i, 0)),
                  pl.BlockSpec((nk, tk, Dm), lambda i: (0, 0, 0)),
                  row, pl.BlockSpec((1, 3, Dm), lambda i: (i // tpb, 0, 0)), vec, vec],
        out_specs=[row, row],
        out_shape=[jax.ShapeDtypeStruct((T, Dm), F32), jax.ShapeDtypeStruct((T, Dm), F32)],
        compiler_params=_cp(1),
    )(a, w, x, mod3, lng, lnb)


def post_bwd(dxn, x, out, mod3, lng, w, weight, name, tm=512, gu=None, exchange=()):
    T, Dm = x.shape
    nk, tk, _ = w.shape
    Bl = mod3.shape[0]
    tm = _tile(T // Bl, tm)
    tpb = T // Bl // tm
    fused = gu is not None

    def body(dxn_ref, x_ref, out_ref, mod_ref, g_ref, w_ref, *rest):
        if fused:
            gg_ref, uu_ref = rest[:2]
            rest = rest[2:]
        dxr_ref, dout_ref, da_ref, dg_ref, db_ref, dgate_ref = rest
        i = pl.program_id(0)
        out = out_ref[...]
        dxn = dxn_ref[...]
        coef = weight * (1.0 + mod_ref[0][2:3, :])
        xhat, rstd = _ln_stats(ALPHA * x_ref[...] + coef * out)
        dyh = dxn * g_ref[...]
        dz = rstd * (dyh - jnp.mean(dyh, axis=-1, keepdims=True)
                     - xhat * jnp.mean(dyh * xhat, axis=-1, keepdims=True))
        dxr_ref[...] = ALPHA * dz
        dout = (coef * dz).astype(BF16)
        dout_ref[...] = dout

        @pl.when(i == 0)
        def _():
            dg_ref[...] = jnp.zeros_like(dg_ref)
            db_ref[...] = jnp.zeros_like(db_ref)

        @pl.when(i % tpb == 0)
        def _():
            dgate_ref[...] = jnp.zeros_like(dgate_ref)
        dg_ref[...] += _sum0(dxn * xhat)
        db_ref[...] += _sum0(dxn)
        dgate_ref[0] += _sum0((weight * out) * dz)
        for k in range(nk):
            da = _dot_nt(dout, w_ref[k])
            if fused:
                gg = gg_ref[k].astype(F32)
                s = pl.reciprocal(1.0 + jnp.exp(-gg), approx=True)
                da_ref[k] = (da * uu_ref[k].astype(F32) * (s * (1.0 + gg * (1.0 - s)))).astype(BF16)
                da_ref[nk + k] = (da * (gg * s)).astype(BF16)
            else:
                da_ref[k] = da.astype(BF16)

    row = pl.BlockSpec((tm, Dm), lambda i: (i, 0))
    vec = pl.BlockSpec((1, Dm), lambda i: (0, 0))
    wide = pl.BlockSpec((nk, tm, tk), lambda i: (0, i, 0))
    nda = 2 * nk if fused else nk
    grid = (T // tm,)
    body, c_in, c_out, c_shape, c_scratch = host_comm(body, grid, 8 if fused else 6, 6, exchange=exchange)
    *results, = pl.pallas_call(
        body, name=name, grid=grid,
        in_specs=[row, row, row, pl.BlockSpec((1, 3, Dm), lambda i: (i // tpb, 0, 0)), vec,
                  pl.BlockSpec((nk, tk, Dm), lambda i: (0, 0, 0))] + ([wide, wide] if fused else []) + c_in,
        out_specs=[row, row, pl.BlockSpec((nda, tm, tk), lambda i: (0, i, 0)),
                   vec, vec, pl.BlockSpec((1, 1, Dm), lambda i: (i // tpb, 0, 0))] + c_out,
        out_shape=[jax.ShapeDtypeStruct((T, Dm), F32), jax.ShapeDtypeStruct((T, Dm), BF16),
                   jax.ShapeDtypeStruct((nda, T, tk), BF16), jax.ShapeDtypeStruct((1, Dm), F32),
                   jax.ShapeDtypeStruct((1, Dm), F32), jax.ShapeDtypeStruct((Bl, 1, Dm), F32)] + c_shape,
        scratch_shapes=c_scratch, compiler_params=_cp(1),
    )(dxn, x, out, mod3, lng, w, *(gu if fused else ()), *exchange)
    return tuple(results[:6]) + ((results[6:],) if exchange else ())


def mm_tn(a, b, name, bw=None, a_copies=False, exchange=()):
    a3, b3 = a.ndim == 3, b.ndim == 3
    nk, T, tk = a.shape if a3 else (1,) + a.shape
    if a_copies:
        nk = 1
    nc, wn = (b.shape[0], b.shape[2]) if b3 else (b.shape[1] // bw, bw)
    tt = _tile(T, 2048)
    nt = T // tt

    def body(a_ref, b_ref, o_ref, ob_ref):
        t = pl.program_id(2)

        @pl.when(t == 0)
        def _():
            o_ref[...] = jnp.zeros_like(o_ref)
        o_ref[...] += _dot_tn(a_ref[...], b_ref[...])

        @pl.when(t == nt - 1)
        def _():
            ob_ref[...] = o_ref[...].astype(BF16)

    a_spec = (pl.BlockSpec((None, tt, tk), lambda k, c, t: (k, t, 0)) if a3
              else pl.BlockSpec((tt, tk), lambda k, c, t: (t, 0)))
    b_spec = (pl.BlockSpec((None, tt, wn), lambda k, c, t: (c, t, 0)) if b3
              else pl.BlockSpec((tt, wn), lambda k, c, t: (t, c)))
    o_spec = pl.BlockSpec((None, tk, wn), lambda k, c, t: (k * nc + c, 0, 0))
    grid = (nk, nc, nt)
    body, c_in, c_out, c_shape, c_scratch = host_comm(body, grid, 2, 2, exchange=exchange)
    o32, o16, *received = pl.pallas_call(
        body, name=name, grid=grid, in_specs=[a_spec, b_spec] + c_in, out_specs=[o_spec, o_spec] + c_out,
        out_shape=[jax.ShapeDtypeStruct((nk * nc, tk, wn), F32), jax.ShapeDtypeStruct((nk * nc, tk, wn), BF16)]
        + c_shape,
        scratch_shapes=c_scratch, compiler_params=_cp(3),
    )(a, b, *exchange)
    return (o32, o16, received) if exchange else (o32, o16)


def ffn_in(x, mod3, w, name, gather=()):
    T, Dm = x.shape
    nj, tf = w.shape[0] // 2, w.shape[2]
    Bl = mod3.shape[0]
    tm = _tile(T // Bl, 1024)
    tpb = T // Bl // tm

    def body(x_ref, mod_ref, wg_ref, wu_ref, a_ref, g_ref, u_ref, h_ref):
        m = mod_ref[0]
        h = (x_ref[...] * (1.0 + m[1:2, :]) + m[0:1, :]).astype(BF16)
        h_ref[...] = h
        g = _dot(h, wg_ref[...])
        u = _dot(h, wu_ref[...])
        a_ref[...] = (g * _sigmoid(g) * u).astype(BF16)
        g_ref[...] = g.astype(BF16)
        u_ref[...] = u.astype(BF16)

    col = pl.BlockSpec((None, tm, tf), lambda j, i: (j, i, 0))
    grid = (nj, T // tm)
    body, c_in, c_out, c_shape, c_scratch = host_comm(body, grid, 4, 4, gather=gather)
    a, g, u, h, *gathered = pl.pallas_call(
        body, name=name, grid=grid,
        in_specs=[pl.BlockSpec((tm, Dm), lambda j, i: (i, 0)),
                  pl.BlockSpec((1, 3, Dm), lambda j, i: (i // tpb, 0, 0)),
                  pl.BlockSpec((None, Dm, tf), lambda j, i: (j, 0, 0)),
                  pl.BlockSpec((None, Dm, tf), lambda j, i: (nj + j, 0, 0))] + c_in,
        out_specs=[col, col, col, pl.BlockSpec((None, tm, Dm), lambda j, i: (j, i, 0))] + c_out,
        out_shape=[jax.ShapeDtypeStruct((nj, T, tf), BF16)] * 3 + [jax.ShapeDtypeStruct((nj, T, Dm), BF16)]
        + c_shape,
        scratch_shapes=c_scratch, compiler_params=_cp(2),
    )(x, mod3, w, w, *gather)
    return a, g, u, h, gathered


def loss_head(y, tgt, name):
    T, Dm = y.shape
    tm = _tile(T, 512)
    nt = T // tm

    def body(y_ref, t_ref, dy_ref, l_ref, acc):
        i = pl.program_id(0)

        @pl.when(i == 0)
        def _():
            acc[...] = jnp.zeros_like(acc)
        e = y_ref[...] - t_ref[...]
        dy_ref[...] = e * (1.0 / Dm)
        acc[...] += _sum0(e * e)

        @pl.when(i == nt - 1)
        def _():
            l_ref[...] = jnp.broadcast_to(_sum1(acc[...]) * (0.5 / Dm), l_ref.shape)

    return pl.pallas_call(
        body, name=name, grid=(nt,),
        in_specs=[pl.BlockSpec((tm, Dm), lambda i: (i, 0)), pl.BlockSpec((tm, Dm), lambda i: (i, 0))],
        out_specs=[pl.BlockSpec((tm, Dm), lambda i: (i, 0)), pl.BlockSpec((1, 128), lambda i: (0, 0))],
        out_shape=[jax.ShapeDtypeStruct((T, Dm), F32), jax.ShapeDtypeStruct((1, 128), F32)],
        scratch_shapes=[pltpu.VMEM((1, Dm), F32)], compiler_params=_cp(1),
    )(y, tgt)


def adamw(w, g, m, v, name):
    R, C = w.shape
    tr = _tile(R, 512) if R % 8 == 0 else R

    def body(w_ref, g_ref, m_ref, v_ref, d_ref, nm_ref, nv_ref):
        gg = g_ref[...]
        mm = ADAM_B1 * m_ref[...] + (1.0 - ADAM_B1) * gg
        vv = ADAM_B2 * v_ref[...] + (1.0 - ADAM_B2) * (gg * gg)
        m_hat = mm / (1.0 - ADAM_B1 ** ADAM_STEP)
        v_hat = vv / (1.0 - ADAM_B2 ** ADAM_STEP)
        d_ref[...] = -ADAM_LR * (m_hat / (jnp.sqrt(v_hat) + ADAM_EPS) + ADAM_WD * w_ref[...])
        nm_ref[...] = mm
        nv_ref[...] = vv

    spec = pl.BlockSpec((tr, C), lambda i: (i, 0))
    return pl.pallas_call(
        body, name=name, grid=(R // tr,), in_specs=[spec] * 4, out_specs=[spec] * 3,
        out_shape=[jax.ShapeDtypeStruct((R, C), F32)] * 3, compiler_params=_cp(1),
    )(w, g, m, v)


def ada_fwd(c_all, ada_w, ada_b_cols, name):
    Lr, Dm, Nc = ada_w.shape
    Bg = c_all.shape[0]

    def body(c_ref, w_ref, b_ref, o_ref):
        cc = c_ref[...]
        cond = cc * _sigmoid(cc)
        o_ref[0] = _dot(cond.astype(BF16), w_ref[0].astype(BF16)) + b_ref[0]

    return pl.pallas_call(
        body, name=name, grid=(Lr,),
        in_specs=[pl.BlockSpec((Bg, Dm), lambda l: (0, 0)),
                  pl.BlockSpec((1, Dm, Nc), lambda l: (l, 0, 0)),
                  pl.BlockSpec((1, 1, Nc), lambda l: (l, 0, 0))],
        out_specs=pl.BlockSpec((1, Bg, Nc), lambda l: (l, 0, 0)),
        out_shape=jax.ShapeDtypeStruct((Lr, Bg, Nc), F32), compiler_params=_cp(1),
    )(c_all, ada_w, ada_b_cols)


def ada_bwd(c_all_t, dmod_cols, dmod_all, name):
    Dm, Bg = c_all_t.shape
    Lr, _, Nc = dmod_cols.shape
    Nf = dmod_all.shape[2]

    def body(c_ref, dm_ref, da_ref, gw_ref, gb_ref):
        cc = c_ref[...]
        cond = cc * _sigmoid(cc)
        gw_ref[0] = _dot(cond.astype(BF16), dm_ref[0].astype(BF16))
        gb_ref[0] = _sum0(da_ref[0])

    return pl.pallas_call(
        body, name=name, grid=(Lr,),
        in_specs=[pl.BlockSpec((Dm, Bg), lambda l: (0, 0)),
                  pl.BlockSpec((1, Bg, Nc), lambda l: (l, 0, 0)),
                  pl.BlockSpec((1, Bg, Nf), lambda l: (l, 0, 0))],
        out_specs=[pl.BlockSpec((1, Dm, Nc), lambda l: (l, 0, 0)),
                   pl.BlockSpec((1, 1, Nf), lambda l: (l, 0, 0))],
        out_shape=[jax.ShapeDtypeStruct((Lr, Dm, Nc), F32), jax.ShapeDtypeStruct((Lr, 1, Nf), F32)],
        compiler_params=_cp(1),
    )(c_all_t, dmod_cols, dmod_all)


def _conv_taps(x, w, rows):
    shifted = [x]
    c = w[3:4, :] * x
    for k in range(1, 4):
        xs = jnp.where(rows >= k, pltpu.roll(x, k, 0), 0.0)
        shifted.append(xs)
        c = c + w[3 - k:4 - k, :] * xs
    return c, shifted


def conv_silu(proj3, conv_w, name):
    Bl, S, _ = proj3.shape
    ncb = conv_w.shape[1] // 128

    def body(x_ref, w_ref, o_ref):
        rows = lax.broadcasted_iota(jnp.int32, (S, 128), 0)
        c, _ = _conv_taps(_round(x_ref[0]), _round(w_ref[...]), rows)
        o_ref[0] = c * _sigmoid(c)

    return pl.pallas_call(
        body, name=name, grid=(Bl, ncb),
        in_specs=[pl.BlockSpec((1, S, 128), lambda b, j: (b, 0, j)),
                  pl.BlockSpec((4, 128), lambda b, j: (0, j))],
        out_specs=pl.BlockSpec((1, S, 128), lambda b, j: (b, 0, j)),
        out_shape=jax.ShapeDtypeStruct((Bl, S, conv_w.shape[1]), F32), compiler_params=_cp(2),
    )(proj3, conv_w)


def conv_silu_bwd(proj3, conv_w, dq, dk, name):
    Bl, S, _ = proj3.shape
    nq = dq.shape[2] // 128

    def body(x_ref, w_ref, dq_ref, dk_ref, dx_ref, dw_ref):
        j = pl.program_id(1)
        rows = lax.broadcasted_iota(jnp.int32, (S, 128), 0)
        w = _round(w_ref[...])
        c, shifted = _conv_taps(_round(x_ref[0]), w, rows)
        s = _sigmoid(c)
        dact = jnp.where(j < nq, dq_ref[0], dk_ref[0])
        dc = _round(dact * (s * (1.0 + c * (1.0 - s))))
        dx = w[3:4, :] * dc
        dws = [_sum0(dc * shifted[0])]
        for k in range(1, 4):
            up = jnp.where(rows < S - k, pltpu.roll(dc, S - k, 0), 0.0)
            dx = dx + w[3 - k:4 - k, :] * up
            dws.append(_sum0(dc * shifted[k]))
        dx_ref[0] = dx.astype(BF16)
        tap = lax.broadcasted_iota(jnp.int32, (4, 128), 0)
        dw_ref[0] = functools.reduce(lambda a, b: a + b, [jnp.where(tap == 3 - k, dws[k], 0.0) for k in range(4)])

    return pl.pallas_call(
        body, name=name, grid=(Bl, 2 * nq),
        in_specs=[pl.BlockSpec((1, S, 128), lambda b, j: (b, 0, j)),
                  pl.BlockSpec((4, 128), lambda b, j: (0, j)),
                  pl.BlockSpec((1, S, 128), lambda b, j: (b, 0, jnp.minimum(j, nq - 1))),
                  pl.BlockSpec((1, S, 128), lambda b, j: (b, 0, jnp.maximum(j - nq, 0)))],
        out_specs=[pl.BlockSpec((1, S, 128), lambda b, j: (b, 0, j)),
                   pl.BlockSpec((1, 4, 128), lambda b, j: (b, 0, j))],
        out_shape=[jax.ShapeDtypeStruct((Bl, S, 2 * nq * 128), BF16),
                   jax.ShapeDtypeStruct((Bl, 4, 2 * nq * 128), F32)],
        compiler_params=_cp(2),
    )(proj3, conv_w, dq, dk)


def _log_sigmoid(a):
    return jnp.minimum(a, 0.0) - jnp.log(1.0 + jnp.exp(-jnp.abs(a)))


def _interleave(gens):
    live = list(gens)
    while live:
        still = []
        for g in live:
            try:
                next(g)
                still.append(g)
            except StopIteration:
                pass
        live = still


def _finish(gen):
    while True:
        try:
            next(gen)
        except StopIteration as done:
            return done.value


def _chunk_state(kc, vc, gi, bcum, b_last, C, n, m):
    a = b_last - bcum + gi
    m_loc = jnp.max(a, axis=0, keepdims=True)
    wa = jnp.exp(a - m_loc)
    c_loc = _dot_tn((wa * vc).astype(BF16), kc.astype(BF16))
    n_loc = _sum0(_round(wa) * _round(kc))
    m_new = jnp.maximum(b_last + m, m_loc)
    sp = jnp.exp(b_last + m - m_new)
    sl = jnp.exp(m_loc - m_new)
    yield
    return sp * C + sl * c_loc, sp * n + sl * n_loc, m_new, wa, sp, sl


def _chunk_out(qs, kc, vc, gi_row, bcum, bcum_row, low, C, n, m):
    inter_log = bcum + m
    dlog = jnp.where(low, bcum - bcum_row + gi_row, NEG)
    m_i = jnp.maximum(inter_log, jnp.max(dlog, axis=1, keepdims=True))
    dm = jnp.exp(dlog - m_i)
    iw = jnp.exp(inter_log - m_i)
    qs_b, k_b, v_b = qs.astype(BF16), kc.astype(BF16), vc.astype(BF16)
    sqk = _dot_nt(qs_b, k_b)
    qc_ = _dot_nt(qs_b, C.astype(BF16))
    qn = _sum1(_round(qs) * _round(n))
    floor = jnp.exp(-m_i)
    yield
    sc = sqk * dm
    sv = _dot(sc.astype(BF16), v_b)
    den = _sum1(sc) + iw * qn
    dn = jnp.maximum(jnp.abs(den), floor)
    yield
    num = sv + iw * qc_
    return dict(hc=num / dn, den=den, dn=dn, floor=floor, sc=sc, dm=dm, iw=iw, qc=qc_, qn=qn,
                qs_b=qs_b, k_b=k_b, v_b=v_b)


def _cell_consts(L):
    ri = lax.broadcasted_iota(jnp.int32, (L, L), 0)
    ci = lax.broadcasted_iota(jnp.int32, (L, L), 1)
    return ri == ci, ci <= ri, ri <= ci


def _load_chunk(q_ref, k_ref, v_ref, G, off, L, h, lane):
    hh = h % 2
    qmask = (lane >= M_DQK * hh) & (lane < M_DQK * (hh + 1))
    pair = pl.ds(128 * (h // 2), 128)
    qc = jnp.where(qmask, q_ref[0, pl.ds(off, L), pair], 0.0)
    kc = jnp.where(qmask, k_ref[0, pl.ds(off, L), pair], 0.0)
    vc = v_ref[0, pl.ds(off, L), pl.ds(M_DV * h, M_DV)]
    gi = _sum1(jnp.where(lane == h, G, 0.0))
    gf = _sum1(jnp.where(lane == h + HEADS, G, 0.0))
    return qmask, qc, kc, vc, gi, gf


def _gate_rows(gi, gf, eye, low, upp):
    lf = _log_sigmoid(gf)
    lf_row = _sum0(jnp.where(eye, lf, 0.0))
    gi_row = _sum0(jnp.where(eye, gi, 0.0))
    bcum = _sum1(jnp.where(low, lf_row, 0.0))
    bcum_row = _sum0(jnp.where(upp, lf, 0.0))
    b_last = _sum0(lf)
    return gi_row, bcum, bcum_row, b_last


def _cell_specs(SB, cpb, blk):
    def seq(width, col):
        return pl.BlockSpec((1, SB, width), lambda b, s: (b, blk(s), col))

    def state(rows):
        return pl.BlockSpec((1, HEADS, cpb, rows, 128), lambda b, s: (b, 0, blk(s), 0, 0))

    ins = [seq(D // 2, 0), seq(D // 2, 1), seq(D, 1), seq(D, 2), seq(128, 3 * D // 128),
           pl.BlockSpec((1, D), lambda b, s: (0, 0)), pl.BlockSpec((1, 128), lambda b, s: (0, 0))]
    return ins, [state(M_DV), state(1), state(1)], seq


def mlstm_cell_fwd(qk3, proj3, gain, gbias, name, gather=()):
    Bl, S, _ = qk3.shape
    L = M_CHUNK
    SB = min(M_SLAB, S)
    cpb, nc, nsb = SB // L, S // L, S // SB
    scale = M_DQK ** -0.5

    def body(q_ref, k_ref, v_ref, o_ref, g_ref, gain_ref, gb_ref, y_ref, cst_ref, nst_ref, mst_ref, *state):
        C_s, n_s, m_s = state[:HEADS], state[HEADS:2 * HEADS], state[2 * HEADS:]

        @pl.when(pl.program_id(1) == 0)
        def _():
            for ref in state:
                ref[...] = jnp.zeros_like(ref)
        lane = lax.broadcasted_iota(jnp.int32, (L, 128), 1)
        eye, low, upp = _cell_consts(L)

        def step(c, carry):
            off = pl.multiple_of(c * L, L)
            G = g_ref[0, pl.ds(off, L), :] + gb_ref[...]

            def head(h):
                C, n, mb = C_s[h][...], n_s[h][...], m_s[h][...]
                cst_ref[0, h, c] = C
                nst_ref[0, h, c] = n
                mst_ref[0, h, c] = mb
                m = mb[:, 0:1]
                _, qc, kc, vc, gi, gf = _load_chunk(q_ref, k_ref, v_ref, G, off, L, h, lane)
                gi_row, bcum, bcum_row, b_last = _gate_rows(gi, gf, eye, low, upp)
                state = _chunk_state(kc, vc, gi, bcum, b_last, C, n, m)
                next(state)
                r = yield from _chunk_out(qc * scale, kc, vc, gi_row, bcum, bcum_row, low, C, n, m)
                hc = r["hc"]
                hn = hc * lax.rsqrt(jnp.mean(hc * hc, axis=-1, keepdims=True) + RMS_EPS)
                cols = pl.ds(M_DV * h, M_DV)
                oc = o_ref[0, pl.ds(off, L), cols]
                y_ref[0, pl.ds(off, L), cols] = (_sigmoid(oc) * hn * gain_ref[:, cols]).astype(BF16)
                C2, n2, m2, _, _, _ = _finish(state)
                C_s[h][...] = C2
                n_s[h][...] = n2
                m_s[h][...] = jnp.broadcast_to(m2, (1, 128))

            _interleave(head(h) for h in range(HEADS))
            return carry

        lax.fori_loop(0, cpb, step, 0)

    ins, states, seq = _cell_specs(SB, cpb, lambda s: s)
    grid = (Bl, nsb)
    body, c_in, c_out, c_shape, c_scratch = host_comm(body, grid, 7, 4, gather=gather)
    return pl.pallas_call(
        body, name=name, grid=grid, in_specs=ins + c_in, out_specs=[seq(D, 0)] + states + c_out,
        out_shape=[jax.ShapeDtypeStruct((Bl, S, D), BF16),
                   jax.ShapeDtypeStruct((Bl, HEADS, nc, M_DV, 128), F32),
                   jax.ShapeDtypeStruct((Bl, HEADS, nc, 1, 128), F32),
                   jax.ShapeDtypeStruct((Bl, HEADS, nc, 1, 128), F32)] + c_shape,
        scratch_shapes=[pltpu.VMEM((M_DV, 128), F32)] * HEADS + [pltpu.VMEM((1, 128), F32)] * (2 * HEADS) + c_scratch,
        compiler_params=_cp(2),
    )(qk3, qk3, proj3, proj3, proj3, gain, gbias, *gather)


def mlstm_cell_bwd(qk3, proj3, gain, gbias, dy3, states, name, exchange=()):
    Bl, S, _ = qk3.shape
    L = M_CHUNK
    SB = min(M_SLAB, S)
    cpb, nsb = SB // L, S // SB
    scale = M_DQK ** -0.5

    def body(q_ref, k_ref, v_ref, o_ref, g_ref, gain_ref, gb_ref, cst_ref, nst_ref, mst_ref, dy_ref,
             dq_ref, dk_ref, dv_ref, do_ref, dg_ref, dgain_ref, dgb_ref, *state):
        dC_s, dn_s, dgain_s, dgb_s = state[:HEADS], state[HEADS:2 * HEADS], state[2 * HEADS:3 * HEADS], state[-1]
        s = pl.program_id(1)

        @pl.when(s == 0)
        def _():
            for ref in state:
                ref[...] = jnp.zeros_like(ref)
        lane = lax.broadcasted_iota(jnp.int32, (L, 128), 1)
        rowi = lax.broadcasted_iota(jnp.int32, (L, 1), 0)
        eye, low, upp = _cell_consts(L)

        def bstep(t, carry):
            c = cpb - 1 - t
            off = pl.multiple_of(c * L, L)
            G = g_ref[0, pl.ds(off, L), :] + gb_ref[...]
            shared = dict(slab=jnp.zeros((L, 128), F32))

            def head(h):
                cols = pl.ds(M_DV * h, M_DV)
                gain_h = gain_ref[:, cols]
                C, n, m = cst_ref[0, h, c], nst_ref[0, h, c], mst_ref[0, h, c][:, 0:1]
                dC_n, dn_n = dC_s[h][...], dn_s[h][...]
                qmask, qc, kc, vc, gi, gf = _load_chunk(q_ref, k_ref, v_ref, G, off, L, h, lane)
                gi_row, bcum, bcum_row, b_last = _gate_rows(gi, gf, eye, low, upp)
                qs = qc * scale
                _, _, _, wa, sp, sl = _finish(_chunk_state(kc, vc, gi, bcum, b_last, C, n, m))
                dcl_b = (sl * dC_n).astype(BF16)
                t1_mm = _dot(vc.astype(BF16), dcl_b)
                dv_mm = _dot_nt(kc.astype(BF16), dcl_b)
                r = yield from _chunk_out(qs, kc, vc, gi_row, bcum, bcum_row, low, C, n, m)
                hc, den, dn, sc, dm, iw, qn = r["hc"], r["den"], r["dn"], r["sc"], r["dm"], r["iw"], r["qn"]
                qs_b, k_b, v_b = r["qs_b"], r["k_b"], r["v_b"]
                dy = dy_ref[0, pl.ds(off, L), cols].astype(F32)
                oc = o_ref[0, pl.ds(off, L), cols]
                sig_o = _sigmoid(oc)
                rr = lax.rsqrt(jnp.mean(hc * hc, axis=-1, keepdims=True) + RMS_EPS)
                hn = hc * rr
                dgain_s[h][...] += _sum0(dy * sig_o * hn)
                do_ref[0, pl.ds(off, L), cols] = (
                    dy * hn * gain_h * sig_o * (1.0 - sig_o)).astype(BF16)
                dhn = dy * sig_o * gain_h
                dhc = rr * dhn - hc * (rr * rr * rr) * jnp.mean(dhn * hc, axis=-1, keepdims=True)
                dnum = dhc / dn
                gden = -_sum1(dhc * hc) / dn
                dden = jnp.where(jnp.abs(den) > r["floor"], gden * jnp.sign(den), 0.0)
                dnum_b = dnum.astype(BF16)
                dqc_b = (iw * dnum).astype(BF16)
                dsc_mm = _dot_nt(dnum_b, v_b)
                dv = _dot_tn(sc.astype(BF16), dnum_b)
                dqs_mm = _dot(dqc_b, C.astype(BF16))
                dC_out = _dot_tn(dqc_b, qs_b)
                diw = _sum1(dnum * r["qc"]) + dden * qn
                wq = iw * dden
                dn_out = _sum0(wq * qs)
                dn_loc = sl * dn_n
                dsp = _sum1(_sum0(dC_n * C)) + _sum1(dn_n * n)
                yield
                dsc = dsc_mm + dden
                dS_b = (dsc * dm).astype(BF16)
                gm = dsc * sc
                dqs2_mm = _dot(dS_b, k_b)
                dk = _dot_tn(dS_b, qs_b)
                dqs = dqs_mm + wq * n
                dbc = _sum1(gm) + diw * iw
                colg = _sum0(gm)
                dC_p = sp * dC_n + dC_out
                dn_p = sp * dn_n + dn_out
                db_last = dsp * sp
                t1 = t1_mm + dn_loc
                dwa = _sum1(t1 * kc)
                dv = dv + wa * dv_mm
                yield
                dqs = dqs + dqs2_mm
                dk = dk + wa * t1
                da = dwa * wa
                db_last = db_last + _sum0(da)
                dbc = dbc - da + jnp.where(rowi == L - 1, db_last, 0.0)
                dbc_row = _sum0(jnp.where(eye, dbc, 0.0)) - colg
                dgi = da + _sum1(jnp.where(eye, colg, 0.0))
                dlf = _sum1(jnp.where(upp, dbc_row, 0.0))
                dgf = dlf * _sigmoid(-gf)
                dq = jnp.where(qmask, dqs * scale, 0.0)
                dk = jnp.where(qmask, dk, 0.0)
                shared["slab"] = (shared["slab"] + jnp.where(lane == h, dgi, 0.0)
                                  + jnp.where(lane == h + HEADS, dgf, 0.0))
                dv_ref[0, pl.ds(off, L), cols] = dv.astype(BF16)
                dC_s[h][...] = dC_p
                dn_s[h][...] = dn_p
                if h % 2 == 0:
                    shared["dq"], shared["dk"] = dq, dk
                else:
                    pair = pl.ds(128 * (h // 2), 128)
                    dq_ref[0, pl.ds(off, L), pair] = shared["dq"] + dq
                    dk_ref[0, pl.ds(off, L), pair] = shared["dk"] + dk

            _interleave(head(h) for h in range(HEADS))
            dg_ref[0, pl.ds(off, L), :] = shared["slab"]
            dgb_s[...] += _sum0(shared["slab"])
            return carry

        lax.fori_loop(0, cpb, bstep, 0)

        @pl.when(s == nsb - 1)
        def _():
            for h in range(HEADS):
                dgain_ref[0, :, pl.ds(M_DV * h, M_DV)] = dgain_s[h][...]
            dgb_ref[0] = dgb_s[...]

    ins, states_specs, seq = _cell_specs(SB, cpb, lambda s: nsb - 1 - s)
    once = lambda width: pl.BlockSpec((1, 1, width), lambda b, s: (b, 0, 0))
    grid = (Bl, nsb)
    body, c_in, c_out, c_shape, c_scratch = host_comm(body, grid, 11, 7, exchange=exchange)
    return pl.pallas_call(
        body, name=name, grid=grid, in_specs=ins + states_specs + [seq(D, 0)] + c_in,
        out_specs=[seq(D // 2, 0), seq(D // 2, 0), seq(D, 0), seq(D, 0), seq(128, 0), once(D), once(128)] + c_out,
        out_shape=[jax.ShapeDtypeStruct((Bl, S, D // 2), F32), jax.ShapeDtypeStruct((Bl, S, D // 2), F32),
                   jax.ShapeDtypeStruct((Bl, S, D), BF16), jax.ShapeDtypeStruct((Bl, S, D), BF16),
                   jax.ShapeDtypeStruct((Bl, S, 128), F32), jax.ShapeDtypeStruct((Bl, 1, D), F32),
                   jax.ShapeDtypeStruct((Bl, 1, 128), F32)] + c_shape,
        scratch_shapes=[pltpu.VMEM((M_DV, 128), F32)] * HEADS + [pltpu.VMEM((1, 128), F32)] * (2 * HEADS + 1)
        + c_scratch,
        compiler_params=_cp(2),
    )(qk3, qk3, proj3, proj3, proj3, gain, gbias, *states, dy3, *exchange)


def _attn_scores(q, kc, kp, n, row, col, scale):
    s_c = jnp.where(col <= row, _dot_nt(q, kc) * scale, NEG)
    s_p = jnp.where(jnp.logical_and(col >= row, n > 0), _dot_nt(q, kp) * scale, NEG)
    return s_c, s_p


def _to_streams(src, dst, tmp, dil, Sd):
    if dil == 1:
        dst[...] = src[...].astype(dst.dtype)
        return
    if src.dtype != F32:
        tmp[...] = src[...].astype(F32)
        src = tmp
    for r in range(dil):
        dst[pl.ds(r * Sd, Sd), :] = src[pl.ds(r, Sd, stride=dil), :].astype(dst.dtype)


def _from_streams(src, dst, dil, Sd):
    if dil == 1:
        dst[...] = src[...]
        return
    for r in range(dil):
        dst[pl.ds(r, Sd, stride=dil), :] = src[pl.ds(r * Sd, Sd), :]


def attn_fwd(proj, Bl, S, g, dil, name):
    Sd = S // dil
    nb = Sd // A_BLK
    scale = A_BLK ** -0.5
    pv = proj.reshape(Bl, S, A_PROJ)

    def body(q_ref, k_ref, v_ref, o_ref, l_ref, tmp, qs, ks, vs, os_, ls):
        row = lax.broadcasted_iota(jnp.int32, (A_BLK, A_BLK), 0)
        col = lax.broadcasted_iota(jnp.int32, (A_BLK, A_BLK), 1)
        for src, dst in ((q_ref, qs), (k_ref, ks), (v_ref, vs)):
            _to_streams(src.at[0], dst, tmp, dil, Sd)

        def step(i, carry):
            n = i % nb
            off = pl.multiple_of(i * A_BLK, A_BLK)
            offp = pl.multiple_of(jnp.maximum(i - 1, 0) * A_BLK, A_BLK)
            q = qs[pl.ds(off, A_BLK), :]
            s_c, s_p = _attn_scores(q, ks[pl.ds(off, A_BLK), :], ks[pl.ds(offp, A_BLK), :], n, row, col, scale)
            m = jnp.maximum(jnp.max(s_c, axis=1, keepdims=True), jnp.max(s_p, axis=1, keepdims=True))
            p_c = jnp.exp(s_c - m)
            p_p = jnp.exp(s_p - m)
            den = _sum1(p_c) + _sum1(p_p)
            o = _dot(p_c.astype(BF16), vs[pl.ds(off, A_BLK), :]) + _dot(p_p.astype(BF16), vs[pl.ds(offp, A_BLK), :])
            os_[pl.ds(off, A_BLK), :] = o / den
            ls[pl.ds(off, A_BLK), :] = jnp.broadcast_to(m + jnp.log(den), (A_BLK, 128))
            return carry

        lax.fori_loop(0, dil * nb, step, 0, unroll=A_UNROLL)
        _from_streams(os_, o_ref.at[0], dil, Sd)
        _from_streams(ls, l_ref.at[0], dil, Sd)

    def spec(j):
        return pl.BlockSpec((1, S, 128), lambda b, h: (b, 0, g * 24 + j * HEADS + h))

    ospec = pl.BlockSpec((1, S, 128), lambda b, h: (b, 0, h))
    o, lse = pl.pallas_call(
        body, name=name, grid=(Bl, HEADS),
        in_specs=[spec(0), spec(1), spec(2)], out_specs=[ospec, ospec],
        out_shape=[jax.ShapeDtypeStruct((Bl, S, D), F32)] * 2,
        scratch_shapes=[pltpu.VMEM((S, 128), F32)] + [pltpu.VMEM((S, 128), BF16)] * 3 + [pltpu.VMEM((S, 128), F32)] * 2,
        compiler_params=_cp(2),
    )(pv, pv, pv)
    return o.reshape(Bl * S, D), lse.reshape(Bl * S, D)


def attn_merge(os_, lses, name):
    T = os_[0].shape[0]
    tm = _tile(T, 512)
    ng = len(os_)

    def body(*refs):
        o_refs, l_refs = refs[:ng], refs[ng:2 * ng]
        ob_ref, of_ref, lt_ref = refs[2 * ng:]
        ls = [r[...] for r in l_refs]
        m = functools.reduce(jnp.maximum, ls)
        ws = [jnp.exp(l - m) for l in ls]
        den = functools.reduce(lambda a, b: a + b, ws)
        o = functools.reduce(lambda a, b: a + b, [w * r[...] for w, r in zip(ws, o_refs)]) / den
        of_ref[...] = o
        ob_ref[...] = o.astype(BF16)
        lt_ref[...] = m + jnp.log(den)

    spec = pl.BlockSpec((tm, D), lambda i: (i, 0))
    return pl.pallas_call(
        body, name=name, grid=(T // tm,), in_specs=[spec] * (2 * ng), out_specs=[spec] * 3,
        out_shape=[jax.ShapeDtypeStruct((T, D), BF16), jax.ShapeDtypeStruct((T, D), F32),
                   jax.ShapeDtypeStruct((T, D), F32)],
        compiler_params=_cp(1),
    )(*os_, *lses)


def attn_bwd(proj, do, o, lse, Bl, S, g, dil, name):
    Sd = S // dil
    nb = Sd // A_BLK
    scale = A_BLK ** -0.5
    pv = proj.reshape(Bl, S, A_PROJ)
    dov, ov, lv = (t.reshape(Bl, S, D) for t in (do, o, lse))

    def body(q_ref, k_ref, v_ref, do_ref, o_ref, l_ref, dq_ref, dk_ref, dv_ref,
             tmp, qs, ks, vs, dos, dls, lts, dq_s, dk_s, dv_s):
        row = lax.broadcasted_iota(jnp.int32, (A_BLK, A_BLK), 0)
        col = lax.broadcasted_iota(jnp.int32, (A_BLK, A_BLK), 1)
        for src, dst in ((q_ref, qs), (k_ref, ks), (v_ref, vs), (do_ref, dos), (l_ref, lts)):
            _to_streams(src.at[0], dst, tmp, dil, Sd)
        tmp[...] = jnp.broadcast_to(_sum1(do_ref[0].astype(F32) * o_ref[0]), (S, 128))
        _to_streams(tmp, dls, None, dil, Sd)
        dk_s[...] = jnp.zeros_like(dk_s)
        dv_s[...] = jnp.zeros_like(dv_s)

        def step(i, carry):
            n = i % nb
            off = pl.multiple_of(i * A_BLK, A_BLK)
            offp = pl.multiple_of(jnp.maximum(i - 1, 0) * A_BLK, A_BLK)
            q = qs[pl.ds(off, A_BLK), :]
            kc, kp = ks[pl.ds(off, A_BLK), :], ks[pl.ds(offp, A_BLK), :]
            vc, vp = vs[pl.ds(off, A_BLK), :], vs[pl.ds(offp, A_BLK), :]
            do_b = dos[pl.ds(off, A_BLK), :]
            delta = dls[pl.ds(off, A_BLK), :][:, 0:1]
            lt = lts[pl.ds(off, A_BLK), :][:, 0:1]
            s_c, s_p = _attn_scores(q, kc, kp, n, row, col, scale)
            p_c = jnp.exp(s_c - lt)
            p_p = jnp.exp(s_p - lt)
            ds_c = (p_c * (_dot_nt(do_b, vc) - delta) * scale).astype(BF16)
            ds_p = (p_p * (_dot_nt(do_b, vp) - delta) * scale).astype(BF16)
            dq_s[pl.ds(off, A_BLK), :] = _dot(ds_c, kc) + _dot(ds_p, kp)
            dk_s[pl.ds(off, A_BLK), :] += _dot_tn(ds_c, q)
            dk_s[pl.ds(offp, A_BLK), :] += _dot_tn(ds_p, q)
            dv_s[pl.ds(off, A_BLK), :] += _dot_tn(p_c.astype(BF16), do_b)
            dv_s[pl.ds(offp, A_BLK), :] += _dot_tn(p_p.astype(BF16), do_b)
            return carry

        lax.fori_loop(0, dil * nb, step, 0, unroll=A_UNROLL)
        for src, dst in ((dq_s, dq_ref), (dk_s, dk_ref), (dv_s, dv_ref)):
            _from_streams(src, tmp, dil, Sd)
            dst[0] = tmp[...].astype(BF16)

    def spec(j):
        return pl.BlockSpec((1, S, 128), lambda b, h: (b, 0, g * 24 + j * HEADS + h))

    ospec = pl.BlockSpec((1, S, 128), lambda b, h: (b, 0, h))
    slab = lambda dt: pltpu.VMEM((S, 128), dt)
    outs = pl.pallas_call(
        body, name=name, grid=(Bl, HEADS),
        in_specs=[spec(0), spec(1), spec(2), ospec, ospec, ospec], out_specs=[ospec] * 3,
        out_shape=[jax.ShapeDtypeStruct((Bl, S, D), BF16)] * 3,
        scratch_shapes=[slab(F32)] + [slab(BF16)] * 4 + [slab(F32)] * 5,
        compiler_params=_cp(2),
    )(pv, pv, pv, dov, ov, lv)
    return [t.reshape(Bl * S, D) for t in outs]


def _as_slots(pair, shape):
    return tuple(t.reshape(shape) for t in pair)


def ffn_fwd(x, mod3, w_in, w_out, lng, lnb, tag, gather=()):
    a, g, u, h, gathered = ffn_in(x, mod3, w_in, tag + "_in", gather=gather)
    out, xn = proj_post(a, w_out, x, mod3, lng, lnb, 0.5, tag + "_out")
    return xn, (x, out, g, u, h, a), gathered


def ffn_bwd(dxn, saved, mod3, w_in, w_out, lng, tag, exchange=(), exchange2=(), exchange_own=False):
    x, out, g, u, h, a = saved
    dxres, dout, dgu, dlg, dlb, dgate, *got = post_bwd(dxn, x, out, mod3, lng, w_out, 0.5, tag + "_outb",
                                                       tm=256, gu=(g, u), exchange=exchange)
    *dw_in, got2 = mm_tn(h, dgu, tag + "_dwin", a_copies=True, exchange=exchange2) + (() if exchange2 else ([],))
    dw_out = _as_slots(mm_tn(a, dout, tag + "_dwout", bw=D), (N_DEV, D_FF // N_DEV, D))
    dx, dsh, dsc, *own = modmm_bwd(dgu, w_in, x, mod3, dxres, tag + "_inb", tm=256,
                                   exchange=[dw_in[1], dw_out[1]] if exchange_own else ())
    dmod3 = jnp.concatenate([dsh, dsc, dgate], axis=1)
    return (dx, [tuple(dw_in), dw_out], dlg, dlb, dmod3,
            (got[0] if got else []), got2, (own[0] if own else []))


def mlstm_fwd(x, mod3, w_in, w_out, conv_w, gain, gbias, lng, lnb, Bl, S, gather=()):
    proj, h = modmm(x, mod3, w_in, F32, "ml_in", tn=M_PROJ_PAD // 5)
    proj3 = proj.reshape(Bl, S, M_PROJ_PAD)
    qk3 = conv_silu(proj3, conv_w, "ml_conv")
    y3, *rest = mlstm_cell_fwd(qk3, proj3, gain, gbias, "ml_cell", gather=gather)
    states, gathered = rest[:3], rest[3:]
    y = y3.reshape(Bl * S, D)
    out, xn = proj_post(y[None], w_out, x, mod3, lng, lnb, 1.0, "ml_out")
    return xn, (x, out, h, proj3, qk3, y, states), gathered


def mlstm_bwd(dxn, saved, mod3, w_in, w_out, conv_w, gain, gbias, lng, Bl, S, exchange=()):
    x, out, h, proj3, qk3, y, states = saved
    dxres, dout, dy, dlg, dlb, dgate = post_bwd(dxn, x, out, mod3, lng, w_out, 1.0, "ml_outb")
    dq, dk, dv, do, dg, dgain, dgb, *received = mlstm_cell_bwd(qk3, proj3, gain, gbias, dy.reshape(Bl, S, D),
                                                               states, "ml_cellb", exchange=exchange)
    dqk, dconv = conv_silu_bwd(proj3, conv_w, dq, dk, "ml_convb")
    dproj = jnp.concatenate([dqk, dv, do, dg.astype(BF16)], axis=2).reshape(Bl * S, M_PROJ_PAD)
    dx, dsh, dsc = modmm_bwd(dproj, w_in, x, mod3, dxres, "ml_inb", tn=M_PROJ_PAD // 5)
    dwi, _ = mm_tn(h, dproj, "ml_dwin", bw=M_PROJ_PAD // 5)
    dwi = _restack(jnp.moveaxis(dwi, 0, 1).reshape(D, M_PROJ_PAD)[:, :M_PROJ], 1)
    dw_out = _as_slots(mm_tn(y, dout, "ml_dwout", bw=D), (N_DEV, D // N_DEV, D))
    small = (jnp.sum(dconv, axis=0), jnp.sum(dgain, axis=0), jnp.sum(dgb, axis=0)[:, :2 * HEADS])
    dmod3 = jnp.concatenate([dsh, dsc, dgate], axis=1)
    return dx, [(dwi, dwi.astype(BF16)), dw_out], dlg, dlb, dmod3, small, received


def attn_mixer_fwd(x, mod3, w_in, w_out, lng, lnb, Bl, S):
    proj, h = modmm(x, mod3, w_in, BF16, "at_in")
    os_, lses = [], []
    for g, (_, dil) in enumerate(DIL_GROUPS):
        o_g, l_g = attn_fwd(proj, Bl, S, g, dil, "at_core%d" % g)
        os_.append(o_g)
        lses.append(l_g)
    ob, of, lt = attn_merge(os_, lses, "at_merge")
    out, xn = proj_post(ob[None], w_out, x, mod3, lng, lnb, 1.0, "at_out")
    return xn, (x, out, h, proj, ob, of, lt)


def attn_mixer_bwd(dxn, saved, mod3, w_in, w_out, lng, Bl, S):
    x, out, h, proj, ob, of, lt = saved
    dxres, dout, do, dlg, dlb, dgate = post_bwd(dxn, x, out, mod3, lng, w_out, 1.0, "at_outb")
    do = do[0]
    parts = []
    for g, (_, dil) in enumerate(DIL_GROUPS):
        parts += attn_bwd(proj, do, of, lt, Bl, S, g, dil, "at_coreb%d" % g)
    dproj = jnp.concatenate(parts, axis=1)
    dx, dsh, dsc = modmm_bwd(dproj, w_in, x, mod3, dxres, "at_inb")
    dw_in = mm_tn(h, dproj, "at_dwin", bw=w_in.shape[2])
    dw_out = _as_slots(mm_tn(ob, dout, "at_dwout", bw=D), (N_DEV, D // N_DEV, D))
    return dx, [dw_in, dw_out], dlg, dlb, jnp.concatenate([dsh, dsc, dgate], axis=1)


def _unstack(stacked, axis):
    full = jnp.moveaxis(stacked, 0, axis)
    shp = list(full.shape)
    shp[axis:axis + 2] = [shp[axis] * shp[axis + 1]]
    return full.reshape(shp)


def _restack(full, axis):
    shp = list(full.shape)
    shp[axis:axis + 1] = [N_DEV, shp[axis] // N_DEV]
    return jnp.moveaxis(full.reshape(shp), axis, 0)


def kernel(x, c, ada_w, ada_b, ln_g, ln_b, ffn_w_in, ffn_w_out, mlstm_w_in, mlstm_gate_bias, mlstm_conv_w, mlstm_head_gain, mlstm_w_out, attn_w_in, attn_w_out, loss_target, m_ada_w, m_ada_b, m_ln_g, m_ln_b, m_ffn_w_in, m_ffn_w_out, m_mlstm_w_in, m_mlstm_gate_bias, m_mlstm_conv_w, m_mlstm_head_gain, m_mlstm_w_out, m_attn_w_in, m_attn_w_out, v_ada_w, v_ada_b, v_ln_g, v_ln_b, v_ffn_w_in, v_ffn_w_out, v_mlstm_w_in, v_mlstm_gate_bias, v_mlstm_conv_w, v_mlstm_head_gain, v_mlstm_w_out, v_attn_w_in, v_attn_w_out):
    Bl, S, _ = x.shape
    T = Bl * S
    Bg = Bl * N_DEV
    me = 4 * lax.axis_index("x") + 2 * lax.axis_index("y") + lax.axis_index("c")
    onehot = (jnp.arange(N_DEV) == me).astype(F32)
    weights = dict(ada_w=ada_w, ada_b=ada_b, ln_g=ln_g, ln_b=ln_b, ffn_w_in=ffn_w_in, ffn_w_out=ffn_w_out,
                   mlstm_w_in=mlstm_w_in, mlstm_gate_bias=mlstm_gate_bias, mlstm_conv_w=mlstm_conv_w,
                   mlstm_head_gain=mlstm_head_gain, mlstm_w_out=mlstm_w_out, attn_w_in=attn_w_in,
                   attn_w_out=attn_w_out)
    m_in = dict(ada_w=m_ada_w, ada_b=m_ada_b, ln_g=m_ln_g, ln_b=m_ln_b, ffn_w_in=m_ffn_w_in,
                ffn_w_out=m_ffn_w_out, mlstm_w_in=m_mlstm_w_in, mlstm_gate_bias=m_mlstm_gate_bias,
                mlstm_conv_w=m_mlstm_conv_w, mlstm_head_gain=m_mlstm_head_gain, mlstm_w_out=m_mlstm_w_out,
                attn_w_in=m_attn_w_in, attn_w_out=m_attn_w_out)
    v_in = dict(ada_w=v_ada_w, ada_b=v_ada_b, ln_g=v_ln_g, ln_b=v_ln_b, ffn_w_in=v_ffn_w_in,
                ffn_w_out=v_ffn_w_out, mlstm_w_in=v_mlstm_w_in, mlstm_gate_bias=v_mlstm_gate_bias,
                mlstm_conv_w=v_mlstm_conv_w, mlstm_head_gain=v_mlstm_head_gain, mlstm_w_out=v_mlstm_w_out,
                attn_w_in=v_attn_w_in, attn_w_out=v_attn_w_out)

    mixer = ("mlstm", "attn")
    shards = [[ffn_w_in[layer, 0], ffn_w_in[layer, 1], ffn_w_out[layer, 0], ffn_w_out[layer, 1],
               weights[mixer[layer] + "_w_in"][0], weights[mixer[layer] + "_w_out"][0]] for layer in range(DEPTH)]
    sends = [[s.astype(BF16) for s in layer_shards] for layer_shards in shards]
    small = jnp.concatenate([c.reshape(-1), ln_g.reshape(-1), ln_b.reshape(-1), mlstm_conv_w.reshape(-1)])
    n_small = small.shape[0]
    small = jnp.pad(small, (0, -n_small % (8 * PACK_COLS))).reshape(-1, PACK_COLS)

    def gathered_weights(g):
        return ((g[0], g[1]), (g[2].reshape(4, D_FF // 4, D), g[3].reshape(4, D_FF // 4, D)), g[4],
                g[5].reshape(1, D, D))

    first_in, small_all = all_gather([sends[0][0], small], "ag_params")
    full = [None, None]
    small_flat = small_all.reshape(N_DEV, -1)
    o0 = 0
    c_all = small_flat[:, o0:o0 + c.size].reshape(Bg, D)
    o0 += c.size
    lng_full = _unstack(small_flat[:, o0:o0 + ln_g.size].reshape((N_DEV,) + ln_g.shape), 2)
    o0 += ln_g.size
    lnb_full = _unstack(small_flat[:, o0:o0 + ln_b.size].reshape((N_DEV,) + ln_b.shape), 2)
    o0 += ln_b.size
    conv_full = _unstack(small_flat[:, o0:o0 + mlstm_conv_w.size].reshape((N_DEV,) + mlstm_conv_w.shape), 2)[0]
    gbias =jnp.pad(mlstm_gate_bias, ((0, 0), (0, 128 - 2 * HEADS)))

    ncols = ada_w.shape[2]
    ada_b_cols = lax.dynamic_slice_in_dim(ada_b, me * ncols, ncols, axis=1).reshape(DEPTH, 1, ncols)
    mod_cols = ada_fwd(c_all, ada_w, ada_b_cols, "ada_fwd")
    (mod_g,) = all_gather([mod_cols.reshape(DEPTH * Bg, ncols)], "ag_mod")
    mod_full = _unstack(mod_g.reshape(N_DEV, DEPTH, Bg, ncols), 2)
    mod_mine = lax.dynamic_slice_in_dim(mod_full, me * Bl, Bl, axis=1).reshape(DEPTH, Bl, 3, 3, D)

    xt = x.reshape(T, D)
    saved = []
    for layer in range(DEPTH):
        def lnp(s, layer=layer):
            return lng_full[layer, s].reshape(1, D), lnb_full[layer, s].reshape(1, D)
        md = mod_mine[layer]
        if layer == 0:
            a, g, u, h, late = ffn_in(xt, md[:, 0], first_in, "f0a_in", gather=sends[0][1:])
            full[0] = gathered_weights([first_in] + late)
            out, xn = proj_post(a, full[0][1][0], xt, md[:, 0], *lnp(0), 0.5, "f0a_out")
            xt, sv0 = xn, (xt, out, g, u, h, a)
            mw_in = jnp.pad(_unstack(full[0][2], 1), ((0, 0), (0, M_PROJ_PAD - M_PROJ)))
        else:
            xt, sv0, _ = ffn_fwd(xt, md[:, 0], full[layer][0][0], full[layer][1][0], *lnp(0), "f%da" % layer)
        f_in, f_out, mix_in, mix_out = full[layer]
        if layer % 2 == 0:
            xt, sv1, g1 = mlstm_fwd(xt, md[:, 1], mw_in, mix_out, conv_full, mlstm_head_gain, gbias, *lnp(1), Bl, S,
                                    gather=sends[1])
            full[1] = gathered_weights(g1)
        else:
            xt, sv1 = attn_mixer_fwd(xt, md[:, 1], mix_in, mix_out, *lnp(1), Bl, S)
        xt, sv2, _ = ffn_fwd(xt, md[:, 2], f_in[1], f_out[1], *lnp(2), "f%db" % layer)
        saved.append((sv0, sv1, sv2))

    dxt, lsum = loss_head(xt, loss_target.reshape(T, D), "loss")
    loss = lax.psum(lsum[0, 0], MESH_AXES)

    dmod, dlg_all, dlb_all = [None] * DEPTH, [None] * DEPTH, [None] * DEPTH
    wgrads = [None] * DEPTH
    recvs = [[None] * 6 for _ in range(DEPTH)]
    ml_small = None
    for layer in reversed(range(DEPTH)):
        md = mod_mine[layer]
        f_in, f_out, mix_in, mix_out = full[layer]
        sv0, sv1, sv2 = saved[layer]
        dxt, dw2, dlg2, dlb2, dm2, _, _, _ = ffn_bwd(dxt, sv2, md[:, 2], f_in[1], f_out[1],
                                                     lng_full[layer, 2].reshape(1, D), "f%db" % layer)
        lg1 = lng_full[layer, 1].reshape(1, D)
        if layer % 2 == 0:
            dxt, dw1, dlg1, dlb1, dm1, ml_small, got = mlstm_bwd(
                dxt, sv1, md[:, 1], mw_in, mix_out, conv_full, mlstm_head_gain, gbias, lg1, Bl, S,
                exchange=[b16 for _, b16 in wgrads[1]])
            recvs[1] = got
            dxt, dw0, dlg0, dlb0, dm0, got, got2, own = ffn_bwd(
                dxt, sv0, md[:, 0], f_in[0], f_out[0], lng_full[layer, 0].reshape(1, D), "f%da" % layer,
                exchange=[dw2[0][1]], exchange2=[dw2[1][1], dw1[0][1], dw1[1][1]], exchange_own=True)
            (recvs[0][1],), (recvs[0][3], recvs[0][4], recvs[0][5]), (recvs[0][0], recvs[0][2]) = got, got2, own
        else:
            dxt, dw1, dlg1, dlb1, dm1 = attn_mixer_bwd(dxt, sv1, md[:, 1], mix_in, mix_out, lg1, Bl, S)
            dxt, dw0, dlg0, dlb0, dm0, _, _, _ = ffn_bwd(dxt, sv0, md[:, 0], f_in[0], f_out[0],
                                                         lng_full[layer, 0].reshape(1, D), "f%da" % layer)
        wgrads[layer] = [dw0[0], dw2[0], dw0[1], dw2[1], dw1[0], dw1[1]]
        dmod[layer] = jnp.stack([dm0, dm1, dm2], axis=1).reshape(Bl, 9 * D)
        dlg_all[layer] = jnp.concatenate([dlg0, dlg1, dlg2], axis=0)
        dlb_all[layer] = jnp.concatenate([dlb0, dlb1, dlb2], axis=0)
    grad_x = dxt.reshape(Bl, S, D)

    gsh = [[shard_sum(lax.dynamic_index_in_dim(f32, me, axis=0, keepdims=False), recv, onehot,
                      "rs_sum%d_%d" % (layer, i))
            for i, ((f32, _), recv) in enumerate(zip(wgrads[layer], recvs[layer]))] for layer in range(DEPTH)]
    grads = {"ffn_w_in": jnp.stack([jnp.stack(g[0:2]) for g in gsh]),
             "ffn_w_out": jnp.stack([jnp.stack(g[2:4]) for g in gsh]),
             "mlstm_w_in": gsh[0][4][None], "mlstm_w_out": gsh[0][5][None],
             "attn_w_in": gsh[1][4][None], "attn_w_out": gsh[1][5][None]}

    dconv, dgain, dgbias = ml_small
    parts = [jnp.stack(dmod).reshape(-1), dgbias.reshape(-1), dgain.reshape(-1),
             jnp.stack(dlg_all).reshape(-1), jnp.stack(dlb_all).reshape(-1), dconv.reshape(-1)]
    sizes = [p.shape[0] for p in parts]
    flat = jnp.concatenate(parts)
    flat = jnp.pad(flat, (0, -flat.shape[0] % (8 * PACK_COLS))).reshape(-1, PACK_COLS)
    (sm_all,) = all_gather([flat], "ag_small")
    sm_sum = sum_leading(sm_all, "small_sum").reshape(-1)
    dmod_all = sm_all.reshape(N_DEV, -1)[:, :sizes[0]].reshape(N_DEV, DEPTH, Bl, 9 * D)
    dmod_all = jnp.moveaxis(dmod_all, 0, 1).reshape(DEPTH, Bg, 9 * D)
    o0 = sizes[0]
    grads["mlstm_gate_bias"] = sm_sum[o0:o0 + sizes[1]].reshape(mlstm_gate_bias.shape)
    o0 += sizes[1]
    grads["mlstm_head_gain"] = sm_sum[o0:o0 + sizes[2]].reshape(mlstm_head_gain.shape)
    o0 += sizes[2]
    nl = ln_g.shape[2]
    g_lng = sm_sum[o0:o0 + sizes[3]].reshape(DEPTH, 3, D)
    o0 += sizes[3]
    g_lnb = sm_sum[o0:o0 + sizes[4]].reshape(DEPTH, 3, D)
    o0 += sizes[4]
    g_conv = sm_sum[o0:o0 + sizes[5]].reshape(1, 4, D)
    grads["ln_g"] = lax.dynamic_slice_in_dim(g_lng, me * nl, nl, axis=2)
    grads["ln_b"] = lax.dynamic_slice_in_dim(g_lnb, me * nl, nl, axis=2)
    grads["mlstm_conv_w"] = lax.dynamic_slice_in_dim(g_conv, me * nl, nl, axis=2)
    dmod_cols = lax.dynamic_slice_in_dim(dmod_all, me * ncols, ncols, axis=2)
    gw, gb = ada_bwd(c_all.T, dmod_cols, dmod_all, "ada_bwd")
    grads["ada_w"] = gw
    grads["ada_b"] = gb.reshape(ada_b.shape)

    names = ["ada_w", "ada_b", "ln_g", "ln_b", "ffn_w_in", "ffn_w_out", "mlstm_w_in", "mlstm_gate_bias",
             "mlstm_conv_w", "mlstm_head_gain", "mlstm_w_out", "attn_w_in", "attn_w_out"]
    deltas, new_m, new_v = [], [], []
    for k in names:
        w = weights[k]
        shp2 = (math.prod(w.shape[:-1]), w.shape[-1])
        d_, m_, v_ = adamw(w.reshape(shp2), grads[k].reshape(shp2), m_in[k].reshape(shp2), v_in[k].reshape(shp2),
                           "adamw_" + k)
        deltas.append(d_.reshape(w.shape))
        new_m.append(m_.reshape(w.shape))
        new_v.append(v_.reshape(w.shape))
    return (loss, grad_x, *[grads[k] for k in names], *deltas, *new_m, *new_v)
```

```python
import functools
import math

import jax
import jax.numpy as jnp
from jax import lax
from jax.experimental import pallas as pl
from jax.experimental.pallas import tpu as pltpu

F32 = jnp.float32
BF16 = jnp.bfloat16

N_DEV = 8
MESH_AXES = ("x", "y", "c")
D = 1024
DEPTH = 2
D_FF = 2816
HEADS = 8
M_DQK = 64
M_DV = 128
M_CHUNK = 64
M_SLAB = 512
M_PROJ = 3088
M_PROJ_PAD = 3200
A_PROJ = 9216
DIL_GROUPS = ((128, 1), (512, 4), (2048, 16))
A_BLK = 128
A_UNROLL = 8
ALPHA = (2 * DEPTH) ** 0.25
LN_EPS = 1e-5
RMS_EPS = 1e-6
ADAM_LR = 0.001
ADAM_B1 = 0.9
ADAM_B2 = 0.999
ADAM_EPS = 1e-08
ADAM_WD = 0.01
ADAM_STEP = 10
NEG = -1e30
V7X_VMEM_LIMIT = 56 * 1024 * 1024
PACK_COLS = 1024
MESH_ID = pl.DeviceIdType.MESH
ANY_SPEC = pl.BlockSpec(memory_space=pl.ANY)


def _cp(n_axes):
    return pltpu.CompilerParams(dimension_semantics=("arbitrary",) * n_axes,
                                vmem_limit_bytes=V7X_VMEM_LIMIT)


def _dot(a, b):
    return jnp.dot(a, b, preferred_element_type=F32)


def _dot_nt(a, b):
    return lax.dot_general(a, b, (((1,), (1,)), ((), ())), preferred_element_type=F32)


def _dot_tn(a, b):
    return lax.dot_general(a, b, (((0,), (0,)), ((), ())), preferred_element_type=F32)


def _sum0(a):
    return jnp.sum(a, axis=0, keepdims=True)


def _sum1(a):
    return jnp.sum(a, axis=1, keepdims=True)


def _round(a):
    return a.astype(BF16).astype(F32)


def _sigmoid(a):
    return 1.0 / (1.0 + jnp.exp(-a))


def _tile(n, pref):
    t = min(n, pref)
    while n % t:
        t //= 2
    return t


def all_gather(arrs, name):
    n = len(arrs)

    def body(*refs):
        gather = Gather(refs[:n], refs[n:2 * n], *refs[2 * n:])
        gather.start()
        gather.finish()

    return pl.pallas_call(
        body, name=name, out_shape=Gather.out_shape(arrs),
        in_specs=[ANY_SPEC] * n, out_specs=[ANY_SPEC] * n, scratch_shapes=Gather.scratch(n),
    )(*arrs)


class Gather:
    def __init__(self, ins, outs, send_sems, recv_sems, local_sems):
        x, y, c = lax.axis_index("x"), lax.axis_index("y"), lax.axis_index("c")
        me, sibling = (x, y, c), (x, y, 1 - c)
        chips = [(1 - x, y), (x, 1 - y), (1 - x, 1 - y)]

        def slot(a, p):
            return outs[a].at[4 * p[0] + 2 * p[1] + p[2]]

        def copy(a, k, block, to, src=None):
            return pltpu.make_async_remote_copy(
                src_ref=slot(a, block) if src is None else src, dst_ref=slot(a, block),
                send_sem=send_sems.at[7 * a + k], recv_sem=recv_sems.at[7 * a + k],
                device_id=to, device_id_type=MESH_ID)

        n = len(ins)
        self.mine = [pltpu.make_async_copy(ins[a], slot(a, me), local_sems.at[a]) for a in range(n)]
        self.first, self.over_ici, self.passed, self.from_sibling = [], [], [], []
        for a in range(n):
            self.first.append(copy(a, 0, me, sibling, src=ins[a]))
            self.from_sibling.append(copy(a, 0, sibling, me))
            for j, chip in enumerate(chips):
                self.first.append(copy(a, 1 + j, me, (*chip, c), src=ins[a]))
                self.over_ici.append(copy(a, 1 + j, (*chip, c), me))
                self.passed.append(copy(a, 4 + j, (*chip, c), sibling))
                self.from_sibling.append(copy(a, 4 + j, (*chip, 1 - c), me))

    @staticmethod
    def out_shape(arrs):
        return [jax.ShapeDtypeStruct((N_DEV,) + a.shape, a.dtype) for a in arrs]

    @staticmethod
    def scratch(n):
        return [pltpu.SemaphoreType.DMA((7 * n,)), pltpu.SemaphoreType.DMA((7 * n,)),
                pltpu.SemaphoreType.DMA((n,))]

    def start(self):
        for cp in self.mine + self.first:
            cp.start()

    def finish(self):
        for landed, onward in zip(self.over_ici, self.passed):
            landed.wait_recv()
            onward.start()
        for cp in self.from_sibling:
            cp.wait_recv()
        for cp in self.first + self.passed:
            cp.wait_send()
        for cp in self.mine:
            cp.wait()


class Exchange:
    def __init__(self, sends, recvs, send_sems, recv_sems, local_sems):
        x, y, c = lax.axis_index("x"), lax.axis_index("y"), lax.axis_index("c")
        me = 4 * x + 2 * y + c
        self.own = [pltpu.make_async_copy(s.at[me], r.at[me], local_sems.at[a])
                    for a, (s, r) in enumerate(zip(sends, recvs))]
        self.copies = []
        for a, (s_ref, r_ref) in enumerate(zip(sends, recvs)):
            for k in range(1, N_DEV):
                px = 1 - x if (k >> 2) & 1 else x
                py = 1 - y if (k >> 1) & 1 else y
                pc = 1 - c if k & 1 else c
                self.copies.append(pltpu.make_async_remote_copy(
                    src_ref=s_ref.at[4 * px + 2 * py + pc], dst_ref=r_ref.at[me],
                    send_sem=send_sems.at[7 * a + k - 1], recv_sem=recv_sems.at[7 * a + k - 1],
                    device_id=(px, py, pc), device_id_type=MESH_ID))

    @staticmethod
    def scratch(n):
        return [pltpu.SemaphoreType.DMA((7 * n,)), pltpu.SemaphoreType.DMA((7 * n,)),
                pltpu.SemaphoreType.DMA((n,))]

    def start(self):
        for cp in self.own + self.copies:
            cp.start()

    def finish(self):
        for cp in self.copies:
            cp.wait_send()
            cp.wait_recv()
        for cp in self.own:
            cp.wait()


def host_comm(body, grid, n_in, n_out, gather=(), exchange=()):
    ng, nx = len(gather), len(exchange)
    if ng + nx == 0:
        return body, [], [], [], []

    def hosted(*refs):
        ins, c_in, rest = refs[:n_in], refs[n_in:n_in + ng + nx], refs[n_in + ng + nx:]
        outs, c_out, rest = rest[:n_out], rest[n_out:n_out + ng + nx], rest[n_out + ng + nx:]
        n_sems = 3 * ((ng > 0) + (nx > 0))
        scratch, sems = rest[:len(rest) - n_sems], rest[len(rest) - n_sems:]

        def comms():
            made = [Gather(c_in[:ng], c_out[:ng], *sems[:3])] if ng else []
            return made + ([Exchange(c_in[ng:], c_out[ng:], *sems[-3:])] if nx else [])

        ids = [pl.program_id(a) for a in range(len(grid))]

        @pl.when(functools.reduce(jnp.logical_and, [i == 0 for i in ids]))
        def _():
            for cm in comms():
                cm.start()
        body(*ins, *outs, *scratch)

        @pl.when(functools.reduce(jnp.logical_and, [i == g - 1 for i, g in zip(ids, grid)]))
        def _():
            for cm in comms():
                cm.finish()

    shapes = Gather.out_shape(gather) + [jax.ShapeDtypeStruct(a.shape, a.dtype) for a in exchange]
    scratch = (Gather.scratch(ng) if ng else []) + (Exchange.scratch(nx) if nx else [])
    return hosted, [ANY_SPEC] * (ng + nx), [ANY_SPEC] * (ng + nx), shapes, scratch


def shard_sum(own, recv, onehot, name):
    R, C = own.shape
    tr = _tile(R, 512)

    def body(oh_ref, own_ref, recv_ref, o_ref):
        acc = None
        for j in range(N_DEV):
            term = jnp.where(oh_ref[j] > 0.5, own_ref[...], recv_ref[j].astype(F32))
            acc = term if acc is None else acc + term
        o_ref[...] = acc

    return pl.pallas_call(
        body, name=name, grid=(R // tr,),
        in_specs=[pl.BlockSpec(memory_space=pltpu.SMEM),
                  pl.BlockSpec((tr, C), lambda i: (i, 0)),
                  pl.BlockSpec((N_DEV, tr, C), lambda i: (0, i, 0))],
        out_specs=pl.BlockSpec((tr, C), lambda i: (i, 0)),
        out_shape=jax.ShapeDtypeStruct((R, C), F32), compiler_params=_cp(1),
    )(onehot, own, recv)


def sum_leading(a, name):
    _, R, C = a.shape

    def body(a_ref, o_ref):
        acc = a_ref[0]
        for j in range(1, N_DEV):
            acc = acc + a_ref[j]
        o_ref[...] = acc

    return pl.pallas_call(body, name=name, out_shape=jax.ShapeDtypeStruct((R, C), F32),
                          compiler_params=_cp(0))(a)


def _col_chunks(w, tn):
    if w.ndim == 3:
        return w.shape[0], w.shape[2], pl.BlockSpec((None, w.shape[1], w.shape[2]), lambda i, j: (j, 0, 0))
    return w.shape[1] // tn, tn, pl.BlockSpec((w.shape[0], tn), lambda i, j: (0, j))


def modmm(x, mod3, w, out_dtype, name, tn=None):
    T, Dm = x.shape
    nj, tn, w_spec = _col_chunks(w, tn)
    N = nj * tn
    Bl = mod3.shape[0]
    tm = _tile(T // Bl, 1024)
    tpb = T // Bl // tm

    def body(x_ref, mod_ref, w_ref, o_ref, h_ref, hs):
        @pl.when(pl.program_id(1) == 0)
        def _():
            m = mod_ref[0]
            hs[...] = (x_ref[...] * (1.0 + m[1:2, :]) + m[0:1, :]).astype(BF16)
            h_ref[...] = hs[...]
        o_ref[...] = _dot(hs[...], w_ref[...]).astype(o_ref.dtype)

    return pl.pallas_call(
        body, name=name, grid=(T // tm, nj),
        in_specs=[pl.BlockSpec((tm, Dm), lambda i, j: (i, 0)),
                  pl.BlockSpec((1, 3, Dm), lambda i, j: (i // tpb, 0, 0)), w_spec],
        out_specs=[pl.BlockSpec((tm, tn), lambda i, j: (i, j)),
                   pl.BlockSpec((tm, Dm), lambda i, j: (i, 0))],
        out_shape=[jax.ShapeDtypeStruct((T, N), out_dtype), jax.ShapeDtypeStruct((T, Dm), BF16)],
        scratch_shapes=[pltpu.VMEM((tm, Dm), BF16)], compiler_params=_cp(2),
    )(x, mod3, w)


def modmm_bwd(dp, w, x, mod3, dxres, name, tn=None, tm=1024, exchange=()):
    T, Dm = x.shape
    Bl = mod3.shape[0]
    tm = _tile(T // Bl, tm)
    tpb = T // Bl // tm
    resident = dp.ndim == 3
    if resident:
        nc, nj = dp.shape[0], 1
        dp_spec = pl.BlockSpec((nc, tm, dp.shape[2]), lambda i, j: (0, i, 0))
        w_spec = pl.BlockSpec(w.shape, lambda i, j: (0, 0, 0))
    else:
        nj, tn, w_spec = _col_chunks(w, tn)
        dp_spec = pl.BlockSpec((tm, tn), lambda i, j: (i, j))

    def body(dp_ref, w_ref, x_ref, mod_ref, dxr_ref, dx_ref, dsh_ref, dsc_ref, acc):
        i, j = pl.program_id(0), pl.program_id(1)

        @pl.when(j == 0)
        def _():
            acc[...] = jnp.zeros_like(acc)
        if resident:
            for c in range(nc):
                acc[...] += _dot_nt(dp_ref[c], w_ref[c])
        else:
            acc[...] += _dot_nt(dp_ref[...], w_ref[...])

        @pl.when(j == nj - 1)
        def _():
            dh = acc[...]
            xx = x_ref[...]
            dx_ref[...] = dxr_ref[...] + dh * (1.0 + mod_ref[0][1:2, :])

            @pl.when(i % tpb == 0)
            def _():
                dsh_ref[...] = jnp.zeros_like(dsh_ref)
                dsc_ref[...] = jnp.zeros_like(dsc_ref)
            dsh_ref[0] += _sum0(dh)
            dsc_ref[0] += _sum0(dh * xx)

    grid = (T // tm, nj)
    body, c_in, c_out, c_shape, c_scratch = host_comm(body, grid, 5, 3, exchange=exchange)
    dx, dsh, dsc, *received = pl.pallas_call(
        body, name=name, grid=grid,
        in_specs=[dp_spec, w_spec,
                  pl.BlockSpec((tm, Dm), lambda i, j: (i, 0)),
                  pl.BlockSpec((1, 3, Dm), lambda i, j: (i // tpb, 0, 0)),
                  pl.BlockSpec((tm, Dm), lambda i, j: (i, 0))] + c_in,
        out_specs=[pl.BlockSpec((tm, Dm), lambda i, j: (i, 0)),
                   pl.BlockSpec((1, 1, Dm), lambda i, j: (i // tpb, 0, 0)),
                   pl.BlockSpec((1, 1, Dm), lambda i, j: (i // tpb, 0, 0))] + c_out,
        out_shape=[jax.ShapeDtypeStruct((T, Dm), F32), jax.ShapeDtypeStruct((Bl, 1, Dm), F32),
                   jax.ShapeDtypeStruct((Bl, 1, Dm), F32)] + c_shape,
        scratch_shapes=[pltpu.VMEM((tm, Dm), F32)] + c_scratch, compiler_params=_cp(2),
    )(dp, w, x, mod3, dxres, *exchange)
    return (dx, dsh, dsc, received) if exchange else (dx, dsh, dsc)


def _ln_stats(z):
    mu = jnp.mean(z, axis=-1, keepdims=True)
    zc = z - mu
    var = jnp.mean(zc * zc, axis=-1, keepdims=True)
    rstd = lax.rsqrt(var + LN_EPS)
    return zc * rstd, rstd


def proj_post(a, w, x, mod3, lng, lnb, weight, name):
    nk, T, tk = a.shape
    Dm = w.shape[2]
    Bl = mod3.shape[0]
    tm = _tile(T // Bl, 512)
    tpb = T // Bl // tm

    def body(a_ref, w_ref, x_ref, mod_ref, g_ref, b_ref, out_ref, xn_ref):
        out = _dot(a_ref[0], w_ref[0])
        for k in range(1, nk):
            out = out + _dot(a_ref[k], w_ref[k])
        out_ref[...] = out
        z = ALPHA * x_ref[...] + (weight * (1.0 + mod_ref[0][2:3, :])) * out
        xhat, _ = _ln_stats(z)
        xn_ref[...] = xhat * g_ref[...] + b_ref[...]

    row = pl.BlockSpec((tm, Dm), lambda i: (i, 0))
    vec = pl.BlockSpec((1, Dm), lambda i: (0, 0))
    return pl.pallas_call(
        body, name=name, grid=(T // tm,),
        in_specs=[pl.BlockSpec((nk, tm, tk), lambda i: (0, i, 0)),
                  pl.BlockSpec((nk, tk, Dm), lambda i: (0, 0, 0)),
                  row, pl.BlockSpec((1, 3, Dm), lambda i: (i // tpb, 0, 0)), vec, vec],
        out_specs=[row, row],
        out_shape=[jax.ShapeDtypeStruct((T, Dm), F32), jax.ShapeDtypeStruct((T, Dm), F32)],
        compiler_params=_cp(1),
    )(a, w, x, mod3, lng, lnb)


def post_bwd(dxn, x, out, mod3, lng, w, weight, name, tm=512, gu=None, exchange=()):
    T, Dm = x.shape
    nk, tk, _ = w.shape
    Bl = mod3.shape[0]
    tm = _tile(T // Bl, tm)
    tpb = T // Bl // tm
    fused = gu is not None

    def body(dxn_ref, x_ref, out_ref, mod_ref, g_ref, w_ref, *rest):
        if fused:
            gg_ref, uu_ref = rest[:2]
            rest = rest[2:]
        dxr_ref, dout_ref, da_ref, dg_ref, db_ref, dgate_ref = rest
        i = pl.program_id(0)
        out = out_ref[...]
        dxn = dxn_ref[...]
        coef = weight * (1.0 + mod_ref[0][2:3, :])
        xhat, rstd = _ln_stats(ALPHA * x_ref[...] + coef * out)
        dyh = dxn * g_ref[...]
        dz = rstd * (dyh - jnp.mean(dyh, axis=-1, keepdims=True)
                     - xhat * jnp.mean(dyh * xhat, axis=-1, keepdims=True))
        dxr_ref[...] = ALPHA * dz
        dout = (coef * dz).astype(BF16)
        dout_ref[...] = dout

        @pl.when(i == 0)
        def _():
            dg_ref[...] = jnp.zeros_like(dg_ref)
            db_ref[...] = jnp.zeros_like(db_ref)

        @pl.when(i % tpb == 0)
        def _():
            dgate_ref[...] = jnp.zeros_like(dgate_ref)
        dg_ref[...] += _sum0(dxn * xhat)
        db_ref[...] += _sum0(dxn)
        dgate_ref[0] += _sum0((weight * out) * dz)
        for k in range(nk):
            da = _dot_nt(dout, w_ref[k])
            if fused:
                gg = gg_ref[k].astype(F32)
                s = _sigmoid(gg)
                da_ref[k] = (da * uu_ref[k].astype(F32) * (s * (1.0 + gg * (1.0 - s)))).astype(BF16)
                da_ref[nk + k] = (da * (gg * s)).astype(BF16)
            else:
                da_ref[k] = da.astype(BF16)

    row = pl.BlockSpec((tm, Dm), lambda i: (i, 0))
    vec = pl.BlockSpec((1, Dm), lambda i: (0, 0))
    wide = pl.BlockSpec((nk, tm, tk), lambda i: (0, i, 0))
    nda = 2 * nk if fused else nk
    grid = (T // tm,)
    body, c_in, c_out, c_shape, c_scratch = host_comm(body, grid, 8 if fused else 6, 6, exchange=exchange)
    *results, = pl.pallas_call(
        body, name=name, grid=grid,
        in_specs=[row, row, row, pl.BlockSpec((1, 3, Dm), lambda i: (i // tpb, 0, 0)), vec,
                  pl.BlockSpec((nk, tk, Dm), lambda i: (0, 0, 0))] + ([wide, wide] if fused else []) + c_in,
        out_specs=[row, row, pl.BlockSpec((nda, tm, tk), lambda i: (0, i, 0)),
                   vec, vec, pl.BlockSpec((1, 1, Dm), lambda i: (i // tpb, 0, 0))] + c_out,
        out_shape=[jax.ShapeDtypeStruct((T, Dm), F32), jax.ShapeDtypeStruct((T, Dm), BF16),
                   jax.ShapeDtypeStruct((nda, T, tk), BF16), jax.ShapeDtypeStruct((1, Dm), F32),
                   jax.ShapeDtypeStruct((1, Dm), F32), jax.ShapeDtypeStruct((Bl, 1, Dm), F32)] + c_shape,
        scratch_shapes=c_scratch, compiler_params=_cp(1),
    )(dxn, x, out, mod3, lng, w, *(gu if fused else ()), *exchange)
    return tuple(results[:6]) + ((results[6:],) if exchange else ())


def mm_tn(a, b, name, bw=None, a_copies=False, exchange=()):
    a3, b3 = a.ndim == 3, b.ndim == 3
    nk, T, tk = a.shape if a3 else (1,) + a.shape
    if a_copies:
        nk = 1
    nc, wn = (b.shape[0], b.shape[2]) if b3 else (b.shape[1] // bw, bw)
    tt = _tile(T, 2048)
    nt = T // tt

    def body(a_ref, b_ref, o_ref, ob_ref):
        t = pl.program_id(2)

        @pl.when(t == 0)
        def _():
            o_ref[...] = jnp.zeros_like(o_ref)
        o_ref[...] += _dot_tn(a_ref[...], b_ref[...])

        @pl.when(t == nt - 1)
        def _():
            ob_ref[...] = o_ref[...].astype(BF16)

    a_spec = (pl.BlockSpec((None, tt, tk), lambda k, c, t: (k, t, 0)) if a3
              else pl.BlockSpec((tt, tk), lambda k, c, t: (t, 0)))
    b_spec = (pl.BlockSpec((None, tt, wn), lambda k, c, t: (c, t, 0)) if b3
              else pl.BlockSpec((tt, wn), lambda k, c, t: (t, c)))
    o_spec = pl.BlockSpec((None, tk, wn), lambda k, c, t: (k * nc + c, 0, 0))
    grid = (nk, nc, nt)
    body, c_in, c_out, c_shape, c_scratch = host_comm(body, grid, 2, 2, exchange=exchange)
    o32, o16, *received = pl.pallas_call(
        body, name=name, grid=grid, in_specs=[a_spec, b_spec] + c_in, out_specs=[o_spec, o_spec] + c_out,
        out_shape=[jax.ShapeDtypeStruct((nk * nc, tk, wn), F32), jax.ShapeDtypeStruct((nk * nc, tk, wn), BF16)]
        + c_shape,
        scratch_shapes=c_scratch, compiler_params=_cp(3),
    )(a, b, *exchange)
    return (o32, o16, received) if exchange else (o32, o16)


def ffn_in(x, mod3, w, name, gather=()):
    T, Dm = x.shape
    nj, tf = w.shape[0] // 2, w.shape[2]
    Bl = mod3.shape[0]
    tm = _tile(T // Bl, 1024)
    tpb = T // Bl // tm

    def body(x_ref, mod_ref, wg_ref, wu_ref, a_ref, g_ref, u_ref, h_ref):
        m = mod_ref[0]
        h = (x_ref[...] * (1.0 + m[1:2, :]) + m[0:1, :]).astype(BF16)
        h_ref[...] = h
        g = _dot(h, wg_ref[...])
        u = _dot(h, wu_ref[...])
        a_ref[...] = (g * _sigmoid(g) * u).astype(BF16)
        g_ref[...] = g.astype(BF16)
        u_ref[...] = u.astype(BF16)

    col = pl.BlockSpec((None, tm, tf), lambda j, i: (j, i, 0))
    grid = (nj, T // tm)
    body, c_in, c_out, c_shape, c_scratch = host_comm(body, grid, 4, 4, gather=gather)
    a, g, u, h, *gathered = pl.pallas_call(
        body, name=name, grid=grid,
        in_specs=[pl.BlockSpec((tm, Dm), lambda j, i: (i, 0)),
                  pl.BlockSpec((1, 3, Dm), lambda j, i: (i // tpb, 0, 0)),
                  pl.BlockSpec((None, Dm, tf), lambda j, i: (j, 0, 0)),
                  pl.BlockSpec((None, Dm, tf), lambda j, i: (nj + j, 0, 0))] + c_in,
        out_specs=[col, col, col, pl.BlockSpec((None, tm, Dm), lambda j, i: (j, i, 0))] + c_out,
        out_shape=[jax.ShapeDtypeStruct((nj, T, tf), BF16)] * 3 + [jax.ShapeDtypeStruct((nj, T, Dm), BF16)]
        + c_shape,
        scratch_shapes=c_scratch, compiler_params=_cp(2),
    )(x, mod3, w, w, *gather)
    return a, g, u, h, gathered


def loss_head(y, tgt, name):
    T, Dm = y.shape
    tm = _tile(T, 512)
    nt = T // tm

    def body(y_ref, t_ref, dy_ref, l_ref, acc):
        i = pl.program_id(0)

        @pl.when(i == 0)
        def _():
            acc[...] = jnp.zeros_like(acc)
        e = y_ref[...] - t_ref[...]
        dy_ref[...] = e * (1.0 / Dm)
        acc[...] += _sum0(e * e)

        @pl.when(i == nt - 1)
        def _():
            l_ref[...] = jnp.broadcast_to(_sum1(acc[...]) * (0.5 / Dm), l_ref.shape)

    return pl.pallas_call(
        body, name=name, grid=(nt,),
        in_specs=[pl.BlockSpec((tm, Dm), lambda i: (i, 0)), pl.BlockSpec((tm, Dm), lambda i: (i, 0))],
        out_specs=[pl.BlockSpec((tm, Dm), lambda i: (i, 0)), pl.BlockSpec((1, 128), lambda i: (0, 0))],
        out_shape=[jax.ShapeDtypeStruct((T, Dm), F32), jax.ShapeDtypeStruct((1, 128), F32)],
        scratch_shapes=[pltpu.VMEM((1, Dm), F32)], compiler_params=_cp(1),
    )(y, tgt)


def adamw(w, g, m, v, name):
    R, C = w.shape
    tr = _tile(R, 512) if R % 8 == 0 else R

    def body(w_ref, g_ref, m_ref, v_ref, d_ref, nm_ref, nv_ref):
        gg = g_ref[...]
        mm = ADAM_B1 * m_ref[...] + (1.0 - ADAM_B1) * gg
        vv = ADAM_B2 * v_ref[...] + (1.0 - ADAM_B2) * (gg * gg)
        m_hat = mm / (1.0 - ADAM_B1 ** ADAM_STEP)
        v_hat = vv / (1.0 - ADAM_B2 ** ADAM_STEP)
        d_ref[...] = -ADAM_LR * (m_hat / (jnp.sqrt(v_hat) + ADAM_EPS) + ADAM_WD * w_ref[...])
        nm_ref[...] = mm
        nv_ref[...] = vv

    spec = pl.BlockSpec((tr, C), lambda i: (i, 0))
    return pl.pallas_call(
        body, name=name, grid=(R // tr,), in_specs=[spec] * 4, out_specs=[spec] * 3,
        out_shape=[jax.ShapeDtypeStruct((R, C), F32)] * 3, compiler_params=_cp(1),
    )(w, g, m, v)


def ada_fwd(c_all, ada_w, ada_b_cols, name):
    Lr, Dm, Nc = ada_w.shape
    Bg = c_all.shape[0]

    def body(c_ref, w_ref, b_ref, o_ref):
        cc = c_ref[...]
        cond = cc * _sigmoid(cc)
        o_ref[0] = _dot(cond.astype(BF16), w_ref[0].astype(BF16)) + b_ref[0]

    return pl.pallas_call(
        body, name=name, grid=(Lr,),
        in_specs=[pl.BlockSpec((Bg, Dm), lambda l: (0, 0)),
                  pl.BlockSpec((1, Dm, Nc), lambda l: (l, 0, 0)),
                  pl.BlockSpec((1, 1, Nc), lambda l: (l, 0, 0))],
        out_specs=pl.BlockSpec((1, Bg, Nc), lambda l: (l, 0, 0)),
        out_shape=jax.ShapeDtypeStruct((Lr, Bg, Nc), F32), compiler_params=_cp(1),
    )(c_all, ada_w, ada_b_cols)


def ada_bwd(c_all_t, dmod_cols, dmod_all, name):
    Dm, Bg = c_all_t.shape
    Lr, _, Nc = dmod_cols.shape
    Nf = dmod_all.shape[2]

    def body(c_ref, dm_ref, da_ref, gw_ref, gb_ref):
        cc = c_ref[...]
        cond = cc * _sigmoid(cc)
        gw_ref[0] = _dot(cond.astype(BF16), dm_ref[0].astype(BF16))
        gb_ref[0] = _sum0(da_ref[0])

    return pl.pallas_call(
        body, name=name, grid=(Lr,),
        in_specs=[pl.BlockSpec((Dm, Bg), lambda l: (0, 0)),
                  pl.BlockSpec((1, Bg, Nc), lambda l: (l, 0, 0)),
                  pl.BlockSpec((1, Bg, Nf), lambda l: (l, 0, 0))],
        out_specs=[pl.BlockSpec((1, Dm, Nc), lambda l: (l, 0, 0)),
                   pl.BlockSpec((1, 1, Nf), lambda l: (l, 0, 0))],
        out_shape=[jax.ShapeDtypeStruct((Lr, Dm, Nc), F32), jax.ShapeDtypeStruct((Lr, 1, Nf), F32)],
        compiler_params=_cp(1),
    )(c_all_t, dmod_cols, dmod_all)


def _conv_taps(x, w, rows):
    shifted = [x]
    c = w[3:4, :] * x
    for k in range(1, 4):
        xs = jnp.where(rows >= k, pltpu.roll(x, k, 0), 0.0)
        shifted.append(xs)
        c = c + w[3 - k:4 - k, :] * xs
    return c, shifted


def conv_silu(proj3, conv_w, name):
    Bl, S, _ = proj3.shape
    ncb = conv_w.shape[1] // 128

    def body(x_ref, w_ref, o_ref):
        rows = lax.broadcasted_iota(jnp.int32, (S, 128), 0)
        c, _ = _conv_taps(_round(x_ref[0]), _round(w_ref[...]), rows)
        o_ref[0] = c * _sigmoid(c)

    return pl.pallas_call(
        body, name=name, grid=(Bl, ncb),
        in_specs=[pl.BlockSpec((1, S, 128), lambda b, j: (b, 0, j)),
                  pl.BlockSpec((4, 128), lambda b, j: (0, j))],
        out_specs=pl.BlockSpec((1, S, 128), lambda b, j: (b, 0, j)),
        out_shape=jax.ShapeDtypeStruct((Bl, S, conv_w.shape[1]), F32), compiler_params=_cp(2),
    )(proj3, conv_w)


def conv_silu_bwd(proj3, conv_w, dq, dk, name):
    Bl, S, _ = proj3.shape
    nq = dq.shape[2] // 128

    def body(x_ref, w_ref, dq_ref, dk_ref, dx_ref, dw_ref):
        j = pl.program_id(1)
        rows = lax.broadcasted_iota(jnp.int32, (S, 128), 0)
        w = _round(w_ref[...])
        c, shifted = _conv_taps(_round(x_ref[0]), w, rows)
        s = _sigmoid(c)
        dact = jnp.where(j < nq, dq_ref[0], dk_ref[0])
        dc = _round(dact * (s * (1.0 + c * (1.0 - s))))
        dx = w[3:4, :] * dc
        dws = [_sum0(dc * shifted[0])]
        for k in range(1, 4):
            up = jnp.where(rows < S - k, pltpu.roll(dc, S - k, 0), 0.0)
            dx = dx + w[3 - k:4 - k, :] * up
            dws.append(_sum0(dc * shifted[k]))
        dx_ref[0] = dx.astype(BF16)
        tap = lax.broadcasted_iota(jnp.int32, (4, 128), 0)
        dw_ref[0] = functools.reduce(lambda a, b: a + b, [jnp.where(tap == 3 - k, dws[k], 0.0) for k in range(4)])

    return pl.pallas_call(
        body, name=name, grid=(Bl, 2 * nq),
        in_specs=[pl.BlockSpec((1, S, 128), lambda b, j: (b, 0, j)),
                  pl.BlockSpec((4, 128), lambda b, j: (0, j)),
                  pl.BlockSpec((1, S, 128), lambda b, j: (b, 0, jnp.minimum(j, nq - 1))),
                  pl.BlockSpec((1, S, 128), lambda b, j: (b, 0, jnp.maximum(j - nq, 0)))],
        out_specs=[pl.BlockSpec((1, S, 128), lambda b, j: (b, 0, j)),
                   pl.BlockSpec((1, 4, 128), lambda b, j: (b, 0, j))],
        out_shape=[jax.ShapeDtypeStruct((Bl, S, 2 * nq * 128), BF16),
                   jax.ShapeDtypeStruct((Bl, 4, 2 * nq * 128), F32)],
        compiler_params=_cp(2),
    )(proj3, conv_w, dq, dk)


def _log_sigmoid(a):
    return jnp.minimum(a, 0.0) - jnp.log(1.0 + jnp.exp(-jnp.abs(a)))


def _interleave(gens):
    live = list(gens)
    while live:
        still = []
        for g in live:
            try:
                next(g)
                still.append(g)
            except StopIteration:
                pass
        live = still


def _finish(gen):
    while True:
        try:
            next(gen)
        except StopIteration as done:
            return done.value


def _chunk_state(kc, vc, gi, bcum, b_last, C, n, m):
    a = b_last - bcum + gi
    m_loc = jnp.max(a, axis=0, keepdims=True)
    wa = jnp.exp(a - m_loc)
    c_loc = _dot_tn((wa * vc).astype(BF16), kc.astype(BF16))
    n_loc = _sum0(_round(wa) * _round(kc))
    m_new = jnp.maximum(b_last + m, m_loc)
    sp = jnp.exp(b_last + m - m_new)
    sl = jnp.exp(m_loc - m_new)
    yield
    return sp * C + sl * c_loc, sp * n + sl * n_loc, m_new, wa, sp, sl


def _chunk_out(qs, kc, vc, gi_row, bcum, bcum_row, low, C, n, m):
    inter_log = bcum + m
    dlog = jnp.where(low, bcum - bcum_row + gi_row, NEG)
    m_i = jnp.maximum(inter_log, jnp.max(dlog, axis=1, keepdims=True))
    dm = jnp.exp(dlog - m_i)
    iw = jnp.exp(inter_log - m_i)
    qs_b, k_b, v_b = qs.astype(BF16), kc.astype(BF16), vc.astype(BF16)
    sqk = _dot_nt(qs_b, k_b)
    qc_ = _dot_nt(qs_b, C.astype(BF16))
    qn = _sum1(_round(qs) * _round(n))
    floor = jnp.exp(-m_i)
    yield
    sc = sqk * dm
    sv = _dot(sc.astype(BF16), v_b)
    den = _sum1(sc) + iw * qn
    dn = jnp.maximum(jnp.abs(den), floor)
    yield
    num = sv + iw * qc_
    return dict(hc=num / dn, den=den, dn=dn, floor=floor, sc=sc, dm=dm, iw=iw, qc=qc_, qn=qn,
                qs_b=qs_b, k_b=k_b, v_b=v_b)


def _cell_consts(L):
    ri = lax.broadcasted_iota(jnp.int32, (L, L), 0)
    ci = lax.broadcasted_iota(jnp.int32, (L, L), 1)
    return ri == ci, ci <= ri, ri <= ci


def _load_chunk(q_ref, k_ref, v_ref, G, off, L, h, lane):
    hh = h % 2
    qmask = (lane >= M_DQK * hh) & (lane < M_DQK * (hh + 1))
    pair = pl.ds(128 * (h // 2), 128)
    qc = jnp.where(qmask, q_ref[0, pl.ds(off, L), pair], 0.0)
    kc = jnp.where(qmask, k_ref[0, pl.ds(off, L), pair], 0.0)
    vc = v_ref[0, pl.ds(off, L), pl.ds(M_DV * h, M_DV)]
    gi = _sum1(jnp.where(lane == h, G, 0.0))
    gf = _sum1(jnp.where(lane == h + HEADS, G, 0.0))
    return qmask, qc, kc, vc, gi, gf


def _gate_rows(gi, gf, eye, low, upp):
    lf = _log_sigmoid(gf)
    lf_row = _sum0(jnp.where(eye, lf, 0.0))
    gi_row = _sum0(jnp.where(eye, gi, 0.0))
    bcum = _sum1(jnp.where(low, lf_row, 0.0))
    bcum_row = _sum0(jnp.where(upp, lf, 0.0))
    b_last = _sum0(lf)
    return gi_row, bcum, bcum_row, b_last


def _cell_specs(SB, cpb, blk):
    def seq(width, col):
        return pl.BlockSpec((1, SB, width), lambda b, s: (b, blk(s), col))

    def state(rows):
        return pl.BlockSpec((1, HEADS, cpb, rows, 128), lambda b, s: (b, 0, blk(s), 0, 0))

    ins = [seq(D // 2, 0), seq(D // 2, 1), seq(D, 1), seq(D, 2), seq(128, 3 * D // 128),
           pl.BlockSpec((1, D), lambda b, s: (0, 0)), pl.BlockSpec((1, 128), lambda b, s: (0, 0))]
    return ins, [state(M_DV), state(1), state(1)], seq


def mlstm_cell_fwd(qk3, proj3, gain, gbias, name, gather=()):
    Bl, S, _ = qk3.shape
    L = M_CHUNK
    SB = min(M_SLAB, S)
    cpb, nc, nsb = SB // L, S // L, S // SB
    scale = M_DQK ** -0.5

    def body(q_ref, k_ref, v_ref, o_ref, g_ref, gain_ref, gb_ref, y_ref, cst_ref, nst_ref, mst_ref, *state):
        C_s, n_s, m_s = state[:HEADS], state[HEADS:2 * HEADS], state[2 * HEADS:]

        @pl.when(pl.program_id(1) == 0)
        def _():
            for ref in state:
                ref[...] = jnp.zeros_like(ref)
        lane = lax.broadcasted_iota(jnp.int32, (L, 128), 1)
        eye, low, upp = _cell_consts(L)

        def step(c, carry):
            off = pl.multiple_of(c * L, L)
            G = g_ref[0, pl.ds(off, L), :] + gb_ref[...]

            def head(h):
                C, n, mb = C_s[h][...], n_s[h][...], m_s[h][...]
                cst_ref[0, h, c] = C
                nst_ref[0, h, c] = n
                mst_ref[0, h, c] = mb
                m = mb[:, 0:1]
                _, qc, kc, vc, gi, gf = _load_chunk(q_ref, k_ref, v_ref, G, off, L, h, lane)
                gi_row, bcum, bcum_row, b_last = _gate_rows(gi, gf, eye, low, upp)
                state = _chunk_state(kc, vc, gi, bcum, b_last, C, n, m)
                next(state)
                r = yield from _chunk_out(qc * scale, kc, vc, gi_row, bcum, bcum_row, low, C, n, m)
                hc = r["hc"]
                hn = hc * lax.rsqrt(jnp.mean(hc * hc, axis=-1, keepdims=True) + RMS_EPS)
                cols = pl.ds(M_DV * h, M_DV)
                oc = o_ref[0, pl.ds(off, L), cols]
                y_ref[0, pl.ds(off, L), cols] = (_sigmoid(oc) * hn * gain_ref[:, cols]).astype(BF16)
                C2, n2, m2, _, _, _ = _finish(state)
                C_s[h][...] = C2
                n_s[h][...] = n2
                m_s[h][...] = jnp.broadcast_to(m2, (1, 128))

            _interleave(head(h) for h in range(HEADS))
            return carry

        lax.fori_loop(0, cpb, step, 0)

    ins, states, seq = _cell_specs(SB, cpb, lambda s: s)
    grid = (Bl, nsb)
    body, c_in, c_out, c_shape, c_scratch = host_comm(body, grid, 7, 4, gather=gather)
    return pl.pallas_call(
        body, name=name, grid=grid, in_specs=ins + c_in, out_specs=[seq(D, 0)] + states + c_out,
        out_shape=[jax.ShapeDtypeStruct((Bl, S, D), BF16),
                   jax.ShapeDtypeStruct((Bl, HEADS, nc, M_DV, 128), F32),
                   jax.ShapeDtypeStruct((Bl, HEADS, nc, 1, 128), F32),
                   jax.ShapeDtypeStruct((Bl, HEADS, nc, 1, 128), F32)] + c_shape,
        scratch_shapes=[pltpu.VMEM((M_DV, 128), F32)] * HEADS + [pltpu.VMEM((1, 128), F32)] * (2 * HEADS) + c_scratch,
        compiler_params=_cp(2),
    )(qk3, qk3, proj3, proj3, proj3, gain, gbias, *gather)


def mlstm_cell_bwd(qk3, proj3, gain, gbias, dy3, states, name, exchange=()):
    Bl, S, _ = qk3.shape
    L = M_CHUNK
    SB = min(M_SLAB, S)
    cpb, nsb = SB // L, S // SB
    scale = M_DQK ** -0.5

    def body(q_ref, k_ref, v_ref, o_ref, g_ref, gain_ref, gb_ref, cst_ref, nst_ref, mst_ref, dy_ref,
             dq_ref, dk_ref, dv_ref, do_ref, dg_ref, dgain_ref, dgb_ref, *state):
        dC_s, dn_s, dgain_s, dgb_s = state[:HEADS], state[HEADS:2 * HEADS], state[2 * HEADS:3 * HEADS], state[-1]
        s = pl.program_id(1)

        @pl.when(s == 0)
        def _():
            for ref in state:
                ref[...] = jnp.zeros_like(ref)
        lane = lax.broadcasted_iota(jnp.int32, (L, 128), 1)
        rowi = lax.broadcasted_iota(jnp.int32, (L, 1), 0)
        eye, low, upp = _cell_consts(L)

        def bstep(t, carry):
            c = cpb - 1 - t
            off = pl.multiple_of(c * L, L)
            G = g_ref[0, pl.ds(off, L), :] + gb_ref[...]
            shared = dict(slab=jnp.zeros((L, 128), F32))

            def head(h):
                cols = pl.ds(M_DV * h, M_DV)
                gain_h = gain_ref[:, cols]
                C, n, m = cst_ref[0, h, c], nst_ref[0, h, c], mst_ref[0, h, c][:, 0:1]
                dC_n, dn_n = dC_s[h][...], dn_s[h][...]
                qmask, qc, kc, vc, gi, gf = _load_chunk(q_ref, k_ref, v_ref, G, off, L, h, lane)
                gi_row, bcum, bcum_row, b_last = _gate_rows(gi, gf, eye, low, upp)
                qs = qc * scale
                _, _, _, wa, sp, sl = _finish(_chunk_state(kc, vc, gi, bcum, b_last, C, n, m))
                dcl_b = (sl * dC_n).astype(BF16)
                t1_mm = _dot(vc.astype(BF16), dcl_b)
                dv_mm = _dot_nt(kc.astype(BF16), dcl_b)
                r = yield from _chunk_out(qs, kc, vc, gi_row, bcum, bcum_row, low, C, n, m)
                hc, den, dn, sc, dm, iw, qn = r["hc"], r["den"], r["dn"], r["sc"], r["dm"], r["iw"], r["qn"]
                qs_b, k_b, v_b = r["qs_b"], r["k_b"], r["v_b"]
                dy = dy_ref[0, pl.ds(off, L), cols].astype(F32)
                oc = o_ref[0, pl.ds(off, L), cols]
                sig_o = _sigmoid(oc)
                rr = lax.rsqrt(jnp.mean(hc * hc, axis=-1, keepdims=True) + RMS_EPS)
                hn = hc * rr
                dgain_s[h][...] += _sum0(dy * sig_o * hn)
                do_ref[0, pl.ds(off, L), cols] = (
                    dy * hn * gain_h * sig_o * (1.0 - sig_o)).astype(BF16)
                dhn = dy * sig_o * gain_h
                dhc = rr * dhn - hc * (rr * rr * rr) * jnp.mean(dhn * hc, axis=-1, keepdims=True)
                dnum = dhc / dn
                gden = -_sum1(dhc * hc) / dn
                dden = jnp.where(jnp.abs(den) > r["floor"], gden * jnp.sign(den), 0.0)
                dnum_b = dnum.astype(BF16)
                dqc_b = (iw * dnum).astype(BF16)
                dsc_mm = _dot_nt(dnum_b, v_b)
                dv = _dot_tn(sc.astype(BF16), dnum_b)
                dqs_mm = _dot(dqc_b, C.astype(BF16))
                dC_out = _dot_tn(dqc_b, qs_b)
                diw = _sum1(dnum * r["qc"]) + dden * qn
                wq = iw * dden
                dn_out = _sum0(wq * qs)
                dn_loc = sl * dn_n
                dsp = _sum1(_sum0(dC_n * C)) + _sum1(dn_n * n)
                yield
                dsc = dsc_mm + dden
                dS_b = (dsc * dm).astype(BF16)
                gm = dsc * sc
                dqs2_mm = _dot(dS_b, k_b)
                dk = _dot_tn(dS_b, qs_b)
                dqs = dqs_mm + wq * n
                dbc = _sum1(gm) + diw * iw
                colg = _sum0(gm)
                dC_p = sp * dC_n + dC_out
                dn_p = sp * dn_n + dn_out
                db_last = dsp * sp
                t1 = t1_mm + dn_loc
                dwa = _sum1(t1 * kc)
                dv = dv + wa * dv_mm
                yield
                dqs = dqs + dqs2_mm
                dk = dk + wa * t1
                da = dwa * wa
                db_last = db_last + _sum0(da)
                dbc = dbc - da + jnp.where(rowi == L - 1, db_last, 0.0)
                dbc_row = _sum0(jnp.where(eye, dbc, 0.0)) - colg
                dgi = da + _sum1(jnp.where(eye, colg, 0.0))
                dlf = _sum1(jnp.where(upp, dbc_row, 0.0))
                dgf = dlf * _sigmoid(-gf)
                dq = jnp.where(qmask, dqs * scale, 0.0)
                dk = jnp.where(qmask, dk, 0.0)
                shared["slab"] = (shared["slab"] + jnp.where(lane == h, dgi, 0.0)
                                  + jnp.where(lane == h + HEADS, dgf, 0.0))
                dv_ref[0, pl.ds(off, L), cols] = dv.astype(BF16)
                dC_s[h][...] = dC_p
                dn_s[h][...] = dn_p
                if h % 2 == 0:
                    shared["dq"], shared["dk"] = dq, dk
                else:
                    pair = pl.ds(128 * (h // 2), 128)
                    dq_ref[0, pl.ds(off, L), pair] = shared["dq"] + dq
                    dk_ref[0, pl.ds(off, L), pair] = shared["dk"] + dk

            _interleave(head(h) for h in range(HEADS))
            dg_ref[0, pl.ds(off, L), :] = shared["slab"]
            dgb_s[...] += _sum0(shared["slab"])
            return carry

        lax.fori_loop(0, cpb, bstep, 0)

        @pl.when(s == nsb - 1)
        def _():
            for h in range(HEADS):
                dgain_ref[0, :, pl.ds(M_DV * h, M_DV)] = dgain_s[h][...]
            dgb_ref[0] = dgb_s[...]

    ins, states_specs, seq = _cell_specs(SB, cpb, lambda s: nsb - 1 - s)
    once = lambda width: pl.BlockSpec((1, 1, width), lambda b, s: (b, 0, 0))
    grid = (Bl, nsb)
    body, c_in, c_out, c_shape, c_scratch = host_comm(body, grid, 11, 7, exchange=exchange)
    return pl.pallas_call(
        body, name=name, grid=grid, in_specs=ins + states_specs + [seq(D, 0)] + c_in,
        out_specs=[seq(D // 2, 0), seq(D // 2, 0), seq(D, 0), seq(D, 0), seq(128, 0), once(D), once(128)] + c_out,
        out_shape=[jax.ShapeDtypeStruct((Bl, S, D // 2), F32), jax.ShapeDtypeStruct((Bl, S, D // 2), F32),
                   jax.ShapeDtypeStruct((Bl, S, D), BF16), jax.ShapeDtypeStruct((Bl, S, D), BF16),
                   jax.ShapeDtypeStruct((Bl, S, 128), F32), jax.ShapeDtypeStruct((Bl, 1, D), F32),
                   jax.ShapeDtypeStruct((Bl, 1, 128), F32)] + c_shape,
        scratch_shapes=[pltpu.VMEM((M_DV, 128), F32)] * HEADS + [pltpu.VMEM((1, 128), F32)] * (2 * HEADS + 1)
        + c_scratch,
        compiler_params=_cp(2),
    )(qk3, qk3, proj3, proj3, proj3, gain, gbias, *states, dy3, *exchange)


def _attn_scores(q, kc, kp, n, row, col, scale):
    s_c = jnp.where(col <= row, _dot_nt(q, kc) * scale, NEG)
    s_p = jnp.where(jnp.logical_and(col >= row, n > 0), _dot_nt(q, kp) * scale, NEG)
    return s_c, s_p


def _to_streams(src, dst, tmp, dil, Sd):
    if dil == 1:
        dst[...] = src[...].astype(dst.dtype)
        return
    if src.dtype != F32:
        tmp[...] = src[...].astype(F32)
        src = tmp
    for r in range(dil):
        dst[pl.ds(r * Sd, Sd), :] = src[pl.ds(r, Sd, stride=dil), :].astype(dst.dtype)


def _from_streams(src, dst, dil, Sd):
    if dil == 1:
        dst[...] = src[...]
        return
    for r in range(dil):
        dst[pl.ds(r, Sd, stride=dil), :] = src[pl.ds(r * Sd, Sd), :]


def attn_fwd(proj, Bl, S, g, dil, name):
    Sd = S // dil
    nb = Sd // A_BLK
    scale = A_BLK ** -0.5
    pv = proj.reshape(Bl, S, A_PROJ)

    def body(q_ref, k_ref, v_ref, o_ref, l_ref, tmp, qs, ks, vs, os_, ls):
        row = lax.broadcasted_iota(jnp.int32, (A_BLK, A_BLK), 0)
        col = lax.broadcasted_iota(jnp.int32, (A_BLK, A_BLK), 1)
        for src, dst in ((q_ref, qs), (k_ref, ks), (v_ref, vs)):
            _to_streams(src.at[0], dst, tmp, dil, Sd)

        def step(i, carry):
            n = i % nb
            off = pl.multiple_of(i * A_BLK, A_BLK)
            offp = pl.multiple_of(jnp.maximum(i - 1, 0) * A_BLK, A_BLK)
            q = qs[pl.ds(off, A_BLK), :]
            s_c, s_p = _attn_scores(q, ks[pl.ds(off, A_BLK), :], ks[pl.ds(offp, A_BLK), :], n, row, col, scale)
            m = jnp.maximum(jnp.max(s_c, axis=1, keepdims=True), jnp.max(s_p, axis=1, keepdims=True))
            p_c = jnp.exp(s_c - m)
            p_p = jnp.exp(s_p - m)
            den = _sum1(p_c) + _sum1(p_p)
            o = _dot(p_c.astype(BF16), vs[pl.ds(off, A_BLK), :]) + _dot(p_p.astype(BF16), vs[pl.ds(offp, A_BLK), :])
            os_[pl.ds(off, A_BLK), :] = o / den
            ls[pl.ds(off, A_BLK), :] = jnp.broadcast_to(m + jnp.log(den), (A_BLK, 128))
            return carry

        lax.fori_loop(0, dil * nb, step, 0, unroll=A_UNROLL)
        _from_streams(os_, o_ref.at[0], dil, Sd)
        _from_streams(ls, l_ref.at[0], dil, Sd)

    def spec(j):
        return pl.BlockSpec((1, S, 128), lambda b, h: (b, 0, g * 24 + j * HEADS + h))

    ospec = pl.BlockSpec((1, S, 128), lambda b, h: (b, 0, h))
    o, lse = pl.pallas_call(
        body, name=name, grid=(Bl, HEADS),
        in_specs=[spec(0), spec(1), spec(2)], out_specs=[ospec, ospec],
        out_shape=[jax.ShapeDtypeStruct((Bl, S, D), F32)] * 2,
        scratch_shapes=[pltpu.VMEM((S, 128), F32)] + [pltpu.VMEM((S, 128), BF16)] * 3 + [pltpu.VMEM((S, 128), F32)] * 2,
        compiler_params=_cp(2),
    )(pv, pv, pv)
    return o.reshape(Bl * S, D), lse.reshape(Bl * S, D)


def attn_merge(os_, lses, name):
    T = os_[0].shape[0]
    tm = _tile(T, 512)
    ng = len(os_)

    def body(*refs):
        o_refs, l_refs = refs[:ng], refs[ng:2 * ng]
        ob_ref, of_ref, lt_ref = refs[2 * ng:]
        ls = [r[...] for r in l_refs]
        m = functools.reduce(jnp.maximum, ls)
        ws = [jnp.exp(l - m) for l in ls]
        den = functools.reduce(lambda a, b: a + b, ws)
        o = functools.reduce(lambda a, b: a + b, [w * r[...] for w, r in zip(ws, o_refs)]) / den
        of_ref[...] = o
        ob_ref[...] = o.astype(BF16)
        lt_ref[...] = m + jnp.log(den)

    spec = pl.BlockSpec((tm, D), lambda i: (i, 0))
    return pl.pallas_call(
        body, name=name, grid=(T // tm,), in_specs=[spec] * (2 * ng), out_specs=[spec] * 3,
        out_shape=[jax.ShapeDtypeStruct((T, D), BF16), jax.ShapeDtypeStruct((T, D), F32),
                   jax.ShapeDtypeStruct((T, D), F32)],
        compiler_params=_cp(1),
    )(*os_, *lses)


def attn_bwd(proj, do, o, lse, Bl, S, g, dil, name):
    Sd = S // dil
    nb = Sd // A_BLK
    scale = A_BLK ** -0.5
    pv = proj.reshape(Bl, S, A_PROJ)
    dov, ov, lv = (t.reshape(Bl, S, D) for t in (do, o, lse))

    def body(q_ref, k_ref, v_ref, do_ref, o_ref, l_ref, dq_ref, dk_ref, dv_ref,
             tmp, qs, ks, vs, dos, dls, lts, dq_s, dk_s, dv_s):
        row = lax.broadcasted_iota(jnp.int32, (A_BLK, A_BLK), 0)
        col = lax.broadcasted_iota(jnp.int32, (A_BLK, A_BLK), 1)
        for src, dst in ((q_ref, qs), (k_ref, ks), (v_ref, vs), (do_ref, dos), (l_ref, lts)):
            _to_streams(src.at[0], dst, tmp, dil, Sd)
        tmp[...] = jnp.broadcast_to(_sum1(do_ref[0].astype(F32) * o_ref[0]), (S, 128))
        _to_streams(tmp, dls, None, dil, Sd)
        dk_s[...] = jnp.zeros_like(dk_s)
        dv_s[...] = jnp.zeros_like(dv_s)

        def step(i, carry):
            n = i % nb
            off = pl.multiple_of(i * A_BLK, A_BLK)
            offp = pl.multiple_of(jnp.maximum(i - 1, 0) * A_BLK, A_BLK)
            q = qs[pl.ds(off, A_BLK), :]
            kc, kp = ks[pl.ds(off, A_BLK), :], ks[pl.ds(offp, A_BLK), :]
            vc, vp = vs[pl.ds(off, A_BLK), :], vs[pl.ds(offp, A_BLK), :]
            do_b = dos[pl.ds(off, A_BLK), :]
            delta = dls[pl.ds(off, A_BLK), :][:, 0:1]
            lt = lts[pl.ds(off, A_BLK), :][:, 0:1]
            s_c, s_p = _attn_scores(q, kc, kp, n, row, col, scale)
            p_c = jnp.exp(s_c - lt)
            p_p = jnp.exp(s_p - lt)
            ds_c = (p_c * (_dot_nt(do_b, vc) - delta) * scale).astype(BF16)
            ds_p = (p_p * (_dot_nt(do_b, vp) - delta) * scale).astype(BF16)
            dq_s[pl.ds(off, A_BLK), :] = _dot(ds_c, kc) + _dot(ds_p, kp)
            dk_s[pl.ds(off, A_BLK), :] += _dot_tn(ds_c, q)
            dk_s[pl.ds(offp, A_BLK), :] += _dot_tn(ds_p, q)
            dv_s[pl.ds(off, A_BLK), :] += _dot_tn(p_c.astype(BF16), do_b)
            dv_s[pl.ds(offp, A_BLK), :] += _dot_tn(p_p.astype(BF16), do_b)
            return carry

        lax.fori_loop(0, dil * nb, step, 0, unroll=A_UNROLL)
        for src, dst in ((dq_s, dq_ref), (dk_s, dk_ref), (dv_s, dv_ref)):
            _from_streams(src, tmp, dil, Sd)
            dst[0] = tmp[...].astype(BF16)

    def spec(j):
        return pl.BlockSpec((1, S, 128), lambda b, h: (b, 0, g * 24 + j * HEADS + h))

    ospec = pl.BlockSpec((1, S, 128), lambda b, h: (b, 0, h))
    slab = lambda dt: pltpu.VMEM((S, 128), dt)
    outs = pl.pallas_call(
        body, name=name, grid=(Bl, HEADS),
        in_specs=[spec(0), spec(1), spec(2), ospec, ospec, ospec], out_specs=[ospec] * 3,
        out_shape=[jax.ShapeDtypeStruct((Bl, S, D), BF16)] * 3,
        scratch_shapes=[slab(F32)] + [slab(BF16)] * 4 + [slab(F32)] * 5,
        compiler_params=_cp(2),
    )(pv, pv, pv, dov, ov, lv)
    return [t.reshape(Bl * S, D) for t in outs]


def _as_slots(pair, shape):
    return tuple(t.reshape(shape) for t in pair)


def ffn_fwd(x, mod3, w_in, w_out, lng, lnb, tag, gather=()):
    a, g, u, h, gathered = ffn_in(x, mod3, w_in, tag + "_in", gather=gather)
    out, xn = proj_post(a, w_out, x, mod3, lng, lnb, 0.5, tag + "_out")
    return xn, (x, out, g, u, h, a), gathered


def ffn_bwd(dxn, saved, mod3, w_in, w_out, lng, tag, exchange=(), exchange2=(), exchange_own=False):
    x, out, g, u, h, a = saved
    dxres, dout, dgu, dlg, dlb, dgate, *got = post_bwd(dxn, x, out, mod3, lng, w_out, 0.5, tag + "_outb",
                                                       tm=256, gu=(g, u), exchange=exchange)
    *dw_in, got2 = mm_tn(h, dgu, tag + "_dwin", a_copies=True, exchange=exchange2) + (() if exchange2 else ([],))
    dw_out = _as_slots(mm_tn(a, dout, tag + "_dwout", bw=D), (N_DEV, D_FF // N_DEV, D))
    dx, dsh, dsc, *own = modmm_bwd(dgu, w_in, x, mod3, dxres, tag + "_inb", tm=256,
                                   exchange=[dw_in[1], dw_out[1]] if exchange_own else ())
    dmod3 = jnp.concatenate([dsh, dsc, dgate], axis=1)
    return (dx, [tuple(dw_in), dw_out], dlg, dlb, dmod3,
            (got[0] if got else []), got2, (own[0] if own else []))


def mlstm_fwd(x, mod3, w_in, w_out, conv_w, gain, gbias, lng, lnb, Bl, S, gather=()):
    proj, h = modmm(x, mod3, w_in, F32, "ml_in", tn=M_PROJ_PAD // 5)
    proj3 = proj.reshape(Bl, S, M_PROJ_PAD)
    qk3 = conv_silu(proj3, conv_w, "ml_conv")
    y3, *rest = mlstm_cell_fwd(qk3, proj3, gain, gbias, "ml_cell", gather=gather)
    states, gathered = rest[:3], rest[3:]
    y = y3.reshape(Bl * S, D)
    out, xn = proj_post(y[None], w_out, x, mod3, lng, lnb, 1.0, "ml_out")
    return xn, (x, out, h, proj3, qk3, y, states), gathered


def mlstm_bwd(dxn, saved, mod3, w_in, w_out, conv_w, gain, gbias, lng, Bl, S, exchange=()):
    x, out, h, proj3, qk3, y, states = saved
    dxres, dout, dy, dlg, dlb, dgate = post_bwd(dxn, x, out, mod3, lng, w_out, 1.0, "ml_outb")
    dq, dk, dv, do, dg, dgain, dgb, *received = mlstm_cell_bwd(qk3, proj3, gain, gbias, dy.reshape(Bl, S, D),
                                                               states, "ml_cellb", exchange=exchange)
    dqk, dconv = conv_silu_bwd(proj3, conv_w, dq, dk, "ml_convb")
    dproj = jnp.concatenate([dqk, dv, do, dg.astype(BF16)], axis=2).reshape(Bl * S, M_PROJ_PAD)
    dx, dsh, dsc = modmm_bwd(dproj, w_in, x, mod3, dxres, "ml_inb", tn=M_PROJ_PAD // 5)
    dwi, _ = mm_tn(h, dproj, "ml_dwin", bw=M_PROJ_PAD // 5)
    dwi = _restack(jnp.moveaxis(dwi, 0, 1).reshape(D, M_PROJ_PAD)[:, :M_PROJ], 1)
    dw_out = _as_slots(mm_tn(y, dout, "ml_dwout", bw=D), (N_DEV, D // N_DEV, D))
    small = (jnp.sum(dconv, axis=0), jnp.sum(dgain, axis=0), jnp.sum(dgb, axis=0)[:, :2 * HEADS])
    dmod3 = jnp.concatenate([dsh, dsc, dgate], axis=1)
    return dx, [(dwi, dwi.astype(BF16)), dw_out], dlg, dlb, dmod3, small, received


def attn_mixer_fwd(x, mod3, w_in, w_out, lng, lnb, Bl, S):
    proj, h = modmm(x, mod3, w_in, BF16, "at_in")
    os_, lses = [], []
    for g, (_, dil) in enumerate(DIL_GROUPS):
        o_g, l_g = attn_fwd(proj, Bl, S, g, dil, "at_core%d" % g)
        os_.append(o_g)
        lses.append(l_g)
    ob, of, lt = attn_merge(os_, lses, "at_merge")
    out, xn = proj_post(ob[None], w_out, x, mod3, lng, lnb, 1.0, "at_out")
    return xn, (x, out, h, proj, ob, of, lt)


def attn_mixer_bwd(dxn, saved, mod3, w_in, w_out, lng, Bl, S):
    x, out, h, proj, ob, of, lt = saved
    dxres, dout, do, dlg, dlb, dgate = post_bwd(dxn, x, out, mod3, lng, w_out, 1.0, "at_outb")
    do = do[0]
    parts = []
    for g, (_, dil) in enumerate(DIL_GROUPS):
        parts += attn_bwd(proj, do, of, lt, Bl, S, g, dil, "at_coreb%d" % g)
    dproj = jnp.concatenate(parts, axis=1)
    dx, dsh, dsc = modmm_bwd(dproj, w_in, x, mod3, dxres, "at_inb")
    dw_in = mm_tn(h, dproj, "at_dwin", bw=w_in.shape[2])
    dw_out = _as_slots(mm_tn(ob, dout, "at_dwout", bw=D), (N_DEV, D // N_DEV, D))
    return dx, [dw_in, dw_out], dlg, dlb, jnp.concatenate([dsh, dsc, dgate], axis=1)


def _unstack(stacked, axis):
    full = jnp.moveaxis(stacked, 0, axis)
    shp = list(full.shape)
    shp[axis:axis + 2] = [shp[axis] * shp[axis + 1]]
    return full.reshape(shp)


def _restack(full, axis):
    shp = list(full.shape)
    shp[axis:axis + 1] = [N_DEV, shp[axis] // N_DEV]
    return jnp.moveaxis(full.reshape(shp), axis, 0)


def kernel(x, c, ada_w, ada_b, ln_g, ln_b, ffn_w_in, ffn_w_out, mlstm_w_in, mlstm_gate_bias, mlstm_conv_w, mlstm_head_gain, mlstm_w_out, attn_w_in, attn_w_out, loss_target, m_ada_w, m_ada_b, m_ln_g, m_ln_b, m_ffn_w_in, m_ffn_w_out, m_mlstm_w_in, m_mlstm_gate_bias, m_mlstm_conv_w, m_mlstm_head_gain, m_mlstm_w_out, m_attn_w_in, m_attn_w_out, v_ada_w, v_ada_b, v_ln_g, v_ln_b, v_ffn_w_in, v_ffn_w_out, v_mlstm_w_in, v_mlstm_gate_bias, v_mlstm_conv_w, v_mlstm_head_gain, v_mlstm_w_out, v_attn_w_in, v_attn_w_out):
    Bl, S, _ = x.shape
    T = Bl * S
    Bg = Bl * N_DEV
    me = 4 * lax.axis_index("x") + 2 * lax.axis_index("y") + lax.axis_index("c")
    onehot = (jnp.arange(N_DEV) == me).astype(F32)
    weights = dict(ada_w=ada_w, ada_b=ada_b, ln_g=ln_g, ln_b=ln_b, ffn_w_in=ffn_w_in, ffn_w_out=ffn_w_out,
                   mlstm_w_in=mlstm_w_in, mlstm_gate_bias=mlstm_gate_bias, mlstm_conv_w=mlstm_conv_w,
                   mlstm_head_gain=mlstm_head_gain, mlstm_w_out=mlstm_w_out, attn_w_in=attn_w_in,
                   attn_w_out=attn_w_out)
    m_in = dict(ada_w=m_ada_w, ada_b=m_ada_b, ln_g=m_ln_g, ln_b=m_ln_b, ffn_w_in=m_ffn_w_in,
                ffn_w_out=m_ffn_w_out, mlstm_w_in=m_mlstm_w_in, mlstm_gate_bias=m_mlstm_gate_bias,
                mlstm_conv_w=m_mlstm_conv_w, mlstm_head_gain=m_mlstm_head_gain, mlstm_w_out=m_mlstm_w_out,
                attn_w_in=m_attn_w_in, attn_w_out=m_attn_w_out)
    v_in = dict(ada_w=v_ada_w, ada_b=v_ada_b, ln_g=v_ln_g, ln_b=v_ln_b, ffn_w_in=v_ffn_w_in,
                ffn_w_out=v_ffn_w_out, mlstm_w_in=v_mlstm_w_in, mlstm_gate_bias=v_mlstm_gate_bias,
                mlstm_conv_w=v_mlstm_conv_w, mlstm_head_gain=v_mlstm_head_gain, mlstm_w_out=v_mlstm_w_out,
                attn_w_in=v_attn_w_in, attn_w_out=v_attn_w_out)

    mixer = ("mlstm", "attn")
    shards = [[ffn_w_in[layer, 0], ffn_w_in[layer, 1], ffn_w_out[layer, 0], ffn_w_out[layer, 1],
               weights[mixer[layer] + "_w_in"][0], weights[mixer[layer] + "_w_out"][0]] for layer in range(DEPTH)]
    sends = [[s.astype(BF16) for s in layer_shards] for layer_shards in shards]
    small = jnp.concatenate([c.reshape(-1), ln_g.reshape(-1), ln_b.reshape(-1), mlstm_conv_w.reshape(-1)])
    n_small = small.shape[0]
    small = jnp.pad(small, (0, -n_small % (8 * PACK_COLS))).reshape(-1, PACK_COLS)

    def gathered_weights(g):
        return ((g[0], g[1]), (g[2].reshape(4, D_FF // 4, D), g[3].reshape(4, D_FF // 4, D)), g[4],
                g[5].reshape(1, D, D))

    first_in, small_all = all_gather([sends[0][0], small], "ag_params")
    full = [None, None]
    small_flat = small_all.reshape(N_DEV, -1)
    o0 = 0
    c_all = small_flat[:, o0:o0 + c.size].reshape(Bg, D)
    o0 += c.size
    lng_full = _unstack(small_flat[:, o0:o0 + ln_g.size].reshape((N_DEV,) + ln_g.shape), 2)
    o0 += ln_g.size
    lnb_full = _unstack(small_flat[:, o0:o0 + ln_b.size].reshape((N_DEV,) + ln_b.shape), 2)
    o0 += ln_b.size
    conv_full = _unstack(small_flat[:, o0:o0 + mlstm_conv_w.size].reshape((N_DEV,) + mlstm_conv_w.shape), 2)[0]
    gbias =jnp.pad(mlstm_gate_bias, ((0, 0), (0, 128 - 2 * HEADS)))

    ncols = ada_w.shape[2]
    ada_b_cols = lax.dynamic_slice_in_dim(ada_b, me * ncols, ncols, axis=1).reshape(DEPTH, 1, ncols)
    mod_cols = ada_fwd(c_all, ada_w, ada_b_cols, "ada_fwd")
    (mod_g,) = all_gather([mod_cols.reshape(DEPTH * Bg, ncols)], "ag_mod")
    mod_full = _unstack(mod_g.reshape(N_DEV, DEPTH, Bg, ncols), 2)
    mod_mine = lax.dynamic_slice_in_dim(mod_full, me * Bl, Bl, axis=1).reshape(DEPTH, Bl, 3, 3, D)

    xt = x.reshape(T, D)
    saved = []
    for layer in range(DEPTH):
        def lnp(s, layer=layer):
            return lng_full[layer, s].reshape(1, D), lnb_full[layer, s].reshape(1, D)
        md = mod_mine[layer]
        if layer == 0:
            a, g, u, h, late = ffn_in(xt, md[:, 0], first_in, "f0a_in", gather=sends[0][1:])
            full[0] = gathered_weights([first_in] + late)
            out, xn = proj_post(a, full[0][1][0], xt, md[:, 0], *lnp(0), 0.5, "f0a_out")
            xt, sv0 = xn, (xt, out, g, u, h, a)
            mw_in = jnp.pad(_unstack(full[0][2], 1), ((0, 0), (0, M_PROJ_PAD - M_PROJ)))
        else:
            xt, sv0, _ = ffn_fwd(xt, md[:, 0], full[layer][0][0], full[layer][1][0], *lnp(0), "f%da" % layer)
        f_in, f_out, mix_in, mix_out = full[layer]
        if layer % 2 == 0:
            xt, sv1, g1 = mlstm_fwd(xt, md[:, 1], mw_in, mix_out, conv_full, mlstm_head_gain, gbias, *lnp(1), Bl, S,
                                    gather=sends[1])
            full[1] = gathered_weights(g1)
        else:
            xt, sv1 = attn_mixer_fwd(xt, md[:, 1], mix_in, mix_out, *lnp(1), Bl, S)
        xt, sv2, _ = ffn_fwd(xt, md[:, 2], f_in[1], f_out[1], *lnp(2), "f%db" % layer)
        saved.append((sv0, sv1, sv2))

    dxt, lsum = loss_head(xt, loss_target.reshape(T, D), "loss")
    loss = lax.psum(lsum[0, 0], MESH_AXES)

    dmod, dlg_all, dlb_all = [None] * DEPTH, [None] * DEPTH, [None] * DEPTH
    wgrads = [None] * DEPTH
    recvs = [[None] * 6 for _ in range(DEPTH)]
    ml_small = None
    for layer in reversed(range(DEPTH)):
        md = mod_mine[layer]
        f_in, f_out, mix_in, mix_out = full[layer]
        sv0, sv1, sv2 = saved[layer]
        dxt, dw2, dlg2, dlb2, dm2, _, _, _ = ffn_bwd(dxt, sv2, md[:, 2], f_in[1], f_out[1],
                                                     lng_full[layer, 2].reshape(1, D), "f%db" % layer)
        lg1 = lng_full[layer, 1].reshape(1, D)
        if layer % 2 == 0:
            dxt, dw1, dlg1, dlb1, dm1, ml_small, got = mlstm_bwd(
                dxt, sv1, md[:, 1], mw_in, mix_out, conv_full, mlstm_head_gain, gbias, lg1, Bl, S,
                exchange=[b16 for _, b16 in wgrads[1]])
            recvs[1] = got
            dxt, dw0, dlg0, dlb0, dm0, got, got2, own = ffn_bwd(
                dxt, sv0, md[:, 0], f_in[0], f_out[0], lng_full[layer, 0].reshape(1, D), "f%da" % layer,
                exchange=[dw2[0][1]], exchange2=[dw2[1][1], dw1[0][1], dw1[1][1]], exchange_own=True)
            (recvs[0][1],), (recvs[0][3], recvs[0][4], recvs[0][5]), (recvs[0][0], recvs[0][2]) = got, got2, own
        else:
            dxt, dw1, dlg1, dlb1, dm1 = attn_mixer_bwd(dxt, sv1, md[:, 1], mix_in, mix_out, lg1, Bl, S)
            dxt, dw0, dlg0, dlb0, dm0, _, _, _ = ffn_bwd(dxt, sv0, md[:, 0], f_in[0], f_out[0],
                                                         lng_full[layer, 0].reshape(1, D), "f%da" % layer)
        wgrads[layer] = [dw0[0], dw2[0], dw0[1], dw2[1], dw1[0], dw1[1]]
        dmod[layer] = jnp.stack([dm0, dm1, dm2], axis=1).reshape(Bl, 9 * D)
        dlg_all[layer] = jnp.concatenate([dlg0, dlg1, dlg2], axis=0)
        dlb_all[layer] = jnp.concatenate([dlb0, dlb1, dlb2], axis=0)
    grad_x = dxt.reshape(Bl, S, D)

    gsh = [[shard_sum(lax.dynamic_index_in_dim(f32, me, axis=0, keepdims=False), recv, onehot,
                      "rs_sum%d_%d" % (layer, i))
            for i, ((f32, _), recv) in enumerate(zip(wgrads[layer], recvs[layer]))] for layer in range(DEPTH)]
    grads = {"ffn_w_in": jnp.stack([jnp.stack(g[0:2]) for g in gsh]),
             "ffn_w_out": jnp.stack([jnp.stack(g[2:4]) for g in gsh]),
             "mlstm_w_in": gsh[0][4][None], "mlstm_w_out": gsh[0][5][None],
             "attn_w_in": gsh[1][4][None], "attn_w_out": gsh[1][5][None]}

    dconv, dgain, dgbias = ml_small
    parts = [jnp.stack(dmod).reshape(-1), dgbias.reshape(-1), dgain.reshape(-1),
             jnp.stack(dlg_all).reshape(-1), jnp.stack(dlb_all).reshape(-1), dconv.reshape(-1)]
    sizes = [p.shape[0] for p in parts]
    flat = jnp.concatenate(parts)
    flat = jnp.pad(flat, (0, -flat.shape[0] % (8 * PACK_COLS))).reshape(-1, PACK_COLS)
    (sm_all,) = all_gather([flat], "ag_small")
    sm_sum = sum_leading(sm_all, "small_sum").reshape(-1)
    dmod_all = sm_all.reshape(N_DEV, -1)[:, :sizes[0]].reshape(N_DEV, DEPTH, Bl, 9 * D)
    dmod_all = jnp.moveaxis(dmod_all, 0, 1).reshape(DEPTH, Bg, 9 * D)
    o0 = sizes[0]
    grads["mlstm_gate_bias"] = sm_sum[o0:o0 + sizes[1]].reshape(mlstm_gate_bias.shape)
    o0 += sizes[1]
    grads["mlstm_head_gain"] = sm_sum[o0:o0 + sizes[2]].reshape(mlstm_head_gain.shape)
    o0 += sizes[2]
    nl = ln_g.shape[2]
    g_lng = sm_sum[o0:o0 + sizes[3]].reshape(DEPTH, 3, D)
    o0 += sizes[3]
    g_lnb = sm_sum[o0:o0 + sizes[4]].reshape(DEPTH, 3, D)
    o0 += sizes[4]
    g_conv = sm_sum[o0:o0 + sizes[5]].reshape(1, 4, D)
    grads["ln_g"] = lax.dynamic_slice_in_dim(g_lng, me * nl, nl, axis=2)
    grads["ln_b"] = lax.dynamic_slice_in_dim(g_lnb, me * nl, nl, axis=2)
    grads["mlstm_conv_w"] = lax.dynamic_slice_in_dim(g_conv, me * nl, nl, axis=2)
    dmod_cols = lax.dynamic_slice_in_dim(dmod_all, me * ncols, ncols, axis=2)
    gw, gb = ada_bwd(c_all.T, dmod_cols, dmod_all, "ada_bwd")
    grads["ada_w"] = gw
    grads["ada_b"] = gb.reshape(ada_b.shape)

    names = ["ada_w", "ada_b", "ln_g", "ln_b", "ffn_w_in", "ffn_w_out", "mlstm_w_in", "mlstm_gate_bias",
             "mlstm_conv_w", "mlstm_head_gain", "mlstm_w_out", "attn_w_in", "attn_w_out"]
    deltas, new_m, new_v = [], [], []
    for k in names:
        w = weights[k]
        shp2 = (math.prod(w.shape[:-1]), w.shape[-1])
        d_, m_, v_ = adamw(w.reshape(shp2), grads[k].reshape(shp2), m_in[k].reshape(shp2), v_in[k].reshape(shp2),
                           "adamw_" + k)
        deltas.append(d_.reshape(w.shape))
        new_m.append(m_.reshape(w.shape))
        new_v.append(v_.reshape(w.shape))
    return (loss, grad_x, *[grads[k] for k in names], *deltas, *new_m, *new_v)
```

```python
import functools
import math

import jax
import jax.numpy as jnp
from jax import lax
from jax.experimental import pallas as pl
from jax.experimental.pallas import tpu as pltpu

F32 = jnp.float32
BF16 = jnp.bfloat16

N_DEV = 8
MESH_AXES = ("x", "y", "c")
D = 1024
DEPTH = 2
D_FF = 2816
HEADS = 8
M_DQK = 64
M_DV = 128
M_CHUNK = 64
M_SLAB = 512
M_PROJ = 3088
M_PROJ_PAD = 3200
A_PROJ = 9216
DIL_GROUPS = ((128, 1), (512, 4), (2048, 16))
A_BLK = 128
A_UNROLL = 8
ALPHA = (2 * DEPTH) ** 0.25
LN_EPS = 1e-5
RMS_EPS = 1e-6
ADAM_LR = 0.001
ADAM_B1 = 0.9
ADAM_B2 = 0.999
ADAM_EPS = 1e-08
ADAM_WD = 0.01
ADAM_STEP = 10
NEG = -1e30
V7X_VMEM_LIMIT = 56 * 1024 * 1024
PACK_COLS = 1024
MESH_ID = pl.DeviceIdType.MESH
ANY_SPEC = pl.BlockSpec(memory_space=pl.ANY)


def _cp(n_axes):
    return pltpu.CompilerParams(dimension_semantics=("arbitrary",) * n_axes,
                                vmem_limit_bytes=V7X_VMEM_LIMIT)


def _dot(a, b):
    return jnp.dot(a, b, preferred_element_type=F32)


def _dot_nt(a, b):
    return lax.dot_general(a, b, (((1,), (1,)), ((), ())), preferred_element_type=F32)


def _dot_tn(a, b):
    return lax.dot_general(a, b, (((0,), (0,)), ((), ())), preferred_element_type=F32)


def _sum0(a):
    return jnp.sum(a, axis=0, keepdims=True)


def _sum1(a):
    return jnp.sum(a, axis=1, keepdims=True)


def _round(a):
    return a.astype(BF16).astype(F32)


def _sigmoid(a):
    return 1.0 / (1.0 + jnp.exp(-a))


def _tile(n, pref):
    t = min(n, pref)
    while n % t:
        t //= 2
    return t


def all_gather(arrs, name):
    n = len(arrs)

    def body(*refs):
        gather = Gather(refs[:n], refs[n:2 * n], *refs[2 * n:])
        gather.start()
        gather.finish()

    return pl.pallas_call(
        body, name=name, out_shape=Gather.out_shape(arrs),
        in_specs=[ANY_SPEC] * n, out_specs=[ANY_SPEC] * n, scratch_shapes=Gather.scratch(n),
    )(*arrs)


class Gather:
    def __init__(self, ins, outs, send_sems, recv_sems, local_sems):
        x, y, c = lax.axis_index("x"), lax.axis_index("y"), lax.axis_index("c")
        me, sibling = (x, y, c), (x, y, 1 - c)
        chips = [(1 - x, y), (x, 1 - y), (1 - x, 1 - y)]

        def slot(a, p):
            return outs[a].at[4 * p[0] + 2 * p[1] + p[2]]

        def copy(a, k, block, to, src=None):
            return pltpu.make_async_remote_copy(
                src_ref=slot(a, block) if src is None else src, dst_ref=slot(a, block),
                send_sem=send_sems.at[7 * a + k], recv_sem=recv_sems.at[7 * a + k],
                device_id=to, device_id_type=MESH_ID)

        n = len(ins)
        self.mine = [pltpu.make_async_copy(ins[a], slot(a, me), local_sems.at[a]) for a in range(n)]
        self.first, self.over_ici, self.passed, self.from_sibling = [], [], [], []
        for a in range(n):
            self.first.append(copy(a, 0, me, sibling, src=ins[a]))
            self.from_sibling.append(copy(a, 0, sibling, me))
            for j, chip in enumerate(chips):
                self.first.append(copy(a, 1 + j, me, (*chip, c), src=ins[a]))
                self.over_ici.append(copy(a, 1 + j, (*chip, c), me))
                self.passed.append(copy(a, 4 + j, (*chip, c), sibling))
                self.from_sibling.append(copy(a, 4 + j, (*chip, 1 - c), me))

    @staticmethod
    def out_shape(arrs):
        return [jax.ShapeDtypeStruct((N_DEV,) + a.shape, a.dtype) for a in arrs]

    @staticmethod
    def scratch(n):
        return [pltpu.SemaphoreType.DMA((7 * n,)), pltpu.SemaphoreType.DMA((7 * n,)),
                pltpu.SemaphoreType.DMA((n,))]

    def start(self):
        for cp in self.mine + self.first:
            cp.start()

    def finish(self):
        for landed, onward in zip(self.over_ici, self.passed):
            landed.wait_recv()
            onward.start()
        for cp in self.from_sibling:
            cp.wait_recv()
        for cp in self.first + self.passed:
            cp.wait_send()
        for cp in self.mine:
            cp.wait()


class Exchange:
    def __init__(self, sends, recvs, send_sems, recv_sems, local_sems):
        x, y, c = lax.axis_index("x"), lax.axis_index("y"), lax.axis_index("c")
        me = 4 * x + 2 * y + c
        self.own = [pltpu.make_async_copy(s.at[me], r.at[me], local_sems.at[a])
                    for a, (s, r) in enumerate(zip(sends, recvs))]
        self.copies = []
        for a, (s_ref, r_ref) in enumerate(zip(sends, recvs)):
            for k in range(1, N_DEV):
                px = 1 - x if (k >> 2) & 1 else x
                py = 1 - y if (k >> 1) & 1 else y
                pc = 1 - c if k & 1 else c
                self.copies.append(pltpu.make_async_remote_copy(
                    src_ref=s_ref.at[4 * px + 2 * py + pc], dst_ref=r_ref.at[me],
                    send_sem=send_sems.at[7 * a + k - 1], recv_sem=recv_sems.at[7 * a + k - 1],
                    device_id=(px, py, pc), device_id_type=MESH_ID))

    @staticmethod
    def scratch(n):
        return [pltpu.SemaphoreType.DMA((7 * n,)), pltpu.SemaphoreType.DMA((7 * n,)),
                pltpu.SemaphoreType.DMA((n,))]

    def start(self):
        for cp in self.own + self.copies:
            cp.start()

    def finish(self):
        for cp in self.copies:
            cp.wait_send()
            cp.wait_recv()
        for cp in self.own:
            cp.wait()


def host_comm(body, grid, n_in, n_out, gather=(), exchange=()):
    ng, nx = len(gather), len(exchange)
    if ng + nx == 0:
        return body, [], [], [], []

    def hosted(*refs):
        ins, c_in, rest = refs[:n_in], refs[n_in:n_in + ng + nx], refs[n_in + ng + nx:]
        outs, c_out, rest = rest[:n_out], rest[n_out:n_out + ng + nx], rest[n_out + ng + nx:]
        n_sems = 3 * ((ng > 0) + (nx > 0))
        scratch, sems = rest[:len(rest) - n_sems], rest[len(rest) - n_sems:]

        def comms():
            made = [Gather(c_in[:ng], c_out[:ng], *sems[:3])] if ng else []
            return made + ([Exchange(c_in[ng:], c_out[ng:], *sems[-3:])] if nx else [])

        ids = [pl.program_id(a) for a in range(len(grid))]

        @pl.when(functools.reduce(jnp.logical_and, [i == 0 for i in ids]))
        def _():
            for cm in comms():
                cm.start()
        body(*ins, *outs, *scratch)

        @pl.when(functools.reduce(jnp.logical_and, [i == g - 1 for i, g in zip(ids, grid)]))
        def _():
            for cm in comms():
                cm.finish()

    shapes = Gather.out_shape(gather) + [jax.ShapeDtypeStruct(a.shape, a.dtype) for a in exchange]
    scratch = (Gather.scratch(ng) if ng else []) + (Exchange.scratch(nx) if nx else [])
    return hosted, [ANY_SPEC] * (ng + nx), [ANY_SPEC] * (ng + nx), shapes, scratch


def shard_sum(own, recv, onehot, name):
    R, C = own.shape
    tr = _tile(R, 512)

    def body(oh_ref, own_ref, recv_ref, o_ref):
        acc = None
        for j in range(N_DEV):
            term = jnp.where(oh_ref[j] > 0.5, own_ref[...], recv_ref[j].astype(F32))
            acc = term if acc is None else acc + term
        o_ref[...] = acc

    return pl.pallas_call(
        body, name=name, grid=(R // tr,),
        in_specs=[pl.BlockSpec(memory_space=pltpu.SMEM),
                  pl.BlockSpec((tr, C), lambda i: (i, 0)),
                  pl.BlockSpec((N_DEV, tr, C), lambda i: (0, i, 0))],
        out_specs=pl.BlockSpec((tr, C), lambda i: (i, 0)),
        out_shape=jax.ShapeDtypeStruct((R, C), F32), compiler_params=_cp(1),
    )(onehot, own, recv)


def sum_leading(a, name):
    _, R, C = a.shape

    def body(a_ref, o_ref):
        acc = a_ref[0]
        for j in range(1, N_DEV):
            acc = acc + a_ref[j]
        o_ref[...] = acc

    return pl.pallas_call(body, name=name, out_shape=jax.ShapeDtypeStruct((R, C), F32),
                          compiler_params=_cp(0))(a)


def _col_chunks(w, tn):
    if w.ndim == 3:
        return w.shape[0], w.shape[2], pl.BlockSpec((None, w.shape[1], w.shape[2]), lambda i, j: (j, 0, 0))
    return w.shape[1] // tn, tn, pl.BlockSpec((w.shape[0], tn), lambda i, j: (0, j))


def modmm(x, mod3, w, out_dtype, name, tn=None):
    T, Dm = x.shape
    nj, tn, w_spec = _col_chunks(w, tn)
    N = nj * tn
    Bl = mod3.shape[0]
    tm = _tile(T // Bl, 1024)
    tpb = T // Bl // tm

    def body(x_ref, mod_ref, w_ref, o_ref, h_ref, hs):
        @pl.when(pl.program_id(1) == 0)
        def _():
            m = mod_ref[0]
            hs[...] = (x_ref[...] * (1.0 + m[1:2, :]) + m[0:1, :]).astype(BF16)
            h_ref[...] = hs[...]
        o_ref[...] = _dot(hs[...], w_ref[...]).astype(o_ref.dtype)

    return pl.pallas_call(
        body, name=name, grid=(T // tm, nj),
        in_specs=[pl.BlockSpec((tm, Dm), lambda i, j: (i, 0)),
                  pl.BlockSpec((1, 3, Dm), lambda i, j: (i // tpb, 0, 0)), w_spec],
        out_specs=[pl.BlockSpec((tm, tn), lambda i, j: (i, j)),
                   pl.BlockSpec((tm, Dm), lambda i, j: (i, 0))],
        out_shape=[jax.ShapeDtypeStruct((T, N), out_dtype), jax.ShapeDtypeStruct((T, Dm), BF16)],
        scratch_shapes=[pltpu.VMEM((tm, Dm), BF16)], compiler_params=_cp(2),
    )(x, mod3, w)


def modmm_bwd(dp, w, x, mod3, dxres, name, tn=None, tm=1024, exchange=()):
    T, Dm = x.shape
    Bl = mod3.shape[0]
    tm = _tile(T // Bl, tm)
    tpb = T // Bl // tm
    resident = dp.ndim == 3
    if resident:
        nc, nj = dp.shape[0], 1
        dp_spec = pl.BlockSpec((nc, tm, dp.shape[2]), lambda i, j: (0, i, 0))
        w_spec = pl.BlockSpec(w.shape, lambda i, j: (0, 0, 0))
    else:
        nj, tn, w_spec = _col_chunks(w, tn)
        dp_spec = pl.BlockSpec((tm, tn), lambda i, j: (i, j))

    def body(dp_ref, w_ref, x_ref, mod_ref, dxr_ref, dx_ref, dsh_ref, dsc_ref, acc):
        i, j = pl.program_id(0), pl.program_id(1)

        @pl.when(j == 0)
        def _():
            acc[...] = jnp.zeros_like(acc)
        if resident:
            for c in range(nc):
                acc[...] += _dot_nt(dp_ref[c], w_ref[c])
        else:
            acc[...] += _dot_nt(dp_ref[...], w_ref[...])

        @pl.when(j == nj - 1)
        def _():
            dh = acc[...]
            xx = x_ref[...]
            dx_ref[...] = dxr_ref[...] + dh * (1.0 + mod_ref[0][1:2, :])

            @pl.when(i % tpb == 0)
            def _():
                dsh_ref[...] = jnp.zeros_like(dsh_ref)
                dsc_ref[...] = jnp.zeros_like(dsc_ref)
            dsh_ref[0] += _sum0(dh)
            dsc_ref[0] += _sum0(dh * xx)

    grid = (T // tm, nj)
    body, c_in, c_out, c_shape, c_scratch = host_comm(body, grid, 5, 3, exchange=exchange)
    dx, dsh, dsc, *received = pl.pallas_call(
        body, name=name, grid=grid,
        in_specs=[dp_spec, w_spec,
                  pl.BlockSpec((tm, Dm), lambda i, j: (i, 0)),
                  pl.BlockSpec((1, 3, Dm), lambda i, j: (i // tpb, 0, 0)),
                  pl.BlockSpec((tm, Dm), lambda i, j: (i, 0))] + c_in,
        out_specs=[pl.BlockSpec((tm, Dm), lambda i, j: (i, 0)),
                   pl.BlockSpec((1, 1, Dm), lambda i, j: (i // tpb, 0, 0)),
                   pl.BlockSpec((1, 1, Dm), lambda i, j: (i // tpb, 0, 0))] + c_out,
        out_shape=[jax.ShapeDtypeStruct((T, Dm), F32), jax.ShapeDtypeStruct((Bl, 1, Dm), F32),
                   jax.ShapeDtypeStruct((Bl, 1, Dm), F32)] + c_shape,
        scratch_shapes=[pltpu.VMEM((tm, Dm), F32)] + c_scratch, compiler_params=_cp(2),
    )(dp, w, x, mod3, dxres, *exchange)
    return (dx, dsh, dsc, received) if exchange else (dx, dsh, dsc)


def _ln_stats(z):
    mu = jnp.mean(z, axis=-1, keepdims=True)
    zc = z - mu
    var = jnp.mean(zc * zc, axis=-1, keepdims=True)
    rstd = lax.rsqrt(var + LN_EPS)
    return zc * rstd, rstd


def proj_post(a, w, x, mod3, lng, lnb, weight, name):
    nk, T, tk = a.shape
    Dm = w.shape[2]
    Bl = mod3.shape[0]
    tm = _tile(T // Bl, 512)
    tpb = T // Bl // tm

    def body(a_ref, w_ref, x_ref, mod_ref, g_ref, b_ref, out_ref, xn_ref):
        out = _dot(a_ref[0], w_ref[0])
        for k in range(1, nk):
            out = out + _dot(a_ref[k], w_ref[k])
        out_ref[...] = out
        z = ALPHA * x_ref[...] + (weight * (1.0 + mod_ref[0][2:3, :])) * out
        xhat, _ = _ln_stats(z)
        xn_ref[...] = xhat * g_ref[...] + b_ref[...]

    row = pl.BlockSpec((tm, Dm), lambda i: (i, 0))
    vec = pl.BlockSpec((1, Dm), lambda i: (0, 0))
    return pl.pallas_call(
        body, name=name, grid=(T // tm,),
        in_specs=[pl.BlockSpec((nk, tm, tk), lambda i: (0, i, 0)),
                  pl.BlockSpec((nk, tk, Dm), lambda i: (0, 0, 0)),
                  row, pl.BlockSpec((1, 3, Dm), lambda i: (i // tpb, 0, 0)), vec, vec],
        out_specs=[row, row],
        out_shape=[jax.ShapeDtypeStruct((T, Dm), F32), jax.ShapeDtypeStruct((T, Dm), F32)],
        compiler_params=_cp(1),
    )(a, w, x, mod3, lng, lnb)


def post_bwd(dxn, x, out, mod3, lng, w, weight, name, tm=512, gu=None, exchange=()):
    T, Dm = x.shape
    nk, tk, _ = w.shape
    Bl = mod3.shape[0]
    tm = _tile(T // Bl, tm)
    tpb = T // Bl // tm
    fused = gu is not None

    def body(dxn_ref, x_ref, out_ref, mod_ref, g_ref, w_ref, *rest):
        if fused:
            gg_ref, uu_ref = rest[:2]
            rest = rest[2:]
        dxr_ref, dout_ref, da_ref, dg_ref, db_ref, dgate_ref = rest
        i = pl.program_id(0)
        out = out_ref[...]
        dxn = dxn_ref[...]
        coef = weight * (1.0 + mod_ref[0][2:3, :])
        xhat, rstd = _ln_stats(ALPHA * x_ref[...] + coef * out)
        dyh = dxn * g_ref[...]
        dz = rstd * (dyh - jnp.mean(dyh, axis=-1, keepdims=True)
                     - xhat * jnp.mean(dyh * xhat, axis=-1, keepdims=True))
        dxr_ref[...] = ALPHA * dz
        dout = (coef * dz).astype(BF16)
        dout_ref[...] = dout

        @pl.when(i == 0)
        def _():
            dg_ref[...] = jnp.zeros_like(dg_ref)
            db_ref[...] = jnp.zeros_like(db_ref)

        @pl.when(i % tpb == 0)
        def _():
            dgate_ref[...] = jnp.zeros_like(dgate_ref)
        dg_ref[...] += _sum0(dxn * xhat)
        db_ref[...] += _sum0(dxn)
        dgate_ref[0] += _sum0((weight * out) * dz)
        for k in range(nk):
            da = _dot_nt(dout, w_ref[k])
            if fused:
                gg = gg_ref[k].astype(F32)
                s = _sigmoid(gg)
                da_ref[k] = (da * uu_ref[k].astype(F32) * (s * (1.0 + gg * (1.0 - s)))).astype(BF16)
                da_ref[nk + k] = (da * (gg * s)).astype(BF16)
            else:
                da_ref[k] = da.astype(BF16)

    row = pl.BlockSpec((tm, Dm), lambda i: (i, 0))
    vec = pl.BlockSpec((1, Dm), lambda i: (0, 0))
    wide = pl.BlockSpec((nk, tm, tk), lambda i: (0, i, 0))
    nda = 2 * nk if fused else nk
    grid = (T // tm,)
    body, c_in, c_out, c_shape, c_scratch = host_comm(body, grid, 8 if fused else 6, 6, exchange=exchange)
    *results, = pl.pallas_call(
        body, name=name, grid=grid,
        in_specs=[row, row, row, pl.BlockSpec((1, 3, Dm), lambda i: (i // tpb, 0, 0)), vec,
                  pl.BlockSpec((nk, tk, Dm), lambda i: (0, 0, 0))] + ([wide, wide] if fused else []) + c_in,
        out_specs=[row, row, pl.BlockSpec((nda, tm, tk), lambda i: (0, i, 0)),
                   vec, vec, pl.BlockSpec((1, 1, Dm), lambda i: (i // tpb, 0, 0))] + c_out,
        out_shape=[jax.ShapeDtypeStruct((T, Dm), F32), jax.ShapeDtypeStruct((T, Dm), BF16),
                   jax.ShapeDtypeStruct((nda, T, tk), BF16), jax.ShapeDtypeStruct((1, Dm), F32),
                   jax.ShapeDtypeStruct((1, Dm), F32), jax.ShapeDtypeStruct((Bl, 1, Dm), F32)] + c_shape,
        scratch_shapes=c_scratch, compiler_params=_cp(1),
    )(dxn, x, out, mod3, lng, w, *(gu if fused else ()), *exchange)
    return tuple(results[:6]) + ((results[6:],) if exchange else ())


def mm_tn(a, b, name, bw=None, a_copies=False, exchange=()):
    a3, b3 = a.ndim == 3, b.ndim == 3
    nk, T, tk = a.shape if a3 else (1,) + a.shape
    if a_copies:
        nk = 1
    nc, wn = (b.shape[0], b.shape[2]) if b3 else (b.shape[1] // bw, bw)
    tt = _tile(T, 2048)
    nt = T // tt

    def body(a_ref, b_ref, o_ref, ob_ref):
        t = pl.program_id(2)

        @pl.when(t == 0)
        def _():
            o_ref[...] = jnp.zeros_like(o_ref)
        o_ref[...] += _dot_tn(a_ref[...], b_ref[...])

        @pl.when(t == nt - 1)
        def _():
            ob_ref[...] = o_ref[...].astype(BF16)

    a_spec = (pl.BlockSpec((None, tt, tk), lambda k, c, t: (k, t, 0)) if a3
              else pl.BlockSpec((tt, tk), lambda k, c, t: (t, 0)))
    b_spec = (pl.BlockSpec((None, tt, wn), lambda k, c, t: (c, t, 0)) if b3
              else pl.BlockSpec((tt, wn), lambda k, c, t: (t, c)))
    o_spec = pl.BlockSpec((None, tk, wn), lambda k, c, t: (k * nc + c, 0, 0))
    grid = (nk, nc, nt)
    body, c_in, c_out, c_shape, c_scratch = host_comm(body, grid, 2, 2, exchange=exchange)
    o32, o16, *received = pl.pallas_call(
        body, name=name, grid=grid, in_specs=[a_spec, b_spec] + c_in, out_specs=[o_spec, o_spec] + c_out,
        out_shape=[jax.ShapeDtypeStruct((nk * nc, tk, wn), F32), jax.ShapeDtypeStruct((nk * nc, tk, wn), BF16)]
        + c_shape,
        scratch_shapes=c_scratch, compiler_params=_cp(3),
    )(a, b, *exchange)
    return (o32, o16, received) if exchange else (o32, o16)


def ffn_in(x, mod3, w, name, gather=()):
    T, Dm = x.shape
    nj, tf = w.shape[0] // 2, w.shape[2]
    Bl = mod3.shape[0]
    tm = _tile(T // Bl, 1024)
    tpb = T // Bl // tm

    def body(x_ref, mod_ref, wg_ref, wu_ref, a_ref, g_ref, u_ref, h_ref):
        m = mod_ref[0]
        h = (x_ref[...] * (1.0 + m[1:2, :]) + m[0:1, :]).astype(BF16)
        h_ref[...] = h
        g = _dot(h, wg_ref[...])
        u = _dot(h, wu_ref[...])
        a_ref[...] = (g * _sigmoid(g) * u).astype(BF16)
        g_ref[...] = g.astype(BF16)
        u_ref[...] = u.astype(BF16)

    col = pl.BlockSpec((None, tm, tf), lambda j, i: (j, i, 0))
    grid = (nj, T // tm)
    body, c_in, c_out, c_shape, c_scratch = host_comm(body, grid, 4, 4, gather=gather)
    a, g, u, h, *gathered = pl.pallas_call(
        body, name=name, grid=grid,
        in_specs=[pl.BlockSpec((tm, Dm), lambda j, i: (i, 0)),
                  pl.BlockSpec((1, 3, Dm), lambda j, i: (i // tpb, 0, 0)),
                  pl.BlockSpec((None, Dm, tf), lambda j, i: (j, 0, 0)),
                  pl.BlockSpec((None, Dm, tf), lambda j, i: (nj + j, 0, 0))] + c_in,
        out_specs=[col, col, col, pl.BlockSpec((None, tm, Dm), lambda j, i: (j, i, 0))] + c_out,
        out_shape=[jax.ShapeDtypeStruct((nj, T, tf), BF16)] * 3 + [jax.ShapeDtypeStruct((nj, T, Dm), BF16)]
        + c_shape,
        scratch_shapes=c_scratch, compiler_params=_cp(2),
    )(x, mod3, w, w, *gather)
    return a, g, u, h, gathered


def loss_head(y, tgt, name):
    T, Dm = y.shape
    tm = _tile(T, 512)
    nt = T // tm

    def body(y_ref, t_ref, dy_ref, l_ref, acc):
        i = pl.program_id(0)

        @pl.when(i == 0)
        def _():
            acc[...] = jnp.zeros_like(acc)
        e = y_ref[...] - t_ref[...]
        dy_ref[...] = e * (1.0 / Dm)
        acc[...] += _sum0(e * e)

        @pl.when(i == nt - 1)
        def _():
            l_ref[...] = jnp.broadcast_to(_sum1(acc[...]) * (0.5 / Dm), l_ref.shape)

    return pl.pallas_call(
        body, name=name, grid=(nt,),
        in_specs=[pl.BlockSpec((tm, Dm), lambda i: (i, 0)), pl.BlockSpec((tm, Dm), lambda i: (i, 0))],
        out_specs=[pl.BlockSpec((tm, Dm), lambda i: (i, 0)), pl.BlockSpec((1, 128), lambda i: (0, 0))],
        out_shape=[jax.ShapeDtypeStruct((T, Dm), F32), jax.ShapeDtypeStruct((1, 128), F32)],
        scratch_shapes=[pltpu.VMEM((1, Dm), F32)], compiler_params=_cp(1),
    )(y, tgt)


def adamw(w, g, m, v, name):
    R, C = w.shape
    tr = _tile(R, 512) if R % 8 == 0 else R

    def body(w_ref, g_ref, m_ref, v_ref, d_ref, nm_ref, nv_ref):
        gg = g_ref[...]
        mm = ADAM_B1 * m_ref[...] + (1.0 - ADAM_B1) * gg
        vv = ADAM_B2 * v_ref[...] + (1.0 - ADAM_B2) * (gg * gg)
        m_hat = mm / (1.0 - ADAM_B1 ** ADAM_STEP)
        v_hat = vv / (1.0 - ADAM_B2 ** ADAM_STEP)
        d_ref[...] = -ADAM_LR * (m_hat / (jnp.sqrt(v_hat) + ADAM_EPS) + ADAM_WD * w_ref[...])
        nm_ref[...] = mm
        nv_ref[...] = vv

    spec = pl.BlockSpec((tr, C), lambda i: (i, 0))
    return pl.pallas_call(
        body, name=name, grid=(R // tr,), in_specs=[spec] * 4, out_specs=[spec] * 3,
        out_shape=[jax.ShapeDtypeStruct((R, C), F32)] * 3, compiler_params=_cp(1),
    )(w, g, m, v)


def ada_fwd(c_all, ada_w, ada_b_cols, name):
    Lr, Dm, Nc = ada_w.shape
    Bg = c_all.shape[0]

    def body(c_ref, w_ref, b_ref, o_ref):
        cc = c_ref[...]
        cond = cc * _sigmoid(cc)
        o_ref[0] = _dot(cond.astype(BF16), w_ref[0].astype(BF16)) + b_ref[0]

    return pl.pallas_call(
        body, name=name, grid=(Lr,),
        in_specs=[pl.BlockSpec((Bg, Dm), lambda l: (0, 0)),
                  pl.BlockSpec((1, Dm, Nc), lambda l: (l, 0, 0)),
                  pl.BlockSpec((1, 1, Nc), lambda l: (l, 0, 0))],
        out_specs=pl.BlockSpec((1, Bg, Nc), lambda l: (l, 0, 0)),
        out_shape=jax.ShapeDtypeStruct((Lr, Bg, Nc), F32), compiler_params=_cp(1),
    )(c_all, ada_w, ada_b_cols)


def ada_bwd(c_all_t, dmod_cols, dmod_all, name):
    Dm, Bg = c_all_t.shape
    Lr, _, Nc = dmod_cols.shape
    Nf = dmod_all.shape[2]

    def body(c_ref, dm_ref, da_ref, gw_ref, gb_ref):
        cc = c_ref[...]
        cond = cc * _sigmoid(cc)
        gw_ref[0] = _dot(cond.astype(BF16), dm_ref[0].astype(BF16))
        gb_ref[0] = _sum0(da_ref[0])

    return pl.pallas_call(
        body, name=name, grid=(Lr,),
        in_specs=[pl.BlockSpec((Dm, Bg), lambda l: (0, 0)),
                  pl.BlockSpec((1, Bg, Nc), lambda l: (l, 0, 0)),
                  pl.BlockSpec((1, Bg, Nf), lambda l: (l, 0, 0))],
        out_specs=[pl.BlockSpec((1, Dm, Nc), lambda l: (l, 0, 0)),
                   pl.BlockSpec((1, 1, Nf), lambda l: (l, 0, 0))],
        out_shape=[jax.ShapeDtypeStruct((Lr, Dm, Nc), F32), jax.ShapeDtypeStruct((Lr, 1, Nf), F32)],
        compiler_params=_cp(1),
    )(c_all_t, dmod_cols, dmod_all)


def _conv_taps(x, w, rows):
    shifted = [x]
    c = w[3:4, :] * x
    for k in range(1, 4):
        xs = jnp.where(rows >= k, pltpu.roll(x, k, 0), 0.0)
        shifted.append(xs)
        c = c + w[3 - k:4 - k, :] * xs
    return c, shifted


def conv_silu(proj3, conv_w, name):
    Bl, S, _ = proj3.shape
    ncb = conv_w.shape[1] // 128

    def body(x_ref, w_ref, o_ref):
        rows = lax.broadcasted_iota(jnp.int32, (S, 128), 0)
        c, _ = _conv_taps(_round(x_ref[0]), _round(w_ref[...]), rows)
        o_ref[0] = c * _sigmoid(c)

    return pl.pallas_call(
        body, name=name, grid=(Bl, ncb),
        in_specs=[pl.BlockSpec((1, S, 128), lambda b, j: (b, 0, j)),
                  pl.BlockSpec((4, 128), lambda b, j: (0, j))],
        out_specs=pl.BlockSpec((1, S, 128), lambda b, j: (b, 0, j)),
        out_shape=jax.ShapeDtypeStruct((Bl, S, conv_w.shape[1]), F32), compiler_params=_cp(2),
    )(proj3, conv_w)


def conv_silu_bwd(proj3, conv_w, dq, dk, name):
    Bl, S, _ = proj3.shape
    nq = dq.shape[2] // 128

    def body(x_ref, w_ref, dq_ref, dk_ref, dx_ref, dw_ref):
        j = pl.program_id(1)
        rows = lax.broadcasted_iota(jnp.int32, (S, 128), 0)
        w = _round(w_ref[...])
        c, shifted = _conv_taps(_round(x_ref[0]), w, rows)
        s = _sigmoid(c)
        dact = jnp.where(j < nq, dq_ref[0], dk_ref[0])
        dc = _round(dact * (s * (1.0 + c * (1.0 - s))))
        dx = w[3:4, :] * dc
        dws = [_sum0(dc * shifted[0])]
        for k in range(1, 4):
            up = jnp.where(rows < S - k, pltpu.roll(dc, S - k, 0), 0.0)
            dx = dx + w[3 - k:4 - k, :] * up
            dws.append(_sum0(dc * shifted[k]))
        dx_ref[0] = dx.astype(BF16)
        tap = lax.broadcasted_iota(jnp.int32, (4, 128), 0)
        dw_ref[0] = functools.reduce(lambda a, b: a + b, [jnp.where(tap == 3 - k, dws[k], 0.0) for k in range(4)])

    return pl.pallas_call(
        body, name=name, grid=(Bl, 2 * nq),
        in_specs=[pl.BlockSpec((1, S, 128), lambda b, j: (b, 0, j)),
                  pl.BlockSpec((4, 128), lambda b, j: (0, j)),
                  pl.BlockSpec((1, S, 128), lambda b, j: (b, 0, jnp.minimum(j, nq - 1))),
                  pl.BlockSpec((1, S, 128), lambda b, j: (b, 0, jnp.maximum(j - nq, 0)))],
        out_specs=[pl.BlockSpec((1, S, 128), lambda b, j: (b, 0, j)),
                   pl.BlockSpec((1, 4, 128), lambda b, j: (b, 0, j))],
        out_shape=[jax.ShapeDtypeStruct((Bl, S, 2 * nq * 128), BF16),
                   jax.ShapeDtypeStruct((Bl, 4, 2 * nq * 128), F32)],
        compiler_params=_cp(2),
    )(proj3, conv_w, dq, dk)


def _log_sigmoid(a):
    return jnp.minimum(a, 0.0) - jnp.log(1.0 + jnp.exp(-jnp.abs(a)))


def _interleave(gens):
    live = list(gens)
    while live:
        still = []
        for g in live:
            try:
                next(g)
                still.append(g)
            except StopIteration:
                pass
        live = still


def _finish(gen):
    while True:
        try:
            next(gen)
        except StopIteration as done:
            return done.value


def _chunk_state(kc, vc, gi, bcum, b_last, C, n, m):
    a = b_last - bcum + gi
    m_loc = jnp.max(a, axis=0, keepdims=True)
    wa = jnp.exp(a - m_loc)
    c_loc = _dot_tn((wa * vc).astype(BF16), kc.astype(BF16))
    n_loc = _sum0(_round(wa) * _round(kc))
    m_new = jnp.maximum(b_last + m, m_loc)
    sp = jnp.exp(b_last + m - m_new)
    sl = jnp.exp(m_loc - m_new)
    yield
    return sp * C + sl * c_loc, sp * n + sl * n_loc, m_new, wa, sp, sl


def _chunk_out(qs, kc, vc, gi_row, bcum, bcum_row, low, C, n, m):
    inter_log = bcum + m
    dlog = jnp.where(low, bcum - bcum_row + gi_row, NEG)
    m_i = jnp.maximum(inter_log, jnp.max(dlog, axis=1, keepdims=True))
    dm = jnp.exp(dlog - m_i)
    iw = jnp.exp(inter_log - m_i)
    qs_b, k_b, v_b = qs.astype(BF16), kc.astype(BF16), vc.astype(BF16)
    sqk = _dot_nt(qs_b, k_b)
    qc_ = _dot_nt(qs_b, C.astype(BF16))
    qn = _sum1(_round(qs) * _round(n))
    floor = jnp.exp(-m_i)
    yield
    sc = sqk * dm
    sv = _dot(sc.astype(BF16), v_b)
    den = _sum1(sc) + iw * qn
    dn = jnp.maximum(jnp.abs(den), floor)
    yield
    num = sv + iw * qc_
    return dict(hc=num / dn, den=den, dn=dn, floor=floor, sc=sc, dm=dm, iw=iw, qc=qc_, qn=qn,
                qs_b=qs_b, k_b=k_b, v_b=v_b)


def _cell_consts(L):
    ri = lax.broadcasted_iota(jnp.int32, (L, L), 0)
    ci = lax.broadcasted_iota(jnp.int32, (L, L), 1)
    return ri == ci, ci <= ri, ri <= ci


def _load_chunk(q_ref, k_ref, v_ref, G, off, L, h, lane):
    hh = h % 2
    qmask = (lane >= M_DQK * hh) & (lane < M_DQK * (hh + 1))
    pair = pl.ds(128 * (h // 2), 128)
    qc = jnp.where(qmask, q_ref[0, pl.ds(off, L), pair], 0.0)
    kc = jnp.where(qmask, k_ref[0, pl.ds(off, L), pair], 0.0)
    vc = v_ref[0, pl.ds(off, L), pl.ds(M_DV * h, M_DV)]
    gi = _sum1(jnp.where(lane == h, G, 0.0))
    gf = _sum1(jnp.where(lane == h + HEADS, G, 0.0))
    return qmask, qc, kc, vc, gi, gf


def _gate_rows(gi, gf, eye, low, upp):
    lf = _log_sigmoid(gf)
    lf_row = _sum0(jnp.where(eye, lf, 0.0))
    gi_row = _sum0(jnp.where(eye, gi, 0.0))
    bcum = _sum1(jnp.where(low, lf_row, 0.0))
    bcum_row = _sum0(jnp.where(upp, lf, 0.0))
    b_last = _sum0(lf)
    return gi_row, bcum, bcum_row, b_last


def _cell_specs(SB, cpb, blk):
    def seq(width, col):
        return pl.BlockSpec((1, SB, width), lambda b, s: (b, blk(s), col))

    def state(rows):
        return pl.BlockSpec((1, HEADS, cpb, rows, 128), lambda b, s: (b, 0, blk(s), 0, 0))

    ins = [seq(D // 2, 0), seq(D // 2, 1), seq(D, 1), seq(D, 2), seq(128, 3 * D // 128),
           pl.BlockSpec((1, D), lambda b, s: (0, 0)), pl.BlockSpec((1, 128), lambda b, s: (0, 0))]
    return ins, [state(M_DV), state(1), state(1)], seq


def mlstm_cell_fwd(qk3, proj3, gain, gbias, name, gather=()):
    Bl, S, _ = qk3.shape
    L = M_CHUNK
    SB = min(M_SLAB, S)
    cpb, nc, nsb = SB // L, S // L, S // SB
    scale = M_DQK ** -0.5

    def body(q_ref, k_ref, v_ref, o_ref, g_ref, gain_ref, gb_ref, y_ref, cst_ref, nst_ref, mst_ref, *state):
        C_s, n_s, m_s = state[:HEADS], state[HEADS:2 * HEADS], state[2 * HEADS:]

        @pl.when(pl.program_id(1) == 0)
        def _():
            for ref in state:
                ref[...] = jnp.zeros_like(ref)
        lane = lax.broadcasted_iota(jnp.int32, (L, 128), 1)
        eye, low, upp = _cell_consts(L)

        def step(c, carry):
            off = pl.multiple_of(c * L, L)
            G = g_ref[0, pl.ds(off, L), :] + gb_ref[...]

            def head(h):
                C, n, mb = C_s[h][...], n_s[h][...], m_s[h][...]
                cst_ref[0, h, c] = C
                nst_ref[0, h, c] = n
                mst_ref[0, h, c] = mb
                m = mb[:, 0:1]
                _, qc, kc, vc, gi, gf = _load_chunk(q_ref, k_ref, v_ref, G, off, L, h, lane)
                gi_row, bcum, bcum_row, b_last = _gate_rows(gi, gf, eye, low, upp)
                state = _chunk_state(kc, vc, gi, bcum, b_last, C, n, m)
                next(state)
                r = yield from _chunk_out(qc * scale, kc, vc, gi_row, bcum, bcum_row, low, C, n, m)
                hc = r["hc"]
                hn = hc * lax.rsqrt(jnp.mean(hc * hc, axis=-1, keepdims=True) + RMS_EPS)
                cols = pl.ds(M_DV * h, M_DV)
                oc = o_ref[0, pl.ds(off, L), cols]
                y_ref[0, pl.ds(off, L), cols] = (_sigmoid(oc) * hn * gain_ref[:, cols]).astype(BF16)
                C2, n2, m2, _, _, _ = _finish(state)
                C_s[h][...] = C2
                n_s[h][...] = n2
                m_s[h][...] = jnp.broadcast_to(m2, (1, 128))

            _interleave(head(h) for h in range(HEADS))
            return carry

        lax.fori_loop(0, cpb, step, 0)

    ins, states, seq = _cell_specs(SB, cpb, lambda s: s)
    grid = (Bl, nsb)
    body, c_in, c_out, c_shape, c_scratch = host_comm(body, grid, 7, 4, gather=gather)
    return pl.pallas_call(
        body, name=name, grid=grid, in_specs=ins + c_in, out_specs=[seq(D, 0)] + states + c_out,
        out_shape=[jax.ShapeDtypeStruct((Bl, S, D), BF16),
                   jax.ShapeDtypeStruct((Bl, HEADS, nc, M_DV, 128), F32),
                   jax.ShapeDtypeStruct((Bl, HEADS, nc, 1, 128), F32),
                   jax.ShapeDtypeStruct((Bl, HEADS, nc, 1, 128), F32)] + c_shape,
        scratch_shapes=[pltpu.VMEM((M_DV, 128), F32)] * HEADS + [pltpu.VMEM((1, 128), F32)] * (2 * HEADS) + c_scratch,
        compiler_params=_cp(2),
    )(qk3, qk3, proj3, proj3, proj3, gain, gbias, *gather)


def mlstm_cell_bwd(qk3, proj3, gain, gbias, dy3, states, name, exchange=()):
    Bl, S, _ = qk3.shape
    L = M_CHUNK
    SB = min(M_SLAB, S)
    cpb, nsb = SB // L, S // SB
    scale = M_DQK ** -0.5

    def body(q_ref, k_ref, v_ref, o_ref, g_ref, gain_ref, gb_ref, cst_ref, nst_ref, mst_ref, dy_ref,
             dq_ref, dk_ref, dv_ref, do_ref, dg_ref, dgain_ref, dgb_ref, *state):
        dC_s, dn_s, dgain_s, dgb_s = state[:HEADS], state[HEADS:2 * HEADS], state[2 * HEADS:3 * HEADS], state[-1]
        s = pl.program_id(1)

        @pl.when(s == 0)
        def _():
            for ref in state:
                ref[...] = jnp.zeros_like(ref)
        lane = lax.broadcasted_iota(jnp.int32, (L, 128), 1)
        rowi = lax.broadcasted_iota(jnp.int32, (L, 1), 0)
        eye, low, upp = _cell_consts(L)

        def bstep(t, carry):
            c = cpb - 1 - t
            off = pl.multiple_of(c * L, L)
            G = g_ref[0, pl.ds(off, L), :] + gb_ref[...]
            shared = dict(slab=jnp.zeros((L, 128), F32))

            def head(h):
                cols = pl.ds(M_DV * h, M_DV)
                gain_h = gain_ref[:, cols]
                C, n, m = cst_ref[0, h, c], nst_ref[0, h, c], mst_ref[0, h, c][:, 0:1]
                dC_n, dn_n = dC_s[h][...], dn_s[h][...]
                qmask, qc, kc, vc, gi, gf = _load_chunk(q_ref, k_ref, v_ref, G, off, L, h, lane)
                gi_row, bcum, bcum_row, b_last = _gate_rows(gi, gf, eye, low, upp)
                qs = qc * scale
                _, _, _, wa, sp, sl = _finish(_chunk_state(kc, vc, gi, bcum, b_last, C, n, m))
                dcl_b = (sl * dC_n).astype(BF16)
                t1_mm = _dot(vc.astype(BF16), dcl_b)
                dv_mm = _dot_nt(kc.astype(BF16), dcl_b)
                r = yield from _chunk_out(qs, kc, vc, gi_row, bcum, bcum_row, low, C, n, m)
                hc, den, dn, sc, dm, iw, qn = r["hc"], r["den"], r["dn"], r["sc"], r["dm"], r["iw"], r["qn"]
                qs_b, k_b, v_b = r["qs_b"], r["k_b"], r["v_b"]
                dy = dy_ref[0, pl.ds(off, L), cols].astype(F32)
                oc = o_ref[0, pl.ds(off, L), cols]
                sig_o = _sigmoid(oc)
                rr = lax.rsqrt(jnp.mean(hc * hc, axis=-1, keepdims=True) + RMS_EPS)
                hn = hc * rr
                dgain_s[h][...] += _sum0(dy * sig_o * hn)
                do_ref[0, pl.ds(off, L), cols] = (
                    dy * hn * gain_h * sig_o * (1.0 - sig_o)).astype(BF16)
                dhn = dy * sig_o * gain_h
                dhc = rr * dhn - hc * (rr * rr * rr) * jnp.mean(dhn * hc, axis=-1, keepdims=True)
                dnum = dhc / dn
                gden = -_sum1(dhc * hc) / dn
                dden = jnp.where(jnp.abs(den) > r["floor"], gden * jnp.sign(den), 0.0)
                dnum_b = dnum.astype(BF16)
                dqc_b = (iw * dnum).astype(BF16)
                dsc_mm = _dot_nt(dnum_b, v_b)
                dv = _dot_tn(sc.astype(BF16), dnum_b)
                dqs_mm = _dot(dqc_b, C.astype(BF16))
                dC_out = _dot_tn(dqc_b, qs_b)
                diw = _sum1(dnum * r["qc"]) + dden * qn
                wq = iw * dden
                dn_out = _sum0(wq * qs)
                dn_loc = sl * dn_n
                dsp = _sum1(_sum0(dC_n * C)) + _sum1(dn_n * n)
                yield
                dsc = dsc_mm + dden
                dS_b = (dsc * dm).astype(BF16)
                gm = dsc * sc
                dqs2_mm = _dot(dS_b, k_b)
                dk = _dot_tn(dS_b, qs_b)
                dqs = dqs_mm + wq * n
                dbc = _sum1(gm) + diw * iw
                colg = _sum0(gm)
                dC_p = sp * dC_n + dC_out
                dn_p = sp * dn_n + dn_out
                db_last = dsp * sp
                t1 = t1_mm + dn_loc
                dwa = _sum1(t1 * kc)
                dv = dv + wa * dv_mm
                yield
                dqs = dqs + dqs2_mm
                dk = dk + wa * t1
                da = dwa * wa
                db_last = db_last + _sum0(da)
                dbc = dbc - da + jnp.where(rowi == L - 1, db_last, 0.0)
                dbc_row = _sum0(jnp.where(eye, dbc, 0.0)) - colg
                dgi = da + _sum1(jnp.where(eye, colg, 0.0))
                dlf = _sum1(jnp.where(upp, dbc_row, 0.0))
                dgf = dlf * _sigmoid(-gf)
                dq = jnp.where(qmask, dqs * scale, 0.0)
                dk = jnp.where(qmask, dk, 0.0)
                shared["slab"] = (shared["slab"] + jnp.where(lane == h, dgi, 0.0)
                                  + jnp.where(lane == h + HEADS, dgf, 0.0))
                dv_ref[0, pl.ds(off, L), cols] = dv.astype(BF16)
                dC_s[h][...] = dC_p
                dn_s[h][...] = dn_p
                if h % 2 == 0:
                    shared["dq"], shared["dk"] = dq, dk
                else:
                    pair = pl.ds(128 * (h // 2), 128)
                    dq_ref[0, pl.ds(off, L), pair] = shared["dq"] + dq
                    dk_ref[0, pl.ds(off, L), pair] = shared["dk"] + dk

            _interleave(head(h) for h in range(HEADS))
            dg_ref[0, pl.ds(off, L), :] = shared["slab"]
            dgb_s[...] += _sum0(shared["slab"])
            return carry

        lax.fori_loop(0, cpb, bstep, 0)

        @pl.when(s == nsb - 1)
        def _():
            for h in range(HEADS):
                dgain_ref[0, :, pl.ds(M_DV * h, M_DV)] = dgain_s[h][...]
            dgb_ref[0] = dgb_s[...]

    ins, states_specs, seq = _cell_specs(SB, cpb, lambda s: nsb - 1 - s)
    once = lambda width: pl.BlockSpec((1, 1, width), lambda b, s: (b, 0, 0))
    grid = (Bl, nsb)
    body, c_in, c_out, c_shape, c_scratch = host_comm(body, grid, 11, 7, exchange=exchange)
    return pl.pallas_call(
        body, name=name, grid=grid, in_specs=ins + states_specs + [seq(D, 0)] + c_in,
        out_specs=[seq(D // 2, 0), seq(D // 2, 0), seq(D, 0), seq(D, 0), seq(128, 0), once(D), once(128)] + c_out,
        out_shape=[jax.ShapeDtypeStruct((Bl, S, D // 2), F32), jax.ShapeDtypeStruct((Bl, S, D // 2), F32),
                   jax.ShapeDtypeStruct((Bl, S, D), BF16), jax.ShapeDtypeStruct((Bl, S, D), BF16),
                   jax.ShapeDtypeStruct((Bl, S, 128), F32), jax.ShapeDtypeStruct((Bl, 1, D), F32),
                   jax.ShapeDtypeStruct((Bl, 1, 128), F32)] + c_shape,
        scratch_shapes=[pltpu.VMEM((M_DV, 128), F32)] * HEADS + [pltpu.VMEM((1, 128), F32)] * (2 * HEADS + 1)
        + c_scratch,
        compiler_params=_cp(2),
    )(qk3, qk3, proj3, proj3, proj3, gain, gbias, *states, dy3, *exchange)


def _attn_scores(q, kc, kp, n, row, col, scale):
    s_c = jnp.where(col <= row, _dot_nt(q, kc) * scale, NEG)
    s_p = jnp.where(jnp.logical_and(col >= row, n > 0), _dot_nt(q, kp) * scale, NEG)
    return s_c, s_p


def _to_streams(src, dst, tmp, dil, Sd):
    if dil == 1:
        dst[...] = src[...].astype(dst.dtype)
        return
    if src.dtype != F32:
        tmp[...] = src[...].astype(F32)
        src = tmp
    for r in range(dil):
        dst[pl.ds(r * Sd, Sd), :] = src[pl.ds(r, Sd, stride=dil), :].astype(dst.dtype)


def _from_streams(src, dst, dil, Sd):
    if dil == 1:
        dst[...] = src[...]
        return
    for r in range(dil):
        dst[pl.ds(r, Sd, stride=dil), :] = src[pl.ds(r * Sd, Sd), :]


def attn_fwd(proj, Bl, S, g, dil, name):
    Sd = S // dil
    nb = Sd // A_BLK
    scale = A_BLK ** -0.5
    pv = proj.reshape(Bl, S, A_PROJ)

    def body(q_ref, k_ref, v_ref, o_ref, l_ref, tmp, qs, ks, vs, os_, ls):
        row = lax.broadcasted_iota(jnp.int32, (A_BLK, A_BLK), 0)
        col = lax.broadcasted_iota(jnp.int32, (A_BLK, A_BLK), 1)
        for src, dst in ((q_ref, qs), (k_ref, ks), (v_ref, vs)):
            _to_streams(src.at[0], dst, tmp, dil, Sd)

        def step(i, carry):
            n = i % nb
            off = pl.multiple_of(i * A_BLK, A_BLK)
            offp = pl.multiple_of(jnp.maximum(i - 1, 0) * A_BLK, A_BLK)
            q = qs[pl.ds(off, A_BLK), :]
            s_c, s_p = _attn_scores(q, ks[pl.ds(off, A_BLK), :], ks[pl.ds(offp, A_BLK), :], n, row, col, scale)
            m = jnp.maximum(jnp.max(s_c, axis=1, keepdims=True), jnp.max(s_p, axis=1, keepdims=True))
            p_c = jnp.exp(s_c - m)
            p_p = jnp.exp(s_p - m)
            den = _sum1(p_c) + _sum1(p_p)
            o = _dot(p_c.astype(BF16), vs[pl.ds(off, A_BLK), :]) + _dot(p_p.astype(BF16), vs[pl.ds(offp, A_BLK), :])
            os_[pl.ds(off, A_BLK), :] = o / den
            ls[pl.ds(off, A_BLK), :] = jnp.broadcast_to(m + jnp.log(den), (A_BLK, 128))
            return carry

        lax.fori_loop(0, dil * nb, step, 0, unroll=A_UNROLL)
        _from_streams(os_, o_ref.at[0], dil, Sd)
        _from_streams(ls, l_ref.at[0], dil, Sd)

    def spec(j):
        return pl.BlockSpec((1, S, 128), lambda b, h: (b, 0, g * 24 + j * HEADS + h))

    ospec = pl.BlockSpec((1, S, 128), lambda b, h: (b, 0, h))
    o, lse = pl.pallas_call(
        body, name=name, grid=(Bl, HEADS),
        in_specs=[spec(0), spec(1), spec(2)], out_specs=[ospec, ospec],
        out_shape=[jax.ShapeDtypeStruct((Bl, S, D), F32)] * 2,
        scratch_shapes=[pltpu.VMEM((S, 128), F32)] + [pltpu.VMEM((S, 128), BF16)] * 3 + [pltpu.VMEM((S, 128), F32)] * 2,
        compiler_params=_cp(2),
    )(pv, pv, pv)
    return o.reshape(Bl * S, D), lse.reshape(Bl * S, D)


def attn_merge(os_, lses, name):
    T = os_[0].shape[0]
    tm = _tile(T, 512)
    ng = len(os_)

    def body(*refs):
        o_refs, l_refs = refs[:ng], refs[ng:2 * ng]
        ob_ref, of_ref, lt_ref = refs[2 * ng:]
        ls = [r[...] for r in l_refs]
        m = functools.reduce(jnp.maximum, ls)
        ws = [jnp.exp(l - m) for l in ls]
        den = functools.reduce(lambda a, b: a + b, ws)
        o = functools.reduce(lambda a, b: a + b, [w * r[...] for w, r in zip(ws, o_refs)]) / den
        of_ref[...] = o
        ob_ref[...] = o.astype(BF16)
        lt_ref[...] = m + jnp.log(den)

    spec = pl.BlockSpec((tm, D), lambda i: (i, 0))
    return pl.pallas_call(
        body, name=name, grid=(T // tm,), in_specs=[spec] * (2 * ng), out_specs=[spec] * 3,
        out_shape=[jax.ShapeDtypeStruct((T, D), BF16), jax.ShapeDtypeStruct((T, D), F32),
                   jax.ShapeDtypeStruct((T, D), F32)],
        compiler_params=_cp(1),
    )(*os_, *lses)


def attn_bwd(proj, do, o, lse, Bl, S, g, dil, name):
    Sd = S // dil
    nb = Sd // A_BLK
    scale = A_BLK ** -0.5
    pv = proj.reshape(Bl, S, A_PROJ)
    dov, ov, lv = (t.reshape(Bl, S, D) for t in (do, o, lse))

    def body(q_ref, k_ref, v_ref, do_ref, o_ref, l_ref, dq_ref, dk_ref, dv_ref,
             tmp, qs, ks, vs, dos, dls, lts, dq_s, dk_s, dv_s):
        row = lax.broadcasted_iota(jnp.int32, (A_BLK, A_BLK), 0)
        col = lax.broadcasted_iota(jnp.int32, (A_BLK, A_BLK), 1)
        for src, dst in ((q_ref, qs), (k_ref, ks), (v_ref, vs), (do_ref, dos), (l_ref, lts)):
            _to_streams(src.at[0], dst, tmp, dil, Sd)
        tmp[...] = jnp.broadcast_to(_sum1(do_ref[0].astype(F32) * o_ref[0]), (S, 128))
        _to_streams(tmp, dls, None, dil, Sd)
        dk_s[...] = jnp.zeros_like(dk_s)
        dv_s[...] = jnp.zeros_like(dv_s)

        def step(i, carry):
            n = i % nb
            off = pl.multiple_of(i * A_BLK, A_BLK)
            offp = pl.multiple_of(jnp.maximum(i - 1, 0) * A_BLK, A_BLK)
            q = qs[pl.ds(off, A_BLK), :]
            kc, kp = ks[pl.ds(off, A_BLK), :], ks[pl.ds(offp, A_BLK), :]
            vc, vp = vs[pl.ds(off, A_BLK), :], vs[pl.ds(offp, A_BLK), :]
            do_b = dos[pl.ds(off, A_BLK), :]
            delta = dls[pl.ds(off, A_BLK), :][:, 0:1]
            lt = lts[pl.ds(off, A_BLK), :][:, 0:1]
            s_c, s_p = _attn_scores(q, kc, kp, n, row, col, scale)
            p_c = jnp.exp(s_c - lt)
            p_p = jnp.exp(s_p - lt)
            ds_c = (p_c * (_dot_nt(do_b, vc) - delta) * scale).astype(BF16)
            ds_p = (p_p * (_dot_nt(do_b, vp) - delta) * scale).astype(BF16)
            dq_s[pl.ds(off, A_BLK), :] = _dot(ds_c, kc) + _dot(ds_p, kp)
            dk_s[pl.ds(off, A_BLK), :] += _dot_tn(ds_c, q)
            dk_s[pl.ds(offp, A_BLK), :] += _dot_tn(ds_p, q)
            dv_s[pl.ds(off, A_BLK), :] += _dot_tn(p_c.astype(BF16), do_b)
            dv_s[pl.ds(offp, A_BLK), :] += _dot_tn(p_p.astype(BF16), do_b)
            return carry

        lax.fori_loop(0, dil * nb, step, 0, unroll=A_UNROLL)
        for src, dst in ((dq_s, dq_ref), (dk_s, dk_ref), (dv_s, dv_ref)):
            _from_streams(src, tmp, dil, Sd)
            dst[0] = tmp[...].astype(BF16)

    def spec(j):
        return pl.BlockSpec((1, S, 128), lambda b, h: (b, 0, g * 24 + j * HEADS + h))

    ospec = pl.BlockSpec((1, S, 128), lambda b, h: (b, 0, h))
    slab = lambda dt: pltpu.VMEM((S, 128), dt)
    outs = pl.pallas_call(
        body, name=name, grid=(Bl, HEADS),
        in_specs=[spec(0), spec(1), spec(2), ospec, ospec, ospec], out_specs=[ospec] * 3,
        out_shape=[jax.ShapeDtypeStruct((Bl, S, D), BF16)] * 3,
        scratch_shapes=[slab(F32)] + [slab(BF16)] * 4 + [slab(F32)] * 5,
        compiler_params=_cp(2),
    )(pv, pv, pv, dov, ov, lv)
    return [t.reshape(Bl * S, D) for t in outs]


def _as_slots(pair, shape):
    return tuple(t.reshape(shape) for t in pair)


def ffn_fwd(x, mod3, w_in, w_out, lng, lnb, tag, gather=()):
    a, g, u, h, gathered = ffn_in(x, mod3, w_in, tag + "_in", gather=gather)
    out, xn = proj_post(a, w_out, x, mod3, lng, lnb, 0.5, tag + "_out")
    return xn, (x, out, g, u, h, a), gathered


def ffn_bwd(dxn, saved, mod3, w_in, w_out, lng, tag, exchange=((), (), ()), exchange_own=False):
    x, out, g, u, h, a = saved
    dxres, dout, dgu, dlg, dlb, dgate, *got0 = post_bwd(dxn, x, out, mod3, lng, w_out, 0.5, tag + "_outb",
                                                        tm=256, gu=(g, u), exchange=exchange[0])
    *dw_in, got1 = mm_tn(h, dgu, tag + "_dwin", a_copies=True, exchange=exchange[1]) + (() if exchange[1] else ([],))
    *dw_out, got2 = mm_tn(a, dout, tag + "_dwout", bw=D, exchange=exchange[2]) + (() if exchange[2] else ([],))
    dw_out = _as_slots(dw_out, (N_DEV, D_FF // N_DEV, D))
    dx, dsh, dsc, *own = modmm_bwd(dgu, w_in, x, mod3, dxres, tag + "_inb", tm=256,
                                   exchange=[dw_in[1], dw_out[1]] if exchange_own else ())
    dmod3 = jnp.concatenate([dsh, dsc, dgate], axis=1)
    return (dx, [tuple(dw_in), dw_out], dlg, dlb, dmod3,
            [got0[0] if got0 else [], got1, got2], (own[0] if own else []))


def mlstm_fwd(x, mod3, w_in, w_out, conv_w, gain, gbias, lng, lnb, Bl, S, gather=()):
    proj, h = modmm(x, mod3, w_in, F32, "ml_in", tn=M_PROJ_PAD // 5)
    proj3 = proj.reshape(Bl, S, M_PROJ_PAD)
    qk3 = conv_silu(proj3, conv_w, "ml_conv")
    y3, *rest = mlstm_cell_fwd(qk3, proj3, gain, gbias, "ml_cell", gather=gather)
    states, gathered = rest[:3], rest[3:]
    y = y3.reshape(Bl * S, D)
    out, xn = proj_post(y[None], w_out, x, mod3, lng, lnb, 1.0, "ml_out")
    return xn, (x, out, h, proj3, qk3, y, states), gathered


def mlstm_bwd(dxn, saved, mod3, w_in, w_out, conv_w, gain, gbias, lng, Bl, S, exchange=()):
    x, out, h, proj3, qk3, y, states = saved
    dxres, dout, dy, dlg, dlb, dgate = post_bwd(dxn, x, out, mod3, lng, w_out, 1.0, "ml_outb")
    dq, dk, dv, do, dg, dgain, dgb, *received = mlstm_cell_bwd(qk3, proj3, gain, gbias, dy.reshape(Bl, S, D),
                                                               states, "ml_cellb", exchange=exchange)
    dqk, dconv = conv_silu_bwd(proj3, conv_w, dq, dk, "ml_convb")
    dproj = jnp.concatenate([dqk, dv, do, dg.astype(BF16)], axis=2).reshape(Bl * S, M_PROJ_PAD)
    dx, dsh, dsc = modmm_bwd(dproj, w_in, x, mod3, dxres, "ml_inb", tn=M_PROJ_PAD // 5)
    dwi, _ = mm_tn(h, dproj, "ml_dwin", bw=M_PROJ_PAD // 5)
    dwi = _restack(jnp.moveaxis(dwi, 0, 1).reshape(D, M_PROJ_PAD)[:, :M_PROJ], 1)
    dw_out = _as_slots(mm_tn(y, dout, "ml_dwout", bw=D), (N_DEV, D // N_DEV, D))
    small = (jnp.sum(dconv, axis=0), jnp.sum(dgain, axis=0), jnp.sum(dgb, axis=0)[:, :2 * HEADS])
    dmod3 = jnp.concatenate([dsh, dsc, dgate], axis=1)
    return dx, [(dwi, dwi.astype(BF16)), dw_out], dlg, dlb, dmod3, small, received


def attn_mixer_fwd(x, mod3, w_in, w_out, lng, lnb, Bl, S):
    proj, h = modmm(x, mod3, w_in, BF16, "at_in")
    os_, lses = [], []
    for g, (_, dil) in enumerate(DIL_GROUPS):
        o_g, l_g = attn_fwd(proj, Bl, S, g, dil, "at_core%d" % g)
        os_.append(o_g)
        lses.append(l_g)
    ob, of, lt = attn_merge(os_, lses, "at_merge")
    out, xn = proj_post(ob[None], w_out, x, mod3, lng, lnb, 1.0, "at_out")
    return xn, (x, out, h, proj, ob, of, lt)


def attn_mixer_bwd(dxn, saved, mod3, w_in, w_out, lng, Bl, S):
    x, out, h, proj, ob, of, lt = saved
    dxres, dout, do, dlg, dlb, dgate = post_bwd(dxn, x, out, mod3, lng, w_out, 1.0, "at_outb")
    do = do[0]
    parts = []
    for g, (_, dil) in enumerate(DIL_GROUPS):
        parts += attn_bwd(proj, do, of, lt, Bl, S, g, dil, "at_coreb%d" % g)
    dproj = jnp.concatenate(parts, axis=1)
    dx, dsh, dsc = modmm_bwd(dproj, w_in, x, mod3, dxres, "at_inb")
    dw_in = mm_tn(h, dproj, "at_dwin", bw=w_in.shape[2])
    dw_out = _as_slots(mm_tn(ob, dout, "at_dwout", bw=D), (N_DEV, D // N_DEV, D))
    return dx, [dw_in, dw_out], dlg, dlb, jnp.concatenate([dsh, dsc, dgate], axis=1)


def _unstack(stacked, axis):
    full = jnp.moveaxis(stacked, 0, axis)
    shp = list(full.shape)
    shp[axis:axis + 2] = [shp[axis] * shp[axis + 1]]
    return full.reshape(shp)


def _restack(full, axis):
    shp = list(full.shape)
    shp[axis:axis + 1] = [N_DEV, shp[axis] // N_DEV]
    return jnp.moveaxis(full.reshape(shp), axis, 0)


def kernel(x, c, ada_w, ada_b, ln_g, ln_b, ffn_w_in, ffn_w_out, mlstm_w_in, mlstm_gate_bias, mlstm_conv_w, mlstm_head_gain, mlstm_w_out, attn_w_in, attn_w_out, loss_target, m_ada_w, m_ada_b, m_ln_g, m_ln_b, m_ffn_w_in, m_ffn_w_out, m_mlstm_w_in, m_mlstm_gate_bias, m_mlstm_conv_w, m_mlstm_head_gain, m_mlstm_w_out, m_attn_w_in, m_attn_w_out, v_ada_w, v_ada_b, v_ln_g, v_ln_b, v_ffn_w_in, v_ffn_w_out, v_mlstm_w_in, v_mlstm_gate_bias, v_mlstm_conv_w, v_mlstm_head_gain, v_mlstm_w_out, v_attn_w_in, v_attn_w_out):
    Bl, S, _ = x.shape
    T = Bl * S
    Bg = Bl * N_DEV
    me = 4 * lax.axis_index("x") + 2 * lax.axis_index("y") + lax.axis_index("c")
    onehot = (jnp.arange(N_DEV) == me).astype(F32)
    weights = dict(ada_w=ada_w, ada_b=ada_b, ln_g=ln_g, ln_b=ln_b, ffn_w_in=ffn_w_in, ffn_w_out=ffn_w_out,
                   mlstm_w_in=mlstm_w_in, mlstm_gate_bias=mlstm_gate_bias, mlstm_conv_w=mlstm_conv_w,
                   mlstm_head_gain=mlstm_head_gain, mlstm_w_out=mlstm_w_out, attn_w_in=attn_w_in,
                   attn_w_out=attn_w_out)
    m_in = dict(ada_w=m_ada_w, ada_b=m_ada_b, ln_g=m_ln_g, ln_b=m_ln_b, ffn_w_in=m_ffn_w_in,
                ffn_w_out=m_ffn_w_out, mlstm_w_in=m_mlstm_w_in, mlstm_gate_bias=m_mlstm_gate_bias,
                mlstm_conv_w=m_mlstm_conv_w, mlstm_head_gain=m_mlstm_head_gain, mlstm_w_out=m_mlstm_w_out,
                attn_w_in=m_attn_w_in, attn_w_out=m_attn_w_out)
    v_in = dict(ada_w=v_ada_w, ada_b=v_ada_b, ln_g=v_ln_g, ln_b=v_ln_b, ffn_w_in=v_ffn_w_in,
                ffn_w_out=v_ffn_w_out, mlstm_w_in=v_mlstm_w_in, mlstm_gate_bias=v_mlstm_gate_bias,
                mlstm_conv_w=v_mlstm_conv_w, mlstm_head_gain=v_mlstm_head_gain, mlstm_w_out=v_mlstm_w_out,
                attn_w_in=v_attn_w_in, attn_w_out=v_attn_w_out)

    mixer = ("mlstm", "attn")
    shards = [[ffn_w_in[layer, 0], ffn_w_in[layer, 1], ffn_w_out[layer, 0], ffn_w_out[layer, 1],
               weights[mixer[layer] + "_w_in"][0], weights[mixer[layer] + "_w_out"][0]] for layer in range(DEPTH)]
    sends = [[s.astype(BF16) for s in layer_shards] for layer_shards in shards]
    small = jnp.concatenate([c.reshape(-1), ln_g.reshape(-1), ln_b.reshape(-1), mlstm_conv_w.reshape(-1)])
    n_small = small.shape[0]
    small = jnp.pad(small, (0, -n_small % (8 * PACK_COLS))).reshape(-1, PACK_COLS)

    def gathered_weights(g):
        return ((g[0], g[1]), (g[2].reshape(4, D_FF // 4, D), g[3].reshape(4, D_FF // 4, D)), g[4],
                g[5].reshape(1, D, D))

    first_in, small_all = all_gather([sends[0][0], small], "ag_params")
    full = [None, None]
    small_flat = small_all.reshape(N_DEV, -1)
    o0 = 0
    c_all = small_flat[:, o0:o0 + c.size].reshape(Bg, D)
    o0 += c.size
    lng_full = _unstack(small_flat[:, o0:o0 + ln_g.size].reshape((N_DEV,) + ln_g.shape), 2)
    o0 += ln_g.size
    lnb_full = _unstack(small_flat[:, o0:o0 + ln_b.size].reshape((N_DEV,) + ln_b.shape), 2)
    o0 += ln_b.size
    conv_full = _unstack(small_flat[:, o0:o0 + mlstm_conv_w.size].reshape((N_DEV,) + mlstm_conv_w.shape), 2)[0]
    gbias =jnp.pad(mlstm_gate_bias, ((0, 0), (0, 128 - 2 * HEADS)))

    ncols = ada_w.shape[2]
    ada_b_cols = lax.dynamic_slice_in_dim(ada_b, me * ncols, ncols, axis=1).reshape(DEPTH, 1, ncols)
    mod_cols = ada_fwd(c_all, ada_w, ada_b_cols, "ada_fwd")
    (mod_g,) = all_gather([mod_cols.reshape(DEPTH * Bg, ncols)], "ag_mod")
    mod_full = _unstack(mod_g.reshape(N_DEV, DEPTH, Bg, ncols), 2)
    mod_mine = lax.dynamic_slice_in_dim(mod_full, me * Bl, Bl, axis=1).reshape(DEPTH, Bl, 3, 3, D)

    xt = x.reshape(T, D)
    saved = []
    for layer in range(DEPTH):
        def lnp(s, layer=layer):
            return lng_full[layer, s].reshape(1, D), lnb_full[layer, s].reshape(1, D)
        md = mod_mine[layer]
        if layer == 0:
            a, g, u, h, late = ffn_in(xt, md[:, 0], first_in, "f0a_in", gather=sends[0][1:])
            full[0] = gathered_weights([first_in] + late)
            out, xn = proj_post(a, full[0][1][0], xt, md[:, 0], *lnp(0), 0.5, "f0a_out")
            xt, sv0 = xn, (xt, out, g, u, h, a)
            mw_in = jnp.pad(_unstack(full[0][2], 1), ((0, 0), (0, M_PROJ_PAD - M_PROJ)))
        else:
            xt, sv0, _ = ffn_fwd(xt, md[:, 0], full[layer][0][0], full[layer][1][0], *lnp(0), "f%da" % layer)
        f_in, f_out, mix_in, mix_out = full[layer]
        if layer % 2 == 0:
            xt, sv1, g1 = mlstm_fwd(xt, md[:, 1], mw_in, mix_out, conv_full, mlstm_head_gain, gbias, *lnp(1), Bl, S,
                                    gather=sends[1])
            full[1] = gathered_weights(g1)
        else:
            xt, sv1 = attn_mixer_fwd(xt, md[:, 1], mix_in, mix_out, *lnp(1), Bl, S)
        xt, sv2, _ = ffn_fwd(xt, md[:, 2], f_in[1], f_out[1], *lnp(2), "f%db" % layer)
        saved.append((sv0, sv1, sv2))

    dxt, lsum = loss_head(xt, loss_target.reshape(T, D), "loss")
    loss = lax.psum(lsum[0, 0], MESH_AXES)

    dmod, dlg_all, dlb_all = [None] * DEPTH, [None] * DEPTH, [None] * DEPTH
    wgrads = [None] * DEPTH
    recvs = [[None] * 6 for _ in range(DEPTH)]
    ml_small = None
    for layer in reversed(range(DEPTH)):
        md = mod_mine[layer]
        f_in, f_out, mix_in, mix_out = full[layer]
        sv0, sv1, sv2 = saved[layer]
        dxt, dw2, dlg2, dlb2, dm2, _, _ = ffn_bwd(dxt, sv2, md[:, 2], f_in[1], f_out[1],
                                                  lng_full[layer, 2].reshape(1, D), "f%db" % layer)
        lg1 = lng_full[layer, 1].reshape(1, D)
        if layer % 2 == 0:
            dxt, dw1, dlg1, dlb1, dm1, ml_small, got = mlstm_bwd(
                dxt, sv1, md[:, 1], mw_in, mix_out, conv_full, mlstm_head_gain, gbias, lg1, Bl, S,
                exchange=[b16 for _, b16 in wgrads[1]])
            recvs[1] = got
            dxt, dw0, dlg0, dlb0, dm0, got, own = ffn_bwd(
                dxt, sv0, md[:, 0], f_in[0], f_out[0], lng_full[layer, 0].reshape(1, D), "f%da" % layer,
                exchange=([dw2[0][1]], [dw2[1][1], dw1[0][1]], [dw1[1][1]]), exchange_own=True)
            (recvs[0][1],), (recvs[0][3], recvs[0][4]), (recvs[0][5],) = got
            recvs[0][0], recvs[0][2] = own
        else:
            dxt, dw1, dlg1, dlb1, dm1 = attn_mixer_bwd(dxt, sv1, md[:, 1], mix_in, mix_out, lg1, Bl, S)
            dxt, dw0, dlg0, dlb0, dm0, _, _ = ffn_bwd(dxt, sv0, md[:, 0], f_in[0], f_out[0],
                                                      lng_full[layer, 0].reshape(1, D), "f%da" % layer)
        wgrads[layer] = [dw0[0], dw2[0], dw0[1], dw2[1], dw1[0], dw1[1]]
        dmod[layer] = jnp.stack([dm0, dm1, dm2], axis=1).reshape(Bl, 9 * D)
        dlg_all[layer] = jnp.concatenate([dlg0, dlg1, dlg2], axis=0)
        dlb_all[layer] = jnp.concatenate([dlb0, dlb1, dlb2], axis=0)
    grad_x = dxt.reshape(Bl, S, D)

    gsh = [[shard_sum(lax.dynamic_index_in_dim(f32, me, axis=0, keepdims=False), recv, onehot,
                      "rs_sum%d_%d" % (layer, i))
            for i, ((f32, _), recv) in enumerate(zip(wgrads[layer], recvs[layer]))] for layer in range(DEPTH)]
    grads = {"ffn_w_in": jnp.stack([jnp.stack(g[0:2]) for g in gsh]),
             "ffn_w_out": jnp.stack([jnp.stack(g[2:4]) for g in gsh]),
             "mlstm_w_in": gsh[0][4][None], "mlstm_w_out": gsh[0][5][None],
             "attn_w_in": gsh[1][4][None], "attn_w_out": gsh[1][5][None]}

    dconv, dgain, dgbias = ml_small
    parts = [jnp.stack(dmod).reshape(-1), dgbias.reshape(-1), dgain.reshape(-1),
             jnp.stack(dlg_all).reshape(-1), jnp.stack(dlb_all).reshape(-1), dconv.reshape(-1)]
    sizes = [p.shape[0] for p in parts]
    flat = jnp.concatenate(parts)
    flat = jnp.pad(flat, (0, -flat.shape[0] % (8 * PACK_COLS))).reshape(-1, PACK_COLS)
    (sm_all,) = all_gather([flat], "ag_small")
    sm_sum = sum_leading(sm_all, "small_sum").reshape(-1)
    dmod_all = sm_all.reshape(N_DEV, -1)[:, :sizes[0]].reshape(N_DEV, DEPTH, Bl, 9 * D)
    dmod_all = jnp.moveaxis(dmod_all, 0, 1).reshape(DEPTH, Bg, 9 * D)
    o0 = sizes[0]
    grads["mlstm_gate_bias"] = sm_sum[o0:o0 + sizes[1]].reshape(mlstm_gate_bias.shape)
    o0 += sizes[1]
    grads["mlstm_head_gain"] = sm_sum[o0:o0 + sizes[2]].reshape(mlstm_head_gain.shape)
    o0 += sizes[2]
    nl = ln_g.shape[2]
    g_lng = sm_sum[o0:o0 + sizes[3]].reshape(DEPTH, 3, D)
    o0 += sizes[3]
    g_lnb = sm_sum[o0:o0 + sizes[4]].reshape(DEPTH, 3, D)
    o0 += sizes[4]
    g_conv = sm_sum[o0:o0 + sizes[5]].reshape(1, 4, D)
    grads["ln_g"] = lax.dynamic_slice_in_dim(g_lng, me * nl, nl, axis=2)
    grads["ln_b"] = lax.dynamic_slice_in_dim(g_lnb, me * nl, nl, axis=2)
    grads["mlstm_conv_w"] = lax.dynamic_slice_in_dim(g_conv, me * nl, nl, axis=2)
    dmod_cols = lax.dynamic_slice_in_dim(dmod_all, me * ncols, ncols, axis=2)
    gw, gb = ada_bwd(c_all.T, dmod_cols, dmod_all, "ada_bwd")
    grads["ada_w"] = gw
    grads["ada_b"] = gb.reshape(ada_b.shape)

    names = ["ada_w", "ada_b", "ln_g", "ln_b", "ffn_w_in", "ffn_w_out", "mlstm_w_in", "mlstm_gate_bias",
             "mlstm_conv_w", "mlstm_head_gain", "mlstm_w_out", "attn_w_in", "attn_w_out"]
    deltas, new_m, new_v = [], [], []
    for k in names:
        w = weights[k]
        shp2 = (math.prod(w.shape[:-1]), w.shape[-1])
        d_, m_, v_ = adamw(w.reshape(shp2), grads[k].reshape(shp2), m_in[k].reshape(shp2), v_in[k].reshape(shp2),
                           "adamw_" + k)
        deltas.append(d_.reshape(w.shape))
        new_m.append(m_.reshape(w.shape))
        new_v.append(v_.reshape(w.shape))
    return (loss, grad_x, *[grads[k] for k in names], *deltas, *new_m, *new_v)
```

```python
import functools
import math

import jax
import jax.numpy as jnp
from jax import lax
from jax.experimental import pallas as pl
from jax.experimental.pallas import tpu as pltpu

F32 = jnp.float32
BF16 = jnp.bfloat16

N_DEV = 8
MESH_AXES = ("x", "y", "c")
D = 1024
DEPTH = 2
D_FF = 2816
HEADS = 8
M_DQK = 64
M_DV = 128
M_CHUNK = 64
M_SLAB = 512
M_PROJ = 3088
M_PROJ_PAD = 3200
A_PROJ = 9216
DIL_GROUPS = ((128, 1), (512, 4), (2048, 16))
A_BLK = 128
A_UNROLL = 8
ALPHA = (2 * DEPTH) ** 0.25
LN_EPS = 1e-5
RMS_EPS = 1e-6
ADAM_LR = 0.001
ADAM_B1 = 0.9
ADAM_B2 = 0.999
ADAM_EPS = 1e-08
ADAM_WD = 0.01
ADAM_STEP = 10
NEG = -1e30
V7X_VMEM_LIMIT = 56 * 1024 * 1024
PACK_COLS = 1024
MESH_ID = pl.DeviceIdType.MESH
ANY_SPEC = pl.BlockSpec(memory_space=pl.ANY)


def _cp(n_axes):
    return pltpu.CompilerParams(dimension_semantics=("arbitrary",) * n_axes,
                                vmem_limit_bytes=V7X_VMEM_LIMIT)


def _dot(a, b):
    return jnp.dot(a, b, preferred_element_type=F32)


def _dot_nt(a, b):
    return lax.dot_general(a, b, (((1,), (1,)), ((), ())), preferred_element_type=F32)


def _dot_tn(a, b):
    return lax.dot_general(a, b, (((0,), (0,)), ((), ())), preferred_element_type=F32)


def _sum0(a):
    return jnp.sum(a, axis=0, keepdims=True)


def _sum1(a):
    return jnp.sum(a, axis=1, keepdims=True)


def _round(a):
    return a.astype(BF16).astype(F32)


def _sigmoid(a):
    return 1.0 / (1.0 + jnp.exp(-a))


def _tile(n, pref):
    t = min(n, pref)
    while n % t:
        t //= 2
    return t


def all_gather(arrs, name):
    n = len(arrs)

    def body(*refs):
        gather = Gather(refs[:n], refs[n:2 * n], *refs[2 * n:])
        gather.start()
        gather.finish()

    return pl.pallas_call(
        body, name=name, out_shape=Gather.out_shape(arrs),
        in_specs=[ANY_SPEC] * n, out_specs=[ANY_SPEC] * n, scratch_shapes=Gather.scratch(n),
    )(*arrs)


class Gather:
    def __init__(self, ins, outs, send_sems, recv_sems, local_sems):
        x, y, c = lax.axis_index("x"), lax.axis_index("y"), lax.axis_index("c")
        me, sibling = (x, y, c), (x, y, 1 - c)
        chips = [(1 - x, y), (x, 1 - y), (1 - x, 1 - y)]

        def slot(a, p):
            return outs[a].at[4 * p[0] + 2 * p[1] + p[2]]

        def copy(a, k, block, to, src=None):
            return pltpu.make_async_remote_copy(
                src_ref=slot(a, block) if src is None else src, dst_ref=slot(a, block),
                send_sem=send_sems.at[7 * a + k], recv_sem=recv_sems.at[7 * a + k],
                device_id=to, device_id_type=MESH_ID)

        n = len(ins)
        self.mine = [pltpu.make_async_copy(ins[a], slot(a, me), local_sems.at[a]) for a in range(n)]
        self.first, self.over_ici, self.passed, self.from_sibling = [], [], [], []
        for a in range(n):
            self.first.append(copy(a, 0, me, sibling, src=ins[a]))
            self.from_sibling.append(copy(a, 0, sibling, me))
            for j, chip in enumerate(chips):
                self.first.append(copy(a, 1 + j, me, (*chip, c), src=ins[a]))
                self.over_ici.append(copy(a, 1 + j, (*chip, c), me))
                self.passed.append(copy(a, 4 + j, (*chip, c), sibling))
                self.from_sibling.append(copy(a, 4 + j, (*chip, 1 - c), me))

    @staticmethod
    def out_shape(arrs):
        return [jax.ShapeDtypeStruct((N_DEV,) + a.shape, a.dtype) for a in arrs]

    @staticmethod
    def scratch(n):
        return [pltpu.SemaphoreType.DMA((7 * n,)), pltpu.SemaphoreType.DMA((7 * n,)),
                pltpu.SemaphoreType.DMA((n,))]

    def start(self):
        for cp in self.mine + self.first:
            cp.start()

    def finish(self):
        for landed, onward in zip(self.over_ici, self.passed):
            landed.wait_recv()
            onward.start()
        for cp in self.from_sibling:
            cp.wait_recv()
        for cp in self.first + self.passed:
            cp.wait_send()
        for cp in self.mine:
            cp.wait()


class Exchange:
    def __init__(self, sends, recvs, send_sems, recv_sems, local_sems):
        x, y, c = lax.axis_index("x"), lax.axis_index("y"), lax.axis_index("c")
        me = 4 * x + 2 * y + c
        self.own = [pltpu.make_async_copy(s.at[me], r.at[me], local_sems.at[a])
                    for a, (s, r) in enumerate(zip(sends, recvs))]
        self.copies = []
        for a, (s_ref, r_ref) in enumerate(zip(sends, recvs)):
            for k in range(1, N_DEV):
                px = 1 - x if (k >> 2) & 1 else x
                py = 1 - y if (k >> 1) & 1 else y
                pc = 1 - c if k & 1 else c
                self.copies.append(pltpu.make_async_remote_copy(
                    src_ref=s_ref.at[4 * px + 2 * py + pc], dst_ref=r_ref.at[me],
                    send_sem=send_sems.at[7 * a + k - 1], recv_sem=recv_sems.at[7 * a + k - 1],
                    device_id=(px, py, pc), device_id_type=MESH_ID))

    @staticmethod
    def scratch(n):
        return [pltpu.SemaphoreType.DMA((7 * n,)), pltpu.SemaphoreType.DMA((7 * n,)),
                pltpu.SemaphoreType.DMA((n,))]

    def start(self):
        for cp in self.own + self.copies:
            cp.start()

    def finish(self):
        for cp in self.copies:
            cp.wait_send()
            cp.wait_recv()
        for cp in self.own:
            cp.wait()


def host_comm(body, grid, n_in, n_out, gather=(), exchange=()):
    ng, nx = len(gather), len(exchange)
    if ng + nx == 0:
        return body, [], [], [], []

    def hosted(*refs):
        ins, c_in, rest = refs[:n_in], refs[n_in:n_in + ng + nx], refs[n_in + ng + nx:]
        outs, c_out, rest = rest[:n_out], rest[n_out:n_out + ng + nx], rest[n_out + ng + nx:]
        n_sems = 3 * ((ng > 0) + (nx > 0))
        scratch, sems = rest[:len(rest) - n_sems], rest[len(rest) - n_sems:]

        def comms():
            made = [Gather(c_in[:ng], c_out[:ng], *sems[:3])] if ng else []
            return made + ([Exchange(c_in[ng:], c_out[ng:], *sems[-3:])] if nx else [])

        ids = [pl.program_id(a) for a in range(len(grid))]

        @pl.when(functools.reduce(jnp.logical_and, [i == 0 for i in ids]))
        def _():
            for cm in comms():
                cm.start()
        body(*ins, *outs, *scratch)

        @pl.when(functools.reduce(jnp.logical_and, [i == g - 1 for i, g in zip(ids, grid)]))
        def _():
            for cm in comms():
                cm.finish()

    shapes = Gather.out_shape(gather) + [jax.ShapeDtypeStruct(a.shape, a.dtype) for a in exchange]
    scratch = (Gather.scratch(ng) if ng else []) + (Exchange.scratch(nx) if nx else [])
    return hosted, [ANY_SPEC] * (ng + nx), [ANY_SPEC] * (ng + nx), shapes, scratch


def shard_sum(own, recv, onehot, name):
    R, C = own.shape
    tr = _tile(R, 512)

    def body(oh_ref, own_ref, recv_ref, o_ref):
        acc = None
        for j in range(N_DEV):
            term = jnp.where(oh_ref[j] > 0.5, own_ref[...], recv_ref[j].astype(F32))
            acc = term if acc is None else acc + term
        o_ref[...] = acc

    return pl.pallas_call(
        body, name=name, grid=(R // tr,),
        in_specs=[pl.BlockSpec(memory_space=pltpu.SMEM),
                  pl.BlockSpec((tr, C), lambda i: (i, 0)),
                  pl.BlockSpec((N_DEV, tr, C), lambda i: (0, i, 0))],
        out_specs=pl.BlockSpec((tr, C), lambda i: (i, 0)),
        out_shape=jax.ShapeDtypeStruct((R, C), F32), compiler_params=_cp(1),
    )(onehot, own, recv)


def sum_leading(a, name):
    _, R, C = a.shape

    def body(a_ref, o_ref):
        acc = a_ref[0]
        for j in range(1, N_DEV):
            acc = acc + a_ref[j]
        o_ref[...] = acc

    return pl.pallas_call(body, name=name, out_shape=jax.ShapeDtypeStruct((R, C), F32),
                          compiler_params=_cp(0))(a)


def _col_chunks(w, tn):
    if w.ndim == 3:
        return w.shape[0], w.shape[2], pl.BlockSpec((None, w.shape[1], w.shape[2]), lambda i, j: (j, 0, 0))
    return w.shape[1] // tn, tn, pl.BlockSpec((w.shape[0], tn), lambda i, j: (0, j))


def modmm(x, mod3, w, out_dtype, name, tn=None):
    T, Dm = x.shape
    nj, tn, w_spec = _col_chunks(w, tn)
    N = nj * tn
    Bl = mod3.shape[0]
    tm = _tile(T // Bl, 1024)
    tpb = T // Bl // tm

    def body(x_ref, mod_ref, w_ref, o_ref, h_ref, hs):
        @pl.when(pl.program_id(1) == 0)
        def _():
            m = mod_ref[0]
            hs[...] = (x_ref[...] * (1.0 + m[1:2, :]) + m[0:1, :]).astype(BF16)
            h_ref[...] = hs[...]
        o_ref[...] = _dot(hs[...], w_ref[...]).astype(o_ref.dtype)

    return pl.pallas_call(
        body, name=name, grid=(T // tm, nj),
        in_specs=[pl.BlockSpec((tm, Dm), lambda i, j: (i, 0)),
                  pl.BlockSpec((1, 3, Dm), lambda i, j: (i // tpb, 0, 0)), w_spec],
        out_specs=[pl.BlockSpec((tm, tn), lambda i, j: (i, j)),
                   pl.BlockSpec((tm, Dm), lambda i, j: (i, 0))],
        out_shape=[jax.ShapeDtypeStruct((T, N), out_dtype), jax.ShapeDtypeStruct((T, Dm), BF16)],
        scratch_shapes=[pltpu.VMEM((tm, Dm), BF16)], compiler_params=_cp(2),
    )(x, mod3, w)


def modmm_bwd(dp, w, x, mod3, dxres, name, tn=None, tm=1024, exchange=()):
    T, Dm = x.shape
    Bl = mod3.shape[0]
    tm = _tile(T // Bl, tm)
    tpb = T // Bl // tm
    resident = dp.ndim == 3
    if resident:
        nc, nj = dp.shape[0], 1
        dp_spec = pl.BlockSpec((nc, tm, dp.shape[2]), lambda i, j: (0, i, 0))
        w_spec = pl.BlockSpec(w.shape, lambda i, j: (0, 0, 0))
    else:
        nj, tn, w_spec = _col_chunks(w, tn)
        dp_spec = pl.BlockSpec((tm, tn), lambda i, j: (i, j))

    def body(dp_ref, w_ref, x_ref, mod_ref, dxr_ref, dx_ref, dsh_ref, dsc_ref, acc):
        i, j = pl.program_id(0), pl.program_id(1)

        @pl.when(j == 0)
        def _():
            acc[...] = jnp.zeros_like(acc)
        if resident:
            for c in range(nc):
                acc[...] += _dot_nt(dp_ref[c], w_ref[c])
        else:
            acc[...] += _dot_nt(dp_ref[...], w_ref[...])

        @pl.when(j == nj - 1)
        def _():
            dh = acc[...]
            xx = x_ref[...]
            dx_ref[...] = dxr_ref[...] + dh * (1.0 + mod_ref[0][1:2, :])

            @pl.when(i % tpb == 0)
            def _():
                dsh_ref[...] = jnp.zeros_like(dsh_ref)
                dsc_ref[...] = jnp.zeros_like(dsc_ref)
            dsh_ref[0] += _sum0(dh)
            dsc_ref[0] += _sum0(dh * xx)

    grid = (T // tm, nj)
    body, c_in, c_out, c_shape, c_scratch = host_comm(body, grid, 5, 3, exchange=exchange)
    dx, dsh, dsc, *received = pl.pallas_call(
        body, name=name, grid=grid,
        in_specs=[dp_spec, w_spec,
                  pl.BlockSpec((tm, Dm), lambda i, j: (i, 0)),
                  pl.BlockSpec((1, 3, Dm), lambda i, j: (i // tpb, 0, 0)),
                  pl.BlockSpec((tm, Dm), lambda i, j: (i, 0))] + c_in,
        out_specs=[pl.BlockSpec((tm, Dm), lambda i, j: (i, 0)),
                   pl.BlockSpec((1, 1, Dm), lambda i, j: (i // tpb, 0, 0)),
                   pl.BlockSpec((1, 1, Dm), lambda i, j: (i // tpb, 0, 0))] + c_out,
        out_shape=[jax.ShapeDtypeStruct((T, Dm), F32), jax.ShapeDtypeStruct((Bl, 1, Dm), F32),
                   jax.ShapeDtypeStruct((Bl, 1, Dm), F32)] + c_shape,
        scratch_shapes=[pltpu.VMEM((tm, Dm), F32)] + c_scratch, compiler_params=_cp(2),
    )(dp, w, x, mod3, dxres, *exchange)
    return (dx, dsh, dsc, received) if exchange else (dx, dsh, dsc)


def _ln_stats(z):
    mu = jnp.mean(z, axis=-1, keepdims=True)
    zc = z - mu
    var = jnp.mean(zc * zc, axis=-1, keepdims=True)
    rstd = lax.rsqrt(var + LN_EPS)
    return zc * rstd, rstd


def proj_post(a, w, x, mod3, lng, lnb, weight, name):
    nk, T, tk = a.shape
    Dm = w.shape[2]
    Bl = mod3.shape[0]
    tm = _tile(T // Bl, 512)
    tpb = T // Bl // tm

    def body(a_ref, w_ref, x_ref, mod_ref, g_ref, b_ref, out_ref, xn_ref):
        out = _dot(a_ref[0], w_ref[0])
        for k in range(1, nk):
            out = out + _dot(a_ref[k], w_ref[k])
        out_ref[...] = out
        z = ALPHA * x_ref[...] + (weight * (1.0 + mod_ref[0][2:3, :])) * out
        xhat, _ = _ln_stats(z)
        xn_ref[...] = xhat * g_ref[...] + b_ref[...]

    row = pl.BlockSpec((tm, Dm), lambda i: (i, 0))
    vec = pl.BlockSpec((1, Dm), lambda i: (0, 0))
    return pl.pallas_call(
        body, name=name, grid=(T // tm,),
        in_specs=[pl.BlockSpec((nk, tm, tk), lambda i: (0, i, 0)),
                  pl.BlockSpec((nk, tk, Dm), lambda i: (0, 0, 0)),
                  row, pl.BlockSpec((1, 3, Dm), lambda i: (i // tpb, 0, 0)), vec, vec],
        out_specs=[row, row],
        out_shape=[jax.ShapeDtypeStruct((T, Dm), F32), jax.ShapeDtypeStruct((T, Dm), F32)],
        compiler_params=_cp(1),
    )(a, w, x, mod3, lng, lnb)


def post_bwd(dxn, x, out, mod3, lng, w, weight, name, tm=512, gu=None, exchange=()):
    T, Dm = x.shape
    nk, tk, _ = w.shape
    Bl = mod3.shape[0]
    tm = _tile(T // Bl, tm)
    tpb = T // Bl // tm
    fused = gu is not None

    def body(dxn_ref, x_ref, out_ref, mod_ref, g_ref, w_ref, *rest):
        if fused:
            gg_ref, uu_ref = rest[:2]
            rest = rest[2:]
        dxr_ref, dout_ref, da_ref, dg_ref, db_ref, dgate_ref = rest
        i = pl.program_id(0)
        out = out_ref[...]
        dxn = dxn_ref[...]
        coef = weight * (1.0 + mod_ref[0][2:3, :])
        xhat, rstd = _ln_stats(ALPHA * x_ref[...] + coef * out)
        dyh = dxn * g_ref[...]
        dz = rstd * (dyh - jnp.mean(dyh, axis=-1, keepdims=True)
                     - xhat * jnp.mean(dyh * xhat, axis=-1, keepdims=True))
        dxr_ref[...] = ALPHA * dz
        dout = (coef * dz).astype(BF16)
        dout_ref[...] = dout

        @pl.when(i == 0)
        def _():
            dg_ref[...] = jnp.zeros_like(dg_ref)
            db_ref[...] = jnp.zeros_like(db_ref)

        @pl.when(i % tpb == 0)
        def _():
            dgate_ref[...] = jnp.zeros_like(dgate_ref)
        dg_ref[...] += _sum0(dxn * xhat)
        db_ref[...] += _sum0(dxn)
        dgate_ref[0] += _sum0((weight * out) * dz)
        for k in range(nk):
            da = _dot_nt(dout, w_ref[k])
            if fused:
                gg = gg_ref[k].astype(F32)
                s = _sigmoid(gg)
                da_ref[k] = (da * uu_ref[k].astype(F32) * (s * (1.0 + gg * (1.0 - s)))).astype(BF16)
                da_ref[nk + k] = (da * (gg * s)).astype(BF16)
            else:
                da_ref[k] = da.astype(BF16)

    row = pl.BlockSpec((tm, Dm), lambda i: (i, 0))
    vec = pl.BlockSpec((1, Dm), lambda i: (0, 0))
    wide = pl.BlockSpec((nk, tm, tk), lambda i: (0, i, 0))
    nda = 2 * nk if fused else nk
    grid = (T // tm,)
    body, c_in, c_out, c_shape, c_scratch = host_comm(body, grid, 8 if fused else 6, 6, exchange=exchange)
    *results, = pl.pallas_call(
        body, name=name, grid=grid,
        in_specs=[row, row, row, pl.BlockSpec((1, 3, Dm), lambda i: (i // tpb, 0, 0)), vec,
                  pl.BlockSpec((nk, tk, Dm), lambda i: (0, 0, 0))] + ([wide, wide] if fused else []) + c_in,
        out_specs=[row, row, pl.BlockSpec((nda, tm, tk), lambda i: (0, i, 0)),
                   vec, vec, pl.BlockSpec((1, 1, Dm), lambda i: (i // tpb, 0, 0))] + c_out,
        out_shape=[jax.ShapeDtypeStruct((T, Dm), F32), jax.ShapeDtypeStruct((T, Dm), BF16),
                   jax.ShapeDtypeStruct((nda, T, tk), BF16), jax.ShapeDtypeStruct((1, Dm), F32),
                   jax.ShapeDtypeStruct((1, Dm), F32), jax.ShapeDtypeStruct((Bl, 1, Dm), F32)] + c_shape,
        scratch_shapes=c_scratch, compiler_params=_cp(1),
    )(dxn, x, out, mod3, lng, w, *(gu if fused else ()), *exchange)
    return tuple(results[:6]) + ((results[6:],) if exchange else ())


def mm_tn(a, b, name, bw=None, a_copies=False, exchange=()):
    a3, b3 = a.ndim == 3, b.ndim == 3
    nk, T, tk = a.shape if a3 else (1,) + a.shape
    if a_copies:
        nk = 1
    nc, wn = (b.shape[0], b.shape[2]) if b3 else (b.shape[1] // bw, bw)
    tt = _tile(T, 2048)
    nt = T // tt

    def body(a_ref, b_ref, o_ref, ob_ref):
        t = pl.program_id(2)

        @pl.when(t == 0)
        def _():
            o_ref[...] = jnp.zeros_like(o_ref)
        o_ref[...] += _dot_tn(a_ref[...], b_ref[...])

        @pl.when(t == nt - 1)
        def _():
            ob_ref[...] = o_ref[...].astype(BF16)

    a_spec = (pl.BlockSpec((None, tt, tk), lambda k, c, t: (k, t, 0)) if a3
              else pl.BlockSpec((tt, tk), lambda k, c, t: (t, 0)))
    b_spec = (pl.BlockSpec((None, tt, wn), lambda k, c, t: (c, t, 0)) if b3
              else pl.BlockSpec((tt, wn), lambda k, c, t: (t, c)))
    o_spec = pl.BlockSpec((None, tk, wn), lambda k, c, t: (k * nc + c, 0, 0))
    grid = (nk, nc, nt)
    body, c_in, c_out, c_shape, c_scratch = host_comm(body, grid, 2, 2, exchange=exchange)
    o32, o16, *received = pl.pallas_call(
        body, name=name, grid=grid, in_specs=[a_spec, b_spec] + c_in, out_specs=[o_spec, o_spec] + c_out,
        out_shape=[jax.ShapeDtypeStruct((nk * nc, tk, wn), F32), jax.ShapeDtypeStruct((nk * nc, tk, wn), BF16)]
        + c_shape,
        scratch_shapes=c_scratch, compiler_params=_cp(3),
    )(a, b, *exchange)
    return (o32, o16, received) if exchange else (o32, o16)


def ffn_in(x, mod3, w, name, gather=()):
    T, Dm = x.shape
    nj, tf = w.shape[0] // 2, w.shape[2]
    Bl = mod3.shape[0]
    tm = _tile(T // Bl, 1024)
    tpb = T // Bl // tm

    def body(x_ref, mod_ref, wg_ref, wu_ref, a_ref, g_ref, u_ref, h_ref):
        m = mod_ref[0]
        h = (x_ref[...] * (1.0 + m[1:2, :]) + m[0:1, :]).astype(BF16)
        h_ref[...] = h
        g = _dot(h, wg_ref[...])
        u = _dot(h, wu_ref[...])
        a_ref[...] = (g * _sigmoid(g) * u).astype(BF16)
        g_ref[...] = g.astype(BF16)
        u_ref[...] = u.astype(BF16)

    col = pl.BlockSpec((None, tm, tf), lambda j, i: (j, i, 0))
    grid = (nj, T // tm)
    body, c_in, c_out, c_shape, c_scratch = host_comm(body, grid, 4, 4, gather=gather)
    a, g, u, h, *gathered = pl.pallas_call(
        body, name=name, grid=grid,
        in_specs=[pl.BlockSpec((tm, Dm), lambda j, i: (i, 0)),
                  pl.BlockSpec((1, 3, Dm), lambda j, i: (i // tpb, 0, 0)),
                  pl.BlockSpec((None, Dm, tf), lambda j, i: (j, 0, 0)),
                  pl.BlockSpec((None, Dm, tf), lambda j, i: (nj + j, 0, 0))] + c_in,
        out_specs=[col, col, col, pl.BlockSpec((None, tm, Dm), lambda j, i: (j, i, 0))] + c_out,
        out_shape=[jax.ShapeDtypeStruct((nj, T, tf), BF16)] * 3 + [jax.ShapeDtypeStruct((nj, T, Dm), BF16)]
        + c_shape,
        scratch_shapes=c_scratch, compiler_params=_cp(2),
    )(x, mod3, w, w, *gather)
    return a, g, u, h, gathered


def loss_head(y, tgt, name):
    T, Dm = y.shape
    tm = _tile(T, 512)
    nt = T // tm

    def body(y_ref, t_ref, dy_ref, l_ref, acc):
        i = pl.program_id(0)

        @pl.when(i == 0)
        def _():
            acc[...] = jnp.zeros_like(acc)
        e = y_ref[...] - t_ref[...]
        dy_ref[...] = e * (1.0 / Dm)
        acc[...] += _sum0(e * e)

        @pl.when(i == nt - 1)
        def _():
            l_ref[...] = jnp.broadcast_to(_sum1(acc[...]) * (0.5 / Dm), l_ref.shape)

    return pl.pallas_call(
        body, name=name, grid=(nt,),
        in_specs=[pl.BlockSpec((tm, Dm), lambda i: (i, 0)), pl.BlockSpec((tm, Dm), lambda i: (i, 0))],
        out_specs=[pl.BlockSpec((tm, Dm), lambda i: (i, 0)), pl.BlockSpec((1, 128), lambda i: (0, 0))],
        out_shape=[jax.ShapeDtypeStruct((T, Dm), F32), jax.ShapeDtypeStruct((1, 128), F32)],
        scratch_shapes=[pltpu.VMEM((1, Dm), F32)], compiler_params=_cp(1),
    )(y, tgt)


def adamw(w, g, m, v, name):
    R, C = w.shape
    tr = _tile(R, 512) if R % 8 == 0 else R

    def body(w_ref, g_ref, m_ref, v_ref, d_ref, nm_ref, nv_ref):
        gg = g_ref[...]
        mm = ADAM_B1 * m_ref[...] + (1.0 - ADAM_B1) * gg
        vv = ADAM_B2 * v_ref[...] + (1.0 - ADAM_B2) * (gg * gg)
        m_hat = mm / (1.0 - ADAM_B1 ** ADAM_STEP)
        v_hat = vv / (1.0 - ADAM_B2 ** ADAM_STEP)
        d_ref[...] = -ADAM_LR * (m_hat / (jnp.sqrt(v_hat) + ADAM_EPS) + ADAM_WD * w_ref[...])
        nm_ref[...] = mm
        nv_ref[...] = vv

    spec = pl.BlockSpec((tr, C), lambda i: (i, 0))
    return pl.pallas_call(
        body, name=name, grid=(R // tr,), in_specs=[spec] * 4, out_specs=[spec] * 3,
        out_shape=[jax.ShapeDtypeStruct((R, C), F32)] * 3, compiler_params=_cp(1),
    )(w, g, m, v)


def ada_fwd(c_all, ada_w, ada_b_cols, name):
    Lr, Dm, Nc = ada_w.shape
    Bg = c_all.shape[0]

    def body(c_ref, w_ref, b_ref, o_ref):
        cc = c_ref[...]
        cond = cc * _sigmoid(cc)
        o_ref[0] = _dot(cond.astype(BF16), w_ref[0].astype(BF16)) + b_ref[0]

    return pl.pallas_call(
        body, name=name, grid=(Lr,),
        in_specs=[pl.BlockSpec((Bg, Dm), lambda l: (0, 0)),
                  pl.BlockSpec((1, Dm, Nc), lambda l: (l, 0, 0)),
                  pl.BlockSpec((1, 1, Nc), lambda l: (l, 0, 0))],
        out_specs=pl.BlockSpec((1, Bg, Nc), lambda l: (l, 0, 0)),
        out_shape=jax.ShapeDtypeStruct((Lr, Bg, Nc), F32), compiler_params=_cp(1),
    )(c_all, ada_w, ada_b_cols)


def ada_bwd(c_all_t, dmod_cols, dmod_all, name):
    Dm, Bg = c_all_t.shape
    Lr, _, Nc = dmod_cols.shape
    Nf = dmod_all.shape[2]

    def body(c_ref, dm_ref, da_ref, gw_ref, gb_ref):
        cc = c_ref[...]
        cond = cc * _sigmoid(cc)
        gw_ref[0] = _dot(cond.astype(BF16), dm_ref[0].astype(BF16))
        gb_ref[0] = _sum0(da_ref[0])

    return pl.pallas_call(
        body, name=name, grid=(Lr,),
        in_specs=[pl.BlockSpec((Dm, Bg), lambda l: (0, 0)),
                  pl.BlockSpec((1, Bg, Nc), lambda l: (l, 0, 0)),
                  pl.BlockSpec((1, Bg, Nf), lambda l: (l, 0, 0))],
        out_specs=[pl.BlockSpec((1, Dm, Nc), lambda l: (l, 0, 0)),
                   pl.BlockSpec((1, 1, Nf), lambda l: (l, 0, 0))],
        out_shape=[jax.ShapeDtypeStruct((Lr, Dm, Nc), F32), jax.ShapeDtypeStruct((Lr, 1, Nf), F32)],
        compiler_params=_cp(1),
    )(c_all_t, dmod_cols, dmod_all)


def _conv_taps(x, w, rows):
    shifted = [x]
    c = w[3:4, :] * x
    for k in range(1, 4):
        xs = jnp.where(rows >= k, pltpu.roll(x, k, 0), 0.0)
        shifted.append(xs)
        c = c + w[3 - k:4 - k, :] * xs
    return c, shifted


def conv_silu(proj3, conv_w, name):
    Bl, S, _ = proj3.shape
    ncb = conv_w.shape[1] // 128

    def body(x_ref, w_ref, o_ref):
        rows = lax.broadcasted_iota(jnp.int32, (S, 128), 0)
        c, _ = _conv_taps(_round(x_ref[0]), _round(w_ref[...]), rows)
        o_ref[0] = c * _sigmoid(c)

    return pl.pallas_call(
        body, name=name, grid=(Bl, ncb),
        in_specs=[pl.BlockSpec((1, S, 128), lambda b, j: (b, 0, j)),
                  pl.BlockSpec((4, 128), lambda b, j: (0, j))],
        out_specs=pl.BlockSpec((1, S, 128), lambda b, j: (b, 0, j)),
        out_shape=jax.ShapeDtypeStruct((Bl, S, conv_w.shape[1]), F32), compiler_params=_cp(2),
    )(proj3, conv_w)


def conv_silu_bwd(proj3, conv_w, dq, dk, name):
    Bl, S, _ = proj3.shape
    nq = dq.shape[2] // 128

    def body(x_ref, w_ref, dq_ref, dk_ref, dx_ref, dw_ref):
        j = pl.program_id(1)
        rows = lax.broadcasted_iota(jnp.int32, (S, 128), 0)
        w = _round(w_ref[...])
        c, shifted = _conv_taps(_round(x_ref[0]), w, rows)
        s = _sigmoid(c)
        dact = jnp.where(j < nq, dq_ref[0], dk_ref[0])
        dc = _round(dact * (s * (1.0 + c * (1.0 - s))))
        dx = w[3:4, :] * dc
        dws = [_sum0(dc * shifted[0])]
        for k in range(1, 4):
            up = jnp.where(rows < S - k, pltpu.roll(dc, S - k, 0), 0.0)
            dx = dx + w[3 - k:4 - k, :] * up
            dws.append(_sum0(dc * shifted[k]))
        dx_ref[0] = dx.astype(BF16)
        tap = lax.broadcasted_iota(jnp.int32, (4, 128), 0)
        dw_ref[0] = functools.reduce(lambda a, b: a + b, [jnp.where(tap == 3 - k, dws[k], 0.0) for k in range(4)])

    return pl.pallas_call(
        body, name=name, grid=(Bl, 2 * nq),
        in_specs=[pl.BlockSpec((1, S, 128), lambda b, j: (b, 0, j)),
                  pl.BlockSpec((4, 128), lambda b, j: (0, j)),
                  pl.BlockSpec((1, S, 128), lambda b, j: (b, 0, jnp.minimum(j, nq - 1))),
                  pl.BlockSpec((1, S, 128), lambda b, j: (b, 0, jnp.maximum(j - nq, 0)))],
        out_specs=[pl.BlockSpec((1, S, 128), lambda b, j: (b, 0, j)),
                   pl.BlockSpec((1, 4, 128), lambda b, j: (b, 0, j))],
        out_shape=[jax.ShapeDtypeStruct((Bl, S, 2 * nq * 128), BF16),
                   jax.ShapeDtypeStruct((Bl, 4, 2 * nq * 128), F32)],
        compiler_params=_cp(2),
    )(proj3, conv_w, dq, dk)


def _log_sigmoid(a):
    return jnp.minimum(a, 0.0) - jnp.log(1.0 + jnp.exp(-jnp.abs(a)))


def _interleave(gens):
    live = list(gens)
    while live:
        still = []
        for g in live:
            try:
                next(g)
                still.append(g)
            except StopIteration:
                pass
        live = still


def _finish(gen):
    while True:
        try:
            next(gen)
        except StopIteration as done:
            return done.value


def _chunk_state(kc, vc, gi, bcum, b_last, C, n, m):
    a = b_last - bcum + gi
    m_loc = jnp.max(a, axis=0, keepdims=True)
    wa = jnp.exp(a - m_loc)
    c_loc = _dot_tn((wa * vc).astype(BF16), kc.astype(BF16))
    n_loc = _sum0(_round(wa) * _round(kc))
    m_new = jnp.maximum(b_last + m, m_loc)
    sp = jnp.exp(b_last + m - m_new)
    sl = jnp.exp(m_loc - m_new)
    yield
    return sp * C + sl * c_loc, sp * n + sl * n_loc, m_new, wa, sp, sl


def _chunk_out(qs, kc, vc, gi_row, bcum, bcum_row, low, C, n, m):
    inter_log = bcum + m
    dlog = jnp.where(low, bcum - bcum_row + gi_row, NEG)
    m_i = jnp.maximum(inter_log, jnp.max(dlog, axis=1, keepdims=True))
    dm = jnp.exp(dlog - m_i)
    iw = jnp.exp(inter_log - m_i)
    qs_b, k_b, v_b = qs.astype(BF16), kc.astype(BF16), vc.astype(BF16)
    sqk = _dot_nt(qs_b, k_b)
    qc_ = _dot_nt(qs_b, C.astype(BF16))
    qn = _sum1(_round(qs) * _round(n))
    floor = jnp.exp(-m_i)
    yield
    sc = sqk * dm
    sv = _dot(sc.astype(BF16), v_b)
    den = _sum1(sc) + iw * qn
    dn = jnp.maximum(jnp.abs(den), floor)
    yield
    num = sv + iw * qc_
    return dict(hc=num / dn, den=den, dn=dn, floor=floor, sc=sc, dm=dm, iw=iw, qc=qc_, qn=qn,
                qs_b=qs_b, k_b=k_b, v_b=v_b)


def _cell_consts(L):
    ri = lax.broadcasted_iota(jnp.int32, (L, L), 0)
    ci = lax.broadcasted_iota(jnp.int32, (L, L), 1)
    return ri == ci, ci <= ri, ri <= ci


def _load_chunk(q_ref, k_ref, v_ref, G, off, L, h, lane):
    hh = h % 2
    qmask = (lane >= M_DQK * hh) & (lane < M_DQK * (hh + 1))
    pair = pl.ds(128 * (h // 2), 128)
    qc = jnp.where(qmask, q_ref[0, pl.ds(off, L), pair], 0.0)
    kc = jnp.where(qmask, k_ref[0, pl.ds(off, L), pair], 0.0)
    vc = v_ref[0, pl.ds(off, L), pl.ds(M_DV * h, M_DV)]
    gi = _sum1(jnp.where(lane == h, G, 0.0))
    gf = _sum1(jnp.where(lane == h + HEADS, G, 0.0))
    return qmask, qc, kc, vc, gi, gf


def _gate_rows(gi, gf, eye, low, upp):
    lf = _log_sigmoid(gf)
    lf_row = _sum0(jnp.where(eye, lf, 0.0))
    gi_row = _sum0(jnp.where(eye, gi, 0.0))
    bcum = _sum1(jnp.where(low, lf_row, 0.0))
    bcum_row = _sum0(jnp.where(upp, lf, 0.0))
    b_last = _sum0(lf)
    return gi_row, bcum, bcum_row, b_last


def _cell_specs(SB, cpb, blk):
    def seq(width, col):
        return pl.BlockSpec((1, SB, width), lambda b, s: (b, blk(s), col))

    def state(rows):
        return pl.BlockSpec((1, HEADS, cpb, rows, 128), lambda b, s: (b, 0, blk(s), 0, 0))

    ins = [seq(D // 2, 0), seq(D // 2, 1), seq(D, 1), seq(D, 2), seq(128, 3 * D // 128),
           pl.BlockSpec((1, D), lambda b, s: (0, 0)), pl.BlockSpec((1, 128), lambda b, s: (0, 0))]
    return ins, [state(M_DV), state(1), state(1)], seq


def mlstm_cell_fwd(qk3, proj3, gain, gbias, name, gather=()):
    Bl, S, _ = qk3.shape
    L = M_CHUNK
    SB = min(M_SLAB, S)
    cpb, nc, nsb = SB // L, S // L, S // SB
    scale = M_DQK ** -0.5

    def body(q_ref, k_ref, v_ref, o_ref, g_ref, gain_ref, gb_ref, y_ref, cst_ref, nst_ref, mst_ref, *state):
        C_s, n_s, m_s = state[:HEADS], state[HEADS:2 * HEADS], state[2 * HEADS:]

        @pl.when(pl.program_id(1) == 0)
        def _():
            for ref in state:
                ref[...] = jnp.zeros_like(ref)
        lane = lax.broadcasted_iota(jnp.int32, (L, 128), 1)
        eye, low, upp = _cell_consts(L)

        def step(c, carry):
            off = pl.multiple_of(c * L, L)
            G = g_ref[0, pl.ds(off, L), :] + gb_ref[...]

            def head(h):
                C, n, mb = C_s[h][...], n_s[h][...], m_s[h][...]
                cst_ref[0, h, c] = C
                nst_ref[0, h, c] = n
                mst_ref[0, h, c] = mb
                m = mb[:, 0:1]
                _, qc, kc, vc, gi, gf = _load_chunk(q_ref, k_ref, v_ref, G, off, L, h, lane)
                gi_row, bcum, bcum_row, b_last = _gate_rows(gi, gf, eye, low, upp)
                state = _chunk_state(kc, vc, gi, bcum, b_last, C, n, m)
                next(state)
                r = yield from _chunk_out(qc * scale, kc, vc, gi_row, bcum, bcum_row, low, C, n, m)
                hc = r["hc"]
                hn = hc * lax.rsqrt(jnp.mean(hc * hc, axis=-1, keepdims=True) + RMS_EPS)
                cols = pl.ds(M_DV * h, M_DV)
                oc = o_ref[0, pl.ds(off, L), cols]
                y_ref[0, pl.ds(off, L), cols] = (_sigmoid(oc) * hn * gain_ref[:, cols]).astype(BF16)
                C2, n2, m2, _, _, _ = _finish(state)
                C_s[h][...] = C2
                n_s[h][...] = n2
                m_s[h][...] = jnp.broadcast_to(m2, (1, 128))

            _interleave(head(h) for h in range(HEADS))
            return carry

        lax.fori_loop(0, cpb, step, 0)

    ins, states, seq = _cell_specs(SB, cpb, lambda s: s)
    grid = (Bl, nsb)
    body, c_in, c_out, c_shape, c_scratch = host_comm(body, grid, 7, 4, gather=gather)
    return pl.pallas_call(
        body, name=name, grid=grid, in_specs=ins + c_in, out_specs=[seq(D, 0)] + states + c_out,
        out_shape=[jax.ShapeDtypeStruct((Bl, S, D), BF16),
                   jax.ShapeDtypeStruct((Bl, HEADS, nc, M_DV, 128), F32),
                   jax.ShapeDtypeStruct((Bl, HEADS, nc, 1, 128), F32),
                   jax.ShapeDtypeStruct((Bl, HEADS, nc, 1, 128), F32)] + c_shape,
        scratch_shapes=[pltpu.VMEM((M_DV, 128), F32)] * HEADS + [pltpu.VMEM((1, 128), F32)] * (2 * HEADS) + c_scratch,
        compiler_params=_cp(2),
    )(qk3, qk3, proj3, proj3, proj3, gain, gbias, *gather)


def mlstm_cell_bwd(qk3, proj3, gain, gbias, dy3, states, name, exchange=()):
    Bl, S, _ = qk3.shape
    L = M_CHUNK
    SB = min(M_SLAB, S)
    cpb, nsb = SB // L, S // SB
    scale = M_DQK ** -0.5

    def body(q_ref, k_ref, v_ref, o_ref, g_ref, gain_ref, gb_ref, cst_ref, nst_ref, mst_ref, dy_ref,
             dq_ref, dk_ref, dv_ref, do_ref, dg_ref, dgain_ref, dgb_ref, *state):
        dC_s, dn_s, dgain_s, dgb_s = state[:HEADS], state[HEADS:2 * HEADS], state[2 * HEADS:3 * HEADS], state[-1]
        s = pl.program_id(1)

        @pl.when(s == 0)
        def _():
            for ref in state:
                ref[...] = jnp.zeros_like(ref)
        lane = lax.broadcasted_iota(jnp.int32, (L, 128), 1)
        rowi = lax.broadcasted_iota(jnp.int32, (L, 1), 0)
        eye, low, upp = _cell_consts(L)

        def bstep(t, carry):
            c = cpb - 1 - t
            off = pl.multiple_of(c * L, L)
            G = g_ref[0, pl.ds(off, L), :] + gb_ref[...]
            shared = dict(slab=jnp.zeros((L, 128), F32))

            def head(h):
                cols = pl.ds(M_DV * h, M_DV)
                gain_h = gain_ref[:, cols]
                C, n, m = cst_ref[0, h, c], nst_ref[0, h, c], mst_ref[0, h, c][:, 0:1]
                dC_n, dn_n = dC_s[h][...], dn_s[h][...]
                qmask, qc, kc, vc, gi, gf = _load_chunk(q_ref, k_ref, v_ref, G, off, L, h, lane)
                gi_row, bcum, bcum_row, b_last = _gate_rows(gi, gf, eye, low, upp)
                qs = qc * scale
                _, _, _, wa, sp, sl = _finish(_chunk_state(kc, vc, gi, bcum, b_last, C, n, m))
                dcl_b = (sl * dC_n).astype(BF16)
                t1_mm = _dot(vc.astype(BF16), dcl_b)
                dv_mm = _dot_nt(kc.astype(BF16), dcl_b)
                r = yield from _chunk_out(qs, kc, vc, gi_row, bcum, bcum_row, low, C, n, m)
                hc, den, dn, sc, dm, iw, qn = r["hc"], r["den"], r["dn"], r["sc"], r["dm"], r["iw"], r["qn"]
                qs_b, k_b, v_b = r["qs_b"], r["k_b"], r["v_b"]
                dy = dy_ref[0, pl.ds(off, L), cols].astype(F32)
                oc = o_ref[0, pl.ds(off, L), cols]
                sig_o = _sigmoid(oc)
                rr = lax.rsqrt(jnp.mean(hc * hc, axis=-1, keepdims=True) + RMS_EPS)
                hn = hc * rr
                dgain_s[h][...] += _sum0(dy * sig_o * hn)
                do_ref[0, pl.ds(off, L), cols] = (
                    dy * hn * gain_h * sig_o * (1.0 - sig_o)).astype(BF16)
                dhn = dy * sig_o * gain_h
                dhc = rr * dhn - hc * (rr * rr * rr) * jnp.mean(dhn * hc, axis=-1, keepdims=True)
                dnum = dhc / dn
                gden = -_sum1(dhc * hc) / dn
                dden = jnp.where(jnp.abs(den) > r["floor"], gden * jnp.sign(den), 0.0)
                dnum_b = dnum.astype(BF16)
                dqc_b = (iw * dnum).astype(BF16)
                dsc_mm = _dot_nt(dnum_b, v_b)
                dv = _dot_tn(sc.astype(BF16), dnum_b)
                dqs_mm = _dot(dqc_b, C.astype(BF16))
                dC_out = _dot_tn(dqc_b, qs_b)
                diw = _sum1(dnum * r["qc"]) + dden * qn
                wq = iw * dden
                dn_out = _sum0(wq * qs)
                dn_loc = sl * dn_n
                dsp = _sum1(_sum0(dC_n * C)) + _sum1(dn_n * n)
                yield
                dsc = dsc_mm + dden
                dS_b = (dsc * dm).astype(BF16)
                gm = dsc * sc
                dqs2_mm = _dot(dS_b, k_b)
                dk = _dot_tn(dS_b, qs_b)
                dqs = dqs_mm + wq * n
                dbc = _sum1(gm) + diw * iw
                colg = _sum0(gm)
                dC_p = sp * dC_n + dC_out
                dn_p = sp * dn_n + dn_out
                db_last = dsp * sp
                t1 = t1_mm + dn_loc
                dwa = _sum1(t1 * kc)
                dv = dv + wa * dv_mm
                yield
                dqs = dqs + dqs2_mm
                dk = dk + wa * t1
                da = dwa * wa
                db_last = db_last + _sum0(da)
                dbc = dbc - da + jnp.where(rowi == L - 1, db_last, 0.0)
                dbc_row = _sum0(jnp.where(eye, dbc, 0.0)) - colg
                dgi = da + _sum1(jnp.where(eye, colg, 0.0))
                dlf = _sum1(jnp.where(upp, dbc_row, 0.0))
                dgf = dlf * _sigmoid(-gf)
                dq = jnp.where(qmask, dqs * scale, 0.0)
                dk = jnp.where(qmask, dk, 0.0)
                shared["slab"] = (shared["slab"] + jnp.where(lane == h, dgi, 0.0)
                                  + jnp.where(lane == h + HEADS, dgf, 0.0))
                dv_ref[0, pl.ds(off, L), cols] = dv.astype(BF16)
                dC_s[h][...] = dC_p
                dn_s[h][...] = dn_p
                if h % 2 == 0:
                    shared["dq"], shared["dk"] = dq, dk
                else:
                    pair = pl.ds(128 * (h // 2), 128)
                    dq_ref[0, pl.ds(off, L), pair] = shared["dq"] + dq
                    dk_ref[0, pl.ds(off, L), pair] = shared["dk"] + dk

            _interleave(head(h) for h in range(HEADS))
            dg_ref[0, pl.ds(off, L), :] = shared["slab"]
            dgb_s[...] += _sum0(shared["slab"])
            return carry

        lax.fori_loop(0, cpb, bstep, 0)

        @pl.when(s == nsb - 1)
        def _():
            for h in range(HEADS):
                dgain_ref[0, :, pl.ds(M_DV * h, M_DV)] = dgain_s[h][...]
            dgb_ref[0] = dgb_s[...]

    ins, states_specs, seq = _cell_specs(SB, cpb, lambda s: nsb - 1 - s)
    once = lambda width: pl.BlockSpec((1, 1, width), lambda b, s: (b, 0, 0))
    grid = (Bl, nsb)
    body, c_in, c_out, c_shape, c_scratch = host_comm(body, grid, 11, 7, exchange=exchange)
    return pl.pallas_call(
        body, name=name, grid=grid, in_specs=ins + states_specs + [seq(D, 0)] + c_in,
        out_specs=[seq(D // 2, 0), seq(D // 2, 0), seq(D, 0), seq(D, 0), seq(128, 0), once(D), once(128)] + c_out,
        out_shape=[jax.ShapeDtypeStruct((Bl, S, D // 2), F32), jax.ShapeDtypeStruct((Bl, S, D // 2), F32),
                   jax.ShapeDtypeStruct((Bl, S, D), BF16), jax.ShapeDtypeStruct((Bl, S, D), BF16),
                   jax.ShapeDtypeStruct((Bl, S, 128), F32), jax.ShapeDtypeStruct((Bl, 1, D), F32),
                   jax.ShapeDtypeStruct((Bl, 1, 128), F32)] + c_shape,
        scratch_shapes=[pltpu.VMEM((M_DV, 128), F32)] * HEADS + [pltpu.VMEM((1, 128), F32)] * (2 * HEADS + 1)
        + c_scratch,
        compiler_params=_cp(2),
    )(qk3, qk3, proj3, proj3, proj3, gain, gbias, *states, dy3, *exchange)


def _attn_scores(q, kc, kp, n, row, col, scale):
    s_c = jnp.where(col <= row, _dot_nt(q, kc) * scale, NEG)
    s_p = jnp.where(jnp.logical_and(col >= row, n > 0), _dot_nt(q, kp) * scale, NEG)
    return s_c, s_p


def _to_streams(src, dst, tmp, dil, Sd):
    if dil == 1:
        dst[...] = src[...].astype(dst.dtype)
        return
    if src.dtype != F32:
        tmp[...] = src[...].astype(F32)
        src = tmp
    for r in range(dil):
        dst[pl.ds(r * Sd, Sd), :] = src[pl.ds(r, Sd, stride=dil), :].astype(dst.dtype)


def _from_streams(src, dst, dil, Sd):
    if dil == 1:
        dst[...] = src[...]
        return
    for r in range(dil):
        dst[pl.ds(r, Sd, stride=dil), :] = src[pl.ds(r * Sd, Sd), :]


def attn_fwd(proj, Bl, S, name):
    scale = A_BLK ** -0.5
    pv = proj.reshape(Bl, S, A_PROJ)
    ng = len(DIL_GROUPS)
    rows = 512

    def body(*refs):
        ins, (ob_ref, of_ref, lt_ref) = refs[:3 * ng], refs[3 * ng:3 * ng + 3]
        tmp, qs, ks, vs, os_, ls = refs[3 * ng + 3:3 * ng + 9]
        o_nat, l_nat = refs[3 * ng + 9:4 * ng + 9], refs[4 * ng + 9:]
        row = lax.broadcasted_iota(jnp.int32, (A_BLK, A_BLK), 0)
        col = lax.broadcasted_iota(jnp.int32, (A_BLK, A_BLK), 1)
        for g, (_, dil) in enumerate(DIL_GROUPS):
            Sd = S // dil
            nb = Sd // A_BLK
            for src, dst in zip(ins[3 * g:3 * g + 3], (qs, ks, vs)):
                _to_streams(src.at[0], dst, tmp, dil, Sd)

            def step(i, carry, nb=nb):
                n = i % nb
                off = pl.multiple_of(i * A_BLK, A_BLK)
                offp = pl.multiple_of(jnp.maximum(i - 1, 0) * A_BLK, A_BLK)
                q = qs[pl.ds(off, A_BLK), :]
                s_c, s_p = _attn_scores(q, ks[pl.ds(off, A_BLK), :], ks[pl.ds(offp, A_BLK), :], n, row, col, scale)
                m = jnp.maximum(jnp.max(s_c, axis=1, keepdims=True), jnp.max(s_p, axis=1, keepdims=True))
                p_c = jnp.exp(s_c - m)
                p_p = jnp.exp(s_p - m)
                den = _sum1(p_c) + _sum1(p_p)
                o = (_dot(p_c.astype(BF16), vs[pl.ds(off, A_BLK), :])
                     + _dot(p_p.astype(BF16), vs[pl.ds(offp, A_BLK), :]))
                os_[pl.ds(off, A_BLK), :] = o / den
                ls[pl.ds(off, A_BLK), :] = jnp.broadcast_to(m + jnp.log(den), (A_BLK, 128))
                return carry

            lax.fori_loop(0, dil * nb, step, 0, unroll=A_UNROLL)
            _from_streams(os_, o_nat[g], dil, Sd)
            _from_streams(ls, l_nat[g], dil, Sd)

        def merge(t, carry):
            sl = pl.ds(pl.multiple_of(t * rows, rows), rows)
            lses = [l[sl, :] for l in l_nat]
            m = functools.reduce(jnp.maximum, lses)
            ws = [jnp.exp(l - m) for l in lses]
            den = functools.reduce(lambda a, b: a + b, ws)
            o = functools.reduce(lambda a, b: a + b, [w * r[sl, :] for w, r in zip(ws, o_nat)]) / den
            of_ref[0, sl, :] = o
            ob_ref[0, sl, :] = o.astype(BF16)
            lt_ref[0, sl, :] = m + jnp.log(den)
            return carry

        lax.fori_loop(0, S // rows, merge, 0)

    in_specs = [pl.BlockSpec((1, S, 128), lambda b, h, c=g * 24 + j * HEADS: (b, 0, c + h))
                for g in range(ng) for j in range(3)]
    ospec = pl.BlockSpec((1, S, 128), lambda b, h: (b, 0, h))
    slab = lambda dt: pltpu.VMEM((S, 128), dt)
    outs = pl.pallas_call(
        body, name=name, grid=(Bl, HEADS), in_specs=in_specs, out_specs=[ospec] * 3,
        out_shape=[jax.ShapeDtypeStruct((Bl, S, D), BF16), jax.ShapeDtypeStruct((Bl, S, D), F32),
                   jax.ShapeDtypeStruct((Bl, S, D), F32)],
        scratch_shapes=[slab(F32)] + [slab(BF16)] * 3 + [slab(F32)] * (2 + 2 * ng),
        compiler_params=_cp(2),
    )(*([pv] * (3 * ng)))
    return [t.reshape(Bl * S, D) for t in outs]


def attn_bwd(proj, do, o, lse, Bl, S, g, dil, name):
    Sd = S // dil
    nb = Sd // A_BLK
    scale = A_BLK ** -0.5
    pv = proj.reshape(Bl, S, A_PROJ)
    dov, ov, lv = (t.reshape(Bl, S, D) for t in (do, o, lse))

    def body(q_ref, k_ref, v_ref, do_ref, o_ref, l_ref, dq_ref, dk_ref, dv_ref,
             tmp, qs, ks, vs, dos, dls, lts, dq_s, dk_s, dv_s):
        row = lax.broadcasted_iota(jnp.int32, (A_BLK, A_BLK), 0)
        col = lax.broadcasted_iota(jnp.int32, (A_BLK, A_BLK), 1)
        for src, dst in ((q_ref, qs), (k_ref, ks), (v_ref, vs), (do_ref, dos), (l_ref, lts)):
            _to_streams(src.at[0], dst, tmp, dil, Sd)
        tmp[...] = jnp.broadcast_to(_sum1(do_ref[0].astype(F32) * o_ref[0]), (S, 128))
        _to_streams(tmp, dls, None, dil, Sd)
        dk_s[...] = jnp.zeros_like(dk_s)
        dv_s[...] = jnp.zeros_like(dv_s)

        def step(i, carry):
            n = i % nb
            off = pl.multiple_of(i * A_BLK, A_BLK)
            offp = pl.multiple_of(jnp.maximum(i - 1, 0) * A_BLK, A_BLK)
            q = qs[pl.ds(off, A_BLK), :]
            kc, kp = ks[pl.ds(off, A_BLK), :], ks[pl.ds(offp, A_BLK), :]
            vc, vp = vs[pl.ds(off, A_BLK), :], vs[pl.ds(offp, A_BLK), :]
            do_b = dos[pl.ds(off, A_BLK), :]
            delta = dls[pl.ds(off, A_BLK), :][:, 0:1]
            lt = lts[pl.ds(off, A_BLK), :][:, 0:1]
            s_c, s_p = _attn_scores(q, kc, kp, n, row, col, scale)
            p_c = jnp.exp(s_c - lt)
            p_p = jnp.exp(s_p - lt)
            ds_c = (p_c * (_dot_nt(do_b, vc) - delta) * scale).astype(BF16)
            ds_p = (p_p * (_dot_nt(do_b, vp) - delta) * scale).astype(BF16)
            dq_s[pl.ds(off, A_BLK), :] = _dot(ds_c, kc) + _dot(ds_p, kp)
            dk_s[pl.ds(off, A_BLK), :] += _dot_tn(ds_c, q)
            dk_s[pl.ds(offp, A_BLK), :] += _dot_tn(ds_p, q)
            dv_s[pl.ds(off, A_BLK), :] += _dot_tn(p_c.astype(BF16), do_b)
            dv_s[pl.ds(offp, A_BLK), :] += _dot_tn(p_p.astype(BF16), do_b)
            return carry

        lax.fori_loop(0, dil * nb, step, 0, unroll=A_UNROLL)
        for src, dst in ((dq_s, dq_ref), (dk_s, dk_ref), (dv_s, dv_ref)):
            _from_streams(src, tmp, dil, Sd)
            dst[0] = tmp[...].astype(BF16)

    def spec(j):
        return pl.BlockSpec((1, S, 128), lambda b, h: (b, 0, g * 24 + j * HEADS + h))

    ospec = pl.BlockSpec((1, S, 128), lambda b, h: (b, 0, h))
    slab = lambda dt: pltpu.VMEM((S, 128), dt)
    outs = pl.pallas_call(
        body, name=name, grid=(Bl, HEADS),
        in_specs=[spec(0), spec(1), spec(2), ospec, ospec, ospec], out_specs=[ospec] * 3,
        out_shape=[jax.ShapeDtypeStruct((Bl, S, D), BF16)] * 3,
        scratch_shapes=[slab(F32)] + [slab(BF16)] * 4 + [slab(F32)] * 5,
        compiler_params=_cp(2),
    )(pv, pv, pv, dov, ov, lv)
    return [t.reshape(Bl * S, D) for t in outs]


def _as_slots(pair, shape):
    return tuple(t.reshape(shape) for t in pair)


def ffn_fwd(x, mod3, w_in, w_out, lng, lnb, tag, gather=()):
    a, g, u, h, gathered = ffn_in(x, mod3, w_in, tag + "_in", gather=gather)
    out, xn = proj_post(a, w_out, x, mod3, lng, lnb, 0.5, tag + "_out")
    return xn, (x, out, g, u, h, a), gathered


def ffn_bwd(dxn, saved, mod3, w_in, w_out, lng, tag, exchange=((), (), ()), exchange_own=False):
    x, out, g, u, h, a = saved
    dxres, dout, dgu, dlg, dlb, dgate, *got0 = post_bwd(dxn, x, out, mod3, lng, w_out, 0.5, tag + "_outb",
                                                        tm=256, gu=(g, u), exchange=exchange[0])
    *dw_in, got1 = mm_tn(h, dgu, tag + "_dwin", a_copies=True, exchange=exchange[1]) + (() if exchange[1] else ([],))
    *dw_out, got2 = mm_tn(a, dout, tag + "_dwout", bw=D, exchange=exchange[2]) + (() if exchange[2] else ([],))
    dw_out = _as_slots(dw_out, (N_DEV, D_FF // N_DEV, D))
    dx, dsh, dsc, *own = modmm_bwd(dgu, w_in, x, mod3, dxres, tag + "_inb", tm=256,
                                   exchange=[dw_in[1], dw_out[1]] if exchange_own else ())
    dmod3 = jnp.concatenate([dsh, dsc, dgate], axis=1)
    return (dx, [tuple(dw_in), dw_out], dlg, dlb, dmod3,
            [got0[0] if got0 else [], got1, got2], (own[0] if own else []))


def mlstm_fwd(x, mod3, w_in, w_out, conv_w, gain, gbias, lng, lnb, Bl, S, gather=()):
    proj, h = modmm(x, mod3, w_in, F32, "ml_in", tn=M_PROJ_PAD // 5)
    proj3 = proj.reshape(Bl, S, M_PROJ_PAD)
    qk3 = conv_silu(proj3, conv_w, "ml_conv")
    y3, *rest = mlstm_cell_fwd(qk3, proj3, gain, gbias, "ml_cell", gather=gather)
    states, gathered = rest[:3], rest[3:]
    y = y3.reshape(Bl * S, D)
    out, xn = proj_post(y[None], w_out, x, mod3, lng, lnb, 1.0, "ml_out")
    return xn, (x, out, h, proj3, qk3, y, states), gathered


def mlstm_bwd(dxn, saved, mod3, w_in, w_out, conv_w, gain, gbias, lng, Bl, S, exchange=()):
    x, out, h, proj3, qk3, y, states = saved
    dxres, dout, dy, dlg, dlb, dgate = post_bwd(dxn, x, out, mod3, lng, w_out, 1.0, "ml_outb")
    dq, dk, dv, do, dg, dgain, dgb, *received = mlstm_cell_bwd(qk3, proj3, gain, gbias, dy.reshape(Bl, S, D),
                                                               states, "ml_cellb", exchange=exchange)
    dqk, dconv = conv_silu_bwd(proj3, conv_w, dq, dk, "ml_convb")
    dproj = jnp.concatenate([dqk, dv, do, dg.astype(BF16)], axis=2).reshape(Bl * S, M_PROJ_PAD)
    dx, dsh, dsc = modmm_bwd(dproj, w_in, x, mod3, dxres, "ml_inb", tn=M_PROJ_PAD // 5)
    dwi, _ = mm_tn(h, dproj, "ml_dwin", bw=M_PROJ_PAD // 5)
    dwi = _restack(jnp.moveaxis(dwi, 0, 1).reshape(D, M_PROJ_PAD)[:, :M_PROJ], 1)
    dw_out = _as_slots(mm_tn(y, dout, "ml_dwout", bw=D), (N_DEV, D // N_DEV, D))
    small = (jnp.sum(dconv, axis=0), jnp.sum(dgain, axis=0), jnp.sum(dgb, axis=0)[:, :2 * HEADS])
    dmod3 = jnp.concatenate([dsh, dsc, dgate], axis=1)
    return dx, [(dwi, dwi.astype(BF16)), dw_out], dlg, dlb, dmod3, small, received


def attn_mixer_fwd(x, mod3, w_in, w_out, lng, lnb, Bl, S):
    proj, h = modmm(x, mod3, w_in, BF16, "at_in")
    ob, of, lt = attn_fwd(proj, Bl, S, "at_core")
    out, xn = proj_post(ob[None], w_out, x, mod3, lng, lnb, 1.0, "at_out")
    return xn, (x, out, h, proj, ob, of, lt)


def attn_mixer_bwd(dxn, saved, mod3, w_in, w_out, lng, Bl, S):
    x, out, h, proj, ob, of, lt = saved
    dxres, dout, do, dlg, dlb, dgate = post_bwd(dxn, x, out, mod3, lng, w_out, 1.0, "at_outb")
    do = do[0]
    parts = []
    for g, (_, dil) in enumerate(DIL_GROUPS):
        parts += attn_bwd(proj, do, of, lt, Bl, S, g, dil, "at_coreb%d" % g)
    dproj = jnp.concatenate(parts, axis=1)
    dx, dsh, dsc = modmm_bwd(dproj, w_in, x, mod3, dxres, "at_inb")
    dw_in = mm_tn(h, dproj, "at_dwin", bw=w_in.shape[2])
    dw_out = _as_slots(mm_tn(ob, dout, "at_dwout", bw=D), (N_DEV, D // N_DEV, D))
    return dx, [dw_in, dw_out], dlg, dlb, jnp.concatenate([dsh, dsc, dgate], axis=1)


def _unstack(stacked, axis):
    full = jnp.moveaxis(stacked, 0, axis)
    shp = list(full.shape)
    shp[axis:axis + 2] = [shp[axis] * shp[axis + 1]]
    return full.reshape(shp)


def _restack(full, axis):
    shp = list(full.shape)
    shp[axis:axis + 1] = [N_DEV, shp[axis] // N_DEV]
    return jnp.moveaxis(full.reshape(shp), axis, 0)


def kernel(x, c, ada_w, ada_b, ln_g, ln_b, ffn_w_in, ffn_w_out, mlstm_w_in, mlstm_gate_bias, mlstm_conv_w, mlstm_head_gain, mlstm_w_out, attn_w_in, attn_w_out, loss_target, m_ada_w, m_ada_b, m_ln_g, m_ln_b, m_ffn_w_in, m_ffn_w_out, m_mlstm_w_in, m_mlstm_gate_bias, m_mlstm_conv_w, m_mlstm_head_gain, m_mlstm_w_out, m_attn_w_in, m_attn_w_out, v_ada_w, v_ada_b, v_ln_g, v_ln_b, v_ffn_w_in, v_ffn_w_out, v_mlstm_w_in, v_mlstm_gate_bias, v_mlstm_conv_w, v_mlstm_head_gain, v_mlstm_w_out, v_attn_w_in, v_attn_w_out):
    Bl, S, _ = x.shape
    T = Bl * S
    Bg = Bl * N_DEV
    me = 4 * lax.axis_index("x") + 2 * lax.axis_index("y") + lax.axis_index("c")
    onehot = (jnp.arange(N_DEV) == me).astype(F32)
    weights = dict(ada_w=ada_w, ada_b=ada_b, ln_g=ln_g, ln_b=ln_b, ffn_w_in=ffn_w_in, ffn_w_out=ffn_w_out,
                   mlstm_w_in=mlstm_w_in, mlstm_gate_bias=mlstm_gate_bias, mlstm_conv_w=mlstm_conv_w,
                   mlstm_head_gain=mlstm_head_gain, mlstm_w_out=mlstm_w_out, attn_w_in=attn_w_in,
                   attn_w_out=attn_w_out)
    m_in = dict(ada_w=m_ada_w, ada_b=m_ada_b, ln_g=m_ln_g, ln_b=m_ln_b, ffn_w_in=m_ffn_w_in,
                ffn_w_out=m_ffn_w_out, mlstm_w_in=m_mlstm_w_in, mlstm_gate_bias=m_mlstm_gate_bias,
                mlstm_conv_w=m_mlstm_conv_w, mlstm_head_gain=m_mlstm_head_gain, mlstm_w_out=m_mlstm_w_out,
                attn_w_in=m_attn_w_in, attn_w_out=m_attn_w_out)
    v_in = dict(ada_w=v_ada_w, ada_b=v_ada_b, ln_g=v_ln_g, ln_b=v_ln_b, ffn_w_in=v_ffn_w_in,
                ffn_w_out=v_ffn_w_out, mlstm_w_in=v_mlstm_w_in, mlstm_gate_bias=v_mlstm_gate_bias,
                mlstm_conv_w=v_mlstm_conv_w, mlstm_head_gain=v_mlstm_head_gain, mlstm_w_out=v_mlstm_w_out,
                attn_w_in=v_attn_w_in, attn_w_out=v_attn_w_out)

    mixer = ("mlstm", "attn")
    shards = [[ffn_w_in[layer, 0], ffn_w_in[layer, 1], ffn_w_out[layer, 0], ffn_w_out[layer, 1],
               weights[mixer[layer] + "_w_in"][0], weights[mixer[layer] + "_w_out"][0]] for layer in range(DEPTH)]
    sends = [[s.astype(BF16) for s in layer_shards] for layer_shards in shards]
    small = jnp.concatenate([c.reshape(-1), ln_g.reshape(-1), ln_b.reshape(-1), mlstm_conv_w.reshape(-1)])
    n_small = small.shape[0]
    small = jnp.pad(small, (0, -n_small % (8 * PACK_COLS))).reshape(-1, PACK_COLS)

    def gathered_weights(g):
        return ((g[0], g[1]), (g[2].reshape(4, D_FF // 4, D), g[3].reshape(4, D_FF // 4, D)), g[4],
                g[5].reshape(1, D, D))

    first_in, small_all = all_gather([sends[0][0], small], "ag_params")
    full = [None, None]
    small_flat = small_all.reshape(N_DEV, -1)
    o0 = 0
    c_all = small_flat[:, o0:o0 + c.size].reshape(Bg, D)
    o0 += c.size
    lng_full = _unstack(small_flat[:, o0:o0 + ln_g.size].reshape((N_DEV,) + ln_g.shape), 2)
    o0 += ln_g.size
    lnb_full = _unstack(small_flat[:, o0:o0 + ln_b.size].reshape((N_DEV,) + ln_b.shape), 2)
    o0 += ln_b.size
    conv_full = _unstack(small_flat[:, o0:o0 + mlstm_conv_w.size].reshape((N_DEV,) + mlstm_conv_w.shape), 2)[0]
    gbias =jnp.pad(mlstm_gate_bias, ((0, 0), (0, 128 - 2 * HEADS)))

    ncols = ada_w.shape[2]
    ada_b_cols = lax.dynamic_slice_in_dim(ada_b, me * ncols, ncols, axis=1).reshape(DEPTH, 1, ncols)
    mod_cols = ada_fwd(c_all, ada_w, ada_b_cols, "ada_fwd")
    (mod_g,) = all_gather([mod_cols.reshape(DEPTH * Bg, ncols)], "ag_mod")
    mod_full = _unstack(mod_g.reshape(N_DEV, DEPTH, Bg, ncols), 2)
    mod_mine = lax.dynamic_slice_in_dim(mod_full, me * Bl, Bl, axis=1).reshape(DEPTH, Bl, 3, 3, D)

    xt = x.reshape(T, D)
    saved = []
    for layer in range(DEPTH):
        def lnp(s, layer=layer):
            return lng_full[layer, s].reshape(1, D), lnb_full[layer, s].reshape(1, D)
        md = mod_mine[layer]
        if layer == 0:
            a, g, u, h, late = ffn_in(xt, md[:, 0], first_in, "f0a_in", gather=sends[0][1:])
            full[0] = gathered_weights([first_in] + late)
            out, xn = proj_post(a, full[0][1][0], xt, md[:, 0], *lnp(0), 0.5, "f0a_out")
            xt, sv0 = xn, (xt, out, g, u, h, a)
            mw_in = jnp.pad(_unstack(full[0][2], 1), ((0, 0), (0, M_PROJ_PAD - M_PROJ)))
        else:
            xt, sv0, _ = ffn_fwd(xt, md[:, 0], full[layer][0][0], full[layer][1][0], *lnp(0), "f%da" % layer)
        f_in, f_out, mix_in, mix_out = full[layer]
        if layer % 2 == 0:
            xt, sv1, g1 = mlstm_fwd(xt, md[:, 1], mw_in, mix_out, conv_full, mlstm_head_gain, gbias, *lnp(1), Bl, S,
                                    gather=sends[1])
            full[1] = gathered_weights(g1)
        else:
            xt, sv1 = attn_mixer_fwd(xt, md[:, 1], mix_in, mix_out, *lnp(1), Bl, S)
        xt, sv2, _ = ffn_fwd(xt, md[:, 2], f_in[1], f_out[1], *lnp(2), "f%db" % layer)
        saved.append((sv0, sv1, sv2))

    dxt, lsum = loss_head(xt, loss_target.reshape(T, D), "loss")
    loss = lax.psum(lsum[0, 0], MESH_AXES)

    dmod, dlg_all, dlb_all = [None] * DEPTH, [None] * DEPTH, [None] * DEPTH
    wgrads = [None] * DEPTH
    recvs = [[None] * 6 for _ in range(DEPTH)]
    ml_small = None
    for layer in reversed(range(DEPTH)):
        md = mod_mine[layer]
        f_in, f_out, mix_in, mix_out = full[layer]
        sv0, sv1, sv2 = saved[layer]
        dxt, dw2, dlg2, dlb2, dm2, _, _ = ffn_bwd(dxt, sv2, md[:, 2], f_in[1], f_out[1],
                                                  lng_full[layer, 2].reshape(1, D), "f%db" % layer)
        lg1 = lng_full[layer, 1].reshape(1, D)
        if layer % 2 == 0:
            dxt, dw1, dlg1, dlb1, dm1, ml_small, got = mlstm_bwd(
                dxt, sv1, md[:, 1], mw_in, mix_out, conv_full, mlstm_head_gain, gbias, lg1, Bl, S,
                exchange=[b16 for _, b16 in wgrads[1]])
            recvs[1] = got
            dxt, dw0, dlg0, dlb0, dm0, got, own = ffn_bwd(
                dxt, sv0, md[:, 0], f_in[0], f_out[0], lng_full[layer, 0].reshape(1, D), "f%da" % layer,
                exchange=([dw2[0][1]], [dw2[1][1], dw1[0][1]], [dw1[1][1]]), exchange_own=True)
            (recvs[0][1],), (recvs[0][3], recvs[0][4]), (recvs[0][5],) = got
            recvs[0][0], recvs[0][2] = own
        else:
            dxt, dw1, dlg1, dlb1, dm1 = attn_mixer_bwd(dxt, sv1, md[:, 1], mix_in, mix_out, lg1, Bl, S)
            dxt, dw0, dlg0, dlb0, dm0, _, _ = ffn_bwd(dxt, sv0, md[:, 0], f_in[0], f_out[0],
                                                      lng_full[layer, 0].reshape(1, D), "f%da" % layer)
        wgrads[layer] = [dw0[0], dw2[0], dw0[1], dw2[1], dw1[0], dw1[1]]
        dmod[layer] = jnp.stack([dm0, dm1, dm2], axis=1).reshape(Bl, 9 * D)
        dlg_all[layer] = jnp.concatenate([dlg0, dlg1, dlg2], axis=0)
        dlb_all[layer] = jnp.concatenate([dlb0, dlb1, dlb2], axis=0)
    grad_x = dxt.reshape(Bl, S, D)

    gsh = [[shard_sum(lax.dynamic_index_in_dim(f32, me, axis=0, keepdims=False), recv, onehot,
                      "rs_sum%d_%d" % (layer, i))
            for i, ((f32, _), recv) in enumerate(zip(wgrads[layer], recvs[layer]))] for layer in range(DEPTH)]
    grads = {"ffn_w_in": jnp.stack([jnp.stack(g[0:2]) for g in gsh]),
             "ffn_w_out": jnp.stack([jnp.stack(g[2:4]) for g in gsh]),
             "mlstm_w_in": gsh[0][4][None], "mlstm_w_out": gsh[0][5][None],
             "attn_w_in": gsh[1][4][None], "attn_w_out": gsh[1][5][None]}

    dconv, dgain, dgbias = ml_small
    parts = [jnp.stack(dmod).reshape(-1), dgbias.reshape(-1), dgain.reshape(-1),
             jnp.stack(dlg_all).reshape(-1), jnp.stack(dlb_all).reshape(-1), dconv.reshape(-1)]
    sizes = [p.shape[0] for p in parts]
    flat = jnp.concatenate(parts)
    flat = jnp.pad(flat, (0, -flat.shape[0] % (8 * PACK_COLS))).reshape(-1, PACK_COLS)
    (sm_all,) = all_gather([flat], "ag_small")
    sm_sum = sum_leading(sm_all, "small_sum").reshape(-1)
    dmod_all = sm_all.reshape(N_DEV, -1)[:, :sizes[0]].reshape(N_DEV, DEPTH, Bl, 9 * D)
    dmod_all = jnp.moveaxis(dmod_all, 0, 1).reshape(DEPTH, Bg, 9 * D)
    o0 = sizes[0]
    grads["mlstm_gate_bias"] = sm_sum[o0:o0 + sizes[1]].reshape(mlstm_gate_bias.shape)
    o0 += sizes[1]
    grads["mlstm_head_gain"] = sm_sum[o0:o0 + sizes[2]].reshape(mlstm_head_gain.shape)
    o0 += sizes[2]
    nl = ln_g.shape[2]
    g_lng = sm_sum[o0:o0 + sizes[3]].reshape(DEPTH, 3, D)
    o0 += sizes[3]
    g_lnb = sm_sum[o0:o0 + sizes[4]].reshape(DEPTH, 3, D)
    o0 += sizes[4]
    g_conv = sm_sum[o0:o0 + sizes[5]].reshape(1, 4, D)
    grads["ln_g"] = lax.dynamic_slice_in_dim(g_lng, me * nl, nl, axis=2)
    grads["ln_b"] = lax.dynamic_slice_in_dim(g_lnb, me * nl, nl, axis=2)
    grads["mlstm_conv_w"] = lax.dynamic_slice_in_dim(g_conv, me * nl, nl, axis=2)
    dmod_cols = lax.dynamic_slice_in_dim(dmod_all, me * ncols, ncols, axis=2)
    gw, gb = ada_bwd(c_all.T, dmod_cols, dmod_all, "ada_bwd")
    grads["ada_w"] = gw
    grads["ada_b"] = gb.reshape(ada_b.shape)

    names = ["ada_w", "ada_b", "ln_g", "ln_b", "ffn_w_in", "ffn_w_out", "mlstm_w_in", "mlstm_gate_bias",
             "mlstm_conv_w", "mlstm_head_gain", "mlstm_w_out", "attn_w_in", "attn_w_out"]
    deltas, new_m, new_v = [], [], []
    for k in names:
        w = weights[k]
        shp2 = (math.prod(w.shape[:-1]), w.shape[-1])
        d_, m_, v_ = adamw(w.reshape(shp2), grads[k].reshape(shp2), m_in[k].reshape(shp2), v_in[k].reshape(shp2),
                           "adamw_" + k)
        deltas.append(d_.reshape(w.shape))
        new_m.append(m_.reshape(w.shape))
        new_v.append(v_.reshape(w.shape))
    return (loss, grad_x, *[grads[k] for k in names], *deltas, *new_m, *new_v)
```

```python
import functools
import math

import jax
import jax.numpy as jnp
from jax import lax
from jax.experimental import pallas as pl
from jax.experimental.pallas import tpu as pltpu

F32 = jnp.float32
BF16 = jnp.bfloat16

N_DEV = 8
MESH_AXES = ("x", "y", "c")
D = 1024
DEPTH = 2
D_FF = 2816
HEADS = 8
M_DQK = 64
M_DV = 128
M_CHUNK = 64
M_SLAB = 512
M_PROJ = 3088
M_PROJ_PAD = 3200
A_PROJ = 9216
DIL_GROUPS = ((128, 1), (512, 4), (2048, 16))
A_BLK = 128
A_UNROLL = 16
ALPHA = (2 * DEPTH) ** 0.25
LN_EPS = 1e-5
RMS_EPS = 1e-6
ADAM_LR = 0.001
ADAM_B1 = 0.9
ADAM_B2 = 0.999
ADAM_EPS = 1e-08
ADAM_WD = 0.01
ADAM_STEP = 10
NEG = -1e30
V7X_VMEM_LIMIT = 56 * 1024 * 1024
PACK_COLS = 1024
MESH_ID = pl.DeviceIdType.MESH
ANY_SPEC = pl.BlockSpec(memory_space=pl.ANY)


def _cp(n_axes):
    return pltpu.CompilerParams(dimension_semantics=("arbitrary",) * n_axes,
                                vmem_limit_bytes=V7X_VMEM_LIMIT)


def _dot(a, b):
    return jnp.dot(a, b, preferred_element_type=F32)


def _dot_nt(a, b):
    return lax.dot_general(a, b, (((1,), (1,)), ((), ())), preferred_element_type=F32)


def _dot_tn(a, b):
    return lax.dot_general(a, b, (((0,), (0,)), ((), ())), preferred_element_type=F32)


def _sum0(a):
    return jnp.sum(a, axis=0, keepdims=True)


def _sum1(a):
    return jnp.sum(a, axis=1, keepdims=True)


def _round(a):
    return a.astype(BF16).astype(F32)


def _sigmoid(a):
    return 1.0 / (1.0 + jnp.exp(-a))


def _tile(n, pref):
    t = min(n, pref)
    while n % t:
        t //= 2
    return t


def all_gather(arrs, name):
    n = len(arrs)

    def body(*refs):
        gather = Gather(refs[:n], refs[n:2 * n], *refs[2 * n:])
        gather.start()
        gather.finish()

    return pl.pallas_call(
        body, name=name, out_shape=Gather.out_shape(arrs),
        in_specs=[ANY_SPEC] * n, out_specs=[ANY_SPEC] * n, scratch_shapes=Gather.scratch(n),
    )(*arrs)


class Gather:
    def __init__(self, ins, outs, send_sems, recv_sems, local_sems):
        x, y, c = lax.axis_index("x"), lax.axis_index("y"), lax.axis_index("c")
        me, sibling = (x, y, c), (x, y, 1 - c)
        chips = [(1 - x, y), (x, 1 - y), (1 - x, 1 - y)]

        def slot(a, p):
            return outs[a].at[4 * p[0] + 2 * p[1] + p[2]]

        def copy(a, k, block, to, src=None):
            return pltpu.make_async_remote_copy(
                src_ref=slot(a, block) if src is None else src, dst_ref=slot(a, block),
                send_sem=send_sems.at[7 * a + k], recv_sem=recv_sems.at[7 * a + k],
                device_id=to, device_id_type=MESH_ID)

        n = len(ins)
        self.mine = [pltpu.make_async_copy(ins[a], slot(a, me), local_sems.at[a]) for a in range(n)]
        self.first, self.over_ici, self.passed, self.from_sibling = [], [], [], []
        for a in range(n):
            self.first.append(copy(a, 0, me, sibling, src=ins[a]))
            self.from_sibling.append(copy(a, 0, sibling, me))
            for j, chip in enumerate(chips):
                self.first.append(copy(a, 1 + j, me, (*chip, c), src=ins[a]))
                self.over_ici.append(copy(a, 1 + j, (*chip, c), me))
                self.passed.append(copy(a, 4 + j, (*chip, c), sibling))
                self.from_sibling.append(copy(a, 4 + j, (*chip, 1 - c), me))

    @staticmethod
    def out_shape(arrs):
        return [jax.ShapeDtypeStruct((N_DEV,) + a.shape, a.dtype) for a in arrs]

    @staticmethod
    def scratch(n):
        return [pltpu.SemaphoreType.DMA((7 * n,)), pltpu.SemaphoreType.DMA((7 * n,)),
                pltpu.SemaphoreType.DMA((n,))]

    def start(self):
        for cp in self.mine + self.first:
            cp.start()

    def finish(self):
        for landed, onward in zip(self.over_ici, self.passed):
            landed.wait_recv()
            onward.start()
        for cp in self.from_sibling:
            cp.wait_recv()
        for cp in self.first + self.passed:
            cp.wait_send()
        for cp in self.mine:
            cp.wait()


class Exchange:
    def __init__(self, sends, recvs, send_sems, recv_sems, local_sems):
        x, y, c = lax.axis_index("x"), lax.axis_index("y"), lax.axis_index("c")
        me = 4 * x + 2 * y + c
        self.own = [pltpu.make_async_copy(s.at[me], r.at[me], local_sems.at[a])
                    for a, (s, r) in enumerate(zip(sends, recvs))]
        self.copies = []
        for a, (s_ref, r_ref) in enumerate(zip(sends, recvs)):
            for k in range(1, N_DEV):
                px = 1 - x if (k >> 2) & 1 else x
                py = 1 - y if (k >> 1) & 1 else y
                pc = 1 - c if k & 1 else c
                self.copies.append(pltpu.make_async_remote_copy(
                    src_ref=s_ref.at[4 * px + 2 * py + pc], dst_ref=r_ref.at[me],
                    send_sem=send_sems.at[7 * a + k - 1], recv_sem=recv_sems.at[7 * a + k - 1],
                    device_id=(px, py, pc), device_id_type=MESH_ID))

    @staticmethod
    def scratch(n):
        return [pltpu.SemaphoreType.DMA((7 * n,)), pltpu.SemaphoreType.DMA((7 * n,)),
                pltpu.SemaphoreType.DMA((n,))]

    def start(self):
        for cp in self.own + self.copies:
            cp.start()

    def finish(self):
        for cp in self.copies:
            cp.wait_send()
            cp.wait_recv()
        for cp in self.own:
            cp.wait()


def host_comm(body, grid, n_in, n_out, gather=(), exchange=()):
    ng, nx = len(gather), len(exchange)
    if ng + nx == 0:
        return body, [], [], [], []

    def hosted(*refs):
        ins, c_in, rest = refs[:n_in], refs[n_in:n_in + ng + nx], refs[n_in + ng + nx:]
        outs, c_out, rest = rest[:n_out], rest[n_out:n_out + ng + nx], rest[n_out + ng + nx:]
        n_sems = 3 * ((ng > 0) + (nx > 0))
        scratch, sems = rest[:len(rest) - n_sems], rest[len(rest) - n_sems:]

        def comms():
            made = [Gather(c_in[:ng], c_out[:ng], *sems[:3])] if ng else []
            return made + ([Exchange(c_in[ng:], c_out[ng:], *sems[-3:])] if nx else [])

        ids = [pl.program_id(a) for a in range(len(grid))]

        @pl.when(functools.reduce(jnp.logical_and, [i == 0 for i in ids]))
        def _():
            for cm in comms():
                cm.start()
        body(*ins, *outs, *scratch)

        @pl.when(functools.reduce(jnp.logical_and, [i == g - 1 for i, g in zip(ids, grid)]))
        def _():
            for cm in comms():
                cm.finish()

    shapes = Gather.out_shape(gather) + [jax.ShapeDtypeStruct(a.shape, a.dtype) for a in exchange]
    scratch = (Gather.scratch(ng) if ng else []) + (Exchange.scratch(nx) if nx else [])
    return hosted, [ANY_SPEC] * (ng + nx), [ANY_SPEC] * (ng + nx), shapes, scratch


def shard_sum(own, recv, onehot, name):
    R, C = own.shape
    tr = _tile(R, 512)

    def body(oh_ref, own_ref, recv_ref, o_ref):
        acc = None
        for j in range(N_DEV):
            term = jnp.where(oh_ref[j] > 0.5, own_ref[...], recv_ref[j].astype(F32))
            acc = term if acc is None else acc + term
        o_ref[...] = acc

    return pl.pallas_call(
        body, name=name, grid=(R // tr,),
        in_specs=[pl.BlockSpec(memory_space=pltpu.SMEM),
                  pl.BlockSpec((tr, C), lambda i: (i, 0)),
                  pl.BlockSpec((N_DEV, tr, C), lambda i: (0, i, 0))],
        out_specs=pl.BlockSpec((tr, C), lambda i: (i, 0)),
        out_shape=jax.ShapeDtypeStruct((R, C), F32), compiler_params=_cp(1),
    )(onehot, own, recv)


def sum_leading(a, name):
    _, R, C = a.shape

    def body(a_ref, o_ref):
        acc = a_ref[0]
        for j in range(1, N_DEV):
            acc = acc + a_ref[j]
        o_ref[...] = acc

    return pl.pallas_call(body, name=name, out_shape=jax.ShapeDtypeStruct((R, C), F32),
                          compiler_params=_cp(0))(a)


def _col_chunks(w, tn):
    if w.ndim == 3:
        return w.shape[0], w.shape[2], pl.BlockSpec((None, w.shape[1], w.shape[2]), lambda i, j: (j, 0, 0))
    return w.shape[1] // tn, tn, pl.BlockSpec((w.shape[0], tn), lambda i, j: (0, j))


def modmm(x, mod3, w, out_dtype, name, tn=None):
    T, Dm = x.shape
    nj, tn, w_spec = _col_chunks(w, tn)
    N = nj * tn
    Bl = mod3.shape[0]
    tm = _tile(T // Bl, 1024)
    tpb = T // Bl // tm

    def body(x_ref, mod_ref, w_ref, o_ref, h_ref, hs):
        @pl.when(pl.program_id(1) == 0)
        def _():
            m = mod_ref[0]
            hs[...] = (x_ref[...] * (1.0 + m[1:2, :]) + m[0:1, :]).astype(BF16)
            h_ref[...] = hs[...]
        o_ref[...] = _dot(hs[...], w_ref[...]).astype(o_ref.dtype)

    return pl.pallas_call(
        body, name=name, grid=(T // tm, nj),
        in_specs=[pl.BlockSpec((tm, Dm), lambda i, j: (i, 0)),
                  pl.BlockSpec((1, 3, Dm), lambda i, j: (i // tpb, 0, 0)), w_spec],
        out_specs=[pl.BlockSpec((tm, tn), lambda i, j: (i, j)),
                   pl.BlockSpec((tm, Dm), lambda i, j: (i, 0))],
        out_shape=[jax.ShapeDtypeStruct((T, N), out_dtype), jax.ShapeDtypeStruct((T, Dm), BF16)],
        scratch_shapes=[pltpu.VMEM((tm, Dm), BF16)], compiler_params=_cp(2),
    )(x, mod3, w)


def modmm_bwd(dp, w, x, mod3, dxres, name, tn=None, tm=1024, exchange=()):
    T, Dm = x.shape
    Bl = mod3.shape[0]
    tm = _tile(T // Bl, tm)
    tpb = T // Bl // tm
    resident = dp.ndim == 3
    if resident:
        nc, nj = dp.shape[0], 1
        dp_spec = pl.BlockSpec((nc, tm, dp.shape[2]), lambda i, j: (0, i, 0))
        w_spec = pl.BlockSpec(w.shape, lambda i, j: (0, 0, 0))
    else:
        nj, tn, w_spec = _col_chunks(w, tn)
        dp_spec = pl.BlockSpec((tm, tn), lambda i, j: (i, j))

    def body(dp_ref, w_ref, x_ref, mod_ref, dxr_ref, dx_ref, dsh_ref, dsc_ref, acc):
        i, j = pl.program_id(0), pl.program_id(1)

        @pl.when(j == 0)
        def _():
            acc[...] = jnp.zeros_like(acc)
        if resident:
            for c in range(nc):
                acc[...] += _dot_nt(dp_ref[c], w_ref[c])
        else:
            acc[...] += _dot_nt(dp_ref[...], w_ref[...])

        @pl.when(j == nj - 1)
        def _():
            dh = acc[...]
            xx = x_ref[...]
            dx_ref[...] = dxr_ref[...] + dh * (1.0 + mod_ref[0][1:2, :])

            @pl.when(i % tpb == 0)
            def _():
                dsh_ref[...] = jnp.zeros_like(dsh_ref)
                dsc_ref[...] = jnp.zeros_like(dsc_ref)
            dsh_ref[0] += _sum0(dh)
            dsc_ref[0] += _sum0(dh * xx)

    grid = (T // tm, nj)
    body, c_in, c_out, c_shape, c_scratch = host_comm(body, grid, 5, 3, exchange=exchange)
    dx, dsh, dsc, *received = pl.pallas_call(
        body, name=name, grid=grid,
        in_specs=[dp_spec, w_spec,
                  pl.BlockSpec((tm, Dm), lambda i, j: (i, 0)),
                  pl.BlockSpec((1, 3, Dm), lambda i, j: (i // tpb, 0, 0)),
                  pl.BlockSpec((tm, Dm), lambda i, j: (i, 0))] + c_in,
        out_specs=[pl.BlockSpec((tm, Dm), lambda i, j: (i, 0)),
                   pl.BlockSpec((1, 1, Dm), lambda i, j: (i // tpb, 0, 0)),
                   pl.BlockSpec((1, 1, Dm), lambda i, j: (i // tpb, 0, 0))] + c_out,
        out_shape=[jax.ShapeDtypeStruct((T, Dm), F32), jax.ShapeDtypeStruct((Bl, 1, Dm), F32),
                   jax.ShapeDtypeStruct((Bl, 1, Dm), F32)] + c_shape,
        scratch_shapes=[pltpu.VMEM((tm, Dm), F32)] + c_scratch, compiler_params=_cp(2),
    )(dp, w, x, mod3, dxres, *exchange)
    return (dx, dsh, dsc, received) if exchange else (dx, dsh, dsc)


def _ln_stats(z):
    mu = jnp.mean(z, axis=-1, keepdims=True)
    zc = z - mu
    var = jnp.mean(zc * zc, axis=-1, keepdims=True)
    rstd = lax.rsqrt(var + LN_EPS)
    return zc * rstd, rstd


def proj_post(a, w, x, mod3, lng, lnb, weight, name):
    nk, T, tk = a.shape
    Dm = w.shape[2]
    Bl = mod3.shape[0]
    tm = _tile(T // Bl, 1024 if nk == 1 else 512)
    tpb = T // Bl // tm

    def body(a_ref, w_ref, x_ref, mod_ref, g_ref, b_ref, out_ref, xn_ref):
        out = _dot(a_ref[0], w_ref[0])
        for k in range(1, nk):
            out = out + _dot(a_ref[k], w_ref[k])
        out_ref[...] = out
        z = ALPHA * x_ref[...] + (weight * (1.0 + mod_ref[0][2:3, :])) * out
        xhat, _ = _ln_stats(z)
        xn_ref[...] = xhat * g_ref[...] + b_ref[...]

    row = pl.BlockSpec((tm, Dm), lambda i: (i, 0))
    vec = pl.BlockSpec((1, Dm), lambda i: (0, 0))
    return pl.pallas_call(
        body, name=name, grid=(T // tm,),
        in_specs=[pl.BlockSpec((nk, tm, tk), lambda i: (0, i, 0)),
                  pl.BlockSpec((nk, tk, Dm), lambda i: (0, 0, 0)),
                  row, pl.BlockSpec((1, 3, Dm), lambda i: (i // tpb, 0, 0)), vec, vec],
        out_specs=[row, row],
        out_shape=[jax.ShapeDtypeStruct((T, Dm), F32), jax.ShapeDtypeStruct((T, Dm), F32)],
        compiler_params=_cp(1),
    )(a, w, x, mod3, lng, lnb)


def post_bwd(dxn, x, out, mod3, lng, w, weight, name, tm=512, gu=None, exchange=()):
    T, Dm = x.shape
    nk, tk, _ = w.shape
    Bl = mod3.shape[0]
    tm = _tile(T // Bl, tm)
    tpb = T // Bl // tm
    fused = gu is not None

    def body(dxn_ref, x_ref, out_ref, mod_ref, g_ref, w_ref, *rest):
        if fused:
            gg_ref, uu_ref = rest[:2]
            rest = rest[2:]
        dxr_ref, dout_ref, da_ref, dg_ref, db_ref, dgate_ref = rest
        i = pl.program_id(0)
        out = out_ref[...]
        dxn = dxn_ref[...]
        coef = weight * (1.0 + mod_ref[0][2:3, :])
        xhat, rstd = _ln_stats(ALPHA * x_ref[...] + coef * out)
        dyh = dxn * g_ref[...]
        dz = rstd * (dyh - jnp.mean(dyh, axis=-1, keepdims=True)
                     - xhat * jnp.mean(dyh * xhat, axis=-1, keepdims=True))
        dxr_ref[...] = ALPHA * dz
        dout = (coef * dz).astype(BF16)
        dout_ref[...] = dout

        @pl.when(i == 0)
        def _():
            dg_ref[...] = jnp.zeros_like(dg_ref)
            db_ref[...] = jnp.zeros_like(db_ref)

        @pl.when(i % tpb == 0)
        def _():
            dgate_ref[...] = jnp.zeros_like(dgate_ref)
        dg_ref[...] += _sum0(dxn * xhat)
        db_ref[...] += _sum0(dxn)
        dgate_ref[0] += _sum0((weight * out) * dz)
        for k in range(nk):
            da = _dot_nt(dout, w_ref[k])
            if fused:
                gg = gg_ref[k].astype(F32)
                s = _sigmoid(gg)
                da_ref[k] = (da * uu_ref[k].astype(F32) * (s * (1.0 + gg * (1.0 - s)))).astype(BF16)
                da_ref[nk + k] = (da * (gg * s)).astype(BF16)
            else:
                da_ref[k] = da.astype(BF16)

    row = pl.BlockSpec((tm, Dm), lambda i: (i, 0))
    vec = pl.BlockSpec((1, Dm), lambda i: (0, 0))
    wide = pl.BlockSpec((nk, tm, tk), lambda i: (0, i, 0))
    nda = 2 * nk if fused else nk
    grid = (T // tm,)
    body, c_in, c_out, c_shape, c_scratch = host_comm(body, grid, 8 if fused else 6, 6, exchange=exchange)
    *results, = pl.pallas_call(
        body, name=name, grid=grid,
        in_specs=[row, row, row, pl.BlockSpec((1, 3, Dm), lambda i: (i // tpb, 0, 0)), vec,
                  pl.BlockSpec((nk, tk, Dm), lambda i: (0, 0, 0))] + ([wide, wide] if fused else []) + c_in,
        out_specs=[row, row, pl.BlockSpec((nda, tm, tk), lambda i: (0, i, 0)),
                   vec, vec, pl.BlockSpec((1, 1, Dm), lambda i: (i // tpb, 0, 0))] + c_out,
        out_shape=[jax.ShapeDtypeStruct((T, Dm), F32), jax.ShapeDtypeStruct((T, Dm), BF16),
                   jax.ShapeDtypeStruct((nda, T, tk), BF16), jax.ShapeDtypeStruct((1, Dm), F32),
                   jax.ShapeDtypeStruct((1, Dm), F32), jax.ShapeDtypeStruct((Bl, 1, Dm), F32)] + c_shape,
        scratch_shapes=c_scratch, compiler_params=_cp(1),
    )(dxn, x, out, mod3, lng, w, *(gu if fused else ()), *exchange)
    return tuple(results[:6]) + ((results[6:],) if exchange else ())


def mm_tn(a, b, name, bw=None, a_copies=False, exchange=()):
    a3, b3 = a.ndim == 3, b.ndim == 3
    nk, T, tk = a.shape if a3 else (1,) + a.shape
    if a_copies:
        nk = 1
    nc, wn = (b.shape[0], b.shape[2]) if b3 else (b.shape[1] // bw, bw)
    tt = _tile(T, 2048)
    nt = T // tt

    def body(a_ref, b_ref, o_ref, ob_ref):
        t = pl.program_id(2)

        @pl.when(t == 0)
        def _():
            o_ref[...] = jnp.zeros_like(o_ref)
        o_ref[...] += _dot_tn(a_ref[...], b_ref[...])

        @pl.when(t == nt - 1)
        def _():
            ob_ref[...] = o_ref[...].astype(BF16)

    a_spec = (pl.BlockSpec((None, tt, tk), lambda k, c, t: (k, t, 0)) if a3
              else pl.BlockSpec((tt, tk), lambda k, c, t: (t, 0)))
    b_spec = (pl.BlockSpec((None, tt, wn), lambda k, c, t: (c, t, 0)) if b3
              else pl.BlockSpec((tt, wn), lambda k, c, t: (t, c)))
    o_spec = pl.BlockSpec((None, tk, wn), lambda k, c, t: (k * nc + c, 0, 0))
    grid = (nk, nc, nt)
    body, c_in, c_out, c_shape, c_scratch = host_comm(body, grid, 2, 2, exchange=exchange)
    o32, o16, *received = pl.pallas_call(
        body, name=name, grid=grid, in_specs=[a_spec, b_spec] + c_in, out_specs=[o_spec, o_spec] + c_out,
        out_shape=[jax.ShapeDtypeStruct((nk * nc, tk, wn), F32), jax.ShapeDtypeStruct((nk * nc, tk, wn), BF16)]
        + c_shape,
        scratch_shapes=c_scratch, compiler_params=_cp(3),
    )(a, b, *exchange)
    return (o32, o16, received) if exchange else (o32, o16)


def ffn_in(x, mod3, w, name, gather=()):
    T, Dm = x.shape
    nj, tf = w.shape[0] // 2, w.shape[2]
    Bl = mod3.shape[0]
    tm = _tile(T // Bl, 1024)
    tpb = T // Bl // tm

    def body(x_ref, mod_ref, wg_ref, wu_ref, a_ref, g_ref, u_ref, h_ref):
        m = mod_ref[0]
        h = (x_ref[...] * (1.0 + m[1:2, :]) + m[0:1, :]).astype(BF16)
        h_ref[...] = h
        g = _dot(h, wg_ref[...])
        u = _dot(h, wu_ref[...])
        a_ref[...] = (g * _sigmoid(g) * u).astype(BF16)
        g_ref[...] = g.astype(BF16)
        u_ref[...] = u.astype(BF16)

    col = pl.BlockSpec((None, tm, tf), lambda j, i: (j, i, 0))
    grid = (nj, T // tm)
    body, c_in, c_out, c_shape, c_scratch = host_comm(body, grid, 4, 4, gather=gather)
    a, g, u, h, *gathered = pl.pallas_call(
        body, name=name, grid=grid,
        in_specs=[pl.BlockSpec((tm, Dm), lambda j, i: (i, 0)),
                  pl.BlockSpec((1, 3, Dm), lambda j, i: (i // tpb, 0, 0)),
                  pl.BlockSpec((None, Dm, tf), lambda j, i: (j, 0, 0)),
                  pl.BlockSpec((None, Dm, tf), lambda j, i: (nj + j, 0, 0))] + c_in,
        out_specs=[col, col, col, pl.BlockSpec((None, tm, Dm), lambda j, i: (j, i, 0))] + c_out,
        out_shape=[jax.ShapeDtypeStruct((nj, T, tf), BF16)] * 3 + [jax.ShapeDtypeStruct((nj, T, Dm), BF16)]
        + c_shape,
        scratch_shapes=c_scratch, compiler_params=_cp(2),
    )(x, mod3, w, w, *gather)
    return a, g, u, h, gathered


def loss_head(y, tgt, name):
    T, Dm = y.shape
    tm = _tile(T, 512)
    nt = T // tm

    def body(y_ref, t_ref, dy_ref, l_ref, acc):
        i = pl.program_id(0)

        @pl.when(i == 0)
        def _():
            acc[...] = jnp.zeros_like(acc)
        e = y_ref[...] - t_ref[...]
        dy_ref[...] = e * (1.0 / Dm)
        acc[...] += _sum0(e * e)

        @pl.when(i == nt - 1)
        def _():
            l_ref[...] = jnp.broadcast_to(_sum1(acc[...]) * (0.5 / Dm), l_ref.shape)

    return pl.pallas_call(
        body, name=name, grid=(nt,),
        in_specs=[pl.BlockSpec((tm, Dm), lambda i: (i, 0)), pl.BlockSpec((tm, Dm), lambda i: (i, 0))],
        out_specs=[pl.BlockSpec((tm, Dm), lambda i: (i, 0)), pl.BlockSpec((1, 128), lambda i: (0, 0))],
        out_shape=[jax.ShapeDtypeStruct((T, Dm), F32), jax.ShapeDtypeStruct((1, 128), F32)],
        scratch_shapes=[pltpu.VMEM((1, Dm), F32)], compiler_params=_cp(1),
    )(y, tgt)


def adamw(w, g, m, v, name):
    R, C = w.shape
    tr = _tile(R, 512) if R % 8 == 0 else R

    def body(w_ref, g_ref, m_ref, v_ref, d_ref, nm_ref, nv_ref):
        gg = g_ref[...]
        mm = ADAM_B1 * m_ref[...] + (1.0 - ADAM_B1) * gg
        vv = ADAM_B2 * v_ref[...] + (1.0 - ADAM_B2) * (gg * gg)
        m_hat = mm / (1.0 - ADAM_B1 ** ADAM_STEP)
        v_hat = vv / (1.0 - ADAM_B2 ** ADAM_STEP)
        d_ref[...] = -ADAM_LR * (m_hat / (jnp.sqrt(v_hat) + ADAM_EPS) + ADAM_WD * w_ref[...])
        nm_ref[...] = mm
        nv_ref[...] = vv

    spec = pl.BlockSpec((tr, C), lambda i: (i, 0))
    return pl.pallas_call(
        body, name=name, grid=(R // tr,), in_specs=[spec] * 4, out_specs=[spec] * 3,
        out_shape=[jax.ShapeDtypeStruct((R, C), F32)] * 3, compiler_params=_cp(1),
    )(w, g, m, v)


def ada_fwd(c_all, ada_w, ada_b_cols, name):
    Lr, Dm, Nc = ada_w.shape
    Bg = c_all.shape[0]

    def body(c_ref, w_ref, b_ref, o_ref):
        cc = c_ref[...]
        cond = cc * _sigmoid(cc)
        o_ref[0] = _dot(cond.astype(BF16), w_ref[0].astype(BF16)) + b_ref[0]

    return pl.pallas_call(
        body, name=name, grid=(Lr,),
        in_specs=[pl.BlockSpec((Bg, Dm), lambda l: (0, 0)),
                  pl.BlockSpec((1, Dm, Nc), lambda l: (l, 0, 0)),
                  pl.BlockSpec((1, 1, Nc), lambda l: (l, 0, 0))],
        out_specs=pl.BlockSpec((1, Bg, Nc), lambda l: (l, 0, 0)),
        out_shape=jax.ShapeDtypeStruct((Lr, Bg, Nc), F32), compiler_params=_cp(1),
    )(c_all, ada_w, ada_b_cols)


def ada_bwd(c_all_t, dmod_cols, dmod_all, name):
    Dm, Bg = c_all_t.shape
    Lr, _, Nc = dmod_cols.shape
    Nf = dmod_all.shape[2]

    def body(c_ref, dm_ref, da_ref, gw_ref, gb_ref):
        cc = c_ref[...]
        cond = cc * _sigmoid(cc)
        gw_ref[0] = _dot(cond.astype(BF16), dm_ref[0].astype(BF16))
        gb_ref[0] = _sum0(da_ref[0])

    return pl.pallas_call(
        body, name=name, grid=(Lr,),
        in_specs=[pl.BlockSpec((Dm, Bg), lambda l: (0, 0)),
                  pl.BlockSpec((1, Bg, Nc), lambda l: (l, 0, 0)),
                  pl.BlockSpec((1, Bg, Nf), lambda l: (l, 0, 0))],
        out_specs=[pl.BlockSpec((1, Dm, Nc), lambda l: (l, 0, 0)),
                   pl.BlockSpec((1, 1, Nf), lambda l: (l, 0, 0))],
        out_shape=[jax.ShapeDtypeStruct((Lr, Dm, Nc), F32), jax.ShapeDtypeStruct((Lr, 1, Nf), F32)],
        compiler_params=_cp(1),
    )(c_all_t, dmod_cols, dmod_all)


def _conv_taps(x, w, rows):
    shifted = [x]
    c = w[3:4, :] * x
    for k in range(1, 4):
        xs = jnp.where(rows >= k, pltpu.roll(x, k, 0), 0.0)
        shifted.append(xs)
        c = c + w[3 - k:4 - k, :] * xs
    return c, shifted


def conv_silu(proj3, conv_w, name):
    Bl, S, _ = proj3.shape
    ncb = conv_w.shape[1] // 128

    def body(x_ref, w_ref, o_ref):
        rows = lax.broadcasted_iota(jnp.int32, (S, 128), 0)
        c, _ = _conv_taps(_round(x_ref[0]), _round(w_ref[...]), rows)
        o_ref[0] = c * _sigmoid(c)

    return pl.pallas_call(
        body, name=name, grid=(Bl, ncb),
        in_specs=[pl.BlockSpec((1, S, 128), lambda b, j: (b, 0, j)),
                  pl.BlockSpec((4, 128), lambda b, j: (0, j))],
        out_specs=pl.BlockSpec((1, S, 128), lambda b, j: (b, 0, j)),
        out_shape=jax.ShapeDtypeStruct((Bl, S, conv_w.shape[1]), F32), compiler_params=_cp(2),
    )(proj3, conv_w)


def conv_silu_bwd(proj3, conv_w, dq, dk, name):
    Bl, S, _ = proj3.shape
    nq = dq.shape[2] // 128

    def body(x_ref, w_ref, dq_ref, dk_ref, dx_ref, dw_ref):
        j = pl.program_id(1)
        rows = lax.broadcasted_iota(jnp.int32, (S, 128), 0)
        w = _round(w_ref[...])
        c, shifted = _conv_taps(_round(x_ref[0]), w, rows)
        s = _sigmoid(c)
        dact = jnp.where(j < nq, dq_ref[0], dk_ref[0])
        dc = _round(dact * (s * (1.0 + c * (1.0 - s))))
        dx = w[3:4, :] * dc
        dws = [_sum0(dc * shifted[0])]
        for k in range(1, 4):
            up = jnp.where(rows < S - k, pltpu.roll(dc, S - k, 0), 0.0)
            dx = dx + w[3 - k:4 - k, :] * up
            dws.append(_sum0(dc * shifted[k]))
        dx_ref[0] = dx.astype(BF16)
        tap = lax.broadcasted_iota(jnp.int32, (4, 128), 0)
        dw_ref[0] = functools.reduce(lambda a, b: a + b, [jnp.where(tap == 3 - k, dws[k], 0.0) for k in range(4)])

    return pl.pallas_call(
        body, name=name, grid=(Bl, 2 * nq),
        in_specs=[pl.BlockSpec((1, S, 128), lambda b, j: (b, 0, j)),
                  pl.BlockSpec((4, 128), lambda b, j: (0, j)),
                  pl.BlockSpec((1, S, 128), lambda b, j: (b, 0, jnp.minimum(j, nq - 1))),
                  pl.BlockSpec((1, S, 128), lambda b, j: (b, 0, jnp.maximum(j - nq, 0)))],
        out_specs=[pl.BlockSpec((1, S, 128), lambda b, j: (b, 0, j)),
                   pl.BlockSpec((1, 4, 128), lambda b, j: (b, 0, j))],
        out_shape=[jax.ShapeDtypeStruct((Bl, S, 2 * nq * 128), BF16),
                   jax.ShapeDtypeStruct((Bl, 4, 2 * nq * 128), F32)],
        compiler_params=_cp(2),
    )(proj3, conv_w, dq, dk)


def _log_sigmoid(a):
    return jnp.minimum(a, 0.0) - jnp.log(1.0 + jnp.exp(-jnp.abs(a)))


def _interleave(gens):
    live = list(gens)
    while live:
        still = []
        for g in live:
            try:
                next(g)
                still.append(g)
            except StopIteration:
                pass
        live = still


def _finish(gen):
    while True:
        try:
            next(gen)
        except StopIteration as done:
            return done.value


def _chunk_state(kc, vc, gi, bcum, b_last, C, n, m):
    a = b_last - bcum + gi
    m_loc = jnp.max(a, axis=0, keepdims=True)
    wa = jnp.exp(a - m_loc)
    c_loc = _dot_tn((wa * vc).astype(BF16), kc.astype(BF16))
    n_loc = _sum0(_round(wa) * _round(kc))
    m_new = jnp.maximum(b_last + m, m_loc)
    sp = jnp.exp(b_last + m - m_new)
    sl = jnp.exp(m_loc - m_new)
    yield
    return sp * C + sl * c_loc, sp * n + sl * n_loc, m_new, wa, sp, sl


def _chunk_out(qs, kc, vc, gi_row, bcum, bcum_row, low, C, n, m):
    inter_log = bcum + m
    dlog = jnp.where(low, bcum - bcum_row + gi_row, NEG)
    m_i = jnp.maximum(inter_log, jnp.max(dlog, axis=1, keepdims=True))
    dm = jnp.exp(dlog - m_i)
    iw = jnp.exp(inter_log - m_i)
    qs_b, k_b, v_b = qs.astype(BF16), kc.astype(BF16), vc.astype(BF16)
    sqk = _dot_nt(qs_b, k_b)
    qc_ = _dot_nt(qs_b, C.astype(BF16))
    qn = _sum1(_round(qs) * _round(n))
    floor = jnp.exp(-m_i)
    yield
    sc = sqk * dm
    sv = _dot(sc.astype(BF16), v_b)
    den = _sum1(sc) + iw * qn
    dn = jnp.maximum(jnp.abs(den), floor)
    yield
    num = sv + iw * qc_
    return dict(hc=num / dn, den=den, dn=dn, floor=floor, sc=sc, dm=dm, iw=iw, qc=qc_, qn=qn,
                qs_b=qs_b, k_b=k_b, v_b=v_b)


def _cell_consts(L):
    ri = lax.broadcasted_iota(jnp.int32, (L, L), 0)
    ci = lax.broadcasted_iota(jnp.int32, (L, L), 1)
    return ri == ci, ci <= ri, ri <= ci


def _load_chunk(q_ref, k_ref, v_ref, G, off, L, h, lane):
    hh = h % 2
    qmask = (lane >= M_DQK * hh) & (lane < M_DQK * (hh + 1))
    pair = pl.ds(128 * (h // 2), 128)
    qc = jnp.where(qmask, q_ref[0, pl.ds(off, L), pair], 0.0)
    kc = jnp.where(qmask, k_ref[0, pl.ds(off, L), pair], 0.0)
    vc = v_ref[0, pl.ds(off, L), pl.ds(M_DV * h, M_DV)]
    gi = _sum1(jnp.where(lane == h, G, 0.0))
    gf = _sum1(jnp.where(lane == h + HEADS, G, 0.0))
    return qmask, qc, kc, vc, gi, gf


def _gate_rows(gi, gf, eye, low, upp):
    lf = _log_sigmoid(gf)
    lf_row = _sum0(jnp.where(eye, lf, 0.0))
    gi_row = _sum0(jnp.where(eye, gi, 0.0))
    bcum = _sum1(jnp.where(low, lf_row, 0.0))
    bcum_row = _sum0(jnp.where(upp, lf, 0.0))
    b_last = _sum0(lf)
    return gi_row, bcum, bcum_row, b_last


def _cell_specs(SB, cpb, blk):
    def seq(width, col):
        return pl.BlockSpec((1, SB, width), lambda b, s: (b, blk(s), col))

    def state(rows):
        return pl.BlockSpec((1, HEADS, cpb, rows, 128), lambda b, s: (b, 0, blk(s), 0, 0))

    ins = [seq(D // 2, 0), seq(D // 2, 1), seq(D, 1), seq(D, 2), seq(128, 3 * D // 128),
           pl.BlockSpec((1, D), lambda b, s: (0, 0)), pl.BlockSpec((1, 128), lambda b, s: (0, 0))]
    return ins, [state(M_DV), state(1), state(1)], seq


def mlstm_cell_fwd(qk3, proj3, gain, gbias, name, gather=()):
    Bl, S, _ = qk3.shape
    L = M_CHUNK
    SB = min(M_SLAB, S)
    cpb, nc, nsb = SB // L, S // L, S // SB
    scale = M_DQK ** -0.5

    def body(q_ref, k_ref, v_ref, o_ref, g_ref, gain_ref, gb_ref, y_ref, cst_ref, nst_ref, mst_ref, *state):
        C_s, n_s, m_s = state[:HEADS], state[HEADS:2 * HEADS], state[2 * HEADS:]

        @pl.when(pl.program_id(1) == 0)
        def _():
            for ref in state:
                ref[...] = jnp.zeros_like(ref)
        lane = lax.broadcasted_iota(jnp.int32, (L, 128), 1)
        eye, low, upp = _cell_consts(L)

        def step(c, carry):
            off = pl.multiple_of(c * L, L)
            G = g_ref[0, pl.ds(off, L), :] + gb_ref[...]

            def head(h):
                C, n, mb = C_s[h][...], n_s[h][...], m_s[h][...]
                cst_ref[0, h, c] = C
                nst_ref[0, h, c] = n
                mst_ref[0, h, c] = mb
                m = mb[:, 0:1]
                _, qc, kc, vc, gi, gf = _load_chunk(q_ref, k_ref, v_ref, G, off, L, h, lane)
                gi_row, bcum, bcum_row, b_last = _gate_rows(gi, gf, eye, low, upp)
                state = _chunk_state(kc, vc, gi, bcum, b_last, C, n, m)
                next(state)
                r = yield from _chunk_out(qc * scale, kc, vc, gi_row, bcum, bcum_row, low, C, n, m)
                hc = r["hc"]
                hn = hc * lax.rsqrt(jnp.mean(hc * hc, axis=-1, keepdims=True) + RMS_EPS)
                cols = pl.ds(M_DV * h, M_DV)
                oc = o_ref[0, pl.ds(off, L), cols]
                y_ref[0, pl.ds(off, L), cols] = (_sigmoid(oc) * hn * gain_ref[:, cols]).astype(BF16)
                C2, n2, m2, _, _, _ = _finish(state)
                C_s[h][...] = C2
                n_s[h][...] = n2
                m_s[h][...] = jnp.broadcast_to(m2, (1, 128))

            _interleave(head(h) for h in range(HEADS))
            return carry

        lax.fori_loop(0, cpb, step, 0)

    ins, states, seq = _cell_specs(SB, cpb, lambda s: s)
    grid = (Bl, nsb)
    body, c_in, c_out, c_shape, c_scratch = host_comm(body, grid, 7, 4, gather=gather)
    return pl.pallas_call(
        body, name=name, grid=grid, in_specs=ins + c_in, out_specs=[seq(D, 0)] + states + c_out,
        out_shape=[jax.ShapeDtypeStruct((Bl, S, D), BF16),
                   jax.ShapeDtypeStruct((Bl, HEADS, nc, M_DV, 128), F32),
                   jax.ShapeDtypeStruct((Bl, HEADS, nc, 1, 128), F32),
                   jax.ShapeDtypeStruct((Bl, HEADS, nc, 1, 128), F32)] + c_shape,
        scratch_shapes=[pltpu.VMEM((M_DV, 128), F32)] * HEADS + [pltpu.VMEM((1, 128), F32)] * (2 * HEADS) + c_scratch,
        compiler_params=_cp(2),
    )(qk3, qk3, proj3, proj3, proj3, gain, gbias, *gather)


def mlstm_cell_bwd(qk3, proj3, gain, gbias, dy3, states, name, exchange=()):
    Bl, S, _ = qk3.shape
    L = M_CHUNK
    SB = min(M_SLAB, S)
    cpb, nsb = SB // L, S // SB
    scale = M_DQK ** -0.5

    def body(q_ref, k_ref, v_ref, o_ref, g_ref, gain_ref, gb_ref, cst_ref, nst_ref, mst_ref, dy_ref,
             dq_ref, dk_ref, dv_ref, do_ref, dg_ref, dgain_ref, dgb_ref, *state):
        dC_s, dn_s, dgain_s, dgb_s = state[:HEADS], state[HEADS:2 * HEADS], state[2 * HEADS:3 * HEADS], state[-1]
        s = pl.program_id(1)

        @pl.when(s == 0)
        def _():
            for ref in state:
                ref[...] = jnp.zeros_like(ref)
        lane = lax.broadcasted_iota(jnp.int32, (L, 128), 1)
        rowi = lax.broadcasted_iota(jnp.int32, (L, 1), 0)
        eye, low, upp = _cell_consts(L)

        def bstep(t, carry):
            c = cpb - 1 - t
            off = pl.multiple_of(c * L, L)
            G = g_ref[0, pl.ds(off, L), :] + gb_ref[...]
            shared = dict(slab=jnp.zeros((L, 128), F32))

            def head(h):
                cols = pl.ds(M_DV * h, M_DV)
                gain_h = gain_ref[:, cols]
                C, n, m = cst_ref[0, h, c], nst_ref[0, h, c], mst_ref[0, h, c][:, 0:1]
                dC_n, dn_n = dC_s[h][...], dn_s[h][...]
                qmask, qc, kc, vc, gi, gf = _load_chunk(q_ref, k_ref, v_ref, G, off, L, h, lane)
                gi_row, bcum, bcum_row, b_last = _gate_rows(gi, gf, eye, low, upp)
                qs = qc * scale
                _, _, _, wa, sp, sl = _finish(_chunk_state(kc, vc, gi, bcum, b_last, C, n, m))
                dcl_b = (sl * dC_n).astype(BF16)
                t1_mm = _dot(vc.astype(BF16), dcl_b)
                dv_mm = _dot_nt(kc.astype(BF16), dcl_b)
                r = yield from _chunk_out(qs, kc, vc, gi_row, bcum, bcum_row, low, C, n, m)
                hc, den, dn, sc, dm, iw, qn = r["hc"], r["den"], r["dn"], r["sc"], r["dm"], r["iw"], r["qn"]
                qs_b, k_b, v_b = r["qs_b"], r["k_b"], r["v_b"]
                dy = dy_ref[0, pl.ds(off, L), cols].astype(F32)
                oc = o_ref[0, pl.ds(off, L), cols]
                sig_o = _sigmoid(oc)
                rr = lax.rsqrt(jnp.mean(hc * hc, axis=-1, keepdims=True) + RMS_EPS)
                hn = hc * rr
                dgain_s[h][...] += _sum0(dy * sig_o * hn)
                do_ref[0, pl.ds(off, L), cols] = (
                    dy * hn * gain_h * sig_o * (1.0 - sig_o)).astype(BF16)
                dhn = dy * sig_o * gain_h
                dhc = rr * dhn - hc * (rr * rr * rr) * jnp.mean(dhn * hc, axis=-1, keepdims=True)
                dnum = dhc / dn
                gden = -_sum1(dhc * hc) / dn
                dden = jnp.where(jnp.abs(den) > r["floor"], gden * jnp.sign(den), 0.0)
                dnum_b = dnum.astype(BF16)
                dqc_b = (iw * dnum).astype(BF16)
                dsc_mm = _dot_nt(dnum_b, v_b)
                dv = _dot_tn(sc.astype(BF16), dnum_b)
                dqs_mm = _dot(dqc_b, C.astype(BF16))
                dC_out = _dot_tn(dqc_b, qs_b)
                diw = _sum1(dnum * r["qc"]) + dden * qn
                wq = iw * dden
                dn_out = _sum0(wq * qs)
                dn_loc = sl * dn_n
                dsp = _sum1(_sum0(dC_n * C)) + _sum1(dn_n * n)
                yield
                dsc = dsc_mm + dden
                dS_b = (dsc * dm).astype(BF16)
                gm = dsc * sc
                dqs2_mm = _dot(dS_b, k_b)
                dk = _dot_tn(dS_b, qs_b)
                dqs = dqs_mm + wq * n
                dbc = _sum1(gm) + diw * iw
                colg = _sum0(gm)
                dC_p = sp * dC_n + dC_out
                dn_p = sp * dn_n + dn_out
                db_last = dsp * sp
                t1 = t1_mm + dn_loc
                dwa = _sum1(t1 * kc)
                dv = dv + wa * dv_mm
                yield
                dqs = dqs + dqs2_mm
                dk = dk + wa * t1
                da = dwa * wa
                db_last = db_last + _sum0(da)
                dbc = dbc - da + jnp.where(rowi == L - 1, db_last, 0.0)
                dbc_row = _sum0(jnp.where(eye, dbc, 0.0)) - colg
                dgi = da + _sum1(jnp.where(eye, colg, 0.0))
                dlf = _sum1(jnp.where(upp, dbc_row, 0.0))
                dgf = dlf * _sigmoid(-gf)
                dq = jnp.where(qmask, dqs * scale, 0.0)
                dk = jnp.where(qmask, dk, 0.0)
                shared["slab"] = (shared["slab"] + jnp.where(lane == h, dgi, 0.0)
                                  + jnp.where(lane == h + HEADS, dgf, 0.0))
                dv_ref[0, pl.ds(off, L), cols] = dv.astype(BF16)
                dC_s[h][...] = dC_p
                dn_s[h][...] = dn_p
                if h % 2 == 0:
                    shared["dq"], shared["dk"] = dq, dk
                else:
                    pair = pl.ds(128 * (h // 2), 128)
                    dq_ref[0, pl.ds(off, L), pair] = shared["dq"] + dq
                    dk_ref[0, pl.ds(off, L), pair] = shared["dk"] + dk

            _interleave(head(h) for h in range(HEADS))
            dg_ref[0, pl.ds(off, L), :] = shared["slab"]
            dgb_s[...] += _sum0(shared["slab"])
            return carry

        lax.fori_loop(0, cpb, bstep, 0)

        @pl.when(s == nsb - 1)
        def _():
            for h in range(HEADS):
                dgain_ref[0, :, pl.ds(M_DV * h, M_DV)] = dgain_s[h][...]
            dgb_ref[0] = dgb_s[...]

    ins, states_specs, seq = _cell_specs(SB, cpb, lambda s: nsb - 1 - s)
    once = lambda width: pl.BlockSpec((1, 1, width), lambda b, s: (b, 0, 0))
    grid = (Bl, nsb)
    body, c_in, c_out, c_shape, c_scratch = host_comm(body, grid, 11, 7, exchange=exchange)
    return pl.pallas_call(
        body, name=name, grid=grid, in_specs=ins + states_specs + [seq(D, 0)] + c_in,
        out_specs=[seq(D // 2, 0), seq(D // 2, 0), seq(D, 0), seq(D, 0), seq(128, 0), once(D), once(128)] + c_out,
        out_shape=[jax.ShapeDtypeStruct((Bl, S, D // 2), F32), jax.ShapeDtypeStruct((Bl, S, D // 2), F32),
                   jax.ShapeDtypeStruct((Bl, S, D), BF16), jax.ShapeDtypeStruct((Bl, S, D), BF16),
                   jax.ShapeDtypeStruct((Bl, S, 128), F32), jax.ShapeDtypeStruct((Bl, 1, D), F32),
                   jax.ShapeDtypeStruct((Bl, 1, 128), F32)] + c_shape,
        scratch_shapes=[pltpu.VMEM((M_DV, 128), F32)] * HEADS + [pltpu.VMEM((1, 128), F32)] * (2 * HEADS + 1)
        + c_scratch,
        compiler_params=_cp(2),
    )(qk3, qk3, proj3, proj3, proj3, gain, gbias, *states, dy3, *exchange)


def _attn_scores(q, kc, kp, n, row, col, scale):
    s_c = jnp.where(col <= row, _dot_nt(q, kc) * scale, NEG)
    s_p = jnp.where(jnp.logical_and(col >= row, n > 0), _dot_nt(q, kp) * scale, NEG)
    return s_c, s_p


def _to_streams(src, dst, tmp, dil, Sd):
    if dil == 1:
        dst[...] = src[...].astype(dst.dtype)
        return
    if src.dtype != F32:
        tmp[...] = src[...].astype(F32)
        src = tmp
    for r in range(dil):
        dst[pl.ds(r * Sd, Sd), :] = src[pl.ds(r, Sd, stride=dil), :].astype(dst.dtype)


def _from_streams(src, dst, dil, Sd):
    if dil == 1:
        dst[...] = src[...]
        return
    for r in range(dil):
        dst[pl.ds(r, Sd, stride=dil), :] = src[pl.ds(r * Sd, Sd), :]


def attn_fwd(proj, Bl, S, name):
    scale = A_BLK ** -0.5
    pv = proj.reshape(Bl, S, A_PROJ)
    ng = len(DIL_GROUPS)
    rows = 512

    def body(*refs):
        ins, (ob_ref, of_ref, lt_ref) = refs[:3 * ng], refs[3 * ng:3 * ng + 3]
        tmp, qs, ks, vs, os_, ls = refs[3 * ng + 3:3 * ng + 9]
        o_nat, l_nat = refs[3 * ng + 9:4 * ng + 9], refs[4 * ng + 9:]
        row = lax.broadcasted_iota(jnp.int32, (A_BLK, A_BLK), 0)
        col = lax.broadcasted_iota(jnp.int32, (A_BLK, A_BLK), 1)
        for g, (_, dil) in enumerate(DIL_GROUPS):
            Sd = S // dil
            nb = Sd // A_BLK
            for src, dst in zip(ins[3 * g:3 * g + 3], (qs, ks, vs)):
                _to_streams(src.at[0], dst, tmp, dil, Sd)

            def step(i, carry, nb=nb):
                n = i % nb
                off = pl.multiple_of(i * A_BLK, A_BLK)
                offp = pl.multiple_of(jnp.maximum(i - 1, 0) * A_BLK, A_BLK)
                q = qs[pl.ds(off, A_BLK), :]
                s_c, s_p = _attn_scores(q, ks[pl.ds(off, A_BLK), :], ks[pl.ds(offp, A_BLK), :], n, row, col, scale)
                m = jnp.maximum(jnp.max(s_c, axis=1, keepdims=True), jnp.max(s_p, axis=1, keepdims=True))
                p_c = jnp.exp(s_c - m)
                p_p = jnp.exp(s_p - m)
                den = _sum1(p_c) + _sum1(p_p)
                o = (_dot(p_c.astype(BF16), vs[pl.ds(off, A_BLK), :])
                     + _dot(p_p.astype(BF16), vs[pl.ds(offp, A_BLK), :]))
                os_[pl.ds(off, A_BLK), :] = o / den
                ls[pl.ds(off, A_BLK), :] = jnp.broadcast_to(m + jnp.log(den), (A_BLK, 128))
                return carry

            lax.fori_loop(0, dil * nb, step, 0, unroll=A_UNROLL)
            _from_streams(os_, o_nat[g], dil, Sd)
            _from_streams(ls, l_nat[g], dil, Sd)

        def merge(t, carry):
            sl = pl.ds(pl.multiple_of(t * rows, rows), rows)
            lses = [l[sl, :] for l in l_nat]
            m = functools.reduce(jnp.maximum, lses)
            ws = [jnp.exp(l - m) for l in lses]
            den = functools.reduce(lambda a, b: a + b, ws)
            o = functools.reduce(lambda a, b: a + b, [w * r[sl, :] for w, r in zip(ws, o_nat)]) / den
            of_ref[0, sl, :] = o
            ob_ref[0, sl, :] = o.astype(BF16)
            lt_ref[0, sl, :] = m + jnp.log(den)
            return carry

        lax.fori_loop(0, S // rows, merge, 0)

    in_specs = [pl.BlockSpec((1, S, 128), lambda b, h, c=g * 24 + j * HEADS: (b, 0, c + h))
                for g in range(ng) for j in range(3)]
    ospec = pl.BlockSpec((1, S, 128), lambda b, h: (b, 0, h))
    slab = lambda dt: pltpu.VMEM((S, 128), dt)
    outs = pl.pallas_call(
        body, name=name, grid=(Bl, HEADS), in_specs=in_specs, out_specs=[ospec] * 3,
        out_shape=[jax.ShapeDtypeStruct((Bl, S, D), BF16), jax.ShapeDtypeStruct((Bl, S, D), F32),
                   jax.ShapeDtypeStruct((Bl, S, D), F32)],
        scratch_shapes=[slab(F32)] + [slab(BF16)] * 3 + [slab(F32)] * (2 + 2 * ng),
        compiler_params=_cp(2),
    )(*([pv] * (3 * ng)))
    return [t.reshape(Bl * S, D) for t in outs]


def attn_bwd(proj, do, o, lse, Bl, S, g, dil, name):
    Sd = S // dil
    nb = Sd // A_BLK
    scale = A_BLK ** -0.5
    pv = proj.reshape(Bl, S, A_PROJ)
    dov, ov, lv = (t.reshape(Bl, S, D) for t in (do, o, lse))

    def body(q_ref, k_ref, v_ref, do_ref, o_ref, l_ref, dq_ref, dk_ref, dv_ref,
             tmp, qs, ks, vs, dos, dls, lts, dq_s, dk_s, dv_s):
        row = lax.broadcasted_iota(jnp.int32, (A_BLK, A_BLK), 0)
        col = lax.broadcasted_iota(jnp.int32, (A_BLK, A_BLK), 1)
        for src, dst in ((q_ref, qs), (k_ref, ks), (v_ref, vs), (do_ref, dos), (l_ref, lts)):
            _to_streams(src.at[0], dst, tmp, dil, Sd)
        tmp[...] = jnp.broadcast_to(_sum1(do_ref[0].astype(F32) * o_ref[0]), (S, 128))
        _to_streams(tmp, dls, None, dil, Sd)
        dk_s[...] = jnp.zeros_like(dk_s)
        dv_s[...] = jnp.zeros_like(dv_s)

        def step(i, carry):
            n = i % nb
            off = pl.multiple_of(i * A_BLK, A_BLK)
            offp = pl.multiple_of(jnp.maximum(i - 1, 0) * A_BLK, A_BLK)
            q = qs[pl.ds(off, A_BLK), :]
            kc, kp = ks[pl.ds(off, A_BLK), :], ks[pl.ds(offp, A_BLK), :]
            vc, vp = vs[pl.ds(off, A_BLK), :], vs[pl.ds(offp, A_BLK), :]
            do_b = dos[pl.ds(off, A_BLK), :]
            delta = dls[pl.ds(off, A_BLK), :][:, 0:1]
            lt = lts[pl.ds(off, A_BLK), :][:, 0:1]
            s_c, s_p = _attn_scores(q, kc, kp, n, row, col, scale)
            p_c = jnp.exp(s_c - lt)
            p_p = jnp.exp(s_p - lt)
            ds_c = (p_c * (_dot_nt(do_b, vc) - delta) * scale).astype(BF16)
            ds_p = (p_p * (_dot_nt(do_b, vp) - delta) * scale).astype(BF16)
            dq_s[pl.ds(off, A_BLK), :] = _dot(ds_c, kc) + _dot(ds_p, kp)
            dk_s[pl.ds(off, A_BLK), :] += _dot_tn(ds_c, q)
            dk_s[pl.ds(offp, A_BLK), :] += _dot_tn(ds_p, q)
            dv_s[pl.ds(off, A_BLK), :] += _dot_tn(p_c.astype(BF16), do_b)
            dv_s[pl.ds(offp, A_BLK), :] += _dot_tn(p_p.astype(BF16), do_b)
            return carry

        lax.fori_loop(0, dil * nb, step, 0, unroll=A_UNROLL)
        for src, dst in ((dq_s, dq_ref), (dk_s, dk_ref), (dv_s, dv_ref)):
            _from_streams(src, tmp, dil, Sd)
            dst[0] = tmp[...].astype(BF16)

    def spec(j):
        return pl.BlockSpec((1, S, 128), lambda b, h: (b, 0, g * 24 + j * HEADS + h))

    ospec = pl.BlockSpec((1, S, 128), lambda b, h: (b, 0, h))
    slab = lambda dt: pltpu.VMEM((S, 128), dt)
    outs = pl.pallas_call(
        body, name=name, grid=(Bl, HEADS),
        in_specs=[spec(0), spec(1), spec(2), ospec, ospec, ospec], out_specs=[ospec] * 3,
        out_shape=[jax.ShapeDtypeStruct((Bl, S, D), BF16)] * 3,
        scratch_shapes=[slab(F32)] + [slab(BF16)] * 4 + [slab(F32)] * 5,
        compiler_params=_cp(2),
    )(pv, pv, pv, dov, ov, lv)
    return [t.reshape(Bl * S, D) for t in outs]


def _as_slots(pair, shape):
    return tuple(t.reshape(shape) for t in pair)


def ffn_fwd(x, mod3, w_in, w_out, lng, lnb, tag, gather=()):
    a, g, u, h, gathered = ffn_in(x, mod3, w_in, tag + "_in", gather=gather)
    out, xn = proj_post(a, w_out, x, mod3, lng, lnb, 0.5, tag + "_out")
    return xn, (x, out, g, u, h, a), gathered


def ffn_bwd(dxn, saved, mod3, w_in, w_out, lng, tag, exchange=((), (), ()), exchange_own=False):
    x, out, g, u, h, a = saved
    dxres, dout, dgu, dlg, dlb, dgate, *got0 = post_bwd(dxn, x, out, mod3, lng, w_out, 0.5, tag + "_outb",
                                                        tm=256, gu=(g, u), exchange=exchange[0])
    *dw_in, got1 = mm_tn(h, dgu, tag + "_dwin", a_copies=True, exchange=exchange[1]) + (() if exchange[1] else ([],))
    *dw_out, got2 = mm_tn(a, dout, tag + "_dwout", bw=D, exchange=exchange[2]) + (() if exchange[2] else ([],))
    dw_out = _as_slots(dw_out, (N_DEV, D_FF // N_DEV, D))
    dx, dsh, dsc, *own = modmm_bwd(dgu, w_in, x, mod3, dxres, tag + "_inb", tm=256,
                                   exchange=[dw_in[1], dw_out[1]] if exchange_own else ())
    dmod3 = jnp.concatenate([dsh, dsc, dgate], axis=1)
    return (dx, [tuple(dw_in), dw_out], dlg, dlb, dmod3,
            [got0[0] if got0 else [], got1, got2], (own[0] if own else []))


def mlstm_fwd(x, mod3, w_in, w_out, conv_w, gain, gbias, lng, lnb, Bl, S, gather=()):
    proj, h = modmm(x, mod3, w_in, F32, "ml_in", tn=M_PROJ_PAD // 5)
    proj3 = proj.reshape(Bl, S, M_PROJ_PAD)
    qk3 = conv_silu(proj3, conv_w, "ml_conv")
    y3, *rest = mlstm_cell_fwd(qk3, proj3, gain, gbias, "ml_cell", gather=gather)
    states, gathered = rest[:3], rest[3:]
    y = y3.reshape(Bl * S, D)
    out, xn = proj_post(y[None], w_out, x, mod3, lng, lnb, 1.0, "ml_out")
    return xn, (x, out, h, proj3, qk3, y, states), gathered


def mlstm_bwd(dxn, saved, mod3, w_in, w_out, conv_w, gain, gbias, lng, Bl, S, exchange=()):
    x, out, h, proj3, qk3, y, states = saved
    dxres, dout, dy, dlg, dlb, dgate = post_bwd(dxn, x, out, mod3, lng, w_out, 1.0, "ml_outb")
    dq, dk, dv, do, dg, dgain, dgb, *received = mlstm_cell_bwd(qk3, proj3, gain, gbias, dy.reshape(Bl, S, D),
                                                               states, "ml_cellb", exchange=exchange)
    dqk, dconv = conv_silu_bwd(proj3, conv_w, dq, dk, "ml_convb")
    dproj = jnp.concatenate([dqk, dv, do, dg.astype(BF16)], axis=2).reshape(Bl * S, M_PROJ_PAD)
    dx, dsh, dsc = modmm_bwd(dproj, w_in, x, mod3, dxres, "ml_inb", tn=M_PROJ_PAD // 5)
    dwi, _ = mm_tn(h, dproj, "ml_dwin", bw=M_PROJ_PAD // 5)
    dwi = _restack(jnp.moveaxis(dwi, 0, 1).reshape(D, M_PROJ_PAD)[:, :M_PROJ], 1)
    dw_out = _as_slots(mm_tn(y, dout, "ml_dwout", bw=D), (N_DEV, D // N_DEV, D))
    small = (jnp.sum(dconv, axis=0), jnp.sum(dgain, axis=0), jnp.sum(dgb, axis=0)[:, :2 * HEADS])
    dmod3 = jnp.concatenate([dsh, dsc, dgate], axis=1)
    return dx, [(dwi, dwi.astype(BF16)), dw_out], dlg, dlb, dmod3, small, received


def attn_mixer_fwd(x, mod3, w_in, w_out, lng, lnb, Bl, S):
    proj, h = modmm(x, mod3, w_in, BF16, "at_in")
    ob, of, lt = attn_fwd(proj, Bl, S, "at_core")
    out, xn = proj_post(ob[None], w_out, x, mod3, lng, lnb, 1.0, "at_out")
    return xn, (x, out, h, proj, ob, of, lt)


def attn_mixer_bwd(dxn, saved, mod3, w_in, w_out, lng, Bl, S):
    x, out, h, proj, ob, of, lt = saved
    dxres, dout, do, dlg, dlb, dgate = post_bwd(dxn, x, out, mod3, lng, w_out, 1.0, "at_outb")
    do = do[0]
    parts = []
    for g, (_, dil) in enumerate(DIL_GROUPS):
        parts += attn_bwd(proj, do, of, lt, Bl, S, g, dil, "at_coreb%d" % g)
    dproj = jnp.concatenate(parts, axis=1)
    dx, dsh, dsc = modmm_bwd(dproj, w_in, x, mod3, dxres, "at_inb")
    dw_in = mm_tn(h, dproj, "at_dwin", bw=w_in.shape[2])
    dw_out = _as_slots(mm_tn(ob, dout, "at_dwout", bw=D), (N_DEV, D // N_DEV, D))
    return dx, [dw_in, dw_out], dlg, dlb, jnp.concatenate([dsh, dsc, dgate], axis=1)


def _unstack(stacked, axis):
    full = jnp.moveaxis(stacked, 0, axis)
    shp = list(full.shape)
    shp[axis:axis + 2] = [shp[axis] * shp[axis + 1]]
    return full.reshape(shp)


def _restack(full, axis):
    shp = list(full.shape)
    shp[axis:axis + 1] = [N_DEV, shp[axis] // N_DEV]
    return jnp.moveaxis(full.reshape(shp), axis, 0)


def kernel(x, c, ada_w, ada_b, ln_g, ln_b, ffn_w_in, ffn_w_out, mlstm_w_in, mlstm_gate_bias, mlstm_conv_w, mlstm_head_gain, mlstm_w_out, attn_w_in, attn_w_out, loss_target, m_ada_w, m_ada_b, m_ln_g, m_ln_b, m_ffn_w_in, m_ffn_w_out, m_mlstm_w_in, m_mlstm_gate_bias, m_mlstm_conv_w, m_mlstm_head_gain, m_mlstm_w_out, m_attn_w_in, m_attn_w_out, v_ada_w, v_ada_b, v_ln_g, v_ln_b, v_ffn_w_in, v_ffn_w_out, v_mlstm_w_in, v_mlstm_gate_bias, v_mlstm_conv_w, v_mlstm_head_gain, v_mlstm_w_out, v_attn_w_in, v_attn_w_out):
    Bl, S, _ = x.shape
    T = Bl * S
    Bg = Bl * N_DEV
    me = 4 * lax.axis_index("x") + 2 * lax.axis_index("y") + lax.axis_index("c")
    onehot = (jnp.arange(N_DEV) == me).astype(F32)
    weights = dict(ada_w=ada_w, ada_b=ada_b, ln_g=ln_g, ln_b=ln_b, ffn_w_in=ffn_w_in, ffn_w_out=ffn_w_out,
                   mlstm_w_in=mlstm_w_in, mlstm_gate_bias=mlstm_gate_bias, mlstm_conv_w=mlstm_conv_w,
                   mlstm_head_gain=mlstm_head_gain, mlstm_w_out=mlstm_w_out, attn_w_in=attn_w_in,
                   attn_w_out=attn_w_out)
    m_in = dict(ada_w=m_ada_w, ada_b=m_ada_b, ln_g=m_ln_g, ln_b=m_ln_b, ffn_w_in=m_ffn_w_in,
                ffn_w_out=m_ffn_w_out, mlstm_w_in=m_mlstm_w_in, mlstm_gate_bias=m_mlstm_gate_bias,
                mlstm_conv_w=m_mlstm_conv_w, mlstm_head_gain=m_mlstm_head_gain, mlstm_w_out=m_mlstm_w_out,
                attn_w_in=m_attn_w_in, attn_w_out=m_attn_w_out)
    v_in = dict(ada_w=v_ada_w, ada_b=v_ada_b, ln_g=v_ln_g, ln_b=v_ln_b, ffn_w_in=v_ffn_w_in,
                ffn_w_out=v_ffn_w_out, mlstm_w_in=v_mlstm_w_in, mlstm_gate_bias=v_mlstm_gate_bias,
                mlstm_conv_w=v_mlstm_conv_w, mlstm_head_gain=v_mlstm_head_gain, mlstm_w_out=v_mlstm_w_out,
                attn_w_in=v_attn_w_in, attn_w_out=v_attn_w_out)

    mixer = ("mlstm", "attn")
    shards = [[ffn_w_in[layer, 0], ffn_w_in[layer, 1], ffn_w_out[layer, 0], ffn_w_out[layer, 1],
               weights[mixer[layer] + "_w_in"][0], weights[mixer[layer] + "_w_out"][0]] for layer in range(DEPTH)]
    sends = [[s.astype(BF16) for s in layer_shards] for layer_shards in shards]
    small = jnp.concatenate([c.reshape(-1), ln_g.reshape(-1), ln_b.reshape(-1), mlstm_conv_w.reshape(-1)])
    n_small = small.shape[0]
    small = jnp.pad(small, (0, -n_small % (8 * PACK_COLS))).reshape(-1, PACK_COLS)

    def gathered_weights(g):
        return ((g[0], g[1]), (g[2].reshape(4, D_FF // 4, D), g[3].reshape(4, D_FF // 4, D)), g[4],
                g[5].reshape(1, D, D))

    first_in, small_all = all_gather([sends[0][0], small], "ag_params")
    full = [None, None]
    small_flat = small_all.reshape(N_DEV, -1)
    o0 = 0
    c_all = small_flat[:, o0:o0 + c.size].reshape(Bg, D)
    o0 += c.size
    lng_full = _unstack(small_flat[:, o0:o0 + ln_g.size].reshape((N_DEV,) + ln_g.shape), 2)
    o0 += ln_g.size
    lnb_full = _unstack(small_flat[:, o0:o0 + ln_b.size].reshape((N_DEV,) + ln_b.shape), 2)
    o0 += ln_b.size
    conv_full = _unstack(small_flat[:, o0:o0 + mlstm_conv_w.size].reshape((N_DEV,) + mlstm_conv_w.shape), 2)[0]
    gbias =jnp.pad(mlstm_gate_bias, ((0, 0), (0, 128 - 2 * HEADS)))

    ncols = ada_w.shape[2]
    ada_b_cols = lax.dynamic_slice_in_dim(ada_b, me * ncols, ncols, axis=1).reshape(DEPTH, 1, ncols)
    mod_cols = ada_fwd(c_all, ada_w, ada_b_cols, "ada_fwd")
    (mod_g,) = all_gather([mod_cols.reshape(DEPTH * Bg, ncols)], "ag_mod")
    mod_full = _unstack(mod_g.reshape(N_DEV, DEPTH, Bg, ncols), 2)
    mod_mine = lax.dynamic_slice_in_dim(mod_full, me * Bl, Bl, axis=1).reshape(DEPTH, Bl, 3, 3, D)

    xt = x.reshape(T, D)
    saved = []
    for layer in range(DEPTH):
        def lnp(s, layer=layer):
            return lng_full[layer, s].reshape(1, D), lnb_full[layer, s].reshape(1, D)
        md = mod_mine[layer]
        if layer == 0:
            a, g, u, h, late = ffn_in(xt, md[:, 0], first_in, "f0a_in", gather=sends[0][1:])
            full[0] = gathered_weights([first_in] + late)
            out, xn = proj_post(a, full[0][1][0], xt, md[:, 0], *lnp(0), 0.5, "f0a_out")
            xt, sv0 = xn, (xt, out, g, u, h, a)
            mw_in = jnp.pad(_unstack(full[0][2], 1), ((0, 0), (0, M_PROJ_PAD - M_PROJ)))
        else:
            xt, sv0, _ = ffn_fwd(xt, md[:, 0], full[layer][0][0], full[layer][1][0], *lnp(0), "f%da" % layer)
        f_in, f_out, mix_in, mix_out = full[layer]
        if layer % 2 == 0:
            xt, sv1, g1 = mlstm_fwd(xt, md[:, 1], mw_in, mix_out, conv_full, mlstm_head_gain, gbias, *lnp(1), Bl, S,
                                    gather=sends[1])
            full[1] = gathered_weights(g1)
        else:
            xt, sv1 = attn_mixer_fwd(xt, md[:, 1], mix_in, mix_out, *lnp(1), Bl, S)
        xt, sv2, _ = ffn_fwd(xt, md[:, 2], f_in[1], f_out[1], *lnp(2), "f%db" % layer)
        saved.append((sv0, sv1, sv2))

    dxt, lsum = loss_head(xt, loss_target.reshape(T, D), "loss")
    loss = lax.psum(lsum[0, 0], MESH_AXES)

    dmod, dlg_all, dlb_all = [None] * DEPTH, [None] * DEPTH, [None] * DEPTH
    wgrads = [None] * DEPTH
    recvs = [[None] * 6 for _ in range(DEPTH)]
    ml_small = None
    for layer in reversed(range(DEPTH)):
        md = mod_mine[layer]
        f_in, f_out, mix_in, mix_out = full[layer]
        sv0, sv1, sv2 = saved[layer]
        dxt, dw2, dlg2, dlb2, dm2, _, _ = ffn_bwd(dxt, sv2, md[:, 2], f_in[1], f_out[1],
                                                  lng_full[layer, 2].reshape(1, D), "f%db" % layer)
        lg1 = lng_full[layer, 1].reshape(1, D)
        if layer % 2 == 0:
            dxt, dw1, dlg1, dlb1, dm1, ml_small, got = mlstm_bwd(
                dxt, sv1, md[:, 1], mw_in, mix_out, conv_full, mlstm_head_gain, gbias, lg1, Bl, S,
                exchange=[b16 for _, b16 in wgrads[1]])
            recvs[1] = got
            dxt, dw0, dlg0, dlb0, dm0, got, own = ffn_bwd(
                dxt, sv0, md[:, 0], f_in[0], f_out[0], lng_full[layer, 0].reshape(1, D), "f%da" % layer,
                exchange=([dw2[0][1]], [dw2[1][1], dw1[0][1]], [dw1[1][1]]), exchange_own=True)
            (recvs[0][1],), (recvs[0][3], recvs[0][4]), (recvs[0][5],) = got
            recvs[0][0], recvs[0][2] = own
        else:
            dxt, dw1, dlg1, dlb1, dm1 = attn_mixer_bwd(dxt, sv1, md[:, 1], mix_in, mix_out, lg1, Bl, S)
            dxt, dw0, dlg0, dlb0, dm0, _, _ = ffn_bwd(dxt, sv0, md[:, 0], f_in[0], f_out[0],
                                                      lng_full[layer, 0].reshape(1, D), "f%da" % layer)
        wgrads[layer] = [dw0[0], dw2[0], dw0[1], dw2[1], dw1[0], dw1[1]]
        dmod[layer] = jnp.stack([dm0, dm1, dm2], axis=1).reshape(Bl, 9 * D)
        dlg_all[layer] = jnp.concatenate([dlg0, dlg1, dlg2], axis=0)
        dlb_all[layer] = jnp.concatenate([dlb0, dlb1, dlb2], axis=0)
    grad_x = dxt.reshape(Bl, S, D)

    gsh = [[shard_sum(lax.dynamic_index_in_dim(f32, me, axis=0, keepdims=False), recv, onehot,
                      "rs_sum%d_%d" % (layer, i))
            for i, ((f32, _), recv) in enumerate(zip(wgrads[layer], recvs[layer]))] for layer in range(DEPTH)]
    grads = {"ffn_w_in": jnp.stack([jnp.stack(g[0:2]) for g in gsh]),
             "ffn_w_out": jnp.stack([jnp.stack(g[2:4]) for g in gsh]),
             "mlstm_w_in": gsh[0][4][None], "mlstm_w_out": gsh[0][5][None],
             "attn_w_in": gsh[1][4][None], "attn_w_out": gsh[1][5][None]}

    dconv, dgain, dgbias = ml_small
    parts = [jnp.stack(dmod).reshape(-1), dgbias.reshape(-1), dgain.reshape(-1),
             jnp.stack(dlg_all).reshape(-1), jnp.stack(dlb_all).reshape(-1), dconv.reshape(-1)]
    sizes = [p.shape[0] for p in parts]
    flat = jnp.concatenate(parts)
    flat = jnp.pad(flat, (0, -flat.shape[0] % (8 * PACK_COLS))).reshape(-1, PACK_COLS)
    (sm_all,) = all_gather([flat], "ag_small")
    sm_sum = sum_leading(sm_all, "small_sum").reshape(-1)
    dmod_all = sm_all.reshape(N_DEV, -1)[:, :sizes[0]].reshape(N_DEV, DEPTH, Bl, 9 * D)
    dmod_all = jnp.moveaxis(dmod_all, 0, 1).reshape(DEPTH, Bg, 9 * D)
    o0 = sizes[0]
    grads["mlstm_gate_bias"] = sm_sum[o0:o0 + sizes[1]].reshape(mlstm_gate_bias.shape)
    o0 += sizes[1]
    grads["mlstm_head_gain"] = sm_sum[o0:o0 + sizes[2]].reshape(mlstm_head_gain.shape)
    o0 += sizes[2]
    nl = ln_g.shape[2]
    g_lng = sm_sum[o0:o0 + sizes[3]].reshape(DEPTH, 3, D)
    o0 += sizes[3]
    g_lnb = sm_sum[o0:o0 + sizes[4]].reshape(DEPTH, 3, D)
    o0 += sizes[4]
    g_conv = sm_sum[o0:o0 + sizes[5]].reshape(1, 4, D)
    grads["ln_g"] = lax.dynamic_slice_in_dim(g_lng, me * nl, nl, axis=2)
    grads["ln_b"] = lax.dynamic_slice_in_dim(g_lnb, me * nl, nl, axis=2)
    grads["mlstm_conv_w"] = lax.dynamic_slice_in_dim(g_conv, me * nl, nl, axis=2)
    dmod_cols = lax.dynamic_slice_in_dim(dmod_all, me * ncols, ncols, axis=2)
    gw, gb = ada_bwd(c_all.T, dmod_cols, dmod_all, "ada_bwd")
    grads["ada_w"] = gw
    grads["ada_b"] = gb.reshape(ada_b.shape)

    names = ["ada_w", "ada_b", "ln_g", "ln_b", "ffn_w_in", "ffn_w_out", "mlstm_w_in", "mlstm_gate_bias",
             "mlstm_conv_w", "mlstm_head_gain", "mlstm_w_out", "attn_w_in", "attn_w_out"]
    deltas, new_m, new_v = [], [], []
    for k in names:
        w = weights[k]
        shp2 = (math.prod(w.shape[:-1]), w.shape[-1])
        d_, m_, v_ = adamw(w.reshape(shp2), grads[k].reshape(shp2), m_in[k].reshape(shp2), v_in[k].reshape(shp2),
                           "adamw_" + k)
        deltas.append(d_.reshape(w.shape))
        new_m.append(m_.reshape(w.shape))
        new_v.append(v_.reshape(w.shape))
    return (loss, grad_x, *[grads[k] for k in names], *deltas, *new_m, *new_v)
```

```python
import functools
import math

import jax
import jax.numpy as jnp
from jax import lax
from jax.experimental import pallas as pl
from jax.experimental.pallas import tpu as pltpu

F32 = jnp.float32
BF16 = jnp.bfloat16

N_DEV = 8
MESH_AXES = ("x", "y", "c")
D = 1024
DEPTH = 2
D_FF = 2816
HEADS = 8
M_DQK = 64
M_DV = 128
M_CHUNK = 64
M_SLAB = 512
M_PROJ = 3088
M_PROJ_PAD = 3200
A_PROJ = 9216
DIL_GROUPS = ((128, 1), (512, 4), (2048, 16))
A_BLK = 128
A_UNROLL = 16
ALPHA = (2 * DEPTH) ** 0.25
LN_EPS = 1e-5
RMS_EPS = 1e-6
ADAM_LR = 0.001
ADAM_B1 = 0.9
ADAM_B2 = 0.999
ADAM_EPS = 1e-08
ADAM_WD = 0.01
ADAM_STEP = 10
NEG = -1e30
V7X_VMEM_LIMIT = 56 * 1024 * 1024
PACK_COLS = 1024
MESH_ID = pl.DeviceIdType.MESH
ANY_SPEC = pl.BlockSpec(memory_space=pl.ANY)


def _cp(n_axes):
    return pltpu.CompilerParams(dimension_semantics=("arbitrary",) * n_axes,
                                vmem_limit_bytes=V7X_VMEM_LIMIT)


def _dot(a, b):
    return jnp.dot(a, b, preferred_element_type=F32)


def _dot_nt(a, b):
    return lax.dot_general(a, b, (((1,), (1,)), ((), ())), preferred_element_type=F32)


def _dot_tn(a, b):
    return lax.dot_general(a, b, (((0,), (0,)), ((), ())), preferred_element_type=F32)


def _sum0(a):
    return jnp.sum(a, axis=0, keepdims=True)


def _sum1(a):
    return jnp.sum(a, axis=1, keepdims=True)


def _round(a):
    return a.astype(BF16).astype(F32)


def _sigmoid(a):
    return 1.0 / (1.0 + jnp.exp(-a))


def _tile(n, pref):
    t = min(n, pref)
    while n % t:
        t //= 2
    return t


def all_gather(arrs, name):
    n = len(arrs)

    def body(*refs):
        gather = Gather(refs[:n], refs[n:2 * n], *refs[2 * n:])
        gather.start()
        gather.finish()

    return pl.pallas_call(
        body, name=name, out_shape=Gather.out_shape(arrs),
        in_specs=[ANY_SPEC] * n, out_specs=[ANY_SPEC] * n, scratch_shapes=Gather.scratch(n),
    )(*arrs)


class Gather:
    def __init__(self, ins, outs, send_sems, recv_sems, local_sems):
        x, y, c = lax.axis_index("x"), lax.axis_index("y"), lax.axis_index("c")
        me, sibling = (x, y, c), (x, y, 1 - c)
        chips = [(1 - x, y), (x, 1 - y), (1 - x, 1 - y)]

        def slot(a, p):
            return outs[a].at[4 * p[0] + 2 * p[1] + p[2]]

        def copy(a, k, block, to, src=None):
            return pltpu.make_async_remote_copy(
                src_ref=slot(a, block) if src is None else src, dst_ref=slot(a, block),
                send_sem=send_sems.at[7 * a + k], recv_sem=recv_sems.at[7 * a + k],
                device_id=to, device_id_type=MESH_ID)

        n = len(ins)
        self.mine = [pltpu.make_async_copy(ins[a], slot(a, me), local_sems.at[a]) for a in range(n)]
        self.first, self.over_ici, self.passed, self.from_sibling = [], [], [], []
        for a in range(n):
            self.first.append(copy(a, 0, me, sibling, src=ins[a]))
            self.from_sibling.append(copy(a, 0, sibling, me))
            for j, chip in enumerate(chips):
                self.first.append(copy(a, 1 + j, me, (*chip, c), src=ins[a]))
                self.over_ici.append(copy(a, 1 + j, (*chip, c), me))
                self.passed.append(copy(a, 4 + j, (*chip, c), sibling))
                self.from_sibling.append(copy(a, 4 + j, (*chip, 1 - c), me))

    @staticmethod
    def out_shape(arrs):
        return [jax.ShapeDtypeStruct((N_DEV,) + a.shape, a.dtype) for a in arrs]

    @staticmethod
    def scratch(n):
        return [pltpu.SemaphoreType.DMA((7 * n,)), pltpu.SemaphoreType.DMA((7 * n,)),
                pltpu.SemaphoreType.DMA((n,))]

    def start(self):
        for cp in self.mine + self.first:
            cp.start()

    def finish(self):
        for landed, onward in zip(self.over_ici, self.passed):
            landed.wait_recv()
            onward.start()
        for cp in self.from_sibling:
            cp.wait_recv()
        for cp in self.first + self.passed:
            cp.wait_send()
        for cp in self.mine:
            cp.wait()


class Exchange:
    def __init__(self, sends, recvs, send_sems, recv_sems, local_sems):
        x, y, c = lax.axis_index("x"), lax.axis_index("y"), lax.axis_index("c")
        me = 4 * x + 2 * y + c
        self.own = [pltpu.make_async_copy(s.at[me], r.at[me], local_sems.at[a])
                    for a, (s, r) in enumerate(zip(sends, recvs))]
        self.copies = []
        for a, (s_ref, r_ref) in enumerate(zip(sends, recvs)):
            for k in range(1, N_DEV):
                px = 1 - x if (k >> 2) & 1 else x
                py = 1 - y if (k >> 1) & 1 else y
                pc = 1 - c if k & 1 else c
                self.copies.append(pltpu.make_async_remote_copy(
                    src_ref=s_ref.at[4 * px + 2 * py + pc], dst_ref=r_ref.at[me],
                    send_sem=send_sems.at[7 * a + k - 1], recv_sem=recv_sems.at[7 * a + k - 1],
                    device_id=(px, py, pc), device_id_type=MESH_ID))

    @staticmethod
    def scratch(n):
        return [pltpu.SemaphoreType.DMA((7 * n,)), pltpu.SemaphoreType.DMA((7 * n,)),
                pltpu.SemaphoreType.DMA((n,))]

    def start(self):
        for cp in self.own + self.copies:
            cp.start()

    def finish(self):
        for cp in self.copies:
            cp.wait_send()
            cp.wait_recv()
        for cp in self.own:
            cp.wait()


def host_comm(body, grid, n_in, n_out, gather=(), exchange=()):
    ng, nx = len(gather), len(exchange)
    if ng + nx == 0:
        return body, [], [], [], []

    def hosted(*refs):
        ins, c_in, rest = refs[:n_in], refs[n_in:n_in + ng + nx], refs[n_in + ng + nx:]
        outs, c_out, rest = rest[:n_out], rest[n_out:n_out + ng + nx], rest[n_out + ng + nx:]
        n_sems = 3 * ((ng > 0) + (nx > 0))
        scratch, sems = rest[:len(rest) - n_sems], rest[len(rest) - n_sems:]

        def comms():
            made = [Gather(c_in[:ng], c_out[:ng], *sems[:3])] if ng else []
            return made + ([Exchange(c_in[ng:], c_out[ng:], *sems[-3:])] if nx else [])

        ids = [pl.program_id(a) for a in range(len(grid))]

        @pl.when(functools.reduce(jnp.logical_and, [i == 0 for i in ids]))
        def _():
            for cm in comms():
                cm.start()
        body(*ins, *outs, *scratch)

        @pl.when(functools.reduce(jnp.logical_and, [i == g - 1 for i, g in zip(ids, grid)]))
        def _():
            for cm in comms():
                cm.finish()

    shapes = Gather.out_shape(gather) + [jax.ShapeDtypeStruct(a.shape, a.dtype) for a in exchange]
    scratch = (Gather.scratch(ng) if ng else []) + (Exchange.scratch(nx) if nx else [])
    return hosted, [ANY_SPEC] * (ng + nx), [ANY_SPEC] * (ng + nx), shapes, scratch


def shard_sum(own, recv, onehot, name):
    R, C = own.shape
    tr = _tile(R, 512)

    def body(oh_ref, own_ref, recv_ref, o_ref):
        acc = None
        for j in range(N_DEV):
            term = jnp.where(oh_ref[j] > 0.5, own_ref[...], recv_ref[j].astype(F32))
            acc = term if acc is None else acc + term
        o_ref[...] = acc

    return pl.pallas_call(
        body, name=name, grid=(R // tr,),
        in_specs=[pl.BlockSpec(memory_space=pltpu.SMEM),
                  pl.BlockSpec((tr, C), lambda i: (i, 0)),
                  pl.BlockSpec((N_DEV, tr, C), lambda i: (0, i, 0))],
        out_specs=pl.BlockSpec((tr, C), lambda i: (i, 0)),
        out_shape=jax.ShapeDtypeStruct((R, C), F32), compiler_params=_cp(1),
    )(onehot, own, recv)


def sum_leading(a, name):
    _, R, C = a.shape

    def body(a_ref, o_ref):
        acc = a_ref[0]
        for j in range(1, N_DEV):
            acc = acc + a_ref[j]
        o_ref[...] = acc

    return pl.pallas_call(body, name=name, out_shape=jax.ShapeDtypeStruct((R, C), F32),
                          compiler_params=_cp(0))(a)


def _col_chunks(w, tn):
    if w.ndim == 3:
        return w.shape[0], w.shape[2], pl.BlockSpec((None, w.shape[1], w.shape[2]), lambda i, j: (j, 0, 0))
    return w.shape[1] // tn, tn, pl.BlockSpec((w.shape[0], tn), lambda i, j: (0, j))


def modmm(x, mod3, w, out_dtype, name, tn=None):
    T, Dm = x.shape
    nj, tn, w_spec = _col_chunks(w, tn)
    N = nj * tn
    Bl = mod3.shape[0]
    tm = _tile(T // Bl, 1024)
    tpb = T // Bl // tm

    def body(x_ref, mod_ref, w_ref, o_ref, h_ref, hs):
        @pl.when(pl.program_id(1) == 0)
        def _():
            m = mod_ref[0]
            hs[...] = (x_ref[...] * (1.0 + m[1:2, :]) + m[0:1, :]).astype(BF16)
            h_ref[...] = hs[...]
        o_ref[...] = _dot(hs[...], w_ref[...]).astype(o_ref.dtype)

    return pl.pallas_call(
        body, name=name, grid=(T // tm, nj),
        in_specs=[pl.BlockSpec((tm, Dm), lambda i, j: (i, 0)),
                  pl.BlockSpec((1, 3, Dm), lambda i, j: (i // tpb, 0, 0)), w_spec],
        out_specs=[pl.BlockSpec((tm, tn), lambda i, j: (i, j)),
                   pl.BlockSpec((tm, Dm), lambda i, j: (i, 0))],
        out_shape=[jax.ShapeDtypeStruct((T, N), out_dtype), jax.ShapeDtypeStruct((T, Dm), BF16)],
        scratch_shapes=[pltpu.VMEM((tm, Dm), BF16)], compiler_params=_cp(2),
    )(x, mod3, w)


def modmm_bwd(dp, w, x, mod3, dxres, name, tn=None, tm=1024, exchange=()):
    T, Dm = x.shape
    Bl = mod3.shape[0]
    tm = _tile(T // Bl, tm)
    tpb = T // Bl // tm
    resident = dp.ndim == 3
    if resident:
        nc, nj = dp.shape[0], 1
        dp_spec = pl.BlockSpec((nc, tm, dp.shape[2]), lambda i, j: (0, i, 0))
        w_spec = pl.BlockSpec(w.shape, lambda i, j: (0, 0, 0))
    else:
        nj, tn, w_spec = _col_chunks(w, tn)
        dp_spec = pl.BlockSpec((tm, tn), lambda i, j: (i, j))

    def body(dp_ref, w_ref, x_ref, mod_ref, dxr_ref, dx_ref, dsh_ref, dsc_ref, acc):
        i, j = pl.program_id(0), pl.program_id(1)

        @pl.when(j == 0)
        def _():
            acc[...] = jnp.zeros_like(acc)
        if resident:
            for c in range(nc):
                acc[...] += _dot_nt(dp_ref[c], w_ref[c])
        else:
            acc[...] += _dot_nt(dp_ref[...], w_ref[...])

        @pl.when(j == nj - 1)
        def _():
            dh = acc[...]
            xx = x_ref[...]
            dx_ref[...] = dxr_ref[...] + dh * (1.0 + mod_ref[0][1:2, :])

            @pl.when(i % tpb == 0)
            def _():
                dsh_ref[...] = jnp.zeros_like(dsh_ref)
                dsc_ref[...] = jnp.zeros_like(dsc_ref)
            dsh_ref[0] += _sum0(dh)
            dsc_ref[0] += _sum0(dh * xx)

    grid = (T // tm, nj)
    body, c_in, c_out, c_shape, c_scratch = host_comm(body, grid, 5, 3, exchange=exchange)
    dx, dsh, dsc, *received = pl.pallas_call(
        body, name=name, grid=grid,
        in_specs=[dp_spec, w_spec,
                  pl.BlockSpec((tm, Dm), lambda i, j: (i, 0)),
                  pl.BlockSpec((1, 3, Dm), lambda i, j: (i // tpb, 0, 0)),
                  pl.BlockSpec((tm, Dm), lambda i, j: (i, 0))] + c_in,
        out_specs=[pl.BlockSpec((tm, Dm), lambda i, j: (i, 0)),
                   pl.BlockSpec((1, 1, Dm), lambda i, j: (i // tpb, 0, 0)),
                   pl.BlockSpec((1, 1, Dm), lambda i, j: (i // tpb, 0, 0))] + c_out,
        out_shape=[jax.ShapeDtypeStruct((T, Dm), F32), jax.ShapeDtypeStruct((Bl, 1, Dm), F32),
                   jax.ShapeDtypeStruct((Bl, 1, Dm), F32)] + c_shape,
        scratch_shapes=[pltpu.VMEM((tm, Dm), F32)] + c_scratch, compiler_params=_cp(2),
    )(dp, w, x, mod3, dxres, *exchange)
    return (dx, dsh, dsc, received) if exchange else (dx, dsh, dsc)


def _ln_stats(z):
    mu = jnp.mean(z, axis=-1, keepdims=True)
    zc = z - mu
    var = jnp.mean(zc * zc, axis=-1, keepdims=True)
    rstd = lax.rsqrt(var + LN_EPS)
    return zc * rstd, rstd


def proj_post(a, w, x, mod3, lng, lnb, weight, name, gather=()):
    nk, T, tk = a.shape
    Dm = w.shape[2]
    Bl = mod3.shape[0]
    tm = _tile(T // Bl, 1024 if nk == 1 else 512)
    tpb = T // Bl // tm

    def body(a_ref, w_ref, x_ref, mod_ref, g_ref, b_ref, out_ref, xn_ref):
        out = _dot(a_ref[0], w_ref[0])
        for k in range(1, nk):
            out = out + _dot(a_ref[k], w_ref[k])
        out_ref[...] = out
        z = ALPHA * x_ref[...] + (weight * (1.0 + mod_ref[0][2:3, :])) * out
        xhat, _ = _ln_stats(z)
        xn_ref[...] = xhat * g_ref[...] + b_ref[...]

    row = pl.BlockSpec((tm, Dm), lambda i: (i, 0))
    vec = pl.BlockSpec((1, Dm), lambda i: (0, 0))
    grid = (T // tm,)
    body, c_in, c_out, c_shape, c_scratch = host_comm(body, grid, 6, 2, gather=gather)
    out, xn, *gathered = pl.pallas_call(
        body, name=name, grid=grid,
        in_specs=[pl.BlockSpec((nk, tm, tk), lambda i: (0, i, 0)),
                  pl.BlockSpec((nk, tk, Dm), lambda i: (0, 0, 0)),
                  row, pl.BlockSpec((1, 3, Dm), lambda i: (i // tpb, 0, 0)), vec, vec] + c_in,
        out_specs=[row, row] + c_out,
        out_shape=[jax.ShapeDtypeStruct((T, Dm), F32), jax.ShapeDtypeStruct((T, Dm), F32)] + c_shape,
        scratch_shapes=c_scratch, compiler_params=_cp(1),
    )(a, w, x, mod3, lng, lnb, *gather)
    return (out, xn, gathered) if gather else (out, xn)


def post_bwd(dxn, x, out, mod3, lng, w, weight, name, tm=512, gu=None, exchange=()):
    T, Dm = x.shape
    nk, tk, _ = w.shape
    Bl = mod3.shape[0]
    tm = _tile(T // Bl, tm)
    tpb = T // Bl // tm
    fused = gu is not None

    def body(dxn_ref, x_ref, out_ref, mod_ref, g_ref, w_ref, *rest):
        if fused:
            gg_ref, uu_ref = rest[:2]
            rest = rest[2:]
        dxr_ref, dout_ref, da_ref, dg_ref, db_ref, dgate_ref = rest
        i = pl.program_id(0)
        out = out_ref[...]
        dxn = dxn_ref[...]
        coef = weight * (1.0 + mod_ref[0][2:3, :])
        xhat, rstd = _ln_stats(ALPHA * x_ref[...] + coef * out)
        dyh = dxn * g_ref[...]
        dz = rstd * (dyh - jnp.mean(dyh, axis=-1, keepdims=True)
                     - xhat * jnp.mean(dyh * xhat, axis=-1, keepdims=True))
        dxr_ref[...] = ALPHA * dz
        dout = (coef * dz).astype(BF16)
        dout_ref[...] = dout

        @pl.when(i == 0)
        def _():
            dg_ref[...] = jnp.zeros_like(dg_ref)
            db_ref[...] = jnp.zeros_like(db_ref)

        @pl.when(i % tpb == 0)
        def _():
            dgate_ref[...] = jnp.zeros_like(dgate_ref)
        dg_ref[...] += _sum0(dxn * xhat)
        db_ref[...] += _sum0(dxn)
        dgate_ref[0] += _sum0((weight * out) * dz)
        for k in range(nk):
            da = _dot_nt(dout, w_ref[k])
            if fused:
                gg = gg_ref[k].astype(F32)
                s = _sigmoid(gg)
                da_ref[k] = (da * uu_ref[k].astype(F32) * (s * (1.0 + gg * (1.0 - s)))).astype(BF16)
                da_ref[nk + k] = (da * (gg * s)).astype(BF16)
            else:
                da_ref[k] = da.astype(BF16)

    row = pl.BlockSpec((tm, Dm), lambda i: (i, 0))
    vec = pl.BlockSpec((1, Dm), lambda i: (0, 0))
    wide = pl.BlockSpec((nk, tm, tk), lambda i: (0, i, 0))
    nda = 2 * nk if fused else nk
    grid = (T // tm,)
    body, c_in, c_out, c_shape, c_scratch = host_comm(body, grid, 8 if fused else 6, 6, exchange=exchange)
    *results, = pl.pallas_call(
        body, name=name, grid=grid,
        in_specs=[row, row, row, pl.BlockSpec((1, 3, Dm), lambda i: (i // tpb, 0, 0)), vec,
                  pl.BlockSpec((nk, tk, Dm), lambda i: (0, 0, 0))] + ([wide, wide] if fused else []) + c_in,
        out_specs=[row, row, pl.BlockSpec((nda, tm, tk), lambda i: (0, i, 0)),
                   vec, vec, pl.BlockSpec((1, 1, Dm), lambda i: (i // tpb, 0, 0))] + c_out,
        out_shape=[jax.ShapeDtypeStruct((T, Dm), F32), jax.ShapeDtypeStruct((T, Dm), BF16),
                   jax.ShapeDtypeStruct((nda, T, tk), BF16), jax.ShapeDtypeStruct((1, Dm), F32),
                   jax.ShapeDtypeStruct((1, Dm), F32), jax.ShapeDtypeStruct((Bl, 1, Dm), F32)] + c_shape,
        scratch_shapes=c_scratch, compiler_params=_cp(1),
    )(dxn, x, out, mod3, lng, w, *(gu if fused else ()), *exchange)
    return tuple(results[:6]) + ((results[6:],) if exchange else ())


def mm_tn(a, b, name, bw=None, a_copies=False, exchange=()):
    a3, b3 = a.ndim == 3, b.ndim == 3
    nk, T, tk = a.shape if a3 else (1,) + a.shape
    if a_copies:
        nk = 1
    nc, wn = (b.shape[0], b.shape[2]) if b3 else (b.shape[1] // bw, bw)
    tt = _tile(T, 2048)
    nt = T // tt

    def body(a_ref, b_ref, o_ref, ob_ref):
        t = pl.program_id(2)

        @pl.when(t == 0)
        def _():
            o_ref[...] = jnp.zeros_like(o_ref)
        o_ref[...] += _dot_tn(a_ref[...], b_ref[...])

        @pl.when(t == nt - 1)
        def _():
            ob_ref[...] = o_ref[...].astype(BF16)

    a_spec = (pl.BlockSpec((None, tt, tk), lambda k, c, t: (k, t, 0)) if a3
              else pl.BlockSpec((tt, tk), lambda k, c, t: (t, 0)))
    b_spec = (pl.BlockSpec((None, tt, wn), lambda k, c, t: (c, t, 0)) if b3
              else pl.BlockSpec((tt, wn), lambda k, c, t: (t, c)))
    o_spec = pl.BlockSpec((None, tk, wn), lambda k, c, t: (k * nc + c, 0, 0))
    grid = (nk, nc, nt)
    body, c_in, c_out, c_shape, c_scratch = host_comm(body, grid, 2, 2, exchange=exchange)
    o32, o16, *received = pl.pallas_call(
        body, name=name, grid=grid, in_specs=[a_spec, b_spec] + c_in, out_specs=[o_spec, o_spec] + c_out,
        out_shape=[jax.ShapeDtypeStruct((nk * nc, tk, wn), F32), jax.ShapeDtypeStruct((nk * nc, tk, wn), BF16)]
        + c_shape,
        scratch_shapes=c_scratch, compiler_params=_cp(3),
    )(a, b, *exchange)
    return (o32, o16, received) if exchange else (o32, o16)


def ffn_in(x, mod3, w, name, gather=()):
    T, Dm = x.shape
    nj, tf = w.shape[0] // 2, w.shape[2]
    Bl = mod3.shape[0]
    tm = _tile(T // Bl, 1024)
    tpb = T // Bl // tm

    def body(x_ref, mod_ref, wg_ref, wu_ref, a_ref, g_ref, u_ref, h_ref):
        m = mod_ref[0]
        h = (x_ref[...] * (1.0 + m[1:2, :]) + m[0:1, :]).astype(BF16)
        h_ref[...] = h
        g = _dot(h, wg_ref[...])
        u = _dot(h, wu_ref[...])
        a_ref[...] = (g * _sigmoid(g) * u).astype(BF16)
        g_ref[...] = g.astype(BF16)
        u_ref[...] = u.astype(BF16)

    col = pl.BlockSpec((None, tm, tf), lambda j, i: (j, i, 0))
    grid = (nj, T // tm)
    body, c_in, c_out, c_shape, c_scratch = host_comm(body, grid, 4, 4, gather=gather)
    a, g, u, h, *gathered = pl.pallas_call(
        body, name=name, grid=grid,
        in_specs=[pl.BlockSpec((tm, Dm), lambda j, i: (i, 0)),
                  pl.BlockSpec((1, 3, Dm), lambda j, i: (i // tpb, 0, 0)),
                  pl.BlockSpec((None, Dm, tf), lambda j, i: (j, 0, 0)),
                  pl.BlockSpec((None, Dm, tf), lambda j, i: (nj + j, 0, 0))] + c_in,
        out_specs=[col, col, col, pl.BlockSpec((None, tm, Dm), lambda j, i: (j, i, 0))] + c_out,
        out_shape=[jax.ShapeDtypeStruct((nj, T, tf), BF16)] * 3 + [jax.ShapeDtypeStruct((nj, T, Dm), BF16)]
        + c_shape,
        scratch_shapes=c_scratch, compiler_params=_cp(2),
    )(x, mod3, w, w, *gather)
    return a, g, u, h, gathered


def loss_head(y, tgt, name):
    T, Dm = y.shape
    tm = _tile(T, 512)
    nt = T // tm

    def body(y_ref, t_ref, dy_ref, l_ref, acc):
        i = pl.program_id(0)

        @pl.when(i == 0)
        def _():
            acc[...] = jnp.zeros_like(acc)
        e = y_ref[...] - t_ref[...]
        dy_ref[...] = e * (1.0 / Dm)
        acc[...] += _sum0(e * e)

        @pl.when(i == nt - 1)
        def _():
            l_ref[...] = jnp.broadcast_to(_sum1(acc[...]) * (0.5 / Dm), l_ref.shape)

    return pl.pallas_call(
        body, name=name, grid=(nt,),
        in_specs=[pl.BlockSpec((tm, Dm), lambda i: (i, 0)), pl.BlockSpec((tm, Dm), lambda i: (i, 0))],
        out_specs=[pl.BlockSpec((tm, Dm), lambda i: (i, 0)), pl.BlockSpec((1, 128), lambda i: (0, 0))],
        out_shape=[jax.ShapeDtypeStruct((T, Dm), F32), jax.ShapeDtypeStruct((1, 128), F32)],
        scratch_shapes=[pltpu.VMEM((1, Dm), F32)], compiler_params=_cp(1),
    )(y, tgt)


def adamw(w, g, m, v, name):
    R, C = w.shape
    tr = _tile(R, 512) if R % 8 == 0 else R

    def body(w_ref, g_ref, m_ref, v_ref, d_ref, nm_ref, nv_ref):
        gg = g_ref[...]
        mm = ADAM_B1 * m_ref[...] + (1.0 - ADAM_B1) * gg
        vv = ADAM_B2 * v_ref[...] + (1.0 - ADAM_B2) * (gg * gg)
        m_hat = mm / (1.0 - ADAM_B1 ** ADAM_STEP)
        v_hat = vv / (1.0 - ADAM_B2 ** ADAM_STEP)
        d_ref[...] = -ADAM_LR * (m_hat / (jnp.sqrt(v_hat) + ADAM_EPS) + ADAM_WD * w_ref[...])
        nm_ref[...] = mm
        nv_ref[...] = vv

    spec = pl.BlockSpec((tr, C), lambda i: (i, 0))
    return pl.pallas_call(
        body, name=name, grid=(R // tr,), in_specs=[spec] * 4, out_specs=[spec] * 3,
        out_shape=[jax.ShapeDtypeStruct((R, C), F32)] * 3, compiler_params=_cp(1),
    )(w, g, m, v)


def ada_fwd(c_all, ada_w, ada_b_cols, name):
    Lr, Dm, Nc = ada_w.shape
    Bg = c_all.shape[0]

    def body(c_ref, w_ref, b_ref, o_ref):
        cc = c_ref[...]
        cond = cc * _sigmoid(cc)
        o_ref[0] = _dot(cond.astype(BF16), w_ref[0].astype(BF16)) + b_ref[0]

    return pl.pallas_call(
        body, name=name, grid=(Lr,),
        in_specs=[pl.BlockSpec((Bg, Dm), lambda l: (0, 0)),
                  pl.BlockSpec((1, Dm, Nc), lambda l: (l, 0, 0)),
                  pl.BlockSpec((1, 1, Nc), lambda l: (l, 0, 0))],
        out_specs=pl.BlockSpec((1, Bg, Nc), lambda l: (l, 0, 0)),
        out_shape=jax.ShapeDtypeStruct((Lr, Bg, Nc), F32), compiler_params=_cp(1),
    )(c_all, ada_w, ada_b_cols)


def ada_bwd(c_all_t, dmod_cols, dmod_all, name):
    Dm, Bg = c_all_t.shape
    Lr, _, Nc = dmod_cols.shape
    Nf = dmod_all.shape[2]

    def body(c_ref, dm_ref, da_ref, gw_ref, gb_ref):
        cc = c_ref[...]
        cond = cc * _sigmoid(cc)
        gw_ref[0] = _dot(cond.astype(BF16), dm_ref[0].astype(BF16))
        gb_ref[0] = _sum0(da_ref[0])

    return pl.pallas_call(
        body, name=name, grid=(Lr,),
        in_specs=[pl.BlockSpec((Dm, Bg), lambda l: (0, 0)),
                  pl.BlockSpec((1, Bg, Nc), lambda l: (l, 0, 0)),
                  pl.BlockSpec((1, Bg, Nf), lambda l: (l, 0, 0))],
        out_specs=[pl.BlockSpec((1, Dm, Nc), lambda l: (l, 0, 0)),
                   pl.BlockSpec((1, 1, Nf), lambda l: (l, 0, 0))],
        out_shape=[jax.ShapeDtypeStruct((Lr, Dm, Nc), F32), jax.ShapeDtypeStruct((Lr, 1, Nf), F32)],
        compiler_params=_cp(1),
    )(c_all_t, dmod_cols, dmod_all)


def _conv_taps(x, w, rows):
    shifted = [x]
    c = w[3:4, :] * x
    for k in range(1, 4):
        xs = jnp.where(rows >= k, pltpu.roll(x, k, 0), 0.0)
        shifted.append(xs)
        c = c + w[3 - k:4 - k, :] * xs
    return c, shifted


def conv_silu(proj3, conv_w, name):
    Bl, S, _ = proj3.shape
    ncb = conv_w.shape[1] // 128

    def body(x_ref, w_ref, o_ref):
        rows = lax.broadcasted_iota(jnp.int32, (S, 128), 0)
        c, _ = _conv_taps(_round(x_ref[0]), _round(w_ref[...]), rows)
        o_ref[0] = c * _sigmoid(c)

    return pl.pallas_call(
        body, name=name, grid=(Bl, ncb),
        in_specs=[pl.BlockSpec((1, S, 128), lambda b, j: (b, 0, j)),
                  pl.BlockSpec((4, 128), lambda b, j: (0, j))],
        out_specs=pl.BlockSpec((1, S, 128), lambda b, j: (b, 0, j)),
        out_shape=jax.ShapeDtypeStruct((Bl, S, conv_w.shape[1]), F32), compiler_params=_cp(2),
    )(proj3, conv_w)


def conv_silu_bwd(proj3, conv_w, dq, dk, name):
    Bl, S, _ = proj3.shape
    nq = dq.shape[2] // 128

    def body(x_ref, w_ref, dq_ref, dk_ref, dx_ref, dw_ref):
        j = pl.program_id(1)
        rows = lax.broadcasted_iota(jnp.int32, (S, 128), 0)
        w = _round(w_ref[...])
        c, shifted = _conv_taps(_round(x_ref[0]), w, rows)
        s = _sigmoid(c)
        dact = jnp.where(j < nq, dq_ref[0], dk_ref[0])
        dc = _round(dact * (s * (1.0 + c * (1.0 - s))))
        dx = w[3:4, :] * dc
        dws = [_sum0(dc * shifted[0])]
        for k in range(1, 4):
            up = jnp.where(rows < S - k, pltpu.roll(dc, S - k, 0), 0.0)
            dx = dx + w[3 - k:4 - k, :] * up
            dws.append(_sum0(dc * shifted[k]))
        dx_ref[0] = dx.astype(BF16)
        tap = lax.broadcasted_iota(jnp.int32, (4, 128), 0)
        dw_ref[0] = functools.reduce(lambda a, b: a + b, [jnp.where(tap == 3 - k, dws[k], 0.0) for k in range(4)])

    return pl.pallas_call(
        body, name=name, grid=(Bl, 2 * nq),
        in_specs=[pl.BlockSpec((1, S, 128), lambda b, j: (b, 0, j)),
                  pl.BlockSpec((4, 128), lambda b, j: (0, j)),
                  pl.BlockSpec((1, S, 128), lambda b, j: (b, 0, jnp.minimum(j, nq - 1))),
                  pl.BlockSpec((1, S, 128), lambda b, j: (b, 0, jnp.maximum(j - nq, 0)))],
        out_specs=[pl.BlockSpec((1, S, 128), lambda b, j: (b, 0, j)),
                   pl.BlockSpec((1, 4, 128), lambda b, j: (b, 0, j))],
        out_shape=[jax.ShapeDtypeStruct((Bl, S, 2 * nq * 128), BF16),
                   jax.ShapeDtypeStruct((Bl, 4, 2 * nq * 128), F32)],
        compiler_params=_cp(2),
    )(proj3, conv_w, dq, dk)


def _log_sigmoid(a):
    return jnp.minimum(a, 0.0) - jnp.log(1.0 + jnp.exp(-jnp.abs(a)))


def _interleave(gens):
    live = list(gens)
    while live:
        still = []
        for g in live:
            try:
                next(g)
                still.append(g)
            except StopIteration:
                pass
        live = still


def _finish(gen):
    while True:
        try:
            next(gen)
        except StopIteration as done:
            return done.value


def _chunk_state(kc, vc, gi, bcum, b_last, C, n, m):
    a = b_last - bcum + gi
    m_loc = jnp.max(a, axis=0, keepdims=True)
    wa = jnp.exp(a - m_loc)
    c_loc = _dot_tn((wa * vc).astype(BF16), kc.astype(BF16))
    n_loc = _sum0(_round(wa) * _round(kc))
    m_new = jnp.maximum(b_last + m, m_loc)
    sp = jnp.exp(b_last + m - m_new)
    sl = jnp.exp(m_loc - m_new)
    yield
    return sp * C + sl * c_loc, sp * n + sl * n_loc, m_new, wa, sp, sl


def _chunk_out(qs, kc, vc, gi_row, bcum, bcum_row, low, C, n, m):
    inter_log = bcum + m
    dlog = jnp.where(low, bcum - bcum_row + gi_row, NEG)
    m_i = jnp.maximum(inter_log, jnp.max(dlog, axis=1, keepdims=True))
    dm = jnp.exp(dlog - m_i)
    iw = jnp.exp(inter_log - m_i)
    qs_b, k_b, v_b = qs.astype(BF16), kc.astype(BF16), vc.astype(BF16)
    sqk = _dot_nt(qs_b, k_b)
    qc_ = _dot_nt(qs_b, C.astype(BF16))
    qn = _sum1(_round(qs) * _round(n))
    floor = jnp.exp(-m_i)
    yield
    sc = sqk * dm
    sv = _dot(sc.astype(BF16), v_b)
    den = _sum1(sc) + iw * qn
    dn = jnp.maximum(jnp.abs(den), floor)
    yield
    num = sv + iw * qc_
    return dict(hc=num / dn, den=den, dn=dn, floor=floor, sc=sc, dm=dm, iw=iw, qc=qc_, qn=qn,
                qs_b=qs_b, k_b=k_b, v_b=v_b)


def _cell_consts(L):
    ri = lax.broadcasted_iota(jnp.int32, (L, L), 0)
    ci = lax.broadcasted_iota(jnp.int32, (L, L), 1)
    return ri == ci, ci <= ri, ri <= ci


def _load_chunk(q_ref, k_ref, v_ref, G, off, L, h, lane):
    hh = h % 2
    qmask = (lane >= M_DQK * hh) & (lane < M_DQK * (hh + 1))
    pair = pl.ds(128 * (h // 2), 128)
    qc = jnp.where(qmask, q_ref[0, pl.ds(off, L), pair], 0.0)
    kc = jnp.where(qmask, k_ref[0, pl.ds(off, L), pair], 0.0)
    vc = v_ref[0, pl.ds(off, L), pl.ds(M_DV * h, M_DV)]
    gi = _sum1(jnp.where(lane == h, G, 0.0))
    gf = _sum1(jnp.where(lane == h + HEADS, G, 0.0))
    return qmask, qc, kc, vc, gi, gf


def _gate_rows(gi, gf, eye, low, upp):
    lf = _log_sigmoid(gf)
    lf_row = _sum0(jnp.where(eye, lf, 0.0))
    gi_row = _sum0(jnp.where(eye, gi, 0.0))
    bcum = _sum1(jnp.where(low, lf_row, 0.0))
    bcum_row = _sum0(jnp.where(upp, lf, 0.0))
    b_last = _sum0(lf)
    return gi_row, bcum, bcum_row, b_last


def _cell_specs(SB, cpb, blk):
    def seq(width, col):
        return pl.BlockSpec((1, SB, width), lambda b, s: (b, blk(s), col))

    def state(rows):
        return pl.BlockSpec((1, HEADS, cpb, rows, 128), lambda b, s: (b, 0, blk(s), 0, 0))

    ins = [seq(D // 2, 0), seq(D // 2, 1), seq(D, 1), seq(D, 2), seq(128, 3 * D // 128),
           pl.BlockSpec((1, D), lambda b, s: (0, 0)), pl.BlockSpec((1, 128), lambda b, s: (0, 0))]
    return ins, [state(M_DV), state(1), state(1)], seq


def mlstm_cell_fwd(qk3, proj3, gain, gbias, name, gather=()):
    Bl, S, _ = qk3.shape
    L = M_CHUNK
    SB = min(M_SLAB, S)
    cpb, nc, nsb = SB // L, S // L, S // SB
    scale = M_DQK ** -0.5

    def body(q_ref, k_ref, v_ref, o_ref, g_ref, gain_ref, gb_ref, y_ref, cst_ref, nst_ref, mst_ref, *state):
        C_s, n_s, m_s = state[:HEADS], state[HEADS:2 * HEADS], state[2 * HEADS:]

        @pl.when(pl.program_id(1) == 0)
        def _():
            for ref in state:
                ref[...] = jnp.zeros_like(ref)
        lane = lax.broadcasted_iota(jnp.int32, (L, 128), 1)
        eye, low, upp = _cell_consts(L)

        def step(c, carry):
            off = pl.multiple_of(c * L, L)
            G = g_ref[0, pl.ds(off, L), :] + gb_ref[...]

            def head(h):
                C, n, mb = C_s[h][...], n_s[h][...], m_s[h][...]
                cst_ref[0, h, c] = C
                nst_ref[0, h, c] = n
                mst_ref[0, h, c] = mb
                m = mb[:, 0:1]
                _, qc, kc, vc, gi, gf = _load_chunk(q_ref, k_ref, v_ref, G, off, L, h, lane)
                gi_row, bcum, bcum_row, b_last = _gate_rows(gi, gf, eye, low, upp)
                state = _chunk_state(kc, vc, gi, bcum, b_last, C, n, m)
                next(state)
                r = yield from _chunk_out(qc * scale, kc, vc, gi_row, bcum, bcum_row, low, C, n, m)
                hc = r["hc"]
                hn = hc * lax.rsqrt(jnp.mean(hc * hc, axis=-1, keepdims=True) + RMS_EPS)
                cols = pl.ds(M_DV * h, M_DV)
                oc = o_ref[0, pl.ds(off, L), cols]
                y_ref[0, pl.ds(off, L), cols] = (_sigmoid(oc) * hn * gain_ref[:, cols]).astype(BF16)
                C2, n2, m2, _, _, _ = _finish(state)
                C_s[h][...] = C2
                n_s[h][...] = n2
                m_s[h][...] = jnp.broadcast_to(m2, (1, 128))

            _interleave(head(h) for h in range(HEADS))
            return carry

        lax.fori_loop(0, cpb, step, 0)

    ins, states, seq = _cell_specs(SB, cpb, lambda s: s)
    grid = (Bl, nsb)
    body, c_in, c_out, c_shape, c_scratch = host_comm(body, grid, 7, 4, gather=gather)
    return pl.pallas_call(
        body, name=name, grid=grid, in_specs=ins + c_in, out_specs=[seq(D, 0)] + states + c_out,
        out_shape=[jax.ShapeDtypeStruct((Bl, S, D), BF16),
                   jax.ShapeDtypeStruct((Bl, HEADS, nc, M_DV, 128), F32),
                   jax.ShapeDtypeStruct((Bl, HEADS, nc, 1, 128), F32),
                   jax.ShapeDtypeStruct((Bl, HEADS, nc, 1, 128), F32)] + c_shape,
        scratch_shapes=[pltpu.VMEM((M_DV, 128), F32)] * HEADS + [pltpu.VMEM((1, 128), F32)] * (2 * HEADS) + c_scratch,
        compiler_params=_cp(2),
    )(qk3, qk3, proj3, proj3, proj3, gain, gbias, *gather)


def mlstm_cell_bwd(qk3, proj3, gain, gbias, dy3, states, name, exchange=()):
    Bl, S, _ = qk3.shape
    L = M_CHUNK
    SB = min(M_SLAB, S)
    cpb, nsb = SB // L, S // SB
    scale = M_DQK ** -0.5

    def body(q_ref, k_ref, v_ref, o_ref, g_ref, gain_ref, gb_ref, cst_ref, nst_ref, mst_ref, dy_ref,
             dq_ref, dk_ref, dv_ref, do_ref, dg_ref, dgain_ref, dgb_ref, *state):
        dC_s, dn_s, dgain_s, dgb_s = state[:HEADS], state[HEADS:2 * HEADS], state[2 * HEADS:3 * HEADS], state[-1]
        s = pl.program_id(1)

        @pl.when(s == 0)
        def _():
            for ref in state:
                ref[...] = jnp.zeros_like(ref)
        lane = lax.broadcasted_iota(jnp.int32, (L, 128), 1)
        rowi = lax.broadcasted_iota(jnp.int32, (L, 1), 0)
        eye, low, upp = _cell_consts(L)

        def bstep(t, carry):
            c = cpb - 1 - t
            off = pl.multiple_of(c * L, L)
            G = g_ref[0, pl.ds(off, L), :] + gb_ref[...]
            shared = dict(slab=jnp.zeros((L, 128), F32))

            def head(h):
                cols = pl.ds(M_DV * h, M_DV)
                gain_h = gain_ref[:, cols]
                C, n, m = cst_ref[0, h, c], nst_ref[0, h, c], mst_ref[0, h, c][:, 0:1]
                dC_n, dn_n = dC_s[h][...], dn_s[h][...]
                qmask, qc, kc, vc, gi, gf = _load_chunk(q_ref, k_ref, v_ref, G, off, L, h, lane)
                gi_row, bcum, bcum_row, b_last = _gate_rows(gi, gf, eye, low, upp)
                qs = qc * scale
                _, _, _, wa, sp, sl = _finish(_chunk_state(kc, vc, gi, bcum, b_last, C, n, m))
                dcl_b = (sl * dC_n).astype(BF16)
                t1_mm = _dot(vc.astype(BF16), dcl_b)
                dv_mm = _dot_nt(kc.astype(BF16), dcl_b)
                r = yield from _chunk_out(qs, kc, vc, gi_row, bcum, bcum_row, low, C, n, m)
                hc, den, dn, sc, dm, iw, qn = r["hc"], r["den"], r["dn"], r["sc"], r["dm"], r["iw"], r["qn"]
                qs_b, k_b, v_b = r["qs_b"], r["k_b"], r["v_b"]
                dy = dy_ref[0, pl.ds(off, L), cols].astype(F32)
                oc = o_ref[0, pl.ds(off, L), cols]
                sig_o = _sigmoid(oc)
                rr = lax.rsqrt(jnp.mean(hc * hc, axis=-1, keepdims=True) + RMS_EPS)
                hn = hc * rr
                dgain_s[h][...] += _sum0(dy * sig_o * hn)
                do_ref[0, pl.ds(off, L), cols] = (
                    dy * hn * gain_h * sig_o * (1.0 - sig_o)).astype(BF16)
                dhn = dy * sig_o * gain_h
                dhc = rr * dhn - hc * (rr * rr * rr) * jnp.mean(dhn * hc, axis=-1, keepdims=True)
                dnum = dhc / dn
                gden = -_sum1(dhc * hc) / dn
                dden = jnp.where(jnp.abs(den) > r["floor"], gden * jnp.sign(den), 0.0)
                dnum_b = dnum.astype(BF16)
                dqc_b = (iw * dnum).astype(BF16)
                dsc_mm = _dot_nt(dnum_b, v_b)
                dv = _dot_tn(sc.astype(BF16), dnum_b)
                dqs_mm = _dot(dqc_b, C.astype(BF16))
                dC_out = _dot_tn(dqc_b, qs_b)
                diw = _sum1(dnum * r["qc"]) + dden * qn
                wq = iw * dden
                dn_out = _sum0(wq * qs)
                dn_loc = sl * dn_n
                dsp = _sum1(_sum0(dC_n * C)) + _sum1(dn_n * n)
                yield
                dsc = dsc_mm + dden
                dS_b = (dsc * dm).astype(BF16)
                gm = dsc * sc
                dqs2_mm = _dot(dS_b, k_b)
                dk = _dot_tn(dS_b, qs_b)
                dqs = dqs_mm + wq * n
                dbc = _sum1(gm) + diw * iw
                colg = _sum0(gm)
                dC_p = sp * dC_n + dC_out
                dn_p = sp * dn_n + dn_out
                db_last = dsp * sp
                t1 = t1_mm + dn_loc
                dwa = _sum1(t1 * kc)
                dv = dv + wa * dv_mm
                yield
                dqs = dqs + dqs2_mm
                dk = dk + wa * t1
                da = dwa * wa
                db_last = db_last + _sum0(da)
                dbc = dbc - da + jnp.where(rowi == L - 1, db_last, 0.0)
                dbc_row = _sum0(jnp.where(eye, dbc, 0.0)) - colg
                dgi = da + _sum1(jnp.where(eye, colg, 0.0))
                dlf = _sum1(jnp.where(upp, dbc_row, 0.0))
                dgf = dlf * _sigmoid(-gf)
                dq = jnp.where(qmask, dqs * scale, 0.0)
                dk = jnp.where(qmask, dk, 0.0)
                shared["slab"] = (shared["slab"] + jnp.where(lane == h, dgi, 0.0)
                                  + jnp.where(lane == h + HEADS, dgf, 0.0))
                dv_ref[0, pl.ds(off, L), cols] = dv.astype(BF16)
                dC_s[h][...] = dC_p
                dn_s[h][...] = dn_p
                if h % 2 == 0:
                    shared["dq"], shared["dk"] = dq, dk
                else:
                    pair = pl.ds(128 * (h // 2), 128)
                    dq_ref[0, pl.ds(off, L), pair] = shared["dq"] + dq
                    dk_ref[0, pl.ds(off, L), pair] = shared["dk"] + dk

            _interleave(head(h) for h in range(HEADS))
            dg_ref[0, pl.ds(off, L), :] = shared["slab"]
            dgb_s[...] += _sum0(shared["slab"])
            return carry

        lax.fori_loop(0, cpb, bstep, 0)

        @pl.when(s == nsb - 1)
        def _():
            for h in range(HEADS):
                dgain_ref[0, :, pl.ds(M_DV * h, M_DV)] = dgain_s[h][...]
            dgb_ref[0] = dgb_s[...]

    ins, states_specs, seq = _cell_specs(SB, cpb, lambda s: nsb - 1 - s)
    once = lambda width: pl.BlockSpec((1, 1, width), lambda b, s: (b, 0, 0))
    grid = (Bl, nsb)
    body, c_in, c_out, c_shape, c_scratch = host_comm(body, grid, 11, 7, exchange=exchange)
    return pl.pallas_call(
        body, name=name, grid=grid, in_specs=ins + states_specs + [seq(D, 0)] + c_in,
        out_specs=[seq(D // 2, 0), seq(D // 2, 0), seq(D, 0), seq(D, 0), seq(128, 0), once(D), once(128)] + c_out,
        out_shape=[jax.ShapeDtypeStruct((Bl, S, D // 2), F32), jax.ShapeDtypeStruct((Bl, S, D // 2), F32),
                   jax.ShapeDtypeStruct((Bl, S, D), BF16), jax.ShapeDtypeStruct((Bl, S, D), BF16),
                   jax.ShapeDtypeStruct((Bl, S, 128), F32), jax.ShapeDtypeStruct((Bl, 1, D), F32),
                   jax.ShapeDtypeStruct((Bl, 1, 128), F32)] + c_shape,
        scratch_shapes=[pltpu.VMEM((M_DV, 128), F32)] * HEADS + [pltpu.VMEM((1, 128), F32)] * (2 * HEADS + 1)
        + c_scratch,
        compiler_params=_cp(2),
    )(qk3, qk3, proj3, proj3, proj3, gain, gbias, *states, dy3, *exchange)


def _attn_scores(q, kc, kp, n, row, col, scale):
    s_c = jnp.where(col <= row, _dot_nt(q, kc) * scale, NEG)
    s_p = jnp.where(jnp.logical_and(col >= row, n > 0), _dot_nt(q, kp) * scale, NEG)
    return s_c, s_p


def _to_streams(src, dst, tmp, dil, Sd):
    if dil == 1:
        dst[...] = src[...].astype(dst.dtype)
        return
    if src.dtype != F32:
        tmp[...] = src[...].astype(F32)
        src = tmp
    for r in range(dil):
        dst[pl.ds(r * Sd, Sd), :] = src[pl.ds(r, Sd, stride=dil), :].astype(dst.dtype)


def _from_streams(src, dst, dil, Sd):
    if dil == 1:
        dst[...] = src[...]
        return
    for r in range(dil):
        dst[pl.ds(r, Sd, stride=dil), :] = src[pl.ds(r * Sd, Sd), :]


def attn_fwd(proj, Bl, S, name):
    scale = A_BLK ** -0.5
    pv = proj.reshape(Bl, S, A_PROJ)
    ng = len(DIL_GROUPS)
    rows = 512

    def body(*refs):
        ins, (ob_ref, of_ref, lt_ref) = refs[:3 * ng], refs[3 * ng:3 * ng + 3]
        tmp, qs, ks, vs, os_, ls = refs[3 * ng + 3:3 * ng + 9]
        o_nat, l_nat = refs[3 * ng + 9:4 * ng + 9], refs[4 * ng + 9:]
        row = lax.broadcasted_iota(jnp.int32, (A_BLK, A_BLK), 0)
        col = lax.broadcasted_iota(jnp.int32, (A_BLK, A_BLK), 1)
        for g, (_, dil) in enumerate(DIL_GROUPS):
            Sd = S // dil
            nb = Sd // A_BLK
            for src, dst in zip(ins[3 * g:3 * g + 3], (qs, ks, vs)):
                _to_streams(src.at[0], dst, tmp, dil, Sd)

            def step(i, carry, nb=nb):
                n = i % nb
                off = pl.multiple_of(i * A_BLK, A_BLK)
                offp = pl.multiple_of(jnp.maximum(i - 1, 0) * A_BLK, A_BLK)
                q = qs[pl.ds(off, A_BLK), :]
                s_c, s_p = _attn_scores(q, ks[pl.ds(off, A_BLK), :], ks[pl.ds(offp, A_BLK), :], n, row, col, scale)
                m = jnp.maximum(jnp.max(s_c, axis=1, keepdims=True), jnp.max(s_p, axis=1, keepdims=True))
                p_c = jnp.exp(s_c - m)
                p_p = jnp.exp(s_p - m)
                den = _sum1(p_c) + _sum1(p_p)
                o = (_dot(p_c.astype(BF16), vs[pl.ds(off, A_BLK), :])
                     + _dot(p_p.astype(BF16), vs[pl.ds(offp, A_BLK), :]))
                os_[pl.ds(off, A_BLK), :] = o / den
                ls[pl.ds(off, A_BLK), :] = jnp.broadcast_to(m + jnp.log(den), (A_BLK, 128))
                return carry

            lax.fori_loop(0, dil * nb, step, 0, unroll=A_UNROLL)
            _from_streams(os_, o_nat[g], dil, Sd)
            _from_streams(ls, l_nat[g], dil, Sd)

        def merge(t, carry):
            sl = pl.ds(pl.multiple_of(t * rows, rows), rows)
            lses = [l[sl, :] for l in l_nat]
            m = functools.reduce(jnp.maximum, lses)
            ws = [jnp.exp(l - m) for l in lses]
            den = functools.reduce(lambda a, b: a + b, ws)
            o = functools.reduce(lambda a, b: a + b, [w * r[sl, :] for w, r in zip(ws, o_nat)]) / den
            of_ref[0, sl, :] = o
            ob_ref[0, sl, :] = o.astype(BF16)
            lt_ref[0, sl, :] = m + jnp.log(den)
            return carry

        lax.fori_loop(0, S // rows, merge, 0)

    in_specs = [pl.BlockSpec((1, S, 128), lambda b, h, c=g * 24 + j * HEADS: (b, 0, c + h))
                for g in range(ng) for j in range(3)]
    ospec = pl.BlockSpec((1, S, 128), lambda b, h: (b, 0, h))
    slab = lambda dt: pltpu.VMEM((S, 128), dt)
    outs = pl.pallas_call(
        body, name=name, grid=(Bl, HEADS), in_specs=in_specs, out_specs=[ospec] * 3,
        out_shape=[jax.ShapeDtypeStruct((Bl, S, D), BF16), jax.ShapeDtypeStruct((Bl, S, D), F32),
                   jax.ShapeDtypeStruct((Bl, S, D), F32)],
        scratch_shapes=[slab(F32)] + [slab(BF16)] * 3 + [slab(F32)] * (2 + 2 * ng),
        compiler_params=_cp(2),
    )(*([pv] * (3 * ng)))
    return [t.reshape(Bl * S, D) for t in outs]


def attn_bwd(proj, do, o, lse, Bl, S, g, dil, name):
    Sd = S // dil
    nb = Sd // A_BLK
    scale = A_BLK ** -0.5
    pv = proj.reshape(Bl, S, A_PROJ)
    dov, ov, lv = (t.reshape(Bl, S, D) for t in (do, o, lse))

    def body(q_ref, k_ref, v_ref, do_ref, o_ref, l_ref, dq_ref, dk_ref, dv_ref,
             tmp, qs, ks, vs, dos, dls, lts, dq_s, dk_s, dv_s):
        row = lax.broadcasted_iota(jnp.int32, (A_BLK, A_BLK), 0)
        col = lax.broadcasted_iota(jnp.int32, (A_BLK, A_BLK), 1)
        for src, dst in ((q_ref, qs), (k_ref, ks), (v_ref, vs), (do_ref, dos), (l_ref, lts)):
            _to_streams(src.at[0], dst, tmp, dil, Sd)
        tmp[...] = jnp.broadcast_to(_sum1(do_ref[0].astype(F32) * o_ref[0]), (S, 128))
        _to_streams(tmp, dls, None, dil, Sd)
        dk_s[...] = jnp.zeros_like(dk_s)
        dv_s[...] = jnp.zeros_like(dv_s)

        def step(i, carry):
            n = i % nb
            off = pl.multiple_of(i * A_BLK, A_BLK)
            offp = pl.multiple_of(jnp.maximum(i - 1, 0) * A_BLK, A_BLK)
            q = qs[pl.ds(off, A_BLK), :]
            kc, kp = ks[pl.ds(off, A_BLK), :], ks[pl.ds(offp, A_BLK), :]
            vc, vp = vs[pl.ds(off, A_BLK), :], vs[pl.ds(offp, A_BLK), :]
            do_b = dos[pl.ds(off, A_BLK), :]
            delta = dls[pl.ds(off, A_BLK), :][:, 0:1]
            lt = lts[pl.ds(off, A_BLK), :][:, 0:1]
            s_c, s_p = _attn_scores(q, kc, kp, n, row, col, scale)
            p_c = jnp.exp(s_c - lt)
            p_p = jnp.exp(s_p - lt)
            ds_c = (p_c * (_dot_nt(do_b, vc) - delta) * scale).astype(BF16)
            ds_p = (p_p * (_dot_nt(do_b, vp) - delta) * scale).astype(BF16)
            dq_s[pl.ds(off, A_BLK), :] = _dot(ds_c, kc) + _dot(ds_p, kp)
            dk_s[pl.ds(off, A_BLK), :] += _dot_tn(ds_c, q)
            dk_s[pl.ds(offp, A_BLK), :] += _dot_tn(ds_p, q)
            dv_s[pl.ds(off, A_BLK), :] += _dot_tn(p_c.astype(BF16), do_b)
            dv_s[pl.ds(offp, A_BLK), :] += _dot_tn(p_p.astype(BF16), do_b)
            return carry

        lax.fori_loop(0, dil * nb, step, 0, unroll=A_UNROLL)
        for src, dst in ((dq_s, dq_ref), (dk_s, dk_ref), (dv_s, dv_ref)):
            _from_streams(src, tmp, dil, Sd)
            dst[0] = tmp[...].astype(BF16)

    def spec(j):
        return pl.BlockSpec((1, S, 128), lambda b, h: (b, 0, g * 24 + j * HEADS + h))

    ospec = pl.BlockSpec((1, S, 128), lambda b, h: (b, 0, h))
    slab = lambda dt: pltpu.VMEM((S, 128), dt)
    outs = pl.pallas_call(
        body, name=name, grid=(Bl, HEADS),
        in_specs=[spec(0), spec(1), spec(2), ospec, ospec, ospec], out_specs=[ospec] * 3,
        out_shape=[jax.ShapeDtypeStruct((Bl, S, D), BF16)] * 3,
        scratch_shapes=[slab(F32)] + [slab(BF16)] * 4 + [slab(F32)] * 5,
        compiler_params=_cp(2),
    )(pv, pv, pv, dov, ov, lv)
    return [t.reshape(Bl * S, D) for t in outs]


def _as_slots(pair, shape):
    return tuple(t.reshape(shape) for t in pair)


def ffn_fwd(x, mod3, w_in, w_out, lng, lnb, tag, gather=()):
    a, g, u, h, gathered = ffn_in(x, mod3, w_in, tag + "_in", gather=gather)
    out, xn = proj_post(a, w_out, x, mod3, lng, lnb, 0.5, tag + "_out")
    return xn, (x, out, g, u, h, a), gathered


def ffn_bwd(dxn, saved, mod3, w_in, w_out, lng, tag, exchange=((), (), ()), exchange_own=False):
    x, out, g, u, h, a = saved
    dxres, dout, dgu, dlg, dlb, dgate, *got0 = post_bwd(dxn, x, out, mod3, lng, w_out, 0.5, tag + "_outb",
                                                        tm=256, gu=(g, u), exchange=exchange[0])
    *dw_in, got1 = mm_tn(h, dgu, tag + "_dwin", a_copies=True, exchange=exchange[1]) + (() if exchange[1] else ([],))
    *dw_out, got2 = mm_tn(a, dout, tag + "_dwout", bw=D, exchange=exchange[2]) + (() if exchange[2] else ([],))
    dw_out = _as_slots(dw_out, (N_DEV, D_FF // N_DEV, D))
    dx, dsh, dsc, *own = modmm_bwd(dgu, w_in, x, mod3, dxres, tag + "_inb", tm=256,
                                   exchange=[dw_in[1], dw_out[1]] if exchange_own else ())
    dmod3 = jnp.concatenate([dsh, dsc, dgate], axis=1)
    return (dx, [tuple(dw_in), dw_out], dlg, dlb, dmod3,
            [got0[0] if got0 else [], got1, got2], (own[0] if own else []))


def mlstm_fwd(x, mod3, w_in, w_out, conv_w, gain, gbias, lng, lnb, Bl, S, gather=()):
    proj, h = modmm(x, mod3, w_in, F32, "ml_in", tn=M_PROJ_PAD // 5)
    proj3 = proj.reshape(Bl, S, M_PROJ_PAD)
    qk3 = conv_silu(proj3, conv_w, "ml_conv")
    y3, *rest = mlstm_cell_fwd(qk3, proj3, gain, gbias, "ml_cell", gather=gather)
    states, gathered = rest[:3], rest[3:]
    y = y3.reshape(Bl * S, D)
    out, xn = proj_post(y[None], w_out, x, mod3, lng, lnb, 1.0, "ml_out")
    return xn, (x, out, h, proj3, qk3, y, states), gathered


def mlstm_bwd(dxn, saved, mod3, w_in, w_out, conv_w, gain, gbias, lng, Bl, S, exchange=()):
    x, out, h, proj3, qk3, y, states = saved
    dxres, dout, dy, dlg, dlb, dgate = post_bwd(dxn, x, out, mod3, lng, w_out, 1.0, "ml_outb")
    dq, dk, dv, do, dg, dgain, dgb, *received = mlstm_cell_bwd(qk3, proj3, gain, gbias, dy.reshape(Bl, S, D),
                                                               states, "ml_cellb", exchange=exchange)
    dqk, dconv = conv_silu_bwd(proj3, conv_w, dq, dk, "ml_convb")
    dproj = jnp.concatenate([dqk, dv, do, dg.astype(BF16)], axis=2).reshape(Bl * S, M_PROJ_PAD)
    dx, dsh, dsc = modmm_bwd(dproj, w_in, x, mod3, dxres, "ml_inb", tn=M_PROJ_PAD // 5)
    dwi, _ = mm_tn(h, dproj, "ml_dwin", bw=M_PROJ_PAD // 5)
    dwi = _restack(jnp.moveaxis(dwi, 0, 1).reshape(D, M_PROJ_PAD)[:, :M_PROJ], 1)
    dw_out = _as_slots(mm_tn(y, dout, "ml_dwout", bw=D), (N_DEV, D // N_DEV, D))
    small = (jnp.sum(dconv, axis=0), jnp.sum(dgain, axis=0), jnp.sum(dgb, axis=0)[:, :2 * HEADS])
    dmod3 = jnp.concatenate([dsh, dsc, dgate], axis=1)
    return dx, [(dwi, dwi.astype(BF16)), dw_out], dlg, dlb, dmod3, small, received


def attn_mixer_fwd(x, mod3, w_in, w_out, lng, lnb, Bl, S):
    proj, h = modmm(x, mod3, w_in, BF16, "at_in")
    ob, of, lt = attn_fwd(proj, Bl, S, "at_core")
    out, xn = proj_post(ob[None], w_out, x, mod3, lng, lnb, 1.0, "at_out")
    return xn, (x, out, h, proj, ob, of, lt)


def attn_mixer_bwd(dxn, saved, mod3, w_in, w_out, lng, Bl, S):
    x, out, h, proj, ob, of, lt = saved
    dxres, dout, do, dlg, dlb, dgate = post_bwd(dxn, x, out, mod3, lng, w_out, 1.0, "at_outb")
    do = do[0]
    parts = []
    for g, (_, dil) in enumerate(DIL_GROUPS):
        parts += attn_bwd(proj, do, of, lt, Bl, S, g, dil, "at_coreb%d" % g)
    dproj = jnp.concatenate(parts, axis=1)
    dx, dsh, dsc = modmm_bwd(dproj, w_in, x, mod3, dxres, "at_inb")
    dw_in = mm_tn(h, dproj, "at_dwin", bw=w_in.shape[2])
    dw_out = _as_slots(mm_tn(ob, dout, "at_dwout", bw=D), (N_DEV, D // N_DEV, D))
    return dx, [dw_in, dw_out], dlg, dlb, jnp.concatenate([dsh, dsc, dgate], axis=1)


def _unstack(stacked, axis):
    full = jnp.moveaxis(stacked, 0, axis)
    shp = list(full.shape)
    shp[axis:axis + 2] = [shp[axis] * shp[axis + 1]]
    return full.reshape(shp)


def _restack(full, axis):
    shp = list(full.shape)
    shp[axis:axis + 1] = [N_DEV, shp[axis] // N_DEV]
    return jnp.moveaxis(full.reshape(shp), axis, 0)


def kernel(x, c, ada_w, ada_b, ln_g, ln_b, ffn_w_in, ffn_w_out, mlstm_w_in, mlstm_gate_bias, mlstm_conv_w, mlstm_head_gain, mlstm_w_out, attn_w_in, attn_w_out, loss_target, m_ada_w, m_ada_b, m_ln_g, m_ln_b, m_ffn_w_in, m_ffn_w_out, m_mlstm_w_in, m_mlstm_gate_bias, m_mlstm_conv_w, m_mlstm_head_gain, m_mlstm_w_out, m_attn_w_in, m_attn_w_out, v_ada_w, v_ada_b, v_ln_g, v_ln_b, v_ffn_w_in, v_ffn_w_out, v_mlstm_w_in, v_mlstm_gate_bias, v_mlstm_conv_w, v_mlstm_head_gain, v_mlstm_w_out, v_attn_w_in, v_attn_w_out):
    Bl, S, _ = x.shape
    T = Bl * S
    Bg = Bl * N_DEV
    me = 4 * lax.axis_index("x") + 2 * lax.axis_index("y") + lax.axis_index("c")
    onehot = (jnp.arange(N_DEV) == me).astype(F32)
    weights = dict(ada_w=ada_w, ada_b=ada_b, ln_g=ln_g, ln_b=ln_b, ffn_w_in=ffn_w_in, ffn_w_out=ffn_w_out,
                   mlstm_w_in=mlstm_w_in, mlstm_gate_bias=mlstm_gate_bias, mlstm_conv_w=mlstm_conv_w,
                   mlstm_head_gain=mlstm_head_gain, mlstm_w_out=mlstm_w_out, attn_w_in=attn_w_in,
                   attn_w_out=attn_w_out)
    m_in = dict(ada_w=m_ada_w, ada_b=m_ada_b, ln_g=m_ln_g, ln_b=m_ln_b, ffn_w_in=m_ffn_w_in,
                ffn_w_out=m_ffn_w_out, mlstm_w_in=m_mlstm_w_in, mlstm_gate_bias=m_mlstm_gate_bias,
                mlstm_conv_w=m_mlstm_conv_w, mlstm_head_gain=m_mlstm_head_gain, mlstm_w_out=m_mlstm_w_out,
                attn_w_in=m_attn_w_in, attn_w_out=m_attn_w_out)
    v_in = dict(ada_w=v_ada_w, ada_b=v_ada_b, ln_g=v_ln_g, ln_b=v_ln_b, ffn_w_in=v_ffn_w_in,
                ffn_w_out=v_ffn_w_out, mlstm_w_in=v_mlstm_w_in, mlstm_gate_bias=v_mlstm_gate_bias,
                mlstm_conv_w=v_mlstm_conv_w, mlstm_head_gain=v_mlstm_head_gain, mlstm_w_out=v_mlstm_w_out,
                attn_w_in=v_attn_w_in, attn_w_out=v_attn_w_out)

    mixer = ("mlstm", "attn")
    shards = [[ffn_w_in[layer, 0], ffn_w_in[layer, 1], ffn_w_out[layer, 0], ffn_w_out[layer, 1],
               weights[mixer[layer] + "_w_in"][0], weights[mixer[layer] + "_w_out"][0]] for layer in range(DEPTH)]
    sends = [[s.astype(BF16) for s in layer_shards] for layer_shards in shards]
    small = jnp.concatenate([c.reshape(-1), ln_g.reshape(-1), ln_b.reshape(-1), mlstm_conv_w.reshape(-1)])
    n_small = small.shape[0]
    small = jnp.pad(small, (0, -n_small % (8 * PACK_COLS))).reshape(-1, PACK_COLS)

    def gathered_weights(g):
        return ((g[0], g[1]), (g[2].reshape(4, D_FF // 4, D), g[3].reshape(4, D_FF // 4, D)), g[4],
                g[5].reshape(1, D, D))

    first_in, small_all = all_gather([sends[0][0], small], "ag_params")
    full = [None, None]
    small_flat = small_all.reshape(N_DEV, -1)
    o0 = 0
    c_all = small_flat[:, o0:o0 + c.size].reshape(Bg, D)
    o0 += c.size
    lng_full = _unstack(small_flat[:, o0:o0 + ln_g.size].reshape((N_DEV,) + ln_g.shape), 2)
    o0 += ln_g.size
    lnb_full = _unstack(small_flat[:, o0:o0 + ln_b.size].reshape((N_DEV,) + ln_b.shape), 2)
    o0 += ln_b.size
    conv_full = _unstack(small_flat[:, o0:o0 + mlstm_conv_w.size].reshape((N_DEV,) + mlstm_conv_w.shape), 2)[0]
    gbias =jnp.pad(mlstm_gate_bias, ((0, 0), (0, 128 - 2 * HEADS)))

    ncols = ada_w.shape[2]
    ada_b_cols = lax.dynamic_slice_in_dim(ada_b, me * ncols, ncols, axis=1).reshape(DEPTH, 1, ncols)
    mod_cols = ada_fwd(c_all, ada_w, ada_b_cols, "ada_fwd")
    (mod_g,) = all_gather([mod_cols.reshape(DEPTH * Bg, ncols)], "ag_mod")
    mod_full = _unstack(mod_g.reshape(N_DEV, DEPTH, Bg, ncols), 2)
    mod_mine = lax.dynamic_slice_in_dim(mod_full, me * Bl, Bl, axis=1).reshape(DEPTH, Bl, 3, 3, D)

    xt = x.reshape(T, D)
    saved = []
    for layer in range(DEPTH):
        def lnp(s, layer=layer):
            return lng_full[layer, s].reshape(1, D), lnb_full[layer, s].reshape(1, D)
        md = mod_mine[layer]
        if layer == 0:
            a, g, u, h, late = ffn_in(xt, md[:, 0], first_in, "f0a_in", gather=sends[0][1:4])
            out, xn, later = proj_post(a, late[1].reshape(4, D_FF // 4, D), xt, md[:, 0], *lnp(0), 0.5, "f0a_out",
                                       gather=sends[0][4:])
            full[0] = gathered_weights([first_in] + late + later)
            xt, sv0 = xn, (xt, out, g, u, h, a)
            mw_in = jnp.pad(_unstack(full[0][2], 1), ((0, 0), (0, M_PROJ_PAD - M_PROJ)))
        else:
            xt, sv0, _ = ffn_fwd(xt, md[:, 0], full[layer][0][0], full[layer][1][0], *lnp(0), "f%da" % layer)
        f_in, f_out, mix_in, mix_out = full[layer]
        if layer % 2 == 0:
            xt, sv1, g1 = mlstm_fwd(xt, md[:, 1], mw_in, mix_out, conv_full, mlstm_head_gain, gbias, *lnp(1), Bl, S,
                                    gather=sends[1])
            full[1] = gathered_weights(g1)
        else:
            xt, sv1 = attn_mixer_fwd(xt, md[:, 1], mix_in, mix_out, *lnp(1), Bl, S)
        xt, sv2, _ = ffn_fwd(xt, md[:, 2], f_in[1], f_out[1], *lnp(2), "f%db" % layer)
        saved.append((sv0, sv1, sv2))

    dxt, lsum = loss_head(xt, loss_target.reshape(T, D), "loss")
    loss = lax.psum(lsum[0, 0], MESH_AXES)

    dmod, dlg_all, dlb_all = [None] * DEPTH, [None] * DEPTH, [None] * DEPTH
    wgrads = [None] * DEPTH
    recvs = [[None] * 6 for _ in range(DEPTH)]
    ml_small = None
    for layer in reversed(range(DEPTH)):
        md = mod_mine[layer]
        f_in, f_out, mix_in, mix_out = full[layer]
        sv0, sv1, sv2 = saved[layer]
        dxt, dw2, dlg2, dlb2, dm2, _, _ = ffn_bwd(dxt, sv2, md[:, 2], f_in[1], f_out[1],
                                                  lng_full[layer, 2].reshape(1, D), "f%db" % layer)
        lg1 = lng_full[layer, 1].reshape(1, D)
        if layer % 2 == 0:
            dxt, dw1, dlg1, dlb1, dm1, ml_small, got = mlstm_bwd(
                dxt, sv1, md[:, 1], mw_in, mix_out, conv_full, mlstm_head_gain, gbias, lg1, Bl, S,
                exchange=[b16 for _, b16 in wgrads[1]])
            recvs[1] = got
            dxt, dw0, dlg0, dlb0, dm0, got, own = ffn_bwd(
                dxt, sv0, md[:, 0], f_in[0], f_out[0], lng_full[layer, 0].reshape(1, D), "f%da" % layer,
                exchange=([dw2[0][1]], [dw2[1][1], dw1[0][1]], [dw1[1][1]]), exchange_own=True)
            (recvs[0][1],), (recvs[0][3], recvs[0][4]), (recvs[0][5],) = got
            recvs[0][0], recvs[0][2] = own
        else:
            dxt, dw1, dlg1, dlb1, dm1 = attn_mixer_bwd(dxt, sv1, md[:, 1], mix_in, mix_out, lg1, Bl, S)
            dxt, dw0, dlg0, dlb0, dm0, _, _ = ffn_bwd(dxt, sv0, md[:, 0], f_in[0], f_out[0],
                                                      lng_full[layer, 0].reshape(1, D), "f%da" % layer)
        wgrads[layer] = [dw0[0], dw2[0], dw0[1], dw2[1], dw1[0], dw1[1]]
        dmod[layer] = jnp.stack([dm0, dm1, dm2], axis=1).reshape(Bl, 9 * D)
        dlg_all[layer] = jnp.concatenate([dlg0, dlg1, dlg2], axis=0)
        dlb_all[layer] = jnp.concatenate([dlb0, dlb1, dlb2], axis=0)
    grad_x = dxt.reshape(Bl, S, D)

    gsh = [[shard_sum(lax.dynamic_index_in_dim(f32, me, axis=0, keepdims=False), recv, onehot,
                      "rs_sum%d_%d" % (layer, i))
            for i, ((f32, _), recv) in enumerate(zip(wgrads[layer], recvs[layer]))] for layer in range(DEPTH)]
    grads = {"ffn_w_in": jnp.stack([jnp.stack(g[0:2]) for g in gsh]),
             "ffn_w_out": jnp.stack([jnp.stack(g[2:4]) for g in gsh]),
             "mlstm_w_in": gsh[0][4][None], "mlstm_w_out": gsh[0][5][None],
             "attn_w_in": gsh[1][4][None], "attn_w_out": gsh[1][5][None]}

    dconv, dgain, dgbias = ml_small
    parts = [jnp.stack(dmod).reshape(-1), dgbias.reshape(-1), dgain.reshape(-1),
             jnp.stack(dlg_all).reshape(-1), jnp.stack(dlb_all).reshape(-1), dconv.reshape(-1)]
    sizes = [p.shape[0] for p in parts]
    flat = jnp.concatenate(parts)
    flat = jnp.pad(flat, (0, -flat.shape[0] % (8 * PACK_COLS))).reshape(-1, PACK_COLS)
    (sm_all,) = all_gather([flat], "ag_small")
    sm_sum = sum_leading(sm_all, "small_sum").reshape(-1)
    dmod_all = sm_all.reshape(N_DEV, -1)[:, :sizes[0]].reshape(N_DEV, DEPTH, Bl, 9 * D)
    dmod_all = jnp.moveaxis(dmod_all, 0, 1).reshape(DEPTH, Bg, 9 * D)
    o0 = sizes[0]
    grads["mlstm_gate_bias"] = sm_sum[o0:o0 + sizes[1]].reshape(mlstm_gate_bias.shape)
    o0 += sizes[1]
    grads["mlstm_head_gain"] = sm_sum[o0:o0 + sizes[2]].reshape(mlstm_head_gain.shape)
    o0 += sizes[2]
    nl = ln_g.shape[2]
    g_lng = sm_sum[o0:o0 + sizes[3]].reshape(DEPTH, 3, D)
    o0 += sizes[3]
    g_lnb = sm_sum[o0:o0 + sizes[4]].reshape(DEPTH, 3, D)
    o0 += sizes[4]
    g_conv = sm_sum[o0:o0 + sizes[5]].reshape(1, 4, D)
    grads["ln_g"] = lax.dynamic_slice_in_dim(g_lng, me * nl, nl, axis=2)
    grads["ln_b"] = lax.dynamic_slice_in_dim(g_lnb, me * nl, nl, axis=2)
    grads["mlstm_conv_w"] = lax.dynamic_slice_in_dim(g_conv, me * nl, nl, axis=2)
    dmod_cols = lax.dynamic_slice_in_dim(dmod_all, me * ncols, ncols, axis=2)
    gw, gb = ada_bwd(c_all.T, dmod_cols, dmod_all, "ada_bwd")
    grads["ada_w"] = gw
    grads["ada_b"] = gb.reshape(ada_b.shape)

    names = ["ada_w", "ada_b", "ln_g", "ln_b", "ffn_w_in", "ffn_w_out", "mlstm_w_in", "mlstm_gate_bias",
             "mlstm_conv_w", "mlstm_head_gain", "mlstm_w_out", "attn_w_in", "attn_w_out"]
    deltas, new_m, new_v = [], [], []
    for k in names:
        w = weights[k]
        shp2 = (math.prod(w.shape[:-1]), w.shape[-1])
        d_, m_, v_ = adamw(w.reshape(shp2), grads[k].reshape(shp2), m_in[k].reshape(shp2), v_in[k].reshape(shp2),
                           "adamw_" + k)
        deltas.append(d_.reshape(w.shape))
        new_m.append(m_.reshape(w.shape))
        new_v.append(v_.reshape(w.shape))
    return (loss, grad_x, *[grads[k] for k in names], *deltas, *new_m, *new_v)
```

```python
import functools
import math

import jax
import jax.numpy as jnp
from jax import lax
from jax.experimental import pallas as pl
from jax.experimental.pallas import tpu as pltpu

F32 = jnp.float32
BF16 = jnp.bfloat16

N_DEV = 8
MESH_AXES = ("x", "y", "c")
D = 1024
DEPTH = 2
D_FF = 2816
HEADS = 8
M_DQK = 64
M_DV = 128
M_CHUNK = 64
M_SLAB = 512
M_PROJ = 3088
M_PROJ_PAD = 3200
A_PROJ = 9216
DIL_GROUPS = ((128, 1), (512, 4), (2048, 16))
A_BLK = 128
A_UNROLL = 16
ALPHA = (2 * DEPTH) ** 0.25
LN_EPS = 1e-5
RMS_EPS = 1e-6
ADAM_LR = 0.001
ADAM_B1 = 0.9
ADAM_B2 = 0.999
ADAM_EPS = 1e-08
ADAM_WD = 0.01
ADAM_STEP = 10
NEG = -1e30
V7X_VMEM_LIMIT = 56 * 1024 * 1024
PACK_COLS = 1024
MESH_ID = pl.DeviceIdType.MESH
ANY_SPEC = pl.BlockSpec(memory_space=pl.ANY)


def _cp(n_axes):
    return pltpu.CompilerParams(dimension_semantics=("arbitrary",) * n_axes,
                                vmem_limit_bytes=V7X_VMEM_LIMIT)


def _dot(a, b):
    return jnp.dot(a, b, preferred_element_type=F32)


def _dot_nt(a, b):
    return lax.dot_general(a, b, (((1,), (1,)), ((), ())), preferred_element_type=F32)


def _dot_tn(a, b):
    return lax.dot_general(a, b, (((0,), (0,)), ((), ())), preferred_element_type=F32)


def _sum0(a):
    return jnp.sum(a, axis=0, keepdims=True)


def _sum1(a):
    return jnp.sum(a, axis=1, keepdims=True)


def _round(a):
    return a.astype(BF16).astype(F32)


def _sigmoid(a):
    return 1.0 / (1.0 + jnp.exp(-a))


def _tile(n, pref):
    t = min(n, pref)
    while n % t:
        t //= 2
    return t


def all_gather(arrs, name):
    n = len(arrs)

    def body(*refs):
        gather = Gather(refs[:n], refs[n:2 * n], *refs[2 * n:])
        gather.start()
        gather.finish()

    return pl.pallas_call(
        body, name=name, out_shape=Gather.out_shape(arrs),
        in_specs=[ANY_SPEC] * n, out_specs=[ANY_SPEC] * n, scratch_shapes=Gather.scratch(n),
    )(*arrs)


class Gather:
    def __init__(self, ins, outs, send_sems, recv_sems, local_sems):
        x, y, c = lax.axis_index("x"), lax.axis_index("y"), lax.axis_index("c")
        me, sibling = (x, y, c), (x, y, 1 - c)
        chips = [(1 - x, y), (x, 1 - y), (1 - x, 1 - y)]

        def slot(a, p):
            return outs[a].at[4 * p[0] + 2 * p[1] + p[2]]

        def copy(a, k, block, to, src=None):
            return pltpu.make_async_remote_copy(
                src_ref=slot(a, block) if src is None else src, dst_ref=slot(a, block),
                send_sem=send_sems.at[7 * a + k], recv_sem=recv_sems.at[7 * a + k],
                device_id=to, device_id_type=MESH_ID)

        n = len(ins)
        self.mine = [pltpu.make_async_copy(ins[a], slot(a, me), local_sems.at[a]) for a in range(n)]
        self.first, self.over_ici, self.passed, self.from_sibling = [], [], [], []
        for a in range(n):
            self.first.append(copy(a, 0, me, sibling, src=ins[a]))
            self.from_sibling.append(copy(a, 0, sibling, me))
            for j, chip in enumerate(chips):
                self.first.append(copy(a, 1 + j, me, (*chip, c), src=ins[a]))
                self.over_ici.append(copy(a, 1 + j, (*chip, c), me))
                self.passed.append(copy(a, 4 + j, (*chip, c), sibling))
                self.from_sibling.append(copy(a, 4 + j, (*chip, 1 - c), me))

    @staticmethod
    def out_shape(arrs):
        return [jax.ShapeDtypeStruct((N_DEV,) + a.shape, a.dtype) for a in arrs]

    @staticmethod
    def scratch(n):
        return [pltpu.SemaphoreType.DMA((7 * n,)), pltpu.SemaphoreType.DMA((7 * n,)),
                pltpu.SemaphoreType.DMA((n,))]

    def start(self):
        for cp in self.mine + self.first:
            cp.start()

    def finish(self):
        for landed, onward in zip(self.over_ici, self.passed):
            landed.wait_recv()
            onward.start()
        for cp in self.from_sibling:
            cp.wait_recv()
        for cp in self.first + self.passed:
            cp.wait_send()
        for cp in self.mine:
            cp.wait()


class Exchange:
    def __init__(self, sends, recvs, send_sems, recv_sems, local_sems):
        x, y, c = lax.axis_index("x"), lax.axis_index("y"), lax.axis_index("c")
        me = 4 * x + 2 * y + c
        self.own = [pltpu.make_async_copy(s.at[me], r.at[me], local_sems.at[a])
                    for a, (s, r) in enumerate(zip(sends, recvs))]
        self.copies = []
        for a, (s_ref, r_ref) in enumerate(zip(sends, recvs)):
            for k in range(1, N_DEV):
                px = 1 - x if (k >> 2) & 1 else x
                py = 1 - y if (k >> 1) & 1 else y
                pc = 1 - c if k & 1 else c
                self.copies.append(pltpu.make_async_remote_copy(
                    src_ref=s_ref.at[4 * px + 2 * py + pc], dst_ref=r_ref.at[me],
                    send_sem=send_sems.at[7 * a + k - 1], recv_sem=recv_sems.at[7 * a + k - 1],
                    device_id=(px, py, pc), device_id_type=MESH_ID))

    @staticmethod
    def scratch(n):
        return [pltpu.SemaphoreType.DMA((7 * n,)), pltpu.SemaphoreType.DMA((7 * n,)),
                pltpu.SemaphoreType.DMA((n,))]

    def start(self):
        for cp in self.own + self.copies:
            cp.start()

    def finish(self):
        for cp in self.copies:
            cp.wait_send()
            cp.wait_recv()
        for cp in self.own:
            cp.wait()


def host_comm(body, grid, n_in, n_out, gather=(), exchange=()):
    ng, nx = len(gather), len(exchange)
    if ng + nx == 0:
        return body, [], [], [], []

    def hosted(*refs):
        ins, c_in, rest = refs[:n_in], refs[n_in:n_in + ng + nx], refs[n_in + ng + nx:]
        outs, c_out, rest = rest[:n_out], rest[n_out:n_out + ng + nx], rest[n_out + ng + nx:]
        n_sems = 3 * ((ng > 0) + (nx > 0))
        scratch, sems = rest[:len(rest) - n_sems], rest[len(rest) - n_sems:]

        def comms():
            made = [Gather(c_in[:ng], c_out[:ng], *sems[:3])] if ng else []
            return made + ([Exchange(c_in[ng:], c_out[ng:], *sems[-3:])] if nx else [])

        ids = [pl.program_id(a) for a in range(len(grid))]

        @pl.when(functools.reduce(jnp.logical_and, [i == 0 for i in ids]))
        def _():
            for cm in comms():
                cm.start()
        body(*ins, *outs, *scratch)

        @pl.when(functools.reduce(jnp.logical_and, [i == g - 1 for i, g in zip(ids, grid)]))
        def _():
            for cm in comms():
                cm.finish()

    shapes = Gather.out_shape(gather) + [jax.ShapeDtypeStruct(a.shape, a.dtype) for a in exchange]
    scratch = (Gather.scratch(ng) if ng else []) + (Exchange.scratch(nx) if nx else [])
    return hosted, [ANY_SPEC] * (ng + nx), [ANY_SPEC] * (ng + nx), shapes, scratch


def shard_sum(own, recv, onehot, name):
    R, C = own.shape
    tr = _tile(R, 512)

    def body(oh_ref, own_ref, recv_ref, o_ref):
        acc = None
        for j in range(N_DEV):
            term = jnp.where(oh_ref[j] > 0.5, own_ref[...], recv_ref[j].astype(F32))
            acc = term if acc is None else acc + term
        o_ref[...] = acc

    return pl.pallas_call(
        body, name=name, grid=(R // tr,),
        in_specs=[pl.BlockSpec(memory_space=pltpu.SMEM),
                  pl.BlockSpec((tr, C), lambda i: (i, 0)),
                  pl.BlockSpec((N_DEV, tr, C), lambda i: (0, i, 0))],
        out_specs=pl.BlockSpec((tr, C), lambda i: (i, 0)),
        out_shape=jax.ShapeDtypeStruct((R, C), F32), compiler_params=_cp(1),
    )(onehot, own, recv)


def sum_leading(a, name):
    _, R, C = a.shape

    def body(a_ref, o_ref):
        acc = a_ref[0]
        for j in range(1, N_DEV):
            acc = acc + a_ref[j]
        o_ref[...] = acc

    return pl.pallas_call(body, name=name, out_shape=jax.ShapeDtypeStruct((R, C), F32),
                          compiler_params=_cp(0))(a)


def _col_chunks(w, tn):
    if w.ndim == 3:
        return w.shape[0], w.shape[2], pl.BlockSpec((None, w.shape[1], w.shape[2]), lambda i, j: (j, 0, 0))
    return w.shape[1] // tn, tn, pl.BlockSpec((w.shape[0], tn), lambda i, j: (0, j))


def modmm(x, mod3, w, out_dtype, name, tn=None):
    T, Dm = x.shape
    nj, tn, w_spec = _col_chunks(w, tn)
    N = nj * tn
    Bl = mod3.shape[0]
    tm = _tile(T // Bl, 1024)
    tpb = T // Bl // tm

    def body(x_ref, mod_ref, w_ref, o_ref, h_ref, hs):
        @pl.when(pl.program_id(1) == 0)
        def _():
            m = mod_ref[0]
            hs[...] = (x_ref[...] * (1.0 + m[1:2, :]) + m[0:1, :]).astype(BF16)
            h_ref[...] = hs[...]
        o_ref[...] = _dot(hs[...], w_ref[...]).astype(o_ref.dtype)

    return pl.pallas_call(
        body, name=name, grid=(T // tm, nj),
        in_specs=[pl.BlockSpec((tm, Dm), lambda i, j: (i, 0)),
                  pl.BlockSpec((1, 3, Dm), lambda i, j: (i // tpb, 0, 0)), w_spec],
        out_specs=[pl.BlockSpec((tm, tn), lambda i, j: (i, j)),
                   pl.BlockSpec((tm, Dm), lambda i, j: (i, 0))],
        out_shape=[jax.ShapeDtypeStruct((T, N), out_dtype), jax.ShapeDtypeStruct((T, Dm), BF16)],
        scratch_shapes=[pltpu.VMEM((tm, Dm), BF16)], compiler_params=_cp(2),
    )(x, mod3, w)


def modmm_bwd(dp, w, x, mod3, dxres, name, tn=None, tm=1024, exchange=()):
    T, Dm = x.shape
    Bl = mod3.shape[0]
    tm = _tile(T // Bl, tm)
    tpb = T // Bl // tm
    resident = dp.ndim == 3
    if resident:
        nc, nj = dp.shape[0], 1
        dp_spec = pl.BlockSpec((nc, tm, dp.shape[2]), lambda i, j: (0, i, 0))
        w_spec = pl.BlockSpec(w.shape, lambda i, j: (0, 0, 0))
    else:
        nj, tn, w_spec = _col_chunks(w, tn)
        dp_spec = pl.BlockSpec((tm, tn), lambda i, j: (i, j))

    def body(dp_ref, w_ref, x_ref, mod_ref, dxr_ref, dx_ref, dsh_ref, dsc_ref, acc):
        i, j = pl.program_id(0), pl.program_id(1)

        @pl.when(j == 0)
        def _():
            acc[...] = jnp.zeros_like(acc)
        if resident:
            for c in range(nc):
                acc[...] += _dot_nt(dp_ref[c], w_ref[c])
        else:
            acc[...] += _dot_nt(dp_ref[...], w_ref[...])

        @pl.when(j == nj - 1)
        def _():
            dh = acc[...]
            xx = x_ref[...]
            dx_ref[...] = dxr_ref[...] + dh * (1.0 + mod_ref[0][1:2, :])

            @pl.when(i % tpb == 0)
            def _():
                dsh_ref[...] = jnp.zeros_like(dsh_ref)
                dsc_ref[...] = jnp.zeros_like(dsc_ref)
            dsh_ref[0] += _sum0(dh)
            dsc_ref[0] += _sum0(dh * xx)

    grid = (T // tm, nj)
    body, c_in, c_out, c_shape, c_scratch = host_comm(body, grid, 5, 3, exchange=exchange)
    dx, dsh, dsc, *received = pl.pallas_call(
        body, name=name, grid=grid,
        in_specs=[dp_spec, w_spec,
                  pl.BlockSpec((tm, Dm), lambda i, j: (i, 0)),
                  pl.BlockSpec((1, 3, Dm), lambda i, j: (i // tpb, 0, 0)),
                  pl.BlockSpec((tm, Dm), lambda i, j: (i, 0))] + c_in,
        out_specs=[pl.BlockSpec((tm, Dm), lambda i, j: (i, 0)),
                   pl.BlockSpec((1, 1, Dm), lambda i, j: (i // tpb, 0, 0)),
                   pl.BlockSpec((1, 1, Dm), lambda i, j: (i // tpb, 0, 0))] + c_out,
        out_shape=[jax.ShapeDtypeStruct((T, Dm), F32), jax.ShapeDtypeStruct((Bl, 1, Dm), F32),
                   jax.ShapeDtypeStruct((Bl, 1, Dm), F32)] + c_shape,
        scratch_shapes=[pltpu.VMEM((tm, Dm), F32)] + c_scratch, compiler_params=_cp(2),
    )(dp, w, x, mod3, dxres, *exchange)
    return (dx, dsh, dsc, received) if exchange else (dx, dsh, dsc)


def _ln_stats(z):
    mu = jnp.mean(z, axis=-1, keepdims=True)
    zc = z - mu
    var = jnp.mean(zc * zc, axis=-1, keepdims=True)
    rstd = lax.rsqrt(var + LN_EPS)
    return zc * rstd, rstd


def proj_post(a, w, x, mod3, lng, lnb, weight, name):
    nk, T, tk = a.shape
    Dm = w.shape[2]
    Bl = mod3.shape[0]
    tm = _tile(T // Bl, 1024 if nk == 1 else 512)
    tpb = T // Bl // tm

    def body(a_ref, w_ref, x_ref, mod_ref, g_ref, b_ref, out_ref, xn_ref):
        out = _dot(a_ref[0], w_ref[0])
        for k in range(1, nk):
            out = out + _dot(a_ref[k], w_ref[k])
        out_ref[...] = out
        z = ALPHA * x_ref[...] + (weight * (1.0 + mod_ref[0][2:3, :])) * out
        xhat, _ = _ln_stats(z)
        xn_ref[...] = xhat * g_ref[...] + b_ref[...]

    row = pl.BlockSpec((tm, Dm), lambda i: (i, 0))
    vec = pl.BlockSpec((1, Dm), lambda i: (0, 0))
    return pl.pallas_call(
        body, name=name, grid=(T // tm,),
        in_specs=[pl.BlockSpec((nk, tm, tk), lambda i: (0, i, 0)),
                  pl.BlockSpec((nk, tk, Dm), lambda i: (0, 0, 0)),
                  row, pl.BlockSpec((1, 3, Dm), lambda i: (i // tpb, 0, 0)), vec, vec],
        out_specs=[row, row],
        out_shape=[jax.ShapeDtypeStruct((T, Dm), F32), jax.ShapeDtypeStruct((T, Dm), F32)],
        compiler_params=_cp(1),
    )(a, w, x, mod3, lng, lnb)


def post_bwd(dxn, x, out, mod3, lng, w, weight, name, tm=512, gu=None, exchange=()):
    T, Dm = x.shape
    nk, tk, _ = w.shape
    Bl = mod3.shape[0]
    tm = _tile(T // Bl, tm)
    tpb = T // Bl // tm
    fused = gu is not None

    def body(dxn_ref, x_ref, out_ref, mod_ref, g_ref, w_ref, *rest):
        if fused:
            gg_ref, uu_ref = rest[:2]
            rest = rest[2:]
        dxr_ref, dout_ref, da_ref, dg_ref, db_ref, dgate_ref = rest
        i = pl.program_id(0)
        out = out_ref[...]
        dxn = dxn_ref[...]
        coef = weight * (1.0 + mod_ref[0][2:3, :])
        xhat, rstd = _ln_stats(ALPHA * x_ref[...] + coef * out)
        dyh = dxn * g_ref[...]
        dz = rstd * (dyh - jnp.mean(dyh, axis=-1, keepdims=True)
                     - xhat * jnp.mean(dyh * xhat, axis=-1, keepdims=True))
        dxr_ref[...] = ALPHA * dz
        dout = (coef * dz).astype(BF16)
        dout_ref[...] = dout

        @pl.when(i == 0)
        def _():
            dg_ref[...] = jnp.zeros_like(dg_ref)
            db_ref[...] = jnp.zeros_like(db_ref)

        @pl.when(i % tpb == 0)
        def _():
            dgate_ref[...] = jnp.zeros_like(dgate_ref)
        dg_ref[...] += _sum0(dxn * xhat)
        db_ref[...] += _sum0(dxn)
        dgate_ref[0] += _sum0((weight * out) * dz)
        for k in range(nk):
            da = _dot_nt(dout, w_ref[k])
            if fused:
                gg = gg_ref[k].astype(F32)
                s = _sigmoid(gg)
                da_ref[k] = (da * uu_ref[k].astype(F32) * (s * (1.0 + gg * (1.0 - s)))).astype(BF16)
                da_ref[nk + k] = (da * (gg * s)).astype(BF16)
            else:
                da_ref[k] = da.astype(BF16)

    row = pl.BlockSpec((tm, Dm), lambda i: (i, 0))
    vec = pl.BlockSpec((1, Dm), lambda i: (0, 0))
    wide = pl.BlockSpec((nk, tm, tk), lambda i: (0, i, 0))
    nda = 2 * nk if fused else nk
    grid = (T // tm,)
    body, c_in, c_out, c_shape, c_scratch = host_comm(body, grid, 8 if fused else 6, 6, exchange=exchange)
    *results, = pl.pallas_call(
        body, name=name, grid=grid,
        in_specs=[row, row, row, pl.BlockSpec((1, 3, Dm), lambda i: (i // tpb, 0, 0)), vec,
                  pl.BlockSpec((nk, tk, Dm), lambda i: (0, 0, 0))] + ([wide, wide] if fused else []) + c_in,
        out_specs=[row, row, pl.BlockSpec((nda, tm, tk), lambda i: (0, i, 0)),
                   vec, vec, pl.BlockSpec((1, 1, Dm), lambda i: (i // tpb, 0, 0))] + c_out,
        out_shape=[jax.ShapeDtypeStruct((T, Dm), F32), jax.ShapeDtypeStruct((T, Dm), BF16),
                   jax.ShapeDtypeStruct((nda, T, tk), BF16), jax.ShapeDtypeStruct((1, Dm), F32),
                   jax.ShapeDtypeStruct((1, Dm), F32), jax.ShapeDtypeStruct((Bl, 1, Dm), F32)] + c_shape,
        scratch_shapes=c_scratch, compiler_params=_cp(1),
    )(dxn, x, out, mod3, lng, w, *(gu if fused else ()), *exchange)
    return tuple(results[:6]) + ((results[6:],) if exchange else ())


def mm_tn(a, b, name, bw=None, a_copies=False, exchange=()):
    a3, b3 = a.ndim == 3, b.ndim == 3
    nk, T, tk = a.shape if a3 else (1,) + a.shape
    if a_copies:
        nk = 1
    nc, wn = (b.shape[0], b.shape[2]) if b3 else (b.shape[1] // bw, bw)
    tt = _tile(T, 2048)
    nt = T // tt

    def body(a_ref, b_ref, o_ref, ob_ref):
        t = pl.program_id(2)

        @pl.when(t == 0)
        def _():
            o_ref[...] = jnp.zeros_like(o_ref)
        o_ref[...] += _dot_tn(a_ref[...], b_ref[...])

        @pl.when(t == nt - 1)
        def _():
            ob_ref[...] = o_ref[...].astype(BF16)

    a_spec = (pl.BlockSpec((None, tt, tk), lambda k, c, t: (k, t, 0)) if a3
              else pl.BlockSpec((tt, tk), lambda k, c, t: (t, 0)))
    b_spec = (pl.BlockSpec((None, tt, wn), lambda k, c, t: (c, t, 0)) if b3
              else pl.BlockSpec((tt, wn), lambda k, c, t: (t, c)))
    o_spec = pl.BlockSpec((None, tk, wn), lambda k, c, t: (k * nc + c, 0, 0))
    grid = (nk, nc, nt)
    body, c_in, c_out, c_shape, c_scratch = host_comm(body, grid, 2, 2, exchange=exchange)
    o32, o16, *received = pl.pallas_call(
        body, name=name, grid=grid, in_specs=[a_spec, b_spec] + c_in, out_specs=[o_spec, o_spec] + c_out,
        out_shape=[jax.ShapeDtypeStruct((nk * nc, tk, wn), F32), jax.ShapeDtypeStruct((nk * nc, tk, wn), BF16)]
        + c_shape,
        scratch_shapes=c_scratch, compiler_params=_cp(3),
    )(a, b, *exchange)
    return (o32, o16, received) if exchange else (o32, o16)


def ffn_in(x, mod3, w, name, gather=()):
    T, Dm = x.shape
    nj, tf = w.shape[0] // 2, w.shape[2]
    Bl = mod3.shape[0]
    tm = _tile(T // Bl, 1024)
    tpb = T // Bl // tm

    def body(x_ref, mod_ref, wg_ref, wu_ref, a_ref, g_ref, u_ref, h_ref):
        m = mod_ref[0]
        h = (x_ref[...] * (1.0 + m[1:2, :]) + m[0:1, :]).astype(BF16)
        h_ref[...] = h
        g = _dot(h, wg_ref[...])
        u = _dot(h, wu_ref[...])
        a_ref[...] = (g * _sigmoid(g) * u).astype(BF16)
        g_ref[...] = g.astype(BF16)
        u_ref[...] = u.astype(BF16)

    col = pl.BlockSpec((None, tm, tf), lambda j, i: (j, i, 0))
    grid = (nj, T // tm)
    body, c_in, c_out, c_shape, c_scratch = host_comm(body, grid, 4, 4, gather=gather)
    a, g, u, h, *gathered = pl.pallas_call(
        body, name=name, grid=grid,
        in_specs=[pl.BlockSpec((tm, Dm), lambda j, i: (i, 0)),
                  pl.BlockSpec((1, 3, Dm), lambda j, i: (i // tpb, 0, 0)),
                  pl.BlockSpec((None, Dm, tf), lambda j, i: (j, 0, 0)),
                  pl.BlockSpec((None, Dm, tf), lambda j, i: (nj + j, 0, 0))] + c_in,
        out_specs=[col, col, col, pl.BlockSpec((None, tm, Dm), lambda j, i: (j, i, 0))] + c_out,
        out_shape=[jax.ShapeDtypeStruct((nj, T, tf), BF16)] * 3 + [jax.ShapeDtypeStruct((nj, T, Dm), BF16)]
        + c_shape,
        scratch_shapes=c_scratch, compiler_params=_cp(2),
    )(x, mod3, w, w, *gather)
    return a, g, u, h, gathered


def loss_head(y, tgt, name):
    T, Dm = y.shape
    tm = _tile(T, 512)
    nt = T // tm

    def body(y_ref, t_ref, dy_ref, l_ref, acc):
        i = pl.program_id(0)

        @pl.when(i == 0)
        def _():
            acc[...] = jnp.zeros_like(acc)
        e = y_ref[...] - t_ref[...]
        dy_ref[...] = e * (1.0 / Dm)
        acc[...] += _sum0(e * e)

        @pl.when(i == nt - 1)
        def _():
            l_ref[...] = jnp.broadcast_to(_sum1(acc[...]) * (0.5 / Dm), l_ref.shape)

    return pl.pallas_call(
        body, name=name, grid=(nt,),
        in_specs=[pl.BlockSpec((tm, Dm), lambda i: (i, 0)), pl.BlockSpec((tm, Dm), lambda i: (i, 0))],
        out_specs=[pl.BlockSpec((tm, Dm), lambda i: (i, 0)), pl.BlockSpec((1, 128), lambda i: (0, 0))],
        out_shape=[jax.ShapeDtypeStruct((T, Dm), F32), jax.ShapeDtypeStruct((1, 128), F32)],
        scratch_shapes=[pltpu.VMEM((1, Dm), F32)], compiler_params=_cp(1),
    )(y, tgt)


def adamw_reduce(w, owns, recvs, onehot, m, v, name):
    P = len(owns)
    R, C = owns[0].shape
    tr = _tile(R, 128)
    nr = R // tr

    def body(oh_ref, w_ref, m_ref, v_ref, *rest):
        own_refs, recv_refs = rest[:P], rest[P:2 * P]
        g_ref, d_ref, nm_ref, nv_ref = rest[2 * P:]
        p = pl.program_id(0)
        for k in range(P):
            @pl.when(p == k)
            def _(k=k):
                acc = None
                for j in range(N_DEV):
                    term = jnp.where(oh_ref[j] > 0.5, own_refs[k][...], recv_refs[k][j].astype(F32))
                    acc = term if acc is None else acc + term
                g_ref[...] = acc
        gg = g_ref[...]
        mm = ADAM_B1 * m_ref[...] + (1.0 - ADAM_B1) * gg
        vv = ADAM_B2 * v_ref[...] + (1.0 - ADAM_B2) * (gg * gg)
        m_hat = mm / (1.0 - ADAM_B1 ** ADAM_STEP)
        v_hat = vv / (1.0 - ADAM_B2 ** ADAM_STEP)
        d_ref[...] = -ADAM_LR * (m_hat / (jnp.sqrt(v_hat) + ADAM_EPS) + ADAM_WD * w_ref[...])
        nm_ref[...] = mm
        nv_ref[...] = vv

    def rows_of(k):
        return lambda p, i: jnp.where(p == k, i, jnp.where(p < k, 0, nr - 1))

    full = pl.BlockSpec((tr, C), lambda p, i: (p * nr + i, 0))
    own_specs = [pl.BlockSpec((tr, C), lambda p, i, f=rows_of(k): (f(p, i), 0)) for k in range(P)]
    recv_specs = [pl.BlockSpec((N_DEV, tr, C), lambda p, i, f=rows_of(k): (0, f(p, i), 0)) for k in range(P)]
    return pl.pallas_call(
        body, name=name, grid=(P, nr),
        in_specs=[pl.BlockSpec(memory_space=pltpu.SMEM), full, full, full] + own_specs + recv_specs,
        out_specs=[full] * 4, out_shape=[jax.ShapeDtypeStruct((P * R, C), F32)] * 4, compiler_params=_cp(2),
    )(onehot, w, m, v, *owns, *recvs)


def adamw(w, g, m, v, name):
    R, C = w.shape
    tr = _tile(R, 512) if R % 8 == 0 else R

    def body(w_ref, g_ref, m_ref, v_ref, d_ref, nm_ref, nv_ref):
        gg = g_ref[...]
        mm = ADAM_B1 * m_ref[...] + (1.0 - ADAM_B1) * gg
        vv = ADAM_B2 * v_ref[...] + (1.0 - ADAM_B2) * (gg * gg)
        m_hat = mm / (1.0 - ADAM_B1 ** ADAM_STEP)
        v_hat = vv / (1.0 - ADAM_B2 ** ADAM_STEP)
        d_ref[...] = -ADAM_LR * (m_hat / (jnp.sqrt(v_hat) + ADAM_EPS) + ADAM_WD * w_ref[...])
        nm_ref[...] = mm
        nv_ref[...] = vv

    spec = pl.BlockSpec((tr, C), lambda i: (i, 0))
    return pl.pallas_call(
        body, name=name, grid=(R // tr,), in_specs=[spec] * 4, out_specs=[spec] * 3,
        out_shape=[jax.ShapeDtypeStruct((R, C), F32)] * 3, compiler_params=_cp(1),
    )(w, g, m, v)


def ada_fwd(c_all, ada_w, ada_b_cols, name):
    Lr, Dm, Nc = ada_w.shape
    Bg = c_all.shape[0]

    def body(c_ref, w_ref, b_ref, o_ref):
        cc = c_ref[...]
        cond = cc * _sigmoid(cc)
        o_ref[0] = _dot(cond.astype(BF16), w_ref[0].astype(BF16)) + b_ref[0]

    return pl.pallas_call(
        body, name=name, grid=(Lr,),
        in_specs=[pl.BlockSpec((Bg, Dm), lambda l: (0, 0)),
                  pl.BlockSpec((1, Dm, Nc), lambda l: (l, 0, 0)),
                  pl.BlockSpec((1, 1, Nc), lambda l: (l, 0, 0))],
        out_specs=pl.BlockSpec((1, Bg, Nc), lambda l: (l, 0, 0)),
        out_shape=jax.ShapeDtypeStruct((Lr, Bg, Nc), F32), compiler_params=_cp(1),
    )(c_all, ada_w, ada_b_cols)


def ada_bwd(c_all_t, dmod_cols, dmod_all, name):
    Dm, Bg = c_all_t.shape
    Lr, _, Nc = dmod_cols.shape
    Nf = dmod_all.shape[2]

    def body(c_ref, dm_ref, da_ref, gw_ref, gb_ref):
        cc = c_ref[...]
        cond = cc * _sigmoid(cc)
        gw_ref[0] = _dot(cond.astype(BF16), dm_ref[0].astype(BF16))
        gb_ref[0] = _sum0(da_ref[0])

    return pl.pallas_call(
        body, name=name, grid=(Lr,),
        in_specs=[pl.BlockSpec((Dm, Bg), lambda l: (0, 0)),
                  pl.BlockSpec((1, Bg, Nc), lambda l: (l, 0, 0)),
                  pl.BlockSpec((1, Bg, Nf), lambda l: (l, 0, 0))],
        out_specs=[pl.BlockSpec((1, Dm, Nc), lambda l: (l, 0, 0)),
                   pl.BlockSpec((1, 1, Nf), lambda l: (l, 0, 0))],
        out_shape=[jax.ShapeDtypeStruct((Lr, Dm, Nc), F32), jax.ShapeDtypeStruct((Lr, 1, Nf), F32)],
        compiler_params=_cp(1),
    )(c_all_t, dmod_cols, dmod_all)


def _conv_taps(x, w, rows):
    shifted = [x]
    c = w[3:4, :] * x
    for k in range(1, 4):
        xs = jnp.where(rows >= k, pltpu.roll(x, k, 0), 0.0)
        shifted.append(xs)
        c = c + w[3 - k:4 - k, :] * xs
    return c, shifted


def conv_silu(proj3, conv_w, name):
    Bl, S, _ = proj3.shape
    ncb = conv_w.shape[1] // 128

    def body(x_ref, w_ref, o_ref):
        rows = lax.broadcasted_iota(jnp.int32, (S, 128), 0)
        c, _ = _conv_taps(_round(x_ref[0]), _round(w_ref[...]), rows)
        o_ref[0] = c * _sigmoid(c)

    return pl.pallas_call(
        body, name=name, grid=(Bl, ncb),
        in_specs=[pl.BlockSpec((1, S, 128), lambda b, j: (b, 0, j)),
                  pl.BlockSpec((4, 128), lambda b, j: (0, j))],
        out_specs=pl.BlockSpec((1, S, 128), lambda b, j: (b, 0, j)),
        out_shape=jax.ShapeDtypeStruct((Bl, S, conv_w.shape[1]), F32), compiler_params=_cp(2),
    )(proj3, conv_w)


def conv_silu_bwd(proj3, conv_w, dq, dk, name):
    Bl, S, _ = proj3.shape
    nq = dq.shape[2] // 128

    def body(x_ref, w_ref, dq_ref, dk_ref, dx_ref, dw_ref):
        j = pl.program_id(1)
        rows = lax.broadcasted_iota(jnp.int32, (S, 128), 0)
        w = _round(w_ref[...])
        c, shifted = _conv_taps(_round(x_ref[0]), w, rows)
        s = _sigmoid(c)
        dact = jnp.where(j < nq, dq_ref[0], dk_ref[0])
        dc = _round(dact * (s * (1.0 + c * (1.0 - s))))
        dx = w[3:4, :] * dc
        dws = [_sum0(dc * shifted[0])]
        for k in range(1, 4):
            up = jnp.where(rows < S - k, pltpu.roll(dc, S - k, 0), 0.0)
            dx = dx + w[3 - k:4 - k, :] * up
            dws.append(_sum0(dc * shifted[k]))
        dx_ref[0] = dx.astype(BF16)
        tap = lax.broadcasted_iota(jnp.int32, (4, 128), 0)
        dw_ref[0] = functools.reduce(lambda a, b: a + b, [jnp.where(tap == 3 - k, dws[k], 0.0) for k in range(4)])

    return pl.pallas_call(
        body, name=name, grid=(Bl, 2 * nq),
        in_specs=[pl.BlockSpec((1, S, 128), lambda b, j: (b, 0, j)),
                  pl.BlockSpec((4, 128), lambda b, j: (0, j)),
                  pl.BlockSpec((1, S, 128), lambda b, j: (b, 0, jnp.minimum(j, nq - 1))),
                  pl.BlockSpec((1, S, 128), lambda b, j: (b, 0, jnp.maximum(j - nq, 0)))],
        out_specs=[pl.BlockSpec((1, S, 128), lambda b, j: (b, 0, j)),
                   pl.BlockSpec((1, 4, 128), lambda b, j: (b, 0, j))],
        out_shape=[jax.ShapeDtypeStruct((Bl, S, 2 * nq * 128), BF16),
                   jax.ShapeDtypeStruct((Bl, 4, 2 * nq * 128), F32)],
        compiler_params=_cp(2),
    )(proj3, conv_w, dq, dk)


def _log_sigmoid(a):
    return jnp.minimum(a, 0.0) - jnp.log(1.0 + jnp.exp(-jnp.abs(a)))


def _interleave(gens):
    live = list(gens)
    while live:
        still = []
        for g in live:
            try:
                next(g)
                still.append(g)
            except StopIteration:
                pass
        live = still


def _finish(gen):
    while True:
        try:
            next(gen)
        except StopIteration as done:
            return done.value


def _chunk_state(kc, vc, gi, bcum, b_last, C, n, m):
    a = b_last - bcum + gi
    m_loc = jnp.max(a, axis=0, keepdims=True)
    wa = jnp.exp(a - m_loc)
    c_loc = _dot_tn((wa * vc).astype(BF16), kc.astype(BF16))
    n_loc = _sum0(_round(wa) * _round(kc))
    m_new = jnp.maximum(b_last + m, m_loc)
    sp = jnp.exp(b_last + m - m_new)
    sl = jnp.exp(m_loc - m_new)
    yield
    return sp * C + sl * c_loc, sp * n + sl * n_loc, m_new, wa, sp, sl


def _chunk_out(qs, kc, vc, gi_row, bcum, bcum_row, low, C, n, m):
    inter_log = bcum + m
    dlog = jnp.where(low, bcum - bcum_row + gi_row, NEG)
    m_i = jnp.maximum(inter_log, jnp.max(dlog, axis=1, keepdims=True))
    dm = jnp.exp(dlog - m_i)
    iw = jnp.exp(inter_log - m_i)
    qs_b, k_b, v_b = qs.astype(BF16), kc.astype(BF16), vc.astype(BF16)
    sqk = _dot_nt(qs_b, k_b)
    qc_ = _dot_nt(qs_b, C.astype(BF16))
    qn = _sum1(_round(qs) * _round(n))
    floor = jnp.exp(-m_i)
    yield
    sc = sqk * dm
    sv = _dot(sc.astype(BF16), v_b)
    den = _sum1(sc) + iw * qn
    dn = jnp.maximum(jnp.abs(den), floor)
    yield
    num = sv + iw * qc_
    return dict(hc=num / dn, den=den, dn=dn, floor=floor, sc=sc, dm=dm, iw=iw, qc=qc_, qn=qn,
                qs_b=qs_b, k_b=k_b, v_b=v_b)


def _cell_consts(L):
    ri = lax.broadcasted_iota(jnp.int32, (L, L), 0)
    ci = lax.broadcasted_iota(jnp.int32, (L, L), 1)
    return ri == ci, ci <= ri, ri <= ci


def _load_chunk(q_ref, k_ref, v_ref, G, off, L, h, lane):
    hh = h % 2
    qmask = (lane >= M_DQK * hh) & (lane < M_DQK * (hh + 1))
    pair = pl.ds(128 * (h // 2), 128)
    qc = jnp.where(qmask, q_ref[0, pl.ds(off, L), pair], 0.0)
    kc = jnp.where(qmask, k_ref[0, pl.ds(off, L), pair], 0.0)
    vc = v_ref[0, pl.ds(off, L), pl.ds(M_DV * h, M_DV)]
    gi = _sum1(jnp.where(lane == h, G, 0.0))
    gf = _sum1(jnp.where(lane == h + HEADS, G, 0.0))
    return qmask, qc, kc, vc, gi, gf


def _gate_rows(gi, gf, eye, low, upp):
    lf = _log_sigmoid(gf)
    lf_row = _sum0(jnp.where(eye, lf, 0.0))
    gi_row = _sum0(jnp.where(eye, gi, 0.0))
    bcum = _sum1(jnp.where(low, lf_row, 0.0))
    bcum_row = _sum0(jnp.where(upp, lf, 0.0))
    b_last = _sum0(lf)
    return gi_row, bcum, bcum_row, b_last


def _cell_specs(SB, cpb, blk):
    def seq(width, col):
        return pl.BlockSpec((1, SB, width), lambda b, s: (b, blk(s), col))

    def state(rows):
        return pl.BlockSpec((1, HEADS, cpb, rows, 128), lambda b, s: (b, 0, blk(s), 0, 0))

    ins = [seq(D // 2, 0), seq(D // 2, 1), seq(D, 1), seq(D, 2), seq(128, 3 * D // 128),
           pl.BlockSpec((1, D), lambda b, s: (0, 0)), pl.BlockSpec((1, 128), lambda b, s: (0, 0))]
    return ins, [state(M_DV), state(1), state(1)], seq


def mlstm_cell_fwd(qk3, proj3, gain, gbias, name, gather=()):
    Bl, S, _ = qk3.shape
    L = M_CHUNK
    SB = min(M_SLAB, S)
    cpb, nc, nsb = SB // L, S // L, S // SB
    scale = M_DQK ** -0.5

    def body(q_ref, k_ref, v_ref, o_ref, g_ref, gain_ref, gb_ref, y_ref, cst_ref, nst_ref, mst_ref, *state):
        C_s, n_s, m_s = state[:HEADS], state[HEADS:2 * HEADS], state[2 * HEADS:]

        @pl.when(pl.program_id(1) == 0)
        def _():
            for ref in state:
                ref[...] = jnp.zeros_like(ref)
        lane = lax.broadcasted_iota(jnp.int32, (L, 128), 1)
        eye, low, upp = _cell_consts(L)

        def step(c, carry):
            off = pl.multiple_of(c * L, L)
            G = g_ref[0, pl.ds(off, L), :] + gb_ref[...]

            def head(h):
                C, n, mb = C_s[h][...], n_s[h][...], m_s[h][...]
                cst_ref[0, h, c] = C
                nst_ref[0, h, c] = n
                mst_ref[0, h, c] = mb
                m = mb[:, 0:1]
                _, qc, kc, vc, gi, gf = _load_chunk(q_ref, k_ref, v_ref, G, off, L, h, lane)
                gi_row, bcum, bcum_row, b_last = _gate_rows(gi, gf, eye, low, upp)
                state = _chunk_state(kc, vc, gi, bcum, b_last, C, n, m)
                next(state)
                r = yield from _chunk_out(qc * scale, kc, vc, gi_row, bcum, bcum_row, low, C, n, m)
                hc = r["hc"]
                hn = hc * lax.rsqrt(jnp.mean(hc * hc, axis=-1, keepdims=True) + RMS_EPS)
                cols = pl.ds(M_DV * h, M_DV)
                oc = o_ref[0, pl.ds(off, L), cols]
                y_ref[0, pl.ds(off, L), cols] = (_sigmoid(oc) * hn * gain_ref[:, cols]).astype(BF16)
                C2, n2, m2, _, _, _ = _finish(state)
                C_s[h][...] = C2
                n_s[h][...] = n2
                m_s[h][...] = jnp.broadcast_to(m2, (1, 128))

            _interleave(head(h) for h in range(HEADS))
            return carry

        lax.fori_loop(0, cpb, step, 0)

    ins, states, seq = _cell_specs(SB, cpb, lambda s: s)
    grid = (Bl, nsb)
    body, c_in, c_out, c_shape, c_scratch = host_comm(body, grid, 7, 4, gather=gather)
    return pl.pallas_call(
        body, name=name, grid=grid, in_specs=ins + c_in, out_specs=[seq(D, 0)] + states + c_out,
        out_shape=[jax.ShapeDtypeStruct((Bl, S, D), BF16),
                   jax.ShapeDtypeStruct((Bl, HEADS, nc, M_DV, 128), F32),
                   jax.ShapeDtypeStruct((Bl, HEADS, nc, 1, 128), F32),
                   jax.ShapeDtypeStruct((Bl, HEADS, nc, 1, 128), F32)] + c_shape,
        scratch_shapes=[pltpu.VMEM((M_DV, 128), F32)] * HEADS + [pltpu.VMEM((1, 128), F32)] * (2 * HEADS) + c_scratch,
        compiler_params=_cp(2),
    )(qk3, qk3, proj3, proj3, proj3, gain, gbias, *gather)


def mlstm_cell_bwd(qk3, proj3, gain, gbias, dy3, states, name, exchange=()):
    Bl, S, _ = qk3.shape
    L = M_CHUNK
    SB = min(M_SLAB, S)
    cpb, nsb = SB // L, S // SB
    scale = M_DQK ** -0.5

    def body(q_ref, k_ref, v_ref, o_ref, g_ref, gain_ref, gb_ref, cst_ref, nst_ref, mst_ref, dy_ref,
             dq_ref, dk_ref, dv_ref, do_ref, dg_ref, dgain_ref, dgb_ref, *state):
        dC_s, dn_s, dgain_s, dgb_s = state[:HEADS], state[HEADS:2 * HEADS], state[2 * HEADS:3 * HEADS], state[-1]
        s = pl.program_id(1)

        @pl.when(s == 0)
        def _():
            for ref in state:
                ref[...] = jnp.zeros_like(ref)
        lane = lax.broadcasted_iota(jnp.int32, (L, 128), 1)
        rowi = lax.broadcasted_iota(jnp.int32, (L, 1), 0)
        eye, low, upp = _cell_consts(L)

        def bstep(t, carry):
            c = cpb - 1 - t
            off = pl.multiple_of(c * L, L)
            G = g_ref[0, pl.ds(off, L), :] + gb_ref[...]
            shared = dict(slab=jnp.zeros((L, 128), F32))

            def head(h):
                cols = pl.ds(M_DV * h, M_DV)
                gain_h = gain_ref[:, cols]
                C, n, m = cst_ref[0, h, c], nst_ref[0, h, c], mst_ref[0, h, c][:, 0:1]
                dC_n, dn_n = dC_s[h][...], dn_s[h][...]
                qmask, qc, kc, vc, gi, gf = _load_chunk(q_ref, k_ref, v_ref, G, off, L, h, lane)
                gi_row, bcum, bcum_row, b_last = _gate_rows(gi, gf, eye, low, upp)
                qs = qc * scale
                _, _, _, wa, sp, sl = _finish(_chunk_state(kc, vc, gi, bcum, b_last, C, n, m))
                dcl_b = (sl * dC_n).astype(BF16)
                t1_mm = _dot(vc.astype(BF16), dcl_b)
                dv_mm = _dot_nt(kc.astype(BF16), dcl_b)
                r = yield from _chunk_out(qs, kc, vc, gi_row, bcum, bcum_row, low, C, n, m)
                hc, den, dn, sc, dm, iw, qn = r["hc"], r["den"], r["dn"], r["sc"], r["dm"], r["iw"], r["qn"]
                qs_b, k_b, v_b = r["qs_b"], r["k_b"], r["v_b"]
                dy = dy_ref[0, pl.ds(off, L), cols].astype(F32)
                oc = o_ref[0, pl.ds(off, L), cols]
                sig_o = _sigmoid(oc)
                rr = lax.rsqrt(jnp.mean(hc * hc, axis=-1, keepdims=True) + RMS_EPS)
                hn = hc * rr
                dgain_s[h][...] += _sum0(dy * sig_o * hn)
                do_ref[0, pl.ds(off, L), cols] = (
                    dy * hn * gain_h * sig_o * (1.0 - sig_o)).astype(BF16)
                dhn = dy * sig_o * gain_h
                dhc = rr * dhn - hc * (rr * rr * rr) * jnp.mean(dhn * hc, axis=-1, keepdims=True)
                dnum = dhc / dn
                gden = -_sum1(dhc * hc) / dn
                dden = jnp.where(jnp.abs(den) > r["floor"], gden * jnp.sign(den), 0.0)
                dnum_b = dnum.astype(BF16)
                dqc_b = (iw * dnum).astype(BF16)
                dsc_mm = _dot_nt(dnum_b, v_b)
                dv = _dot_tn(sc.astype(BF16), dnum_b)
                dqs_mm = _dot(dqc_b, C.astype(BF16))
                dC_out = _dot_tn(dqc_b, qs_b)
                diw = _sum1(dnum * r["qc"]) + dden * qn
                wq = iw * dden
                dn_out = _sum0(wq * qs)
                dn_loc = sl * dn_n
                dsp = _sum1(_sum0(dC_n * C)) + _sum1(dn_n * n)
                yield
                dsc = dsc_mm + dden
                dS_b = (dsc * dm).astype(BF16)
                gm = dsc * sc
                dqs2_mm = _dot(dS_b, k_b)
                dk = _dot_tn(dS_b, qs_b)
                dqs = dqs_mm + wq * n
                dbc = _sum1(gm) + diw * iw
                colg = _sum0(gm)
                dC_p = sp * dC_n + dC_out
                dn_p = sp * dn_n + dn_out
                db_last = dsp * sp
                t1 = t1_mm + dn_loc
                dwa = _sum1(t1 * kc)
                dv = dv + wa * dv_mm
                yield
                dqs = dqs + dqs2_mm
                dk = dk + wa * t1
                da = dwa * wa
                db_last = db_last + _sum0(da)
                dbc = dbc - da + jnp.where(rowi == L - 1, db_last, 0.0)
                dbc_row = _sum0(jnp.where(eye, dbc, 0.0)) - colg
                dgi = da + _sum1(jnp.where(eye, colg, 0.0))
                dlf = _sum1(jnp.where(upp, dbc_row, 0.0))
                dgf = dlf * _sigmoid(-gf)
                dq = jnp.where(qmask, dqs * scale, 0.0)
                dk = jnp.where(qmask, dk, 0.0)
                shared["slab"] = (shared["slab"] + jnp.where(lane == h, dgi, 0.0)
                                  + jnp.where(lane == h + HEADS, dgf, 0.0))
                dv_ref[0, pl.ds(off, L), cols] = dv.astype(BF16)
                dC_s[h][...] = dC_p
                dn_s[h][...] = dn_p
                if h % 2 == 0:
                    shared["dq"], shared["dk"] = dq, dk
                else:
                    pair = pl.ds(128 * (h // 2), 128)
                    dq_ref[0, pl.ds(off, L), pair] = shared["dq"] + dq
                    dk_ref[0, pl.ds(off, L), pair] = shared["dk"] + dk

            _interleave(head(h) for h in range(HEADS))
            dg_ref[0, pl.ds(off, L), :] = shared["slab"]
            dgb_s[...] += _sum0(shared["slab"])
            return carry

        lax.fori_loop(0, cpb, bstep, 0)

        @pl.when(s == nsb - 1)
        def _():
            for h in range(HEADS):
                dgain_ref[0, :, pl.ds(M_DV * h, M_DV)] = dgain_s[h][...]
            dgb_ref[0] = dgb_s[...]

    ins, states_specs, seq = _cell_specs(SB, cpb, lambda s: nsb - 1 - s)
    once = lambda width: pl.BlockSpec((1, 1, width), lambda b, s: (b, 0, 0))
    grid = (Bl, nsb)
    body, c_in, c_out, c_shape, c_scratch = host_comm(body, grid, 11, 7, exchange=exchange)
    return pl.pallas_call(
        body, name=name, grid=grid, in_specs=ins + states_specs + [seq(D, 0)] + c_in,
        out_specs=[seq(D // 2, 0), seq(D // 2, 0), seq(D, 0), seq(D, 0), seq(128, 0), once(D), once(128)] + c_out,
        out_shape=[jax.ShapeDtypeStruct((Bl, S, D // 2), F32), jax.ShapeDtypeStruct((Bl, S, D // 2), F32),
                   jax.ShapeDtypeStruct((Bl, S, D), BF16), jax.ShapeDtypeStruct((Bl, S, D), BF16),
                   jax.ShapeDtypeStruct((Bl, S, 128), F32), jax.ShapeDtypeStruct((Bl, 1, D), F32),
                   jax.ShapeDtypeStruct((Bl, 1, 128), F32)] + c_shape,
        scratch_shapes=[pltpu.VMEM((M_DV, 128), F32)] * HEADS + [pltpu.VMEM((1, 128), F32)] * (2 * HEADS + 1)
        + c_scratch,
        compiler_params=_cp(2),
    )(qk3, qk3, proj3, proj3, proj3, gain, gbias, *states, dy3, *exchange)


def _attn_scores(q, kc, kp, n, row, col, scale):
    s_c = jnp.where(col <= row, _dot_nt(q, kc) * scale, NEG)
    s_p = jnp.where(jnp.logical_and(col >= row, n > 0), _dot_nt(q, kp) * scale, NEG)
    return s_c, s_p


def _to_streams(src, dst, tmp, dil, Sd):
    if dil == 1:
        dst[...] = src[...].astype(dst.dtype)
        return
    if src.dtype != F32:
        tmp[...] = src[...].astype(F32)
        src = tmp
    for r in range(dil):
        dst[pl.ds(r * Sd, Sd), :] = src[pl.ds(r, Sd, stride=dil), :].astype(dst.dtype)


def _from_streams(src, dst, dil, Sd):
    if dil == 1:
        dst[...] = src[...]
        return
    for r in range(dil):
        dst[pl.ds(r, Sd, stride=dil), :] = src[pl.ds(r * Sd, Sd), :]


def attn_fwd(proj, Bl, S, name):
    scale = A_BLK ** -0.5
    pv = proj.reshape(Bl, S, A_PROJ)
    ng = len(DIL_GROUPS)
    rows = 512

    def body(*refs):
        ins, (ob_ref, of_ref, lt_ref) = refs[:3 * ng], refs[3 * ng:3 * ng + 3]
        tmp, qs, ks, vs, os_, ls = refs[3 * ng + 3:3 * ng + 9]
        o_nat, l_nat = refs[3 * ng + 9:4 * ng + 9], refs[4 * ng + 9:]
        row = lax.broadcasted_iota(jnp.int32, (A_BLK, A_BLK), 0)
        col = lax.broadcasted_iota(jnp.int32, (A_BLK, A_BLK), 1)
        for g, (_, dil) in enumerate(DIL_GROUPS):
            Sd = S // dil
            nb = Sd // A_BLK
            for src, dst in zip(ins[3 * g:3 * g + 3], (qs, ks, vs)):
                _to_streams(src.at[0], dst, tmp, dil, Sd)

            def step(i, carry, nb=nb):
                n = i % nb
                off = pl.multiple_of(i * A_BLK, A_BLK)
                offp = pl.multiple_of(jnp.maximum(i - 1, 0) * A_BLK, A_BLK)
                q = qs[pl.ds(off, A_BLK), :]
                s_c, s_p = _attn_scores(q, ks[pl.ds(off, A_BLK), :], ks[pl.ds(offp, A_BLK), :], n, row, col, scale)
                m = jnp.maximum(jnp.max(s_c, axis=1, keepdims=True), jnp.max(s_p, axis=1, keepdims=True))
                p_c = jnp.exp(s_c - m)
                p_p = jnp.exp(s_p - m)
                den = _sum1(p_c) + _sum1(p_p)
                o = (_dot(p_c.astype(BF16), vs[pl.ds(off, A_BLK), :])
                     + _dot(p_p.astype(BF16), vs[pl.ds(offp, A_BLK), :]))
                os_[pl.ds(off, A_BLK), :] = o / den
                ls[pl.ds(off, A_BLK), :] = jnp.broadcast_to(m + jnp.log(den), (A_BLK, 128))
                return carry

            lax.fori_loop(0, dil * nb, step, 0, unroll=A_UNROLL)
            _from_streams(os_, o_nat[g], dil, Sd)
            _from_streams(ls, l_nat[g], dil, Sd)

        def merge(t, carry):
            sl = pl.ds(pl.multiple_of(t * rows, rows), rows)
            lses = [l[sl, :] for l in l_nat]
            m = functools.reduce(jnp.maximum, lses)
            ws = [jnp.exp(l - m) for l in lses]
            den = functools.reduce(lambda a, b: a + b, ws)
            o = functools.reduce(lambda a, b: a + b, [w * r[sl, :] for w, r in zip(ws, o_nat)]) / den
            of_ref[0, sl, :] = o
            ob_ref[0, sl, :] = o.astype(BF16)
            lt_ref[0, sl, :] = m + jnp.log(den)
            return carry

        lax.fori_loop(0, S // rows, merge, 0)

    in_specs = [pl.BlockSpec((1, S, 128), lambda b, h, c=g * 24 + j * HEADS: (b, 0, c + h))
                for g in range(ng) for j in range(3)]
    ospec = pl.BlockSpec((1, S, 128), lambda b, h: (b, 0, h))
    slab = lambda dt: pltpu.VMEM((S, 128), dt)
    outs = pl.pallas_call(
        body, name=name, grid=(Bl, HEADS), in_specs=in_specs, out_specs=[ospec] * 3,
        out_shape=[jax.ShapeDtypeStruct((Bl, S, D), BF16), jax.ShapeDtypeStruct((Bl, S, D), F32),
                   jax.ShapeDtypeStruct((Bl, S, D), F32)],
        scratch_shapes=[slab(F32)] + [slab(BF16)] * 3 + [slab(F32)] * (2 + 2 * ng),
        compiler_params=_cp(2),
    )(*([pv] * (3 * ng)))
    return [t.reshape(Bl * S, D) for t in outs]


def attn_bwd(proj, do, o, lse, Bl, S, g, dil, name):
    Sd = S // dil
    nb = Sd // A_BLK
    scale = A_BLK ** -0.5
    pv = proj.reshape(Bl, S, A_PROJ)
    dov, ov, lv = (t.reshape(Bl, S, D) for t in (do, o, lse))

    def body(q_ref, k_ref, v_ref, do_ref, o_ref, l_ref, dq_ref, dk_ref, dv_ref,
             tmp, qs, ks, vs, dos, dls, lts, dq_s, dk_s, dv_s):
        row = lax.broadcasted_iota(jnp.int32, (A_BLK, A_BLK), 0)
        col = lax.broadcasted_iota(jnp.int32, (A_BLK, A_BLK), 1)
        for src, dst in ((q_ref, qs), (k_ref, ks), (v_ref, vs), (do_ref, dos), (l_ref, lts)):
            _to_streams(src.at[0], dst, tmp, dil, Sd)
        tmp[...] = jnp.broadcast_to(_sum1(do_ref[0].astype(F32) * o_ref[0]), (S, 128))
        _to_streams(tmp, dls, None, dil, Sd)
        dk_s[...] = jnp.zeros_like(dk_s)
        dv_s[...] = jnp.zeros_like(dv_s)

        def step(i, carry):
            n = i % nb
            off = pl.multiple_of(i * A_BLK, A_BLK)
            offp = pl.multiple_of(jnp.maximum(i - 1, 0) * A_BLK, A_BLK)
            q = qs[pl.ds(off, A_BLK), :]
            kc, kp = ks[pl.ds(off, A_BLK), :], ks[pl.ds(offp, A_BLK), :]
            vc, vp = vs[pl.ds(off, A_BLK), :], vs[pl.ds(offp, A_BLK), :]
            do_b = dos[pl.ds(off, A_BLK), :]
            delta = dls[pl.ds(off, A_BLK), :][:, 0:1]
            lt = lts[pl.ds(off, A_BLK), :][:, 0:1]
            s_c, s_p = _attn_scores(q, kc, kp, n, row, col, scale)
            p_c = jnp.exp(s_c - lt)
            p_p = jnp.exp(s_p - lt)
            ds_c = (p_c * (_dot_nt(do_b, vc) - delta) * scale).astype(BF16)
            ds_p = (p_p * (_dot_nt(do_b, vp) - delta) * scale).astype(BF16)
            dq_s[pl.ds(off, A_BLK), :] = _dot(ds_c, kc) + _dot(ds_p, kp)
            dk_s[pl.ds(off, A_BLK), :] += _dot_tn(ds_c, q)
            dk_s[pl.ds(offp, A_BLK), :] += _dot_tn(ds_p, q)
            dv_s[pl.ds(off, A_BLK), :] += _dot_tn(p_c.astype(BF16), do_b)
            dv_s[pl.ds(offp, A_BLK), :] += _dot_tn(p_p.astype(BF16), do_b)
            return carry

        lax.fori_loop(0, dil * nb, step, 0, unroll=A_UNROLL)
        for src, dst in ((dq_s, dq_ref), (dk_s, dk_ref), (dv_s, dv_ref)):
            _from_streams(src, tmp, dil, Sd)
            dst[0] = tmp[...].astype(BF16)

    def spec(j):
        return pl.BlockSpec((1, S, 128), lambda b, h: (b, 0, g * 24 + j * HEADS + h))

    ospec = pl.BlockSpec((1, S, 128), lambda b, h: (b, 0, h))
    slab = lambda dt: pltpu.VMEM((S, 128), dt)
    outs = pl.pallas_call(
        body, name=name, grid=(Bl, HEADS),
        in_specs=[spec(0), spec(1), spec(2), ospec, ospec, ospec], out_specs=[ospec] * 3,
        out_shape=[jax.ShapeDtypeStruct((Bl, S, D), BF16)] * 3,
        scratch_shapes=[slab(F32)] + [slab(BF16)] * 4 + [slab(F32)] * 5,
        compiler_params=_cp(2),
    )(pv, pv, pv, dov, ov, lv)
    return [t.reshape(Bl * S, D) for t in outs]


def _as_slots(pair, shape):
    return tuple(t.reshape(shape) for t in pair)


def ffn_fwd(x, mod3, w_in, w_out, lng, lnb, tag, gather=()):
    a, g, u, h, gathered = ffn_in(x, mod3, w_in, tag + "_in", gather=gather)
    out, xn = proj_post(a, w_out, x, mod3, lng, lnb, 0.5, tag + "_out")
    return xn, (x, out, g, u, h, a), gathered


def ffn_bwd(dxn, saved, mod3, w_in, w_out, lng, tag, exchange=((), (), ()), exchange_own=False):
    x, out, g, u, h, a = saved
    dxres, dout, dgu, dlg, dlb, dgate, *got0 = post_bwd(dxn, x, out, mod3, lng, w_out, 0.5, tag + "_outb",
                                                        tm=256, gu=(g, u), exchange=exchange[0])
    *dw_in, got1 = mm_tn(h, dgu, tag + "_dwin", a_copies=True, exchange=exchange[1]) + (() if exchange[1] else ([],))
    *dw_out, got2 = mm_tn(a, dout, tag + "_dwout", bw=D, exchange=exchange[2]) + (() if exchange[2] else ([],))
    dw_out = _as_slots(dw_out, (N_DEV, D_FF // N_DEV, D))
    dx, dsh, dsc, *own = modmm_bwd(dgu, w_in, x, mod3, dxres, tag + "_inb", tm=256,
                                   exchange=[dw_in[1], dw_out[1]] if exchange_own else ())
    dmod3 = jnp.concatenate([dsh, dsc, dgate], axis=1)
    return (dx, [tuple(dw_in), dw_out], dlg, dlb, dmod3,
            [got0[0] if got0 else [], got1, got2], (own[0] if own else []))


def mlstm_fwd(x, mod3, w_in, w_out, conv_w, gain, gbias, lng, lnb, Bl, S, gather=()):
    proj, h = modmm(x, mod3, w_in, F32, "ml_in", tn=M_PROJ_PAD // 5)
    proj3 = proj.reshape(Bl, S, M_PROJ_PAD)
    qk3 = conv_silu(proj3, conv_w, "ml_conv")
    y3, *rest = mlstm_cell_fwd(qk3, proj3, gain, gbias, "ml_cell", gather=gather)
    states, gathered = rest[:3], rest[3:]
    y = y3.reshape(Bl * S, D)
    out, xn = proj_post(y[None], w_out, x, mod3, lng, lnb, 1.0, "ml_out")
    return xn, (x, out, h, proj3, qk3, y, states), gathered


def mlstm_bwd(dxn, saved, mod3, w_in, w_out, conv_w, gain, gbias, lng, Bl, S, exchange=()):
    x, out, h, proj3, qk3, y, states = saved
    dxres, dout, dy, dlg, dlb, dgate = post_bwd(dxn, x, out, mod3, lng, w_out, 1.0, "ml_outb")
    dq, dk, dv, do, dg, dgain, dgb, *received = mlstm_cell_bwd(qk3, proj3, gain, gbias, dy.reshape(Bl, S, D),
                                                               states, "ml_cellb", exchange=exchange)
    dqk, dconv = conv_silu_bwd(proj3, conv_w, dq, dk, "ml_convb")
    dproj = jnp.concatenate([dqk, dv, do, dg.astype(BF16)], axis=2).reshape(Bl * S, M_PROJ_PAD)
    dx, dsh, dsc = modmm_bwd(dproj, w_in, x, mod3, dxres, "ml_inb", tn=M_PROJ_PAD // 5)
    dwi, _ = mm_tn(h, dproj, "ml_dwin", bw=M_PROJ_PAD // 5)
    dwi = _restack(jnp.moveaxis(dwi, 0, 1).reshape(D, M_PROJ_PAD)[:, :M_PROJ], 1)
    dw_out = _as_slots(mm_tn(y, dout, "ml_dwout", bw=D), (N_DEV, D // N_DEV, D))
    small = (jnp.sum(dconv, axis=0), jnp.sum(dgain, axis=0), jnp.sum(dgb, axis=0)[:, :2 * HEADS])
    dmod3 = jnp.concatenate([dsh, dsc, dgate], axis=1)
    return dx, [(dwi, dwi.astype(BF16)), dw_out], dlg, dlb, dmod3, small, received


def attn_mixer_fwd(x, mod3, w_in, w_out, lng, lnb, Bl, S):
    proj, h = modmm(x, mod3, w_in, BF16, "at_in")
    ob, of, lt = attn_fwd(proj, Bl, S, "at_core")
    out, xn = proj_post(ob[None], w_out, x, mod3, lng, lnb, 1.0, "at_out")
    return xn, (x, out, h, proj, ob, of, lt)


def attn_mixer_bwd(dxn, saved, mod3, w_in, w_out, lng, Bl, S):
    x, out, h, proj, ob, of, lt = saved
    dxres, dout, do, dlg, dlb, dgate = post_bwd(dxn, x, out, mod3, lng, w_out, 1.0, "at_outb")
    do = do[0]
    parts = []
    for g, (_, dil) in enumerate(DIL_GROUPS):
        parts += attn_bwd(proj, do, of, lt, Bl, S, g, dil, "at_coreb%d" % g)
    dproj = jnp.concatenate(parts, axis=1)
    dx, dsh, dsc = modmm_bwd(dproj, w_in, x, mod3, dxres, "at_inb")
    dw_in = mm_tn(h, dproj, "at_dwin", bw=w_in.shape[2])
    dw_out = _as_slots(mm_tn(ob, dout, "at_dwout", bw=D), (N_DEV, D // N_DEV, D))
    return dx, [dw_in, dw_out], dlg, dlb, jnp.concatenate([dsh, dsc, dgate], axis=1)


def _unstack(stacked, axis):
    full = jnp.moveaxis(stacked, 0, axis)
    shp = list(full.shape)
    shp[axis:axis + 2] = [shp[axis] * shp[axis + 1]]
    return full.reshape(shp)


def _restack(full, axis):
    shp = list(full.shape)
    shp[axis:axis + 1] = [N_DEV, shp[axis] // N_DEV]
    return jnp.moveaxis(full.reshape(shp), axis, 0)


def kernel(x, c, ada_w, ada_b, ln_g, ln_b, ffn_w_in, ffn_w_out, mlstm_w_in, mlstm_gate_bias, mlstm_conv_w, mlstm_head_gain, mlstm_w_out, attn_w_in, attn_w_out, loss_target, m_ada_w, m_ada_b, m_ln_g, m_ln_b, m_ffn_w_in, m_ffn_w_out, m_mlstm_w_in, m_mlstm_gate_bias, m_mlstm_conv_w, m_mlstm_head_gain, m_mlstm_w_out, m_attn_w_in, m_attn_w_out, v_ada_w, v_ada_b, v_ln_g, v_ln_b, v_ffn_w_in, v_ffn_w_out, v_mlstm_w_in, v_mlstm_gate_bias, v_mlstm_conv_w, v_mlstm_head_gain, v_mlstm_w_out, v_attn_w_in, v_attn_w_out):
    Bl, S, _ = x.shape
    T = Bl * S
    Bg = Bl * N_DEV
    me = 4 * lax.axis_index("x") + 2 * lax.axis_index("y") + lax.axis_index("c")
    onehot = (jnp.arange(N_DEV) == me).astype(F32)
    weights = dict(ada_w=ada_w, ada_b=ada_b, ln_g=ln_g, ln_b=ln_b, ffn_w_in=ffn_w_in, ffn_w_out=ffn_w_out,
                   mlstm_w_in=mlstm_w_in, mlstm_gate_bias=mlstm_gate_bias, mlstm_conv_w=mlstm_conv_w,
                   mlstm_head_gain=mlstm_head_gain, mlstm_w_out=mlstm_w_out, attn_w_in=attn_w_in,
                   attn_w_out=attn_w_out)
    m_in = dict(ada_w=m_ada_w, ada_b=m_ada_b, ln_g=m_ln_g, ln_b=m_ln_b, ffn_w_in=m_ffn_w_in,
                ffn_w_out=m_ffn_w_out, mlstm_w_in=m_mlstm_w_in, mlstm_gate_bias=m_mlstm_gate_bias,
                mlstm_conv_w=m_mlstm_conv_w, mlstm_head_gain=m_mlstm_head_gain, mlstm_w_out=m_mlstm_w_out,
                attn_w_in=m_attn_w_in, attn_w_out=m_attn_w_out)
    v_in = dict(ada_w=v_ada_w, ada_b=v_ada_b, ln_g=v_ln_g, ln_b=v_ln_b, ffn_w_in=v_ffn_w_in,
                ffn_w_out=v_ffn_w_out, mlstm_w_in=v_mlstm_w_in, mlstm_gate_bias=v_mlstm_gate_bias,
                mlstm_conv_w=v_mlstm_conv_w, mlstm_head_gain=v_mlstm_head_gain, mlstm_w_out=v_mlstm_w_out,
                attn_w_in=v_attn_w_in, attn_w_out=v_attn_w_out)

    mixer = ("mlstm", "attn")
    shards = [[ffn_w_in[layer, 0], ffn_w_in[layer, 1], ffn_w_out[layer, 0], ffn_w_out[layer, 1],
               weights[mixer[layer] + "_w_in"][0], weights[mixer[layer] + "_w_out"][0]] for layer in range(DEPTH)]
    sends = [[s.astype(BF16) for s in layer_shards] for layer_shards in shards]
    small = jnp.concatenate([c.reshape(-1), ln_g.reshape(-1), ln_b.reshape(-1), mlstm_conv_w.reshape(-1)])
    n_small = small.shape[0]
    small = jnp.pad(small, (0, -n_small % (8 * PACK_COLS))).reshape(-1, PACK_COLS)

    def gathered_weights(g):
        return ((g[0], g[1]), (g[2].reshape(4, D_FF // 4, D), g[3].reshape(4, D_FF // 4, D)), g[4],
                g[5].reshape(1, D, D))

    first_in, small_all = all_gather([sends[0][0], small], "ag_params")
    full = [None, None]
    small_flat = small_all.reshape(N_DEV, -1)
    o0 = 0
    c_all = small_flat[:, o0:o0 + c.size].reshape(Bg, D)
    o0 += c.size
    lng_full = _unstack(small_flat[:, o0:o0 + ln_g.size].reshape((N_DEV,) + ln_g.shape), 2)
    o0 += ln_g.size
    lnb_full = _unstack(small_flat[:, o0:o0 + ln_b.size].reshape((N_DEV,) + ln_b.shape), 2)
    o0 += ln_b.size
    conv_full = _unstack(small_flat[:, o0:o0 + mlstm_conv_w.size].reshape((N_DEV,) + mlstm_conv_w.shape), 2)[0]
    gbias =jnp.pad(mlstm_gate_bias, ((0, 0), (0, 128 - 2 * HEADS)))

    ncols = ada_w.shape[2]
    ada_b_cols = lax.dynamic_slice_in_dim(ada_b, me * ncols, ncols, axis=1).reshape(DEPTH, 1, ncols)
    mod_cols = ada_fwd(c_all, ada_w, ada_b_cols, "ada_fwd")
    (mod_g,) = all_gather([mod_cols.reshape(DEPTH * Bg, ncols)], "ag_mod")
    mod_full = _unstack(mod_g.reshape(N_DEV, DEPTH, Bg, ncols), 2)
    mod_mine = lax.dynamic_slice_in_dim(mod_full, me * Bl, Bl, axis=1).reshape(DEPTH, Bl, 3, 3, D)

    xt = x.reshape(T, D)
    saved = []
    for layer in range(DEPTH):
        def lnp(s, layer=layer):
            return lng_full[layer, s].reshape(1, D), lnb_full[layer, s].reshape(1, D)
        md = mod_mine[layer]
        if layer == 0:
            a, g, u, h, late = ffn_in(xt, md[:, 0], first_in, "f0a_in", gather=sends[0][1:])
            full[0] = gathered_weights([first_in] + late)
            out, xn = proj_post(a, full[0][1][0], xt, md[:, 0], *lnp(0), 0.5, "f0a_out")
            xt, sv0 = xn, (xt, out, g, u, h, a)
            mw_in = jnp.pad(_unstack(full[0][2], 1), ((0, 0), (0, M_PROJ_PAD - M_PROJ)))
        else:
            xt, sv0, _ = ffn_fwd(xt, md[:, 0], full[layer][0][0], full[layer][1][0], *lnp(0), "f%da" % layer)
        f_in, f_out, mix_in, mix_out = full[layer]
        if layer % 2 == 0:
            xt, sv1, g1 = mlstm_fwd(xt, md[:, 1], mw_in, mix_out, conv_full, mlstm_head_gain, gbias, *lnp(1), Bl, S,
                                    gather=sends[1])
            full[1] = gathered_weights(g1)
        else:
            xt, sv1 = attn_mixer_fwd(xt, md[:, 1], mix_in, mix_out, *lnp(1), Bl, S)
        xt, sv2, _ = ffn_fwd(xt, md[:, 2], f_in[1], f_out[1], *lnp(2), "f%db" % layer)
        saved.append((sv0, sv1, sv2))

    dxt, lsum = loss_head(xt, loss_target.reshape(T, D), "loss")
    loss = lax.psum(lsum[0, 0], MESH_AXES)

    dmod, dlg_all, dlb_all = [None] * DEPTH, [None] * DEPTH, [None] * DEPTH
    wgrads = [None] * DEPTH
    recvs = [[None] * 6 for _ in range(DEPTH)]
    ml_small = None
    for layer in reversed(range(DEPTH)):
        md = mod_mine[layer]
        f_in, f_out, mix_in, mix_out = full[layer]
        sv0, sv1, sv2 = saved[layer]
        dxt, dw2, dlg2, dlb2, dm2, _, _ = ffn_bwd(dxt, sv2, md[:, 2], f_in[1], f_out[1],
                                                  lng_full[layer, 2].reshape(1, D), "f%db" % layer)
        lg1 = lng_full[layer, 1].reshape(1, D)
        if layer % 2 == 0:
            dxt, dw1, dlg1, dlb1, dm1, ml_small, got = mlstm_bwd(
                dxt, sv1, md[:, 1], mw_in, mix_out, conv_full, mlstm_head_gain, gbias, lg1, Bl, S,
                exchange=[b16 for _, b16 in wgrads[1]])
            recvs[1] = got
            dxt, dw0, dlg0, dlb0, dm0, got, own = ffn_bwd(
                dxt, sv0, md[:, 0], f_in[0], f_out[0], lng_full[layer, 0].reshape(1, D), "f%da" % layer,
                exchange=([dw2[0][1]], [dw2[1][1], dw1[0][1]], [dw1[1][1]]), exchange_own=True)
            (recvs[0][1],), (recvs[0][3], recvs[0][4]), (recvs[0][5],) = got
            recvs[0][0], recvs[0][2] = own
        else:
            dxt, dw1, dlg1, dlb1, dm1 = attn_mixer_bwd(dxt, sv1, md[:, 1], mix_in, mix_out, lg1, Bl, S)
            dxt, dw0, dlg0, dlb0, dm0, _, _ = ffn_bwd(dxt, sv0, md[:, 0], f_in[0], f_out[0],
                                                      lng_full[layer, 0].reshape(1, D), "f%da" % layer)
        wgrads[layer] = [dw0[0], dw2[0], dw0[1], dw2[1], dw1[0], dw1[1]]
        dmod[layer] = jnp.stack([dm0, dm1, dm2], axis=1).reshape(Bl, 9 * D)
        dlg_all[layer] = jnp.concatenate([dlg0, dlg1, dlg2], axis=0)
        dlb_all[layer] = jnp.concatenate([dlb0, dlb1, dlb2], axis=0)
    grad_x = dxt.reshape(Bl, S, D)

    own = [[lax.dynamic_index_in_dim(f32, me, axis=0, keepdims=False) for f32, _ in wgrads[layer]]
           for layer in range(DEPTH)]
    pieces = {"ffn_w_in": [(0, 0), (0, 1), (1, 0), (1, 1)], "ffn_w_out": [(0, 2), (0, 3), (1, 2), (1, 3)],
              "mlstm_w_in": [(0, 4)], "mlstm_w_out": [(0, 5)], "attn_w_in": [(1, 4)], "attn_w_out": [(1, 5)]}
    grads = {}

    dconv, dgain, dgbias = ml_small
    parts = [jnp.stack(dmod).reshape(-1), dgbias.reshape(-1), dgain.reshape(-1),
             jnp.stack(dlg_all).reshape(-1), jnp.stack(dlb_all).reshape(-1), dconv.reshape(-1)]
    sizes = [p.shape[0] for p in parts]
    flat = jnp.concatenate(parts)
    flat = jnp.pad(flat, (0, -flat.shape[0] % (8 * PACK_COLS))).reshape(-1, PACK_COLS)
    (sm_all,) = all_gather([flat], "ag_small")
    sm_sum = sum_leading(sm_all, "small_sum").reshape(-1)
    dmod_all = sm_all.reshape(N_DEV, -1)[:, :sizes[0]].reshape(N_DEV, DEPTH, Bl, 9 * D)
    dmod_all = jnp.moveaxis(dmod_all, 0, 1).reshape(DEPTH, Bg, 9 * D)
    o0 = sizes[0]
    grads["mlstm_gate_bias"] = sm_sum[o0:o0 + sizes[1]].reshape(mlstm_gate_bias.shape)
    o0 += sizes[1]
    grads["mlstm_head_gain"] = sm_sum[o0:o0 + sizes[2]].reshape(mlstm_head_gain.shape)
    o0 += sizes[2]
    nl = ln_g.shape[2]
    g_lng = sm_sum[o0:o0 + sizes[3]].reshape(DEPTH, 3, D)
    o0 += sizes[3]
    g_lnb = sm_sum[o0:o0 + sizes[4]].reshape(DEPTH, 3, D)
    o0 += sizes[4]
    g_conv = sm_sum[o0:o0 + sizes[5]].reshape(1, 4, D)
    grads["ln_g"] = lax.dynamic_slice_in_dim(g_lng, me * nl, nl, axis=2)
    grads["ln_b"] = lax.dynamic_slice_in_dim(g_lnb, me * nl, nl, axis=2)
    grads["mlstm_conv_w"] = lax.dynamic_slice_in_dim(g_conv, me * nl, nl, axis=2)
    dmod_cols = lax.dynamic_slice_in_dim(dmod_all, me * ncols, ncols, axis=2)
    gw, gb = ada_bwd(c_all.T, dmod_cols, dmod_all, "ada_bwd")
    grads["ada_w"] = gw
    grads["ada_b"] = gb.reshape(ada_b.shape)

    names = ["ada_w", "ada_b", "ln_g", "ln_b", "ffn_w_in", "ffn_w_out", "mlstm_w_in", "mlstm_gate_bias",
             "mlstm_conv_w", "mlstm_head_gain", "mlstm_w_out", "attn_w_in", "attn_w_out"]
    deltas, new_m, new_v = [], [], []
    for k in names:
        w = weights[k]
        shp2 = (math.prod(w.shape[:-1]), w.shape[-1])
        if k in pieces:
            g_, d_, m_, v_ = adamw_reduce(w.reshape(shp2), [own[l][i] for l, i in pieces[k]],
                                          [recvs[l][i] for l, i in pieces[k]], onehot,
                                          m_in[k].reshape(shp2), v_in[k].reshape(shp2), "adamw_" + k)
            grads[k] = g_.reshape(w.shape)
        else:
            d_, m_, v_ = adamw(w.reshape(shp2), grads[k].reshape(shp2), m_in[k].reshape(shp2),
                               v_in[k].reshape(shp2), "adamw_" + k)
        deltas.append(d_.reshape(w.shape))
        new_m.append(m_.reshape(w.shape))
        new_v.append(v_.reshape(w.shape))
    return (loss, grad_x, *[grads[k] for k in names], *deltas, *new_m, *new_v)
```
